```python
import jax, jax.numpy as jnp
from jax import lax
import numpy as np

D_MODEL = 1024
BATCH = 16
SEQ = 4096
DEPTH = 1

PLE_DIM = 256
CHUNK = 128
RET_HEADS = 4
RET_V_DIM = D_MODEL // RET_HEADS
RET_QK_DIM = RET_V_DIM // 2
RET_WIDTH = RET_HEADS * RET_V_DIM
SGU_GROUPS = 4
SGU_WIDTH = D_MODEL
SGU_GROUP_DIM = SGU_WIDTH // SGU_GROUPS
ROPE_BASE = 10000.0
NORM_EPS = 1e-6
GN_EPS = 1e-5
IN_SPLITS = (RET_HEADS * RET_QK_DIM, RET_HEADS * RET_QK_DIM, RET_WIDTH, RET_WIDTH,
             SGU_WIDTH, SGU_WIDTH, SGU_WIDTH, D_MODEL, D_MODEL)
IN_WIDTH = sum(IN_SPLITS)

kernel_name = 'hybrid_retention_sgu_block'


def rms_norm(x, g):
    xf = x.astype(jnp.float32)
    y = xf * lax.rsqrt(jnp.mean(xf * xf, axis=-1, keepdims=True) + NORM_EPS)
    return (y * g.astype(jnp.float32)).astype(x.dtype)


def unit_norm(x, eps):
    xf = x.astype(jnp.float32)
    mu = jnp.mean(xf, axis=-1, keepdims=True)
    var = jnp.mean(jnp.square(xf - mu), axis=-1, keepdims=True)
    return ((xf - mu) * lax.rsqrt(var + eps)).astype(x.dtype)


def rotary(x):
    s, d = x.shape[1], x.shape[-1]
    half = d // 2
    inv = ROPE_BASE ** (-jnp.arange(half, dtype=jnp.float32) / half)
    ang = jnp.arange(s, dtype=jnp.float32)[:, None] * inv[None, :]
    cos = jnp.cos(ang)[None, :, None, :].astype(x.dtype)
    sin = jnp.sin(ang)[None, :, None, :].astype(x.dtype)
    x1, x2 = x[..., :half], x[..., half:]
    return jnp.concatenate([x1 * cos - x2 * sin, x2 * cos + x1 * sin], axis=-1)


def retention(q, k, v):
    b, s, h, dk = q.shape
    dv = v.shape[-1]
    n = s // CHUNK
    log_g = jnp.log(1.0 - 2.0 ** (-5.0 - jnp.arange(h, dtype=jnp.float32)))
    idx = jnp.arange(CHUNK, dtype=jnp.float32)
    diff = idx[:, None] - idx[None, :]
    decay_in = jnp.where(diff[None] >= 0,
                         jnp.exp(jnp.maximum(diff, 0.0)[None] * log_g[:, None, None]),
                         0.0).astype(q.dtype)
    zeta = jnp.exp((CHUNK - 1.0 - idx)[:, None] * log_g[None, :]).astype(q.dtype)
    xi = jnp.exp((idx + 1.0)[:, None] * log_g[None, :]).astype(q.dtype)
    chunk_decay = jnp.exp(CHUNK * log_g).astype(v.dtype)

    qc = (q * (dk ** -0.5)).reshape(b, n, CHUNK, h, dk)
    kc = k.reshape(b, n, CHUNK, h, dk)
    vc = v.reshape(b, n, CHUNK, h, dv)

    scores = jnp.einsum('bnihd,bnjhd->bnhij', qc, kc) * decay_in
    inner = jnp.einsum('bnhij,bnjhe->bnihe', scores, vc)

    kv = jnp.einsum('bnjhd,bnjhe->nbhde', kc * zeta[:, :, None], vc)

    def step(state, kv_n):
        return kv_n + chunk_decay[None, :, None, None] * state, state

    _, prev = lax.scan(step, jnp.zeros_like(kv[0]), kv)
    cross = jnp.einsum('bnihd,nbhde->bnihe', qc * xi[:, :, None], prev)
    return (inner + cross).reshape(b, s, h, dv)


def spatial_gating(u, v, ws, bs):
    b, s, _ = u.shape
    n = s // CHUNK
    v = unit_norm(v, GN_EPS).reshape(b, n, CHUNK, SGU_GROUPS, SGU_GROUP_DIM)
    ws_causal = ws * jnp.tril(jnp.ones((CHUNK, CHUNK), ws.dtype))[None]
    mixed = jnp.einsum('gij,bnjgd->bnigd', ws_causal, v) + bs.T[None, None, :, :, None]
    return u * mixed.reshape(b, s, SGU_WIDTH)


def _fwd_setup_inputs(seed: int = 0) -> dict:
    key = jax.random.key(seed)
    ks = jax.random.split(key, 13)
    f32 = jnp.float32
    nrm = lambda k, shape, scale: jax.random.normal(k, shape, f32) * scale
    return {
        'x': jax.random.normal(ks[0], (BATCH, SEQ, D_MODEL), f32),
        'p': jax.random.normal(ks[1], (DEPTH, BATCH, SEQ, PLE_DIM), f32),
        'w_in': nrm(ks[2], (DEPTH, D_MODEL, IN_WIDTH), D_MODEL ** -0.5),
        'w_ret_out': nrm(ks[3], (DEPTH, RET_WIDTH, D_MODEL), RET_WIDTH ** -0.5),
        'w_sgu_out': nrm(ks[4], (DEPTH, SGU_WIDTH, D_MODEL), SGU_WIDTH ** -0.5),
        'w_out': nrm(ks[5], (DEPTH, D_MODEL, D_MODEL), D_MODEL ** -0.5),
        'sgu_ws': nrm(ks[6], (DEPTH, SGU_GROUPS, CHUNK, CHUNK), CHUNK ** -0.5),
        'sgu_bs': 1.0 + nrm(ks[7], (DEPTH, SGU_GROUPS, CHUNK), 0.01),
        'w_ple_gate': nrm(ks[8], (DEPTH, D_MODEL, D_MODEL), D_MODEL ** -0.5),
        'w_ple_proj': nrm(ks[9], (DEPTH, PLE_DIM, D_MODEL), PLE_DIM ** -0.5),
        'g_mixer': 1.0 + nrm(ks[10], (DEPTH, D_MODEL), 0.05),
        'g_ple': 1.0 + nrm(ks[11], (DEPTH, D_MODEL), 0.05),
        'g_final': 1.0 + nrm(ks[12], (D_MODEL,), 0.05),
    }


def _fwd_reference(x, p, w_in, w_ret_out, w_sgu_out, w_out, sgu_ws, sgu_bs,
              w_ple_gate, w_ple_proj, g_mixer, g_ple, g_final):
    b, s, _ = x.shape
    split_at = np.cumsum(IN_SPLITS)[:-1].tolist()
    for i in range(DEPTH):
        h = rms_norm(x, g_mixer[i])
        proj = jnp.einsum('bsd,de->bse', h, w_in[i])
        q, k, v, ret_gate, su, sv, sgu_gate, merge_ret, merge_sgu = jnp.split(proj, split_at, axis=-1)

        q = rotary(q.reshape(b, s, RET_HEADS, RET_QK_DIM))
        k = rotary(k.reshape(b, s, RET_HEADS, RET_QK_DIM))
        ret = retention(q, k, v.reshape(b, s, RET_HEADS, RET_V_DIM))
        ret = unit_norm(ret, GN_EPS).reshape(b, s, RET_WIDTH) * jax.nn.silu(ret_gate)

        sgu = spatial_gating(jax.nn.gelu(su, approximate=False), jax.nn.gelu(sv, approximate=False),
                             sgu_ws[i], sgu_bs[i]) * jax.nn.silu(sgu_gate)

        merged = (jax.nn.sigmoid(merge_ret) * jnp.einsum('bse,ed->bsd', ret, w_ret_out[i])
                  + jax.nn.sigmoid(merge_sgu) * jnp.einsum('bse,ed->bsd', sgu, w_sgu_out[i]))
        x = x + jnp.einsum('bsd,de->bse', merged, w_out[i])

        ple_gate = jax.nn.sigmoid(jnp.einsum('bsd,de->bse', rms_norm(x, g_ple[i]), w_ple_gate[i]))
        x = x + ple_gate * jnp.einsum('bsp,pd->bsd', p[i], w_ple_proj[i])
    return rms_norm(x, g_final)


import jax as _jax
import jax.numpy as _jnp

TWIN_FORMAT = 'train_step'
FWD_PARAMS = ['x', 'p', 'w_in', 'w_ret_out', 'w_sgu_out', 'w_out', 'sgu_ws', 'sgu_bs', 'w_ple_gate', 'w_ple_proj', 'g_mixer', 'g_ple', 'g_final']
TWIN_WEIGHTS = ['w_in', 'w_ret_out', 'w_sgu_out', 'w_out', 'sgu_ws', 'sgu_bs', 'w_ple_gate', 'w_ple_proj', 'g_mixer', 'g_ple', 'g_final']
TWIN_DIFF_INPUT = 'x'
TWIN_INPUTS = ['x', 'p', 'w_in', 'w_ret_out', 'w_sgu_out', 'w_out', 'sgu_ws', 'sgu_bs', 'w_ple_gate', 'w_ple_proj', 'g_mixer', 'g_ple', 'g_final', 'loss_target', 'm_w_in', 'm_w_ret_out', 'm_w_sgu_out', 'm_w_out', 'm_sgu_ws', 'm_sgu_bs', 'm_w_ple_gate', 'm_w_ple_proj', 'm_g_mixer', 'm_g_ple', 'm_g_final', 'v_w_in', 'v_w_ret_out', 'v_w_sgu_out', 'v_w_out', 'v_sgu_ws', 'v_sgu_bs', 'v_w_ple_gate', 'v_w_ple_proj', 'v_g_mixer', 'v_g_ple', 'v_g_final']
TWIN_OUTPUTS = ['loss', 'grad_x', 'grad_w_in', 'grad_w_ret_out', 'grad_w_sgu_out', 'grad_w_out', 'grad_sgu_ws', 'grad_sgu_bs', 'grad_w_ple_gate', 'grad_w_ple_proj', 'grad_g_mixer', 'grad_g_ple', 'grad_g_final', 'delta_w_in', 'delta_w_ret_out', 'delta_w_sgu_out', 'delta_w_out', 'delta_sgu_ws', 'delta_sgu_bs', 'delta_w_ple_gate', 'delta_w_ple_proj', 'delta_g_mixer', 'delta_g_ple', 'delta_g_final', 'new_m_w_in', 'new_m_w_ret_out', 'new_m_w_sgu_out', 'new_m_w_out', 'new_m_sgu_ws', 'new_m_sgu_bs', 'new_m_w_ple_gate', 'new_m_w_ple_proj', 'new_m_g_mixer', 'new_m_g_ple', 'new_m_g_final', 'new_v_w_in', 'new_v_w_ret_out', 'new_v_w_sgu_out', 'new_v_w_out', 'new_v_sgu_ws', 'new_v_sgu_bs', 'new_v_w_ple_gate', 'new_v_w_ple_proj', 'new_v_g_mixer', 'new_v_g_ple', 'new_v_g_final']
TWIN_LEAF_KINDS = {'loss': 'loss', 'grad_x': 'grad_x', 'grad_w_in': 'grad_w', 'grad_w_ret_out': 'grad_w', 'grad_w_sgu_out': 'grad_w', 'grad_w_out': 'grad_w', 'grad_sgu_ws': 'grad_w', 'grad_sgu_bs': 'grad_w', 'grad_w_ple_gate': 'grad_w', 'grad_w_ple_proj': 'grad_w', 'grad_g_mixer': 'grad_w', 'grad_g_ple': 'grad_w', 'grad_g_final': 'grad_w', 'delta_w_in': 'delta_w', 'delta_w_ret_out': 'delta_w', 'delta_w_sgu_out': 'delta_w', 'delta_w_out': 'delta_w', 'delta_sgu_ws': 'delta_w', 'delta_sgu_bs': 'delta_w', 'delta_w_ple_gate': 'delta_w', 'delta_w_ple_proj': 'delta_w', 'delta_g_mixer': 'delta_w', 'delta_g_ple': 'delta_w', 'delta_g_final': 'delta_w', 'new_m_w_in': 'new_m', 'new_m_w_ret_out': 'new_m', 'new_m_w_sgu_out': 'new_m', 'new_m_w_out': 'new_m', 'new_m_sgu_ws': 'new_m', 'new_m_sgu_bs': 'new_m', 'new_m_w_ple_gate': 'new_m', 'new_m_w_ple_proj': 'new_m', 'new_m_g_mixer': 'new_m', 'new_m_g_ple': 'new_m', 'new_m_g_final': 'new_m', 'new_v_w_in': 'new_v', 'new_v_w_ret_out': 'new_v', 'new_v_w_sgu_out': 'new_v', 'new_v_w_out': 'new_v', 'new_v_sgu_ws': 'new_v', 'new_v_sgu_bs': 'new_v', 'new_v_w_ple_gate': 'new_v', 'new_v_w_ple_proj': 'new_v', 'new_v_g_mixer': 'new_v', 'new_v_g_ple': 'new_v', 'new_v_g_final': 'new_v'}


def _forward(args):
    return _fwd_reference(*[args[k] for k in FWD_PARAMS])


def _output_shape():
    out = _jax.eval_shape(lambda: _forward(_fwd_setup_inputs(0)))
    return out.shape, out.dtype

N_MICROBATCH = 1
ADAM_LR = 0.001
ADAM_B1 = 0.9
ADAM_B2 = 0.999
ADAM_EPS = 1e-08
ADAM_WD = 0.01
ADAM_STEP = 10
PER_EXAMPLE_BATCH_AXIS = {'x': 0, 'p': 1, 'loss_target': 0}
SHARED_INPUTS = []
_WEIGHT_DTYPES = {'w_in': _jnp.float32, 'w_ret_out': _jnp.float32, 'w_sgu_out': _jnp.float32, 'w_out': _jnp.float32, 'sgu_ws': _jnp.float32, 'sgu_bs': _jnp.float32, 'w_ple_gate': _jnp.float32, 'w_ple_proj': _jnp.float32, 'g_mixer': _jnp.float32, 'g_ple': _jnp.float32, 'g_final': _jnp.float32}
MOMENT_SCALE = {'w_in': 5.981196e-02, 'w_ret_out': 6.900982e-02, 'w_sgu_out': 5.516469e-02, 'w_out': 8.852914e-02, 'sgu_ws': 4.530552e-02, 'sgu_bs': 6.620670e-02, 'w_ple_gate': 4.434554e-02, 'w_ple_proj': 1.169986e-01, 'g_mixer': 1.702953e-01, 'g_ple': 4.347101e-02, 'g_final': 6.411882e+01}


def _to_microbatches(a, axis):
    t = _jnp.moveaxis(a, axis, 0)
    t = t.reshape((N_MICROBATCH, t.shape[0] // N_MICROBATCH) + t.shape[1:])
    return _jnp.moveaxis(t, 1, axis + 1)


def setup_inputs(seed: int = 0) -> dict:
    inp = _fwd_setup_inputs(seed)
    key = _jax.random.fold_in(_jax.random.key(seed), 7919)
    shape, _ = _output_shape()
    out = dict(inp)
    out["loss_target"] = _jax.random.normal(_jax.random.fold_in(key, 0), shape, _jnp.float32)
    for i, name in enumerate(TWIN_WEIGHTS):
        w = inp[name].astype(_jnp.float32)
        if MOMENT_SCALE is None:
            s = _jnp.sqrt(_jnp.mean(_jnp.square(w)) + 1e-30)
        else:
            s = MOMENT_SCALE[name]
        km, kv = _jax.random.split(_jax.random.fold_in(key, i + 1))
        out[name] = w
        out["m_" + name] = s * _jax.random.normal(km, w.shape, _jnp.float32)
        out["v_" + name] = (s * s) * _jax.random.uniform(kv, w.shape, _jnp.float32, 0.5, 1.5)
    if N_MICROBATCH > 1:
        for name, axis in PER_EXAMPLE_BATCH_AXIS.items():
            out[name] = _to_microbatches(out[name], axis)
    return {'x': out['x'], 'p': out['p'], 'w_in': out['w_in'], 'w_ret_out': out['w_ret_out'], 'w_sgu_out': out['w_sgu_out'], 'w_out': out['w_out'], 'sgu_ws': out['sgu_ws'], 'sgu_bs': out['sgu_bs'], 'w_ple_gate': out['w_ple_gate'], 'w_ple_proj': out['w_ple_proj'], 'g_mixer': out['g_mixer'], 'g_ple': out['g_ple'], 'g_final': out['g_final'], 'loss_target': out['loss_target'], 'm_w_in': out['m_w_in'], 'm_w_ret_out': out['m_w_ret_out'], 'm_w_sgu_out': out['m_w_sgu_out'], 'm_w_out': out['m_w_out'], 'm_sgu_ws': out['m_sgu_ws'], 'm_sgu_bs': out['m_sgu_bs'], 'm_w_ple_gate': out['m_w_ple_gate'], 'm_w_ple_proj': out['m_w_ple_proj'], 'm_g_mixer': out['m_g_mixer'], 'm_g_ple': out['m_g_ple'], 'm_g_final': out['m_g_final'], 'v_w_in': out['v_w_in'], 'v_w_ret_out': out['v_w_ret_out'], 'v_w_sgu_out': out['v_w_sgu_out'], 'v_w_out': out['v_w_out'], 'v_sgu_ws': out['v_sgu_ws'], 'v_sgu_bs': out['v_sgu_bs'], 'v_w_ple_gate': out['v_w_ple_gate'], 'v_w_ple_proj': out['v_w_ple_proj'], 'v_g_mixer': out['v_g_mixer'], 'v_g_ple': out['v_g_ple'], 'v_g_final': out['v_g_final']}


def _loss(weights, diff, rest, loss_target):
    with _jax.named_scope("forward"):
        args = {**rest, TWIN_DIFF_INPUT: diff, **{k: w.astype(_WEIGHT_DTYPES[k]) for k, w in weights.items()}}
        y = _forward(args)
    with _jax.named_scope("loss_head"):
        err = _jnp.square(y.astype(_jnp.float32) - loss_target)
        return 0.5 * _jnp.sum(_jnp.mean(err, axis=-1)) if err.ndim else 0.5 * err


def _adamw(w, g, m, v):
    m = ADAM_B1 * m + (1.0 - ADAM_B1) * g
    v = ADAM_B2 * v + (1.0 - ADAM_B2) * _jnp.square(g)
    m_hat = m / (1.0 - ADAM_B1 ** ADAM_STEP)
    v_hat = v / (1.0 - ADAM_B2 ** ADAM_STEP)
    delta = -ADAM_LR * (m_hat / (_jnp.sqrt(v_hat) + ADAM_EPS) + ADAM_WD * w)
    return delta, m, v


def reference(x, p, w_in, w_ret_out, w_sgu_out, w_out, sgu_ws, sgu_bs, w_ple_gate, w_ple_proj, g_mixer, g_ple, g_final, loss_target, m_w_in, m_w_ret_out, m_w_sgu_out, m_w_out, m_sgu_ws, m_sgu_bs, m_w_ple_gate, m_w_ple_proj, m_g_mixer, m_g_ple, m_g_final, v_w_in, v_w_ret_out, v_w_sgu_out, v_w_out, v_sgu_ws, v_sgu_bs, v_w_ple_gate, v_w_ple_proj, v_g_mixer, v_g_ple, v_g_final):
    given = dict(x=x, p=p, w_in=w_in, w_ret_out=w_ret_out, w_sgu_out=w_sgu_out, w_out=w_out, sgu_ws=sgu_ws, sgu_bs=sgu_bs, w_ple_gate=w_ple_gate, w_ple_proj=w_ple_proj, g_mixer=g_mixer, g_ple=g_ple, g_final=g_final, loss_target=loss_target, m_w_in=m_w_in, m_w_ret_out=m_w_ret_out, m_w_sgu_out=m_w_sgu_out, m_w_out=m_w_out, m_sgu_ws=m_sgu_ws, m_sgu_bs=m_sgu_bs, m_w_ple_gate=m_w_ple_gate, m_w_ple_proj=m_w_ple_proj, m_g_mixer=m_g_mixer, m_g_ple=m_g_ple, m_g_final=m_g_final, v_w_in=v_w_in, v_w_ret_out=v_w_ret_out, v_w_sgu_out=v_w_sgu_out, v_w_out=v_w_out, v_sgu_ws=v_sgu_ws, v_sgu_bs=v_sgu_bs, v_w_ple_gate=v_w_ple_gate, v_w_ple_proj=v_w_ple_proj, v_g_mixer=v_g_mixer, v_g_ple=v_g_ple, v_g_final=v_g_final)
    weights = {n: given[n] for n in TWIN_WEIGHTS}
    shared = {n: given[n] for n in SHARED_INPUTS}
    per_example = {n: given[n] for n in ['x', 'p']}
    grad_fn = _jax.value_and_grad(_loss, argnums=(0, 1))

    def one_microbatch(ex, loss_target):
        ex = dict(ex)
        diff = ex.pop(TWIN_DIFF_INPUT)
        return grad_fn(weights, diff, {**shared, **ex}, loss_target)

    if N_MICROBATCH == 1:
        loss, (grad_w, grad_x) = one_microbatch(per_example, given["loss_target"])
    else:
        def body(carry, xs):
            loss_sum, grad_sum = carry
            l_k, (gw_k, gx_k) = one_microbatch(xs[0], xs[1])
            with _jax.named_scope("update"):
                return (loss_sum + l_k, _jax.tree.map(_jnp.add, grad_sum, gw_k)), gx_k

        init = (_jnp.zeros((), _jnp.float32), _jax.tree.map(_jnp.zeros_like, weights))
        (loss, grad_w), grad_x = _jax.lax.scan(body, init, (per_example, given["loss_target"]))
    with _jax.named_scope("update"):
        delta_w, new_m, new_v = {}, {}, {}
        for n in TWIN_WEIGHTS:
            delta_w[n], new_m[n], new_v[n] = _adamw(weights[n], grad_w[n], given["m_" + n], given["v_" + n])
    return (loss, grad_x, *[grad_w[n] for n in TWIN_WEIGHTS], *[delta_w[n] for n in TWIN_WEIGHTS],
            *[new_m[n] for n in TWIN_WEIGHTS], *[new_v[n] for n in TWIN_WEIGHTS])
```

```python
import functools
import math

import numpy as np
import jax
import jax.numpy as jnp
from jax import lax
from jax.experimental import pallas as pl
from jax.experimental.pallas import tpu as pltpu

F32 = jnp.float32
MXU_DTYPE = jnp.bfloat16
COMM_DTYPE = jnp.bfloat16

D_MODEL = 1024
RET_HEADS = 4
QK_DIM = 128
V_DIM = 256
CHUNK = 128
SGU_GROUPS = 4
GROUP_DIM = 256
PLE_DIM = 256
N_CHIPS = 4
SHARD_W = 2048
ROPE_BASE = 10000.0
NORM_EPS = 1e-6
GN_EPS = 1e-5
QK_SCALE = QK_DIM ** -0.5
SQRT_HALF = math.sqrt(0.5)
INV_SQRT_2PI = 1.0 / math.sqrt(2.0 * math.pi)

ADAM_LR = 0.001
ADAM_B1 = 0.9
ADAM_B2 = 0.999
ADAM_EPS = 1e-08
ADAM_WD = 0.01
ADAM_STEP = 10

TILE_M = 256
VMEM_LIMIT = 56 * 1024 * 1024
MESH = pl.DeviceIdType.MESH
ANY = pl.BlockSpec(memory_space=pl.ANY)

CHUNK_DECAY = tuple(
    float(np.exp(np.float32(CHUNK) * np.log(np.float32(1.0 - 2.0 ** (-5.0 - h))))) for h in range(RET_HEADS))


def _mm(a, b):
    return jnp.dot(a.astype(MXU_DTYPE), b.astype(MXU_DTYPE), preferred_element_type=F32)


def _mm_nt(a, b):
    return lax.dot_general(a.astype(MXU_DTYPE), b.astype(MXU_DTYPE), (((1,), (1,)), ((), ())),
                           preferred_element_type=F32)


def _mm_tn(a, b):
    return lax.dot_general(a.astype(MXU_DTYPE), b.astype(MXU_DTYPE), (((0,), (0,)), ((), ())),
                           preferred_element_type=F32)


def _mean(x):
    return jnp.mean(x, axis=-1, keepdims=True)


def _sigmoid(x):
    return jax.nn.sigmoid(x)


def _silu_and_grad(x):
    s = _sigmoid(x)
    return x * s, s * (1.0 + x * (1.0 - s))


def _gelu(x):
    return 0.5 * x * (1.0 + lax.erf(x * SQRT_HALF))


def _gelu_grad(x):
    return 0.5 * (1.0 + lax.erf(x * SQRT_HALF)) + x * jnp.exp(-0.5 * x * x) * INV_SQRT_2PI


def _unit_norm(x, eps):
    xc = x - _mean(x)
    rstd = lax.rsqrt(_mean(xc * xc) + eps)
    return xc * rstd, rstd


def _unit_norm_bwd(dn, n, rstd):
    return rstd * (dn - _mean(dn) - n * _mean(dn * n))


def _rms(x):
    r = lax.rsqrt(_mean(x * x) + NORM_EPS)
    return x * r, r


def _rms_bwd(dxn, xn, r):
    return r * (dxn - xn * _mean(dxn * xn))


def _rot(x):
    return pltpu.roll(x, QK_DIM // 2, 1)


def _params(semantics, **kw):
    return pltpu.CompilerParams(dimension_semantics=semantics, vmem_limit_bytes=VMEM_LIMIT, **kw)


def _row_tile(width, tm=TILE_M):
    return pl.BlockSpec((tm, width), lambda i: (i, 0))


def _resident(shape):
    nd = len(shape)
    return pl.BlockSpec(shape, lambda *_: (0,) * nd, pipeline_mode=pl.Buffered(1))


def _causal_ws(ws_ref):
    row = lax.broadcasted_iota(jnp.int32, (CHUNK, CHUNK), 0)
    col = lax.broadcasted_iota(jnp.int32, (CHUNK, CHUNK), 1)
    return [jnp.where(row >= col, ws_ref[g], 0.0).astype(MXU_DTYPE) for g in range(SGU_GROUPS)]


def _heads(x, width):
    return [x[:, h * width:(h + 1) * width] for h in range(x.shape[1] // width)]


def _branch_forward(pj1, pj2, ret_raw, wsc, bst):
    rg, su = pj1[:, :D_MODEL], pj1[:, D_MODEL:]
    sv, sg = pj2[:, :D_MODEL], pj2[:, D_MODEL:]
    rn_parts, rstd_parts = zip(*[_unit_norm(r, GN_EPS) for r in _heads(ret_raw, V_DIM)])
    rn = jnp.concatenate(rn_parts, axis=1)
    sil_rg, dsil_rg = _silu_and_grad(rg)
    ret = rn * sil_rg
    u = _gelu(su)
    vn, rstd_v = _unit_norm(_gelu(sv), GN_EPS)
    rows = []
    for cc in range(pj1.shape[0] // CHUNK):
        cols = []
        for g in range(SGU_GROUPS):
            blk = vn[cc * CHUNK:(cc + 1) * CHUNK, g * GROUP_DIM:(g + 1) * GROUP_DIM]
            cols.append(_mm(wsc[g], blk) + bst[:, g:g + 1])
        rows.append(jnp.concatenate(cols, axis=1))
    mixed = jnp.concatenate(rows, axis=0)
    sil_sg, dsil_sg = _silu_and_grad(sg)
    sgu = u * mixed * sil_sg
    return dict(rg=rg, su=su, sv=sv, sg=sg, rn=rn, rstd_r=rstd_parts, sil_rg=sil_rg, dsil_rg=dsil_rg, ret=ret,
                u=u, vn=vn, rstd_v=rstd_v, mixed=mixed, sil_sg=sil_sg, dsil_sg=dsil_sg, sgu=sgu)


def _proj_call(x2d, g_mixer, wg_in):
    t = x2d.shape[0]

    def body(x_ref, g_ref, w_ref, h_ref, p0, p1, p2, p3):
        xn, _ = _rms(x_ref[...])
        h = (xn * g_ref[...]).astype(MXU_DTYPE)
        h_ref[...] = h
        for j, p_ref in enumerate((p0, p1, p2, p3)):
            p_ref[...] = jnp.dot(h, w_ref[j], preferred_element_type=F32)

    return pl.pallas_call(
        body, name="proj_fwd", grid=(t // TILE_M,),
        in_specs=[_row_tile(D_MODEL), _resident((1, D_MODEL)), _resident(wg_in.shape)],
        out_specs=[_row_tile(D_MODEL)] + [_row_tile(SHARD_W)] * N_CHIPS,
        out_shape=[jax.ShapeDtypeStruct((t, D_MODEL), MXU_DTYPE)]
        + [jax.ShapeDtypeStruct((t, SHARD_W), F32)] * N_CHIPS,
        compiler_params=_params(("arbitrary",)),
    )(x2d, g_mixer, wg_in)


def _retention_consts(seq):
    half = QK_DIM // 2
    inv = ROPE_BASE ** (-jnp.arange(half, dtype=F32) / half)
    ang = jnp.arange(seq, dtype=F32)[:, None] * inv[None, :]
    cos, sin = jnp.cos(ang), jnp.sin(ang)
    cos_full = jnp.concatenate([cos, cos], axis=1)
    sin_signed = jnp.concatenate([-sin, sin], axis=1)
    log_g = jnp.log(1.0 - 2.0 ** (-5.0 - jnp.arange(RET_HEADS, dtype=F32)))
    idx = jnp.arange(CHUNK, dtype=F32)
    diff = idx[:, None] - idx[None, :]
    decay_in = jnp.where(diff[None] >= 0, jnp.exp(jnp.maximum(diff, 0.0)[None] * log_g[:, None, None]), 0.0)
    zeta = jnp.exp((CHUNK - 1.0 - idx)[None, :] * log_g[:, None])
    xi = jnp.exp((idx + 1.0)[None, :] * log_g[:, None])
    zeta = jnp.broadcast_to(zeta[:, :, None], (RET_HEADS, CHUNK, QK_DIM))
    xi = jnp.broadcast_to(xi[:, :, None], (RET_HEADS, CHUNK, QK_DIM))
    return cos_full, sin_signed, decay_in, zeta, xi


def _qkv(pj_ref, rows, h):
    q = pj_ref[rows, h * QK_DIM:(h + 1) * QK_DIM]
    k = pj_ref[rows, RET_HEADS * QK_DIM + h * QK_DIM:RET_HEADS * QK_DIM + (h + 1) * QK_DIM]
    v = pj_ref[rows, 2 * RET_HEADS * QK_DIM + h * V_DIM:2 * RET_HEADS * QK_DIM + (h + 1) * V_DIM]
    return q, k, v


def _retention_fwd_call(pj0, consts, n_seq, seq):
    cos_full, sin_signed, decay_in, zeta, xi = consts
    nb = seq // TILE_M
    cpb = TILE_M // CHUNK

    def body(pj_ref, cos_ref, sin_ref, d_ref, zeta_ref, xi_ref, o_ref, st_ref, state):
        @pl.when(pl.program_id(1) == 0)
        def _():
            state[...] = jnp.zeros_like(state)

        for cc in range(cpb):
            rows = slice(cc * CHUNK, (cc + 1) * CHUNK)
            cs, sn = cos_ref[rows, :], sin_ref[rows, :]
            for h in range(RET_HEADS):
                q, k, v = _qkv(pj_ref, rows, h)
                qt = (q * cs + _rot(q) * sn) * QK_SCALE
                kt = k * cs + _rot(k) * sn
                prev = state[h]
                st_ref[cc, h] = prev.astype(st_ref.dtype)
                scores = _mm_nt(qt, kt) * d_ref[h]
                o_ref[rows, h * V_DIM:(h + 1) * V_DIM] = _mm(scores, v) + _mm(qt * xi_ref[h], prev)
                state[h] = _mm_tn(kt * zeta_ref[h], v) + CHUNK_DECAY[h] * prev

    row = lambda b, n: (b * nb + n, 0)
    pos = lambda b, n: (n, 0)
    return pl.pallas_call(
        body, name="retention_fwd", grid=(n_seq, nb),
        in_specs=[pl.BlockSpec((TILE_M, SHARD_W), row), pl.BlockSpec((TILE_M, QK_DIM), pos),
                  pl.BlockSpec((TILE_M, QK_DIM), pos), _resident(decay_in.shape), _resident(zeta.shape),
                  _resident(xi.shape)],
        out_specs=[pl.BlockSpec((TILE_M, D_MODEL), row),
                   pl.BlockSpec((cpb, RET_HEADS, QK_DIM, V_DIM), lambda b, n: (b * nb + n, 0, 0, 0))],
        out_shape=[jax.ShapeDtypeStruct((n_seq * seq, D_MODEL), F32),
                   jax.ShapeDtypeStruct((n_seq * seq // CHUNK, RET_HEADS, QK_DIM, V_DIM), MXU_DTYPE)],
        scratch_shapes=[pltpu.VMEM((RET_HEADS, QK_DIM, V_DIM), F32)],
        compiler_params=_params(("arbitrary", "arbitrary")),
    )(pj0, cos_full, sin_signed, decay_in, zeta, xi)


def _merge_fwd_call(pj1, pj2, pj3, ret_raw, x2d, ws, bst, w_ro, w_so, w_o):
    t = x2d.shape[0]

    def body(pj1_ref, pj2_ref, pj3_ref, rr_ref, x_ref, ws_ref, bst_ref, wro_ref, wso_ref, wo_ref,
             x1_ref, ret_ref, sgu_ref, mg_ref, a_ref, b_ref):
        f = _branch_forward(pj1_ref[...], pj2_ref[...], rr_ref[...], _causal_ws(ws_ref), bst_ref[...])
        ret = f["ret"].astype(MXU_DTYPE)
        sgu = f["sgu"].astype(MXU_DTYPE)
        a = jnp.dot(ret, wro_ref[...], preferred_element_type=F32)
        b = jnp.dot(sgu, wso_ref[...], preferred_element_type=F32)
        pj3 = pj3_ref[...]
        merged = (_sigmoid(pj3[:, :D_MODEL]) * a + _sigmoid(pj3[:, D_MODEL:]) * b).astype(MXU_DTYPE)
        x1_ref[...] = x_ref[...] + jnp.dot(merged, wo_ref[...], preferred_element_type=F32)
        ret_ref[...] = ret
        sgu_ref[...] = sgu
        mg_ref[...] = merged
        a_ref[...] = a
        b_ref[...] = b

    sq = (D_MODEL, D_MODEL)
    return pl.pallas_call(
        body, name="merge_fwd", grid=(t // TILE_M,),
        in_specs=[_row_tile(SHARD_W)] * 3 + [_row_tile(D_MODEL)] * 2
        + [_resident(ws.shape), _resident(bst.shape), _resident(sq), _resident(sq), _resident(sq)],
        out_specs=[_row_tile(D_MODEL)] * 6,
        out_shape=[jax.ShapeDtypeStruct((t, D_MODEL), F32)] + [jax.ShapeDtypeStruct((t, D_MODEL), MXU_DTYPE)] * 3
        + [jax.ShapeDtypeStruct((t, D_MODEL), F32)] * 2,
        compiler_params=_params(("arbitrary",)),
    )(pj1, pj2, pj3, ret_raw, x2d, ws, bst, w_ro, w_so, w_o)


def _ple_call(x1, p2d, target, g_ple, g_final, w_pg, w_pp):
    t = x1.shape[0]

    def body(x1_ref, p_ref, t_ref, gp_ref, gf_ref, wpg_ref, wpp_ref,
             dx1_ref, hp_ref, dz_ref, dpp_ref, loss_ref, dgp_ref, dgf_ref):
        @pl.when(pl.program_id(0) == 0)
        def _():
            loss_ref[...] = jnp.zeros_like(loss_ref)
            dgp_ref[...] = jnp.zeros_like(dgp_ref)
            dgf_ref[...] = jnp.zeros_like(dgf_ref)

        x1v = x1_ref[...]
        xn1, r1 = _rms(x1v)
        hp = (xn1 * gp_ref[...]).astype(MXU_DTYPE)
        gate = _sigmoid(jnp.dot(hp, wpg_ref[...], preferred_element_type=F32))
        pp = jnp.dot(p_ref[...].astype(MXU_DTYPE), wpp_ref[...], preferred_element_type=F32)
        x2 = x1v + gate * pp
        xn2, r2 = _rms(x2)
        err = xn2 * gf_ref[...] - t_ref[...]
        loss_ref[...] += (0.5 / D_MODEL) * jnp.sum(jnp.sum(err * err, axis=1, keepdims=True), axis=0, keepdims=True)
        dy = err * (1.0 / D_MODEL)
        dgf_ref[...] += jnp.sum(dy * xn2, axis=0, keepdims=True)
        dx2 = _rms_bwd(dy * gf_ref[...], xn2, r2)
        dpp = (dx2 * gate).astype(MXU_DTYPE)
        dz = (dx2 * pp * gate * (1.0 - gate)).astype(MXU_DTYPE)
        dhp = _mm_nt(dz, wpg_ref[...])
        dgp_ref[...] += jnp.sum(dhp * xn1, axis=0, keepdims=True)
        dx1_ref[...] = dx2 + _rms_bwd(dhp * gp_ref[...], xn1, r1)
        hp_ref[...] = hp
        dz_ref[...] = dz
        dpp_ref[...] = dpp

    vec = _resident((1, D_MODEL))
    return pl.pallas_call(
        body, name="ple_fwd_bwd", grid=(t // TILE_M,),
        in_specs=[_row_tile(D_MODEL), _row_tile(PLE_DIM), _row_tile(D_MODEL), vec, vec,
                  _resident((D_MODEL, D_MODEL)), _resident((PLE_DIM, D_MODEL))],
        out_specs=[_row_tile(D_MODEL)] * 4 + [pl.BlockSpec((1, 1), lambda i: (0, 0)),
                                              pl.BlockSpec((1, D_MODEL), lambda i: (0, 0)),
                                              pl.BlockSpec((1, D_MODEL), lambda i: (0, 0))],
        out_shape=[jax.ShapeDtypeStruct((t, D_MODEL), F32)] + [jax.ShapeDtypeStruct((t, D_MODEL), MXU_DTYPE)] * 3
        + [jax.ShapeDtypeStruct((1, 1), F32), jax.ShapeDtypeStruct((1, D_MODEL), F32),
           jax.ShapeDtypeStruct((1, D_MODEL), F32)],
        compiler_params=_params(("arbitrary",)),
    )(x1, p2d, target, g_ple, g_final, w_pg, w_pp)


def _merge_bwd_call(dx1, pj1, pj2, pj3, ret_raw, a, b, ws, bst, w_ro, w_so, w_o):
    t = dx1.shape[0]

    def body(dx1_ref, pj1_ref, pj2_ref, pj3_ref, rr_ref, a_ref, b_ref, ws_ref, bst_ref, wro_ref, wso_ref, wo_ref,
             dpj1_ref, dpj2_ref, dpj3_ref, drr_ref, da_ref, db_ref, dws_ref, dbst_ref):
        @pl.when(pl.program_id(0) == 0)
        def _():
            dws_ref[...] = jnp.zeros_like(dws_ref)
            dbst_ref[...] = jnp.zeros_like(dbst_ref)

        wsc = _causal_ws(ws_ref)
        f = _branch_forward(pj1_ref[...], pj2_ref[...], rr_ref[...], wsc, bst_ref[...])
        pj3 = pj3_ref[...]
        smr, sms = _sigmoid(pj3[:, :D_MODEL]), _sigmoid(pj3[:, D_MODEL:])
        dmerged = _mm_nt(dx1_ref[...], wo_ref[...])
        da = (dmerged * smr).astype(MXU_DTYPE)
        db = (dmerged * sms).astype(MXU_DTYPE)
        dpj3_ref[:, :D_MODEL] = (dmerged * a_ref[...] * smr * (1.0 - smr)).astype(dpj3_ref.dtype)
        dpj3_ref[:, D_MODEL:] = (dmerged * b_ref[...] * sms * (1.0 - sms)).astype(dpj3_ref.dtype)
        da_ref[...] = da
        db_ref[...] = db

        dret = _mm_nt(da, wro_ref[...])
        dpj1_ref[:, :D_MODEL] = (dret * f["rn"] * f["dsil_rg"]).astype(dpj1_ref.dtype)
        drn = dret * f["sil_rg"]
        for h in range(RET_HEADS):
            cols = slice(h * V_DIM, (h + 1) * V_DIM)
            drr_ref[:, cols] = _unit_norm_bwd(drn[:, cols], f["rn"][:, cols], f["rstd_r"][h])

        dsgu = _mm_nt(db, wso_ref[...])
        dpj2_ref[:, D_MODEL:] = (dsgu * f["u"] * f["mixed"] * f["dsil_sg"]).astype(dpj2_ref.dtype)
        tg = dsgu * f["sil_sg"]
        dpj1_ref[:, D_MODEL:] = (tg * f["mixed"] * _gelu_grad(f["su"])).astype(dpj1_ref.dtype)
        dmixed = tg * f["u"]
        rows = []
        for cc in range(TILE_M // CHUNK):
            cols = []
            for g in range(SGU_GROUPS):
                rs, cs = slice(cc * CHUNK, (cc + 1) * CHUNK), slice(g * GROUP_DIM, (g + 1) * GROUP_DIM)
                dm = dmixed[rs, cs]
                cols.append(_mm_tn(wsc[g], dm))
                dws_ref[g] += _mm_nt(dm, f["vn"][rs, cs])
                dbst_ref[:, g:g + 1] += jnp.sum(dm, axis=1, keepdims=True)
            rows.append(jnp.concatenate(cols, axis=1))
        dvn = jnp.concatenate(rows, axis=0)
        dvv = _unit_norm_bwd(dvn, f["vn"], f["rstd_v"])
        dpj2_ref[:, :D_MODEL] = (dvv * _gelu_grad(f["sv"])).astype(dpj2_ref.dtype)

    sq = (D_MODEL, D_MODEL)
    return pl.pallas_call(
        body, name="merge_bwd", grid=(t // TILE_M,),
        in_specs=[_row_tile(D_MODEL)] + [_row_tile(SHARD_W)] * 3 + [_row_tile(D_MODEL)] * 3
        + [_resident(ws.shape), _resident(bst.shape), _resident(sq), _resident(sq), _resident(sq)],
        out_specs=[_row_tile(SHARD_W)] * 3 + [_row_tile(D_MODEL)] * 3
        + [pl.BlockSpec(ws.shape, lambda i: (0, 0, 0)), pl.BlockSpec(bst.shape, lambda i: (0, 0))],
        out_shape=[jax.ShapeDtypeStruct((t, SHARD_W), MXU_DTYPE)] * 3 + [jax.ShapeDtypeStruct((t, D_MODEL), F32)]
        + [jax.ShapeDtypeStruct((t, D_MODEL), MXU_DTYPE)] * 2
        + [jax.ShapeDtypeStruct(ws.shape, F32), jax.ShapeDtypeStruct(bst.shape, F32)],
        compiler_params=_params(("arbitrary",)),
    )(dx1, pj1, pj2, pj3, ret_raw, a, b, ws, bst, w_ro, w_so, w_o)


def _retention_bwd_call(pj0, drr, states, consts, n_seq, seq):
    cos_full, sin_signed, decay_in, zeta, xi = consts
    nb = seq // TILE_M
    cpb = TILE_M // CHUNK

    def body(pj_ref, do_ref, st_ref, cos_ref, sin_ref, d_ref, zeta_ref, xi_ref, dpj_ref, gstate):
        @pl.when(pl.program_id(1) == 0)
        def _():
            gstate[...] = jnp.zeros_like(gstate)

        for cc in reversed(range(cpb)):
            rows = slice(cc * CHUNK, (cc + 1) * CHUNK)
            cs, sn = cos_ref[rows, :], sin_ref[rows, :]
            for h in range(RET_HEADS):
                q, k, v = _qkv(pj_ref, rows, h)
                qt = (q * cs + _rot(q) * sn) * QK_SCALE
                kt = k * cs + _rot(k) * sn
                d_out = do_ref[rows, h * V_DIM:(h + 1) * V_DIM]
                prev = st_ref[cc, h]
                g = gstate[h]
                dec = d_ref[h]
                scores_d = _mm_nt(qt, kt) * dec
                dscores = _mm_nt(d_out, v) * dec
                kz = kt * zeta_ref[h]
                qx = qt * xi_ref[h]
                dv = _mm_tn(scores_d, d_out) + _mm(kz, g)
                dqt = (_mm(dscores, kt) + _mm_nt(d_out, prev) * xi_ref[h]) * QK_SCALE
                dkt = _mm_tn(dscores, qt) + _mm_nt(v, g) * zeta_ref[h]
                gstate[h] = _mm_tn(qx, d_out) + CHUNK_DECAY[h] * g
                dq = dqt * cs + _rot(dqt * sn)
                dk = dkt * cs + _rot(dkt * sn)
                dpj_ref[rows, h * QK_DIM:(h + 1) * QK_DIM] = dq.astype(dpj_ref.dtype)
                dpj_ref[rows, RET_HEADS * QK_DIM + h * QK_DIM:RET_HEADS * QK_DIM + (h + 1) * QK_DIM] = dk.astype(
                    dpj_ref.dtype)
                dpj_ref[rows, 2 * RET_HEADS * QK_DIM + h * V_DIM:2 * RET_HEADS * QK_DIM + (h + 1) * V_DIM] = dv.astype(
                    dpj_ref.dtype)

    row = lambda b, n: (b * nb + nb - 1 - n, 0)
    pos = lambda b, n: (nb - 1 - n, 0)
    return pl.pallas_call(
        body, name="retention_bwd", grid=(n_seq, nb),
        in_specs=[pl.BlockSpec((TILE_M, SHARD_W), row), pl.BlockSpec((TILE_M, D_MODEL), row),
                  pl.BlockSpec((cpb, RET_HEADS, QK_DIM, V_DIM), lambda b, n: (b * nb + nb - 1 - n, 0, 0, 0)),
                  pl.BlockSpec((TILE_M, QK_DIM), pos), pl.BlockSpec((TILE_M, QK_DIM), pos),
                  _resident(decay_in.shape), _resident(zeta.shape), _resident(xi.shape)],
        out_specs=pl.BlockSpec((TILE_M, SHARD_W), row),
        out_shape=jax.ShapeDtypeStruct((n_seq * seq, SHARD_W), MXU_DTYPE),
        scratch_shapes=[pltpu.VMEM((RET_HEADS, QK_DIM, V_DIM), F32)],
        compiler_params=_params(("arbitrary", "arbitrary")),
    )(pj0, drr, states, cos_full, sin_signed, decay_in, zeta, xi)


def _dx_call(dpj, x2d, dx1, g_mixer, wg_in):
    t = x2d.shape[0]

    def body(d0, d1, d2, d3, x_ref, dx1_ref, g_ref, w_ref, dx_ref, dg_ref):
        @pl.when(pl.program_id(0) == 0)
        def _():
            dg_ref[...] = jnp.zeros_like(dg_ref)

        dh = _mm_nt(d0[...], w_ref[0])
        for j, d_ref in enumerate((d1, d2, d3)):
            dh += _mm_nt(d_ref[...], w_ref[j + 1])
        xn, r = _rms(x_ref[...])
        dg_ref[...] += jnp.sum(dh * xn, axis=0, keepdims=True)
        dx_ref[...] = dx1_ref[...] + _rms_bwd(dh * g_ref[...], xn, r)

    return pl.pallas_call(
        body, name="dx_bwd", grid=(t // TILE_M,),
        in_specs=[_row_tile(SHARD_W)] * N_CHIPS + [_row_tile(D_MODEL)] * 2
        + [_resident((1, D_MODEL)), _resident(wg_in.shape)],
        out_specs=[_row_tile(D_MODEL), pl.BlockSpec((1, D_MODEL), lambda i: (0, 0))],
        out_shape=[jax.ShapeDtypeStruct((t, D_MODEL), F32), jax.ShapeDtypeStruct((1, D_MODEL), F32)],
        compiler_params=_params(("arbitrary",)),
    )(*dpj, x2d, dx1, g_mixer, wg_in)


def _wgrad_call(name, lhs, rhs, block_n, block_t=512):
    t, k = lhs.shape
    n = rhs.shape[1]
    steps = t // block_t

    def body(l_ref, r_ref, o_ref, acc):
        @pl.when(pl.program_id(1) == 0)
        def _():
            acc[...] = jnp.zeros_like(acc)

        acc[...] += _mm_tn(l_ref[...], r_ref[...])

        @pl.when(pl.program_id(1) == steps - 1)
        def _():
            o_ref[...] = acc[...].astype(o_ref.dtype)

    return pl.pallas_call(
        body, name=name, grid=(n // block_n, steps),
        in_specs=[pl.BlockSpec((block_t, k), lambda j, i: (i, 0)), pl.BlockSpec((block_t, block_n), lambda j, i: (i, j))],
        out_specs=pl.BlockSpec((None, k, block_n), lambda j, i: (j, 0, 0)),
        out_shape=jax.ShapeDtypeStruct((n // block_n, k, block_n), COMM_DTYPE),
        scratch_shapes=[pltpu.VMEM((k, block_n), F32)],
        compiler_params=_params(("arbitrary", "arbitrary")),
    )(lhs, rhs)


def _position():
    return lax.axis_index("x"), lax.axis_index("y"), lax.axis_index("c")


def _other_chip(x, y, k):
    return (1 - x if k & 2 else x), (1 - y if k & 1 else y)


def _exchange_call(name, operands, out_shapes, plan, n_local, n_remote):
    n_in = len(operands)

    def body(*refs):
        ins, outs = refs[:n_in], refs[n_in:n_in + len(out_shapes)]
        loc_sem, send_sem, recv_sem = refs[n_in + len(out_shapes):]
        local, remote = plan(ins, outs)
        assert len(local) == n_local and len(remote) == n_remote
        copies = [pltpu.make_async_copy(src, dst, loc_sem.at[i]) for i, (src, dst) in enumerate(local)]
        copies += [pltpu.make_async_remote_copy(src_ref=src, dst_ref=dst, send_sem=send_sem.at[i],
                                                recv_sem=recv_sem.at[i], device_id=dev, device_id_type=MESH)
                   for i, (src, dst, dev) in enumerate(remote)]
        for cp in copies:
            cp.start()
        for cp in copies:
            cp.wait()

    return pl.pallas_call(
        body, name=name, in_specs=[ANY] * n_in, out_specs=[ANY] * len(out_shapes), out_shape=out_shapes,
        scratch_shapes=[pltpu.SemaphoreType.DMA((max(n_local, 1),)), pltpu.SemaphoreType.DMA((n_remote,)),
                        pltpu.SemaphoreType.DMA((n_remote,))],
        compiler_params=pltpu.CompilerParams(has_side_effects=True),
    )(*operands)


def _gather_weights(shards):
    n = len(shards)
    out_shapes = [jax.ShapeDtypeStruct((N_CHIPS,) + s.shape, s.dtype) for s in shards]

    def body(*refs):
        ins, outs = refs[:n], refs[n:2 * n]
        loc_sem, send_sem, recv_sem = refs[2 * n:]
        x, y, c = _position()
        me = 2 * x + y
        sibling = (x, y, 1 - c)

        def copy(i, slot, piece, dev, src=None):
            return pltpu.make_async_remote_copy(src_ref=piece if src is None else src, dst_ref=piece,
                                                send_sem=send_sem.at[i, slot], recv_sem=recv_sem.at[i, slot],
                                                device_id=dev, device_id_type=MESH)

        def half(i, chip, core):
            hr = shards[i].shape[0] // 2
            return outs[i].at[chip, pl.ds(core * hr, hr)]

        own = [pltpu.make_async_copy(ins[i], outs[i].at[me], loc_sem.at[i]) for i in range(n)]
        for cp in own:
            cp.start()
        sent = []
        for i in range(n):
            hr = shards[i].shape[0] // 2
            for k in (1, 2, 3):
                px, py = _other_chip(x, y, k)
                sent.append(copy(i, k - 1, half(i, me, c), (px, py, c), src=ins[i].at[pl.ds(c * hr, hr)]))
                sent[-1].start()
        for i in range(n):
            for k in (1, 2, 3):
                px, py = _other_chip(x, y, k)
                landed = half(i, 2 * px + py, c)
                copy(i, k - 1, landed, sibling).wait_recv()
                sent.append(copy(i, 2 + k, landed, sibling))
                sent[-1].start()
        for i in range(n):
            for k in (1, 2, 3):
                px, py = _other_chip(x, y, k)
                copy(i, 2 + k, half(i, 2 * px + py, 1 - c), sibling).wait_recv()
        for cp in sent:
            cp.wait_send()
        for cp in own:
            cp.wait()

    return pl.pallas_call(
        body, name="gather_weights", in_specs=[ANY] * n, out_specs=[ANY] * n, out_shape=out_shapes,
        scratch_shapes=[pltpu.SemaphoreType.DMA((n,)), pltpu.SemaphoreType.DMA((n, 6)),
                        pltpu.SemaphoreType.DMA((n, 6))],
        compiler_params=pltpu.CompilerParams(has_side_effects=True),
    )(*shards)


def _sum_call(name, terms, out_dtype, block_rows=256):
    rows, width = terms[0][0].shape[0] // _n_slabs(terms[0]), terms[0][0].shape[1]
    block_rows = min(block_rows, rows)
    steps = rows // block_rows

    def body(*refs):
        acc = refs[0][...].astype(F32)
        for r in refs[1:-1]:
            acc = acc + r[...].astype(F32)
        refs[-1][...] = acc.astype(out_dtype)

    def spec(slab):
        return pl.BlockSpec((block_rows, width), lambda i: (slab * steps + i, 0))

    return pl.pallas_call(
        body, name=name, grid=(steps,), in_specs=[spec(slab) for _, slab, _ in terms],
        out_specs=pl.BlockSpec((block_rows, width), lambda i: (i, 0)),
        out_shape=jax.ShapeDtypeStruct((rows, width), out_dtype),
        compiler_params=_params(("arbitrary",)),
    )(*[arr for arr, _, _ in terms])


def _n_slabs(term):
    return term[2]


def _reduce_scatter(grads):
    n = len(grads)
    shapes = [(g.shape[0] // N_CHIPS, g.shape[1]) for g in grads]
    halves = [(r // 2, w) for r, w in shapes]

    def pair_plan(ins, outs):
        x, y, c = _position()
        local, remote = [], []
        for i in range(n):
            (r, _), (hr, _) = shapes[i], halves[i]
            for j in range(N_CHIPS):
                local.append((ins[i].at[pl.ds(j * r + c * hr, hr)], outs[2 * i].at[pl.ds(j * hr, hr)]))
                remote.append((ins[i].at[pl.ds(j * r + (1 - c) * hr, hr)], outs[2 * i + 1].at[pl.ds(j * hr, hr)],
                               (x, y, 1 - c)))
        return local, remote

    pair_shapes = []
    for hr, w in halves:
        pair_shapes += [jax.ShapeDtypeStruct((N_CHIPS * hr, w), COMM_DTYPE)] * 2
    pair = _exchange_call("grad_pair_exchange", grads, pair_shapes, pair_plan, N_CHIPS * n, N_CHIPS * n)
    pair_sums = [_sum_call(f"grad_pair_sum_{i}", [(pair[2 * i], 0, 1), (pair[2 * i + 1], 0, 1)], COMM_DTYPE)
                 for i in range(n)]

    def chip_plan(ins, outs):
        x, y, c = _position()
        me = 2 * x + y
        local, remote = [], []
        for i in range(n):
            hr = halves[i][0]
            local.append((ins[i].at[pl.ds(me * hr, hr)], outs[i].at[pl.ds(0, hr)]))
            for k in (1, 2, 3):
                px, py = _other_chip(x, y, k)
                remote.append((ins[i].at[pl.ds((2 * px + py) * hr, hr)], outs[i].at[pl.ds(k * hr, hr)], (px, py, c)))
        return local, remote

    chip = _exchange_call("grad_chip_exchange", pair_sums,
                          [jax.ShapeDtypeStruct((N_CHIPS * hr, w), COMM_DTYPE) for hr, w in halves],
                          chip_plan, n, 3 * n)
    reduced = [_sum_call(f"grad_chip_sum_{i}", [(chip[i], k, N_CHIPS) for k in range(N_CHIPS)], F32)
               for i in range(n)]

    def half_plan(ins, outs):
        x, y, c = _position()
        local, remote = [], []
        for i in range(n):
            hr = halves[i][0]
            mine = outs[i].at[pl.ds(c * hr, hr)]
            local.append((ins[i], mine))
            remote.append((ins[i], mine, (x, y, 1 - c)))
        return local, remote

    return _exchange_call("grad_half_exchange", reduced, [jax.ShapeDtypeStruct(s, F32) for s in shapes],
                          half_plan, n, n)


def _gather_small(pack):
    rows = pack.shape[0]

    def plan(ins, outs):
        x, y, c = _position()
        mine = outs[0].at[pl.ds((4 * x + 2 * y + c) * rows, rows)]
        remote = []
        for d in range(1, 8):
            px, py = _other_chip(x, y, d >> 1)
            remote.append((ins[0], mine, (px, py, 1 - c if d & 1 else c)))
        return [(ins[0], mine)], remote

    return _exchange_call("small_grad_gather", [pack], [jax.ShapeDtypeStruct((8 * rows, 128), F32)], plan, 1, 7)[0]


def _adamw(w, g, m, v):
    m = ADAM_B1 * m + (1.0 - ADAM_B1) * g
    v = ADAM_B2 * v + (1.0 - ADAM_B2) * (g * g)
    m_hat = m / (1.0 - ADAM_B1 ** ADAM_STEP)
    v_hat = v / (1.0 - ADAM_B2 ** ADAM_STEP)
    delta = -ADAM_LR * (m_hat / (jnp.sqrt(v_hat) + ADAM_EPS) + ADAM_WD * w)
    return delta, m, v


def _adamw_call(name, w, g, m, v):
    rows, width = w.shape
    block_rows = min(256, rows)

    def body(w_ref, g_ref, m_ref, v_ref, d_out, m_out, v_out):
        d_out[...], m_out[...], v_out[...] = _adamw(w_ref[...], g_ref[...], m_ref[...], v_ref[...])

    spec = pl.BlockSpec((block_rows, width), lambda i: (i, 0))
    return pl.pallas_call(
        body, name=name, grid=(rows // block_rows,), in_specs=[spec] * 4, out_specs=[spec] * 3,
        out_shape=[jax.ShapeDtypeStruct(w.shape, F32)] * 3,
        compiler_params=_params(("arbitrary",)),
    )(w, g, m, v)


SMALL_WS_ROWS = SGU_GROUPS * CHUNK


def _small_adamw_call(gathered, w, m, v):
    rows = w.shape[0]

    def body(all_ref, w_ref, m_ref, v_ref, g_out, d_out, m_out, v_out):
        g = all_ref[0:rows, :]
        for d in range(1, 8):
            g = g + all_ref[d * rows:(d + 1) * rows, :]
        row = lax.broadcasted_iota(jnp.int32, (rows, CHUNK), 0)
        col = lax.broadcasted_iota(jnp.int32, (rows, CHUNK), 1)
        keep = jnp.logical_or(row >= SMALL_WS_ROWS, (row % CHUNK) >= col)
        g = jnp.where(keep, g, 0.0)
        g_out[...] = g
        d_out[...], m_out[...], v_out[...] = _adamw(w_ref[...], g, m_ref[...], v_ref[...])

    return pl.pallas_call(
        body, name="small_adamw", out_shape=[jax.ShapeDtypeStruct(w.shape, F32)] * 4,
        compiler_params=pltpu.CompilerParams(vmem_limit_bytes=VMEM_LIMIT),
    )(gathered, w, m, v)


def _pack_small(ws, bs, g_mixer, g_ple, g_final):
    parts = [ws.reshape(SMALL_WS_ROWS, CHUNK), bs.reshape(SGU_GROUPS, CHUNK), g_mixer.reshape(8, 128),
             g_ple.reshape(8, 128), g_final.reshape(8, 128), jnp.zeros((4, 128), F32)]
    return jnp.concatenate(parts, axis=0)


def _unpack_small(pack):
    o = SMALL_WS_ROWS
    return (pack[:o].reshape(1, SGU_GROUPS, CHUNK, CHUNK), pack[o:o + 4].reshape(1, SGU_GROUPS, CHUNK),
            pack[o + 4:o + 12].reshape(1, D_MODEL), pack[o + 12:o + 20].reshape(1, D_MODEL),
            pack[o + 20:o + 28].reshape(D_MODEL))


def kernel(x, p, w_in, w_ret_out, w_sgu_out, w_out, sgu_ws, sgu_bs, w_ple_gate, w_ple_proj, g_mixer, g_ple, g_final, loss_target, m_w_in, m_w_ret_out, m_w_sgu_out, m_w_out, m_sgu_ws, m_sgu_bs, m_w_ple_gate, m_w_ple_proj, m_g_mixer, m_g_ple, m_g_final, v_w_in, v_w_ret_out, v_w_sgu_out, v_w_out, v_sgu_ws, v_sgu_bs, v_w_ple_gate, v_w_ple_proj, v_g_mixer, v_g_ple, v_g_final):
    n_seq, seq, _ = x.shape
    t = n_seq * seq
    x2d = x.reshape(t, D_MODEL)
    p2d = p.reshape(t, PLE_DIM)
    target = loss_target.reshape(t, D_MODEL)
    big = [w_in[0], w_ret_out[0], w_sgu_out[0], w_out[0], w_ple_gate[0], w_ple_proj[0]]
    big_m = [m_w_in[0], m_w_ret_out[0], m_w_sgu_out[0], m_w_out[0], m_w_ple_gate[0], m_w_ple_proj[0]]
    big_v = [v_w_in[0], v_w_ret_out[0], v_w_sgu_out[0], v_w_out[0], v_w_ple_gate[0], v_w_ple_proj[0]]

    wg_in, wg_ro, wg_so, wg_o, wg_pg, wg_pp = _gather_weights([w.astype(MXU_DTYPE) for w in big])
    w_ro, w_so, w_o, w_pg = (w.reshape(D_MODEL, D_MODEL) for w in (wg_ro, wg_so, wg_o, wg_pg))
    w_pp = wg_pp.transpose(1, 0, 2).reshape(PLE_DIM, D_MODEL)
    ws = sgu_ws[0]
    bst = sgu_bs[0].T
    consts = _retention_consts(seq)

    h, pj0, pj1, pj2, pj3 = _proj_call(x2d, g_mixer, wg_in)
    ret_raw, states = _retention_fwd_call(pj0, consts, n_seq, seq)
    x1, ret, sgu, merged, a, b = _merge_fwd_call(pj1, pj2, pj3, ret_raw, x2d, ws, bst, w_ro, w_so, w_o)
    dx1, hp, dz, dpp, loss, dg_ple, dg_final = _ple_call(x1, p2d, target, g_ple, g_final.reshape(1, D_MODEL),
                                                         w_pg, w_pp)
    dpj1, dpj2, dpj3, drr, da, db, dws, dbst = _merge_bwd_call(dx1, pj1, pj2, pj3, ret_raw, a, b, ws, bst,
                                                               w_ro, w_so, w_o)
    dpj0 = _retention_bwd_call(pj0, drr, states, consts, n_seq, seq)
    dx, dg_mixer = _dx_call((dpj0, dpj1, dpj2, dpj3), x2d, dx1, g_mixer, wg_in)

    rows_of = lambda g: g.reshape(g.shape[0] * g.shape[1], g.shape[2])
    grads = [
        jnp.concatenate([rows_of(_wgrad_call(f"wgrad_in_{j}", h, d, SHARD_W)) for j, d in
                         enumerate((dpj0, dpj1, dpj2, dpj3))], axis=0),
        rows_of(_wgrad_call("wgrad_ret_out", ret, da, D_MODEL)),
        rows_of(_wgrad_call("wgrad_sgu_out", sgu, db, D_MODEL)),
        rows_of(_wgrad_call("wgrad_out", merged, dx1, D_MODEL)),
        rows_of(_wgrad_call("wgrad_ple_gate", hp, dz, D_MODEL)),
        rows_of(_wgrad_call("wgrad_ple_proj", p2d, dpp, PLE_DIM)),
    ]
    g_big = _reduce_scatter(grads)
    upd = [_adamw_call(f"adamw_{i}", big[i], g_big[i], big_m[i], big_v[i]) for i in range(len(big))]

    small_w = _pack_small(sgu_ws, sgu_bs, g_mixer, g_ple, g_final)
    small_m = _pack_small(m_sgu_ws, m_sgu_bs, m_g_mixer, m_g_ple, m_g_final)
    small_v = _pack_small(v_sgu_ws, v_sgu_bs, v_g_mixer, v_g_ple, v_g_final)
    small_g = _gather_small(_pack_small(dws, dbst.T, dg_mixer, dg_ple, dg_final))
    small = [_unpack_small(a_) for a_ in _small_adamw_call(small_g, small_w, small_m, small_v)]

    total = lax.psum(loss[0, 0], ("x", "y", "c"))

    def ordered(big_list, small_tuple):
        w_in_, w_ro_, w_so_, w_o_, w_pg_, w_pp_ = [b_[None] for b_ in big_list]
        s_ws, s_bs, s_gm, s_gp, s_gf = small_tuple
        return [w_in_, w_ro_, w_so_, w_o_, s_ws, s_bs, w_pg_, w_pp_, s_gm, s_gp, s_gf]

    out = [total, dx.reshape(x.shape)]
    out += ordered(g_big, small[0])
    out += ordered([u[0] for u in upd], small[1])
    out += ordered([u[1] for u in upd], small[2])
    out += ordered([u[2] for u in upd], small[3])
    return tuple(out)
```

```python
import functools
import math

import numpy as np
import jax
import jax.numpy as jnp
from jax import lax
from jax.experimental import pallas as pl
from jax.experimental.pallas import tpu as pltpu

F32 = jnp.float32
MXU_DTYPE = jnp.bfloat16
COMM_DTYPE = jnp.bfloat16

D_MODEL = 1024
RET_HEADS = 4
QK_DIM = 128
V_DIM = 256
CHUNK = 128
SGU_GROUPS = 4
GROUP_DIM = 256
PLE_DIM = 256
N_CHIPS = 4
SHARD_W = 2048
ROPE_BASE = 10000.0
NORM_EPS = 1e-6
GN_EPS = 1e-5
QK_SCALE = QK_DIM ** -0.5
SQRT_HALF = math.sqrt(0.5)
INV_SQRT_2PI = 1.0 / math.sqrt(2.0 * math.pi)

ADAM_LR = 0.001
ADAM_B1 = 0.9
ADAM_B2 = 0.999
ADAM_EPS = 1e-08
ADAM_WD = 0.01
ADAM_STEP = 10

TILE_M = 256
VMEM_LIMIT = 56 * 1024 * 1024
MESH = pl.DeviceIdType.MESH
ANY = pl.BlockSpec(memory_space=pl.ANY)

CHUNK_DECAY = tuple(
    float(np.exp(np.float32(CHUNK) * np.log(np.float32(1.0 - 2.0 ** (-5.0 - h))))) for h in range(RET_HEADS))


def _mm(a, b):
    return jnp.dot(a.astype(MXU_DTYPE), b.astype(MXU_DTYPE), preferred_element_type=F32)


def _mm_nt(a, b):
    return lax.dot_general(a.astype(MXU_DTYPE), b.astype(MXU_DTYPE), (((1,), (1,)), ((), ())),
                           preferred_element_type=F32)


def _mm_tn(a, b):
    return lax.dot_general(a.astype(MXU_DTYPE), b.astype(MXU_DTYPE), (((0,), (0,)), ((), ())),
                           preferred_element_type=F32)


def _mean(x):
    return jnp.mean(x, axis=-1, keepdims=True)


def _sigmoid(x):
    return jax.nn.sigmoid(x)


def _silu_and_grad(x):
    s = _sigmoid(x)
    return x * s, s * (1.0 + x * (1.0 - s))


def _gelu(x):
    return 0.5 * x * (1.0 + lax.erf(x * SQRT_HALF))


def _gelu_grad(x):
    return 0.5 * (1.0 + lax.erf(x * SQRT_HALF)) + x * jnp.exp(-0.5 * x * x) * INV_SQRT_2PI


def _unit_norm(x, eps):
    xc = x - _mean(x)
    rstd = lax.rsqrt(_mean(xc * xc) + eps)
    return xc * rstd, rstd


def _unit_norm_bwd(dn, n, rstd):
    return rstd * (dn - _mean(dn) - n * _mean(dn * n))


def _rms(x):
    r = lax.rsqrt(_mean(x * x) + NORM_EPS)
    return x * r, r


def _rms_bwd(dxn, xn, r):
    return r * (dxn - xn * _mean(dxn * xn))


def _rot(x):
    return pltpu.roll(x, QK_DIM // 2, 1)


def _params(semantics, **kw):
    return pltpu.CompilerParams(dimension_semantics=semantics, vmem_limit_bytes=VMEM_LIMIT, **kw)


def _row_tile(width, tm=TILE_M):
    return pl.BlockSpec((tm, width), lambda i: (i, 0))


def _resident(shape):
    nd = len(shape)
    return pl.BlockSpec(shape, lambda *_: (0,) * nd, pipeline_mode=pl.Buffered(1))


def _causal_ws(ws_ref):
    row = lax.broadcasted_iota(jnp.int32, (CHUNK, CHUNK), 0)
    col = lax.broadcasted_iota(jnp.int32, (CHUNK, CHUNK), 1)
    return [jnp.where(row >= col, ws_ref[g], 0.0).astype(MXU_DTYPE) for g in range(SGU_GROUPS)]


def _heads(x, width):
    return [x[:, h * width:(h + 1) * width] for h in range(x.shape[1] // width)]


def _branch_forward(pj1, pj2, ret_raw, wsc, bst):
    rg, su = pj1[:, :D_MODEL], pj1[:, D_MODEL:]
    sv, sg = pj2[:, :D_MODEL], pj2[:, D_MODEL:]
    rn_parts, rstd_parts = zip(*[_unit_norm(r, GN_EPS) for r in _heads(ret_raw, V_DIM)])
    rn = jnp.concatenate(rn_parts, axis=1)
    sil_rg, dsil_rg = _silu_and_grad(rg)
    ret = rn * sil_rg
    u = _gelu(su)
    vn, rstd_v = _unit_norm(_gelu(sv), GN_EPS)
    rows = []
    for cc in range(pj1.shape[0] // CHUNK):
        cols = []
        for g in range(SGU_GROUPS):
            blk = vn[cc * CHUNK:(cc + 1) * CHUNK, g * GROUP_DIM:(g + 1) * GROUP_DIM]
            cols.append(_mm(wsc[g], blk) + bst[:, g:g + 1])
        rows.append(jnp.concatenate(cols, axis=1))
    mixed = jnp.concatenate(rows, axis=0)
    sil_sg, dsil_sg = _silu_and_grad(sg)
    sgu = u * mixed * sil_sg
    return dict(rg=rg, su=su, sv=sv, sg=sg, rn=rn, rstd_r=rstd_parts, sil_rg=sil_rg, dsil_rg=dsil_rg, ret=ret,
                u=u, vn=vn, rstd_v=rstd_v, mixed=mixed, sil_sg=sil_sg, dsil_sg=dsil_sg, sgu=sgu)


def _proj_call(x2d, g_mixer, wg_in):
    t = x2d.shape[0]

    def body(x_ref, g_ref, w_ref, h_ref, p0, p1, p2, p3):
        xn, _ = _rms(x_ref[...])
        h = (xn * g_ref[...]).astype(MXU_DTYPE)
        h_ref[...] = h
        for j, p_ref in enumerate((p0, p1, p2, p3)):
            p_ref[...] = jnp.dot(h, w_ref[j], preferred_element_type=F32)

    return pl.pallas_call(
        body, name="proj_fwd", grid=(t // TILE_M,),
        in_specs=[_row_tile(D_MODEL), _resident((1, D_MODEL)), _resident(wg_in.shape)],
        out_specs=[_row_tile(D_MODEL)] + [_row_tile(SHARD_W)] * N_CHIPS,
        out_shape=[jax.ShapeDtypeStruct((t, D_MODEL), MXU_DTYPE)]
        + [jax.ShapeDtypeStruct((t, SHARD_W), F32)] * N_CHIPS,
        compiler_params=_params(("arbitrary",)),
    )(x2d, g_mixer, wg_in)


def _retention_consts(seq):
    half = QK_DIM // 2
    inv = ROPE_BASE ** (-jnp.arange(half, dtype=F32) / half)
    ang = jnp.arange(seq, dtype=F32)[:, None] * inv[None, :]
    cos, sin = jnp.cos(ang), jnp.sin(ang)
    cos_full = jnp.concatenate([cos, cos], axis=1)
    sin_signed = jnp.concatenate([-sin, sin], axis=1)
    log_g = jnp.log(1.0 - 2.0 ** (-5.0 - jnp.arange(RET_HEADS, dtype=F32)))
    idx = jnp.arange(CHUNK, dtype=F32)
    diff = idx[:, None] - idx[None, :]
    decay_in = jnp.where(diff[None] >= 0, jnp.exp(jnp.maximum(diff, 0.0)[None] * log_g[:, None, None]), 0.0)
    zeta = jnp.exp((CHUNK - 1.0 - idx)[None, :] * log_g[:, None])
    xi = jnp.exp((idx + 1.0)[None, :] * log_g[:, None])
    zeta = jnp.broadcast_to(zeta[:, :, None], (RET_HEADS, CHUNK, QK_DIM))
    xi = jnp.broadcast_to(xi[:, :, None], (RET_HEADS, CHUNK, QK_DIM))
    return cos_full, sin_signed, decay_in, zeta, xi


def _qkv(pj_ref, rows, h):
    q = pj_ref[rows, h * QK_DIM:(h + 1) * QK_DIM]
    k = pj_ref[rows, RET_HEADS * QK_DIM + h * QK_DIM:RET_HEADS * QK_DIM + (h + 1) * QK_DIM]
    v = pj_ref[rows, 2 * RET_HEADS * QK_DIM + h * V_DIM:2 * RET_HEADS * QK_DIM + (h + 1) * V_DIM]
    return q, k, v


def _retention_fwd_call(pj0, consts, n_seq, seq):
    cos_full, sin_signed, decay_in, zeta, xi = consts
    nb = seq // TILE_M
    cpb = TILE_M // CHUNK

    def body(pj_ref, cos_ref, sin_ref, d_ref, zeta_ref, xi_ref, o_ref, st_ref, state):
        @pl.when(pl.program_id(1) == 0)
        def _():
            state[...] = jnp.zeros_like(state)

        for cc in range(cpb):
            rows = slice(cc * CHUNK, (cc + 1) * CHUNK)
            cs, sn = cos_ref[rows, :], sin_ref[rows, :]
            for h in range(RET_HEADS):
                q, k, v = _qkv(pj_ref, rows, h)
                qt = (q * cs + _rot(q) * sn) * QK_SCALE
                kt = k * cs + _rot(k) * sn
                prev = state[h]
                st_ref[cc, h] = prev.astype(st_ref.dtype)
                scores = _mm_nt(qt, kt) * d_ref[h]
                o_ref[rows, h * V_DIM:(h + 1) * V_DIM] = _mm(scores, v) + _mm(qt * xi_ref[h], prev)
                state[h] = _mm_tn(kt * zeta_ref[h], v) + CHUNK_DECAY[h] * prev

    row = lambda b, n: (b * nb + n, 0)
    pos = lambda b, n: (n, 0)
    return pl.pallas_call(
        body, name="retention_fwd", grid=(n_seq, nb),
        in_specs=[pl.BlockSpec((TILE_M, SHARD_W), row), pl.BlockSpec((TILE_M, QK_DIM), pos),
                  pl.BlockSpec((TILE_M, QK_DIM), pos), _resident(decay_in.shape), _resident(zeta.shape),
                  _resident(xi.shape)],
        out_specs=[pl.BlockSpec((TILE_M, D_MODEL), row),
                   pl.BlockSpec((cpb, RET_HEADS, QK_DIM, V_DIM), lambda b, n: (b * nb + n, 0, 0, 0))],
        out_shape=[jax.ShapeDtypeStruct((n_seq * seq, D_MODEL), F32),
                   jax.ShapeDtypeStruct((n_seq * seq // CHUNK, RET_HEADS, QK_DIM, V_DIM), MXU_DTYPE)],
        scratch_shapes=[pltpu.VMEM((RET_HEADS, QK_DIM, V_DIM), F32)],
        compiler_params=_params(("arbitrary", "arbitrary")),
    )(pj0, cos_full, sin_signed, decay_in, zeta, xi)


def _merge_fwd_call(pj1, pj2, pj3, ret_raw, x2d, ws, bst, w_ro, w_so, w_o):
    t = x2d.shape[0]

    def body(pj1_ref, pj2_ref, pj3_ref, rr_ref, x_ref, ws_ref, bst_ref, wro_ref, wso_ref, wo_ref,
             x1_ref, ret_ref, sgu_ref, mg_ref, a_ref, b_ref):
        f = _branch_forward(pj1_ref[...], pj2_ref[...], rr_ref[...], _causal_ws(ws_ref), bst_ref[...])
        ret = f["ret"].astype(MXU_DTYPE)
        sgu = f["sgu"].astype(MXU_DTYPE)
        a = jnp.dot(ret, wro_ref[...], preferred_element_type=F32)
        b = jnp.dot(sgu, wso_ref[...], preferred_element_type=F32)
        pj3 = pj3_ref[...]
        merged = (_sigmoid(pj3[:, :D_MODEL]) * a + _sigmoid(pj3[:, D_MODEL:]) * b).astype(MXU_DTYPE)
        x1_ref[...] = x_ref[...] + jnp.dot(merged, wo_ref[...], preferred_element_type=F32)
        ret_ref[...] = ret
        sgu_ref[...] = sgu
        mg_ref[...] = merged
        a_ref[...] = a
        b_ref[...] = b

    sq = (D_MODEL, D_MODEL)
    return pl.pallas_call(
        body, name="merge_fwd", grid=(t // TILE_M,),
        in_specs=[_row_tile(SHARD_W)] * 3 + [_row_tile(D_MODEL)] * 2
        + [_resident(ws.shape), _resident(bst.shape), _resident(sq), _resident(sq), _resident(sq)],
        out_specs=[_row_tile(D_MODEL)] * 6,
        out_shape=[jax.ShapeDtypeStruct((t, D_MODEL), F32)] + [jax.ShapeDtypeStruct((t, D_MODEL), MXU_DTYPE)] * 3
        + [jax.ShapeDtypeStruct((t, D_MODEL), F32)] * 2,
        compiler_params=_params(("arbitrary",)),
    )(pj1, pj2, pj3, ret_raw, x2d, ws, bst, w_ro, w_so, w_o)


def _ple_call(x1, p2d, target, g_ple, g_final, w_pg, w_pp):
    t = x1.shape[0]

    def body(x1_ref, p_ref, t_ref, gp_ref, gf_ref, wpg_ref, wpp_ref,
             dx1_ref, hp_ref, dz_ref, dpp_ref, loss_ref, dgp_ref, dgf_ref):
        @pl.when(pl.program_id(0) == 0)
        def _():
            loss_ref[...] = jnp.zeros_like(loss_ref)
            dgp_ref[...] = jnp.zeros_like(dgp_ref)
            dgf_ref[...] = jnp.zeros_like(dgf_ref)

        x1v = x1_ref[...]
        xn1, r1 = _rms(x1v)
        hp = (xn1 * gp_ref[...]).astype(MXU_DTYPE)
        gate = _sigmoid(jnp.dot(hp, wpg_ref[...], preferred_element_type=F32))
        pp = jnp.dot(p_ref[...].astype(MXU_DTYPE), wpp_ref[...], preferred_element_type=F32)
        x2 = x1v + gate * pp
        xn2, r2 = _rms(x2)
        err = xn2 * gf_ref[...] - t_ref[...]
        loss_ref[...] += (0.5 / D_MODEL) * jnp.sum(jnp.sum(err * err, axis=1, keepdims=True), axis=0, keepdims=True)
        dy = err * (1.0 / D_MODEL)
        dgf_ref[...] += jnp.sum(dy * xn2, axis=0, keepdims=True)
        dx2 = _rms_bwd(dy * gf_ref[...], xn2, r2)
        dpp = (dx2 * gate).astype(MXU_DTYPE)
        dz = (dx2 * pp * gate * (1.0 - gate)).astype(MXU_DTYPE)
        dhp = _mm_nt(dz, wpg_ref[...])
        dgp_ref[...] += jnp.sum(dhp * xn1, axis=0, keepdims=True)
        dx1_ref[...] = dx2 + _rms_bwd(dhp * gp_ref[...], xn1, r1)
        hp_ref[...] = hp
        dz_ref[...] = dz
        dpp_ref[...] = dpp

    vec = _resident((1, D_MODEL))
    return pl.pallas_call(
        body, name="ple_fwd_bwd", grid=(t // TILE_M,),
        in_specs=[_row_tile(D_MODEL), _row_tile(PLE_DIM), _row_tile(D_MODEL), vec, vec,
                  _resident((D_MODEL, D_MODEL)), _resident((PLE_DIM, D_MODEL))],
        out_specs=[_row_tile(D_MODEL)] * 4 + [pl.BlockSpec((1, 1), lambda i: (0, 0)),
                                              pl.BlockSpec((1, D_MODEL), lambda i: (0, 0)),
                                              pl.BlockSpec((1, D_MODEL), lambda i: (0, 0))],
        out_shape=[jax.ShapeDtypeStruct((t, D_MODEL), F32)] + [jax.ShapeDtypeStruct((t, D_MODEL), MXU_DTYPE)] * 3
        + [jax.ShapeDtypeStruct((1, 1), F32), jax.ShapeDtypeStruct((1, D_MODEL), F32),
           jax.ShapeDtypeStruct((1, D_MODEL), F32)],
        compiler_params=_params(("arbitrary",)),
    )(x1, p2d, target, g_ple, g_final, w_pg, w_pp)


def _merge_bwd_call(dx1, pj1, pj2, pj3, ret_raw, a, b, ws, bst, w_ro, w_so, w_o):
    t = dx1.shape[0]

    def body(dx1_ref, pj1_ref, pj2_ref, pj3_ref, rr_ref, a_ref, b_ref, ws_ref, bst_ref, wro_ref, wso_ref, wo_ref,
             dpj1_ref, dpj2_ref, dpj3_ref, drr_ref, da_ref, db_ref, dws_ref, dbst_ref):
        @pl.when(pl.program_id(0) == 0)
        def _():
            dws_ref[...] = jnp.zeros_like(dws_ref)
            dbst_ref[...] = jnp.zeros_like(dbst_ref)

        wsc = _causal_ws(ws_ref)
        f = _branch_forward(pj1_ref[...], pj2_ref[...], rr_ref[...], wsc, bst_ref[...])
        pj3 = pj3_ref[...]
        smr, sms = _sigmoid(pj3[:, :D_MODEL]), _sigmoid(pj3[:, D_MODEL:])
        dmerged = _mm_nt(dx1_ref[...], wo_ref[...])
        da = (dmerged * smr).astype(MXU_DTYPE)
        db = (dmerged * sms).astype(MXU_DTYPE)
        dpj3_ref[:, :D_MODEL] = (dmerged * a_ref[...] * smr * (1.0 - smr)).astype(dpj3_ref.dtype)
        dpj3_ref[:, D_MODEL:] = (dmerged * b_ref[...] * sms * (1.0 - sms)).astype(dpj3_ref.dtype)
        da_ref[...] = da
        db_ref[...] = db

        dret = _mm_nt(da, wro_ref[...])
        dpj1_ref[:, :D_MODEL] = (dret * f["rn"] * f["dsil_rg"]).astype(dpj1_ref.dtype)
        drn = dret * f["sil_rg"]
        for h in range(RET_HEADS):
            cols = slice(h * V_DIM, (h + 1) * V_DIM)
            drr_ref[:, cols] = _unit_norm_bwd(drn[:, cols], f["rn"][:, cols], f["rstd_r"][h])

        dsgu = _mm_nt(db, wso_ref[...])
        dpj2_ref[:, D_MODEL:] = (dsgu * f["u"] * f["mixed"] * f["dsil_sg"]).astype(dpj2_ref.dtype)
        tg = dsgu * f["sil_sg"]
        dpj1_ref[:, D_MODEL:] = (tg * f["mixed"] * _gelu_grad(f["su"])).astype(dpj1_ref.dtype)
        dmixed = tg * f["u"]
        rows = []
        for cc in range(TILE_M // CHUNK):
            cols = []
            for g in range(SGU_GROUPS):
                rs, cs = slice(cc * CHUNK, (cc + 1) * CHUNK), slice(g * GROUP_DIM, (g + 1) * GROUP_DIM)
                dm = dmixed[rs, cs]
                cols.append(_mm_tn(wsc[g], dm))
                dws_ref[g] += _mm_nt(dm, f["vn"][rs, cs])
                dbst_ref[:, g:g + 1] += jnp.sum(dm, axis=1, keepdims=True)
            rows.append(jnp.concatenate(cols, axis=1))
        dvn = jnp.concatenate(rows, axis=0)
        dvv = _unit_norm_bwd(dvn, f["vn"], f["rstd_v"])
        dpj2_ref[:, :D_MODEL] = (dvv * _gelu_grad(f["sv"])).astype(dpj2_ref.dtype)

    sq = (D_MODEL, D_MODEL)
    return pl.pallas_call(
        body, name="merge_bwd", grid=(t // TILE_M,),
        in_specs=[_row_tile(D_MODEL)] + [_row_tile(SHARD_W)] * 3 + [_row_tile(D_MODEL)] * 3
        + [_resident(ws.shape), _resident(bst.shape), _resident(sq), _resident(sq), _resident(sq)],
        out_specs=[_row_tile(SHARD_W)] * 3 + [_row_tile(D_MODEL)] * 3
        + [pl.BlockSpec(ws.shape, lambda i: (0, 0, 0)), pl.BlockSpec(bst.shape, lambda i: (0, 0))],
        out_shape=[jax.ShapeDtypeStruct((t, SHARD_W), MXU_DTYPE)] * 3 + [jax.ShapeDtypeStruct((t, D_MODEL), F32)]
        + [jax.ShapeDtypeStruct((t, D_MODEL), MXU_DTYPE)] * 2
        + [jax.ShapeDtypeStruct(ws.shape, F32), jax.ShapeDtypeStruct(bst.shape, F32)],
        compiler_params=_params(("arbitrary",)),
    )(dx1, pj1, pj2, pj3, ret_raw, a, b, ws, bst, w_ro, w_so, w_o)


def _retention_bwd_call(pj0, drr, states, consts, n_seq, seq):
    cos_full, sin_signed, decay_in, zeta, xi = consts
    nb = seq // TILE_M
    cpb = TILE_M // CHUNK

    def body(pj_ref, do_ref, st_ref, cos_ref, sin_ref, d_ref, zeta_ref, xi_ref, dpj_ref, gstate):
        @pl.when(pl.program_id(1) == 0)
        def _():
            gstate[...] = jnp.zeros_like(gstate)

        for cc in reversed(range(cpb)):
            rows = slice(cc * CHUNK, (cc + 1) * CHUNK)
            cs, sn = cos_ref[rows, :], sin_ref[rows, :]
            for h in range(RET_HEADS):
                q, k, v = _qkv(pj_ref, rows, h)
                qt = (q * cs + _rot(q) * sn) * QK_SCALE
                kt = k * cs + _rot(k) * sn
                d_out = do_ref[rows, h * V_DIM:(h + 1) * V_DIM]
                prev = st_ref[cc, h]
                g = gstate[h]
                dec = d_ref[h]
                scores_d = _mm_nt(qt, kt) * dec
                dscores = _mm_nt(d_out, v) * dec
                kz = kt * zeta_ref[h]
                qx = qt * xi_ref[h]
                dv = _mm_tn(scores_d, d_out) + _mm(kz, g)
                dqt = (_mm(dscores, kt) + _mm_nt(d_out, prev) * xi_ref[h]) * QK_SCALE
                dkt = _mm_tn(dscores, qt) + _mm_nt(v, g) * zeta_ref[h]
                gstate[h] = _mm_tn(qx, d_out) + CHUNK_DECAY[h] * g
                dq = dqt * cs + _rot(dqt * sn)
                dk = dkt * cs + _rot(dkt * sn)
                dpj_ref[rows, h * QK_DIM:(h + 1) * QK_DIM] = dq.astype(dpj_ref.dtype)
                dpj_ref[rows, RET_HEADS * QK_DIM + h * QK_DIM:RET_HEADS * QK_DIM + (h + 1) * QK_DIM] = dk.astype(
                    dpj_ref.dtype)
                dpj_ref[rows, 2 * RET_HEADS * QK_DIM + h * V_DIM:2 * RET_HEADS * QK_DIM + (h + 1) * V_DIM] = dv.astype(
                    dpj_ref.dtype)

    row = lambda b, n: (b * nb + nb - 1 - n, 0)
    pos = lambda b, n: (nb - 1 - n, 0)
    return pl.pallas_call(
        body, name="retention_bwd", grid=(n_seq, nb),
        in_specs=[pl.BlockSpec((TILE_M, SHARD_W), row), pl.BlockSpec((TILE_M, D_MODEL), row),
                  pl.BlockSpec((cpb, RET_HEADS, QK_DIM, V_DIM), lambda b, n: (b * nb + nb - 1 - n, 0, 0, 0)),
                  pl.BlockSpec((TILE_M, QK_DIM), pos), pl.BlockSpec((TILE_M, QK_DIM), pos),
                  _resident(decay_in.shape), _resident(zeta.shape), _resident(xi.shape)],
        out_specs=pl.BlockSpec((TILE_M, SHARD_W), row),
        out_shape=jax.ShapeDtypeStruct((n_seq * seq, SHARD_W), MXU_DTYPE),
        scratch_shapes=[pltpu.VMEM((RET_HEADS, QK_DIM, V_DIM), F32)],
        compiler_params=_params(("arbitrary", "arbitrary")),
    )(pj0, drr, states, cos_full, sin_signed, decay_in, zeta, xi)


def _dx_call(dpj, x2d, dx1, g_mixer, wg_in):
    t = x2d.shape[0]

    def body(d0, d1, d2, d3, x_ref, dx1_ref, g_ref, w_ref, dx_ref, dg_ref):
        @pl.when(pl.program_id(0) == 0)
        def _():
            dg_ref[...] = jnp.zeros_like(dg_ref)

        dh = _mm_nt(d0[...], w_ref[0])
        for j, d_ref in enumerate((d1, d2, d3)):
            dh += _mm_nt(d_ref[...], w_ref[j + 1])
        xn, r = _rms(x_ref[...])
        dg_ref[...] += jnp.sum(dh * xn, axis=0, keepdims=True)
        dx_ref[...] = dx1_ref[...] + _rms_bwd(dh * g_ref[...], xn, r)

    return pl.pallas_call(
        body, name="dx_bwd", grid=(t // TILE_M,),
        in_specs=[_row_tile(SHARD_W)] * N_CHIPS + [_row_tile(D_MODEL)] * 2
        + [_resident((1, D_MODEL)), _resident(wg_in.shape)],
        out_specs=[_row_tile(D_MODEL), pl.BlockSpec((1, D_MODEL), lambda i: (0, 0))],
        out_shape=[jax.ShapeDtypeStruct((t, D_MODEL), F32), jax.ShapeDtypeStruct((1, D_MODEL), F32)],
        compiler_params=_params(("arbitrary",)),
    )(*dpj, x2d, dx1, g_mixer, wg_in)


def _wgrad_call(name, lhs, rhs, block_n, block_t=512):
    t, k = lhs.shape
    n = rhs.shape[1]
    steps = t // block_t

    def body(l_ref, r_ref, o_ref, acc):
        @pl.when(pl.program_id(1) == 0)
        def _():
            acc[...] = jnp.zeros_like(acc)

        acc[...] += _mm_tn(l_ref[...], r_ref[...])

        @pl.when(pl.program_id(1) == steps - 1)
        def _():
            o_ref[...] = acc[...].astype(o_ref.dtype)

    return pl.pallas_call(
        body, name=name, grid=(n // block_n, steps),
        in_specs=[pl.BlockSpec((block_t, k), lambda j, i: (i, 0)), pl.BlockSpec((block_t, block_n), lambda j, i: (i, j))],
        out_specs=pl.BlockSpec((None, k, block_n), lambda j, i: (j, 0, 0)),
        out_shape=jax.ShapeDtypeStruct((n // block_n, k, block_n), COMM_DTYPE),
        scratch_shapes=[pltpu.VMEM((k, block_n), F32)],
        compiler_params=_params(("arbitrary", "arbitrary")),
    )(lhs, rhs)


def _position():
    return lax.axis_index("x"), lax.axis_index("y"), lax.axis_index("c")


def _position_array():
    x, y, c = _position()
    return jnp.stack([2 * x + y, c]).astype(jnp.int32)


def _other_chip(x, y, k):
    return (1 - x if k & 2 else x), (1 - y if k & 1 else y)


def _exchange_call(name, operands, out_shapes, plan, n_local, n_remote, aliases=None):
    n_in = len(operands)

    def body(*refs):
        ins, outs = refs[:n_in], refs[n_in:n_in + len(out_shapes)]
        loc_sem, send_sem, recv_sem = refs[n_in + len(out_shapes):]
        local, remote = plan(ins, outs)
        assert len(local) == n_local and len(remote) == n_remote
        copies = [pltpu.make_async_copy(src, dst, loc_sem.at[i]) for i, (src, dst) in enumerate(local)]
        copies += [pltpu.make_async_remote_copy(src_ref=src, dst_ref=dst, send_sem=send_sem.at[i],
                                                recv_sem=recv_sem.at[i], device_id=dev, device_id_type=MESH)
                   for i, (src, dst, dev) in enumerate(remote)]
        for cp in copies:
            cp.start()
        for cp in copies:
            cp.wait()

    return pl.pallas_call(
        body, name=name, in_specs=[ANY] * n_in, out_specs=[ANY] * len(out_shapes), out_shape=out_shapes,
        scratch_shapes=[pltpu.SemaphoreType.DMA((max(n_local, 1),)), pltpu.SemaphoreType.DMA((n_remote,)),
                        pltpu.SemaphoreType.DMA((n_remote,))],
        input_output_aliases=aliases or {},
        compiler_params=pltpu.CompilerParams(has_side_effects=True),
    )(*operands)


def _place_cast_call(name, w, pos):
    rows, width = w.shape
    block_rows = min(256, rows)

    def body(pos_ref, w_ref, o_ref):
        o_ref[...] = w_ref[...].astype(o_ref.dtype)

    return pl.pallas_call(
        body, name=name,
        grid_spec=pltpu.PrefetchScalarGridSpec(
            num_scalar_prefetch=1, grid=(rows // block_rows,),
            in_specs=[pl.BlockSpec((block_rows, width), lambda i, pos: (i, 0))],
            out_specs=pl.BlockSpec((None, block_rows, width), lambda i, pos: (pos[0], i, 0))),
        out_shape=jax.ShapeDtypeStruct((N_CHIPS, rows, width), MXU_DTYPE),
        compiler_params=_params(("arbitrary",)),
    )(pos, w)


def _gather_weights(placed):
    n = len(placed)

    def body(*refs):
        bufs = refs[n:2 * n]
        send_sem, recv_sem = refs[2 * n:]
        x, y, c = _position()
        me = 2 * x + y
        sibling = (x, y, 1 - c)

        def copy(i, slot, piece, dev):
            return pltpu.make_async_remote_copy(src_ref=piece, dst_ref=piece, send_sem=send_sem.at[i, slot],
                                                recv_sem=recv_sem.at[i, slot], device_id=dev, device_id_type=MESH)

        def half(i, chip, core):
            hr = placed[i].shape[1] // 2
            return bufs[i].at[chip, pl.ds(core * hr, hr)]

        sent = []
        for i in range(n):
            for k in (1, 2, 3):
                px, py = _other_chip(x, y, k)
                sent.append(copy(i, k - 1, half(i, me, c), (px, py, c)))
                sent[-1].start()
        for i in range(n):
            for k in (1, 2, 3):
                px, py = _other_chip(x, y, k)
                landed = half(i, 2 * px + py, c)
                copy(i, k - 1, landed, sibling).wait_recv()
                sent.append(copy(i, 2 + k, landed, sibling))
                sent[-1].start()
        for i in range(n):
            for k in (1, 2, 3):
                px, py = _other_chip(x, y, k)
                copy(i, 2 + k, half(i, 2 * px + py, 1 - c), sibling).wait_recv()
        for cp in sent:
            cp.wait_send()

    return pl.pallas_call(
        body, name="gather_weights", in_specs=[ANY] * n, out_specs=[ANY] * n,
        out_shape=[jax.ShapeDtypeStruct(a.shape, a.dtype) for a in placed],
        scratch_shapes=[pltpu.SemaphoreType.DMA((n, 6)), pltpu.SemaphoreType.DMA((n, 6))],
        input_output_aliases={i: i for i in range(n)},
        compiler_params=pltpu.CompilerParams(has_side_effects=True),
    )(*placed)


def _sum_call(name, pos, terms, rows, width, out_dtype, block_rows, out_rows=None, out_index=None):
    out_rows = rows if out_rows is None else out_rows
    out_index = (lambda i, pos, br: i) if out_index is None else out_index

    def body(pos_ref, *refs):
        acc = refs[0][...].astype(F32)
        for r in refs[1:-1]:
            acc = acc + r[...].astype(F32)
        refs[-1][...] = acc.astype(out_dtype)

    def spec(index):
        return pl.BlockSpec((block_rows, width), lambda i, pos: (index(i, pos, block_rows), 0))

    return pl.pallas_call(
        body, name=name,
        grid_spec=pltpu.PrefetchScalarGridSpec(
            num_scalar_prefetch=1, grid=(rows // block_rows,),
            in_specs=[spec(index) for _, index in terms], out_specs=spec(out_index)),
        out_shape=jax.ShapeDtypeStruct((out_rows, width), out_dtype),
        compiler_params=_params(("arbitrary",)),
    )(pos, *[arr for arr, _ in terms])


def _reduce_scatter(grads, pos):
    n = len(grads)
    shapes = [(g.shape[0] // N_CHIPS, g.shape[1]) for g in grads]
    halves = [(r // 2, w) for r, w in shapes]

    def pair_plan(ins, outs):
        x, y, c = _position()
        remote = []
        for i in range(n):
            (r, _), (hr, _) = shapes[i], halves[i]
            for j in range(N_CHIPS):
                remote.append((ins[i].at[pl.ds(j * r + (1 - c) * hr, hr)], outs[i].at[pl.ds(j * hr, hr)],
                               (x, y, 1 - c)))
        return [], remote

    pair = _exchange_call("grad_pair_exchange", grads,
                          [jax.ShapeDtypeStruct((N_CHIPS * hr, w), COMM_DTYPE) for hr, w in halves],
                          pair_plan, 0, N_CHIPS * n)

    def own_half(i):
        r, hr = shapes[i][0], halves[i][0]

        def index(s, pos, br):
            per = hr // br
            return (s // per) * (r // br) + pos[1] * per + s % per
        return index

    pair_sums = [_sum_call(f"grad_pair_sum_{i}", pos, [(grads[i], own_half(i)), (pair[i], lambda s, pos, br: s)],
                           N_CHIPS * halves[i][0], halves[i][1], COMM_DTYPE, min(256, halves[i][0]))
                 for i in range(n)]

    def chip_plan(ins, outs):
        x, y, c = _position()
        remote = []
        for i in range(n):
            hr = halves[i][0]
            for k in (1, 2, 3):
                px, py = _other_chip(x, y, k)
                remote.append((ins[i].at[pl.ds((2 * px + py) * hr, hr)], outs[i].at[pl.ds((k - 1) * hr, hr)],
                               (px, py, c)))
        return [], remote

    chip = _exchange_call("grad_chip_exchange", pair_sums,
                          [jax.ShapeDtypeStruct((3 * hr, w), COMM_DTYPE) for hr, w in halves], chip_plan, 0, 3 * n)

    def slab(k, hr):
        return lambda s, pos, br: k * (hr // br) + s

    reduced = []
    for i in range(n):
        hr, w = halves[i]
        terms = [(pair_sums[i], lambda s, pos, br, hr=hr: pos[0] * (hr // br) + s)]
        terms += [(chip[i], slab(k, hr)) for k in range(3)]
        reduced.append(_sum_call(f"grad_chip_sum_{i}", pos, terms, hr, w, F32, min(256, hr), out_rows=2 * hr,
                                 out_index=lambda s, pos, br, hr=hr: pos[1] * (hr // br) + s))

    def half_plan(ins, outs):
        x, y, c = _position()
        remote = []
        for i in range(n):
            hr = halves[i][0]
            mine = outs[i].at[pl.ds(c * hr, hr)]
            remote.append((mine, mine, (x, y, 1 - c)))
        return [], remote

    return _exchange_call("grad_half_exchange", reduced, [jax.ShapeDtypeStruct(s, F32) for s in shapes],
                          half_plan, 0, n, aliases={i: i for i in range(n)})


def _gather_small(pack):
    rows = pack.shape[0]

    def plan(ins, outs):
        x, y, c = _position()
        mine = outs[0].at[pl.ds((4 * x + 2 * y + c) * rows, rows)]
        remote = []
        for d in range(1, 8):
            px, py = _other_chip(x, y, d >> 1)
            remote.append((ins[0], mine, (px, py, 1 - c if d & 1 else c)))
        return [(ins[0], mine)], remote

    return _exchange_call("small_grad_gather", [pack], [jax.ShapeDtypeStruct((8 * rows, 128), F32)], plan, 1, 7)[0]


def _adamw(w, g, m, v):
    m = ADAM_B1 * m + (1.0 - ADAM_B1) * g
    v = ADAM_B2 * v + (1.0 - ADAM_B2) * (g * g)
    m_hat = m / (1.0 - ADAM_B1 ** ADAM_STEP)
    v_hat = v / (1.0 - ADAM_B2 ** ADAM_STEP)
    delta = -ADAM_LR * (m_hat / (jnp.sqrt(v_hat) + ADAM_EPS) + ADAM_WD * w)
    return delta, m, v


def _adamw_call(name, w, g, m, v):
    rows, width = w.shape
    block_rows = min(256, rows)

    def body(w_ref, g_ref, m_ref, v_ref, d_out, m_out, v_out):
        d_out[...], m_out[...], v_out[...] = _adamw(w_ref[...], g_ref[...], m_ref[...], v_ref[...])

    spec = pl.BlockSpec((block_rows, width), lambda i: (i, 0))
    return pl.pallas_call(
        body, name=name, grid=(rows // block_rows,), in_specs=[spec] * 4, out_specs=[spec] * 3,
        out_shape=[jax.ShapeDtypeStruct(w.shape, F32)] * 3,
        compiler_params=_params(("arbitrary",)),
    )(w, g, m, v)


SMALL_WS_ROWS = SGU_GROUPS * CHUNK


def _small_adamw_call(gathered, w, m, v):
    rows = w.shape[0]

    def body(all_ref, w_ref, m_ref, v_ref, g_out, d_out, m_out, v_out):
        g = all_ref[0:rows, :]
        for d in range(1, 8):
            g = g + all_ref[d * rows:(d + 1) * rows, :]
        row = lax.broadcasted_iota(jnp.int32, (rows, CHUNK), 0)
        col = lax.broadcasted_iota(jnp.int32, (rows, CHUNK), 1)
        keep = jnp.logical_or(row >= SMALL_WS_ROWS, (row % CHUNK) >= col)
        g = jnp.where(keep, g, 0.0)
        g_out[...] = g
        d_out[...], m_out[...], v_out[...] = _adamw(w_ref[...], g, m_ref[...], v_ref[...])

    return pl.pallas_call(
        body, name="small_adamw", out_shape=[jax.ShapeDtypeStruct(w.shape, F32)] * 4,
        compiler_params=pltpu.CompilerParams(vmem_limit_bytes=VMEM_LIMIT),
    )(gathered, w, m, v)


def _pack_small(ws, bs, g_mixer, g_ple, g_final):
    parts = [ws.reshape(SMALL_WS_ROWS, CHUNK), bs.reshape(SGU_GROUPS, CHUNK), g_mixer.reshape(8, 128),
             g_ple.reshape(8, 128), g_final.reshape(8, 128), jnp.zeros((4, 128), F32)]
    return jnp.concatenate(parts, axis=0)


def _unpack_small(pack):
    o = SMALL_WS_ROWS
    return (pack[:o].reshape(1, SGU_GROUPS, CHUNK, CHUNK), pack[o:o + 4].reshape(1, SGU_GROUPS, CHUNK),
            pack[o + 4:o + 12].reshape(1, D_MODEL), pack[o + 12:o + 20].reshape(1, D_MODEL),
            pack[o + 20:o + 28].reshape(D_MODEL))


def kernel(x, p, w_in, w_ret_out, w_sgu_out, w_out, sgu_ws, sgu_bs, w_ple_gate, w_ple_proj, g_mixer, g_ple, g_final, loss_target, m_w_in, m_w_ret_out, m_w_sgu_out, m_w_out, m_sgu_ws, m_sgu_bs, m_w_ple_gate, m_w_ple_proj, m_g_mixer, m_g_ple, m_g_final, v_w_in, v_w_ret_out, v_w_sgu_out, v_w_out, v_sgu_ws, v_sgu_bs, v_w_ple_gate, v_w_ple_proj, v_g_mixer, v_g_ple, v_g_final):
    n_seq, seq, _ = x.shape
    t = n_seq * seq
    x2d = x.reshape(t, D_MODEL)
    p2d = p.reshape(t, PLE_DIM)
    target = loss_target.reshape(t, D_MODEL)
    big = [w_in[0], w_ret_out[0], w_sgu_out[0], w_out[0], w_ple_gate[0], w_ple_proj[0]]
    big_m = [m_w_in[0], m_w_ret_out[0], m_w_sgu_out[0], m_w_out[0], m_w_ple_gate[0], m_w_ple_proj[0]]
    big_v = [v_w_in[0], v_w_ret_out[0], v_w_sgu_out[0], v_w_out[0], v_w_ple_gate[0], v_w_ple_proj[0]]

    pos = _position_array()
    wg_in, wg_ro, wg_so, wg_o, wg_pg, wg_pp = _gather_weights(
        [_place_cast_call(f"place_weight_{i}", w, pos) for i, w in enumerate(big)])
    w_ro, w_so, w_o, w_pg = (w.reshape(D_MODEL, D_MODEL) for w in (wg_ro, wg_so, wg_o, wg_pg))
    w_pp = wg_pp.transpose(1, 0, 2).reshape(PLE_DIM, D_MODEL)
    ws = sgu_ws[0]
    bst = sgu_bs[0].T
    consts = _retention_consts(seq)

    h, pj0, pj1, pj2, pj3 = _proj_call(x2d, g_mixer, wg_in)
    ret_raw, states = _retention_fwd_call(pj0, consts, n_seq, seq)
    x1, ret, sgu, merged, a, b = _merge_fwd_call(pj1, pj2, pj3, ret_raw, x2d, ws, bst, w_ro, w_so, w_o)
    dx1, hp, dz, dpp, loss, dg_ple, dg_final = _ple_call(x1, p2d, target, g_ple, g_final.reshape(1, D_MODEL),
                                                         w_pg, w_pp)
    dpj1, dpj2, dpj3, drr, da, db, dws, dbst = _merge_bwd_call(dx1, pj1, pj2, pj3, ret_raw, a, b, ws, bst,
                                                               w_ro, w_so, w_o)
    dpj0 = _retention_bwd_call(pj0, drr, states, consts, n_seq, seq)
    dx, dg_mixer = _dx_call((dpj0, dpj1, dpj2, dpj3), x2d, dx1, g_mixer, wg_in)

    rows_of = lambda g: g.reshape(g.shape[0] * g.shape[1], g.shape[2])
    grads = [
        jnp.concatenate([rows_of(_wgrad_call(f"wgrad_in_{j}", h, d, SHARD_W)) for j, d in
                         enumerate((dpj0, dpj1, dpj2, dpj3))], axis=0),
        rows_of(_wgrad_call("wgrad_ret_out", ret, da, D_MODEL)),
        rows_of(_wgrad_call("wgrad_sgu_out", sgu, db, D_MODEL)),
        rows_of(_wgrad_call("wgrad_out", merged, dx1, D_MODEL)),
        rows_of(_wgrad_call("wgrad_ple_gate", hp, dz, D_MODEL)),
        rows_of(_wgrad_call("wgrad_ple_proj", p2d, dpp, PLE_DIM)),
    ]
    g_big = _reduce_scatter(grads, pos)
    upd = [_adamw_call(f"adamw_{i}", big[i], g_big[i], big_m[i], big_v[i]) for i in range(len(big))]

    small_w = _pack_small(sgu_ws, sgu_bs, g_mixer, g_ple, g_final)
    small_m = _pack_small(m_sgu_ws, m_sgu_bs, m_g_mixer, m_g_ple, m_g_final)
    small_v = _pack_small(v_sgu_ws, v_sgu_bs, v_g_mixer, v_g_ple, v_g_final)
    small_g = _gather_small(_pack_small(dws, dbst.T, dg_mixer, dg_ple, dg_final))
    small = [_unpack_small(a_) for a_ in _small_adamw_call(small_g, small_w, small_m, small_v)]

    total = lax.psum(loss[0, 0], ("x", "y", "c"))

    def ordered(big_list, small_tuple):
        w_in_, w_ro_, w_so_, w_o_, w_pg_, w_pp_ = [b_[None] for b_ in big_list]
        s_ws, s_bs, s_gm, s_gp, s_gf = small_tuple
        return [w_in_, w_ro_, w_so_, w_o_, s_ws, s_bs, w_pg_, w_pp_, s_gm, s_gp, s_gf]

    out = [total, dx.reshape(x.shape)]
    out += ordered(g_big, small[0])
    out += ordered([u[0] for u in upd], small[1])
    out += ordered([u[1] for u in upd], small[2])
    out += ordered([u[2] for u in upd], small[3])
    return tuple(out)
```

```python
import functools
import math

import numpy as np
import jax
import jax.numpy as jnp
from jax import lax
from jax.experimental import pallas as pl
from jax.experimental.pallas import tpu as pltpu

F32 = jnp.float32
MXU_DTYPE = jnp.bfloat16
COMM_DTYPE = jnp.bfloat16

D_MODEL = 1024
RET_HEADS = 4
QK_DIM = 128
V_DIM = 256
CHUNK = 128
SGU_GROUPS = 4
GROUP_DIM = 256
PLE_DIM = 256
N_CHIPS = 4
SHARD_W = 2048
ROPE_BASE = 10000.0
NORM_EPS = 1e-6
GN_EPS = 1e-5
QK_SCALE = QK_DIM ** -0.5
SQRT_HALF = math.sqrt(0.5)
INV_SQRT_2PI = 1.0 / math.sqrt(2.0 * math.pi)

ADAM_LR = 0.001
ADAM_B1 = 0.9
ADAM_B2 = 0.999
ADAM_EPS = 1e-08
ADAM_WD = 0.01
ADAM_STEP = 10

TILE_M = 256
VMEM_LIMIT = 56 * 1024 * 1024
MESH = pl.DeviceIdType.MESH
ANY = pl.BlockSpec(memory_space=pl.ANY)

CHUNK_DECAY = tuple(
    float(np.exp(np.float32(CHUNK) * np.log(np.float32(1.0 - 2.0 ** (-5.0 - h))))) for h in range(RET_HEADS))


def _mm(a, b):
    return jnp.dot(a.astype(MXU_DTYPE), b.astype(MXU_DTYPE), preferred_element_type=F32)


def _mm_nt(a, b):
    return lax.dot_general(a.astype(MXU_DTYPE), b.astype(MXU_DTYPE), (((1,), (1,)), ((), ())),
                           preferred_element_type=F32)


def _mm_tn(a, b):
    return lax.dot_general(a.astype(MXU_DTYPE), b.astype(MXU_DTYPE), (((0,), (0,)), ((), ())),
                           preferred_element_type=F32)


def _mean(x):
    return jnp.mean(x, axis=-1, keepdims=True)


def _sigmoid(x):
    return jax.nn.sigmoid(x)


def _silu_and_grad(x):
    s = _sigmoid(x)
    return x * s, s * (1.0 + x * (1.0 - s))


def _gelu(x):
    return 0.5 * x * (1.0 + lax.erf(x * SQRT_HALF))


def _gelu_grad(x):
    return 0.5 * (1.0 + lax.erf(x * SQRT_HALF)) + x * jnp.exp(-0.5 * x * x) * INV_SQRT_2PI


def _unit_norm(x, eps):
    xc = x - _mean(x)
    rstd = lax.rsqrt(_mean(xc * xc) + eps)
    return xc * rstd, rstd


def _unit_norm_bwd(dn, n, rstd):
    return rstd * (dn - _mean(dn) - n * _mean(dn * n))


def _rms(x):
    r = lax.rsqrt(_mean(x * x) + NORM_EPS)
    return x * r, r


def _rms_bwd(dxn, xn, r):
    return r * (dxn - xn * _mean(dxn * xn))


def _rot(x):
    return pltpu.roll(x, QK_DIM // 2, 1)


def _params(semantics, **kw):
    return pltpu.CompilerParams(dimension_semantics=semantics, vmem_limit_bytes=VMEM_LIMIT, **kw)


def _row_tile(width, tm=TILE_M):
    return pl.BlockSpec((tm, width), lambda i: (i, 0))


def _resident(shape):
    nd = len(shape)
    return pl.BlockSpec(shape, lambda *_: (0,) * nd, pipeline_mode=pl.Buffered(1))


def _causal_ws(ws_ref):
    row = lax.broadcasted_iota(jnp.int32, (CHUNK, CHUNK), 0)
    col = lax.broadcasted_iota(jnp.int32, (CHUNK, CHUNK), 1)
    return [jnp.where(row >= col, ws_ref[g], 0.0).astype(MXU_DTYPE) for g in range(SGU_GROUPS)]


def _heads(x, width):
    return [x[:, h * width:(h + 1) * width] for h in range(x.shape[1] // width)]


def _branch_forward(pj1, pj2, ret_raw, wsc, bst):
    rg, su = pj1[:, :D_MODEL], pj1[:, D_MODEL:]
    sv, sg = pj2[:, :D_MODEL], pj2[:, D_MODEL:]
    rn_parts, rstd_parts = zip(*[_unit_norm(r, GN_EPS) for r in _heads(ret_raw, V_DIM)])
    rn = jnp.concatenate(rn_parts, axis=1)
    sil_rg, dsil_rg = _silu_and_grad(rg)
    ret = rn * sil_rg
    u = _gelu(su)
    vn, rstd_v = _unit_norm(_gelu(sv), GN_EPS)
    rows = []
    for cc in range(pj1.shape[0] // CHUNK):
        cols = []
        for g in range(SGU_GROUPS):
            blk = vn[cc * CHUNK:(cc + 1) * CHUNK, g * GROUP_DIM:(g + 1) * GROUP_DIM]
            cols.append(_mm(wsc[g], blk) + bst[:, g:g + 1])
        rows.append(jnp.concatenate(cols, axis=1))
    mixed = jnp.concatenate(rows, axis=0)
    sil_sg, dsil_sg = _silu_and_grad(sg)
    sgu = u * mixed * sil_sg
    return dict(rg=rg, su=su, sv=sv, sg=sg, rn=rn, rstd_r=rstd_parts, sil_rg=sil_rg, dsil_rg=dsil_rg, ret=ret,
                u=u, vn=vn, rstd_v=rstd_v, mixed=mixed, sil_sg=sil_sg, dsil_sg=dsil_sg, sgu=sgu)


class _Side:
    def __init__(self, operands, out_shapes, n_sems, start, finish, aliases=None):
        self.operands, self.out_shapes, self.n_sems = list(operands), list(out_shapes), n_sems
        self.start, self.finish, self.aliases = start, finish, dict(aliases or {})


def _compute_call(body, *, name, grid, in_specs, out_specs, out_shape, operands, semantics, scratch_shapes=(),
                  side=None):
    if side is None:
        return pl.pallas_call(body, name=name, grid=grid, in_specs=in_specs, out_specs=out_specs, out_shape=out_shape,
                              scratch_shapes=list(scratch_shapes), compiler_params=_params(semantics))(*operands)
    n_in, n_out, s_in, s_out = len(operands), len(out_shape), len(side.operands), len(side.out_shapes)

    def carrier(*refs):
        ins, refs = refs[:n_in], refs[n_in:]
        side_ins, refs = refs[:s_in], refs[s_in:]
        outs, refs = refs[:n_out], refs[n_out:]
        side_outs, refs = refs[:s_out], refs[s_out:]
        scratch, (send_sem, recv_sem) = refs[:-2], refs[-2:]
        ids = [pl.program_id(a) for a in range(len(grid))]
        first = functools.reduce(jnp.logical_and, [i == 0 for i in ids])
        last = functools.reduce(jnp.logical_and, [i == g - 1 for i, g in zip(ids, grid)])

        @pl.when(first)
        def _():
            side.start(side_ins, side_outs, send_sem, recv_sem)

        body(*ins, *outs, *scratch)

        @pl.when(last)
        def _():
            side.finish(side_ins, side_outs, send_sem, recv_sem)

    res = pl.pallas_call(
        carrier, name=name, grid=grid, in_specs=list(in_specs) + [ANY] * s_in,
        out_specs=list(out_specs) + [ANY] * s_out, out_shape=list(out_shape) + side.out_shapes,
        scratch_shapes=list(scratch_shapes) + [pltpu.SemaphoreType.DMA((side.n_sems,))] * 2,
        input_output_aliases={n_in + a: n_out + b for a, b in side.aliases.items()},
        compiler_params=_params(semantics, has_side_effects=True),
    )(*operands, *side.operands)
    return res[:n_out], res[n_out:]


def _exchange_call(name, side):
    s_in = len(side.operands)

    def body(*refs):
        ins, outs = refs[:s_in], refs[s_in:s_in + len(side.out_shapes)]
        send_sem, recv_sem = refs[s_in + len(side.out_shapes):]
        side.start(ins, outs, send_sem, recv_sem)
        side.finish(ins, outs, send_sem, recv_sem)

    return pl.pallas_call(
        body, name=name, in_specs=[ANY] * s_in, out_specs=[ANY] * len(side.out_shapes), out_shape=side.out_shapes,
        scratch_shapes=[pltpu.SemaphoreType.DMA((side.n_sems,))] * 2, input_output_aliases=side.aliases,
        compiler_params=pltpu.CompilerParams(has_side_effects=True),
    )(*side.operands)


def _proj_call(x2d, g_mixer, wg_in, side):
    t = x2d.shape[0]

    def body(x_ref, g_ref, w_ref, h_ref, p0, p1, p2, p3):
        xn, _ = _rms(x_ref[...])
        h = (xn * g_ref[...]).astype(MXU_DTYPE)
        h_ref[...] = h
        for j, p_ref in enumerate((p0, p1, p2, p3)):
            p_ref[...] = jnp.dot(h, w_ref[j], preferred_element_type=F32)

    return _compute_call(
        body, name="proj_fwd", grid=(t // TILE_M,),
        in_specs=[_row_tile(D_MODEL), _resident((1, D_MODEL)), _resident(wg_in.shape)],
        out_specs=[_row_tile(D_MODEL)] + [_row_tile(SHARD_W)] * N_CHIPS,
        out_shape=[jax.ShapeDtypeStruct((t, D_MODEL), MXU_DTYPE)]
        + [jax.ShapeDtypeStruct((t, SHARD_W), F32)] * N_CHIPS,
        operands=(x2d, g_mixer, wg_in), semantics=("arbitrary",), side=side)


def _retention_consts(seq):
    half = QK_DIM // 2
    inv = ROPE_BASE ** (-jnp.arange(half, dtype=F32) / half)
    ang = jnp.arange(seq, dtype=F32)[:, None] * inv[None, :]
    cos, sin = jnp.cos(ang), jnp.sin(ang)
    cos_full = jnp.concatenate([cos, cos], axis=1)
    sin_signed = jnp.concatenate([-sin, sin], axis=1)
    log_g = jnp.log(1.0 - 2.0 ** (-5.0 - jnp.arange(RET_HEADS, dtype=F32)))
    idx = jnp.arange(CHUNK, dtype=F32)
    diff = idx[:, None] - idx[None, :]
    decay_in = jnp.where(diff[None] >= 0, jnp.exp(jnp.maximum(diff, 0.0)[None] * log_g[:, None, None]), 0.0)
    zeta = jnp.exp((CHUNK - 1.0 - idx)[None, :] * log_g[:, None])
    xi = jnp.exp((idx + 1.0)[None, :] * log_g[:, None])
    zeta = jnp.broadcast_to(zeta[:, :, None], (RET_HEADS, CHUNK, QK_DIM))
    xi = jnp.broadcast_to(xi[:, :, None], (RET_HEADS, CHUNK, QK_DIM))
    return cos_full, sin_signed, decay_in, zeta, xi


def _qkv(pj_ref, rows, h):
    q = pj_ref[rows, h * QK_DIM:(h + 1) * QK_DIM]
    k = pj_ref[rows, RET_HEADS * QK_DIM + h * QK_DIM:RET_HEADS * QK_DIM + (h + 1) * QK_DIM]
    v = pj_ref[rows, 2 * RET_HEADS * QK_DIM + h * V_DIM:2 * RET_HEADS * QK_DIM + (h + 1) * V_DIM]
    return q, k, v


def _retention_fwd_call(pj0, consts, n_seq, seq):
    cos_full, sin_signed, decay_in, zeta, xi = consts
    nb = seq // TILE_M
    cpb = TILE_M // CHUNK

    def body(pj_ref, cos_ref, sin_ref, d_ref, zeta_ref, xi_ref, o_ref, st_ref, state):
        @pl.when(pl.program_id(1) == 0)
        def _():
            state[...] = jnp.zeros_like(state)

        for cc in range(cpb):
            rows = slice(cc * CHUNK, (cc + 1) * CHUNK)
            cs, sn = cos_ref[rows, :], sin_ref[rows, :]
            for h in range(RET_HEADS):
                q, k, v = _qkv(pj_ref, rows, h)
                qt = (q * cs + _rot(q) * sn) * QK_SCALE
                kt = k * cs + _rot(k) * sn
                prev = state[h]
                st_ref[cc, h] = prev.astype(st_ref.dtype)
                scores = _mm_nt(qt, kt) * d_ref[h]
                o_ref[rows, h * V_DIM:(h + 1) * V_DIM] = _mm(scores, v) + _mm(qt * xi_ref[h], prev)
                state[h] = _mm_tn(kt * zeta_ref[h], v) + CHUNK_DECAY[h] * prev

    row = lambda b, n: (b * nb + n, 0)
    pos = lambda b, n: (n, 0)
    return pl.pallas_call(
        body, name="retention_fwd", grid=(n_seq, nb),
        in_specs=[pl.BlockSpec((TILE_M, SHARD_W), row), pl.BlockSpec((TILE_M, QK_DIM), pos),
                  pl.BlockSpec((TILE_M, QK_DIM), pos), _resident(decay_in.shape), _resident(zeta.shape),
                  _resident(xi.shape)],
        out_specs=[pl.BlockSpec((TILE_M, D_MODEL), row),
                   pl.BlockSpec((cpb, RET_HEADS, QK_DIM, V_DIM), lambda b, n: (b * nb + n, 0, 0, 0))],
        out_shape=[jax.ShapeDtypeStruct((n_seq * seq, D_MODEL), F32),
                   jax.ShapeDtypeStruct((n_seq * seq // CHUNK, RET_HEADS, QK_DIM, V_DIM), MXU_DTYPE)],
        scratch_shapes=[pltpu.VMEM((RET_HEADS, QK_DIM, V_DIM), F32)],
        compiler_params=_params(("arbitrary", "arbitrary")),
    )(pj0, cos_full, sin_signed, decay_in, zeta, xi)


def _merge_fwd_call(pj1, pj2, pj3, ret_raw, x2d, ws, bst, w_ro, w_so, w_o):
    t = x2d.shape[0]

    def body(pj1_ref, pj2_ref, pj3_ref, rr_ref, x_ref, ws_ref, bst_ref, wro_ref, wso_ref, wo_ref,
             x1_ref, ret_ref, sgu_ref, mg_ref, a_ref, b_ref):
        f = _branch_forward(pj1_ref[...], pj2_ref[...], rr_ref[...], _causal_ws(ws_ref), bst_ref[...])
        ret = f["ret"].astype(MXU_DTYPE)
        sgu = f["sgu"].astype(MXU_DTYPE)
        a = jnp.dot(ret, wro_ref[...], preferred_element_type=F32)
        b = jnp.dot(sgu, wso_ref[...], preferred_element_type=F32)
        pj3 = pj3_ref[...]
        merged = (_sigmoid(pj3[:, :D_MODEL]) * a + _sigmoid(pj3[:, D_MODEL:]) * b).astype(MXU_DTYPE)
        x1_ref[...] = x_ref[...] + jnp.dot(merged, wo_ref[...], preferred_element_type=F32)
        ret_ref[...] = ret
        sgu_ref[...] = sgu
        mg_ref[...] = merged
        a_ref[...] = a
        b_ref[...] = b

    sq = (D_MODEL, D_MODEL)
    return pl.pallas_call(
        body, name="merge_fwd", grid=(t // TILE_M,),
        in_specs=[_row_tile(SHARD_W)] * 3 + [_row_tile(D_MODEL)] * 2
        + [_resident(ws.shape), _resident(bst.shape), _resident(sq), _resident(sq), _resident(sq)],
        out_specs=[_row_tile(D_MODEL)] * 6,
        out_shape=[jax.ShapeDtypeStruct((t, D_MODEL), F32)] + [jax.ShapeDtypeStruct((t, D_MODEL), MXU_DTYPE)] * 3
        + [jax.ShapeDtypeStruct((t, D_MODEL), F32)] * 2,
        compiler_params=_params(("arbitrary",)),
    )(pj1, pj2, pj3, ret_raw, x2d, ws, bst, w_ro, w_so, w_o)


def _ple_call(x1, p2d, target, g_ple, g_final, w_pg, w_pp):
    t = x1.shape[0]

    def body(x1_ref, p_ref, t_ref, gp_ref, gf_ref, wpg_ref, wpp_ref,
             dx1_ref, hp_ref, dz_ref, dpp_ref, loss_ref, dgp_ref, dgf_ref):
        @pl.when(pl.program_id(0) == 0)
        def _():
            loss_ref[...] = jnp.zeros_like(loss_ref)
            dgp_ref[...] = jnp.zeros_like(dgp_ref)
            dgf_ref[...] = jnp.zeros_like(dgf_ref)

        x1v = x1_ref[...]
        xn1, r1 = _rms(x1v)
        hp = (xn1 * gp_ref[...]).astype(MXU_DTYPE)
        gate = _sigmoid(jnp.dot(hp, wpg_ref[...], preferred_element_type=F32))
        pp = jnp.dot(p_ref[...].astype(MXU_DTYPE), wpp_ref[...], preferred_element_type=F32)
        x2 = x1v + gate * pp
        xn2, r2 = _rms(x2)
        err = xn2 * gf_ref[...] - t_ref[...]
        loss_ref[...] += (0.5 / D_MODEL) * jnp.sum(jnp.sum(err * err, axis=1, keepdims=True), axis=0, keepdims=True)
        dy = err * (1.0 / D_MODEL)
        dgf_ref[...] += jnp.sum(dy * xn2, axis=0, keepdims=True)
        dx2 = _rms_bwd(dy * gf_ref[...], xn2, r2)
        dpp = (dx2 * gate).astype(MXU_DTYPE)
        dz = (dx2 * pp * gate * (1.0 - gate)).astype(MXU_DTYPE)
        dhp = _mm_nt(dz, wpg_ref[...])
        dgp_ref[...] += jnp.sum(dhp * xn1, axis=0, keepdims=True)
        dx1_ref[...] = dx2 + _rms_bwd(dhp * gp_ref[...], xn1, r1)
        hp_ref[...] = hp
        dz_ref[...] = dz
        dpp_ref[...] = dpp

    vec = _resident((1, D_MODEL))
    return pl.pallas_call(
        body, name="ple_fwd_bwd", grid=(t // TILE_M,),
        in_specs=[_row_tile(D_MODEL), _row_tile(PLE_DIM), _row_tile(D_MODEL), vec, vec,
                  _resident((D_MODEL, D_MODEL)), _resident((PLE_DIM, D_MODEL))],
        out_specs=[_row_tile(D_MODEL)] * 4 + [pl.BlockSpec((1, 1), lambda i: (0, 0)),
                                              pl.BlockSpec((1, D_MODEL), lambda i: (0, 0)),
                                              pl.BlockSpec((1, D_MODEL), lambda i: (0, 0))],
        out_shape=[jax.ShapeDtypeStruct((t, D_MODEL), F32)] + [jax.ShapeDtypeStruct((t, D_MODEL), MXU_DTYPE)] * 3
        + [jax.ShapeDtypeStruct((1, 1), F32), jax.ShapeDtypeStruct((1, D_MODEL), F32),
           jax.ShapeDtypeStruct((1, D_MODEL), F32)],
        compiler_params=_params(("arbitrary",)),
    )(x1, p2d, target, g_ple, g_final, w_pg, w_pp)


def _merge_bwd_call(dx1, pj1, pj2, pj3, ret_raw, a, b, ws, bst, w_ro, w_so, w_o):
    t = dx1.shape[0]

    def body(dx1_ref, pj1_ref, pj2_ref, pj3_ref, rr_ref, a_ref, b_ref, ws_ref, bst_ref, wro_ref, wso_ref, wo_ref,
             dpj1_ref, dpj2_ref, dpj3_ref, drr_ref, da_ref, db_ref, dws_ref, dbst_ref):
        @pl.when(pl.program_id(0) == 0)
        def _():
            dws_ref[...] = jnp.zeros_like(dws_ref)
            dbst_ref[...] = jnp.zeros_like(dbst_ref)

        wsc = _causal_ws(ws_ref)
        f = _branch_forward(pj1_ref[...], pj2_ref[...], rr_ref[...], wsc, bst_ref[...])
        pj3 = pj3_ref[...]
        smr, sms = _sigmoid(pj3[:, :D_MODEL]), _sigmoid(pj3[:, D_MODEL:])
        dmerged = _mm_nt(dx1_ref[...], wo_ref[...])
        da = (dmerged * smr).astype(MXU_DTYPE)
        db = (dmerged * sms).astype(MXU_DTYPE)
        dpj3_ref[:, :D_MODEL] = (dmerged * a_ref[...] * smr * (1.0 - smr)).astype(dpj3_ref.dtype)
        dpj3_ref[:, D_MODEL:] = (dmerged * b_ref[...] * sms * (1.0 - sms)).astype(dpj3_ref.dtype)
        da_ref[...] = da
        db_ref[...] = db

        dret = _mm_nt(da, wro_ref[...])
        dpj1_ref[:, :D_MODEL] = (dret * f["rn"] * f["dsil_rg"]).astype(dpj1_ref.dtype)
        drn = dret * f["sil_rg"]
        for h in range(RET_HEADS):
            cols = slice(h * V_DIM, (h + 1) * V_DIM)
            drr_ref[:, cols] = _unit_norm_bwd(drn[:, cols], f["rn"][:, cols], f["rstd_r"][h])

        dsgu = _mm_nt(db, wso_ref[...])
        dpj2_ref[:, D_MODEL:] = (dsgu * f["u"] * f["mixed"] * f["dsil_sg"]).astype(dpj2_ref.dtype)
        tg = dsgu * f["sil_sg"]
        dpj1_ref[:, D_MODEL:] = (tg * f["mixed"] * _gelu_grad(f["su"])).astype(dpj1_ref.dtype)
        dmixed = tg * f["u"]
        rows = []
        for cc in range(TILE_M // CHUNK):
            cols = []
            for g in range(SGU_GROUPS):
                rs, cs = slice(cc * CHUNK, (cc + 1) * CHUNK), slice(g * GROUP_DIM, (g + 1) * GROUP_DIM)
                dm = dmixed[rs, cs]
                cols.append(_mm_tn(wsc[g], dm))
                dws_ref[g] += _mm_nt(dm, f["vn"][rs, cs])
                dbst_ref[:, g:g + 1] += jnp.sum(dm, axis=1, keepdims=True)
            rows.append(jnp.concatenate(cols, axis=1))
        dvn = jnp.concatenate(rows, axis=0)
        dvv = _unit_norm_bwd(dvn, f["vn"], f["rstd_v"])
        dpj2_ref[:, :D_MODEL] = (dvv * _gelu_grad(f["sv"])).astype(dpj2_ref.dtype)

    sq = (D_MODEL, D_MODEL)
    return pl.pallas_call(
        body, name="merge_bwd", grid=(t // TILE_M,),
        in_specs=[_row_tile(D_MODEL)] + [_row_tile(SHARD_W)] * 3 + [_row_tile(D_MODEL)] * 3
        + [_resident(ws.shape), _resident(bst.shape), _resident(sq), _resident(sq), _resident(sq)],
        out_specs=[_row_tile(SHARD_W)] * 3 + [_row_tile(D_MODEL)] * 3
        + [pl.BlockSpec(ws.shape, lambda i: (0, 0, 0)), pl.BlockSpec(bst.shape, lambda i: (0, 0))],
        out_shape=[jax.ShapeDtypeStruct((t, SHARD_W), MXU_DTYPE)] * 3 + [jax.ShapeDtypeStruct((t, D_MODEL), F32)]
        + [jax.ShapeDtypeStruct((t, D_MODEL), MXU_DTYPE)] * 2
        + [jax.ShapeDtypeStruct(ws.shape, F32), jax.ShapeDtypeStruct(bst.shape, F32)],
        compiler_params=_params(("arbitrary",)),
    )(dx1, pj1, pj2, pj3, ret_raw, a, b, ws, bst, w_ro, w_so, w_o)


def _retention_bwd_call(pj0, drr, states, consts, n_seq, seq, side):
    cos_full, sin_signed, decay_in, zeta, xi = consts
    nb = seq // TILE_M
    cpb = TILE_M // CHUNK

    def body(pj_ref, do_ref, st_ref, cos_ref, sin_ref, d_ref, zeta_ref, xi_ref, dpj_ref, gstate):
        @pl.when(pl.program_id(1) == 0)
        def _():
            gstate[...] = jnp.zeros_like(gstate)

        for cc in reversed(range(cpb)):
            rows = slice(cc * CHUNK, (cc + 1) * CHUNK)
            cs, sn = cos_ref[rows, :], sin_ref[rows, :]
            for h in range(RET_HEADS):
                q, k, v = _qkv(pj_ref, rows, h)
                qt = (q * cs + _rot(q) * sn) * QK_SCALE
                kt = k * cs + _rot(k) * sn
                d_out = do_ref[rows, h * V_DIM:(h + 1) * V_DIM]
                prev = st_ref[cc, h]
                g = gstate[h]
                dec = d_ref[h]
                scores_d = _mm_nt(qt, kt) * dec
                dscores = _mm_nt(d_out, v) * dec
                kz = kt * zeta_ref[h]
                qx = qt * xi_ref[h]
                dv = _mm_tn(scores_d, d_out) + _mm(kz, g)
                dqt = (_mm(dscores, kt) + _mm_nt(d_out, prev) * xi_ref[h]) * QK_SCALE
                dkt = _mm_tn(dscores, qt) + _mm_nt(v, g) * zeta_ref[h]
                gstate[h] = _mm_tn(qx, d_out) + CHUNK_DECAY[h] * g
                dq = dqt * cs + _rot(dqt * sn)
                dk = dkt * cs + _rot(dkt * sn)
                dpj_ref[rows, h * QK_DIM:(h + 1) * QK_DIM] = dq.astype(dpj_ref.dtype)
                dpj_ref[rows, RET_HEADS * QK_DIM + h * QK_DIM:RET_HEADS * QK_DIM + (h + 1) * QK_DIM] = dk.astype(
                    dpj_ref.dtype)
                dpj_ref[rows, 2 * RET_HEADS * QK_DIM + h * V_DIM:2 * RET_HEADS * QK_DIM + (h + 1) * V_DIM] = dv.astype(
                    dpj_ref.dtype)

    row = lambda b, n: (b * nb + nb - 1 - n, 0)
    pos = lambda b, n: (nb - 1 - n, 0)
    return _compute_call(
        body, name="retention_bwd", grid=(n_seq, nb),
        in_specs=[pl.BlockSpec((TILE_M, SHARD_W), row), pl.BlockSpec((TILE_M, D_MODEL), row),
                  pl.BlockSpec((cpb, RET_HEADS, QK_DIM, V_DIM), lambda b, n: (b * nb + nb - 1 - n, 0, 0, 0)),
                  pl.BlockSpec((TILE_M, QK_DIM), pos), pl.BlockSpec((TILE_M, QK_DIM), pos),
                  _resident(decay_in.shape), _resident(zeta.shape), _resident(xi.shape)],
        out_specs=[pl.BlockSpec((TILE_M, SHARD_W), row)],
        out_shape=[jax.ShapeDtypeStruct((n_seq * seq, SHARD_W), MXU_DTYPE)],
        scratch_shapes=[pltpu.VMEM((RET_HEADS, QK_DIM, V_DIM), F32)],
        operands=(pj0, drr, states, cos_full, sin_signed, decay_in, zeta, xi),
        semantics=("arbitrary", "arbitrary"), side=side)


def _dx_call(dpj, x2d, dx1, g_mixer, wg_in, side):
    t = x2d.shape[0]

    def body(d0, d1, d2, d3, x_ref, dx1_ref, g_ref, w_ref, dx_ref, dg_ref):
        @pl.when(pl.program_id(0) == 0)
        def _():
            dg_ref[...] = jnp.zeros_like(dg_ref)

        dh = _mm_nt(d0[...], w_ref[0])
        for j, d_ref in enumerate((d1, d2, d3)):
            dh += _mm_nt(d_ref[...], w_ref[j + 1])
        xn, r = _rms(x_ref[...])
        dg_ref[...] += jnp.sum(dh * xn, axis=0, keepdims=True)
        dx_ref[...] = dx1_ref[...] + _rms_bwd(dh * g_ref[...], xn, r)

    return _compute_call(
        body, name="dx_bwd", grid=(t // TILE_M,),
        in_specs=[_row_tile(SHARD_W)] * N_CHIPS + [_row_tile(D_MODEL)] * 2
        + [_resident((1, D_MODEL)), _resident(wg_in.shape)],
        out_specs=[_row_tile(D_MODEL), pl.BlockSpec((1, D_MODEL), lambda i: (0, 0))],
        out_shape=[jax.ShapeDtypeStruct((t, D_MODEL), F32), jax.ShapeDtypeStruct((1, D_MODEL), F32)],
        operands=(*dpj, x2d, dx1, g_mixer, wg_in), semantics=("arbitrary",), side=side)


def _wgrad_call(name, lhs, rhs, block_n, out_cols=None, block_t=512):
    t, k = lhs.shape
    n = rhs.shape[1]
    steps = t // block_t
    out_cols = block_n if out_cols is None else out_cols
    per = block_n // out_cols

    def body(l_ref, r_ref, o_ref, acc):
        @pl.when(pl.program_id(1) == 0)
        def _():
            acc[...] = jnp.zeros_like(acc)

        acc[...] += _mm_tn(l_ref[...], r_ref[...])

        @pl.when(pl.program_id(1) == steps - 1)
        def _():
            for s in range(per):
                o_ref[s] = acc[:, s * out_cols:(s + 1) * out_cols].astype(o_ref.dtype)

    return pl.pallas_call(
        body, name=name, grid=(n // block_n, steps),
        in_specs=[pl.BlockSpec((block_t, k), lambda j, i: (i, 0)), pl.BlockSpec((block_t, block_n), lambda j, i: (i, j))],
        out_specs=pl.BlockSpec((per, k, out_cols), lambda j, i: (j, 0, 0)),
        out_shape=jax.ShapeDtypeStruct((n // out_cols, k, out_cols), COMM_DTYPE),
        scratch_shapes=[pltpu.VMEM((k, block_n), F32)],
        compiler_params=_params(("arbitrary", "arbitrary")),
    )(lhs, rhs)


def _position():
    return lax.axis_index("x"), lax.axis_index("y"), lax.axis_index("c")


def _position_array():
    x, y, c = _position()
    return jnp.stack([2 * x + y, c]).astype(jnp.int32)


def _other_chip(x, y, k):
    return (1 - x if k & 2 else x), (1 - y if k & 1 else y)


def _plan_side(operands, out_shapes, plan, n_remote, aliases=None):
    def copies(ins, outs, send_sem, recv_sem):
        remote = plan(ins, outs)
        assert len(remote) == n_remote
        return [pltpu.make_async_remote_copy(src_ref=src, dst_ref=dst, send_sem=send_sem.at[i],
                                             recv_sem=recv_sem.at[i], device_id=dev, device_id_type=MESH)
                for i, (src, dst, dev) in enumerate(remote)]

    def start(*a):
        for cp in copies(*a):
            cp.start()

    def finish(*a):
        for cp in copies(*a):
            cp.wait()

    return _Side(operands, out_shapes, n_remote, start, finish, aliases)


def _place_cast_call(name, w, pos):
    rows, width = w.shape
    block_rows = min(256, rows)

    def body(pos_ref, w_ref, o_ref):
        o_ref[...] = w_ref[...].astype(o_ref.dtype)

    return pl.pallas_call(
        body, name=name,
        grid_spec=pltpu.PrefetchScalarGridSpec(
            num_scalar_prefetch=1, grid=(rows // block_rows,),
            in_specs=[pl.BlockSpec((block_rows, width), lambda i, pos: (i, 0))],
            out_specs=pl.BlockSpec((None, block_rows, width), lambda i, pos: (pos[0], i, 0))),
        out_shape=jax.ShapeDtypeStruct((N_CHIPS, rows, width), MXU_DTYPE),
        compiler_params=_params(("arbitrary",)),
    )(pos, w)


def _gather_side(placed):
    n = len(placed)

    def pieces(bufs, send_sem, recv_sem):
        x, y, c = _position()
        me = 2 * x + y
        sibling = (x, y, 1 - c)

        def copy(i, slot, piece, dev):
            return pltpu.make_async_remote_copy(src_ref=piece, dst_ref=piece, send_sem=send_sem.at[6 * i + slot],
                                                recv_sem=recv_sem.at[6 * i + slot], device_id=dev,
                                                device_id_type=MESH)

        def half(i, chip, core):
            hr = placed[i].shape[1] // 2
            return bufs[i].at[chip, pl.ds(core * hr, hr)]

        first, passed, arriving = [], [], []
        for i in range(n):
            for k in (1, 2, 3):
                px, py = _other_chip(x, y, k)
                first.append(copy(i, k - 1, half(i, me, c), (px, py, c)))
                passed.append((copy(i, k - 1, half(i, 2 * px + py, c), sibling),
                               copy(i, 2 + k, half(i, 2 * px + py, c), sibling)))
                arriving.append(copy(i, 2 + k, half(i, 2 * px + py, 1 - c), sibling))
        return first, passed, arriving

    def start(ins, outs, send_sem, recv_sem):
        for cp in pieces(outs, send_sem, recv_sem)[0]:
            cp.start()

    def finish(ins, outs, send_sem, recv_sem):
        first, passed, arriving = pieces(outs, send_sem, recv_sem)
        for landed, onward in passed:
            landed.wait_recv()
            onward.start()
        for cp in arriving:
            cp.wait_recv()
        for cp in first + [onward for _, onward in passed]:
            cp.wait_send()

    return _Side(placed, [jax.ShapeDtypeStruct(a.shape, a.dtype) for a in placed], 6 * n, start, finish,
                 {i: i for i in range(n)})


def _sum_call(name, pos, terms, rows, width, out_dtype, block_rows, out_rows=None, out_index=None):
    out_rows = rows if out_rows is None else out_rows
    out_index = (lambda i, pos, br: i) if out_index is None else out_index

    def body(pos_ref, *refs):
        acc = refs[0][...].astype(F32)
        for r in refs[1:-1]:
            acc = acc + r[...].astype(F32)
        refs[-1][...] = acc.astype(out_dtype)

    def spec(index):
        return pl.BlockSpec((block_rows, width), lambda i, pos: (index(i, pos, block_rows), 0))

    return pl.pallas_call(
        body, name=name,
        grid_spec=pltpu.PrefetchScalarGridSpec(
            num_scalar_prefetch=1, grid=(rows // block_rows,),
            in_specs=[spec(index) for _, index in terms], out_specs=spec(out_index)),
        out_shape=jax.ShapeDtypeStruct((out_rows, width), out_dtype),
        compiler_params=_params(("arbitrary",)),
    )(pos, *[arr for arr, _ in terms])


def _pair_stage(tag, grads, pos):
    n = len(grads)
    shapes = [(g.shape[0] // N_CHIPS, g.shape[1]) for g in grads]

    def plan(ins, outs):
        x, y, c = _position()
        remote = []
        for i, (r, _) in enumerate(shapes):
            hr = r // 2
            for j in range(N_CHIPS):
                remote.append((ins[i].at[pl.ds(j * r + (1 - c) * hr, hr)], outs[i].at[pl.ds(j * hr, hr)],
                               (x, y, 1 - c)))
        return remote

    pair = _exchange_call(f"grad_pair_exchange_{tag}", _plan_side(
        grads, [jax.ShapeDtypeStruct((N_CHIPS * (r // 2), w), COMM_DTYPE) for r, w in shapes], plan, N_CHIPS * n))

    def own_half(r):
        def index(s, pos, br):
            per = (r // 2) // br
            return (s // per) * (r // br) + pos[1] * per + s % per
        return index

    return [_sum_call(f"grad_pair_sum_{tag}{i}", pos, [(grads[i], own_half(r)), (pair[i], lambda s, pos, br: s)],
                      N_CHIPS * (r // 2), w, COMM_DTYPE, min(256, r // 2)) for i, (r, w) in enumerate(shapes)]


def _chip_side(pair_sums):
    halves = [(p.shape[0] // N_CHIPS, p.shape[1]) for p in pair_sums]

    def plan(ins, outs):
        x, y, c = _position()
        remote = []
        for i, (hr, _) in enumerate(halves):
            for k in (1, 2, 3):
                px, py = _other_chip(x, y, k)
                remote.append((ins[i].at[pl.ds((2 * px + py) * hr, hr)], outs[i].at[pl.ds((k - 1) * hr, hr)],
                               (px, py, c)))
        return remote

    return _plan_side(pair_sums, [jax.ShapeDtypeStruct((3 * hr, w), COMM_DTYPE) for hr, w in halves], plan,
                      3 * len(pair_sums))


def _finish_stage(tag, pair_sums, chip, pos):
    n = len(pair_sums)
    halves = [(p.shape[0] // N_CHIPS, p.shape[1]) for p in pair_sums]

    def slab(k, hr):
        return lambda s, pos, br: k * (hr // br) + s

    reduced = []
    for i, (hr, w) in enumerate(halves):
        terms = [(pair_sums[i], lambda s, pos, br, hr=hr: pos[0] * (hr // br) + s)]
        terms += [(chip[i], slab(k, hr)) for k in range(3)]
        reduced.append(_sum_call(f"grad_chip_sum_{tag}{i}", pos, terms, hr, w, F32, min(256, hr), out_rows=2 * hr,
                                 out_index=lambda s, pos, br, hr=hr: pos[1] * (hr // br) + s))

    def plan(ins, outs):
        x, y, c = _position()
        remote = []
        for i, (hr, _) in enumerate(halves):
            mine = outs[i].at[pl.ds(c * hr, hr)]
            remote.append((mine, mine, (x, y, 1 - c)))
        return remote

    return _exchange_call(f"grad_half_exchange_{tag}", _plan_side(
        reduced, [jax.ShapeDtypeStruct((2 * hr, w), F32) for hr, w in halves], plan, n, {i: i for i in range(n)}))


def _gather_small(pack):
    rows = pack.shape[0]

    def body(in_ref, out_ref, loc_sem, send_sem, recv_sem):
        x, y, c = _position()
        mine = out_ref.at[pl.ds((4 * x + 2 * y + c) * rows, rows)]
        copies = [pltpu.make_async_copy(in_ref, mine, loc_sem)]
        for d in range(1, 8):
            px, py = _other_chip(x, y, d >> 1)
            copies.append(pltpu.make_async_remote_copy(
                src_ref=in_ref, dst_ref=mine, send_sem=send_sem.at[d - 1], recv_sem=recv_sem.at[d - 1],
                device_id=(px, py, 1 - c if d & 1 else c), device_id_type=MESH))
        for cp in copies:
            cp.start()
        for cp in copies:
            cp.wait()

    return pl.pallas_call(
        body, name="small_grad_gather", in_specs=[ANY], out_specs=ANY,
        out_shape=jax.ShapeDtypeStruct((8 * rows, 128), F32),
        scratch_shapes=[pltpu.SemaphoreType.DMA, pltpu.SemaphoreType.DMA((7,)), pltpu.SemaphoreType.DMA((7,))],
        compiler_params=pltpu.CompilerParams(has_side_effects=True),
    )(pack)


def _adamw(w, g, m, v):
    m = ADAM_B1 * m + (1.0 - ADAM_B1) * g
    v = ADAM_B2 * v + (1.0 - ADAM_B2) * (g * g)
    m_hat = m / (1.0 - ADAM_B1 ** ADAM_STEP)
    v_hat = v / (1.0 - ADAM_B2 ** ADAM_STEP)
    delta = -ADAM_LR * (m_hat / (jnp.sqrt(v_hat) + ADAM_EPS) + ADAM_WD * w)
    return delta, m, v


def _adamw_call(name, w, g, m, v):
    rows, width = w.shape
    block_rows = min(256, rows)

    def body(w_ref, g_ref, m_ref, v_ref, d_out, m_out, v_out):
        d_out[...], m_out[...], v_out[...] = _adamw(w_ref[...], g_ref[...], m_ref[...], v_ref[...])

    spec = pl.BlockSpec((block_rows, width), lambda i: (i, 0))
    return pl.pallas_call(
        body, name=name, grid=(rows // block_rows,), in_specs=[spec] * 4, out_specs=[spec] * 3,
        out_shape=[jax.ShapeDtypeStruct(w.shape, F32)] * 3,
        compiler_params=_params(("arbitrary",)),
    )(w, g, m, v)


SMALL_WS_ROWS = SGU_GROUPS * CHUNK


def _small_adamw_call(gathered, w, m, v):
    rows = w.shape[0]

    def body(all_ref, w_ref, m_ref, v_ref, g_out, d_out, m_out, v_out):
        g = all_ref[0:rows, :]
        for d in range(1, 8):
            g = g + all_ref[d * rows:(d + 1) * rows, :]
        row = lax.broadcasted_iota(jnp.int32, (rows, CHUNK), 0)
        col = lax.broadcasted_iota(jnp.int32, (rows, CHUNK), 1)
        keep = jnp.logical_or(row >= SMALL_WS_ROWS, (row % CHUNK) >= col)
        g = jnp.where(keep, g, 0.0)
        g_out[...] = g
        d_out[...], m_out[...], v_out[...] = _adamw(w_ref[...], g, m_ref[...], v_ref[...])

    return pl.pallas_call(
        body, name="small_adamw", out_shape=[jax.ShapeDtypeStruct(w.shape, F32)] * 4,
        compiler_params=pltpu.CompilerParams(vmem_limit_bytes=VMEM_LIMIT),
    )(gathered, w, m, v)


def _pack_small(ws, bs, g_mixer, g_ple, g_final):
    parts = [ws.reshape(SMALL_WS_ROWS, CHUNK), bs.reshape(SGU_GROUPS, CHUNK), g_mixer.reshape(8, 128),
             g_ple.reshape(8, 128), g_final.reshape(8, 128), jnp.zeros((4, 128), F32)]
    return jnp.concatenate(parts, axis=0)


def _unpack_small(pack):
    o = SMALL_WS_ROWS
    return (pack[:o].reshape(1, SGU_GROUPS, CHUNK, CHUNK), pack[o:o + 4].reshape(1, SGU_GROUPS, CHUNK),
            pack[o + 4:o + 12].reshape(1, D_MODEL), pack[o + 12:o + 20].reshape(1, D_MODEL),
            pack[o + 20:o + 28].reshape(D_MODEL))


def kernel(x, p, w_in, w_ret_out, w_sgu_out, w_out, sgu_ws, sgu_bs, w_ple_gate, w_ple_proj, g_mixer, g_ple, g_final, loss_target, m_w_in, m_w_ret_out, m_w_sgu_out, m_w_out, m_sgu_ws, m_sgu_bs, m_w_ple_gate, m_w_ple_proj, m_g_mixer, m_g_ple, m_g_final, v_w_in, v_w_ret_out, v_w_sgu_out, v_w_out, v_sgu_ws, v_sgu_bs, v_w_ple_gate, v_w_ple_proj, v_g_mixer, v_g_ple, v_g_final):
    n_seq, seq, _ = x.shape
    t = n_seq * seq
    x2d = x.reshape(t, D_MODEL)
    p2d = p.reshape(t, PLE_DIM)
    target = loss_target.reshape(t, D_MODEL)
    big = [w_in[0], w_ret_out[0], w_sgu_out[0], w_out[0], w_ple_gate[0], w_ple_proj[0]]
    big_m = [m_w_in[0], m_w_ret_out[0], m_w_sgu_out[0], m_w_out[0], m_w_ple_gate[0], m_w_ple_proj[0]]
    big_v = [v_w_in[0], v_w_ret_out[0], v_w_sgu_out[0], v_w_out[0], v_w_ple_gate[0], v_w_ple_proj[0]]

    pos = _position_array()
    placed = [_place_cast_call(f"place_weight_{i}", w, pos) for i, w in enumerate(big)]
    wg_in, = _exchange_call("gather_w_in", _gather_side(placed[:1]))
    ws = sgu_ws[0]
    bst = sgu_bs[0].T
    consts = _retention_consts(seq)

    (h, pj0, pj1, pj2, pj3), gathered = _proj_call(x2d, g_mixer, wg_in, _gather_side(placed[1:]))
    w_ro, w_so, w_o, w_pg = (w.reshape(D_MODEL, D_MODEL) for w in gathered[:4])
    w_pp = gathered[4].transpose(1, 0, 2).reshape(PLE_DIM, D_MODEL)
    ret_raw, states = _retention_fwd_call(pj0, consts, n_seq, seq)
    x1, ret, sgu, merged, a, b = _merge_fwd_call(pj1, pj2, pj3, ret_raw, x2d, ws, bst, w_ro, w_so, w_o)
    dx1, hp, dz, dpp, loss, dg_ple, dg_final = _ple_call(x1, p2d, target, g_ple, g_final.reshape(1, D_MODEL),
                                                         w_pg, w_pp)
    dpj1, dpj2, dpj3, drr, da, db, dws, dbst = _merge_bwd_call(dx1, pj1, pj2, pj3, ret_raw, a, b, ws, bst,
                                                               w_ro, w_so, w_o)

    rows_of = lambda g: g.reshape(g.shape[0] * g.shape[1], g.shape[2])
    tail_grads = [
        rows_of(_wgrad_call("wgrad_ret_out", ret, da, D_MODEL)),
        rows_of(_wgrad_call("wgrad_sgu_out", sgu, db, D_MODEL)),
        rows_of(_wgrad_call("wgrad_out", merged, dx1, D_MODEL)),
        rows_of(_wgrad_call("wgrad_ple_gate", hp, dz, D_MODEL)),
        rows_of(_wgrad_call("wgrad_ple_proj", p2d, dpp, D_MODEL, out_cols=PLE_DIM)),
    ]
    tail_sums = _pair_stage("tail", tail_grads, pos)
    (dpj0,), tail_chip = _retention_bwd_call(pj0, drr, states, consts, n_seq, seq, _chip_side(tail_sums))
    in_grad = jnp.concatenate([rows_of(_wgrad_call(f"wgrad_in_{j}", h, d, SHARD_W)) for j, d in
                               enumerate((dpj0, dpj1, dpj2, dpj3))], axis=0)
    in_sums = _pair_stage("in", [in_grad], pos)
    (dx, dg_mixer), in_chip = _dx_call((dpj0, dpj1, dpj2, dpj3), x2d, dx1, g_mixer, wg_in, _chip_side(in_sums))
    g_big = _finish_stage("in", in_sums, in_chip, pos) + _finish_stage("tail", tail_sums, tail_chip, pos)
    upd = [_adamw_call(f"adamw_{i}", big[i], g_big[i], big_m[i], big_v[i]) for i in range(len(big))]

    small_w = _pack_small(sgu_ws, sgu_bs, g_mixer, g_ple, g_final)
    small_m = _pack_small(m_sgu_ws, m_sgu_bs, m_g_mixer, m_g_ple, m_g_final)
    small_v = _pack_small(v_sgu_ws, v_sgu_bs, v_g_mixer, v_g_ple, v_g_final)
    small_g = _gather_small(_pack_small(dws, dbst.T, dg_mixer, dg_ple, dg_final))
    small = [_unpack_small(a_) for a_ in _small_adamw_call(small_g, small_w, small_m, small_v)]

    total = lax.psum(loss[0, 0], ("x", "y", "c"))

    def ordered(big_list, small_tuple):
        w_in_, w_ro_, w_so_, w_o_, w_pg_, w_pp_ = [b_[None] for b_ in big_list]
        s_ws, s_bs, s_gm, s_gp, s_gf = small_tuple
        return [w_in_, w_ro_, w_so_, w_o_, s_ws, s_bs, w_pg_, w_pp_, s_gm, s_gp, s_gf]

    out = [total, dx.reshape(x.shape)]
    out += ordered(g_big, small[0])
    out += ordered([u[0] for u in upd], small[1])
    out += ordered([u[1] for u in upd], small[2])
    out += ordered([u[2] for u in upd], small[3])
    return tuple(out)
```

```python
import functools
import math

import numpy as np
import jax
import jax.numpy as jnp
from jax import lax
from jax.experimental import pallas as pl
from jax.experimental.pallas import tpu as pltpu

F32 = jnp.float32
MXU_DTYPE = jnp.bfloat16
COMM_DTYPE = jnp.bfloat16

D_MODEL = 1024
RET_HEADS = 4
QK_DIM = 128
V_DIM = 256
CHUNK = 128
SGU_GROUPS = 4
GROUP_DIM = 256
PLE_DIM = 256
N_CHIPS = 4
SHARD_W = 2048
ROPE_BASE = 10000.0
NORM_EPS = 1e-6
GN_EPS = 1e-5
QK_SCALE = QK_DIM ** -0.5
SQRT_HALF = math.sqrt(0.5)
INV_SQRT_2PI = 1.0 / math.sqrt(2.0 * math.pi)

ADAM_LR = 0.001
ADAM_B1 = 0.9
ADAM_B2 = 0.999
ADAM_EPS = 1e-08
ADAM_WD = 0.01
ADAM_STEP = 10

TILE_M = 256
LOSS_TILE = (8, 128)
VMEM_LIMIT = 56 * 1024 * 1024
MESH = pl.DeviceIdType.MESH
ANY = pl.BlockSpec(memory_space=pl.ANY)

CHUNK_DECAY = tuple(
    float(np.exp(np.float32(CHUNK) * np.log(np.float32(1.0 - 2.0 ** (-5.0 - h))))) for h in range(RET_HEADS))


def _mm(a, b):
    return jnp.dot(a.astype(MXU_DTYPE), b.astype(MXU_DTYPE), preferred_element_type=F32)


def _mm_nt(a, b):
    return lax.dot_general(a.astype(MXU_DTYPE), b.astype(MXU_DTYPE), (((1,), (1,)), ((), ())),
                           preferred_element_type=F32)


def _mm_tn(a, b):
    return lax.dot_general(a.astype(MXU_DTYPE), b.astype(MXU_DTYPE), (((0,), (0,)), ((), ())),
                           preferred_element_type=F32)


def _mean(x):
    return jnp.mean(x, axis=-1, keepdims=True)


def _sigmoid(x):
    return jax.nn.sigmoid(x)


def _silu_and_grad(x):
    s = _sigmoid(x)
    return x * s, s * (1.0 + x * (1.0 - s))


def _gelu(x):
    return 0.5 * x * (1.0 + lax.erf(x * SQRT_HALF))


def _gelu_grad(x):
    return 0.5 * (1.0 + lax.erf(x * SQRT_HALF)) + x * jnp.exp(-0.5 * x * x) * INV_SQRT_2PI


def _unit_norm(x, eps):
    xc = x - _mean(x)
    rstd = lax.rsqrt(_mean(xc * xc) + eps)
    return xc * rstd, rstd


def _unit_norm_bwd(dn, n, rstd):
    return rstd * (dn - _mean(dn) - n * _mean(dn * n))


def _rms(x):
    r = lax.rsqrt(_mean(x * x) + NORM_EPS)
    return x * r, r


def _rms_bwd(dxn, xn, r):
    return r * (dxn - xn * _mean(dxn * xn))


def _rot(x):
    return pltpu.roll(x, QK_DIM // 2, 1)


def _params(semantics, **kw):
    return pltpu.CompilerParams(dimension_semantics=semantics, vmem_limit_bytes=VMEM_LIMIT, **kw)


def _row_tile(width, tm=TILE_M):
    return pl.BlockSpec((tm, width), lambda i: (i, 0))


def _resident(shape):
    nd = len(shape)
    return pl.BlockSpec(shape, lambda *_: (0,) * nd, pipeline_mode=pl.Buffered(1))


def _causal_ws(ws_ref):
    row = lax.broadcasted_iota(jnp.int32, (CHUNK, CHUNK), 0)
    col = lax.broadcasted_iota(jnp.int32, (CHUNK, CHUNK), 1)
    return [jnp.where(row >= col, ws_ref[g], 0.0).astype(MXU_DTYPE) for g in range(SGU_GROUPS)]


def _heads(x, width):
    return [x[:, h * width:(h + 1) * width] for h in range(x.shape[1] // width)]


def _branch_forward(pj1, pj2, ret_raw, wsc, bst):
    rg, su = pj1[:, :D_MODEL], pj1[:, D_MODEL:]
    sv, sg = pj2[:, :D_MODEL], pj2[:, D_MODEL:]
    rn_parts, rstd_parts = zip(*[_unit_norm(r, GN_EPS) for r in _heads(ret_raw, V_DIM)])
    rn = jnp.concatenate(rn_parts, axis=1)
    sil_rg, dsil_rg = _silu_and_grad(rg)
    ret = rn * sil_rg
    u = _gelu(su)
    vn, rstd_v = _unit_norm(_gelu(sv), GN_EPS)
    rows = []
    for cc in range(pj1.shape[0] // CHUNK):
        cols = []
        for g in range(SGU_GROUPS):
            blk = vn[cc * CHUNK:(cc + 1) * CHUNK, g * GROUP_DIM:(g + 1) * GROUP_DIM]
            cols.append(_mm(wsc[g], blk) + bst[:, g:g + 1])
        rows.append(jnp.concatenate(cols, axis=1))
    mixed = jnp.concatenate(rows, axis=0)
    sil_sg, dsil_sg = _silu_and_grad(sg)
    sgu = u * mixed * sil_sg
    return dict(rg=rg, su=su, sv=sv, sg=sg, rn=rn, rstd_r=rstd_parts, sil_rg=sil_rg, dsil_rg=dsil_rg, ret=ret,
                u=u, vn=vn, rstd_v=rstd_v, mixed=mixed, sil_sg=sil_sg, dsil_sg=dsil_sg, sgu=sgu)


class _Side:
    def __init__(self, operands, out_shapes, n_sems, start, finish, aliases=None):
        self.operands, self.out_shapes, self.n_sems = list(operands), list(out_shapes), n_sems
        self.start, self.finish, self.aliases = start, finish, dict(aliases or {})


def _compute_call(body, *, name, grid, in_specs, out_specs, out_shape, operands, semantics, scratch_shapes=(),
                  side=None):
    if side is None:
        return pl.pallas_call(body, name=name, grid=grid, in_specs=in_specs, out_specs=out_specs, out_shape=out_shape,
                              scratch_shapes=list(scratch_shapes), compiler_params=_params(semantics))(*operands)
    n_in, n_out, s_in, s_out = len(operands), len(out_shape), len(side.operands), len(side.out_shapes)

    def carrier(*refs):
        ins, refs = refs[:n_in], refs[n_in:]
        side_ins, refs = refs[:s_in], refs[s_in:]
        outs, refs = refs[:n_out], refs[n_out:]
        side_outs, refs = refs[:s_out], refs[s_out:]
        scratch, (send_sem, recv_sem) = refs[:-2], refs[-2:]
        ids = [pl.program_id(a) for a in range(len(grid))]
        first = functools.reduce(jnp.logical_and, [i == 0 for i in ids])
        last = functools.reduce(jnp.logical_and, [i == g - 1 for i, g in zip(ids, grid)])

        @pl.when(first)
        def _():
            side.start(side_ins, side_outs, send_sem, recv_sem)

        body(*ins, *outs, *scratch)

        @pl.when(last)
        def _():
            side.finish(side_ins, side_outs, send_sem, recv_sem)

    res = pl.pallas_call(
        carrier, name=name, grid=grid, in_specs=list(in_specs) + [ANY] * s_in,
        out_specs=list(out_specs) + [ANY] * s_out, out_shape=list(out_shape) + side.out_shapes,
        scratch_shapes=list(scratch_shapes) + [pltpu.SemaphoreType.DMA((side.n_sems,))] * 2,
        input_output_aliases={n_in + a: n_out + b for a, b in side.aliases.items()},
        compiler_params=_params(semantics, has_side_effects=True),
    )(*operands, *side.operands)
    return res[:n_out], res[n_out:]


def _exchange_call(name, side):
    s_in = len(side.operands)

    def body(*refs):
        ins, outs = refs[:s_in], refs[s_in:s_in + len(side.out_shapes)]
        send_sem, recv_sem = refs[s_in + len(side.out_shapes):]
        side.start(ins, outs, send_sem, recv_sem)
        side.finish(ins, outs, send_sem, recv_sem)

    return pl.pallas_call(
        body, name=name, in_specs=[ANY] * s_in, out_specs=[ANY] * len(side.out_shapes), out_shape=side.out_shapes,
        scratch_shapes=[pltpu.SemaphoreType.DMA((side.n_sems,))] * 2, input_output_aliases=side.aliases,
        compiler_params=pltpu.CompilerParams(has_side_effects=True),
    )(*side.operands)


def _proj_call(x2d, g_mixer, wg_in, side):
    t = x2d.shape[0]

    def body(x_ref, g_ref, w_ref, h_ref, p0, p1, p2, p3):
        xn, _ = _rms(x_ref[...])
        h = (xn * g_ref[...]).astype(MXU_DTYPE)
        h_ref[...] = h
        for j, p_ref in enumerate((p0, p1, p2, p3)):
            p_ref[...] = jnp.dot(h, w_ref[j], preferred_element_type=F32)

    return _compute_call(
        body, name="proj_fwd", grid=(t // TILE_M,),
        in_specs=[_row_tile(D_MODEL), _resident((1, D_MODEL)), _resident(wg_in.shape)],
        out_specs=[_row_tile(D_MODEL)] + [_row_tile(SHARD_W)] * N_CHIPS,
        out_shape=[jax.ShapeDtypeStruct((t, D_MODEL), MXU_DTYPE)]
        + [jax.ShapeDtypeStruct((t, SHARD_W), F32)] * N_CHIPS,
        operands=(x2d, g_mixer, wg_in), semantics=("arbitrary",), side=side)


def _retention_consts(seq):
    half = QK_DIM // 2
    inv = ROPE_BASE ** (-jnp.arange(half, dtype=F32) / half)
    ang = jnp.arange(seq, dtype=F32)[:, None] * inv[None, :]
    cos, sin = jnp.cos(ang), jnp.sin(ang)
    cos_full = jnp.concatenate([cos, cos], axis=1)
    sin_signed = jnp.concatenate([-sin, sin], axis=1)
    log_g = jnp.log(1.0 - 2.0 ** (-5.0 - jnp.arange(RET_HEADS, dtype=F32)))
    idx = jnp.arange(CHUNK, dtype=F32)
    diff = idx[:, None] - idx[None, :]
    decay_in = jnp.where(diff[None] >= 0, jnp.exp(jnp.maximum(diff, 0.0)[None] * log_g[:, None, None]), 0.0)
    zeta = jnp.exp((CHUNK - 1.0 - idx)[None, :] * log_g[:, None])
    xi = jnp.exp((idx + 1.0)[None, :] * log_g[:, None])
    zeta = jnp.broadcast_to(zeta[:, :, None], (RET_HEADS, CHUNK, QK_DIM))
    xi = jnp.broadcast_to(xi[:, :, None], (RET_HEADS, CHUNK, QK_DIM))
    return cos_full, sin_signed, decay_in, zeta, xi


def _qkv(pj_ref, rows, h):
    q = pj_ref[rows, h * QK_DIM:(h + 1) * QK_DIM]
    k = pj_ref[rows, RET_HEADS * QK_DIM + h * QK_DIM:RET_HEADS * QK_DIM + (h + 1) * QK_DIM]
    v = pj_ref[rows, 2 * RET_HEADS * QK_DIM + h * V_DIM:2 * RET_HEADS * QK_DIM + (h + 1) * V_DIM]
    return q, k, v


def _retention_fwd_call(pj0, consts, n_seq, seq):
    cos_full, sin_signed, decay_in, zeta, xi = consts
    nb = seq // TILE_M
    cpb = TILE_M // CHUNK

    def body(pj_ref, cos_ref, sin_ref, d_ref, zeta_ref, xi_ref, o_ref, st_ref, state):
        @pl.when(pl.program_id(1) == 0)
        def _():
            state[...] = jnp.zeros_like(state)

        for cc in range(cpb):
            rows = slice(cc * CHUNK, (cc + 1) * CHUNK)
            cs, sn = cos_ref[rows, :], sin_ref[rows, :]
            for h in range(RET_HEADS):
                q, k, v = _qkv(pj_ref, rows, h)
                qt = (q * cs + _rot(q) * sn) * QK_SCALE
                kt = k * cs + _rot(k) * sn
                prev = state[h]
                st_ref[cc, h] = prev.astype(st_ref.dtype)
                scores = _mm_nt(qt, kt) * d_ref[h]
                o_ref[rows, h * V_DIM:(h + 1) * V_DIM] = _mm(scores, v) + _mm(qt * xi_ref[h], prev)
                state[h] = _mm_tn(kt * zeta_ref[h], v) + CHUNK_DECAY[h] * prev

    row = lambda b, n: (b * nb + n, 0)
    pos = lambda b, n: (n, 0)
    return pl.pallas_call(
        body, name="retention_fwd", grid=(n_seq, nb),
        in_specs=[pl.BlockSpec((TILE_M, SHARD_W), row), pl.BlockSpec((TILE_M, QK_DIM), pos),
                  pl.BlockSpec((TILE_M, QK_DIM), pos), _resident(decay_in.shape), _resident(zeta.shape),
                  _resident(xi.shape)],
        out_specs=[pl.BlockSpec((TILE_M, D_MODEL), row),
                   pl.BlockSpec((cpb, RET_HEADS, QK_DIM, V_DIM), lambda b, n: (b * nb + n, 0, 0, 0))],
        out_shape=[jax.ShapeDtypeStruct((n_seq * seq, D_MODEL), F32),
                   jax.ShapeDtypeStruct((n_seq * seq // CHUNK, RET_HEADS, QK_DIM, V_DIM), MXU_DTYPE)],
        scratch_shapes=[pltpu.VMEM((RET_HEADS, QK_DIM, V_DIM), F32)],
        compiler_params=_params(("arbitrary", "arbitrary")),
    )(pj0, cos_full, sin_signed, decay_in, zeta, xi)


def _merge_fwd_call(pj1, pj2, pj3, ret_raw, x2d, ws, bst, w_ro, w_so, w_o):
    t = x2d.shape[0]

    def body(pj1_ref, pj2_ref, pj3_ref, rr_ref, x_ref, ws_ref, bst_ref, wro_ref, wso_ref, wo_ref,
             x1_ref, ret_ref, sgu_ref, mg_ref, a_ref, b_ref):
        f = _branch_forward(pj1_ref[...], pj2_ref[...], rr_ref[...], _causal_ws(ws_ref), bst_ref[...])
        ret = f["ret"].astype(MXU_DTYPE)
        sgu = f["sgu"].astype(MXU_DTYPE)
        a = jnp.dot(ret, wro_ref[...], preferred_element_type=F32)
        b = jnp.dot(sgu, wso_ref[...], preferred_element_type=F32)
        pj3 = pj3_ref[...]
        merged = (_sigmoid(pj3[:, :D_MODEL]) * a + _sigmoid(pj3[:, D_MODEL:]) * b).astype(MXU_DTYPE)
        x1_ref[...] = x_ref[...] + jnp.dot(merged, wo_ref[...], preferred_element_type=F32)
        ret_ref[...] = ret
        sgu_ref[...] = sgu
        mg_ref[...] = merged
        a_ref[...] = a
        b_ref[...] = b

    sq = (D_MODEL, D_MODEL)
    return pl.pallas_call(
        body, name="merge_fwd", grid=(t // TILE_M,),
        in_specs=[_row_tile(SHARD_W)] * 3 + [_row_tile(D_MODEL)] * 2
        + [_resident(ws.shape), _resident(bst.shape), _resident(sq), _resident(sq), _resident(sq)],
        out_specs=[_row_tile(D_MODEL)] * 6,
        out_shape=[jax.ShapeDtypeStruct((t, D_MODEL), F32)] + [jax.ShapeDtypeStruct((t, D_MODEL), MXU_DTYPE)] * 3
        + [jax.ShapeDtypeStruct((t, D_MODEL), F32)] * 2,
        compiler_params=_params(("arbitrary",)),
    )(pj1, pj2, pj3, ret_raw, x2d, ws, bst, w_ro, w_so, w_o)


def _ple_call(x1, p2d, target, g_ple, g_final, w_pg, w_pp):
    t = x1.shape[0]

    def body(x1_ref, p_ref, t_ref, gp_ref, gf_ref, wpg_ref, wpp_ref,
             dx1_ref, hp_ref, dz_ref, dpp_ref, loss_ref, dgp_ref, dgf_ref):
        @pl.when(pl.program_id(0) == 0)
        def _():
            loss_ref[...] = jnp.zeros_like(loss_ref)
            dgp_ref[...] = jnp.zeros_like(dgp_ref)
            dgf_ref[...] = jnp.zeros_like(dgf_ref)

        x1v = x1_ref[...]
        xn1, r1 = _rms(x1v)
        hp = (xn1 * gp_ref[...]).astype(MXU_DTYPE)
        gate = _sigmoid(jnp.dot(hp, wpg_ref[...], preferred_element_type=F32))
        pp = jnp.dot(p_ref[...].astype(MXU_DTYPE), wpp_ref[...], preferred_element_type=F32)
        x2 = x1v + gate * pp
        xn2, r2 = _rms(x2)
        err = xn2 * gf_ref[...] - t_ref[...]
        loss_ref[...] += (0.5 / D_MODEL) * jnp.sum(jnp.sum(err * err, axis=1, keepdims=True), axis=0, keepdims=True)
        dy = err * (1.0 / D_MODEL)
        dgf_ref[...] += jnp.sum(dy * xn2, axis=0, keepdims=True)
        dx2 = _rms_bwd(dy * gf_ref[...], xn2, r2)
        dpp = (dx2 * gate).astype(MXU_DTYPE)
        dz = (dx2 * pp * gate * (1.0 - gate)).astype(MXU_DTYPE)
        dhp = _mm_nt(dz, wpg_ref[...])
        dgp_ref[...] += jnp.sum(dhp * xn1, axis=0, keepdims=True)
        dx1_ref[...] = dx2 + _rms_bwd(dhp * gp_ref[...], xn1, r1)
        hp_ref[...] = hp
        dz_ref[...] = dz
        dpp_ref[...] = dpp

    vec = _resident((1, D_MODEL))
    return pl.pallas_call(
        body, name="ple_fwd_bwd", grid=(t // TILE_M,),
        in_specs=[_row_tile(D_MODEL), _row_tile(PLE_DIM), _row_tile(D_MODEL), vec, vec,
                  _resident((D_MODEL, D_MODEL)), _resident((PLE_DIM, D_MODEL))],
        out_specs=[_row_tile(D_MODEL)] * 4 + [pl.BlockSpec(LOSS_TILE, lambda i: (0, 0)),
                                              pl.BlockSpec((1, D_MODEL), lambda i: (0, 0)),
                                              pl.BlockSpec((1, D_MODEL), lambda i: (0, 0))],
        out_shape=[jax.ShapeDtypeStruct((t, D_MODEL), F32)] + [jax.ShapeDtypeStruct((t, D_MODEL), MXU_DTYPE)] * 3
        + [jax.ShapeDtypeStruct(LOSS_TILE, F32), jax.ShapeDtypeStruct((1, D_MODEL), F32),
           jax.ShapeDtypeStruct((1, D_MODEL), F32)],
        compiler_params=_params(("arbitrary",)),
    )(x1, p2d, target, g_ple, g_final, w_pg, w_pp)


def _merge_bwd_call(dx1, pj1, pj2, pj3, ret_raw, a, b, ws, bst, w_ro, w_so, w_o):
    t = dx1.shape[0]

    def body(dx1_ref, pj1_ref, pj2_ref, pj3_ref, rr_ref, a_ref, b_ref, ws_ref, bst_ref, wro_ref, wso_ref, wo_ref,
             dpj1_ref, dpj2_ref, dpj3_ref, drr_ref, da_ref, db_ref, dws_ref, dbst_ref):
        @pl.when(pl.program_id(0) == 0)
        def _():
            dws_ref[...] = jnp.zeros_like(dws_ref)
            dbst_ref[...] = jnp.zeros_like(dbst_ref)

        wsc = _causal_ws(ws_ref)
        f = _branch_forward(pj1_ref[...], pj2_ref[...], rr_ref[...], wsc, bst_ref[...])
        pj3 = pj3_ref[...]
        smr, sms = _sigmoid(pj3[:, :D_MODEL]), _sigmoid(pj3[:, D_MODEL:])
        dmerged = _mm_nt(dx1_ref[...], wo_ref[...])
        da = (dmerged * smr).astype(MXU_DTYPE)
        db = (dmerged * sms).astype(MXU_DTYPE)
        dpj3_ref[:, :D_MODEL] = (dmerged * a_ref[...] * smr * (1.0 - smr)).astype(dpj3_ref.dtype)
        dpj3_ref[:, D_MODEL:] = (dmerged * b_ref[...] * sms * (1.0 - sms)).astype(dpj3_ref.dtype)
        da_ref[...] = da
        db_ref[...] = db

        dret = _mm_nt(da, wro_ref[...])
        dpj1_ref[:, :D_MODEL] = (dret * f["rn"] * f["dsil_rg"]).astype(dpj1_ref.dtype)
        drn = dret * f["sil_rg"]
        for h in range(RET_HEADS):
            cols = slice(h * V_DIM, (h + 1) * V_DIM)
            drr_ref[:, cols] = _unit_norm_bwd(drn[:, cols], f["rn"][:, cols], f["rstd_r"][h])

        dsgu = _mm_nt(db, wso_ref[...])
        dpj2_ref[:, D_MODEL:] = (dsgu * f["u"] * f["mixed"] * f["dsil_sg"]).astype(dpj2_ref.dtype)
        tg = dsgu * f["sil_sg"]
        dpj1_ref[:, D_MODEL:] = (tg * f["mixed"] * _gelu_grad(f["su"])).astype(dpj1_ref.dtype)
        dmixed = tg * f["u"]
        rows = []
        for cc in range(TILE_M // CHUNK):
            cols = []
            for g in range(SGU_GROUPS):
                rs, cs = slice(cc * CHUNK, (cc + 1) * CHUNK), slice(g * GROUP_DIM, (g + 1) * GROUP_DIM)
                dm = dmixed[rs, cs]
                cols.append(_mm_tn(wsc[g], dm))
                dws_ref[g] += _mm_nt(dm, f["vn"][rs, cs])
                dbst_ref[:, g:g + 1] += jnp.sum(dm, axis=1, keepdims=True)
            rows.append(jnp.concatenate(cols, axis=1))
        dvn = jnp.concatenate(rows, axis=0)
        dvv = _unit_norm_bwd(dvn, f["vn"], f["rstd_v"])
        dpj2_ref[:, :D_MODEL] = (dvv * _gelu_grad(f["sv"])).astype(dpj2_ref.dtype)

    sq = (D_MODEL, D_MODEL)
    return pl.pallas_call(
        body, name="merge_bwd", grid=(t // TILE_M,),
        in_specs=[_row_tile(D_MODEL)] + [_row_tile(SHARD_W)] * 3 + [_row_tile(D_MODEL)] * 3
        + [_resident(ws.shape), _resident(bst.shape), _resident(sq), _resident(sq), _resident(sq)],
        out_specs=[_row_tile(SHARD_W)] * 3 + [_row_tile(D_MODEL)] * 3
        + [pl.BlockSpec(ws.shape, lambda i: (0, 0, 0)), pl.BlockSpec(bst.shape, lambda i: (0, 0))],
        out_shape=[jax.ShapeDtypeStruct((t, SHARD_W), MXU_DTYPE)] * 3 + [jax.ShapeDtypeStruct((t, D_MODEL), F32)]
        + [jax.ShapeDtypeStruct((t, D_MODEL), MXU_DTYPE)] * 2
        + [jax.ShapeDtypeStruct(ws.shape, F32), jax.ShapeDtypeStruct(bst.shape, F32)],
        compiler_params=_params(("arbitrary",)),
    )(dx1, pj1, pj2, pj3, ret_raw, a, b, ws, bst, w_ro, w_so, w_o)


def _retention_bwd_call(pj0, drr, states, consts, n_seq, seq, side):
    cos_full, sin_signed, decay_in, zeta, xi = consts
    nb = seq // TILE_M
    cpb = TILE_M // CHUNK

    def body(pj_ref, do_ref, st_ref, cos_ref, sin_ref, d_ref, zeta_ref, xi_ref, dpj_ref, gstate):
        @pl.when(pl.program_id(1) == 0)
        def _():
            gstate[...] = jnp.zeros_like(gstate)

        for cc in reversed(range(cpb)):
            rows = slice(cc * CHUNK, (cc + 1) * CHUNK)
            cs, sn = cos_ref[rows, :], sin_ref[rows, :]
            for h in range(RET_HEADS):
                q, k, v = _qkv(pj_ref, rows, h)
                qt = (q * cs + _rot(q) * sn) * QK_SCALE
                kt = k * cs + _rot(k) * sn
                d_out = do_ref[rows, h * V_DIM:(h + 1) * V_DIM]
                prev = st_ref[cc, h]
                g = gstate[h]
                dec = d_ref[h]
                scores_d = _mm_nt(qt, kt) * dec
                dscores = _mm_nt(d_out, v) * dec
                kz = kt * zeta_ref[h]
                qx = qt * xi_ref[h]
                dv = _mm_tn(scores_d, d_out) + _mm(kz, g)
                dqt = (_mm(dscores, kt) + _mm_nt(d_out, prev) * xi_ref[h]) * QK_SCALE
                dkt = _mm_tn(dscores, qt) + _mm_nt(v, g) * zeta_ref[h]
                gstate[h] = _mm_tn(qx, d_out) + CHUNK_DECAY[h] * g
                dq = dqt * cs + _rot(dqt * sn)
                dk = dkt * cs + _rot(dkt * sn)
                dpj_ref[rows, h * QK_DIM:(h + 1) * QK_DIM] = dq.astype(dpj_ref.dtype)
                dpj_ref[rows, RET_HEADS * QK_DIM + h * QK_DIM:RET_HEADS * QK_DIM + (h + 1) * QK_DIM] = dk.astype(
                    dpj_ref.dtype)
                dpj_ref[rows, 2 * RET_HEADS * QK_DIM + h * V_DIM:2 * RET_HEADS * QK_DIM + (h + 1) * V_DIM] = dv.astype(
                    dpj_ref.dtype)

    row = lambda b, n: (b * nb + nb - 1 - n, 0)
    pos = lambda b, n: (nb - 1 - n, 0)
    return _compute_call(
        body, name="retention_bwd", grid=(n_seq, nb),
        in_specs=[pl.BlockSpec((TILE_M, SHARD_W), row), pl.BlockSpec((TILE_M, D_MODEL), row),
                  pl.BlockSpec((cpb, RET_HEADS, QK_DIM, V_DIM), lambda b, n: (b * nb + nb - 1 - n, 0, 0, 0)),
                  pl.BlockSpec((TILE_M, QK_DIM), pos), pl.BlockSpec((TILE_M, QK_DIM), pos),
                  _resident(decay_in.shape), _resident(zeta.shape), _resident(xi.shape)],
        out_specs=[pl.BlockSpec((TILE_M, SHARD_W), row)],
        out_shape=[jax.ShapeDtypeStruct((n_seq * seq, SHARD_W), MXU_DTYPE)],
        scratch_shapes=[pltpu.VMEM((RET_HEADS, QK_DIM, V_DIM), F32)],
        operands=(pj0, drr, states, cos_full, sin_signed, decay_in, zeta, xi),
        semantics=("arbitrary", "arbitrary"), side=side)


def _dx_call(dpj, x2d, dx1, g_mixer, wg_in, side):
    t = x2d.shape[0]

    def body(d0, d1, d2, d3, x_ref, dx1_ref, g_ref, w_ref, dx_ref, dg_ref):
        @pl.when(pl.program_id(0) == 0)
        def _():
            dg_ref[...] = jnp.zeros_like(dg_ref)

        dh = _mm_nt(d0[...], w_ref[0])
        for j, d_ref in enumerate((d1, d2, d3)):
            dh += _mm_nt(d_ref[...], w_ref[j + 1])
        xn, r = _rms(x_ref[...])
        dg_ref[...] += jnp.sum(dh * xn, axis=0, keepdims=True)
        dx_ref[...] = dx1_ref[...] + _rms_bwd(dh * g_ref[...], xn, r)

    return _compute_call(
        body, name="dx_bwd", grid=(t // TILE_M,),
        in_specs=[_row_tile(SHARD_W)] * N_CHIPS + [_row_tile(D_MODEL)] * 2
        + [_resident((1, D_MODEL)), _resident(wg_in.shape)],
        out_specs=[_row_tile(D_MODEL), pl.BlockSpec((1, D_MODEL), lambda i: (0, 0))],
        out_shape=[jax.ShapeDtypeStruct((t, D_MODEL), F32), jax.ShapeDtypeStruct((1, D_MODEL), F32)],
        operands=(*dpj, x2d, dx1, g_mixer, wg_in), semantics=("arbitrary",), side=side)


def _wgrad_call(name, lhs, rhs, block_n, out_cols=None, block_t=512, into=None, slot=0, n_slots=1):
    t, k = lhs.shape
    n = rhs.shape[1]
    steps = t // block_t
    out_cols = block_n if out_cols is None else out_cols
    per = block_n // out_cols
    first_block = slot * (n // block_n)

    def body(l_ref, r_ref, *rest):
        o_ref, acc = rest[-2:]
        @pl.when(pl.program_id(1) == 0)
        def _():
            acc[...] = jnp.zeros_like(acc)

        acc[...] += _mm_tn(l_ref[...], r_ref[...])

        @pl.when(pl.program_id(1) == steps - 1)
        def _():
            for s in range(per):
                o_ref[s] = acc[:, s * out_cols:(s + 1) * out_cols].astype(o_ref.dtype)

    return pl.pallas_call(
        body, name=name, grid=(n // block_n, steps),
        in_specs=[pl.BlockSpec((block_t, k), lambda j, i: (i, 0)), pl.BlockSpec((block_t, block_n), lambda j, i: (i, j))]
        + ([] if into is None else [ANY]),
        out_specs=pl.BlockSpec((per, k, out_cols), lambda j, i: (first_block + j, 0, 0)),
        out_shape=jax.ShapeDtypeStruct((n_slots * (n // out_cols), k, out_cols), COMM_DTYPE),
        scratch_shapes=[pltpu.VMEM((k, block_n), F32)],
        input_output_aliases={} if into is None else {2: 0},
        compiler_params=_params(("arbitrary", "arbitrary")),
    )(*((lhs, rhs) if into is None else (lhs, rhs, into)))


def _position():
    return lax.axis_index("x"), lax.axis_index("y"), lax.axis_index("c")


def _position_array():
    x, y, c = _position()
    return jnp.stack([2 * x + y, c]).astype(jnp.int32)


def _other_chip(x, y, k):
    return (1 - x if k & 2 else x), (1 - y if k & 1 else y)


def _plan_side(operands, out_shapes, plan, n_remote, aliases=None):
    def copies(ins, outs, send_sem, recv_sem):
        remote = plan(ins, outs)
        assert len(remote) == n_remote
        return [pltpu.make_async_remote_copy(src_ref=src, dst_ref=dst, send_sem=send_sem.at[i],
                                             recv_sem=recv_sem.at[i], device_id=dev, device_id_type=MESH)
                for i, (src, dst, dev) in enumerate(remote)]

    def start(*a):
        for cp in copies(*a):
            cp.start()

    def finish(*a):
        for cp in copies(*a):
            cp.wait()

    return _Side(operands, out_shapes, n_remote, start, finish, aliases)


def _place_cast_call(name, w, pos):
    rows, width = w.shape
    block_rows = min(256, rows)

    def body(pos_ref, w_ref, o_ref):
        o_ref[...] = w_ref[...].astype(o_ref.dtype)

    return pl.pallas_call(
        body, name=name,
        grid_spec=pltpu.PrefetchScalarGridSpec(
            num_scalar_prefetch=1, grid=(rows // block_rows,),
            in_specs=[pl.BlockSpec((block_rows, width), lambda i, pos: (i, 0))],
            out_specs=pl.BlockSpec((None, block_rows, width), lambda i, pos: (pos[0], i, 0))),
        out_shape=jax.ShapeDtypeStruct((N_CHIPS, rows, width), MXU_DTYPE),
        compiler_params=_params(("arbitrary",)),
    )(pos, w)


def _gather_side(placed):
    n = len(placed)

    def copies(kind, bufs, send_sem, recv_sem):
        x, y, c = _position()
        me = 2 * x + y
        made = []
        for i in range(n):
            hr = placed[i].shape[1] // 2
            for k in (1, 2, 3):
                px, py = _other_chip(x, y, k)
                chip, core, slot, dev = [(me, c, k - 1, (px, py, c)), (2 * px + py, c, 2 + k, (x, y, 1 - c)),
                                         (2 * px + py, 1 - c, 2 + k, (x, y, 1 - c))][kind]
                piece = bufs[i].at[chip, pl.ds(core * hr, hr)]
                made.append(pltpu.make_async_remote_copy(
                    src_ref=piece, dst_ref=piece, send_sem=send_sem.at[6 * i + slot],
                    recv_sem=recv_sem.at[6 * i + slot], device_id=dev, device_id_type=MESH))
        return made

    def start(ins, outs, send_sem, recv_sem):
        for cp in copies(0, outs, send_sem, recv_sem):
            cp.start()

    def finish(ins, outs, send_sem, recv_sem):
        first, onward = copies(0, outs, send_sem, recv_sem), copies(1, outs, send_sem, recv_sem)
        for landed, cp in zip(first, onward):
            landed.wait_recv()
            cp.start()
        for cp in copies(2, outs, send_sem, recv_sem):
            cp.wait_recv()
        for cp in first + onward:
            cp.wait_send()

    return _Side(placed, [jax.ShapeDtypeStruct(a.shape, a.dtype) for a in placed], 6 * n, start, finish,
                 {i: i for i in range(n)})


def _sum_call(name, pos, terms, rows, width, out_dtype, block_rows, out_rows=None, out_index=None):
    out_rows = rows if out_rows is None else out_rows
    out_index = (lambda i, pos, br: i) if out_index is None else out_index

    def body(pos_ref, *refs):
        acc = refs[0][...].astype(F32)
        for r in refs[1:-1]:
            acc = acc + r[...].astype(F32)
        refs[-1][...] = acc.astype(out_dtype)

    def spec(index):
        return pl.BlockSpec((block_rows, width), lambda i, pos: (index(i, pos, block_rows), 0))

    return pl.pallas_call(
        body, name=name,
        grid_spec=pltpu.PrefetchScalarGridSpec(
            num_scalar_prefetch=1, grid=(rows // block_rows,),
            in_specs=[spec(index) for _, index in terms], out_specs=spec(out_index)),
        out_shape=jax.ShapeDtypeStruct((out_rows, width), out_dtype),
        compiler_params=_params(("arbitrary",)),
    )(pos, *[arr for arr, _ in terms])


def _pair_stage(tag, grads, pos):
    n = len(grads)
    shapes = [(g.shape[0] // N_CHIPS, g.shape[1]) for g in grads]

    def plan(ins, outs):
        x, y, c = _position()
        remote = []
        for i, (r, _) in enumerate(shapes):
            hr = r // 2
            for j in range(N_CHIPS):
                remote.append((ins[i].at[pl.ds(j * r + (1 - c) * hr, hr)], outs[i].at[pl.ds(j * hr, hr)],
                               (x, y, 1 - c)))
        return remote

    pair = _exchange_call(f"grad_pair_exchange_{tag}", _plan_side(
        grads, [jax.ShapeDtypeStruct((N_CHIPS * (r // 2), w), COMM_DTYPE) for r, w in shapes], plan, N_CHIPS * n))

    def own_half(r):
        def index(s, pos, br):
            per = (r // 2) // br
            return (s // per) * (r // br) + pos[1] * per + s % per
        return index

    return [_sum_call(f"grad_pair_sum_{tag}{i}", pos, [(grads[i], own_half(r)), (pair[i], lambda s, pos, br: s)],
                      N_CHIPS * (r // 2), w, COMM_DTYPE, min(256, r // 2)) for i, (r, w) in enumerate(shapes)]


def _chip_side(pair_sums):
    halves = [(p.shape[0] // N_CHIPS, p.shape[1]) for p in pair_sums]

    def plan(ins, outs):
        x, y, c = _position()
        remote = []
        for i, (hr, _) in enumerate(halves):
            for k in (1, 2, 3):
                px, py = _other_chip(x, y, k)
                remote.append((ins[i].at[pl.ds((2 * px + py) * hr, hr)], outs[i].at[pl.ds((k - 1) * hr, hr)],
                               (px, py, c)))
        return remote

    return _plan_side(pair_sums, [jax.ShapeDtypeStruct((3 * hr, w), COMM_DTYPE) for hr, w in halves], plan,
                      3 * len(pair_sums))


def _finish_stage(tag, pair_sums, chip, pos):
    n = len(pair_sums)
    halves = [(p.shape[0] // N_CHIPS, p.shape[1]) for p in pair_sums]

    def slab(k, hr):
        return lambda s, pos, br: k * (hr // br) + s

    reduced = []
    for i, (hr, w) in enumerate(halves):
        terms = [(pair_sums[i], lambda s, pos, br, hr=hr: pos[0] * (hr // br) + s)]
        terms += [(chip[i], slab(k, hr)) for k in range(3)]
        reduced.append(_sum_call(f"grad_chip_sum_{tag}{i}", pos, terms, hr, w, F32, min(256, hr), out_rows=2 * hr,
                                 out_index=lambda s, pos, br, hr=hr: pos[1] * (hr // br) + s))

    def plan(ins, outs):
        x, y, c = _position()
        remote = []
        for i, (hr, _) in enumerate(halves):
            mine = outs[i].at[pl.ds(c * hr, hr)]
            remote.append((mine, mine, (x, y, 1 - c)))
        return remote

    return _exchange_call(f"grad_half_exchange_{tag}", _plan_side(
        reduced, [jax.ShapeDtypeStruct((2 * hr, w), F32) for hr, w in halves], plan, n, {i: i for i in range(n)}))


def _gather_small(parts):
    n = len(parts)

    def body(*refs):
        ins, outs = refs[:n], refs[n:2 * n]
        loc_sem, send_sem, recv_sem = refs[2 * n:]
        x, y, c = _position()
        copies = []
        for i in range(n):
            mine = outs[i].at[4 * x + 2 * y + c]
            copies.append(pltpu.make_async_copy(ins[i], mine, loc_sem.at[i]))
            for d in range(1, 8):
                px, py = _other_chip(x, y, d >> 1)
                copies.append(pltpu.make_async_remote_copy(
                    src_ref=ins[i], dst_ref=mine, send_sem=send_sem.at[7 * i + d - 1],
                    recv_sem=recv_sem.at[7 * i + d - 1], device_id=(px, py, 1 - c if d & 1 else c),
                    device_id_type=MESH))
        for cp in copies:
            cp.start()
        for cp in copies:
            cp.wait()

    return pl.pallas_call(
        body, name="small_grad_gather", in_specs=[ANY] * n, out_specs=[ANY] * n,
        out_shape=[jax.ShapeDtypeStruct((8,) + a.shape, F32) for a in parts],
        scratch_shapes=[pltpu.SemaphoreType.DMA((n,)), pltpu.SemaphoreType.DMA((7 * n,)),
                        pltpu.SemaphoreType.DMA((7 * n,))],
        compiler_params=pltpu.CompilerParams(has_side_effects=True),
    )(*parts)


def _adamw(w, g, m, v):
    m = ADAM_B1 * m + (1.0 - ADAM_B1) * g
    v = ADAM_B2 * v + (1.0 - ADAM_B2) * (g * g)
    m_hat = m / (1.0 - ADAM_B1 ** ADAM_STEP)
    v_hat = v / (1.0 - ADAM_B2 ** ADAM_STEP)
    delta = -ADAM_LR * (m_hat / (jnp.sqrt(v_hat) + ADAM_EPS) + ADAM_WD * w)
    return delta, m, v


def _adamw_call(name, w, g, m, v):
    rows, width = w.shape
    block_rows = min(256, rows)

    def body(w_ref, g_ref, m_ref, v_ref, d_out, m_out, v_out):
        d_out[...], m_out[...], v_out[...] = _adamw(w_ref[...], g_ref[...], m_ref[...], v_ref[...])

    spec = pl.BlockSpec((block_rows, width), lambda i: (i, 0))
    return pl.pallas_call(
        body, name=name, grid=(rows // block_rows,), in_specs=[spec] * 4, out_specs=[spec] * 3,
        out_shape=[jax.ShapeDtypeStruct(w.shape, F32)] * 3,
        compiler_params=_params(("arbitrary",)),
    )(w, g, m, v)


def _small_adamw_call(gathered, weights, moments_m, moments_v):
    n = len(weights)

    def body(*refs):
        all_refs, refs = refs[:n + 1], refs[n + 1:]
        w_refs, m_refs, v_refs, outs = refs[:n], refs[n:2 * n], refs[2 * n:3 * n], refs[3 * n:]

        def total(ref):
            acc = ref[0]
            for d in range(1, 8):
                acc = acc + ref[d]
            return acc

        outs[0][...] = total(all_refs[n])
        for i in range(n):
            g = total(all_refs[i])
            if i == 0:
                row = lax.broadcasted_iota(jnp.int32, g.shape, 0)
                col = lax.broadcasted_iota(jnp.int32, g.shape, 1)
                g = jnp.where((row % CHUNK) >= col, g, 0.0)
            g_out, d_out, m_out, v_out = outs[1 + 4 * i:5 + 4 * i]
            g_out[...] = g
            d_out[...], m_out[...], v_out[...] = _adamw(w_refs[i][...], g, m_refs[i][...], v_refs[i][...])

    out_shape = [jax.ShapeDtypeStruct(LOSS_TILE, F32)]
    for w in weights:
        out_shape += [jax.ShapeDtypeStruct(w.shape, F32)] * 4
    res = pl.pallas_call(
        body, name="small_adamw", out_shape=out_shape,
        compiler_params=pltpu.CompilerParams(vmem_limit_bytes=VMEM_LIMIT),
    )(*gathered, *weights, *moments_m, *moments_v)
    return res[0], [res[1 + 4 * i:5 + 4 * i] for i in range(n)]


def kernel(x, p, w_in, w_ret_out, w_sgu_out, w_out, sgu_ws, sgu_bs, w_ple_gate, w_ple_proj, g_mixer, g_ple, g_final, loss_target, m_w_in, m_w_ret_out, m_w_sgu_out, m_w_out, m_sgu_ws, m_sgu_bs, m_w_ple_gate, m_w_ple_proj, m_g_mixer, m_g_ple, m_g_final, v_w_in, v_w_ret_out, v_w_sgu_out, v_w_out, v_sgu_ws, v_sgu_bs, v_w_ple_gate, v_w_ple_proj, v_g_mixer, v_g_ple, v_g_final):
    n_seq, seq, _ = x.shape
    t = n_seq * seq
    x2d = x.reshape(t, D_MODEL)
    p2d = p.reshape(t, PLE_DIM)
    target = loss_target.reshape(t, D_MODEL)
    big = [w_in[0], w_ret_out[0], w_sgu_out[0], w_out[0], w_ple_gate[0], w_ple_proj[0]]
    big_m = [m_w_in[0], m_w_ret_out[0], m_w_sgu_out[0], m_w_out[0], m_w_ple_gate[0], m_w_ple_proj[0]]
    big_v = [v_w_in[0], v_w_ret_out[0], v_w_sgu_out[0], v_w_out[0], v_w_ple_gate[0], v_w_ple_proj[0]]

    pos = _position_array()
    placed = [_place_cast_call(f"place_weight_{i}", w, pos) for i, w in enumerate(big)]
    wg_in, = _exchange_call("gather_w_in", _gather_side(placed[:1]))
    ws = sgu_ws[0]
    bst = sgu_bs[0].T
    consts = _retention_consts(seq)

    (h, pj0, pj1, pj2, pj3), gathered = _proj_call(x2d, g_mixer, wg_in, _gather_side(placed[1:]))
    w_ro, w_so, w_o, w_pg = (w.reshape(D_MODEL, D_MODEL) for w in gathered[:4])
    w_pp = gathered[4].transpose(1, 0, 2).reshape(PLE_DIM, D_MODEL)
    ret_raw, states = _retention_fwd_call(pj0, consts, n_seq, seq)
    x1, ret, sgu, merged, a, b = _merge_fwd_call(pj1, pj2, pj3, ret_raw, x2d, ws, bst, w_ro, w_so, w_o)
    dx1, hp, dz, dpp, loss, dg_ple, dg_final = _ple_call(x1, p2d, target, g_ple, g_final.reshape(1, D_MODEL),
                                                         w_pg, w_pp)
    dpj1, dpj2, dpj3, drr, da, db, dws, dbst = _merge_bwd_call(dx1, pj1, pj2, pj3, ret_raw, a, b, ws, bst,
                                                               w_ro, w_so, w_o)

    rows_of = lambda g: g.reshape(g.shape[0] * g.shape[1], g.shape[2])
    tail_grads = [
        rows_of(_wgrad_call("wgrad_ret_out", ret, da, D_MODEL)),
        rows_of(_wgrad_call("wgrad_sgu_out", sgu, db, D_MODEL)),
        rows_of(_wgrad_call("wgrad_out", merged, dx1, D_MODEL)),
        rows_of(_wgrad_call("wgrad_ple_gate", hp, dz, D_MODEL)),
        rows_of(_wgrad_call("wgrad_ple_proj", p2d, dpp, D_MODEL, out_cols=PLE_DIM)),
    ]
    tail_sums = _pair_stage("tail", tail_grads, pos)
    (dpj0,), tail_chip = _retention_bwd_call(pj0, drr, states, consts, n_seq, seq, _chip_side(tail_sums))
    in_grad = None
    for j, d in enumerate((dpj0, dpj1, dpj2, dpj3)):
        in_grad = _wgrad_call(f"wgrad_in_{j}", h, d, SHARD_W, into=in_grad, slot=j, n_slots=N_CHIPS)
    in_sums = _pair_stage("in", [rows_of(in_grad)], pos)
    (dx, dg_mixer), in_chip = _dx_call((dpj0, dpj1, dpj2, dpj3), x2d, dx1, g_mixer, wg_in, _chip_side(in_sums))
    g_big = _finish_stage("in", in_sums, in_chip, pos) + _finish_stage("tail", tail_sums, tail_chip, pos)
    upd = [_adamw_call(f"adamw_{i}", big[i], g_big[i], big_m[i], big_v[i]) for i in range(len(big))]

    small_shapes = [(SGU_GROUPS * CHUNK, CHUNK), (SGU_GROUPS, CHUNK), (1, D_MODEL), (1, D_MODEL), (1, D_MODEL)]
    as_small = lambda arrays: [a_.reshape(s) for a_, s in zip(arrays, small_shapes)]
    small_g = _gather_small(as_small([dws, dbst.T, dg_mixer, dg_ple, dg_final]) + [loss])
    total, small = _small_adamw_call(small_g, as_small([sgu_ws, sgu_bs, g_mixer, g_ple, g_final]),
                                     as_small([m_sgu_ws, m_sgu_bs, m_g_mixer, m_g_ple, m_g_final]),
                                     as_small([v_sgu_ws, v_sgu_bs, v_g_mixer, v_g_ple, v_g_final]))
    out_small_shapes = [sgu_ws.shape, sgu_bs.shape, g_mixer.shape, g_ple.shape, g_final.shape]

    def ordered(big_list, kind):
        w_in_, w_ro_, w_so_, w_o_, w_pg_, w_pp_ = [b_[None] for b_ in big_list]
        s_ws, s_bs, s_gm, s_gp, s_gf = [small[i][kind].reshape(s) for i, s in enumerate(out_small_shapes)]
        return [w_in_, w_ro_, w_so_, w_o_, s_ws, s_bs, w_pg_, w_pp_, s_gm, s_gp, s_gf]

    out = [total[0, 0], dx.reshape(x.shape)]
    out += ordered(g_big, 0)
    out += ordered([u[0] for u in upd], 1)
    out += ordered([u[1] for u in upd], 2)
    out += ordered([u[2] for u in upd], 3)
    return tuple(out)
```

```python
import functools
import math

import numpy as np
import jax
import jax.numpy as jnp
from jax import lax
from jax.experimental import pallas as pl
from jax.experimental.pallas import tpu as pltpu

F32 = jnp.float32
MXU_DTYPE = jnp.bfloat16
COMM_DTYPE = jnp.bfloat16

D_MODEL = 1024
RET_HEADS = 4
QK_DIM = 128
V_DIM = 256
CHUNK = 128
SGU_GROUPS = 4
GROUP_DIM = 256
PLE_DIM = 256
N_CHIPS = 4
SHARD_W = 2048
ROPE_BASE = 10000.0
NORM_EPS = 1e-6
GN_EPS = 1e-5
QK_SCALE = QK_DIM ** -0.5
SQRT_HALF = math.sqrt(0.5)
INV_SQRT_2PI = 1.0 / math.sqrt(2.0 * math.pi)

ADAM_LR = 0.001
ADAM_B1 = 0.9
ADAM_B2 = 0.999
ADAM_EPS = 1e-08
ADAM_WD = 0.01
ADAM_STEP = 10

TILE_M = 256
LOSS_TILE = (8, 128)
VMEM_LIMIT = 56 * 1024 * 1024
MESH = pl.DeviceIdType.MESH
ANY = pl.BlockSpec(memory_space=pl.ANY)

CHUNK_DECAY = tuple(
    float(np.exp(np.float32(CHUNK) * np.log(np.float32(1.0 - 2.0 ** (-5.0 - h))))) for h in range(RET_HEADS))


def _mm(a, b):
    return jnp.dot(a.astype(MXU_DTYPE), b.astype(MXU_DTYPE), preferred_element_type=F32)


def _mm_nt(a, b):
    return lax.dot_general(a.astype(MXU_DTYPE), b.astype(MXU_DTYPE), (((1,), (1,)), ((), ())),
                           preferred_element_type=F32)


def _mm_tn(a, b):
    return lax.dot_general(a.astype(MXU_DTYPE), b.astype(MXU_DTYPE), (((0,), (0,)), ((), ())),
                           preferred_element_type=F32)


def _mean(x):
    return jnp.mean(x, axis=-1, keepdims=True)


def _sigmoid(x):
    return jax.nn.sigmoid(x)


def _silu_and_grad(x):
    s = _sigmoid(x)
    return x * s, s * (1.0 + x * (1.0 - s))


def _gelu(x):
    return 0.5 * x * (1.0 + lax.erf(x * SQRT_HALF))


def _gelu_grad(x):
    return 0.5 * (1.0 + lax.erf(x * SQRT_HALF)) + x * jnp.exp(-0.5 * x * x) * INV_SQRT_2PI


def _unit_norm(x, eps):
    xc = x - _mean(x)
    rstd = lax.rsqrt(_mean(xc * xc) + eps)
    return xc * rstd, rstd


def _unit_norm_bwd(dn, n, rstd):
    return rstd * (dn - _mean(dn) - n * _mean(dn * n))


def _rms(x):
    r = lax.rsqrt(_mean(x * x) + NORM_EPS)
    return x * r, r


def _rms_bwd(dxn, xn, r):
    return r * (dxn - xn * _mean(dxn * xn))


def _rot(x):
    return pltpu.roll(x, QK_DIM // 2, 1)


def _params(semantics, **kw):
    return pltpu.CompilerParams(dimension_semantics=semantics, vmem_limit_bytes=VMEM_LIMIT, **kw)


def _row_tile(width, tm=TILE_M):
    return pl.BlockSpec((tm, width), lambda i: (i, 0))


def _proj_tile(j):
    return pl.BlockSpec((None, TILE_M, SHARD_W), lambda i: (j, i, 0))


def _resident(shape):
    nd = len(shape)
    return pl.BlockSpec(shape, lambda *_: (0,) * nd, pipeline_mode=pl.Buffered(1))


def _causal_ws(ws_ref):
    row = lax.broadcasted_iota(jnp.int32, (CHUNK, CHUNK), 0)
    col = lax.broadcasted_iota(jnp.int32, (CHUNK, CHUNK), 1)
    return [jnp.where(row >= col, ws_ref[g], 0.0).astype(MXU_DTYPE) for g in range(SGU_GROUPS)]


def _heads(x, width):
    return [x[:, h * width:(h + 1) * width] for h in range(x.shape[1] // width)]


def _branch_forward(pj1, pj2, ret_raw, wsc, bst):
    rg, su = pj1[:, :D_MODEL], pj1[:, D_MODEL:]
    sv, sg = pj2[:, :D_MODEL], pj2[:, D_MODEL:]
    rn_parts, rstd_parts = zip(*[_unit_norm(r, GN_EPS) for r in _heads(ret_raw, V_DIM)])
    rn = jnp.concatenate(rn_parts, axis=1)
    sil_rg, dsil_rg = _silu_and_grad(rg)
    ret = rn * sil_rg
    u = _gelu(su)
    vn, rstd_v = _unit_norm(_gelu(sv), GN_EPS)
    rows = []
    for cc in range(pj1.shape[0] // CHUNK):
        cols = []
        for g in range(SGU_GROUPS):
            blk = vn[cc * CHUNK:(cc + 1) * CHUNK, g * GROUP_DIM:(g + 1) * GROUP_DIM]
            cols.append(_mm(wsc[g], blk) + bst[:, g:g + 1])
        rows.append(jnp.concatenate(cols, axis=1))
    mixed = jnp.concatenate(rows, axis=0)
    sil_sg, dsil_sg = _silu_and_grad(sg)
    sgu = u * mixed * sil_sg
    return dict(rg=rg, su=su, sv=sv, sg=sg, rn=rn, rstd_r=rstd_parts, sil_rg=sil_rg, dsil_rg=dsil_rg, ret=ret,
                u=u, vn=vn, rstd_v=rstd_v, mixed=mixed, sil_sg=sil_sg, dsil_sg=dsil_sg, sgu=sgu)


class _Side:
    def __init__(self, operands, out_shapes, n_sems, start, finish, aliases=None):
        self.operands, self.out_shapes, self.n_sems = list(operands), list(out_shapes), n_sems
        self.start, self.finish, self.aliases = start, finish, dict(aliases or {})


def _compute_call(body, *, name, grid, in_specs, out_specs, out_shape, operands, semantics, scratch_shapes=(),
                  side=None, prefetch=None, aliases=None):
    n_pre = 0 if prefetch is None else 1
    pre = () if prefetch is None else (prefetch,)

    def spec(in_specs, out_specs, scratch):
        return pltpu.PrefetchScalarGridSpec(num_scalar_prefetch=n_pre, grid=grid, in_specs=in_specs,
                                            out_specs=out_specs, scratch_shapes=scratch)

    if side is None:
        return pl.pallas_call(body, name=name, grid_spec=spec(in_specs, out_specs, list(scratch_shapes)),
                              out_shape=out_shape,
                              input_output_aliases={n_pre + a: b for a, b in (aliases or {}).items()},
                              compiler_params=_params(semantics))(*pre, *operands)
    n_in, n_out, s_in, s_out = len(operands), len(out_shape), len(side.operands), len(side.out_shapes)

    def carrier(*refs):
        pre_refs, refs = refs[:n_pre], refs[n_pre:]
        ins, refs = refs[:n_in], refs[n_in:]
        side_ins, refs = refs[:s_in], refs[s_in:]
        outs, refs = refs[:n_out], refs[n_out:]
        side_outs, refs = refs[:s_out], refs[s_out:]
        scratch, (send_sem, recv_sem) = refs[:-2], refs[-2:]
        ids = [pl.program_id(a) for a in range(len(grid))]
        first = functools.reduce(jnp.logical_and, [i == 0 for i in ids])
        last = functools.reduce(jnp.logical_and, [i == g - 1 for i, g in zip(ids, grid)])

        @pl.when(first)
        def _():
            side.start(side_ins, side_outs, send_sem, recv_sem)

        body(*pre_refs, *ins, *outs, *scratch)

        @pl.when(last)
        def _():
            side.finish(side_ins, side_outs, send_sem, recv_sem)

    all_aliases = {n_pre + a: b for a, b in (aliases or {}).items()}
    all_aliases.update({n_pre + n_in + a: n_out + b for a, b in side.aliases.items()})
    res = pl.pallas_call(
        carrier, name=name,
        grid_spec=spec(list(in_specs) + [ANY] * s_in, list(out_specs) + [ANY] * s_out,
                       list(scratch_shapes) + [pltpu.SemaphoreType.DMA((side.n_sems,))] * 2),
        out_shape=list(out_shape) + side.out_shapes, input_output_aliases=all_aliases,
        compiler_params=_params(semantics, has_side_effects=True),
    )(*pre, *operands, *side.operands)
    return res[:n_out], res[n_out:]


def _exchange_call(name, side):
    s_in = len(side.operands)

    def body(*refs):
        ins, outs = refs[:s_in], refs[s_in:s_in + len(side.out_shapes)]
        send_sem, recv_sem = refs[s_in + len(side.out_shapes):]
        side.start(ins, outs, send_sem, recv_sem)
        side.finish(ins, outs, send_sem, recv_sem)

    return pl.pallas_call(
        body, name=name, in_specs=[ANY] * s_in, out_specs=[ANY] * len(side.out_shapes), out_shape=side.out_shapes,
        scratch_shapes=[pltpu.SemaphoreType.DMA((side.n_sems,))] * 2, input_output_aliases=side.aliases,
        compiler_params=pltpu.CompilerParams(has_side_effects=True),
    )(*side.operands)


def _proj_call(x2d, g_mixer, placed_in, pos, side):
    t = x2d.shape[0]
    nt = t // TILE_M
    hr = placed_in.shape[1] // 2

    def body(pos_ref, x_ref, g_ref, win_ref, h_ref, pj_ref, w_ref, w_vmem, loc_sem, send_sem, recv_sem):
        k, i = pl.program_id(0), pl.program_id(1)
        x, y, c = _position()
        me = 2 * x + y

        def copy(slot, chip, core, dev):
            piece = w_ref.at[chip, pl.ds(core * hr, hr)]
            return pltpu.make_async_remote_copy(src_ref=piece, dst_ref=piece, send_sem=send_sem.at[slot],
                                                recv_sem=recv_sem.at[slot], device_id=dev, device_id_type=MESH)

        def first_hop(kk):
            px, py = _other_chip(x, y, kk)
            return copy(kk - 1, me, c, (px, py, c))

        def onward(kk, core):
            px, py = _other_chip(x, y, kk)
            return copy(2 + kk, 2 * px + py, core, (x, y, 1 - c))

        def load(chip):
            cp = pltpu.make_async_copy(w_ref.at[chip], w_vmem, loc_sem)
            cp.start()
            cp.wait()

        @pl.when(jnp.logical_and(k == 0, i == 0))
        def _():
            first_hop(1).start()
            first_hop(2).start()
            load(me)

        for kk in (1, 2, 3):
            @pl.when(jnp.logical_and(k == kk, i == 0))
            def _(kk=kk):
                if kk == 1:
                    first_hop(3).start()
                first_hop(kk).wait_recv()
                onward(kk, c).start()
                onward(kk, 1 - c).wait_recv()
                px, py = _other_chip(x, y, kk)
                load(2 * px + py)

        xn, _ = _rms(x_ref[...])
        h = (xn * g_ref[...]).astype(MXU_DTYPE)
        h_ref[...] = h
        pj_ref[...] = jnp.dot(h, w_vmem[...], preferred_element_type=F32)

        @pl.when(jnp.logical_and(k == N_CHIPS - 1, i == nt - 1))
        def _():
            for kk in (1, 2, 3):
                first_hop(kk).wait_send()
                onward(kk, c).wait_send()

    return _compute_call(
        body, name="proj_fwd", grid=(N_CHIPS, nt),
        in_specs=[pl.BlockSpec((TILE_M, D_MODEL), lambda k, i, pos: (i, 0)),
                  pl.BlockSpec((1, D_MODEL), lambda k, i, pos: (0, 0)), ANY],
        out_specs=[pl.BlockSpec((TILE_M, D_MODEL), lambda k, i, pos: (jnp.where(k == 0, i, nt), 0)),
                   pl.BlockSpec((None, TILE_M, SHARD_W), lambda k, i, pos: (jnp.bitwise_xor(pos[0], k), i, 0)), ANY],
        out_shape=[jax.ShapeDtypeStruct((t + TILE_M, D_MODEL), MXU_DTYPE),
                   jax.ShapeDtypeStruct((N_CHIPS, t, SHARD_W), F32),
                   jax.ShapeDtypeStruct(placed_in.shape, placed_in.dtype)],
        scratch_shapes=[pltpu.VMEM(placed_in.shape[1:], placed_in.dtype), pltpu.SemaphoreType.DMA,
                        pltpu.SemaphoreType.DMA((6,)), pltpu.SemaphoreType.DMA((6,))],
        operands=(x2d, g_mixer, placed_in), semantics=("arbitrary", "arbitrary"), side=side, prefetch=pos,
        aliases={2: 2})


def _retention_consts(seq):
    half = QK_DIM // 2
    inv = ROPE_BASE ** (-jnp.arange(half, dtype=F32) / half)
    ang = jnp.arange(seq, dtype=F32)[:, None] * inv[None, :]
    cos, sin = jnp.cos(ang), jnp.sin(ang)
    cos_full = jnp.concatenate([cos, cos], axis=1)
    sin_signed = jnp.concatenate([-sin, sin], axis=1)
    log_g = jnp.log(1.0 - 2.0 ** (-5.0 - jnp.arange(RET_HEADS, dtype=F32)))
    idx = jnp.arange(CHUNK, dtype=F32)
    diff = idx[:, None] - idx[None, :]
    decay_in = jnp.where(diff[None] >= 0, jnp.exp(jnp.maximum(diff, 0.0)[None] * log_g[:, None, None]), 0.0)
    zeta = jnp.exp((CHUNK - 1.0 - idx)[None, :] * log_g[:, None])
    xi = jnp.exp((idx + 1.0)[None, :] * log_g[:, None])
    zeta = jnp.broadcast_to(zeta[:, :, None], (RET_HEADS, CHUNK, QK_DIM))
    xi = jnp.broadcast_to(xi[:, :, None], (RET_HEADS, CHUNK, QK_DIM))
    return cos_full, sin_signed, decay_in, zeta, xi


def _qkv(pj_ref, rows, h):
    q = pj_ref[rows, h * QK_DIM:(h + 1) * QK_DIM]
    k = pj_ref[rows, RET_HEADS * QK_DIM + h * QK_DIM:RET_HEADS * QK_DIM + (h + 1) * QK_DIM]
    v = pj_ref[rows, 2 * RET_HEADS * QK_DIM + h * V_DIM:2 * RET_HEADS * QK_DIM + (h + 1) * V_DIM]
    return q, k, v


def _retention_fwd_call(pj0, consts, n_seq, seq):
    cos_full, sin_signed, decay_in, zeta, xi = consts
    nb = seq // TILE_M
    cpb = TILE_M // CHUNK

    def body(pj_ref, cos_ref, sin_ref, d_ref, zeta_ref, xi_ref, o_ref, st_ref, state):
        @pl.when(pl.program_id(1) == 0)
        def _():
            state[...] = jnp.zeros_like(state)

        for cc in range(cpb):
            rows = slice(cc * CHUNK, (cc + 1) * CHUNK)
            cs, sn = cos_ref[rows, :], sin_ref[rows, :]
            for h in range(RET_HEADS):
                q, k, v = _qkv(pj_ref, rows, h)
                qt = (q * cs + _rot(q) * sn) * QK_SCALE
                kt = k * cs + _rot(k) * sn
                prev = state[h]
                st_ref[cc, h] = prev.astype(st_ref.dtype)
                scores = _mm_nt(qt, kt) * d_ref[h]
                o_ref[rows, h * V_DIM:(h + 1) * V_DIM] = _mm(scores, v) + _mm(qt * xi_ref[h], prev)
                state[h] = _mm_tn(kt * zeta_ref[h], v) + CHUNK_DECAY[h] * prev

    row = lambda b, n: (b * nb + n, 0)
    pos = lambda b, n: (n, 0)
    return pl.pallas_call(
        body, name="retention_fwd", grid=(n_seq, nb),
        in_specs=[pl.BlockSpec((None, TILE_M, SHARD_W), lambda b, n: (0,) + row(b, n)),
                  pl.BlockSpec((TILE_M, QK_DIM), pos),
                  pl.BlockSpec((TILE_M, QK_DIM), pos), _resident(decay_in.shape), _resident(zeta.shape),
                  _resident(xi.shape)],
        out_specs=[pl.BlockSpec((TILE_M, D_MODEL), row),
                   pl.BlockSpec((cpb, RET_HEADS, QK_DIM, V_DIM), lambda b, n: (b * nb + n, 0, 0, 0))],
        out_shape=[jax.ShapeDtypeStruct((n_seq * seq, D_MODEL), F32),
                   jax.ShapeDtypeStruct((n_seq * seq // CHUNK, RET_HEADS, QK_DIM, V_DIM), MXU_DTYPE)],
        scratch_shapes=[pltpu.VMEM((RET_HEADS, QK_DIM, V_DIM), F32)],
        compiler_params=_params(("arbitrary", "arbitrary")),
    )(pj0, cos_full, sin_signed, decay_in, zeta, xi)


def _merge_fwd_call(pj1, pj2, pj3, ret_raw, x2d, ws, bst, w_ro, w_so, w_o):
    t = x2d.shape[0]

    def body(pj1_ref, pj2_ref, pj3_ref, rr_ref, x_ref, ws_ref, bst_ref, wro_ref, wso_ref, wo_ref,
             x1_ref, ret_ref, sgu_ref, mg_ref, a_ref, b_ref):
        f = _branch_forward(pj1_ref[...], pj2_ref[...], rr_ref[...], _causal_ws(ws_ref), bst_ref[...])
        ret = f["ret"].astype(MXU_DTYPE)
        sgu = f["sgu"].astype(MXU_DTYPE)
        a = jnp.dot(ret, wro_ref[...], preferred_element_type=F32)
        b = jnp.dot(sgu, wso_ref[...], preferred_element_type=F32)
        pj3 = pj3_ref[...]
        merged = (_sigmoid(pj3[:, :D_MODEL]) * a + _sigmoid(pj3[:, D_MODEL:]) * b).astype(MXU_DTYPE)
        x1_ref[...] = x_ref[...] + jnp.dot(merged, wo_ref[...], preferred_element_type=F32)
        ret_ref[...] = ret
        sgu_ref[...] = sgu
        mg_ref[...] = merged
        a_ref[...] = a
        b_ref[...] = b

    sq = (D_MODEL, D_MODEL)
    return pl.pallas_call(
        body, name="merge_fwd", grid=(t // TILE_M,),
        in_specs=[_proj_tile(j) for j in (1, 2, 3)] + [_row_tile(D_MODEL)] * 2
        + [_resident(ws.shape), _resident(bst.shape), _resident(sq), _resident(sq), _resident(sq)],
        out_specs=[_row_tile(D_MODEL)] * 6,
        out_shape=[jax.ShapeDtypeStruct((t, D_MODEL), F32)] + [jax.ShapeDtypeStruct((t, D_MODEL), MXU_DTYPE)] * 3
        + [jax.ShapeDtypeStruct((t, D_MODEL), F32)] * 2,
        compiler_params=_params(("arbitrary",)),
    )(pj1, pj2, pj3, ret_raw, x2d, ws, bst, w_ro, w_so, w_o)


def _ple_call(x1, p2d, target, g_ple, g_final, w_pg, w_pp):
    t = x1.shape[0]

    def body(x1_ref, p_ref, t_ref, gp_ref, gf_ref, wpg_ref, wpp_ref,
             dx1_ref, hp_ref, dz_ref, dpp_ref, loss_ref, dgp_ref, dgf_ref):
        @pl.when(pl.program_id(0) == 0)
        def _():
            loss_ref[...] = jnp.zeros_like(loss_ref)
            dgp_ref[...] = jnp.zeros_like(dgp_ref)
            dgf_ref[...] = jnp.zeros_like(dgf_ref)

        x1v = x1_ref[...]
        xn1, r1 = _rms(x1v)
        hp = (xn1 * gp_ref[...]).astype(MXU_DTYPE)
        gate = _sigmoid(jnp.dot(hp, wpg_ref[...], preferred_element_type=F32))
        pp = jnp.dot(p_ref[...].astype(MXU_DTYPE), wpp_ref[...], preferred_element_type=F32)
        x2 = x1v + gate * pp
        xn2, r2 = _rms(x2)
        err = xn2 * gf_ref[...] - t_ref[...]
        loss_ref[...] += (0.5 / D_MODEL) * jnp.sum(jnp.sum(err * err, axis=1, keepdims=True), axis=0, keepdims=True)
        dy = err * (1.0 / D_MODEL)
        dgf_ref[...] += jnp.sum(dy * xn2, axis=0, keepdims=True)
        dx2 = _rms_bwd(dy * gf_ref[...], xn2, r2)
        dpp = (dx2 * gate).astype(MXU_DTYPE)
        dz = (dx2 * pp * gate * (1.0 - gate)).astype(MXU_DTYPE)
        dhp = _mm_nt(dz, wpg_ref[...])
        dgp_ref[...] += jnp.sum(dhp * xn1, axis=0, keepdims=True)
        dx1_ref[...] = dx2 + _rms_bwd(dhp * gp_ref[...], xn1, r1)
        hp_ref[...] = hp
        dz_ref[...] = dz
        dpp_ref[...] = dpp

    vec = _resident((1, D_MODEL))
    return pl.pallas_call(
        body, name="ple_fwd_bwd", grid=(t // TILE_M,),
        in_specs=[_row_tile(D_MODEL), _row_tile(PLE_DIM), _row_tile(D_MODEL), vec, vec,
                  _resident((D_MODEL, D_MODEL)), _resident((PLE_DIM, D_MODEL))],
        out_specs=[_row_tile(D_MODEL)] * 4 + [pl.BlockSpec(LOSS_TILE, lambda i: (0, 0)),
                                              pl.BlockSpec((1, D_MODEL), lambda i: (0, 0)),
                                              pl.BlockSpec((1, D_MODEL), lambda i: (0, 0))],
        out_shape=[jax.ShapeDtypeStruct((t, D_MODEL), F32)] + [jax.ShapeDtypeStruct((t, D_MODEL), MXU_DTYPE)] * 3
        + [jax.ShapeDtypeStruct(LOSS_TILE, F32), jax.ShapeDtypeStruct((1, D_MODEL), F32),
           jax.ShapeDtypeStruct((1, D_MODEL), F32)],
        compiler_params=_params(("arbitrary",)),
    )(x1, p2d, target, g_ple, g_final, w_pg, w_pp)


def _merge_bwd_call(dx1, pj1, pj2, pj3, ret_raw, a, b, ws, bst, w_ro, w_so, w_o):
    t = dx1.shape[0]

    def body(dx1_ref, pj1_ref, pj2_ref, pj3_ref, rr_ref, a_ref, b_ref, ws_ref, bst_ref, wro_ref, wso_ref, wo_ref,
             dpj1_ref, dpj2_ref, dpj3_ref, drr_ref, da_ref, db_ref, dws_ref, dbst_ref):
        @pl.when(pl.program_id(0) == 0)
        def _():
            dws_ref[...] = jnp.zeros_like(dws_ref)
            dbst_ref[...] = jnp.zeros_like(dbst_ref)

        wsc = _causal_ws(ws_ref)
        f = _branch_forward(pj1_ref[...], pj2_ref[...], rr_ref[...], wsc, bst_ref[...])
        pj3 = pj3_ref[...]
        smr, sms = _sigmoid(pj3[:, :D_MODEL]), _sigmoid(pj3[:, D_MODEL:])
        dmerged = _mm_nt(dx1_ref[...], wo_ref[...])
        da = (dmerged * smr).astype(MXU_DTYPE)
        db = (dmerged * sms).astype(MXU_DTYPE)
        dpj3_ref[:, :D_MODEL] = (dmerged * a_ref[...] * smr * (1.0 - smr)).astype(dpj3_ref.dtype)
        dpj3_ref[:, D_MODEL:] = (dmerged * b_ref[...] * sms * (1.0 - sms)).astype(dpj3_ref.dtype)
        da_ref[...] = da
        db_ref[...] = db

        dret = _mm_nt(da, wro_ref[...])
        dpj1_ref[:, :D_MODEL] = (dret * f["rn"] * f["dsil_rg"]).astype(dpj1_ref.dtype)
        drn = dret * f["sil_rg"]
        for h in range(RET_HEADS):
            cols = slice(h * V_DIM, (h + 1) * V_DIM)
            drr_ref[:, cols] = _unit_norm_bwd(drn[:, cols], f["rn"][:, cols], f["rstd_r"][h])

        dsgu = _mm_nt(db, wso_ref[...])
        dpj2_ref[:, D_MODEL:] = (dsgu * f["u"] * f["mixed"] * f["dsil_sg"]).astype(dpj2_ref.dtype)
        tg = dsgu * f["sil_sg"]
        dpj1_ref[:, D_MODEL:] = (tg * f["mixed"] * _gelu_grad(f["su"])).astype(dpj1_ref.dtype)
        dmixed = tg * f["u"]
        rows = []
        for cc in range(TILE_M // CHUNK):
            cols = []
            for g in range(SGU_GROUPS):
                rs, cs = slice(cc * CHUNK, (cc + 1) * CHUNK), slice(g * GROUP_DIM, (g + 1) * GROUP_DIM)
                dm = dmixed[rs, cs]
                cols.append(_mm_tn(wsc[g], dm))
                dws_ref[g] += _mm_nt(dm, f["vn"][rs, cs])
                dbst_ref[:, g:g + 1] += jnp.sum(dm, axis=1, keepdims=True)
            rows.append(jnp.concatenate(cols, axis=1))
        dvn = jnp.concatenate(rows, axis=0)
        dvv = _unit_norm_bwd(dvn, f["vn"], f["rstd_v"])
        dpj2_ref[:, :D_MODEL] = (dvv * _gelu_grad(f["sv"])).astype(dpj2_ref.dtype)

    sq = (D_MODEL, D_MODEL)
    return pl.pallas_call(
        body, name="merge_bwd", grid=(t // TILE_M,),
        in_specs=[_row_tile(D_MODEL)] + [_proj_tile(j) for j in (1, 2, 3)] + [_row_tile(D_MODEL)] * 3
        + [_resident(ws.shape), _resident(bst.shape), _resident(sq), _resident(sq), _resident(sq)],
        out_specs=[_row_tile(SHARD_W)] * 3 + [_row_tile(D_MODEL)] * 3
        + [pl.BlockSpec(ws.shape, lambda i: (0, 0, 0)), pl.BlockSpec(bst.shape, lambda i: (0, 0))],
        out_shape=[jax.ShapeDtypeStruct((t, SHARD_W), MXU_DTYPE)] * 3 + [jax.ShapeDtypeStruct((t, D_MODEL), F32)]
        + [jax.ShapeDtypeStruct((t, D_MODEL), MXU_DTYPE)] * 2
        + [jax.ShapeDtypeStruct(ws.shape, F32), jax.ShapeDtypeStruct(bst.shape, F32)],
        compiler_params=_params(("arbitrary",)),
    )(dx1, pj1, pj2, pj3, ret_raw, a, b, ws, bst, w_ro, w_so, w_o)


def _retention_bwd_call(pj0, drr, states, consts, n_seq, seq, side):
    cos_full, sin_signed, decay_in, zeta, xi = consts
    nb = seq // TILE_M
    cpb = TILE_M // CHUNK

    def body(pj_ref, do_ref, st_ref, cos_ref, sin_ref, d_ref, zeta_ref, xi_ref, dpj_ref, gstate):
        @pl.when(pl.program_id(1) == 0)
        def _():
            gstate[...] = jnp.zeros_like(gstate)

        for cc in reversed(range(cpb)):
            rows = slice(cc * CHUNK, (cc + 1) * CHUNK)
            cs, sn = cos_ref[rows, :], sin_ref[rows, :]
            for h in range(RET_HEADS):
                q, k, v = _qkv(pj_ref, rows, h)
                qt = (q * cs + _rot(q) * sn) * QK_SCALE
                kt = k * cs + _rot(k) * sn
                d_out = do_ref[rows, h * V_DIM:(h + 1) * V_DIM]
                prev = st_ref[cc, h]
                g = gstate[h]
                dec = d_ref[h]
                scores_d = _mm_nt(qt, kt) * dec
                dscores = _mm_nt(d_out, v) * dec
                kz = kt * zeta_ref[h]
                qx = qt * xi_ref[h]
                dv = _mm_tn(scores_d, d_out) + _mm(kz, g)
                dqt = (_mm(dscores, kt) + _mm_nt(d_out, prev) * xi_ref[h]) * QK_SCALE
                dkt = _mm_tn(dscores, qt) + _mm_nt(v, g) * zeta_ref[h]
                gstate[h] = _mm_tn(qx, d_out) + CHUNK_DECAY[h] * g
                dq = dqt * cs + _rot(dqt * sn)
                dk = dkt * cs + _rot(dkt * sn)
                dpj_ref[rows, h * QK_DIM:(h + 1) * QK_DIM] = dq.astype(dpj_ref.dtype)
                dpj_ref[rows, RET_HEADS * QK_DIM + h * QK_DIM:RET_HEADS * QK_DIM + (h + 1) * QK_DIM] = dk.astype(
                    dpj_ref.dtype)
                dpj_ref[rows, 2 * RET_HEADS * QK_DIM + h * V_DIM:2 * RET_HEADS * QK_DIM + (h + 1) * V_DIM] = dv.astype(
                    dpj_ref.dtype)

    row = lambda b, n: (b * nb + nb - 1 - n, 0)
    pos = lambda b, n: (nb - 1 - n, 0)
    return _compute_call(
        body, name="retention_bwd", grid=(n_seq, nb),
        in_specs=[pl.BlockSpec((None, TILE_M, SHARD_W), lambda b, n: (0,) + row(b, n)),
                  pl.BlockSpec((TILE_M, D_MODEL), row),
                  pl.BlockSpec((cpb, RET_HEADS, QK_DIM, V_DIM), lambda b, n: (b * nb + nb - 1 - n, 0, 0, 0)),
                  pl.BlockSpec((TILE_M, QK_DIM), pos), pl.BlockSpec((TILE_M, QK_DIM), pos),
                  _resident(decay_in.shape), _resident(zeta.shape), _resident(xi.shape)],
        out_specs=[pl.BlockSpec((TILE_M, SHARD_W), row)],
        out_shape=[jax.ShapeDtypeStruct((n_seq * seq, SHARD_W), MXU_DTYPE)],
        scratch_shapes=[pltpu.VMEM((RET_HEADS, QK_DIM, V_DIM), F32)],
        operands=(pj0, drr, states, cos_full, sin_signed, decay_in, zeta, xi),
        semantics=("arbitrary", "arbitrary"), side=side)


def _dx_call(dpj, x2d, dx1, g_mixer, wg_in, side):
    t = x2d.shape[0]

    def body(d0, d1, d2, d3, x_ref, dx1_ref, g_ref, w_ref, dx_ref, dg_ref):
        @pl.when(pl.program_id(0) == 0)
        def _():
            dg_ref[...] = jnp.zeros_like(dg_ref)

        dh = _mm_nt(d0[...], w_ref[0])
        for j, d_ref in enumerate((d1, d2, d3)):
            dh += _mm_nt(d_ref[...], w_ref[j + 1])
        xn, r = _rms(x_ref[...])
        dg_ref[...] += jnp.sum(dh * xn, axis=0, keepdims=True)
        dx_ref[...] = dx1_ref[...] + _rms_bwd(dh * g_ref[...], xn, r)

    return _compute_call(
        body, name="dx_bwd", grid=(t // TILE_M,),
        in_specs=[_row_tile(SHARD_W)] * N_CHIPS + [_row_tile(D_MODEL)] * 2
        + [_resident((1, D_MODEL)), _resident(wg_in.shape)],
        out_specs=[_row_tile(D_MODEL), pl.BlockSpec((1, D_MODEL), lambda i: (0, 0))],
        out_shape=[jax.ShapeDtypeStruct((t, D_MODEL), F32), jax.ShapeDtypeStruct((1, D_MODEL), F32)],
        operands=(*dpj, x2d, dx1, g_mixer, wg_in), semantics=("arbitrary",), side=side)


def _wgrad_call(name, lhs, rhs, block_n, out_cols=None, block_t=512, into=None, slot=0, n_slots=1):
    t, n = rhs.shape
    k = lhs.shape[1]
    steps = t // block_t
    out_cols = block_n if out_cols is None else out_cols
    per = block_n // out_cols
    first_block = slot * (n // block_n)

    def body(l_ref, r_ref, *rest):
        o_ref, acc = rest[-2:]
        @pl.when(pl.program_id(1) == 0)
        def _():
            acc[...] = jnp.zeros_like(acc)

        acc[...] += _mm_tn(l_ref[...], r_ref[...])

        @pl.when(pl.program_id(1) == steps - 1)
        def _():
            for s in range(per):
                o_ref[s] = acc[:, s * out_cols:(s + 1) * out_cols].astype(o_ref.dtype)

    return pl.pallas_call(
        body, name=name, grid=(n // block_n, steps),
        in_specs=[pl.BlockSpec((block_t, k), lambda j, i: (i, 0)), pl.BlockSpec((block_t, block_n), lambda j, i: (i, j))]
        + ([] if into is None else [ANY]),
        out_specs=pl.BlockSpec((per, k, out_cols), lambda j, i: (first_block + j, 0, 0)),
        out_shape=jax.ShapeDtypeStruct((n_slots * (n // out_cols), k, out_cols), COMM_DTYPE),
        scratch_shapes=[pltpu.VMEM((k, block_n), F32)],
        input_output_aliases={} if into is None else {2: 0},
        compiler_params=_params(("arbitrary", "arbitrary")),
    )(*((lhs, rhs) if into is None else (lhs, rhs, into)))


def _position():
    return lax.axis_index("x"), lax.axis_index("y"), lax.axis_index("c")


def _position_array():
    x, y, c = _position()
    return jnp.stack([2 * x + y, c]).astype(jnp.int32)


def _other_chip(x, y, k):
    return (1 - x if k & 2 else x), (1 - y if k & 1 else y)


def _plan_side(operands, out_shapes, plan, n_remote, aliases=None):
    def copies(ins, outs, send_sem, recv_sem):
        remote = plan(ins, outs)
        assert len(remote) == n_remote
        return [pltpu.make_async_remote_copy(src_ref=src, dst_ref=dst, send_sem=send_sem.at[i],
                                             recv_sem=recv_sem.at[i], device_id=dev, device_id_type=MESH)
                for i, (src, dst, dev) in enumerate(remote)]

    def start(*a):
        for cp in copies(*a):
            cp.start()

    def finish(*a):
        for cp in copies(*a):
            cp.wait()

    return _Side(operands, out_shapes, n_remote, start, finish, aliases)


def _place_cast_call(name, w, pos):
    rows, width = w.shape
    block_rows = min(256, rows)

    def body(pos_ref, w_ref, o_ref):
        o_ref[...] = w_ref[...].astype(o_ref.dtype)

    return pl.pallas_call(
        body, name=name,
        grid_spec=pltpu.PrefetchScalarGridSpec(
            num_scalar_prefetch=1, grid=(rows // block_rows,),
            in_specs=[pl.BlockSpec((block_rows, width), lambda i, pos: (i, 0))],
            out_specs=pl.BlockSpec((None, block_rows, width), lambda i, pos: (pos[0], i, 0))),
        out_shape=jax.ShapeDtypeStruct((N_CHIPS, rows, width), MXU_DTYPE),
        compiler_params=_params(("arbitrary",)),
    )(pos, w)


def _gather_side(placed):
    n = len(placed)

    def copies(kind, bufs, send_sem, recv_sem):
        x, y, c = _position()
        me = 2 * x + y
        made = []
        for i in range(n):
            hr = placed[i].shape[1] // 2
            for k in (1, 2, 3):
                px, py = _other_chip(x, y, k)
                chip, core, slot, dev = [(me, c, k - 1, (px, py, c)), (2 * px + py, c, 2 + k, (x, y, 1 - c)),
                                         (2 * px + py, 1 - c, 2 + k, (x, y, 1 - c))][kind]
                piece = bufs[i].at[chip, pl.ds(core * hr, hr)]
                made.append(pltpu.make_async_remote_copy(
                    src_ref=piece, dst_ref=piece, send_sem=send_sem.at[6 * i + slot],
                    recv_sem=recv_sem.at[6 * i + slot], device_id=dev, device_id_type=MESH))
        return made

    def start(ins, outs, send_sem, recv_sem):
        for cp in copies(0, outs, send_sem, recv_sem):
            cp.start()

    def finish(ins, outs, send_sem, recv_sem):
        first, onward = copies(0, outs, send_sem, recv_sem), copies(1, outs, send_sem, recv_sem)
        for landed, cp in zip(first, onward):
            landed.wait_recv()
            cp.start()
        for cp in copies(2, outs, send_sem, recv_sem):
            cp.wait_recv()
        for cp in first + onward:
            cp.wait_send()

    return _Side(placed, [jax.ShapeDtypeStruct(a.shape, a.dtype) for a in placed], 6 * n, start, finish,
                 {i: i for i in range(n)})


def _sum_call(name, pos, terms, rows, width, out_dtype, block_rows, out_rows=None, out_index=None):
    out_rows = rows if out_rows is None else out_rows
    out_index = (lambda i, pos, br: i) if out_index is None else out_index

    def body(pos_ref, *refs):
        acc = refs[0][...].astype(F32)
        for r in refs[1:-1]:
            acc = acc + r[...].astype(F32)
        refs[-1][...] = acc.astype(out_dtype)

    def spec(index):
        return pl.BlockSpec((block_rows, width), lambda i, pos: (index(i, pos, block_rows), 0))

    return pl.pallas_call(
        body, name=name,
        grid_spec=pltpu.PrefetchScalarGridSpec(
            num_scalar_prefetch=1, grid=(rows // block_rows,),
            in_specs=[spec(index) for _, index in terms], out_specs=spec(out_index)),
        out_shape=jax.ShapeDtypeStruct((out_rows, width), out_dtype),
        compiler_params=_params(("arbitrary",)),
    )(pos, *[arr for arr, _ in terms])


def _pair_stage(tag, grads, pos):
    n = len(grads)
    shapes = [(g.shape[0] // N_CHIPS, g.shape[1]) for g in grads]

    def plan(ins, outs):
        x, y, c = _position()
        remote = []
        for i, (r, _) in enumerate(shapes):
            hr = r // 2
            for j in range(N_CHIPS):
                remote.append((ins[i].at[pl.ds(j * r + (1 - c) * hr, hr)], outs[i].at[pl.ds(j * hr, hr)],
                               (x, y, 1 - c)))
        return remote

    pair = _exchange_call(f"grad_pair_exchange_{tag}", _plan_side(
        grads, [jax.ShapeDtypeStruct((N_CHIPS * (r // 2), w), COMM_DTYPE) for r, w in shapes], plan, N_CHIPS * n))

    def own_half(r):
        def index(s, pos, br):
            per = (r // 2) // br
            return (s // per) * (r // br) + pos[1] * per + s % per
        return index

    return [_sum_call(f"grad_pair_sum_{tag}{i}", pos, [(grads[i], own_half(r)), (pair[i], lambda s, pos, br: s)],
                      N_CHIPS * (r // 2), w, COMM_DTYPE, min(256, r // 2)) for i, (r, w) in enumerate(shapes)]


def _chip_side(pair_sums):
    halves = [(p.shape[0] // N_CHIPS, p.shape[1]) for p in pair_sums]

    def plan(ins, outs):
        x, y, c = _position()
        remote = []
        for i, (hr, _) in enumerate(halves):
            for k in (1, 2, 3):
                px, py = _other_chip(x, y, k)
                remote.append((ins[i].at[pl.ds((2 * px + py) * hr, hr)], outs[i].at[pl.ds((k - 1) * hr, hr)],
                               (px, py, c)))
        return remote

    return _plan_side(pair_sums, [jax.ShapeDtypeStruct((3 * hr, w), COMM_DTYPE) for hr, w in halves], plan,
                      3 * len(pair_sums))


def _finish_stage(tag, pair_sums, chip, pos):
    n = len(pair_sums)
    halves = [(p.shape[0] // N_CHIPS, p.shape[1]) for p in pair_sums]

    def slab(k, hr):
        return lambda s, pos, br: k * (hr // br) + s

    reduced = []
    for i, (hr, w) in enumerate(halves):
        terms = [(pair_sums[i], lambda s, pos, br, hr=hr: pos[0] * (hr // br) + s)]
        terms += [(chip[i], slab(k, hr)) for k in range(3)]
        reduced.append(_sum_call(f"grad_chip_sum_{tag}{i}", pos, terms, hr, w, F32, min(256, hr), out_rows=2 * hr,
                                 out_index=lambda s, pos, br, hr=hr: pos[1] * (hr // br) + s))

    def plan(ins, outs):
        x, y, c = _position()
        remote = []
        for i, (hr, _) in enumerate(halves):
            mine = outs[i].at[pl.ds(c * hr, hr)]
            remote.append((mine, mine, (x, y, 1 - c)))
        return remote

    return _exchange_call(f"grad_half_exchange_{tag}", _plan_side(
        reduced, [jax.ShapeDtypeStruct((2 * hr, w), F32) for hr, w in halves], plan, n, {i: i for i in range(n)}))


def _gather_small(parts):
    n = len(parts)

    def body(*refs):
        ins, outs = refs[:n], refs[n:2 * n]
        loc_sem, send_sem, recv_sem = refs[2 * n:]
        x, y, c = _position()
        copies = []
        for i in range(n):
            mine = outs[i].at[4 * x + 2 * y + c]
            copies.append(pltpu.make_async_copy(ins[i], mine, loc_sem.at[i]))
            for d in range(1, 8):
                px, py = _other_chip(x, y, d >> 1)
                copies.append(pltpu.make_async_remote_copy(
                    src_ref=ins[i], dst_ref=mine, send_sem=send_sem.at[7 * i + d - 1],
                    recv_sem=recv_sem.at[7 * i + d - 1], device_id=(px, py, 1 - c if d & 1 else c),
                    device_id_type=MESH))
        for cp in copies:
            cp.start()
        for cp in copies:
            cp.wait()

    return pl.pallas_call(
        body, name="small_grad_gather", in_specs=[ANY] * n, out_specs=[ANY] * n,
        out_shape=[jax.ShapeDtypeStruct((8,) + a.shape, F32) for a in parts],
        scratch_shapes=[pltpu.SemaphoreType.DMA((n,)), pltpu.SemaphoreType.DMA((7 * n,)),
                        pltpu.SemaphoreType.DMA((7 * n,))],
        compiler_params=pltpu.CompilerParams(has_side_effects=True),
    )(*parts)


def _adamw(w, g, m, v):
    m = ADAM_B1 * m + (1.0 - ADAM_B1) * g
    v = ADAM_B2 * v + (1.0 - ADAM_B2) * (g * g)
    m_hat = m / (1.0 - ADAM_B1 ** ADAM_STEP)
    v_hat = v / (1.0 - ADAM_B2 ** ADAM_STEP)
    delta = -ADAM_LR * (m_hat / (jnp.sqrt(v_hat) + ADAM_EPS) + ADAM_WD * w)
    return delta, m, v


def _adamw_call(name, w, g, m, v):
    rows, width = w.shape
    block_rows = min(256, rows)

    def body(w_ref, g_ref, m_ref, v_ref, d_out, m_out, v_out):
        d_out[...], m_out[...], v_out[...] = _adamw(w_ref[...], g_ref[...], m_ref[...], v_ref[...])

    spec = pl.BlockSpec((block_rows, width), lambda i: (i, 0))
    return pl.pallas_call(
        body, name=name, grid=(rows // block_rows,), in_specs=[spec] * 4, out_specs=[spec] * 3,
        out_shape=[jax.ShapeDtypeStruct(w.shape, F32)] * 3,
        compiler_params=_params(("arbitrary",)),
    )(w, g, m, v)


def _small_adamw_call(gathered, weights, moments_m, moments_v):
    n = len(weights)

    def body(*refs):
        all_refs, refs = refs[:n + 1], refs[n + 1:]
        w_refs, m_refs, v_refs, outs = refs[:n], refs[n:2 * n], refs[2 * n:3 * n], refs[3 * n:]

        def total(ref):
            acc = ref[0]
            for d in range(1, 8):
                acc = acc + ref[d]
            return acc

        outs[0][...] = total(all_refs[n])
        for i in range(n):
            g = total(all_refs[i])
            if i == 0:
                row = lax.broadcasted_iota(jnp.int32, g.shape, 0)
                col = lax.broadcasted_iota(jnp.int32, g.shape, 1)
                g = jnp.where((row % CHUNK) >= col, g, 0.0)
            g_out, d_out, m_out, v_out = outs[1 + 4 * i:5 + 4 * i]
            g_out[...] = g
            d_out[...], m_out[...], v_out[...] = _adamw(w_refs[i][...], g, m_refs[i][...], v_refs[i][...])

    out_shape = [jax.ShapeDtypeStruct(LOSS_TILE, F32)]
    for w in weights:
        out_shape += [jax.ShapeDtypeStruct(w.shape, F32)] * 4
    res = pl.pallas_call(
        body, name="small_adamw", out_shape=out_shape,
        compiler_params=pltpu.CompilerParams(vmem_limit_bytes=VMEM_LIMIT),
    )(*gathered, *weights, *moments_m, *moments_v)
    return res[0], [res[1 + 4 * i:5 + 4 * i] for i in range(n)]


def kernel(x, p, w_in, w_ret_out, w_sgu_out, w_out, sgu_ws, sgu_bs, w_ple_gate, w_ple_proj, g_mixer, g_ple, g_final, loss_target, m_w_in, m_w_ret_out, m_w_sgu_out, m_w_out, m_sgu_ws, m_sgu_bs, m_w_ple_gate, m_w_ple_proj, m_g_mixer, m_g_ple, m_g_final, v_w_in, v_w_ret_out, v_w_sgu_out, v_w_out, v_sgu_ws, v_sgu_bs, v_w_ple_gate, v_w_ple_proj, v_g_mixer, v_g_ple, v_g_final):
    n_seq, seq, _ = x.shape
    t = n_seq * seq
    x2d = x.reshape(t, D_MODEL)
    p2d = p.reshape(t, PLE_DIM)
    target = loss_target.reshape(t, D_MODEL)
    big = [w_in[0], w_ret_out[0], w_sgu_out[0], w_out[0], w_ple_gate[0], w_ple_proj[0]]
    big_m = [m_w_in[0], m_w_ret_out[0], m_w_sgu_out[0], m_w_out[0], m_w_ple_gate[0], m_w_ple_proj[0]]
    big_v = [v_w_in[0], v_w_ret_out[0], v_w_sgu_out[0], v_w_out[0], v_w_ple_gate[0], v_w_ple_proj[0]]

    pos = _position_array()
    placed = [_place_cast_call(f"place_weight_{i}", w, pos) for i, w in enumerate(big)]
    ws = sgu_ws[0]
    bst = sgu_bs[0].T
    consts = _retention_consts(seq)

    (h, pj, wg_in), gathered = _proj_call(x2d, g_mixer, placed[0], pos, _gather_side(placed[1:]))
    pj0 = pj1 = pj2 = pj3 = pj
    w_ro, w_so, w_o, w_pg = (w.reshape(D_MODEL, D_MODEL) for w in gathered[:4])
    w_pp = gathered[4].transpose(1, 0, 2).reshape(PLE_DIM, D_MODEL)
    ret_raw, states = _retention_fwd_call(pj0, consts, n_seq, seq)
    x1, ret, sgu, merged, a, b = _merge_fwd_call(pj1, pj2, pj3, ret_raw, x2d, ws, bst, w_ro, w_so, w_o)
    dx1, hp, dz, dpp, loss, dg_ple, dg_final = _ple_call(x1, p2d, target, g_ple, g_final.reshape(1, D_MODEL),
                                                         w_pg, w_pp)
    dpj1, dpj2, dpj3, drr, da, db, dws, dbst = _merge_bwd_call(dx1, pj1, pj2, pj3, ret_raw, a, b, ws, bst,
                                                               w_ro, w_so, w_o)

    rows_of = lambda g: g.reshape(g.shape[0] * g.shape[1], g.shape[2])
    tail_grads = [
        rows_of(_wgrad_call("wgrad_ret_out", ret, da, D_MODEL)),
        rows_of(_wgrad_call("wgrad_sgu_out", sgu, db, D_MODEL)),
        rows_of(_wgrad_call("wgrad_out", merged, dx1, D_MODEL)),
        rows_of(_wgrad_call("wgrad_ple_gate", hp, dz, D_MODEL)),
        rows_of(_wgrad_call("wgrad_ple_proj", p2d, dpp, D_MODEL, out_cols=PLE_DIM)),
    ]
    tail_sums = _pair_stage("tail", tail_grads, pos)
    (dpj0,), tail_chip = _retention_bwd_call(pj0, drr, states, consts, n_seq, seq, _chip_side(tail_sums))
    in_grad = None
    for j, d in enumerate((dpj0, dpj1, dpj2, dpj3)):
        in_grad = _wgrad_call(f"wgrad_in_{j}", h, d, SHARD_W, into=in_grad, slot=j, n_slots=N_CHIPS)
    in_sums = _pair_stage("in", [rows_of(in_grad)], pos)
    (dx, dg_mixer), in_chip = _dx_call((dpj0, dpj1, dpj2, dpj3), x2d, dx1, g_mixer, wg_in, _chip_side(in_sums))
    g_big = _finish_stage("in", in_sums, in_chip, pos) + _finish_stage("tail", tail_sums, tail_chip, pos)
    upd = [_adamw_call(f"adamw_{i}", big[i], g_big[i], big_m[i], big_v[i]) for i in range(len(big))]

    small_shapes = [(SGU_GROUPS * CHUNK, CHUNK), (SGU_GROUPS, CHUNK), (1, D_MODEL), (1, D_MODEL), (1, D_MODEL)]
    as_small = lambda arrays: [a_.reshape(s) for a_, s in zip(arrays, small_shapes)]
    small_g = _gather_small(as_small([dws, dbst.T, dg_mixer, dg_ple, dg_final]) + [loss])
    total, small = _small_adamw_call(small_g, as_small([sgu_ws, sgu_bs, g_mixer, g_ple, g_final]),
                                     as_small([m_sgu_ws, m_sgu_bs, m_g_mixer, m_g_ple, m_g_final]),
                                     as_small([v_sgu_ws, v_sgu_bs, v_g_mixer, v_g_ple, v_g_final]))
    out_small_shapes = [sgu_ws.shape, sgu_bs.shape, g_mixer.shape, g_ple.shape, g_final.shape]

    def ordered(big_list, kind):
        w_in_, w_ro_, w_so_, w_o_, w_pg_, w_pp_ = [b_[None] for b_ in big_list]
        s_ws, s_bs, s_gm, s_gp, s_gf = [small[i][kind].reshape(s) for i, s in enumerate(out_small_shapes)]
        return [w_in_, w_ro_, w_so_, w_o_, s_ws, s_bs, w_pg_, w_pp_, s_gm, s_gp, s_gf]

    out = [total[0, 0], dx.reshape(x.shape)]
    out += ordered(g_big, 0)
    out += ordered([u[0] for u in upd], 1)
    out += ordered([u[1] for u in upd], 2)
    out += ordered([u[2] for u in upd], 3)
    return tuple(out)
```

```python
import functools
import math

import numpy as np
import jax
import jax.numpy as jnp
from jax import lax
from jax.experimental import pallas as pl
from jax.experimental.pallas import tpu as pltpu

F32 = jnp.float32
MXU_DTYPE = jnp.bfloat16
COMM_DTYPE = jnp.bfloat16

D_MODEL = 1024
RET_HEADS = 4
QK_DIM = 128
V_DIM = 256
CHUNK = 128
SGU_GROUPS = 4
GROUP_DIM = 256
PLE_DIM = 256
N_CHIPS = 4
SHARD_W = 2048
ROPE_BASE = 10000.0
NORM_EPS = 1e-6
GN_EPS = 1e-5
QK_SCALE = QK_DIM ** -0.5
SQRT_HALF = math.sqrt(0.5)
INV_SQRT_2PI = 1.0 / math.sqrt(2.0 * math.pi)

ADAM_LR = 0.001
ADAM_B1 = 0.9
ADAM_B2 = 0.999
ADAM_EPS = 1e-08
ADAM_WD = 0.01
ADAM_STEP = 10

TILE_M = 256
LOSS_TILE = (8, 128)
LOAD_PARTS = 16
PROJ_TILE = 512
VMEM_LIMIT = 56 * 1024 * 1024
MESH = pl.DeviceIdType.MESH
ANY = pl.BlockSpec(memory_space=pl.ANY)

CHUNK_DECAY = tuple(
    float(np.exp(np.float32(CHUNK) * np.log(np.float32(1.0 - 2.0 ** (-5.0 - h))))) for h in range(RET_HEADS))


def _mm(a, b):
    return jnp.dot(a.astype(MXU_DTYPE), b.astype(MXU_DTYPE), preferred_element_type=F32)


def _mm_nt(a, b):
    return lax.dot_general(a.astype(MXU_DTYPE), b.astype(MXU_DTYPE), (((1,), (1,)), ((), ())),
                           preferred_element_type=F32)


def _mm_tn(a, b):
    return lax.dot_general(a.astype(MXU_DTYPE), b.astype(MXU_DTYPE), (((0,), (0,)), ((), ())),
                           preferred_element_type=F32)


def _mean(x):
    return jnp.mean(x, axis=-1, keepdims=True)


def _sigmoid(x):
    return jax.nn.sigmoid(x)


def _silu_and_grad(x):
    s = _sigmoid(x)
    return x * s, s * (1.0 + x * (1.0 - s))


def _gelu(x):
    return 0.5 * x * (1.0 + lax.erf(x * SQRT_HALF))


def _gelu_grad(x):
    return 0.5 * (1.0 + lax.erf(x * SQRT_HALF)) + x * jnp.exp(-0.5 * x * x) * INV_SQRT_2PI


def _unit_norm(x, eps):
    xc = x - _mean(x)
    rstd = lax.rsqrt(_mean(xc * xc) + eps)
    return xc * rstd, rstd


def _unit_norm_bwd(dn, n, rstd):
    return rstd * (dn - _mean(dn) - n * _mean(dn * n))


def _rms(x):
    r = lax.rsqrt(_mean(x * x) + NORM_EPS)
    return x * r, r


def _rms_bwd(dxn, xn, r):
    return r * (dxn - xn * _mean(dxn * xn))


def _rot(x):
    return pltpu.roll(x, QK_DIM // 2, 1)


def _params(semantics, **kw):
    return pltpu.CompilerParams(dimension_semantics=semantics, vmem_limit_bytes=VMEM_LIMIT, **kw)


def _row_tile(width, tm=TILE_M):
    return pl.BlockSpec((tm, width), lambda i: (i, 0))


def _proj_tile(j):
    return pl.BlockSpec((None, TILE_M, SHARD_W), lambda i: (j, i, 0))


def _resident(shape):
    nd = len(shape)
    return pl.BlockSpec(shape, lambda *_: (0,) * nd, pipeline_mode=pl.Buffered(1))


def _causal_ws(ws_ref):
    row = lax.broadcasted_iota(jnp.int32, (CHUNK, CHUNK), 0)
    col = lax.broadcasted_iota(jnp.int32, (CHUNK, CHUNK), 1)
    return [jnp.where(row >= col, ws_ref[g], 0.0).astype(MXU_DTYPE) for g in range(SGU_GROUPS)]


def _heads(x, width):
    return [x[:, h * width:(h + 1) * width] for h in range(x.shape[1] // width)]


def _branch_forward(pj1, pj2, ret_raw, wsc, bst):
    rg, su = pj1[:, :D_MODEL], pj1[:, D_MODEL:]
    sv, sg = pj2[:, :D_MODEL], pj2[:, D_MODEL:]
    rn_parts, rstd_parts = zip(*[_unit_norm(r, GN_EPS) for r in _heads(ret_raw, V_DIM)])
    rn = jnp.concatenate(rn_parts, axis=1)
    sil_rg, dsil_rg = _silu_and_grad(rg)
    ret = rn * sil_rg
    u = _gelu(su)
    vn, rstd_v = _unit_norm(_gelu(sv), GN_EPS)
    rows = []
    for cc in range(pj1.shape[0] // CHUNK):
        cols = []
        for g in range(SGU_GROUPS):
            blk = vn[cc * CHUNK:(cc + 1) * CHUNK, g * GROUP_DIM:(g + 1) * GROUP_DIM]
            cols.append(_mm(wsc[g], blk) + bst[:, g:g + 1])
        rows.append(jnp.concatenate(cols, axis=1))
    mixed = jnp.concatenate(rows, axis=0)
    sil_sg, dsil_sg = _silu_and_grad(sg)
    sgu = u * mixed * sil_sg
    return dict(rg=rg, su=su, sv=sv, sg=sg, rn=rn, rstd_r=rstd_parts, sil_rg=sil_rg, dsil_rg=dsil_rg, ret=ret,
                u=u, vn=vn, rstd_v=rstd_v, mixed=mixed, sil_sg=sil_sg, dsil_sg=dsil_sg, sgu=sgu)


class _Side:
    def __init__(self, operands, out_shapes, n_sems, start, finish, aliases=None):
        self.operands, self.out_shapes, self.n_sems = list(operands), list(out_shapes), n_sems
        self.start, self.finish, self.aliases = start, finish, dict(aliases or {})


def _compute_call(body, *, name, grid, in_specs, out_specs, out_shape, operands, semantics, scratch_shapes=(),
                  side=None, prefetch=None, aliases=None, side_start=None):
    n_pre = 0 if prefetch is None else 1
    pre = () if prefetch is None else (prefetch,)

    def spec(in_specs, out_specs, scratch):
        return pltpu.PrefetchScalarGridSpec(num_scalar_prefetch=n_pre, grid=grid, in_specs=in_specs,
                                            out_specs=out_specs, scratch_shapes=scratch)

    if side is None:
        return pl.pallas_call(body, name=name, grid_spec=spec(in_specs, out_specs, list(scratch_shapes)),
                              out_shape=out_shape,
                              input_output_aliases={n_pre + a: b for a, b in (aliases or {}).items()},
                              compiler_params=_params(semantics))(*pre, *operands)
    n_in, n_out, s_in, s_out = len(operands), len(out_shape), len(side.operands), len(side.out_shapes)

    def carrier(*refs):
        pre_refs, refs = refs[:n_pre], refs[n_pre:]
        ins, refs = refs[:n_in], refs[n_in:]
        side_ins, refs = refs[:s_in], refs[s_in:]
        outs, refs = refs[:n_out], refs[n_out:]
        side_outs, refs = refs[:s_out], refs[s_out:]
        scratch, (send_sem, recv_sem) = refs[:-2], refs[-2:]
        ids = [pl.program_id(a) for a in range(len(grid))]
        at = (0,) * len(grid) if side_start is None else side_start
        first = functools.reduce(jnp.logical_and, [i == a for i, a in zip(ids, at)])
        last = functools.reduce(jnp.logical_and, [i == g - 1 for i, g in zip(ids, grid)])

        @pl.when(first)
        def _():
            side.start(side_ins, side_outs, send_sem, recv_sem)

        body(*pre_refs, *ins, *outs, *scratch)

        @pl.when(last)
        def _():
            side.finish(side_ins, side_outs, send_sem, recv_sem)

    all_aliases = {n_pre + a: b for a, b in (aliases or {}).items()}
    all_aliases.update({n_pre + n_in + a: n_out + b for a, b in side.aliases.items()})
    res = pl.pallas_call(
        carrier, name=name,
        grid_spec=spec(list(in_specs) + [ANY] * s_in, list(out_specs) + [ANY] * s_out,
                       list(scratch_shapes) + [pltpu.SemaphoreType.DMA((side.n_sems,))] * 2),
        out_shape=list(out_shape) + side.out_shapes, input_output_aliases=all_aliases,
        compiler_params=_params(semantics, has_side_effects=True),
    )(*pre, *operands, *side.operands)
    return res[:n_out], res[n_out:]


def _exchange_call(name, side):
    s_in = len(side.operands)

    def body(*refs):
        ins, outs = refs[:s_in], refs[s_in:s_in + len(side.out_shapes)]
        send_sem, recv_sem = refs[s_in + len(side.out_shapes):]
        side.start(ins, outs, send_sem, recv_sem)
        side.finish(ins, outs, send_sem, recv_sem)

    return pl.pallas_call(
        body, name=name, in_specs=[ANY] * s_in, out_specs=[ANY] * len(side.out_shapes), out_shape=side.out_shapes,
        scratch_shapes=[pltpu.SemaphoreType.DMA((side.n_sems,))] * 2, input_output_aliases=side.aliases,
        compiler_params=pltpu.CompilerParams(has_side_effects=True),
    )(*side.operands)


def _proj_call(x2d, g_mixer, placed_in, pos, side):
    t = x2d.shape[0]
    nt = t // PROJ_TILE
    hr = placed_in.shape[1] // 2

    def body(pos_ref, x_ref, g_ref, win_ref, h_ref, pj_ref, w_ref, w_vmem, h_all, loc_sem, send_sem, recv_sem):
        k, i = pl.program_id(0), pl.program_id(1)
        x, y, c = _position()
        me = 2 * x + y

        def copy(slot, chip, core, dev):
            piece = w_ref.at[chip, pl.ds(core * hr, hr)]
            return pltpu.make_async_remote_copy(src_ref=piece, dst_ref=piece, send_sem=send_sem.at[slot],
                                                recv_sem=recv_sem.at[slot], device_id=dev, device_id_type=MESH)

        def first_hop(kk):
            px, py = _other_chip(x, y, kk)
            return copy(kk - 1, me, c, (px, py, c))

        def relay():
            source = jnp.bitwise_xor(me, 2 - c)
            return copy(2, source, c, (jnp.bitwise_xor(x, c), jnp.bitwise_xor(y, 1 - c), c))

        def onward(kk, core):
            px, py = _other_chip(x, y, kk)
            return copy(2 + kk, 2 * px + py, core, (x, y, 1 - c))

        def loads(kk):
            px, py = _other_chip(x, y, kk)
            rows = w_vmem.shape[1] // LOAD_PARTS
            return [pltpu.make_async_copy(w_ref.at[2 * px + py, pl.ds(r * rows, rows)],
                                          w_vmem.at[kk % 2, pl.ds(r * rows, rows)], loc_sem.at[r])
                    for r in range(LOAD_PARTS)]

        @pl.when(jnp.logical_and(k == 0, i == 0))
        def _():
            for kk in (1, 2):
                first_hop(kk).start()
            for cp in loads(0):
                cp.start()
            for cp in loads(0):
                cp.wait()

        for kk in (1, 2, 3):
            @pl.when(jnp.logical_and(k == kk - 1, i == nt - 1))
            def _(kk=kk):
                if kk == 1:
                    first_hop(1).wait_recv()
                    first_hop(2).wait_recv()
                    relay().start()
                if kk == 3:
                    relay().wait_recv()
                onward(kk, c).start()
                onward(kk, 1 - c).wait_recv()
                for cp in loads(kk):
                    cp.start()

            @pl.when(jnp.logical_and(k == kk, i == 0))
            def _(kk=kk):
                for cp in loads(kk):
                    cp.wait()

        rows = pl.ds(pl.multiple_of(i * PROJ_TILE, PROJ_TILE), PROJ_TILE)

        @pl.when(k == 0)
        def _():
            xn, _ = _rms(x_ref[...])
            h = (xn * g_ref[...]).astype(MXU_DTYPE)
            h_all[rows, :] = h
            h_ref[...] = h

        pj_ref[...] = jnp.dot(h_all[rows, :], w_vmem[k % 2], preferred_element_type=F32)

        @pl.when(jnp.logical_and(k == N_CHIPS - 1, i == nt - 1))
        def _():
            for cp in [first_hop(1), first_hop(2), relay()] + [onward(kk, c) for kk in (1, 2, 3)]:
                cp.wait_send()

    parked = lambda k, i, pos: (jnp.where(k == 0, i, nt - 1), 0)
    return _compute_call(
        body, name="proj_fwd", grid=(N_CHIPS, nt),
        in_specs=[pl.BlockSpec((PROJ_TILE, D_MODEL), parked), pl.BlockSpec((1, D_MODEL), lambda k, i, pos: (0, 0)), ANY],
        out_specs=[pl.BlockSpec((PROJ_TILE, D_MODEL), lambda k, i, pos: (jnp.where(k == 0, i, nt), 0)),
                   pl.BlockSpec((None, PROJ_TILE, SHARD_W), lambda k, i, pos: (jnp.bitwise_xor(pos[0], k), i, 0)),
                   ANY],
        out_shape=[jax.ShapeDtypeStruct((t + PROJ_TILE, D_MODEL), MXU_DTYPE),
                   jax.ShapeDtypeStruct((N_CHIPS, t, SHARD_W), F32),
                   jax.ShapeDtypeStruct(placed_in.shape, placed_in.dtype)],
        scratch_shapes=[pltpu.VMEM((2,) + placed_in.shape[1:], placed_in.dtype), pltpu.VMEM((t, D_MODEL), MXU_DTYPE),
                        pltpu.SemaphoreType.DMA((LOAD_PARTS,)), pltpu.SemaphoreType.DMA((6,)),
                        pltpu.SemaphoreType.DMA((6,))],
        operands=(x2d, g_mixer, placed_in), semantics=("arbitrary", "arbitrary"), side=side, prefetch=pos,
        aliases={2: 2}, side_start=(2, 0))


def _retention_consts(seq):
    half = QK_DIM // 2
    inv = ROPE_BASE ** (-jnp.arange(half, dtype=F32) / half)
    ang = jnp.arange(seq, dtype=F32)[:, None] * inv[None, :]
    cos, sin = jnp.cos(ang), jnp.sin(ang)
    cos_full = jnp.concatenate([cos, cos], axis=1)
    sin_signed = jnp.concatenate([-sin, sin], axis=1)
    log_g = jnp.log(1.0 - 2.0 ** (-5.0 - jnp.arange(RET_HEADS, dtype=F32)))
    idx = jnp.arange(CHUNK, dtype=F32)
    diff = idx[:, None] - idx[None, :]
    decay_in = jnp.where(diff[None] >= 0, jnp.exp(jnp.maximum(diff, 0.0)[None] * log_g[:, None, None]), 0.0)
    zeta = jnp.exp((CHUNK - 1.0 - idx)[None, :] * log_g[:, None])
    xi = jnp.exp((idx + 1.0)[None, :] * log_g[:, None])
    zeta = jnp.broadcast_to(zeta[:, :, None], (RET_HEADS, CHUNK, QK_DIM))
    xi = jnp.broadcast_to(xi[:, :, None], (RET_HEADS, CHUNK, QK_DIM))
    return cos_full, sin_signed, decay_in, zeta, xi


def _qkv(pj_ref, rows, h):
    q = pj_ref[rows, h * QK_DIM:(h + 1) * QK_DIM]
    k = pj_ref[rows, RET_HEADS * QK_DIM + h * QK_DIM:RET_HEADS * QK_DIM + (h + 1) * QK_DIM]
    v = pj_ref[rows, 2 * RET_HEADS * QK_DIM + h * V_DIM:2 * RET_HEADS * QK_DIM + (h + 1) * V_DIM]
    return q, k, v


def _retention_fwd_call(pj0, consts, n_seq, seq):
    cos_full, sin_signed, decay_in, zeta, xi = consts
    nb = seq // TILE_M
    cpb = TILE_M // CHUNK

    def body(pj_ref, cos_ref, sin_ref, d_ref, zeta_ref, xi_ref, o_ref, st_ref, state):
        @pl.when(pl.program_id(1) == 0)
        def _():
            state[...] = jnp.zeros_like(state)

        for cc in range(cpb):
            rows = slice(cc * CHUNK, (cc + 1) * CHUNK)
            cs, sn = cos_ref[rows, :], sin_ref[rows, :]
            for h in range(RET_HEADS):
                q, k, v = _qkv(pj_ref, rows, h)
                qt = (q * cs + _rot(q) * sn) * QK_SCALE
                kt = k * cs + _rot(k) * sn
                prev = state[h]
                st_ref[cc, h] = prev.astype(st_ref.dtype)
                scores = _mm_nt(qt, kt) * d_ref[h]
                o_ref[rows, h * V_DIM:(h + 1) * V_DIM] = _mm(scores, v) + _mm(qt * xi_ref[h], prev)
                state[h] = _mm_tn(kt * zeta_ref[h], v) + CHUNK_DECAY[h] * prev

    row = lambda b, n: (b * nb + n, 0)
    pos = lambda b, n: (n, 0)
    return pl.pallas_call(
        body, name="retention_fwd", grid=(n_seq, nb),
        in_specs=[pl.BlockSpec((None, TILE_M, SHARD_W), lambda b, n: (0,) + row(b, n)),
                  pl.BlockSpec((TILE_M, QK_DIM), pos),
                  pl.BlockSpec((TILE_M, QK_DIM), pos), _resident(decay_in.shape), _resident(zeta.shape),
                  _resident(xi.shape)],
        out_specs=[pl.BlockSpec((TILE_M, D_MODEL), row),
                   pl.BlockSpec((cpb, RET_HEADS, QK_DIM, V_DIM), lambda b, n: (b * nb + n, 0, 0, 0))],
        out_shape=[jax.ShapeDtypeStruct((n_seq * seq, D_MODEL), F32),
                   jax.ShapeDtypeStruct((n_seq * seq // CHUNK, RET_HEADS, QK_DIM, V_DIM), MXU_DTYPE)],
        scratch_shapes=[pltpu.VMEM((RET_HEADS, QK_DIM, V_DIM), F32)],
        compiler_params=_params(("arbitrary", "arbitrary")),
    )(pj0, cos_full, sin_signed, decay_in, zeta, xi)


def _merge_fwd_call(pj1, pj2, pj3, ret_raw, x2d, ws, bst, w_ro, w_so, w_o):
    t = x2d.shape[0]

    def body(pj1_ref, pj2_ref, pj3_ref, rr_ref, x_ref, ws_ref, bst_ref, wro_ref, wso_ref, wo_ref,
             x1_ref, ret_ref, sgu_ref, mg_ref, a_ref, b_ref):
        f = _branch_forward(pj1_ref[...], pj2_ref[...], rr_ref[...], _causal_ws(ws_ref), bst_ref[...])
        ret = f["ret"].astype(MXU_DTYPE)
        sgu = f["sgu"].astype(MXU_DTYPE)
        a = jnp.dot(ret, wro_ref[...], preferred_element_type=F32)
        b = jnp.dot(sgu, wso_ref[...], preferred_element_type=F32)
        pj3 = pj3_ref[...]
        merged = (_sigmoid(pj3[:, :D_MODEL]) * a + _sigmoid(pj3[:, D_MODEL:]) * b).astype(MXU_DTYPE)
        x1_ref[...] = x_ref[...] + jnp.dot(merged, wo_ref[...], preferred_element_type=F32)
        ret_ref[...] = ret
        sgu_ref[...] = sgu
        mg_ref[...] = merged
        a_ref[...] = a
        b_ref[...] = b

    sq = (D_MODEL, D_MODEL)
    return pl.pallas_call(
        body, name="merge_fwd", grid=(t // TILE_M,),
        in_specs=[_proj_tile(j) for j in (1, 2, 3)] + [_row_tile(D_MODEL)] * 2
        + [_resident(ws.shape), _resident(bst.shape), _resident(sq), _resident(sq), _resident(sq)],
        out_specs=[_row_tile(D_MODEL)] * 6,
        out_shape=[jax.ShapeDtypeStruct((t, D_MODEL), F32)] + [jax.ShapeDtypeStruct((t, D_MODEL), MXU_DTYPE)] * 3
        + [jax.ShapeDtypeStruct((t, D_MODEL), F32)] * 2,
        compiler_params=_params(("arbitrary",)),
    )(pj1, pj2, pj3, ret_raw, x2d, ws, bst, w_ro, w_so, w_o)


def _ple_call(x1, p2d, target, g_ple, g_final, w_pg, w_pp):
    t = x1.shape[0]

    def body(x1_ref, p_ref, t_ref, gp_ref, gf_ref, wpg_ref, wpp_ref,
             dx1_ref, hp_ref, dz_ref, dpp_ref, loss_ref, dgp_ref, dgf_ref):
        @pl.when(pl.program_id(0) == 0)
        def _():
            loss_ref[...] = jnp.zeros_like(loss_ref)
            dgp_ref[...] = jnp.zeros_like(dgp_ref)
            dgf_ref[...] = jnp.zeros_like(dgf_ref)

        x1v = x1_ref[...]
        xn1, r1 = _rms(x1v)
        hp = (xn1 * gp_ref[...]).astype(MXU_DTYPE)
        gate = _sigmoid(jnp.dot(hp, wpg_ref[...], preferred_element_type=F32))
        pp = jnp.dot(p_ref[...].astype(MXU_DTYPE), wpp_ref[...], preferred_element_type=F32)
        x2 = x1v + gate * pp
        xn2, r2 = _rms(x2)
        err = xn2 * gf_ref[...] - t_ref[...]
        loss_ref[...] += (0.5 / D_MODEL) * jnp.sum(jnp.sum(err * err, axis=1, keepdims=True), axis=0, keepdims=True)
        dy = err * (1.0 / D_MODEL)
        dgf_ref[...] += jnp.sum(dy * xn2, axis=0, keepdims=True)
        dx2 = _rms_bwd(dy * gf_ref[...], xn2, r2)
        dpp = (dx2 * gate).astype(MXU_DTYPE)
        dz = (dx2 * pp * gate * (1.0 - gate)).astype(MXU_DTYPE)
        dhp = _mm_nt(dz, wpg_ref[...])
        dgp_ref[...] += jnp.sum(dhp * xn1, axis=0, keepdims=True)
        dx1_ref[...] = dx2 + _rms_bwd(dhp * gp_ref[...], xn1, r1)
        hp_ref[...] = hp
        dz_ref[...] = dz
        dpp_ref[...] = dpp

    vec = _resident((1, D_MODEL))
    return pl.pallas_call(
        body, name="ple_fwd_bwd", grid=(t // TILE_M,),
        in_specs=[_row_tile(D_MODEL), _row_tile(PLE_DIM), _row_tile(D_MODEL), vec, vec,
                  _resident((D_MODEL, D_MODEL)), _resident((PLE_DIM, D_MODEL))],
        out_specs=[_row_tile(D_MODEL)] * 4 + [pl.BlockSpec(LOSS_TILE, lambda i: (0, 0)),
                                              pl.BlockSpec((1, D_MODEL), lambda i: (0, 0)),
                                              pl.BlockSpec((1, D_MODEL), lambda i: (0, 0))],
        out_shape=[jax.ShapeDtypeStruct((t, D_MODEL), F32)] + [jax.ShapeDtypeStruct((t, D_MODEL), MXU_DTYPE)] * 3
        + [jax.ShapeDtypeStruct(LOSS_TILE, F32), jax.ShapeDtypeStruct((1, D_MODEL), F32),
           jax.ShapeDtypeStruct((1, D_MODEL), F32)],
        compiler_params=_params(("arbitrary",)),
    )(x1, p2d, target, g_ple, g_final, w_pg, w_pp)


def _merge_bwd_call(dx1, pj1, pj2, pj3, ret_raw, a, b, ws, bst, w_ro, w_so, w_o):
    t = dx1.shape[0]

    def body(dx1_ref, pj1_ref, pj2_ref, pj3_ref, rr_ref, a_ref, b_ref, ws_ref, bst_ref, wro_ref, wso_ref, wo_ref,
             dpj1_ref, dpj2_ref, dpj3_ref, drr_ref, da_ref, db_ref, dws_ref, dbst_ref):
        @pl.when(pl.program_id(0) == 0)
        def _():
            dws_ref[...] = jnp.zeros_like(dws_ref)
            dbst_ref[...] = jnp.zeros_like(dbst_ref)

        wsc = _causal_ws(ws_ref)
        f = _branch_forward(pj1_ref[...], pj2_ref[...], rr_ref[...], wsc, bst_ref[...])
        pj3 = pj3_ref[...]
        smr, sms = _sigmoid(pj3[:, :D_MODEL]), _sigmoid(pj3[:, D_MODEL:])
        dmerged = _mm_nt(dx1_ref[...], wo_ref[...])
        da = (dmerged * smr).astype(MXU_DTYPE)
        db = (dmerged * sms).astype(MXU_DTYPE)
        dpj3_ref[:, :D_MODEL] = (dmerged * a_ref[...] * smr * (1.0 - smr)).astype(dpj3_ref.dtype)
        dpj3_ref[:, D_MODEL:] = (dmerged * b_ref[...] * sms * (1.0 - sms)).astype(dpj3_ref.dtype)
        da_ref[...] = da
        db_ref[...] = db

        dret = _mm_nt(da, wro_ref[...])
        dpj1_ref[:, :D_MODEL] = (dret * f["rn"] * f["dsil_rg"]).astype(dpj1_ref.dtype)
        drn = dret * f["sil_rg"]
        for h in range(RET_HEADS):
            cols = slice(h * V_DIM, (h + 1) * V_DIM)
            drr_ref[:, cols] = _unit_norm_bwd(drn[:, cols], f["rn"][:, cols], f["rstd_r"][h])

        dsgu = _mm_nt(db, wso_ref[...])
        dpj2_ref[:, D_MODEL:] = (dsgu * f["u"] * f["mixed"] * f["dsil_sg"]).astype(dpj2_ref.dtype)
        tg = dsgu * f["sil_sg"]
        dpj1_ref[:, D_MODEL:] = (tg * f["mixed"] * _gelu_grad(f["su"])).astype(dpj1_ref.dtype)
        dmixed = tg * f["u"]
        rows = []
        for cc in range(TILE_M // CHUNK):
            cols = []
            for g in range(SGU_GROUPS):
                rs, cs = slice(cc * CHUNK, (cc + 1) * CHUNK), slice(g * GROUP_DIM, (g + 1) * GROUP_DIM)
                dm = dmixed[rs, cs]
                cols.append(_mm_tn(wsc[g], dm))
                dws_ref[g] += _mm_nt(dm, f["vn"][rs, cs])
                dbst_ref[:, g:g + 1] += jnp.sum(dm, axis=1, keepdims=True)
            rows.append(jnp.concatenate(cols, axis=1))
        dvn = jnp.concatenate(rows, axis=0)
        dvv = _unit_norm_bwd(dvn, f["vn"], f["rstd_v"])
        dpj2_ref[:, :D_MODEL] = (dvv * _gelu_grad(f["sv"])).astype(dpj2_ref.dtype)

    sq = (D_MODEL, D_MODEL)
    return pl.pallas_call(
        body, name="merge_bwd", grid=(t // TILE_M,),
        in_specs=[_row_tile(D_MODEL)] + [_proj_tile(j) for j in (1, 2, 3)] + [_row_tile(D_MODEL)] * 3
        + [_resident(ws.shape), _resident(bst.shape), _resident(sq), _resident(sq), _resident(sq)],
        out_specs=[_row_tile(SHARD_W)] * 3 + [_row_tile(D_MODEL)] * 3
        + [pl.BlockSpec(ws.shape, lambda i: (0, 0, 0)), pl.BlockSpec(bst.shape, lambda i: (0, 0))],
        out_shape=[jax.ShapeDtypeStruct((t, SHARD_W), MXU_DTYPE)] * 3 + [jax.ShapeDtypeStruct((t, D_MODEL), F32)]
        + [jax.ShapeDtypeStruct((t, D_MODEL), MXU_DTYPE)] * 2
        + [jax.ShapeDtypeStruct(ws.shape, F32), jax.ShapeDtypeStruct(bst.shape, F32)],
        compiler_params=_params(("arbitrary",)),
    )(dx1, pj1, pj2, pj3, ret_raw, a, b, ws, bst, w_ro, w_so, w_o)


def _retention_bwd_call(pj0, drr, states, consts, n_seq, seq, side):
    cos_full, sin_signed, decay_in, zeta, xi = consts
    nb = seq // TILE_M
    cpb = TILE_M // CHUNK

    def body(pj_ref, do_ref, st_ref, cos_ref, sin_ref, d_ref, zeta_ref, xi_ref, dpj_ref, gstate):
        @pl.when(pl.program_id(1) == 0)
        def _():
            gstate[...] = jnp.zeros_like(gstate)

        for cc in reversed(range(cpb)):
            rows = slice(cc * CHUNK, (cc + 1) * CHUNK)
            cs, sn = cos_ref[rows, :], sin_ref[rows, :]
            for h in range(RET_HEADS):
                q, k, v = _qkv(pj_ref, rows, h)
                qt = (q * cs + _rot(q) * sn) * QK_SCALE
                kt = k * cs + _rot(k) * sn
                d_out = do_ref[rows, h * V_DIM:(h + 1) * V_DIM]
                prev = st_ref[cc, h]
                g = gstate[h]
                dec = d_ref[h]
                scores_d = _mm_nt(qt, kt) * dec
                dscores = _mm_nt(d_out, v) * dec
                kz = kt * zeta_ref[h]
                qx = qt * xi_ref[h]
                dv = _mm_tn(scores_d, d_out) + _mm(kz, g)
                dqt = (_mm(dscores, kt) + _mm_nt(d_out, prev) * xi_ref[h]) * QK_SCALE
                dkt = _mm_tn(dscores, qt) + _mm_nt(v, g) * zeta_ref[h]
                gstate[h] = _mm_tn(qx, d_out) + CHUNK_DECAY[h] * g
                dq = dqt * cs + _rot(dqt * sn)
                dk = dkt * cs + _rot(dkt * sn)
                dpj_ref[rows, h * QK_DIM:(h + 1) * QK_DIM] = dq.astype(dpj_ref.dtype)
                dpj_ref[rows, RET_HEADS * QK_DIM + h * QK_DIM:RET_HEADS * QK_DIM + (h + 1) * QK_DIM] = dk.astype(
                    dpj_ref.dtype)
                dpj_ref[rows, 2 * RET_HEADS * QK_DIM + h * V_DIM:2 * RET_HEADS * QK_DIM + (h + 1) * V_DIM] = dv.astype(
                    dpj_ref.dtype)

    row = lambda b, n: (b * nb + nb - 1 - n, 0)
    pos = lambda b, n: (nb - 1 - n, 0)
    return _compute_call(
        body, name="retention_bwd", grid=(n_seq, nb),
        in_specs=[pl.BlockSpec((None, TILE_M, SHARD_W), lambda b, n: (0,) + row(b, n)),
                  pl.BlockSpec((TILE_M, D_MODEL), row),
                  pl.BlockSpec((cpb, RET_HEADS, QK_DIM, V_DIM), lambda b, n: (b * nb + nb - 1 - n, 0, 0, 0)),
                  pl.BlockSpec((TILE_M, QK_DIM), pos), pl.BlockSpec((TILE_M, QK_DIM), pos),
                  _resident(decay_in.shape), _resident(zeta.shape), _resident(xi.shape)],
        out_specs=[pl.BlockSpec((TILE_M, SHARD_W), row)],
        out_shape=[jax.ShapeDtypeStruct((n_seq * seq, SHARD_W), MXU_DTYPE)],
        scratch_shapes=[pltpu.VMEM((RET_HEADS, QK_DIM, V_DIM), F32)],
        operands=(pj0, drr, states, cos_full, sin_signed, decay_in, zeta, xi),
        semantics=("arbitrary", "arbitrary"), side=side)


def _dx_call(dpj, x2d, dx1, g_mixer, wg_in, side):
    t = x2d.shape[0]

    def body(d0, d1, d2, d3, x_ref, dx1_ref, g_ref, w_ref, dx_ref, dg_ref):
        @pl.when(pl.program_id(0) == 0)
        def _():
            dg_ref[...] = jnp.zeros_like(dg_ref)

        dh = _mm_nt(d0[...], w_ref[0])
        for j, d_ref in enumerate((d1, d2, d3)):
            dh += _mm_nt(d_ref[...], w_ref[j + 1])
        xn, r = _rms(x_ref[...])
        dg_ref[...] += jnp.sum(dh * xn, axis=0, keepdims=True)
        dx_ref[...] = dx1_ref[...] + _rms_bwd(dh * g_ref[...], xn, r)

    return _compute_call(
        body, name="dx_bwd", grid=(t // TILE_M,),
        in_specs=[_row_tile(SHARD_W)] * N_CHIPS + [_row_tile(D_MODEL)] * 2
        + [_resident((1, D_MODEL)), _resident(wg_in.shape)],
        out_specs=[_row_tile(D_MODEL), pl.BlockSpec((1, D_MODEL), lambda i: (0, 0))],
        out_shape=[jax.ShapeDtypeStruct((t, D_MODEL), F32), jax.ShapeDtypeStruct((1, D_MODEL), F32)],
        operands=(*dpj, x2d, dx1, g_mixer, wg_in), semantics=("arbitrary",), side=side)


def _wgrad_call(name, lhs, rhs, block_n, out_cols=None, block_t=512, into=None, slot=0, n_slots=1):
    t, n = rhs.shape
    k = lhs.shape[1]
    steps = t // block_t
    out_cols = block_n if out_cols is None else out_cols
    per = block_n // out_cols
    first_block = slot * (n // block_n)

    def body(l_ref, r_ref, *rest):
        o_ref, acc = rest[-2:]
        @pl.when(pl.program_id(1) == 0)
        def _():
            acc[...] = jnp.zeros_like(acc)

        acc[...] += _mm_tn(l_ref[...], r_ref[...])

        @pl.when(pl.program_id(1) == steps - 1)
        def _():
            for s in range(per):
                o_ref[s] = acc[:, s * out_cols:(s + 1) * out_cols].astype(o_ref.dtype)

    return pl.pallas_call(
        body, name=name, grid=(n // block_n, steps),
        in_specs=[pl.BlockSpec((block_t, k), lambda j, i: (i, 0)), pl.BlockSpec((block_t, block_n), lambda j, i: (i, j))]
        + ([] if into is None else [ANY]),
        out_specs=pl.BlockSpec((per, k, out_cols), lambda j, i: (first_block + j, 0, 0)),
        out_shape=jax.ShapeDtypeStruct((n_slots * (n // out_cols), k, out_cols), COMM_DTYPE),
        scratch_shapes=[pltpu.VMEM((k, block_n), F32)],
        input_output_aliases={} if into is None else {2: 0},
        compiler_params=_params(("arbitrary", "arbitrary")),
    )(*((lhs, rhs) if into is None else (lhs, rhs, into)))


def _position():
    return lax.axis_index("x"), lax.axis_index("y"), lax.axis_index("c")


def _position_array():
    x, y, c = _position()
    return jnp.stack([2 * x + y, c]).astype(jnp.int32)


def _other_chip(x, y, k):
    return (1 - x if k & 2 else x), (1 - y if k & 1 else y)


def _plan_side(operands, out_shapes, plan, n_remote, aliases=None):
    def copies(ins, outs, send_sem, recv_sem):
        remote = plan(ins, outs)
        assert len(remote) == n_remote
        return [pltpu.make_async_remote_copy(src_ref=src, dst_ref=dst, send_sem=send_sem.at[i],
                                             recv_sem=recv_sem.at[i], device_id=dev, device_id_type=MESH)
                for i, (src, dst, dev) in enumerate(remote)]

    def start(*a):
        for cp in copies(*a):
            cp.start()

    def finish(*a):
        for cp in copies(*a):
            cp.wait()

    return _Side(operands, out_shapes, n_remote, start, finish, aliases)


def _place_cast_call(name, w, pos):
    rows, width = w.shape
    block_rows = min(256, rows)

    def body(pos_ref, w_ref, o_ref):
        o_ref[...] = w_ref[...].astype(o_ref.dtype)

    return pl.pallas_call(
        body, name=name,
        grid_spec=pltpu.PrefetchScalarGridSpec(
            num_scalar_prefetch=1, grid=(rows // block_rows,),
            in_specs=[pl.BlockSpec((block_rows, width), lambda i, pos: (i, 0))],
            out_specs=pl.BlockSpec((None, block_rows, width), lambda i, pos: (pos[0], i, 0))),
        out_shape=jax.ShapeDtypeStruct((N_CHIPS, rows, width), MXU_DTYPE),
        compiler_params=_params(("arbitrary",)),
    )(pos, w)


def _gather_side(placed):
    n = len(placed)

    def copies(kind, bufs, send_sem, recv_sem):
        x, y, c = _position()
        me = 2 * x + y
        made = []
        for i in range(n):
            hr = placed[i].shape[1] // 2
            for k in (1, 2, 3):
                px, py = _other_chip(x, y, k)
                chip, core, slot, dev = [(me, c, k - 1, (px, py, c)), (2 * px + py, c, 2 + k, (x, y, 1 - c)),
                                         (2 * px + py, 1 - c, 2 + k, (x, y, 1 - c))][kind]
                piece = bufs[i].at[chip, pl.ds(core * hr, hr)]
                made.append(pltpu.make_async_remote_copy(
                    src_ref=piece, dst_ref=piece, send_sem=send_sem.at[6 * i + slot],
                    recv_sem=recv_sem.at[6 * i + slot], device_id=dev, device_id_type=MESH))
        return made

    def start(ins, outs, send_sem, recv_sem):
        for cp in copies(0, outs, send_sem, recv_sem):
            cp.start()

    def finish(ins, outs, send_sem, recv_sem):
        first, onward = copies(0, outs, send_sem, recv_sem), copies(1, outs, send_sem, recv_sem)
        for landed, cp in zip(first, onward):
            landed.wait_recv()
            cp.start()
        for cp in copies(2, outs, send_sem, recv_sem):
            cp.wait_recv()
        for cp in first + onward:
            cp.wait_send()

    return _Side(placed, [jax.ShapeDtypeStruct(a.shape, a.dtype) for a in placed], 6 * n, start, finish,
                 {i: i for i in range(n)})


def _sum_call(name, pos, terms, rows, width, out_dtype, block_rows, out_rows=None, out_index=None):
    out_rows = rows if out_rows is None else out_rows
    out_index = (lambda i, pos, br: i) if out_index is None else out_index

    def body(pos_ref, *refs):
        acc = refs[0][...].astype(F32)
        for r in refs[1:-1]:
            acc = acc + r[...].astype(F32)
        refs[-1][...] = acc.astype(out_dtype)

    def spec(index):
        return pl.BlockSpec((block_rows, width), lambda i, pos: (index(i, pos, block_rows), 0))

    return pl.pallas_call(
        body, name=name,
        grid_spec=pltpu.PrefetchScalarGridSpec(
            num_scalar_prefetch=1, grid=(rows // block_rows,),
            in_specs=[spec(index) for _, index in terms], out_specs=spec(out_index)),
        out_shape=jax.ShapeDtypeStruct((out_rows, width), out_dtype),
        compiler_params=_params(("arbitrary",)),
    )(pos, *[arr for arr, _ in terms])


def _pair_stage(tag, grads, pos):
    n = len(grads)
    shapes = [(g.shape[0] // N_CHIPS, g.shape[1]) for g in grads]

    def plan(ins, outs):
        x, y, c = _position()
        remote = []
        for i, (r, _) in enumerate(shapes):
            hr = r // 2
            for j in range(N_CHIPS):
                remote.append((ins[i].at[pl.ds(j * r + (1 - c) * hr, hr)], outs[i].at[pl.ds(j * hr, hr)],
                               (x, y, 1 - c)))
        return remote

    pair = _exchange_call(f"grad_pair_exchange_{tag}", _plan_side(
        grads, [jax.ShapeDtypeStruct((N_CHIPS * (r // 2), w), COMM_DTYPE) for r, w in shapes], plan, N_CHIPS * n))

    def own_half(r):
        def index(s, pos, br):
            per = (r // 2) // br
            return (s // per) * (r // br) + pos[1] * per + s % per
        return index

    return [_sum_call(f"grad_pair_sum_{tag}{i}", pos, [(grads[i], own_half(r)), (pair[i], lambda s, pos, br: s)],
                      N_CHIPS * (r // 2), w, COMM_DTYPE, min(256, r // 2)) for i, (r, w) in enumerate(shapes)]


def _chip_side(pair_sums):
    halves = [(p.shape[0] // N_CHIPS, p.shape[1]) for p in pair_sums]

    def plan(ins, outs):
        x, y, c = _position()
        remote = []
        for i, (hr, _) in enumerate(halves):
            for k in (1, 2, 3):
                px, py = _other_chip(x, y, k)
                remote.append((ins[i].at[pl.ds((2 * px + py) * hr, hr)], outs[i].at[pl.ds((k - 1) * hr, hr)],
                               (px, py, c)))
        return remote

    return _plan_side(pair_sums, [jax.ShapeDtypeStruct((3 * hr, w), COMM_DTYPE) for hr, w in halves], plan,
                      3 * len(pair_sums))


def _finish_stage(tag, pair_sums, chip, pos):
    n = len(pair_sums)
    halves = [(p.shape[0] // N_CHIPS, p.shape[1]) for p in pair_sums]

    def slab(k, hr):
        return lambda s, pos, br: k * (hr // br) + s

    reduced = []
    for i, (hr, w) in enumerate(halves):
        terms = [(pair_sums[i], lambda s, pos, br, hr=hr: pos[0] * (hr // br) + s)]
        terms += [(chip[i], slab(k, hr)) for k in range(3)]
        reduced.append(_sum_call(f"grad_chip_sum_{tag}{i}", pos, terms, hr, w, F32, min(256, hr), out_rows=2 * hr,
                                 out_index=lambda s, pos, br, hr=hr: pos[1] * (hr // br) + s))

    def plan(ins, outs):
        x, y, c = _position()
        remote = []
        for i, (hr, _) in enumerate(halves):
            mine = outs[i].at[pl.ds(c * hr, hr)]
            remote.append((mine, mine, (x, y, 1 - c)))
        return remote

    return _exchange_call(f"grad_half_exchange_{tag}", _plan_side(
        reduced, [jax.ShapeDtypeStruct((2 * hr, w), F32) for hr, w in halves], plan, n, {i: i for i in range(n)}))


def _gather_small(parts):
    n = len(parts)

    def body(*refs):
        ins, outs = refs[:n], refs[n:2 * n]
        loc_sem, send_sem, recv_sem = refs[2 * n:]
        x, y, c = _position()
        copies = []
        for i in range(n):
            mine = outs[i].at[4 * x + 2 * y + c]
            copies.append(pltpu.make_async_copy(ins[i], mine, loc_sem.at[i]))
            for d in range(1, 8):
                px, py = _other_chip(x, y, d >> 1)
                copies.append(pltpu.make_async_remote_copy(
                    src_ref=ins[i], dst_ref=mine, send_sem=send_sem.at[7 * i + d - 1],
                    recv_sem=recv_sem.at[7 * i + d - 1], device_id=(px, py, 1 - c if d & 1 else c),
                    device_id_type=MESH))
        for cp in copies:
            cp.start()
        for cp in copies:
            cp.wait()

    return pl.pallas_call(
        body, name="small_grad_gather", in_specs=[ANY] * n, out_specs=[ANY] * n,
        out_shape=[jax.ShapeDtypeStruct((8,) + a.shape, F32) for a in parts],
        scratch_shapes=[pltpu.SemaphoreType.DMA((n,)), pltpu.SemaphoreType.DMA((7 * n,)),
                        pltpu.SemaphoreType.DMA((7 * n,))],
        compiler_params=pltpu.CompilerParams(has_side_effects=True),
    )(*parts)


def _adamw(w, g, m, v):
    m = ADAM_B1 * m + (1.0 - ADAM_B1) * g
    v = ADAM_B2 * v + (1.0 - ADAM_B2) * (g * g)
    m_hat = m / (1.0 - ADAM_B1 ** ADAM_STEP)
    v_hat = v / (1.0 - ADAM_B2 ** ADAM_STEP)
    delta = -ADAM_LR * (m_hat / (jnp.sqrt(v_hat) + ADAM_EPS) + ADAM_WD * w)
    return delta, m, v


def _adamw_call(name, w, g, m, v):
    rows, width = w.shape
    block_rows = min(256, rows)

    def body(w_ref, g_ref, m_ref, v_ref, d_out, m_out, v_out):
        d_out[...], m_out[...], v_out[...] = _adamw(w_ref[...], g_ref[...], m_ref[...], v_ref[...])

    spec = pl.BlockSpec((block_rows, width), lambda i: (i, 0))
    return pl.pallas_call(
        body, name=name, grid=(rows // block_rows,), in_specs=[spec] * 4, out_specs=[spec] * 3,
        out_shape=[jax.ShapeDtypeStruct(w.shape, F32)] * 3,
        compiler_params=_params(("arbitrary",)),
    )(w, g, m, v)


def _small_adamw_call(gathered, weights, moments_m, moments_v):
    n = len(weights)

    def body(*refs):
        all_refs, refs = refs[:n + 1], refs[n + 1:]
        w_refs, m_refs, v_refs, outs = refs[:n], refs[n:2 * n], refs[2 * n:3 * n], refs[3 * n:]

        def total(ref):
            acc = ref[0]
            for d in range(1, 8):
                acc = acc + ref[d]
            return acc

        outs[0][...] = total(all_refs[n])
        for i in range(n):
            g = total(all_refs[i])
            if i == 0:
                row = lax.broadcasted_iota(jnp.int32, g.shape, 0)
                col = lax.broadcasted_iota(jnp.int32, g.shape, 1)
                g = jnp.where((row % CHUNK) >= col, g, 0.0)
            g_out, d_out, m_out, v_out = outs[1 + 4 * i:5 + 4 * i]
            g_out[...] = g
            d_out[...], m_out[...], v_out[...] = _adamw(w_refs[i][...], g, m_refs[i][...], v_refs[i][...])

    out_shape = [jax.ShapeDtypeStruct(LOSS_TILE, F32)]
    for w in weights:
        out_shape += [jax.ShapeDtypeStruct(w.shape, F32)] * 4
    res = pl.pallas_call(
        body, name="small_adamw", out_shape=out_shape,
        compiler_params=pltpu.CompilerParams(vmem_limit_bytes=VMEM_LIMIT),
    )(*gathered, *weights, *moments_m, *moments_v)
    return res[0], [res[1 + 4 * i:5 + 4 * i] for i in range(n)]


def kernel(x, p, w_in, w_ret_out, w_sgu_out, w_out, sgu_ws, sgu_bs, w_ple_gate, w_ple_proj, g_mixer, g_ple, g_final, loss_target, m_w_in, m_w_ret_out, m_w_sgu_out, m_w_out, m_sgu_ws, m_sgu_bs, m_w_ple_gate, m_w_ple_proj, m_g_mixer, m_g_ple, m_g_final, v_w_in, v_w_ret_out, v_w_sgu_out, v_w_out, v_sgu_ws, v_sgu_bs, v_w_ple_gate, v_w_ple_proj, v_g_mixer, v_g_ple, v_g_final):
    n_seq, seq, _ = x.shape
    t = n_seq * seq
    x2d = x.reshape(t, D_MODEL)
    p2d = p.reshape(t, PLE_DIM)
    target = loss_target.reshape(t, D_MODEL)
    big = [w_in[0], w_ret_out[0], w_sgu_out[0], w_out[0], w_ple_gate[0], w_ple_proj[0]]
    big_m = [m_w_in[0], m_w_ret_out[0], m_w_sgu_out[0], m_w_out[0], m_w_ple_gate[0], m_w_ple_proj[0]]
    big_v = [v_w_in[0], v_w_ret_out[0], v_w_sgu_out[0], v_w_out[0], v_w_ple_gate[0], v_w_ple_proj[0]]

    pos = _position_array()
    placed = [_place_cast_call(f"place_weight_{i}", w, pos) for i, w in enumerate(big)]
    ws = sgu_ws[0]
    bst = sgu_bs[0].T
    consts = _retention_consts(seq)

    (h, pj, wg_in), gathered = _proj_call(x2d, g_mixer, placed[0], pos, _gather_side(placed[1:]))
    pj0 = pj1 = pj2 = pj3 = pj
    w_ro, w_so, w_o, w_pg = (w.reshape(D_MODEL, D_MODEL) for w in gathered[:4])
    w_pp = gathered[4].transpose(1, 0, 2).reshape(PLE_DIM, D_MODEL)
    ret_raw, states = _retention_fwd_call(pj0, consts, n_seq, seq)
    x1, ret, sgu, merged, a, b = _merge_fwd_call(pj1, pj2, pj3, ret_raw, x2d, ws, bst, w_ro, w_so, w_o)
    dx1, hp, dz, dpp, loss, dg_ple, dg_final = _ple_call(x1, p2d, target, g_ple, g_final.reshape(1, D_MODEL),
                                                         w_pg, w_pp)
    dpj1, dpj2, dpj3, drr, da, db, dws, dbst = _merge_bwd_call(dx1, pj1, pj2, pj3, ret_raw, a, b, ws, bst,
                                                               w_ro, w_so, w_o)

    rows_of = lambda g: g.reshape(g.shape[0] * g.shape[1], g.shape[2])
    tail_grads = [
        rows_of(_wgrad_call("wgrad_ret_out", ret, da, D_MODEL)),
        rows_of(_wgrad_call("wgrad_sgu_out", sgu, db, D_MODEL)),
        rows_of(_wgrad_call("wgrad_out", merged, dx1, D_MODEL)),
        rows_of(_wgrad_call("wgrad_ple_gate", hp, dz, D_MODEL)),
        rows_of(_wgrad_call("wgrad_ple_proj", p2d, dpp, D_MODEL, out_cols=PLE_DIM)),
    ]
    tail_sums = _pair_stage("tail", tail_grads, pos)
    (dpj0,), tail_chip = _retention_bwd_call(pj0, drr, states, consts, n_seq, seq, _chip_side(tail_sums))
    in_grad = None
    for j, d in enumerate((dpj0, dpj1, dpj2, dpj3)):
        in_grad = _wgrad_call(f"wgrad_in_{j}", h, d, SHARD_W, into=in_grad, slot=j, n_slots=N_CHIPS)
    in_sums = _pair_stage("in", [rows_of(in_grad)], pos)
    (dx, dg_mixer), in_chip = _dx_call((dpj0, dpj1, dpj2, dpj3), x2d, dx1, g_mixer, wg_in, _chip_side(in_sums))
    g_big = _finish_stage("in", in_sums, in_chip, pos) + _finish_stage("tail", tail_sums, tail_chip, pos)
    upd = [_adamw_call(f"adamw_{i}", big[i], g_big[i], big_m[i], big_v[i]) for i in range(len(big))]

    small_shapes = [(SGU_GROUPS * CHUNK, CHUNK), (SGU_GROUPS, CHUNK), (1, D_MODEL), (1, D_MODEL), (1, D_MODEL)]
    as_small = lambda arrays: [a_.reshape(s) for a_, s in zip(arrays, small_shapes)]
    small_g = _gather_small(as_small([dws, dbst.T, dg_mixer, dg_ple, dg_final]) + [loss])
    total, small = _small_adamw_call(small_g, as_small([sgu_ws, sgu_bs, g_mixer, g_ple, g_final]),
                                     as_small([m_sgu_ws, m_sgu_bs, m_g_mixer, m_g_ple, m_g_final]),
                                     as_small([v_sgu_ws, v_sgu_bs, v_g_mixer, v_g_ple, v_g_final]))
    out_small_shapes = [sgu_ws.shape, sgu_bs.shape, g_mixer.shape, g_ple.shape, g_final.shape]

    def ordered(big_list, kind):
        w_in_, w_ro_, w_so_, w_o_, w_pg_, w_pp_ = [b_[None] for b_ in big_list]
        s_ws, s_bs, s_gm, s_gp, s_gf = [small[i][kind].reshape(s) for i, s in enumerate(out_small_shapes)]
        return [w_in_, w_ro_, w_so_, w_o_, s_ws, s_bs, w_pg_, w_pp_, s_gm, s_gp, s_gf]

    out = [total[0, 0], dx.reshape(x.shape)]
    out += ordered(g_big, 0)
    out += ordered([u[0] for u in upd], 1)
    out += ordered([u[1] for u in upd], 2)
    out += ordered([u[2] for u in upd], 3)
    return tuple(out)
```

```python
import functools
import math

import numpy as np
import jax
import jax.numpy as jnp
from jax import lax
from jax.experimental import pallas as pl
from jax.experimental.pallas import tpu as pltpu

F32 = jnp.float32
MXU_DTYPE = jnp.bfloat16
COMM_DTYPE = jnp.bfloat16

D_MODEL = 1024
RET_HEADS = 4
QK_DIM = 128
V_DIM = 256
CHUNK = 128
SGU_GROUPS = 4
GROUP_DIM = 256
PLE_DIM = 256
N_CHIPS = 4
SHARD_W = 2048
ROPE_BASE = 10000.0
NORM_EPS = 1e-6
GN_EPS = 1e-5
QK_SCALE = QK_DIM ** -0.5
SQRT_HALF = math.sqrt(0.5)
INV_SQRT_2PI = 1.0 / math.sqrt(2.0 * math.pi)

ADAM_LR = 0.001
ADAM_B1 = 0.9
ADAM_B2 = 0.999
ADAM_EPS = 1e-08
ADAM_WD = 0.01
ADAM_STEP = 10

TILE_M = 256
LOSS_TILE = (8, 128)
LOAD_PARTS = 16
PROJ_TILE = 512
DX_TILE = 512
VMEM_LIMIT = 56 * 1024 * 1024
MESH = pl.DeviceIdType.MESH
ANY = pl.BlockSpec(memory_space=pl.ANY)

CHUNK_DECAY = tuple(
    float(np.exp(np.float32(CHUNK) * np.log(np.float32(1.0 - 2.0 ** (-5.0 - h))))) for h in range(RET_HEADS))


def _mm(a, b):
    return jnp.dot(a.astype(MXU_DTYPE), b.astype(MXU_DTYPE), preferred_element_type=F32)


def _mm_nt(a, b):
    return lax.dot_general(a.astype(MXU_DTYPE), b.astype(MXU_DTYPE), (((1,), (1,)), ((), ())),
                           preferred_element_type=F32)


def _mm_tn(a, b):
    return lax.dot_general(a.astype(MXU_DTYPE), b.astype(MXU_DTYPE), (((0,), (0,)), ((), ())),
                           preferred_element_type=F32)


def _mean(x):
    return jnp.mean(x, axis=-1, keepdims=True)


def _sigmoid(x):
    return jax.nn.sigmoid(x)


def _silu_and_grad(x):
    s = _sigmoid(x)
    return x * s, s * (1.0 + x * (1.0 - s))


def _gelu(x):
    return 0.5 * x * (1.0 + lax.erf(x * SQRT_HALF))


def _gelu_grad(x):
    return 0.5 * (1.0 + lax.erf(x * SQRT_HALF)) + x * jnp.exp(-0.5 * x * x) * INV_SQRT_2PI


def _unit_norm(x, eps):
    xc = x - _mean(x)
    rstd = lax.rsqrt(_mean(xc * xc) + eps)
    return xc * rstd, rstd


def _unit_norm_bwd(dn, n, rstd):
    return rstd * (dn - _mean(dn) - n * _mean(dn * n))


def _rms(x):
    r = lax.rsqrt(_mean(x * x) + NORM_EPS)
    return x * r, r


def _rms_bwd(dxn, xn, r):
    return r * (dxn - xn * _mean(dxn * xn))


def _rot(x):
    return pltpu.roll(x, QK_DIM // 2, 1)


def _params(semantics, **kw):
    return pltpu.CompilerParams(dimension_semantics=semantics, vmem_limit_bytes=VMEM_LIMIT, **kw)


def _row_tile(width, tm=TILE_M):
    return pl.BlockSpec((tm, width), lambda i: (i, 0))


def _proj_tile(j):
    return pl.BlockSpec((None, TILE_M, SHARD_W), lambda i: (j, i, 0))


def _resident(shape):
    nd = len(shape)
    return pl.BlockSpec(shape, lambda *_: (0,) * nd, pipeline_mode=pl.Buffered(1))


def _causal_ws(ws_ref):
    row = lax.broadcasted_iota(jnp.int32, (CHUNK, CHUNK), 0)
    col = lax.broadcasted_iota(jnp.int32, (CHUNK, CHUNK), 1)
    return [jnp.where(row >= col, ws_ref[g], 0.0).astype(MXU_DTYPE) for g in range(SGU_GROUPS)]


def _heads(x, width):
    return [x[:, h * width:(h + 1) * width] for h in range(x.shape[1] // width)]


def _branch_forward(pj1, pj2, ret_raw, wsc, bst):
    rg, su = pj1[:, :D_MODEL], pj1[:, D_MODEL:]
    sv, sg = pj2[:, :D_MODEL], pj2[:, D_MODEL:]
    rn_parts, rstd_parts = zip(*[_unit_norm(r, GN_EPS) for r in _heads(ret_raw, V_DIM)])
    rn = jnp.concatenate(rn_parts, axis=1)
    sil_rg, dsil_rg = _silu_and_grad(rg)
    ret = rn * sil_rg
    u = _gelu(su)
    vn, rstd_v = _unit_norm(_gelu(sv), GN_EPS)
    rows = []
    for cc in range(pj1.shape[0] // CHUNK):
        cols = []
        for g in range(SGU_GROUPS):
            blk = vn[cc * CHUNK:(cc + 1) * CHUNK, g * GROUP_DIM:(g + 1) * GROUP_DIM]
            cols.append(_mm(wsc[g], blk) + bst[:, g:g + 1])
        rows.append(jnp.concatenate(cols, axis=1))
    mixed = jnp.concatenate(rows, axis=0)
    sil_sg, dsil_sg = _silu_and_grad(sg)
    sgu = u * mixed * sil_sg
    return dict(rg=rg, su=su, sv=sv, sg=sg, rn=rn, rstd_r=rstd_parts, sil_rg=sil_rg, dsil_rg=dsil_rg, ret=ret,
                u=u, vn=vn, rstd_v=rstd_v, mixed=mixed, sil_sg=sil_sg, dsil_sg=dsil_sg, sgu=sgu)


class _Side:
    def __init__(self, operands, out_shapes, n_sems, start, finish, aliases=None):
        self.operands, self.out_shapes, self.n_sems = list(operands), list(out_shapes), n_sems
        self.start, self.finish, self.aliases = start, finish, dict(aliases or {})


def _join_sides(*sides):
    spans, a, b, s = [], 0, 0, 0
    for side in sides:
        spans.append((a, b, s))
        a, b, s = a + len(side.operands), b + len(side.out_shapes), s + side.n_sems

    def run(which):
        def go(ins, outs, send_sem, recv_sem, base=0):
            for side, (a0, b0, s0) in zip(sides, spans):
                getattr(side, which)(ins[a0:a0 + len(side.operands)], outs[b0:b0 + len(side.out_shapes)],
                                     send_sem, recv_sem, base + s0)
        return go

    aliases = {a0 + i: b0 + o for side, (a0, b0, _) in zip(sides, spans) for i, o in side.aliases.items()}
    return _Side([x for side in sides for x in side.operands], [x for side in sides for x in side.out_shapes], s,
                 run("start"), run("finish"), aliases)


def _compute_call(body, *, name, grid, in_specs, out_specs, out_shape, operands, semantics, scratch_shapes=(),
                  side=None, prefetch=None, aliases=None, side_start=None):
    n_pre = 0 if prefetch is None else 1
    pre = () if prefetch is None else (prefetch,)

    def spec(in_specs, out_specs, scratch):
        return pltpu.PrefetchScalarGridSpec(num_scalar_prefetch=n_pre, grid=grid, in_specs=in_specs,
                                            out_specs=out_specs, scratch_shapes=scratch)

    if side is None:
        return pl.pallas_call(body, name=name, grid_spec=spec(in_specs, out_specs, list(scratch_shapes)),
                              out_shape=out_shape,
                              input_output_aliases={n_pre + a: b for a, b in (aliases or {}).items()},
                              compiler_params=_params(semantics))(*pre, *operands)
    n_in, n_out, s_in, s_out = len(operands), len(out_shape), len(side.operands), len(side.out_shapes)

    def carrier(*refs):
        pre_refs, refs = refs[:n_pre], refs[n_pre:]
        ins, refs = refs[:n_in], refs[n_in:]
        side_ins, refs = refs[:s_in], refs[s_in:]
        outs, refs = refs[:n_out], refs[n_out:]
        side_outs, refs = refs[:s_out], refs[s_out:]
        scratch, (send_sem, recv_sem) = refs[:-2], refs[-2:]
        ids = [pl.program_id(a) for a in range(len(grid))]
        at = (0,) * len(grid) if side_start is None else side_start
        first = functools.reduce(jnp.logical_and, [i == a for i, a in zip(ids, at)])
        last = functools.reduce(jnp.logical_and, [i == g - 1 for i, g in zip(ids, grid)])

        @pl.when(first)
        def _():
            side.start(side_ins, side_outs, send_sem, recv_sem)

        body(*pre_refs, *ins, *outs, *scratch)

        @pl.when(last)
        def _():
            side.finish(side_ins, side_outs, send_sem, recv_sem)

    all_aliases = {n_pre + a: b for a, b in (aliases or {}).items()}
    all_aliases.update({n_pre + n_in + a: n_out + b for a, b in side.aliases.items()})
    res = pl.pallas_call(
        carrier, name=name,
        grid_spec=spec(list(in_specs) + [ANY] * s_in, list(out_specs) + [ANY] * s_out,
                       list(scratch_shapes) + [pltpu.SemaphoreType.DMA((side.n_sems,))] * 2),
        out_shape=list(out_shape) + side.out_shapes, input_output_aliases=all_aliases,
        compiler_params=_params(semantics, has_side_effects=True),
    )(*pre, *operands, *side.operands)
    return res[:n_out], res[n_out:]


def _exchange_call(name, side):
    s_in = len(side.operands)

    def body(*refs):
        ins, outs = refs[:s_in], refs[s_in:s_in + len(side.out_shapes)]
        send_sem, recv_sem = refs[s_in + len(side.out_shapes):]
        side.start(ins, outs, send_sem, recv_sem)
        side.finish(ins, outs, send_sem, recv_sem)

    return pl.pallas_call(
        body, name=name, in_specs=[ANY] * s_in, out_specs=[ANY] * len(side.out_shapes), out_shape=side.out_shapes,
        scratch_shapes=[pltpu.SemaphoreType.DMA((side.n_sems,))] * 2, input_output_aliases=side.aliases,
        compiler_params=pltpu.CompilerParams(has_side_effects=True),
    )(*side.operands)


def _proj_call(x2d, g_mixer, placed_in, pos, side):
    t = x2d.shape[0]
    nt = t // PROJ_TILE
    hr = placed_in.shape[1] // 2

    def body(pos_ref, x_ref, g_ref, win_ref, h_ref, pj_ref, w_ref, w_vmem, h_all, loc_sem, send_sem, recv_sem):
        k, i = pl.program_id(0), pl.program_id(1)
        x, y, c = _position()
        me = 2 * x + y

        def copy(slot, chip, core, dev):
            piece = w_ref.at[chip, pl.ds(core * hr, hr)]
            return pltpu.make_async_remote_copy(src_ref=piece, dst_ref=piece, send_sem=send_sem.at[slot],
                                                recv_sem=recv_sem.at[slot], device_id=dev, device_id_type=MESH)

        def first_hop(kk):
            px, py = _other_chip(x, y, kk)
            return copy(kk - 1, me, c, (px, py, c))

        def relay():
            source = jnp.bitwise_xor(me, 2 - c)
            return copy(2, source, c, (jnp.bitwise_xor(x, c), jnp.bitwise_xor(y, 1 - c), c))

        def onward(kk, core):
            px, py = _other_chip(x, y, kk)
            return copy(2 + kk, 2 * px + py, core, (x, y, 1 - c))

        def loads(kk):
            px, py = _other_chip(x, y, kk)
            rows = w_vmem.shape[1] // LOAD_PARTS
            return [pltpu.make_async_copy(w_ref.at[2 * px + py, pl.ds(r * rows, rows)],
                                          w_vmem.at[kk % 2, pl.ds(r * rows, rows)], loc_sem.at[r])
                    for r in range(LOAD_PARTS)]

        @pl.when(jnp.logical_and(k == 0, i == 0))
        def _():
            for kk in (1, 2):
                first_hop(kk).start()
            for cp in loads(0):
                cp.start()
            for cp in loads(0):
                cp.wait()

        for kk in (1, 2, 3):
            @pl.when(jnp.logical_and(k == kk - 1, i == nt - 1))
            def _(kk=kk):
                if kk == 1:
                    first_hop(1).wait_recv()
                    first_hop(2).wait_recv()
                    relay().start()
                if kk == 3:
                    relay().wait_recv()
                onward(kk, c).start()
                onward(kk, 1 - c).wait_recv()
                for cp in loads(kk):
                    cp.start()

            @pl.when(jnp.logical_and(k == kk, i == 0))
            def _(kk=kk):
                for cp in loads(kk):
                    cp.wait()

        rows = pl.ds(pl.multiple_of(i * PROJ_TILE, PROJ_TILE), PROJ_TILE)

        @pl.when(k == 0)
        def _():
            xn, _ = _rms(x_ref[...])
            h = (xn * g_ref[...]).astype(MXU_DTYPE)
            h_all[rows, :] = h
            h_ref[...] = h

        pj_ref[...] = jnp.dot(h_all[rows, :], w_vmem[k % 2], preferred_element_type=F32)

        @pl.when(jnp.logical_and(k == N_CHIPS - 1, i == nt - 1))
        def _():
            for cp in [first_hop(1), first_hop(2), relay()] + [onward(kk, c) for kk in (1, 2, 3)]:
                cp.wait_send()

    parked = lambda k, i, pos: (jnp.where(k == 0, i, nt - 1), 0)
    return _compute_call(
        body, name="proj_fwd", grid=(N_CHIPS, nt),
        in_specs=[pl.BlockSpec((PROJ_TILE, D_MODEL), parked), pl.BlockSpec((1, D_MODEL), lambda k, i, pos: (0, 0)), ANY],
        out_specs=[pl.BlockSpec((PROJ_TILE, D_MODEL), lambda k, i, pos: (jnp.where(k == 0, i, nt), 0)),
                   pl.BlockSpec((None, PROJ_TILE, SHARD_W), lambda k, i, pos: (jnp.bitwise_xor(pos[0], k), i, 0)),
                   ANY],
        out_shape=[jax.ShapeDtypeStruct((t + PROJ_TILE, D_MODEL), MXU_DTYPE),
                   jax.ShapeDtypeStruct((N_CHIPS, t, SHARD_W), F32),
                   jax.ShapeDtypeStruct(placed_in.shape, placed_in.dtype)],
        scratch_shapes=[pltpu.VMEM((2,) + placed_in.shape[1:], placed_in.dtype), pltpu.VMEM((t, D_MODEL), MXU_DTYPE),
                        pltpu.SemaphoreType.DMA((LOAD_PARTS,)), pltpu.SemaphoreType.DMA((6,)),
                        pltpu.SemaphoreType.DMA((6,))],
        operands=(x2d, g_mixer, placed_in), semantics=("arbitrary", "arbitrary"), side=side, prefetch=pos,
        aliases={2: 2}, side_start=(2, 0))


def _retention_consts(seq):
    half = QK_DIM // 2
    inv = ROPE_BASE ** (-jnp.arange(half, dtype=F32) / half)
    ang = jnp.arange(seq, dtype=F32)[:, None] * inv[None, :]
    cos, sin = jnp.cos(ang), jnp.sin(ang)
    cos_full = jnp.concatenate([cos, cos], axis=1)
    sin_signed = jnp.concatenate([-sin, sin], axis=1)
    log_g = jnp.log(1.0 - 2.0 ** (-5.0 - jnp.arange(RET_HEADS, dtype=F32)))
    idx = jnp.arange(CHUNK, dtype=F32)
    diff = idx[:, None] - idx[None, :]
    decay_in = jnp.where(diff[None] >= 0, jnp.exp(jnp.maximum(diff, 0.0)[None] * log_g[:, None, None]), 0.0)
    zeta = jnp.exp((CHUNK - 1.0 - idx)[None, :] * log_g[:, None])
    xi = jnp.exp((idx + 1.0)[None, :] * log_g[:, None])
    zeta = jnp.broadcast_to(zeta[:, :, None], (RET_HEADS, CHUNK, QK_DIM))
    xi = jnp.broadcast_to(xi[:, :, None], (RET_HEADS, CHUNK, QK_DIM))
    return cos_full, sin_signed, decay_in, zeta, xi


def _qkv(pj_ref, rows, h):
    q = pj_ref[rows, h * QK_DIM:(h + 1) * QK_DIM]
    k = pj_ref[rows, RET_HEADS * QK_DIM + h * QK_DIM:RET_HEADS * QK_DIM + (h + 1) * QK_DIM]
    v = pj_ref[rows, 2 * RET_HEADS * QK_DIM + h * V_DIM:2 * RET_HEADS * QK_DIM + (h + 1) * V_DIM]
    return q, k, v


def _retention_fwd_call(pj0, consts, n_seq, seq):
    cos_full, sin_signed, decay_in, zeta, xi = consts
    nb = seq // TILE_M
    cpb = TILE_M // CHUNK

    def body(pj_ref, cos_ref, sin_ref, d_ref, zeta_ref, xi_ref, o_ref, st_ref, state):
        @pl.when(pl.program_id(1) == 0)
        def _():
            state[...] = jnp.zeros_like(state)

        for cc in range(cpb):
            rows = slice(cc * CHUNK, (cc + 1) * CHUNK)
            cs, sn = cos_ref[rows, :], sin_ref[rows, :]
            for h in range(RET_HEADS):
                q, k, v = _qkv(pj_ref, rows, h)
                qt = (q * cs + _rot(q) * sn) * QK_SCALE
                kt = k * cs + _rot(k) * sn
                prev = state[h]
                st_ref[cc, h] = prev.astype(st_ref.dtype)
                scores = _mm_nt(qt, kt) * d_ref[h]
                o_ref[rows, h * V_DIM:(h + 1) * V_DIM] = _mm(scores, v) + _mm(qt * xi_ref[h], prev)
                state[h] = _mm_tn(kt * zeta_ref[h], v) + CHUNK_DECAY[h] * prev

    row = lambda b, n: (b * nb + n, 0)
    pos = lambda b, n: (n, 0)
    return pl.pallas_call(
        body, name="retention_fwd", grid=(n_seq, nb),
        in_specs=[pl.BlockSpec((None, TILE_M, SHARD_W), lambda b, n: (0,) + row(b, n)),
                  pl.BlockSpec((TILE_M, QK_DIM), pos),
                  pl.BlockSpec((TILE_M, QK_DIM), pos), _resident(decay_in.shape), _resident(zeta.shape),
                  _resident(xi.shape)],
        out_specs=[pl.BlockSpec((TILE_M, D_MODEL), row),
                   pl.BlockSpec((cpb, RET_HEADS, QK_DIM, V_DIM), lambda b, n: (b * nb + n, 0, 0, 0))],
        out_shape=[jax.ShapeDtypeStruct((n_seq * seq, D_MODEL), F32),
                   jax.ShapeDtypeStruct((n_seq * seq // CHUNK, RET_HEADS, QK_DIM, V_DIM), MXU_DTYPE)],
        scratch_shapes=[pltpu.VMEM((RET_HEADS, QK_DIM, V_DIM), F32)],
        compiler_params=_params(("arbitrary", "arbitrary")),
    )(pj0, cos_full, sin_signed, decay_in, zeta, xi)


def _merge_fwd_call(pj1, pj2, pj3, ret_raw, x2d, ws, bst, w_ro, w_so, w_o):
    t = x2d.shape[0]

    def body(pj1_ref, pj2_ref, pj3_ref, rr_ref, x_ref, ws_ref, bst_ref, wro_ref, wso_ref, wo_ref,
             x1_ref, ret_ref, sgu_ref, mg_ref, a_ref, b_ref):
        f = _branch_forward(pj1_ref[...], pj2_ref[...], rr_ref[...], _causal_ws(ws_ref), bst_ref[...])
        ret = f["ret"].astype(MXU_DTYPE)
        sgu = f["sgu"].astype(MXU_DTYPE)
        a = jnp.dot(ret, wro_ref[...], preferred_element_type=F32)
        b = jnp.dot(sgu, wso_ref[...], preferred_element_type=F32)
        pj3 = pj3_ref[...]
        merged = (_sigmoid(pj3[:, :D_MODEL]) * a + _sigmoid(pj3[:, D_MODEL:]) * b).astype(MXU_DTYPE)
        x1_ref[...] = x_ref[...] + jnp.dot(merged, wo_ref[...], preferred_element_type=F32)
        ret_ref[...] = ret
        sgu_ref[...] = sgu
        mg_ref[...] = merged
        a_ref[...] = a
        b_ref[...] = b

    sq = (D_MODEL, D_MODEL)
    return pl.pallas_call(
        body, name="merge_fwd", grid=(t // TILE_M,),
        in_specs=[_proj_tile(j) for j in (1, 2, 3)] + [_row_tile(D_MODEL)] * 2
        + [_resident(ws.shape), _resident(bst.shape), _resident(sq), _resident(sq), _resident(sq)],
        out_specs=[_row_tile(D_MODEL)] * 6,
        out_shape=[jax.ShapeDtypeStruct((t, D_MODEL), F32)] + [jax.ShapeDtypeStruct((t, D_MODEL), MXU_DTYPE)] * 3
        + [jax.ShapeDtypeStruct((t, D_MODEL), F32)] * 2,
        compiler_params=_params(("arbitrary",)),
    )(pj1, pj2, pj3, ret_raw, x2d, ws, bst, w_ro, w_so, w_o)


def _ple_call(x1, p2d, target, g_ple, g_final, w_pg, w_pp):
    t = x1.shape[0]

    def body(x1_ref, p_ref, t_ref, gp_ref, gf_ref, wpg_ref, wpp_ref,
             dx1_ref, hp_ref, dz_ref, dpp_ref, loss_ref, dgp_ref, dgf_ref):
        @pl.when(pl.program_id(0) == 0)
        def _():
            loss_ref[...] = jnp.zeros_like(loss_ref)
            dgp_ref[...] = jnp.zeros_like(dgp_ref)
            dgf_ref[...] = jnp.zeros_like(dgf_ref)

        x1v = x1_ref[...]
        xn1, r1 = _rms(x1v)
        hp = (xn1 * gp_ref[...]).astype(MXU_DTYPE)
        gate = _sigmoid(jnp.dot(hp, wpg_ref[...], preferred_element_type=F32))
        pp = jnp.dot(p_ref[...].astype(MXU_DTYPE), wpp_ref[...], preferred_element_type=F32)
        x2 = x1v + gate * pp
        xn2, r2 = _rms(x2)
        err = xn2 * gf_ref[...] - t_ref[...]
        loss_ref[...] += (0.5 / D_MODEL) * jnp.sum(jnp.sum(err * err, axis=1, keepdims=True), axis=0, keepdims=True)
        dy = err * (1.0 / D_MODEL)
        dgf_ref[...] += jnp.sum(dy * xn2, axis=0, keepdims=True)
        dx2 = _rms_bwd(dy * gf_ref[...], xn2, r2)
        dpp = (dx2 * gate).astype(MXU_DTYPE)
        dz = (dx2 * pp * gate * (1.0 - gate)).astype(MXU_DTYPE)
        dhp = _mm_nt(dz, wpg_ref[...])
        dgp_ref[...] += jnp.sum(dhp * xn1, axis=0, keepdims=True)
        dx1_ref[...] = dx2 + _rms_bwd(dhp * gp_ref[...], xn1, r1)
        hp_ref[...] = hp
        dz_ref[...] = dz
        dpp_ref[...] = dpp

    vec = _resident((1, D_MODEL))
    return pl.pallas_call(
        body, name="ple_fwd_bwd", grid=(t // TILE_M,),
        in_specs=[_row_tile(D_MODEL), _row_tile(PLE_DIM), _row_tile(D_MODEL), vec, vec,
                  _resident((D_MODEL, D_MODEL)), _resident((PLE_DIM, D_MODEL))],
        out_specs=[_row_tile(D_MODEL)] * 4 + [pl.BlockSpec(LOSS_TILE, lambda i: (0, 0)),
                                              pl.BlockSpec((1, D_MODEL), lambda i: (0, 0)),
                                              pl.BlockSpec((1, D_MODEL), lambda i: (0, 0))],
        out_shape=[jax.ShapeDtypeStruct((t, D_MODEL), F32)] + [jax.ShapeDtypeStruct((t, D_MODEL), MXU_DTYPE)] * 3
        + [jax.ShapeDtypeStruct(LOSS_TILE, F32), jax.ShapeDtypeStruct((1, D_MODEL), F32),
           jax.ShapeDtypeStruct((1, D_MODEL), F32)],
        compiler_params=_params(("arbitrary",)),
    )(x1, p2d, target, g_ple, g_final, w_pg, w_pp)


def _merge_bwd_call(dx1, pj1, pj2, pj3, ret_raw, a, b, ws, bst, w_ro, w_so, w_o):
    t = dx1.shape[0]

    def body(dx1_ref, pj1_ref, pj2_ref, pj3_ref, rr_ref, a_ref, b_ref, ws_ref, bst_ref, wro_ref, wso_ref, wo_ref,
             dpj1_ref, dpj2_ref, dpj3_ref, drr_ref, da_ref, db_ref, dws_ref, dbst_ref):
        @pl.when(pl.program_id(0) == 0)
        def _():
            dws_ref[...] = jnp.zeros_like(dws_ref)
            dbst_ref[...] = jnp.zeros_like(dbst_ref)

        wsc = _causal_ws(ws_ref)
        f = _branch_forward(pj1_ref[...], pj2_ref[...], rr_ref[...], wsc, bst_ref[...])
        pj3 = pj3_ref[...]
        smr, sms = _sigmoid(pj3[:, :D_MODEL]), _sigmoid(pj3[:, D_MODEL:])
        dmerged = _mm_nt(dx1_ref[...], wo_ref[...])
        da = (dmerged * smr).astype(MXU_DTYPE)
        db = (dmerged * sms).astype(MXU_DTYPE)
        dpj3_ref[:, :D_MODEL] = (dmerged * a_ref[...] * smr * (1.0 - smr)).astype(dpj3_ref.dtype)
        dpj3_ref[:, D_MODEL:] = (dmerged * b_ref[...] * sms * (1.0 - sms)).astype(dpj3_ref.dtype)
        da_ref[...] = da
        db_ref[...] = db

        dret = _mm_nt(da, wro_ref[...])
        dpj1_ref[:, :D_MODEL] = (dret * f["rn"] * f["dsil_rg"]).astype(dpj1_ref.dtype)
        drn = dret * f["sil_rg"]
        for h in range(RET_HEADS):
            cols = slice(h * V_DIM, (h + 1) * V_DIM)
            drr_ref[:, cols] = _unit_norm_bwd(drn[:, cols], f["rn"][:, cols], f["rstd_r"][h])

        dsgu = _mm_nt(db, wso_ref[...])
        dpj2_ref[:, D_MODEL:] = (dsgu * f["u"] * f["mixed"] * f["dsil_sg"]).astype(dpj2_ref.dtype)
        tg = dsgu * f["sil_sg"]
        dpj1_ref[:, D_MODEL:] = (tg * f["mixed"] * _gelu_grad(f["su"])).astype(dpj1_ref.dtype)
        dmixed = tg * f["u"]
        rows = []
        for cc in range(TILE_M // CHUNK):
            cols = []
            for g in range(SGU_GROUPS):
                rs, cs = slice(cc * CHUNK, (cc + 1) * CHUNK), slice(g * GROUP_DIM, (g + 1) * GROUP_DIM)
                dm = dmixed[rs, cs]
                cols.append(_mm_tn(wsc[g], dm))
                dws_ref[g] += _mm_nt(dm, f["vn"][rs, cs])
                dbst_ref[:, g:g + 1] += jnp.sum(dm, axis=1, keepdims=True)
            rows.append(jnp.concatenate(cols, axis=1))
        dvn = jnp.concatenate(rows, axis=0)
        dvv = _unit_norm_bwd(dvn, f["vn"], f["rstd_v"])
        dpj2_ref[:, :D_MODEL] = (dvv * _gelu_grad(f["sv"])).astype(dpj2_ref.dtype)

    sq = (D_MODEL, D_MODEL)
    return pl.pallas_call(
        body, name="merge_bwd", grid=(t // TILE_M,),
        in_specs=[_row_tile(D_MODEL)] + [_proj_tile(j) for j in (1, 2, 3)] + [_row_tile(D_MODEL)] * 3
        + [_resident(ws.shape), _resident(bst.shape), _resident(sq), _resident(sq), _resident(sq)],
        out_specs=[_row_tile(SHARD_W)] * 3 + [_row_tile(D_MODEL)] * 3
        + [pl.BlockSpec(ws.shape, lambda i: (0, 0, 0)), pl.BlockSpec(bst.shape, lambda i: (0, 0))],
        out_shape=[jax.ShapeDtypeStruct((t, SHARD_W), MXU_DTYPE)] * 3 + [jax.ShapeDtypeStruct((t, D_MODEL), F32)]
        + [jax.ShapeDtypeStruct((t, D_MODEL), MXU_DTYPE)] * 2
        + [jax.ShapeDtypeStruct(ws.shape, F32), jax.ShapeDtypeStruct(bst.shape, F32)],
        compiler_params=_params(("arbitrary",)),
    )(dx1, pj1, pj2, pj3, ret_raw, a, b, ws, bst, w_ro, w_so, w_o)


def _retention_bwd_call(pj0, drr, states, consts, n_seq, seq, side):
    cos_full, sin_signed, decay_in, zeta, xi = consts
    nb = seq // TILE_M
    cpb = TILE_M // CHUNK

    def body(pj_ref, do_ref, st_ref, cos_ref, sin_ref, d_ref, zeta_ref, xi_ref, dpj_ref, gstate):
        @pl.when(pl.program_id(1) == 0)
        def _():
            gstate[...] = jnp.zeros_like(gstate)

        for cc in reversed(range(cpb)):
            rows = slice(cc * CHUNK, (cc + 1) * CHUNK)
            cs, sn = cos_ref[rows, :], sin_ref[rows, :]
            for h in range(RET_HEADS):
                q, k, v = _qkv(pj_ref, rows, h)
                qt = (q * cs + _rot(q) * sn) * QK_SCALE
                kt = k * cs + _rot(k) * sn
                d_out = do_ref[rows, h * V_DIM:(h + 1) * V_DIM]
                prev = st_ref[cc, h]
                g = gstate[h]
                dec = d_ref[h]
                scores_d = _mm_nt(qt, kt) * dec
                dscores = _mm_nt(d_out, v) * dec
                kz = kt * zeta_ref[h]
                qx = qt * xi_ref[h]
                dv = _mm_tn(scores_d, d_out) + _mm(kz, g)
                dqt = (_mm(dscores, kt) + _mm_nt(d_out, prev) * xi_ref[h]) * QK_SCALE
                dkt = _mm_tn(dscores, qt) + _mm_nt(v, g) * zeta_ref[h]
                gstate[h] = _mm_tn(qx, d_out) + CHUNK_DECAY[h] * g
                dq = dqt * cs + _rot(dqt * sn)
                dk = dkt * cs + _rot(dkt * sn)
                dpj_ref[rows, h * QK_DIM:(h + 1) * QK_DIM] = dq.astype(dpj_ref.dtype)
                dpj_ref[rows, RET_HEADS * QK_DIM + h * QK_DIM:RET_HEADS * QK_DIM + (h + 1) * QK_DIM] = dk.astype(
                    dpj_ref.dtype)
                dpj_ref[rows, 2 * RET_HEADS * QK_DIM + h * V_DIM:2 * RET_HEADS * QK_DIM + (h + 1) * V_DIM] = dv.astype(
                    dpj_ref.dtype)

    row = lambda b, n: (b * nb + nb - 1 - n, 0)
    pos = lambda b, n: (nb - 1 - n, 0)
    return _compute_call(
        body, name="retention_bwd", grid=(n_seq, nb),
        in_specs=[pl.BlockSpec((None, TILE_M, SHARD_W), lambda b, n: (0,) + row(b, n)),
                  pl.BlockSpec((TILE_M, D_MODEL), row),
                  pl.BlockSpec((cpb, RET_HEADS, QK_DIM, V_DIM), lambda b, n: (b * nb + nb - 1 - n, 0, 0, 0)),
                  pl.BlockSpec((TILE_M, QK_DIM), pos), pl.BlockSpec((TILE_M, QK_DIM), pos),
                  _resident(decay_in.shape), _resident(zeta.shape), _resident(xi.shape)],
        out_specs=[pl.BlockSpec((TILE_M, SHARD_W), row)],
        out_shape=[jax.ShapeDtypeStruct((n_seq * seq, SHARD_W), MXU_DTYPE)],
        scratch_shapes=[pltpu.VMEM((RET_HEADS, QK_DIM, V_DIM), F32)],
        operands=(pj0, drr, states, cos_full, sin_signed, decay_in, zeta, xi),
        semantics=("arbitrary", "arbitrary"), side=side)


def _dx_call(dpj, x2d, dx1, g_mixer, wg_in, side):
    t = x2d.shape[0]

    def body(d0, d1, d2, d3, x_ref, dx1_ref, g_ref, w_ref, dx_ref, dg_ref):
        @pl.when(pl.program_id(0) == 0)
        def _():
            dg_ref[...] = jnp.zeros_like(dg_ref)

        dh = _mm_nt(d0[...], w_ref[0])
        for j, d_ref in enumerate((d1, d2, d3)):
            dh += _mm_nt(d_ref[...], w_ref[j + 1])
        xn, r = _rms(x_ref[...])
        dg_ref[...] += jnp.sum(dh * xn, axis=0, keepdims=True)
        dx_ref[...] = dx1_ref[...] + _rms_bwd(dh * g_ref[...], xn, r)

    return _compute_call(
        body, name="dx_bwd", grid=(t // DX_TILE,),
        in_specs=[_row_tile(SHARD_W, DX_TILE)] * N_CHIPS + [_row_tile(D_MODEL, DX_TILE)] * 2
        + [_resident((1, D_MODEL)), _resident(wg_in.shape)],
        out_specs=[_row_tile(D_MODEL, DX_TILE), pl.BlockSpec((1, D_MODEL), lambda i: (0, 0))],
        out_shape=[jax.ShapeDtypeStruct((t, D_MODEL), F32), jax.ShapeDtypeStruct((1, D_MODEL), F32)],
        operands=(*dpj, x2d, dx1, g_mixer, wg_in), semantics=("arbitrary",), side=side)


def _wgrad_call(name, lhs, rhs, block_n, out_cols=None, block_t=512, into=None, slot=0, n_slots=1, side=None):
    t, n = rhs.shape
    k = lhs.shape[1]
    steps = t // block_t
    out_cols = block_n if out_cols is None else out_cols
    per = block_n // out_cols
    first_block = slot * (n // block_n)

    def body(l_ref, r_ref, *rest):
        o_ref, acc = rest[-2:]
        @pl.when(pl.program_id(1) == 0)
        def _():
            acc[...] = jnp.zeros_like(acc)

        acc[...] += _mm_tn(l_ref[...], r_ref[...])

        @pl.when(pl.program_id(1) == steps - 1)
        def _():
            for s in range(per):
                o_ref[s] = acc[:, s * out_cols:(s + 1) * out_cols].astype(o_ref.dtype)

    res = _compute_call(
        body, name=name, grid=(n // block_n, steps),
        in_specs=[pl.BlockSpec((block_t, k), lambda j, i: (i, 0)), pl.BlockSpec((block_t, block_n), lambda j, i: (i, j))]
        + ([] if into is None else [ANY]),
        out_specs=[pl.BlockSpec((per, k, out_cols), lambda j, i: (first_block + j, 0, 0))],
        out_shape=[jax.ShapeDtypeStruct((n_slots * (n // out_cols), k, out_cols), COMM_DTYPE)],
        scratch_shapes=[pltpu.VMEM((k, block_n), F32)], aliases={} if into is None else {2: 0},
        operands=(lhs, rhs) if into is None else (lhs, rhs, into), semantics=("arbitrary", "arbitrary"), side=side)
    return res[0] if side is None else (res[0][0], res[1])


def _position():
    return lax.axis_index("x"), lax.axis_index("y"), lax.axis_index("c")


def _position_array():
    x, y, c = _position()
    return jnp.stack([2 * x + y, c]).astype(jnp.int32)


def _other_chip(x, y, k):
    return (1 - x if k & 2 else x), (1 - y if k & 1 else y)


def _plan_side(operands, out_shapes, plan, n_remote, aliases=None):
    def copies(ins, outs, send_sem, recv_sem, base=0):
        remote = plan(ins, outs)
        assert len(remote) == n_remote
        return [pltpu.make_async_remote_copy(src_ref=src, dst_ref=dst, send_sem=send_sem.at[base + i],
                                             recv_sem=recv_sem.at[base + i], device_id=dev, device_id_type=MESH)
                for i, (src, dst, dev) in enumerate(remote)]

    def start(*a):
        for cp in copies(*a):
            cp.start()

    def finish(*a):
        for cp in copies(*a):
            cp.wait()

    return _Side(operands, out_shapes, n_remote, start, finish, aliases)


def _place_cast_call(name, w, pos):
    rows, width = w.shape
    block_rows = min(256, rows)

    def body(pos_ref, w_ref, o_ref):
        o_ref[...] = w_ref[...].astype(o_ref.dtype)

    return pl.pallas_call(
        body, name=name,
        grid_spec=pltpu.PrefetchScalarGridSpec(
            num_scalar_prefetch=1, grid=(rows // block_rows,),
            in_specs=[pl.BlockSpec((block_rows, width), lambda i, pos: (i, 0))],
            out_specs=pl.BlockSpec((None, block_rows, width), lambda i, pos: (pos[0], i, 0))),
        out_shape=jax.ShapeDtypeStruct((N_CHIPS, rows, width), MXU_DTYPE),
        compiler_params=_params(("arbitrary",)),
    )(pos, w)


def _gather_side(placed):
    n = len(placed)

    def copies(kind, bufs, send_sem, recv_sem, base):
        x, y, c = _position()
        me = 2 * x + y
        made = []
        for i in range(n):
            hr = placed[i].shape[1] // 2
            for k in (1, 2, 3):
                px, py = _other_chip(x, y, k)
                chip, core, slot, dev = [(me, c, k - 1, (px, py, c)), (2 * px + py, c, 2 + k, (x, y, 1 - c)),
                                         (2 * px + py, 1 - c, 2 + k, (x, y, 1 - c))][kind]
                piece = bufs[i].at[chip, pl.ds(core * hr, hr)]
                made.append(pltpu.make_async_remote_copy(
                    src_ref=piece, dst_ref=piece, send_sem=send_sem.at[base + 6 * i + slot],
                    recv_sem=recv_sem.at[base + 6 * i + slot], device_id=dev, device_id_type=MESH))
        return made

    def start(ins, outs, send_sem, recv_sem, base=0):
        for cp in copies(0, outs, send_sem, recv_sem, base):
            cp.start()

    def finish(ins, outs, send_sem, recv_sem, base=0):
        first, onward = copies(0, outs, send_sem, recv_sem, base), copies(1, outs, send_sem, recv_sem, base)
        for landed, cp in zip(first, onward):
            landed.wait_recv()
            cp.start()
        for cp in copies(2, outs, send_sem, recv_sem, base):
            cp.wait_recv()
        for cp in first + onward:
            cp.wait_send()

    return _Side(placed, [jax.ShapeDtypeStruct(a.shape, a.dtype) for a in placed], 6 * n, start, finish,
                 {i: i for i in range(n)})


def _sum_call(name, pos, terms, rows, width, out_dtype, block_rows, out_rows=None, out_index=None):
    out_rows = rows if out_rows is None else out_rows
    out_index = (lambda i, pos, br: i) if out_index is None else out_index

    def body(pos_ref, *refs):
        acc = refs[0][...].astype(F32)
        for r in refs[1:-1]:
            acc = acc + r[...].astype(F32)
        refs[-1][...] = acc.astype(out_dtype)

    def spec(index):
        return pl.BlockSpec((block_rows, width), lambda i, pos: (index(i, pos, block_rows), 0))

    return pl.pallas_call(
        body, name=name,
        grid_spec=pltpu.PrefetchScalarGridSpec(
            num_scalar_prefetch=1, grid=(rows // block_rows,),
            in_specs=[spec(index) for _, index in terms], out_specs=spec(out_index)),
        out_shape=jax.ShapeDtypeStruct((out_rows, width), out_dtype),
        compiler_params=_params(("arbitrary",)),
    )(pos, *[arr for arr, _ in terms])


def _pair_stage(tag, grads, pos):
    n = len(grads)
    shapes = [(g.shape[0] // N_CHIPS, g.shape[1]) for g in grads]

    def plan(ins, outs):
        x, y, c = _position()
        remote = []
        for i, (r, _) in enumerate(shapes):
            hr = r // 2
            for j in range(N_CHIPS):
                remote.append((ins[i].at[pl.ds(j * r + (1 - c) * hr, hr)], outs[i].at[pl.ds(j * hr, hr)],
                               (x, y, 1 - c)))
        return remote

    pair = _exchange_call(f"grad_pair_exchange_{tag}", _plan_side(
        grads, [jax.ShapeDtypeStruct((N_CHIPS * (r // 2), w), COMM_DTYPE) for r, w in shapes], plan, N_CHIPS * n))

    def own_half(r):
        def index(s, pos, br):
            per = (r // 2) // br
            return (s // per) * (r // br) + pos[1] * per + s % per
        return index

    return [_sum_call(f"grad_pair_sum_{tag}{i}", pos, [(grads[i], own_half(r)), (pair[i], lambda s, pos, br: s)],
                      N_CHIPS * (r // 2), w, COMM_DTYPE, min(256, r // 2)) for i, (r, w) in enumerate(shapes)]


def _chip_side(pair_sums):
    halves = [(p.shape[0] // N_CHIPS, p.shape[1]) for p in pair_sums]

    def plan(ins, outs):
        x, y, c = _position()
        remote = []
        for i, (hr, _) in enumerate(halves):
            for k in (1, 2, 3):
                px, py = _other_chip(x, y, k)
                remote.append((ins[i].at[pl.ds((2 * px + py) * hr, hr)], outs[i].at[pl.ds((k - 1) * hr, hr)],
                               (px, py, c)))
        return remote

    return _plan_side(pair_sums, [jax.ShapeDtypeStruct((3 * hr, w), COMM_DTYPE) for hr, w in halves], plan,
                      3 * len(pair_sums))


def _finish_stage(tag, pair_sums, chip, pos, extra=None):
    n = len(pair_sums)
    halves = [(p.shape[0] // N_CHIPS, p.shape[1]) for p in pair_sums]

    def slab(k, hr):
        return lambda s, pos, br: k * (hr // br) + s

    reduced = []
    for i, (hr, w) in enumerate(halves):
        terms = [(pair_sums[i], lambda s, pos, br, hr=hr: pos[0] * (hr // br) + s)]
        terms += [(chip[i], slab(k, hr)) for k in range(3)]
        reduced.append(_sum_call(f"grad_chip_sum_{tag}{i}", pos, terms, hr, w, F32, min(256, hr), out_rows=2 * hr,
                                 out_index=lambda s, pos, br, hr=hr: pos[1] * (hr // br) + s))

    def plan(ins, outs):
        x, y, c = _position()
        remote = []
        for i, (hr, _) in enumerate(halves):
            mine = outs[i].at[pl.ds(c * hr, hr)]
            remote.append((mine, mine, (x, y, 1 - c)))
        return remote

    swap = _plan_side(reduced, [jax.ShapeDtypeStruct((2 * hr, w), F32) for hr, w in halves], plan, n,
                      {i: i for i in range(n)})
    if extra is None:
        return _exchange_call(f"grad_half_exchange_{tag}", swap)
    res = _exchange_call(f"grad_half_exchange_{tag}", _join_sides(swap, extra))
    return res[:n], res[n:]


def _small_gather_side(parts):
    n = len(parts)

    def copies(ins, outs, send_sem, recv_sem, base):
        x, y, c = _position()
        made = []
        for i in range(n):
            mine = outs[i].at[4 * x + 2 * y + c]
            made.append(pltpu.make_async_copy(ins[i], mine, send_sem.at[base + 8 * i + 7]))
            for d in range(1, 8):
                px, py = _other_chip(x, y, d >> 1)
                made.append(pltpu.make_async_remote_copy(
                    src_ref=ins[i], dst_ref=mine, send_sem=send_sem.at[base + 8 * i + d - 1],
                    recv_sem=recv_sem.at[base + 8 * i + d - 1], device_id=(px, py, 1 - c if d & 1 else c),
                    device_id_type=MESH))
        return made

    def start(ins, outs, send_sem, recv_sem, base=0):
        for cp in copies(ins, outs, send_sem, recv_sem, base):
            cp.start()

    def finish(ins, outs, send_sem, recv_sem, base=0):
        for cp in copies(ins, outs, send_sem, recv_sem, base):
            cp.wait()

    return _Side(parts, [jax.ShapeDtypeStruct((8,) + a_.shape, F32) for a_ in parts], 8 * n, start, finish)


def _adamw(w, g, m, v):
    m = ADAM_B1 * m + (1.0 - ADAM_B1) * g
    v = ADAM_B2 * v + (1.0 - ADAM_B2) * (g * g)
    m_hat = m / (1.0 - ADAM_B1 ** ADAM_STEP)
    v_hat = v / (1.0 - ADAM_B2 ** ADAM_STEP)
    delta = -ADAM_LR * (m_hat / (jnp.sqrt(v_hat) + ADAM_EPS) + ADAM_WD * w)
    return delta, m, v


def _adamw_call(name, w, g, m, v):
    rows, width = w.shape
    block_rows = min(256, rows)

    def body(w_ref, g_ref, m_ref, v_ref, d_out, m_out, v_out):
        d_out[...], m_out[...], v_out[...] = _adamw(w_ref[...], g_ref[...], m_ref[...], v_ref[...])

    spec = pl.BlockSpec((block_rows, width), lambda i: (i, 0))
    return pl.pallas_call(
        body, name=name, grid=(rows // block_rows,), in_specs=[spec] * 4, out_specs=[spec] * 3,
        out_shape=[jax.ShapeDtypeStruct(w.shape, F32)] * 3,
        compiler_params=_params(("arbitrary",)),
    )(w, g, m, v)


def _small_adamw_call(gathered, weights, moments_m, moments_v):
    n = len(weights)

    def body(*refs):
        all_refs, refs = refs[:n + 1], refs[n + 1:]
        w_refs, m_refs, v_refs, outs = refs[:n], refs[n:2 * n], refs[2 * n:3 * n], refs[3 * n:]

        def total(ref):
            acc = ref[0]
            for d in range(1, 8):
                acc = acc + ref[d]
            return acc

        outs[0][...] = total(all_refs[n])
        for i in range(n):
            g = total(all_refs[i])
            if i == 0:
                row = lax.broadcasted_iota(jnp.int32, g.shape, 0)
                col = lax.broadcasted_iota(jnp.int32, g.shape, 1)
                g = jnp.where((row % CHUNK) >= col, g, 0.0)
            g_out, d_out, m_out, v_out = outs[1 + 4 * i:5 + 4 * i]
            g_out[...] = g
            d_out[...], m_out[...], v_out[...] = _adamw(w_refs[i][...], g, m_refs[i][...], v_refs[i][...])

    out_shape = [jax.ShapeDtypeStruct(LOSS_TILE, F32)]
    for w in weights:
        out_shape += [jax.ShapeDtypeStruct(w.shape, F32)] * 4
    res = pl.pallas_call(
        body, name="small_adamw", out_shape=out_shape,
        compiler_params=pltpu.CompilerParams(vmem_limit_bytes=VMEM_LIMIT),
    )(*gathered, *weights, *moments_m, *moments_v)
    return res[0], [res[1 + 4 * i:5 + 4 * i] for i in range(n)]


def kernel(x, p, w_in, w_ret_out, w_sgu_out, w_out, sgu_ws, sgu_bs, w_ple_gate, w_ple_proj, g_mixer, g_ple, g_final, loss_target, m_w_in, m_w_ret_out, m_w_sgu_out, m_w_out, m_sgu_ws, m_sgu_bs, m_w_ple_gate, m_w_ple_proj, m_g_mixer, m_g_ple, m_g_final, v_w_in, v_w_ret_out, v_w_sgu_out, v_w_out, v_sgu_ws, v_sgu_bs, v_w_ple_gate, v_w_ple_proj, v_g_mixer, v_g_ple, v_g_final):
    n_seq, seq, _ = x.shape
    t = n_seq * seq
    x2d = x.reshape(t, D_MODEL)
    p2d = p.reshape(t, PLE_DIM)
    target = loss_target.reshape(t, D_MODEL)
    big = [w_in[0], w_ret_out[0], w_sgu_out[0], w_out[0], w_ple_gate[0], w_ple_proj[0]]
    big_m = [m_w_in[0], m_w_ret_out[0], m_w_sgu_out[0], m_w_out[0], m_w_ple_gate[0], m_w_ple_proj[0]]
    big_v = [v_w_in[0], v_w_ret_out[0], v_w_sgu_out[0], v_w_out[0], v_w_ple_gate[0], v_w_ple_proj[0]]

    pos = _position_array()
    placed = [_place_cast_call(f"place_weight_{i}", w, pos) for i, w in enumerate(big)]
    ws = sgu_ws[0]
    bst = sgu_bs[0].T
    consts = _retention_consts(seq)

    (h, pj, wg_in), gathered = _proj_call(x2d, g_mixer, placed[0], pos, _gather_side(placed[1:]))
    pj0 = pj1 = pj2 = pj3 = pj
    w_ro, w_so, w_o, w_pg = (w.reshape(D_MODEL, D_MODEL) for w in gathered[:4])
    w_pp = gathered[4].transpose(1, 0, 2).reshape(PLE_DIM, D_MODEL)
    ret_raw, states = _retention_fwd_call(pj0, consts, n_seq, seq)
    x1, ret, sgu, merged, a, b = _merge_fwd_call(pj1, pj2, pj3, ret_raw, x2d, ws, bst, w_ro, w_so, w_o)
    dx1, hp, dz, dpp, loss, dg_ple, dg_final = _ple_call(x1, p2d, target, g_ple, g_final.reshape(1, D_MODEL),
                                                         w_pg, w_pp)
    dpj1, dpj2, dpj3, drr, da, db, dws, dbst = _merge_bwd_call(dx1, pj1, pj2, pj3, ret_raw, a, b, ws, bst,
                                                               w_ro, w_so, w_o)

    small_shapes = [(SGU_GROUPS * CHUNK, CHUNK), (SGU_GROUPS, CHUNK), (1, D_MODEL), (1, D_MODEL), (1, D_MODEL)]
    as_small = lambda arrays: [None if a_ is None else a_.reshape(s_) for a_, s_ in zip(arrays, small_shapes)]
    early = as_small([dws, dbst.T, None, dg_ple, dg_final])
    g_ret_out, early_all = _wgrad_call("wgrad_ret_out", ret, da, D_MODEL,
                                       side=_small_gather_side([early[0], early[1], early[3], early[4], loss]))
    rows_of = lambda g: g.reshape(g.shape[0] * g.shape[1], g.shape[2])
    tail_grads = [
        rows_of(g_ret_out),
        rows_of(_wgrad_call("wgrad_sgu_out", sgu, db, D_MODEL)),
        rows_of(_wgrad_call("wgrad_out", merged, dx1, D_MODEL)),
        rows_of(_wgrad_call("wgrad_ple_gate", hp, dz, D_MODEL)),
        rows_of(_wgrad_call("wgrad_ple_proj", p2d, dpp, D_MODEL, out_cols=PLE_DIM)),
    ]
    tail_sums = _pair_stage("tail", tail_grads, pos)
    (dpj0,), tail_chip = _retention_bwd_call(pj0, drr, states, consts, n_seq, seq, _chip_side(tail_sums))
    in_grad = None
    for j, d in enumerate((dpj0, dpj1, dpj2, dpj3)):
        in_grad = _wgrad_call(f"wgrad_in_{j}", h, d, SHARD_W, into=in_grad, slot=j, n_slots=N_CHIPS)
    in_sums = _pair_stage("in", [rows_of(in_grad)], pos)
    (dx, dg_mixer), in_chip = _dx_call((dpj0, dpj1, dpj2, dpj3), x2d, dx1, g_mixer, wg_in, _chip_side(in_sums))
    g_in, (mixer_all,) = _finish_stage("in", in_sums, in_chip, pos, _small_gather_side([dg_mixer]))
    g_big = g_in + _finish_stage("tail", tail_sums, tail_chip, pos)
    upd = [_adamw_call(f"adamw_{i}", big[i], g_big[i], big_m[i], big_v[i]) for i in range(len(big))]

    small_g = [early_all[0], early_all[1], mixer_all, early_all[2], early_all[3], early_all[4]]
    total, small = _small_adamw_call(small_g, as_small([sgu_ws, sgu_bs, g_mixer, g_ple, g_final]),
                                     as_small([m_sgu_ws, m_sgu_bs, m_g_mixer, m_g_ple, m_g_final]),
                                     as_small([v_sgu_ws, v_sgu_bs, v_g_mixer, v_g_ple, v_g_final]))
    out_small_shapes = [sgu_ws.shape, sgu_bs.shape, g_mixer.shape, g_ple.shape, g_final.shape]

    def ordered(big_list, kind):
        w_in_, w_ro_, w_so_, w_o_, w_pg_, w_pp_ = [b_[None] for b_ in big_list]
        s_ws, s_bs, s_gm, s_gp, s_gf = [small[i][kind].reshape(s) for i, s in enumerate(out_small_shapes)]
        return [w_in_, w_ro_, w_so_, w_o_, s_ws, s_bs, w_pg_, w_pp_, s_gm, s_gp, s_gf]

    out = [total[0, 0], dx.reshape(x.shape)]
    out += ordered(g_big, 0)
    out += ordered([u[0] for u in upd], 1)
    out += ordered([u[1] for u in upd], 2)
    out += ordered([u[2] for u in upd], 3)
    return tuple(out)
```

```python
import functools
import math

import numpy as np
import jax
import jax.numpy as jnp
from jax import lax
from jax.experimental import pallas as pl
from jax.experimental.pallas import tpu as pltpu

F32 = jnp.float32
MXU_DTYPE = jnp.bfloat16
COMM_DTYPE = jnp.bfloat16

D_MODEL = 1024
RET_HEADS = 4
QK_DIM = 128
V_DIM = 256
CHUNK = 128
SGU_GROUPS = 4
GROUP_DIM = 256
PLE_DIM = 256
N_CHIPS = 4
SHARD_W = 2048
ROPE_BASE = 10000.0
NORM_EPS = 1e-6
GN_EPS = 1e-5
QK_SCALE = QK_DIM ** -0.5
SQRT_HALF = math.sqrt(0.5)
INV_SQRT_2PI = 1.0 / math.sqrt(2.0 * math.pi)

ADAM_LR = 0.001
ADAM_B1 = 0.9
ADAM_B2 = 0.999
ADAM_EPS = 1e-08
ADAM_WD = 0.01
ADAM_STEP = 10

TILE_M = 256
LOSS_TILE = (8, 128)
LOAD_PARTS = 16
PROJ_TILE = 512
RET_TILE = 512
DX_TILE = 512
VMEM_LIMIT = 56 * 1024 * 1024
MESH = pl.DeviceIdType.MESH
ANY = pl.BlockSpec(memory_space=pl.ANY)

CHUNK_DECAY = tuple(
    float(np.exp(np.float32(CHUNK) * np.log(np.float32(1.0 - 2.0 ** (-5.0 - h))))) for h in range(RET_HEADS))


def _mm(a, b):
    return jnp.dot(a.astype(MXU_DTYPE), b.astype(MXU_DTYPE), preferred_element_type=F32)


def _mm_nt(a, b):
    return lax.dot_general(a.astype(MXU_DTYPE), b.astype(MXU_DTYPE), (((1,), (1,)), ((), ())),
                           preferred_element_type=F32)


def _mm_tn(a, b):
    return lax.dot_general(a.astype(MXU_DTYPE), b.astype(MXU_DTYPE), (((0,), (0,)), ((), ())),
                           preferred_element_type=F32)


def _mean(x):
    return jnp.mean(x, axis=-1, keepdims=True)


def _sigmoid(x):
    return jax.nn.sigmoid(x)


def _silu_and_grad(x):
    s = _sigmoid(x)
    return x * s, s * (1.0 + x * (1.0 - s))


def _gelu(x):
    return 0.5 * x * (1.0 + lax.erf(x * SQRT_HALF))


def _gelu_grad(x):
    return 0.5 * (1.0 + lax.erf(x * SQRT_HALF)) + x * jnp.exp(-0.5 * x * x) * INV_SQRT_2PI


def _unit_norm(x, eps):
    xc = x - _mean(x)
    rstd = lax.rsqrt(_mean(xc * xc) + eps)
    return xc * rstd, rstd


def _unit_norm_bwd(dn, n, rstd):
    return rstd * (dn - _mean(dn) - n * _mean(dn * n))


def _rms(x):
    r = lax.rsqrt(_mean(x * x) + NORM_EPS)
    return x * r, r


def _rms_bwd(dxn, xn, r):
    return r * (dxn - xn * _mean(dxn * xn))


def _rot(x):
    return pltpu.roll(x, QK_DIM // 2, 1)


def _params(semantics, **kw):
    return pltpu.CompilerParams(dimension_semantics=semantics, vmem_limit_bytes=VMEM_LIMIT, **kw)


def _row_tile(width, tm=TILE_M):
    return pl.BlockSpec((tm, width), lambda i: (i, 0))


def _proj_tile(j):
    return pl.BlockSpec((None, TILE_M, SHARD_W), lambda i: (j, i, 0))


def _resident(shape):
    nd = len(shape)
    return pl.BlockSpec(shape, lambda *_: (0,) * nd, pipeline_mode=pl.Buffered(1))


def _causal_ws(ws_ref):
    row = lax.broadcasted_iota(jnp.int32, (CHUNK, CHUNK), 0)
    col = lax.broadcasted_iota(jnp.int32, (CHUNK, CHUNK), 1)
    return [jnp.where(row >= col, ws_ref[g], 0.0).astype(MXU_DTYPE) for g in range(SGU_GROUPS)]


def _heads(x, width):
    return [x[:, h * width:(h + 1) * width] for h in range(x.shape[1] // width)]


def _branch_forward(pj1, pj2, ret_raw, wsc, bst):
    rg, su = pj1[:, :D_MODEL], pj1[:, D_MODEL:]
    sv, sg = pj2[:, :D_MODEL], pj2[:, D_MODEL:]
    rn_parts, rstd_parts = zip(*[_unit_norm(r, GN_EPS) for r in _heads(ret_raw, V_DIM)])
    rn = jnp.concatenate(rn_parts, axis=1)
    sil_rg, dsil_rg = _silu_and_grad(rg)
    ret = rn * sil_rg
    u = _gelu(su)
    vn, rstd_v = _unit_norm(_gelu(sv), GN_EPS)
    rows = []
    for cc in range(pj1.shape[0] // CHUNK):
        cols = []
        for g in range(SGU_GROUPS):
            blk = vn[cc * CHUNK:(cc + 1) * CHUNK, g * GROUP_DIM:(g + 1) * GROUP_DIM]
            cols.append(_mm(wsc[g], blk) + bst[:, g:g + 1])
        rows.append(jnp.concatenate(cols, axis=1))
    mixed = jnp.concatenate(rows, axis=0)
    sil_sg, dsil_sg = _silu_and_grad(sg)
    sgu = u * mixed * sil_sg
    return dict(rg=rg, su=su, sv=sv, sg=sg, rn=rn, rstd_r=rstd_parts, sil_rg=sil_rg, dsil_rg=dsil_rg, ret=ret,
                u=u, vn=vn, rstd_v=rstd_v, mixed=mixed, sil_sg=sil_sg, dsil_sg=dsil_sg, sgu=sgu)


class _Side:
    def __init__(self, operands, out_shapes, n_sems, start, finish, aliases=None):
        self.operands, self.out_shapes, self.n_sems = list(operands), list(out_shapes), n_sems
        self.start, self.finish, self.aliases = start, finish, dict(aliases or {})


def _join_sides(*sides):
    spans, a, b, s = [], 0, 0, 0
    for side in sides:
        spans.append((a, b, s))
        a, b, s = a + len(side.operands), b + len(side.out_shapes), s + side.n_sems

    def run(which):
        def go(ins, outs, send_sem, recv_sem, base=0):
            for side, (a0, b0, s0) in zip(sides, spans):
                getattr(side, which)(ins[a0:a0 + len(side.operands)], outs[b0:b0 + len(side.out_shapes)],
                                     send_sem, recv_sem, base + s0)
        return go

    aliases = {a0 + i: b0 + o for side, (a0, b0, _) in zip(sides, spans) for i, o in side.aliases.items()}
    return _Side([x for side in sides for x in side.operands], [x for side in sides for x in side.out_shapes], s,
                 run("start"), run("finish"), aliases)


def _compute_call(body, *, name, grid, in_specs, out_specs, out_shape, operands, semantics, scratch_shapes=(),
                  side=None, prefetch=None, aliases=None, side_start=None):
    n_pre = 0 if prefetch is None else 1
    pre = () if prefetch is None else (prefetch,)

    def spec(in_specs, out_specs, scratch):
        return pltpu.PrefetchScalarGridSpec(num_scalar_prefetch=n_pre, grid=grid, in_specs=in_specs,
                                            out_specs=out_specs, scratch_shapes=scratch)

    if side is None:
        return pl.pallas_call(body, name=name, grid_spec=spec(in_specs, out_specs, list(scratch_shapes)),
                              out_shape=out_shape,
                              input_output_aliases={n_pre + a: b for a, b in (aliases or {}).items()},
                              compiler_params=_params(semantics))(*pre, *operands)
    n_in, n_out, s_in, s_out = len(operands), len(out_shape), len(side.operands), len(side.out_shapes)

    def carrier(*refs):
        pre_refs, refs = refs[:n_pre], refs[n_pre:]
        ins, refs = refs[:n_in], refs[n_in:]
        side_ins, refs = refs[:s_in], refs[s_in:]
        outs, refs = refs[:n_out], refs[n_out:]
        side_outs, refs = refs[:s_out], refs[s_out:]
        scratch, (send_sem, recv_sem) = refs[:-2], refs[-2:]
        ids = [pl.program_id(a) for a in range(len(grid))]
        at = (0,) * len(grid) if side_start is None else side_start
        first = functools.reduce(jnp.logical_and, [i == a for i, a in zip(ids, at)])
        last = functools.reduce(jnp.logical_and, [i == g - 1 for i, g in zip(ids, grid)])

        @pl.when(first)
        def _():
            side.start(side_ins, side_outs, send_sem, recv_sem)

        body(*pre_refs, *ins, *outs, *scratch)

        @pl.when(last)
        def _():
            side.finish(side_ins, side_outs, send_sem, recv_sem)

    all_aliases = {n_pre + a: b for a, b in (aliases or {}).items()}
    all_aliases.update({n_pre + n_in + a: n_out + b for a, b in side.aliases.items()})
    res = pl.pallas_call(
        carrier, name=name,
        grid_spec=spec(list(in_specs) + [ANY] * s_in, list(out_specs) + [ANY] * s_out,
                       list(scratch_shapes) + [pltpu.SemaphoreType.DMA((side.n_sems,))] * 2),
        out_shape=list(out_shape) + side.out_shapes, input_output_aliases=all_aliases,
        compiler_params=_params(semantics, has_side_effects=True),
    )(*pre, *operands, *side.operands)
    return res[:n_out], res[n_out:]


def _exchange_call(name, side):
    s_in = len(side.operands)

    def body(*refs):
        ins, outs = refs[:s_in], refs[s_in:s_in + len(side.out_shapes)]
        send_sem, recv_sem = refs[s_in + len(side.out_shapes):]
        side.start(ins, outs, send_sem, recv_sem)
        side.finish(ins, outs, send_sem, recv_sem)

    return pl.pallas_call(
        body, name=name, in_specs=[ANY] * s_in, out_specs=[ANY] * len(side.out_shapes), out_shape=side.out_shapes,
        scratch_shapes=[pltpu.SemaphoreType.DMA((side.n_sems,))] * 2, input_output_aliases=side.aliases,
        compiler_params=pltpu.CompilerParams(has_side_effects=True),
    )(*side.operands)


def _proj_call(x2d, g_mixer, placed_in, pos, side):
    t = x2d.shape[0]
    nt = t // PROJ_TILE
    hr = placed_in.shape[1] // 2

    def body(pos_ref, x_ref, g_ref, win_ref, h_ref, pj_ref, w_ref, w_vmem, h_all, loc_sem, send_sem, recv_sem):
        k, i = pl.program_id(0), pl.program_id(1)
        x, y, c = _position()
        me = 2 * x + y

        def copy(slot, chip, core, dev):
            piece = w_ref.at[chip, pl.ds(core * hr, hr)]
            return pltpu.make_async_remote_copy(src_ref=piece, dst_ref=piece, send_sem=send_sem.at[slot],
                                                recv_sem=recv_sem.at[slot], device_id=dev, device_id_type=MESH)

        def first_hop(kk):
            px, py = _other_chip(x, y, kk)
            return copy(kk - 1, me, c, (px, py, c))

        def relay():
            source = jnp.bitwise_xor(me, 2 - c)
            return copy(2, source, c, (jnp.bitwise_xor(x, c), jnp.bitwise_xor(y, 1 - c), c))

        def onward(kk, core):
            px, py = _other_chip(x, y, kk)
            return copy(2 + kk, 2 * px + py, core, (x, y, 1 - c))

        def loads(kk):
            px, py = _other_chip(x, y, kk)
            rows = w_vmem.shape[1] // LOAD_PARTS
            return [pltpu.make_async_copy(w_ref.at[2 * px + py, pl.ds(r * rows, rows)],
                                          w_vmem.at[kk % 2, pl.ds(r * rows, rows)], loc_sem.at[r])
                    for r in range(LOAD_PARTS)]

        @pl.when(jnp.logical_and(k == 0, i == 0))
        def _():
            for kk in (1, 2):
                first_hop(kk).start()
            for cp in loads(0):
                cp.start()
            for cp in loads(0):
                cp.wait()

        for kk in (1, 2, 3):
            @pl.when(jnp.logical_and(k == kk - 1, i == nt - 1))
            def _(kk=kk):
                if kk == 1:
                    first_hop(1).wait_recv()
                    first_hop(2).wait_recv()
                    relay().start()
                if kk == 3:
                    relay().wait_recv()
                onward(kk, c).start()
                onward(kk, 1 - c).wait_recv()
                for cp in loads(kk):
                    cp.start()

            @pl.when(jnp.logical_and(k == kk, i == 0))
            def _(kk=kk):
                for cp in loads(kk):
                    cp.wait()

        rows = pl.ds(pl.multiple_of(i * PROJ_TILE, PROJ_TILE), PROJ_TILE)

        @pl.when(k == 0)
        def _():
            xn, _ = _rms(x_ref[...])
            h = (xn * g_ref[...]).astype(MXU_DTYPE)
            h_all[rows, :] = h
            h_ref[...] = h

        pj_ref[...] = jnp.dot(h_all[rows, :], w_vmem[k % 2], preferred_element_type=F32)

        @pl.when(jnp.logical_and(k == N_CHIPS - 1, i == nt - 1))
        def _():
            for cp in [first_hop(1), first_hop(2), relay()] + [onward(kk, c) for kk in (1, 2, 3)]:
                cp.wait_send()

    parked = lambda k, i, pos: (jnp.where(k == 0, i, nt - 1), 0)
    return _compute_call(
        body, name="proj_fwd", grid=(N_CHIPS, nt),
        in_specs=[pl.BlockSpec((PROJ_TILE, D_MODEL), parked), pl.BlockSpec((1, D_MODEL), lambda k, i, pos: (0, 0)), ANY],
        out_specs=[pl.BlockSpec((PROJ_TILE, D_MODEL), lambda k, i, pos: (jnp.where(k == 0, i, nt), 0)),
                   pl.BlockSpec((None, PROJ_TILE, SHARD_W), lambda k, i, pos: (jnp.bitwise_xor(pos[0], k), i, 0)),
                   ANY],
        out_shape=[jax.ShapeDtypeStruct((t + PROJ_TILE, D_MODEL), MXU_DTYPE),
                   jax.ShapeDtypeStruct((N_CHIPS, t, SHARD_W), F32),
                   jax.ShapeDtypeStruct(placed_in.shape, placed_in.dtype)],
        scratch_shapes=[pltpu.VMEM((2,) + placed_in.shape[1:], placed_in.dtype), pltpu.VMEM((t, D_MODEL), MXU_DTYPE),
                        pltpu.SemaphoreType.DMA((LOAD_PARTS,)), pltpu.SemaphoreType.DMA((6,)),
                        pltpu.SemaphoreType.DMA((6,))],
        operands=(x2d, g_mixer, placed_in), semantics=("arbitrary", "arbitrary"), side=side, prefetch=pos,
        aliases={2: 2}, side_start=(2, 0))


def _retention_consts(seq):
    half = QK_DIM // 2
    inv = ROPE_BASE ** (-jnp.arange(half, dtype=F32) / half)
    ang = jnp.arange(seq, dtype=F32)[:, None] * inv[None, :]
    cos, sin = jnp.cos(ang), jnp.sin(ang)
    cos_full = jnp.concatenate([cos, cos], axis=1)
    sin_signed = jnp.concatenate([-sin, sin], axis=1)
    log_g = jnp.log(1.0 - 2.0 ** (-5.0 - jnp.arange(RET_HEADS, dtype=F32)))
    idx = jnp.arange(CHUNK, dtype=F32)
    diff = idx[:, None] - idx[None, :]
    decay_in = jnp.where(diff[None] >= 0, jnp.exp(jnp.maximum(diff, 0.0)[None] * log_g[:, None, None]), 0.0)
    zeta = jnp.exp((CHUNK - 1.0 - idx)[None, :] * log_g[:, None])
    xi = jnp.exp((idx + 1.0)[None, :] * log_g[:, None])
    zeta = jnp.broadcast_to(zeta[:, :, None], (RET_HEADS, CHUNK, QK_DIM))
    xi = jnp.broadcast_to(xi[:, :, None], (RET_HEADS, CHUNK, QK_DIM))
    return cos_full, sin_signed, decay_in, zeta, xi


def _qkv(pj_ref, rows, h):
    q = pj_ref[rows, h * QK_DIM:(h + 1) * QK_DIM]
    k = pj_ref[rows, RET_HEADS * QK_DIM + h * QK_DIM:RET_HEADS * QK_DIM + (h + 1) * QK_DIM]
    v = pj_ref[rows, 2 * RET_HEADS * QK_DIM + h * V_DIM:2 * RET_HEADS * QK_DIM + (h + 1) * V_DIM]
    return q, k, v


def _retention_fwd_call(pj0, consts, n_seq, seq):
    cos_full, sin_signed, decay_in, zeta, xi = consts
    nb = seq // RET_TILE
    cpb = RET_TILE // CHUNK

    def body(pj_ref, cos_ref, sin_ref, d_ref, zeta_ref, xi_ref, o_ref, st_ref, state):
        @pl.when(pl.program_id(1) == 0)
        def _():
            state[...] = jnp.zeros_like(state)

        for cc in range(cpb):
            rows = slice(cc * CHUNK, (cc + 1) * CHUNK)
            cs, sn = cos_ref[rows, :], sin_ref[rows, :]
            for h in range(RET_HEADS):
                q, k, v = _qkv(pj_ref, rows, h)
                qt = (q * cs + _rot(q) * sn) * QK_SCALE
                kt = k * cs + _rot(k) * sn
                prev = state[h]
                st_ref[cc, h] = prev.astype(st_ref.dtype)
                scores = _mm_nt(qt, kt) * d_ref[h]
                o_ref[rows, h * V_DIM:(h + 1) * V_DIM] = _mm(scores, v) + _mm(qt * xi_ref[h], prev)
                state[h] = _mm_tn(kt * zeta_ref[h], v) + CHUNK_DECAY[h] * prev

    row = lambda b, n: (b * nb + n, 0)
    pos = lambda b, n: (n, 0)
    return pl.pallas_call(
        body, name="retention_fwd", grid=(n_seq, nb),
        in_specs=[pl.BlockSpec((None, RET_TILE, SHARD_W), lambda b, n: (0,) + row(b, n)),
                  pl.BlockSpec((RET_TILE, QK_DIM), pos),
                  pl.BlockSpec((RET_TILE, QK_DIM), pos), _resident(decay_in.shape), _resident(zeta.shape),
                  _resident(xi.shape)],
        out_specs=[pl.BlockSpec((RET_TILE, D_MODEL), row),
                   pl.BlockSpec((cpb, RET_HEADS, QK_DIM, V_DIM), lambda b, n: (b * nb + n, 0, 0, 0))],
        out_shape=[jax.ShapeDtypeStruct((n_seq * seq, D_MODEL), F32),
                   jax.ShapeDtypeStruct((n_seq * seq // CHUNK, RET_HEADS, QK_DIM, V_DIM), MXU_DTYPE)],
        scratch_shapes=[pltpu.VMEM((RET_HEADS, QK_DIM, V_DIM), F32)],
        compiler_params=_params(("arbitrary", "arbitrary")),
    )(pj0, cos_full, sin_signed, decay_in, zeta, xi)


def _merge_fwd_call(pj1, pj2, pj3, ret_raw, x2d, ws, bst, w_ro, w_so, w_o):
    t = x2d.shape[0]

    def body(pj1_ref, pj2_ref, pj3_ref, rr_ref, x_ref, ws_ref, bst_ref, wro_ref, wso_ref, wo_ref,
             x1_ref, ret_ref, sgu_ref, mg_ref, a_ref, b_ref):
        f = _branch_forward(pj1_ref[...], pj2_ref[...], rr_ref[...], _causal_ws(ws_ref), bst_ref[...])
        ret = f["ret"].astype(MXU_DTYPE)
        sgu = f["sgu"].astype(MXU_DTYPE)
        a = jnp.dot(ret, wro_ref[...], preferred_element_type=F32)
        b = jnp.dot(sgu, wso_ref[...], preferred_element_type=F32)
        pj3 = pj3_ref[...]
        merged = (_sigmoid(pj3[:, :D_MODEL]) * a + _sigmoid(pj3[:, D_MODEL:]) * b).astype(MXU_DTYPE)
        x1_ref[...] = x_ref[...] + jnp.dot(merged, wo_ref[...], preferred_element_type=F32)
        ret_ref[...] = ret
        sgu_ref[...] = sgu
        mg_ref[...] = merged
        a_ref[...] = a
        b_ref[...] = b

    sq = (D_MODEL, D_MODEL)
    return pl.pallas_call(
        body, name="merge_fwd", grid=(t // TILE_M,),
        in_specs=[_proj_tile(j) for j in (1, 2, 3)] + [_row_tile(D_MODEL)] * 2
        + [_resident(ws.shape), _resident(bst.shape), _resident(sq), _resident(sq), _resident(sq)],
        out_specs=[_row_tile(D_MODEL)] * 6,
        out_shape=[jax.ShapeDtypeStruct((t, D_MODEL), F32)] + [jax.ShapeDtypeStruct((t, D_MODEL), MXU_DTYPE)] * 3
        + [jax.ShapeDtypeStruct((t, D_MODEL), F32)] * 2,
        compiler_params=_params(("arbitrary",)),
    )(pj1, pj2, pj3, ret_raw, x2d, ws, bst, w_ro, w_so, w_o)


def _ple_call(x1, p2d, target, g_ple, g_final, w_pg, w_pp):
    t = x1.shape[0]

    def body(x1_ref, p_ref, t_ref, gp_ref, gf_ref, wpg_ref, wpp_ref,
             dx1_ref, hp_ref, dz_ref, dpp_ref, loss_ref, dgp_ref, dgf_ref):
        @pl.when(pl.program_id(0) == 0)
        def _():
            loss_ref[...] = jnp.zeros_like(loss_ref)
            dgp_ref[...] = jnp.zeros_like(dgp_ref)
            dgf_ref[...] = jnp.zeros_like(dgf_ref)

        x1v = x1_ref[...]
        xn1, r1 = _rms(x1v)
        hp = (xn1 * gp_ref[...]).astype(MXU_DTYPE)
        gate = _sigmoid(jnp.dot(hp, wpg_ref[...], preferred_element_type=F32))
        pp = jnp.dot(p_ref[...].astype(MXU_DTYPE), wpp_ref[...], preferred_element_type=F32)
        x2 = x1v + gate * pp
        xn2, r2 = _rms(x2)
        err = xn2 * gf_ref[...] - t_ref[...]
        loss_ref[...] += (0.5 / D_MODEL) * jnp.sum(jnp.sum(err * err, axis=1, keepdims=True), axis=0, keepdims=True)
        dy = err * (1.0 / D_MODEL)
        dgf_ref[...] += jnp.sum(dy * xn2, axis=0, keepdims=True)
        dx2 = _rms_bwd(dy * gf_ref[...], xn2, r2)
        dpp = (dx2 * gate).astype(MXU_DTYPE)
        dz = (dx2 * pp * gate * (1.0 - gate)).astype(MXU_DTYPE)
        dhp = _mm_nt(dz, wpg_ref[...])
        dgp_ref[...] += jnp.sum(dhp * xn1, axis=0, keepdims=True)
        dx1_ref[...] = dx2 + _rms_bwd(dhp * gp_ref[...], xn1, r1)
        hp_ref[...] = hp
        dz_ref[...] = dz
        dpp_ref[...] = dpp

    vec = _resident((1, D_MODEL))
    return pl.pallas_call(
        body, name="ple_fwd_bwd", grid=(t // TILE_M,),
        in_specs=[_row_tile(D_MODEL), _row_tile(PLE_DIM), _row_tile(D_MODEL), vec, vec,
                  _resident((D_MODEL, D_MODEL)), _resident((PLE_DIM, D_MODEL))],
        out_specs=[_row_tile(D_MODEL)] * 4 + [pl.BlockSpec(LOSS_TILE, lambda i: (0, 0)),
                                              pl.BlockSpec((1, D_MODEL), lambda i: (0, 0)),
                                              pl.BlockSpec((1, D_MODEL), lambda i: (0, 0))],
        out_shape=[jax.ShapeDtypeStruct((t, D_MODEL), F32)] + [jax.ShapeDtypeStruct((t, D_MODEL), MXU_DTYPE)] * 3
        + [jax.ShapeDtypeStruct(LOSS_TILE, F32), jax.ShapeDtypeStruct((1, D_MODEL), F32),
           jax.ShapeDtypeStruct((1, D_MODEL), F32)],
        compiler_params=_params(("arbitrary",)),
    )(x1, p2d, target, g_ple, g_final, w_pg, w_pp)


def _merge_bwd_call(dx1, pj1, pj2, pj3, ret_raw, a, b, ws, bst, w_ro, w_so, w_o):
    t = dx1.shape[0]

    def body(dx1_ref, pj1_ref, pj2_ref, pj3_ref, rr_ref, a_ref, b_ref, ws_ref, bst_ref, wro_ref, wso_ref, wo_ref,
             dpj1_ref, dpj2_ref, dpj3_ref, drr_ref, da_ref, db_ref, dws_ref, dbst_ref):
        @pl.when(pl.program_id(0) == 0)
        def _():
            dws_ref[...] = jnp.zeros_like(dws_ref)
            dbst_ref[...] = jnp.zeros_like(dbst_ref)

        wsc = _causal_ws(ws_ref)
        f = _branch_forward(pj1_ref[...], pj2_ref[...], rr_ref[...], wsc, bst_ref[...])
        pj3 = pj3_ref[...]
        smr, sms = _sigmoid(pj3[:, :D_MODEL]), _sigmoid(pj3[:, D_MODEL:])
        dmerged = _mm_nt(dx1_ref[...], wo_ref[...])
        da = (dmerged * smr).astype(MXU_DTYPE)
        db = (dmerged * sms).astype(MXU_DTYPE)
        dpj3_ref[:, :D_MODEL] = (dmerged * a_ref[...] * smr * (1.0 - smr)).astype(dpj3_ref.dtype)
        dpj3_ref[:, D_MODEL:] = (dmerged * b_ref[...] * sms * (1.0 - sms)).astype(dpj3_ref.dtype)
        da_ref[...] = da
        db_ref[...] = db

        dret = _mm_nt(da, wro_ref[...])
        dpj1_ref[:, :D_MODEL] = (dret * f["rn"] * f["dsil_rg"]).astype(dpj1_ref.dtype)
        drn = dret * f["sil_rg"]
        for h in range(RET_HEADS):
            cols = slice(h * V_DIM, (h + 1) * V_DIM)
            drr_ref[:, cols] = _unit_norm_bwd(drn[:, cols], f["rn"][:, cols], f["rstd_r"][h])

        dsgu = _mm_nt(db, wso_ref[...])
        dpj2_ref[:, D_MODEL:] = (dsgu * f["u"] * f["mixed"] * f["dsil_sg"]).astype(dpj2_ref.dtype)
        tg = dsgu * f["sil_sg"]
        dpj1_ref[:, D_MODEL:] = (tg * f["mixed"] * _gelu_grad(f["su"])).astype(dpj1_ref.dtype)
        dmixed = tg * f["u"]
        rows = []
        for cc in range(TILE_M // CHUNK):
            cols = []
            for g in range(SGU_GROUPS):
                rs, cs = slice(cc * CHUNK, (cc + 1) * CHUNK), slice(g * GROUP_DIM, (g + 1) * GROUP_DIM)
                dm = dmixed[rs, cs]
                cols.append(_mm_tn(wsc[g], dm))
                dws_ref[g] += _mm_nt(dm, f["vn"][rs, cs])
                dbst_ref[:, g:g + 1] += jnp.sum(dm, axis=1, keepdims=True)
            rows.append(jnp.concatenate(cols, axis=1))
        dvn = jnp.concatenate(rows, axis=0)
        dvv = _unit_norm_bwd(dvn, f["vn"], f["rstd_v"])
        dpj2_ref[:, :D_MODEL] = (dvv * _gelu_grad(f["sv"])).astype(dpj2_ref.dtype)

    sq = (D_MODEL, D_MODEL)
    return pl.pallas_call(
        body, name="merge_bwd", grid=(t // TILE_M,),
        in_specs=[_row_tile(D_MODEL)] + [_proj_tile(j) for j in (1, 2, 3)] + [_row_tile(D_MODEL)] * 3
        + [_resident(ws.shape), _resident(bst.shape), _resident(sq), _resident(sq), _resident(sq)],
        out_specs=[_row_tile(SHARD_W)] * 3 + [_row_tile(D_MODEL)] * 3
        + [pl.BlockSpec(ws.shape, lambda i: (0, 0, 0)), pl.BlockSpec(bst.shape, lambda i: (0, 0))],
        out_shape=[jax.ShapeDtypeStruct((t, SHARD_W), MXU_DTYPE)] * 3 + [jax.ShapeDtypeStruct((t, D_MODEL), F32)]
        + [jax.ShapeDtypeStruct((t, D_MODEL), MXU_DTYPE)] * 2
        + [jax.ShapeDtypeStruct(ws.shape, F32), jax.ShapeDtypeStruct(bst.shape, F32)],
        compiler_params=_params(("arbitrary",)),
    )(dx1, pj1, pj2, pj3, ret_raw, a, b, ws, bst, w_ro, w_so, w_o)


def _retention_bwd_call(pj0, drr, states, consts, n_seq, seq, side):
    cos_full, sin_signed, decay_in, zeta, xi = consts
    nb = seq // RET_TILE
    cpb = RET_TILE // CHUNK

    def body(pj_ref, do_ref, st_ref, cos_ref, sin_ref, d_ref, zeta_ref, xi_ref, dpj_ref, gstate):
        @pl.when(pl.program_id(1) == 0)
        def _():
            gstate[...] = jnp.zeros_like(gstate)

        for cc in reversed(range(cpb)):
            rows = slice(cc * CHUNK, (cc + 1) * CHUNK)
            cs, sn = cos_ref[rows, :], sin_ref[rows, :]
            for h in range(RET_HEADS):
                q, k, v = _qkv(pj_ref, rows, h)
                qt = (q * cs + _rot(q) * sn) * QK_SCALE
                kt = k * cs + _rot(k) * sn
                d_out = do_ref[rows, h * V_DIM:(h + 1) * V_DIM]
                prev = st_ref[cc, h]
                g = gstate[h]
                dec = d_ref[h]
                scores_d = _mm_nt(qt, kt) * dec
                dscores = _mm_nt(d_out, v) * dec
                kz = kt * zeta_ref[h]
                qx = qt * xi_ref[h]
                dv = _mm_tn(scores_d, d_out) + _mm(kz, g)
                dqt = (_mm(dscores, kt) + _mm_nt(d_out, prev) * xi_ref[h]) * QK_SCALE
                dkt = _mm_tn(dscores, qt) + _mm_nt(v, g) * zeta_ref[h]
                gstate[h] = _mm_tn(qx, d_out) + CHUNK_DECAY[h] * g
                dq = dqt * cs + _rot(dqt * sn)
                dk = dkt * cs + _rot(dkt * sn)
                dpj_ref[rows, h * QK_DIM:(h + 1) * QK_DIM] = dq.astype(dpj_ref.dtype)
                dpj_ref[rows, RET_HEADS * QK_DIM + h * QK_DIM:RET_HEADS * QK_DIM + (h + 1) * QK_DIM] = dk.astype(
                    dpj_ref.dtype)
                dpj_ref[rows, 2 * RET_HEADS * QK_DIM + h * V_DIM:2 * RET_HEADS * QK_DIM + (h + 1) * V_DIM] = dv.astype(
                    dpj_ref.dtype)

    row = lambda b, n: (b * nb + nb - 1 - n, 0)
    pos = lambda b, n: (nb - 1 - n, 0)
    return _compute_call(
        body, name="retention_bwd", grid=(n_seq, nb),
        in_specs=[pl.BlockSpec((None, RET_TILE, SHARD_W), lambda b, n: (0,) + row(b, n)),
                  pl.BlockSpec((RET_TILE, D_MODEL), row),
                  pl.BlockSpec((cpb, RET_HEADS, QK_DIM, V_DIM), lambda b, n: (b * nb + nb - 1 - n, 0, 0, 0)),
                  pl.BlockSpec((RET_TILE, QK_DIM), pos), pl.BlockSpec((RET_TILE, QK_DIM), pos),
                  _resident(decay_in.shape), _resident(zeta.shape), _resident(xi.shape)],
        out_specs=[pl.BlockSpec((RET_TILE, SHARD_W), row)],
        out_shape=[jax.ShapeDtypeStruct((n_seq * seq, SHARD_W), MXU_DTYPE)],
        scratch_shapes=[pltpu.VMEM((RET_HEADS, QK_DIM, V_DIM), F32)],
        operands=(pj0, drr, states, cos_full, sin_signed, decay_in, zeta, xi),
        semantics=("arbitrary", "arbitrary"), side=side)


def _dx_call(dpj, x2d, dx1, g_mixer, wg_in, side):
    t = x2d.shape[0]

    def body(d0, d1, d2, d3, x_ref, dx1_ref, g_ref, w_ref, dx_ref, dg_ref):
        @pl.when(pl.program_id(0) == 0)
        def _():
            dg_ref[...] = jnp.zeros_like(dg_ref)

        dh = _mm_nt(d0[...], w_ref[0])
        for j, d_ref in enumerate((d1, d2, d3)):
            dh += _mm_nt(d_ref[...], w_ref[j + 1])
        xn, r = _rms(x_ref[...])
        dg_ref[...] += jnp.sum(dh * xn, axis=0, keepdims=True)
        dx_ref[...] = dx1_ref[...] + _rms_bwd(dh * g_ref[...], xn, r)

    return _compute_call(
        body, name="dx_bwd", grid=(t // DX_TILE,),
        in_specs=[_row_tile(SHARD_W, DX_TILE)] * N_CHIPS + [_row_tile(D_MODEL, DX_TILE)] * 2
        + [_resident((1, D_MODEL)), _resident(wg_in.shape)],
        out_specs=[_row_tile(D_MODEL, DX_TILE), pl.BlockSpec((1, D_MODEL), lambda i: (0, 0))],
        out_shape=[jax.ShapeDtypeStruct((t, D_MODEL), F32), jax.ShapeDtypeStruct((1, D_MODEL), F32)],
        operands=(*dpj, x2d, dx1, g_mixer, wg_in), semantics=("arbitrary",), side=side)


def _wgrad_call(name, lhs, rhs, block_n, out_cols=None, block_t=1024, into=None, slot=0, n_slots=1, side=None):
    t, n = rhs.shape
    k = lhs.shape[1]
    steps = t // block_t
    out_cols = block_n if out_cols is None else out_cols
    per = block_n // out_cols
    first_block = slot * (n // block_n)

    def body(l_ref, r_ref, *rest):
        o_ref, acc = rest[-2:]
        @pl.when(pl.program_id(1) == 0)
        def _():
            acc[...] = jnp.zeros_like(acc)

        acc[...] += _mm_tn(l_ref[...], r_ref[...])

        @pl.when(pl.program_id(1) == steps - 1)
        def _():
            for s in range(per):
                o_ref[s] = acc[:, s * out_cols:(s + 1) * out_cols].astype(o_ref.dtype)

    res = _compute_call(
        body, name=name, grid=(n // block_n, steps),
        in_specs=[pl.BlockSpec((block_t, k), lambda j, i: (i, 0)), pl.BlockSpec((block_t, block_n), lambda j, i: (i, j))]
        + ([] if into is None else [ANY]),
        out_specs=[pl.BlockSpec((per, k, out_cols), lambda j, i: (first_block + j, 0, 0))],
        out_shape=[jax.ShapeDtypeStruct((n_slots * (n // out_cols), k, out_cols), COMM_DTYPE)],
        scratch_shapes=[pltpu.VMEM((k, block_n), F32)], aliases={} if into is None else {2: 0},
        operands=(lhs, rhs) if into is None else (lhs, rhs, into), semantics=("arbitrary", "arbitrary"), side=side)
    return res[0] if side is None else (res[0][0], res[1])


def _position():
    return lax.axis_index("x"), lax.axis_index("y"), lax.axis_index("c")


def _position_array():
    x, y, c = _position()
    return jnp.stack([2 * x + y, c]).astype(jnp.int32)


def _other_chip(x, y, k):
    return (1 - x if k & 2 else x), (1 - y if k & 1 else y)


def _plan_side(operands, out_shapes, plan, n_remote, aliases=None):
    def copies(ins, outs, send_sem, recv_sem, base=0):
        remote = plan(ins, outs)
        assert len(remote) == n_remote
        return [pltpu.make_async_remote_copy(src_ref=src, dst_ref=dst, send_sem=send_sem.at[base + i],
                                             recv_sem=recv_sem.at[base + i], device_id=dev, device_id_type=MESH)
                for i, (src, dst, dev) in enumerate(remote)]

    def start(*a):
        for cp in copies(*a):
            cp.start()

    def finish(*a):
        for cp in copies(*a):
            cp.wait()

    return _Side(operands, out_shapes, n_remote, start, finish, aliases)


def _place_cast_call(name, w, pos):
    rows, width = w.shape
    block_rows = min(256, rows)

    def body(pos_ref, w_ref, o_ref):
        o_ref[...] = w_ref[...].astype(o_ref.dtype)

    return pl.pallas_call(
        body, name=name,
        grid_spec=pltpu.PrefetchScalarGridSpec(
            num_scalar_prefetch=1, grid=(rows // block_rows,),
            in_specs=[pl.BlockSpec((block_rows, width), lambda i, pos: (i, 0))],
            out_specs=pl.BlockSpec((None, block_rows, width), lambda i, pos: (pos[0], i, 0))),
        out_shape=jax.ShapeDtypeStruct((N_CHIPS, rows, width), MXU_DTYPE),
        compiler_params=_params(("arbitrary",)),
    )(pos, w)


def _gather_side(placed):
    n = len(placed)

    def copies(kind, bufs, send_sem, recv_sem, base):
        x, y, c = _position()
        me = 2 * x + y
        made = []
        for i in range(n):
            hr = placed[i].shape[1] // 2
            for k in (1, 2, 3):
                px, py = _other_chip(x, y, k)
                chip, core, slot, dev = [(me, c, k - 1, (px, py, c)), (2 * px + py, c, 2 + k, (x, y, 1 - c)),
                                         (2 * px + py, 1 - c, 2 + k, (x, y, 1 - c))][kind]
                piece = bufs[i].at[chip, pl.ds(core * hr, hr)]
                made.append(pltpu.make_async_remote_copy(
                    src_ref=piece, dst_ref=piece, send_sem=send_sem.at[base + 6 * i + slot],
                    recv_sem=recv_sem.at[base + 6 * i + slot], device_id=dev, device_id_type=MESH))
        return made

    def start(ins, outs, send_sem, recv_sem, base=0):
        for cp in copies(0, outs, send_sem, recv_sem, base):
            cp.start()

    def finish(ins, outs, send_sem, recv_sem, base=0):
        first, onward = copies(0, outs, send_sem, recv_sem, base), copies(1, outs, send_sem, recv_sem, base)
        for landed, cp in zip(first, onward):
            landed.wait_recv()
            cp.start()
        for cp in copies(2, outs, send_sem, recv_sem, base):
            cp.wait_recv()
        for cp in first + onward:
            cp.wait_send()

    return _Side(placed, [jax.ShapeDtypeStruct(a.shape, a.dtype) for a in placed], 6 * n, start, finish,
                 {i: i for i in range(n)})


def _sum_call(name, pos, terms, rows, width, out_dtype, block_rows, out_rows=None, out_index=None):
    out_rows = rows if out_rows is None else out_rows
    out_index = (lambda i, pos, br: i) if out_index is None else out_index

    def body(pos_ref, *refs):
        acc = refs[0][...].astype(F32)
        for r in refs[1:-1]:
            acc = acc + r[...].astype(F32)
        refs[-1][...] = acc.astype(out_dtype)

    def spec(index):
        return pl.BlockSpec((block_rows, width), lambda i, pos: (index(i, pos, block_rows), 0))

    return pl.pallas_call(
        body, name=name,
        grid_spec=pltpu.PrefetchScalarGridSpec(
            num_scalar_prefetch=1, grid=(rows // block_rows,),
            in_specs=[spec(index) for _, index in terms], out_specs=spec(out_index)),
        out_shape=jax.ShapeDtypeStruct((out_rows, width), out_dtype),
        compiler_params=_params(("arbitrary",)),
    )(pos, *[arr for arr, _ in terms])


def _pair_stage(tag, grads, pos):
    n = len(grads)
    shapes = [(g.shape[0] // N_CHIPS, g.shape[1]) for g in grads]

    def plan(ins, outs):
        x, y, c = _position()
        remote = []
        for i, (r, _) in enumerate(shapes):
            hr = r // 2
            for j in range(N_CHIPS):
                remote.append((ins[i].at[pl.ds(j * r + (1 - c) * hr, hr)], outs[i].at[pl.ds(j * hr, hr)],
                               (x, y, 1 - c)))
        return remote

    pair = _exchange_call(f"grad_pair_exchange_{tag}", _plan_side(
        grads, [jax.ShapeDtypeStruct((N_CHIPS * (r // 2), w), COMM_DTYPE) for r, w in shapes], plan, N_CHIPS * n))

    def own_half(r):
        def index(s, pos, br):
            per = (r // 2) // br
            return (s // per) * (r // br) + pos[1] * per + s % per
        return index

    return [_sum_call(f"grad_pair_sum_{tag}{i}", pos, [(grads[i], own_half(r)), (pair[i], lambda s, pos, br: s)],
                      N_CHIPS * (r // 2), w, COMM_DTYPE, min(256, r // 2)) for i, (r, w) in enumerate(shapes)]


def _chip_side(pair_sums):
    halves = [(p.shape[0] // N_CHIPS, p.shape[1]) for p in pair_sums]

    def plan(ins, outs):
        x, y, c = _position()
        remote = []
        for i, (hr, _) in enumerate(halves):
            for k in (1, 2, 3):
                px, py = _other_chip(x, y, k)
                remote.append((ins[i].at[pl.ds((2 * px + py) * hr, hr)], outs[i].at[pl.ds((k - 1) * hr, hr)],
                               (px, py, c)))
        return remote

    return _plan_side(pair_sums, [jax.ShapeDtypeStruct((3 * hr, w), COMM_DTYPE) for hr, w in halves], plan,
                      3 * len(pair_sums))


def _finish_stage(tag, pair_sums, chip, pos, extra=None):
    n = len(pair_sums)
    halves = [(p.shape[0] // N_CHIPS, p.shape[1]) for p in pair_sums]

    def slab(k, hr):
        return lambda s, pos, br: k * (hr // br) + s

    reduced = []
    for i, (hr, w) in enumerate(halves):
        terms = [(pair_sums[i], lambda s, pos, br, hr=hr: pos[0] * (hr // br) + s)]
        terms += [(chip[i], slab(k, hr)) for k in range(3)]
        reduced.append(_sum_call(f"grad_chip_sum_{tag}{i}", pos, terms, hr, w, F32, min(256, hr), out_rows=2 * hr,
                                 out_index=lambda s, pos, br, hr=hr: pos[1] * (hr // br) + s))

    def plan(ins, outs):
        x, y, c = _position()
        remote = []
        for i, (hr, _) in enumerate(halves):
            mine = outs[i].at[pl.ds(c * hr, hr)]
            remote.append((mine, mine, (x, y, 1 - c)))
        return remote

    swap = _plan_side(reduced, [jax.ShapeDtypeStruct((2 * hr, w), F32) for hr, w in halves], plan, n,
                      {i: i for i in range(n)})
    if extra is None:
        return _exchange_call(f"grad_half_exchange_{tag}", swap)
    res = _exchange_call(f"grad_half_exchange_{tag}", _join_sides(swap, extra))
    return res[:n], res[n:]


def _small_gather_side(parts):
    n = len(parts)

    def copies(ins, outs, send_sem, recv_sem, base):
        x, y, c = _position()
        made = []
        for i in range(n):
            mine = outs[i].at[4 * x + 2 * y + c]
            made.append(pltpu.make_async_copy(ins[i], mine, send_sem.at[base + 8 * i + 7]))
            for d in range(1, 8):
                px, py = _other_chip(x, y, d >> 1)
                made.append(pltpu.make_async_remote_copy(
                    src_ref=ins[i], dst_ref=mine, send_sem=send_sem.at[base + 8 * i + d - 1],
                    recv_sem=recv_sem.at[base + 8 * i + d - 1], device_id=(px, py, 1 - c if d & 1 else c),
                    device_id_type=MESH))
        return made

    def start(ins, outs, send_sem, recv_sem, base=0):
        for cp in copies(ins, outs, send_sem, recv_sem, base):
            cp.start()

    def finish(ins, outs, send_sem, recv_sem, base=0):
        for cp in copies(ins, outs, send_sem, recv_sem, base):
            cp.wait()

    return _Side(parts, [jax.ShapeDtypeStruct((8,) + a_.shape, F32) for a_ in parts], 8 * n, start, finish)


def _adamw(w, g, m, v):
    m = ADAM_B1 * m + (1.0 - ADAM_B1) * g
    v = ADAM_B2 * v + (1.0 - ADAM_B2) * (g * g)
    m_hat = m / (1.0 - ADAM_B1 ** ADAM_STEP)
    v_hat = v / (1.0 - ADAM_B2 ** ADAM_STEP)
    delta = -ADAM_LR * (m_hat / (jnp.sqrt(v_hat) + ADAM_EPS) + ADAM_WD * w)
    return delta, m, v


def _adamw_call(name, w, g, m, v):
    rows, width = w.shape
    block_rows = min(256, rows)

    def body(w_ref, g_ref, m_ref, v_ref, d_out, m_out, v_out):
        d_out[...], m_out[...], v_out[...] = _adamw(w_ref[...], g_ref[...], m_ref[...], v_ref[...])

    spec = pl.BlockSpec((block_rows, width), lambda i: (i, 0))
    return pl.pallas_call(
        body, name=name, grid=(rows // block_rows,), in_specs=[spec] * 4, out_specs=[spec] * 3,
        out_shape=[jax.ShapeDtypeStruct(w.shape, F32)] * 3,
        compiler_params=_params(("arbitrary",)),
    )(w, g, m, v)


def _small_adamw_call(gathered, weights, moments_m, moments_v):
    n = len(weights)

    def body(*refs):
        all_refs, refs = refs[:n + 1], refs[n + 1:]
        w_refs, m_refs, v_refs, outs = refs[:n], refs[n:2 * n], refs[2 * n:3 * n], refs[3 * n:]

        def total(ref):
            acc = ref[0]
            for d in range(1, 8):
                acc = acc + ref[d]
            return acc

        outs[0][...] = total(all_refs[n])
        for i in range(n):
            g = total(all_refs[i])
            if i == 0:
                row = lax.broadcasted_iota(jnp.int32, g.shape, 0)
                col = lax.broadcasted_iota(jnp.int32, g.shape, 1)
                g = jnp.where((row % CHUNK) >= col, g, 0.0)
            g_out, d_out, m_out, v_out = outs[1 + 4 * i:5 + 4 * i]
            g_out[...] = g
            d_out[...], m_out[...], v_out[...] = _adamw(w_refs[i][...], g, m_refs[i][...], v_refs[i][...])

    out_shape = [jax.ShapeDtypeStruct(LOSS_TILE, F32)]
    for w in weights:
        out_shape += [jax.ShapeDtypeStruct(w.shape, F32)] * 4
    res = pl.pallas_call(
        body, name="small_adamw", out_shape=out_shape,
        compiler_params=pltpu.CompilerParams(vmem_limit_bytes=VMEM_LIMIT),
    )(*gathered, *weights, *moments_m, *moments_v)
    return res[0], [res[1 + 4 * i:5 + 4 * i] for i in range(n)]


def kernel(x, p, w_in, w_ret_out, w_sgu_out, w_out, sgu_ws, sgu_bs, w_ple_gate, w_ple_proj, g_mixer, g_ple, g_final, loss_target, m_w_in, m_w_ret_out, m_w_sgu_out, m_w_out, m_sgu_ws, m_sgu_bs, m_w_ple_gate, m_w_ple_proj, m_g_mixer, m_g_ple, m_g_final, v_w_in, v_w_ret_out, v_w_sgu_out, v_w_out, v_sgu_ws, v_sgu_bs, v_w_ple_gate, v_w_ple_proj, v_g_mixer, v_g_ple, v_g_final):
    n_seq, seq, _ = x.shape
    t = n_seq * seq
    x2d = x.reshape(t, D_MODEL)
    p2d = p.reshape(t, PLE_DIM)
    target = loss_target.reshape(t, D_MODEL)
    big = [w_in[0], w_ret_out[0], w_sgu_out[0], w_out[0], w_ple_gate[0], w_ple_proj[0]]
    big_m = [m_w_in[0], m_w_ret_out[0], m_w_sgu_out[0], m_w_out[0], m_w_ple_gate[0], m_w_ple_proj[0]]
    big_v = [v_w_in[0], v_w_ret_out[0], v_w_sgu_out[0], v_w_out[0], v_w_ple_gate[0], v_w_ple_proj[0]]

    pos = _position_array()
    placed = [_place_cast_call(f"place_weight_{i}", w, pos) for i, w in enumerate(big)]
    ws = sgu_ws[0]
    bst = sgu_bs[0].T
    consts = _retention_consts(seq)

    (h, pj, wg_in), gathered = _proj_call(x2d, g_mixer, placed[0], pos, _gather_side(placed[1:]))
    pj0 = pj1 = pj2 = pj3 = pj
    w_ro, w_so, w_o, w_pg = (w.reshape(D_MODEL, D_MODEL) for w in gathered[:4])
    w_pp = gathered[4].transpose(1, 0, 2).reshape(PLE_DIM, D_MODEL)
    ret_raw, states = _retention_fwd_call(pj0, consts, n_seq, seq)
    x1, ret, sgu, merged, a, b = _merge_fwd_call(pj1, pj2, pj3, ret_raw, x2d, ws, bst, w_ro, w_so, w_o)
    dx1, hp, dz, dpp, loss, dg_ple, dg_final = _ple_call(x1, p2d, target, g_ple, g_final.reshape(1, D_MODEL),
                                                         w_pg, w_pp)
    dpj1, dpj2, dpj3, drr, da, db, dws, dbst = _merge_bwd_call(dx1, pj1, pj2, pj3, ret_raw, a, b, ws, bst,
                                                               w_ro, w_so, w_o)

    small_shapes = [(SGU_GROUPS * CHUNK, CHUNK), (SGU_GROUPS, CHUNK), (1, D_MODEL), (1, D_MODEL), (1, D_MODEL)]
    as_small = lambda arrays: [None if a_ is None else a_.reshape(s_) for a_, s_ in zip(arrays, small_shapes)]
    early = as_small([dws, dbst.T, None, dg_ple, dg_final])
    g_ret_out, early_all = _wgrad_call("wgrad_ret_out", ret, da, D_MODEL,
                                       side=_small_gather_side([early[0], early[1], early[3], early[4], loss]))
    rows_of = lambda g: g.reshape(g.shape[0] * g.shape[1], g.shape[2])
    tail_grads = [
        rows_of(g_ret_out),
        rows_of(_wgrad_call("wgrad_sgu_out", sgu, db, D_MODEL)),
        rows_of(_wgrad_call("wgrad_out", merged, dx1, D_MODEL)),
        rows_of(_wgrad_call("wgrad_ple_gate", hp, dz, D_MODEL)),
        rows_of(_wgrad_call("wgrad_ple_proj", p2d, dpp, D_MODEL, out_cols=PLE_DIM)),
    ]
    tail_sums = _pair_stage("tail", tail_grads, pos)
    (dpj0,), tail_chip = _retention_bwd_call(pj0, drr, states, consts, n_seq, seq, _chip_side(tail_sums))
    in_grad = None
    for j, d in enumerate((dpj0, dpj1, dpj2, dpj3)):
        in_grad = _wgrad_call(f"wgrad_in_{j}", h, d, SHARD_W, into=in_grad, slot=j, n_slots=N_CHIPS)
    in_sums = _pair_stage("in", [rows_of(in_grad)], pos)
    (dx, dg_mixer), in_chip = _dx_call((dpj0, dpj1, dpj2, dpj3), x2d, dx1, g_mixer, wg_in, _chip_side(in_sums))
    g_in, (mixer_all,) = _finish_stage("in", in_sums, in_chip, pos, _small_gather_side([dg_mixer]))
    g_big = g_in + _finish_stage("tail", tail_sums, tail_chip, pos)
    upd = [_adamw_call(f"adamw_{i}", big[i], g_big[i], big_m[i], big_v[i]) for i in range(len(big))]

    small_g = [early_all[0], early_all[1], mixer_all, early_all[2], early_all[3], early_all[4]]
    total, small = _small_adamw_call(small_g, as_small([sgu_ws, sgu_bs, g_mixer, g_ple, g_final]),
                                     as_small([m_sgu_ws, m_sgu_bs, m_g_mixer, m_g_ple, m_g_final]),
                                     as_small([v_sgu_ws, v_sgu_bs, v_g_mixer, v_g_ple, v_g_final]))
    out_small_shapes = [sgu_ws.shape, sgu_bs.shape, g_mixer.shape, g_ple.shape, g_final.shape]

    def ordered(big_list, kind):
        w_in_, w_ro_, w_so_, w_o_, w_pg_, w_pp_ = [b_[None] for b_ in big_list]
        s_ws, s_bs, s_gm, s_gp, s_gf = [small[i][kind].reshape(s) for i, s in enumerate(out_small_shapes)]
        return [w_in_, w_ro_, w_so_, w_o_, s_ws, s_bs, w_pg_, w_pp_, s_gm, s_gp, s_gf]

    out = [total[0, 0], dx.reshape(x.shape)]
    out += ordered(g_big, 0)
    out += ordered([u[0] for u in upd], 1)
    out += ordered([u[1] for u in upd], 2)
    out += ordered([u[2] for u in upd], 3)
    return tuple(out)
```

```python
import functools
import math

import numpy as np
import jax
import jax.numpy as jnp
from jax import lax
from jax.experimental import pallas as pl
from jax.experimental.pallas import tpu as pltpu

F32 = jnp.float32
MXU_DTYPE = jnp.bfloat16
COMM_DTYPE = jnp.bfloat16

D_MODEL = 1024
RET_HEADS = 4
QK_DIM = 128
V_DIM = 256
CHUNK = 128
SGU_GROUPS = 4
GROUP_DIM = 256
PLE_DIM = 256
N_CHIPS = 4
SHARD_W = 2048
ROPE_BASE = 10000.0
NORM_EPS = 1e-6
GN_EPS = 1e-5
QK_SCALE = QK_DIM ** -0.5
SQRT_HALF = math.sqrt(0.5)
INV_SQRT_2PI = 1.0 / math.sqrt(2.0 * math.pi)

ADAM_LR = 0.001
ADAM_B1 = 0.9
ADAM_B2 = 0.999
ADAM_EPS = 1e-08
ADAM_WD = 0.01
ADAM_STEP = 10

TILE_M = 256
LOSS_TILE = (8, 128)
LOAD_PARTS = 16
PROJ_TILE = 512
RET_TILE = 512
DX_TILE = 512
VMEM_LIMIT = 56 * 1024 * 1024
MESH = pl.DeviceIdType.MESH
ANY = pl.BlockSpec(memory_space=pl.ANY)

CHUNK_DECAY = tuple(
    float(np.exp(np.float32(CHUNK) * np.log(np.float32(1.0 - 2.0 ** (-5.0 - h))))) for h in range(RET_HEADS))


def _mm(a, b):
    return jnp.dot(a.astype(MXU_DTYPE), b.astype(MXU_DTYPE), preferred_element_type=F32)


def _mm_nt(a, b):
    return lax.dot_general(a.astype(MXU_DTYPE), b.astype(MXU_DTYPE), (((1,), (1,)), ((), ())),
                           preferred_element_type=F32)


def _mm_tn(a, b):
    return lax.dot_general(a.astype(MXU_DTYPE), b.astype(MXU_DTYPE), (((0,), (0,)), ((), ())),
                           preferred_element_type=F32)


def _mean(x):
    return jnp.mean(x, axis=-1, keepdims=True)


def _sigmoid(x):
    return jax.nn.sigmoid(x)


def _silu_and_grad(x):
    s = _sigmoid(x)
    return x * s, s * (1.0 + x * (1.0 - s))


def _gelu(x):
    return 0.5 * x * (1.0 + lax.erf(x * SQRT_HALF))


def _gelu_grad(x):
    return 0.5 * (1.0 + lax.erf(x * SQRT_HALF)) + x * jnp.exp(-0.5 * x * x) * INV_SQRT_2PI


def _unit_norm(x, eps):
    xc = x - _mean(x)
    rstd = lax.rsqrt(_mean(xc * xc) + eps)
    return xc * rstd, rstd


def _unit_norm_bwd(dn, n, rstd):
    return rstd * (dn - _mean(dn) - n * _mean(dn * n))


def _rms(x):
    r = lax.rsqrt(_mean(x * x) + NORM_EPS)
    return x * r, r


def _rms_bwd(dxn, xn, r):
    return r * (dxn - xn * _mean(dxn * xn))


def _rot(x):
    return pltpu.roll(x, QK_DIM // 2, 1)


def _params(semantics, **kw):
    return pltpu.CompilerParams(dimension_semantics=semantics, vmem_limit_bytes=VMEM_LIMIT, **kw)


def _row_tile(width, tm=TILE_M):
    return pl.BlockSpec((tm, width), lambda i: (i, 0))


def _proj_tile(j):
    return pl.BlockSpec((None, TILE_M, SHARD_W), lambda i: (j, i, 0))


def _resident(shape):
    nd = len(shape)
    return pl.BlockSpec(shape, lambda *_: (0,) * nd, pipeline_mode=pl.Buffered(1))


def _causal_ws(ws_ref):
    row = lax.broadcasted_iota(jnp.int32, (CHUNK, CHUNK), 0)
    col = lax.broadcasted_iota(jnp.int32, (CHUNK, CHUNK), 1)
    return [jnp.where(row >= col, ws_ref[g], 0.0).astype(MXU_DTYPE) for g in range(SGU_GROUPS)]


def _heads(x, width):
    return [x[:, h * width:(h + 1) * width] for h in range(x.shape[1] // width)]


def _branch_forward(pj1, pj2, ret_raw, wsc, bst):
    rg, su = pj1[:, :D_MODEL], pj1[:, D_MODEL:]
    sv, sg = pj2[:, :D_MODEL], pj2[:, D_MODEL:]
    rn_parts, rstd_parts = zip(*[_unit_norm(r, GN_EPS) for r in _heads(ret_raw, V_DIM)])
    rn = jnp.concatenate(rn_parts, axis=1)
    sil_rg, dsil_rg = _silu_and_grad(rg)
    ret = rn * sil_rg
    u = _gelu(su)
    vn, rstd_v = _unit_norm(_gelu(sv), GN_EPS)
    rows = []
    for cc in range(pj1.shape[0] // CHUNK):
        cols = []
        for g in range(SGU_GROUPS):
            blk = vn[cc * CHUNK:(cc + 1) * CHUNK, g * GROUP_DIM:(g + 1) * GROUP_DIM]
            cols.append(_mm(wsc[g], blk) + bst[:, g:g + 1])
        rows.append(jnp.concatenate(cols, axis=1))
    mixed = jnp.concatenate(rows, axis=0)
    sil_sg, dsil_sg = _silu_and_grad(sg)
    sgu = u * mixed * sil_sg
    return dict(rg=rg, su=su, sv=sv, sg=sg, rn=rn, rstd_r=rstd_parts, sil_rg=sil_rg, dsil_rg=dsil_rg, ret=ret,
                u=u, vn=vn, rstd_v=rstd_v, mixed=mixed, sil_sg=sil_sg, dsil_sg=dsil_sg, sgu=sgu)


class _Side:
    def __init__(self, operands, out_shapes, n_sems, start, finish, aliases=None):
        self.operands, self.out_shapes, self.n_sems = list(operands), list(out_shapes), n_sems
        self.start, self.finish, self.aliases = start, finish, dict(aliases or {})


def _join_sides(*sides):
    spans, a, b, s = [], 0, 0, 0
    for side in sides:
        spans.append((a, b, s))
        a, b, s = a + len(side.operands), b + len(side.out_shapes), s + side.n_sems

    def run(which):
        def go(ins, outs, send_sem, recv_sem, base=0):
            for side, (a0, b0, s0) in zip(sides, spans):
                getattr(side, which)(ins[a0:a0 + len(side.operands)], outs[b0:b0 + len(side.out_shapes)],
                                     send_sem, recv_sem, base + s0)
        return go

    aliases = {a0 + i: b0 + o for side, (a0, b0, _) in zip(sides, spans) for i, o in side.aliases.items()}
    return _Side([x for side in sides for x in side.operands], [x for side in sides for x in side.out_shapes], s,
                 run("start"), run("finish"), aliases)


def _compute_call(body, *, name, grid, in_specs, out_specs, out_shape, operands, semantics, scratch_shapes=(),
                  side=None, prefetch=None, aliases=None, side_start=None):
    n_pre = 0 if prefetch is None else 1
    pre = () if prefetch is None else (prefetch,)

    def spec(in_specs, out_specs, scratch):
        return pltpu.PrefetchScalarGridSpec(num_scalar_prefetch=n_pre, grid=grid, in_specs=in_specs,
                                            out_specs=out_specs, scratch_shapes=scratch)

    if side is None:
        return pl.pallas_call(body, name=name, grid_spec=spec(in_specs, out_specs, list(scratch_shapes)),
                              out_shape=out_shape,
                              input_output_aliases={n_pre + a: b for a, b in (aliases or {}).items()},
                              compiler_params=_params(semantics))(*pre, *operands)
    n_in, n_out, s_in, s_out = len(operands), len(out_shape), len(side.operands), len(side.out_shapes)

    def carrier(*refs):
        pre_refs, refs = refs[:n_pre], refs[n_pre:]
        ins, refs = refs[:n_in], refs[n_in:]
        side_ins, refs = refs[:s_in], refs[s_in:]
        outs, refs = refs[:n_out], refs[n_out:]
        side_outs, refs = refs[:s_out], refs[s_out:]
        scratch, (send_sem, recv_sem) = refs[:-2], refs[-2:]
        ids = [pl.program_id(a) for a in range(len(grid))]
        at = (0,) * len(grid) if side_start is None else side_start
        first = functools.reduce(jnp.logical_and, [i == a for i, a in zip(ids, at)])
        last = functools.reduce(jnp.logical_and, [i == g - 1 for i, g in zip(ids, grid)])

        @pl.when(first)
        def _():
            side.start(side_ins, side_outs, send_sem, recv_sem)

        body(*pre_refs, *ins, *outs, *scratch)

        @pl.when(last)
        def _():
            side.finish(side_ins, side_outs, send_sem, recv_sem)

    all_aliases = {n_pre + a: b for a, b in (aliases or {}).items()}
    all_aliases.update({n_pre + n_in + a: n_out + b for a, b in side.aliases.items()})
    res = pl.pallas_call(
        carrier, name=name,
        grid_spec=spec(list(in_specs) + [ANY] * s_in, list(out_specs) + [ANY] * s_out,
                       list(scratch_shapes) + [pltpu.SemaphoreType.DMA((side.n_sems,))] * 2),
        out_shape=list(out_shape) + side.out_shapes, input_output_aliases=all_aliases,
        compiler_params=_params(semantics, has_side_effects=True),
    )(*pre, *operands, *side.operands)
    return res[:n_out], res[n_out:]


def _exchange_call(name, side):
    s_in = len(side.operands)

    def body(*refs):
        ins, outs = refs[:s_in], refs[s_in:s_in + len(side.out_shapes)]
        send_sem, recv_sem = refs[s_in + len(side.out_shapes):]
        side.start(ins, outs, send_sem, recv_sem)
        side.finish(ins, outs, send_sem, recv_sem)

    return pl.pallas_call(
        body, name=name, in_specs=[ANY] * s_in, out_specs=[ANY] * len(side.out_shapes), out_shape=side.out_shapes,
        scratch_shapes=[pltpu.SemaphoreType.DMA((side.n_sems,))] * 2, input_output_aliases=side.aliases,
        compiler_params=pltpu.CompilerParams(has_side_effects=True),
    )(*side.operands)


def _proj_call(x2d, g_mixer, placed_in, pos, side):
    t = x2d.shape[0]
    nt = t // PROJ_TILE
    hr = placed_in.shape[1] // 2

    def body(pos_ref, x_ref, g_ref, win_ref, h_ref, pj_ref, w_ref, w_vmem, h_all, loc_sem, send_sem, recv_sem):
        k, i = pl.program_id(0), pl.program_id(1)
        x, y, c = _position()
        me = 2 * x + y

        def copy(slot, chip, core, dev):
            piece = w_ref.at[chip, pl.ds(core * hr, hr)]
            return pltpu.make_async_remote_copy(src_ref=piece, dst_ref=piece, send_sem=send_sem.at[slot],
                                                recv_sem=recv_sem.at[slot], device_id=dev, device_id_type=MESH)

        def first_hop(kk):
            px, py = _other_chip(x, y, kk)
            return copy(kk - 1, me, c, (px, py, c))

        def relay():
            source = jnp.bitwise_xor(me, 2 - c)
            return copy(2, source, c, (jnp.bitwise_xor(x, c), jnp.bitwise_xor(y, 1 - c), c))

        def onward(kk, core):
            px, py = _other_chip(x, y, kk)
            return copy(2 + kk, 2 * px + py, core, (x, y, 1 - c))

        def loads(kk):
            px, py = _other_chip(x, y, kk)
            rows = w_vmem.shape[1] // LOAD_PARTS
            return [pltpu.make_async_copy(w_ref.at[2 * px + py, pl.ds(r * rows, rows)],
                                          w_vmem.at[kk % 2, pl.ds(r * rows, rows)], loc_sem.at[r])
                    for r in range(LOAD_PARTS)]

        @pl.when(jnp.logical_and(k == 0, i == 0))
        def _():
            for kk in (1, 2):
                first_hop(kk).start()
            for cp in loads(0):
                cp.start()
            for cp in loads(0):
                cp.wait()

        for kk in (1, 2, 3):
            @pl.when(jnp.logical_and(k == kk - 1, i == nt - 1))
            def _(kk=kk):
                if kk == 1:
                    first_hop(1).wait_recv()
                    first_hop(2).wait_recv()
                    relay().start()
                if kk == 3:
                    relay().wait_recv()
                onward(kk, c).start()
                onward(kk, 1 - c).wait_recv()
                for cp in loads(kk):
                    cp.start()

            @pl.when(jnp.logical_and(k == kk, i == 0))
            def _(kk=kk):
                for cp in loads(kk):
                    cp.wait()

        rows = pl.ds(pl.multiple_of(i * PROJ_TILE, PROJ_TILE), PROJ_TILE)

        @pl.when(k == 0)
        def _():
            xn, _ = _rms(x_ref[...])
            h = (xn * g_ref[...]).astype(MXU_DTYPE)
            h_all[rows, :] = h
            h_ref[...] = h

        pj_ref[...] = jnp.dot(h_all[rows, :], w_vmem[k % 2], preferred_element_type=F32)

        @pl.when(jnp.logical_and(k == N_CHIPS - 1, i == nt - 1))
        def _():
            for cp in [first_hop(1), first_hop(2), relay()] + [onward(kk, c) for kk in (1, 2, 3)]:
                cp.wait_send()

    parked = lambda k, i, pos: (jnp.where(k == 0, i, nt - 1), 0)
    return _compute_call(
        body, name="proj_fwd", grid=(N_CHIPS, nt),
        in_specs=[pl.BlockSpec((PROJ_TILE, D_MODEL), parked), pl.BlockSpec((1, D_MODEL), lambda k, i, pos: (0, 0)), ANY],
        out_specs=[pl.BlockSpec((PROJ_TILE, D_MODEL), lambda k, i, pos: (jnp.where(k == 0, i, nt), 0)),
                   pl.BlockSpec((None, PROJ_TILE, SHARD_W), lambda k, i, pos: (jnp.bitwise_xor(pos[0], k), i, 0)),
                   ANY],
        out_shape=[jax.ShapeDtypeStruct((t + PROJ_TILE, D_MODEL), MXU_DTYPE),
                   jax.ShapeDtypeStruct((N_CHIPS, t, SHARD_W), F32),
                   jax.ShapeDtypeStruct(placed_in.shape, placed_in.dtype)],
        scratch_shapes=[pltpu.VMEM((2,) + placed_in.shape[1:], placed_in.dtype), pltpu.VMEM((t, D_MODEL), MXU_DTYPE),
                        pltpu.SemaphoreType.DMA((LOAD_PARTS,)), pltpu.SemaphoreType.DMA((6,)),
                        pltpu.SemaphoreType.DMA((6,))],
        operands=(x2d, g_mixer, placed_in), semantics=("arbitrary", "arbitrary"), side=side, prefetch=pos,
        aliases={2: 2}, side_start=(2, 0))


def _retention_consts(seq):
    half = QK_DIM // 2
    inv = ROPE_BASE ** (-jnp.arange(half, dtype=F32) / half)
    ang = jnp.arange(seq, dtype=F32)[:, None] * inv[None, :]
    cos, sin = jnp.cos(ang), jnp.sin(ang)
    cos_full = jnp.concatenate([cos, cos], axis=1)
    sin_signed = jnp.concatenate([-sin, sin], axis=1)
    log_g = jnp.log(1.0 - 2.0 ** (-5.0 - jnp.arange(RET_HEADS, dtype=F32)))
    idx = jnp.arange(CHUNK, dtype=F32)
    diff = idx[:, None] - idx[None, :]
    decay_in = jnp.where(diff[None] >= 0, jnp.exp(jnp.maximum(diff, 0.0)[None] * log_g[:, None, None]), 0.0)
    zeta = jnp.exp((CHUNK - 1.0 - idx)[None, :] * log_g[:, None])
    xi = jnp.exp((idx + 1.0)[None, :] * log_g[:, None])
    zeta = jnp.broadcast_to(zeta[:, :, None], (RET_HEADS, CHUNK, QK_DIM))
    xi = jnp.broadcast_to(xi[:, :, None], (RET_HEADS, CHUNK, QK_DIM))
    return cos_full, sin_signed, decay_in, zeta, xi


def _qkv(pj_ref, rows, h):
    q = pj_ref[rows, h * QK_DIM:(h + 1) * QK_DIM]
    k = pj_ref[rows, RET_HEADS * QK_DIM + h * QK_DIM:RET_HEADS * QK_DIM + (h + 1) * QK_DIM]
    v = pj_ref[rows, 2 * RET_HEADS * QK_DIM + h * V_DIM:2 * RET_HEADS * QK_DIM + (h + 1) * V_DIM]
    return q, k, v


def _retention_fwd_call(pj0, consts, n_seq, seq):
    cos_full, sin_signed, decay_in, zeta, xi = consts
    nb = seq // RET_TILE
    cpb = RET_TILE // CHUNK

    def body(pj_ref, cos_ref, sin_ref, d_ref, zeta_ref, xi_ref, o_ref, st_ref, state):
        @pl.when(pl.program_id(1) == 0)
        def _():
            state[...] = jnp.zeros_like(state)

        for cc in range(cpb):
            rows = slice(cc * CHUNK, (cc + 1) * CHUNK)
            cs, sn = cos_ref[rows, :], sin_ref[rows, :]
            for h in range(RET_HEADS):
                q, k, v = _qkv(pj_ref, rows, h)
                qt = (q * cs + _rot(q) * sn) * QK_SCALE
                kt = k * cs + _rot(k) * sn
                prev = state[h]
                st_ref[cc, h] = prev.astype(st_ref.dtype)
                scores = _mm_nt(qt, kt) * d_ref[h]
                o_ref[rows, h * V_DIM:(h + 1) * V_DIM] = _mm(scores, v) + _mm(qt * xi_ref[h], prev)
                state[h] = _mm_tn(kt * zeta_ref[h], v) + CHUNK_DECAY[h] * prev

    row = lambda b, n: (b * nb + n, 0)
    pos = lambda b, n: (n, 0)
    return pl.pallas_call(
        body, name="retention_fwd", grid=(n_seq, nb),
        in_specs=[pl.BlockSpec((None, RET_TILE, SHARD_W), lambda b, n: (0,) + row(b, n)),
                  pl.BlockSpec((RET_TILE, QK_DIM), pos),
                  pl.BlockSpec((RET_TILE, QK_DIM), pos), _resident(decay_in.shape), _resident(zeta.shape),
                  _resident(xi.shape)],
        out_specs=[pl.BlockSpec((RET_TILE, D_MODEL), row),
                   pl.BlockSpec((cpb, RET_HEADS, QK_DIM, V_DIM), lambda b, n: (b * nb + n, 0, 0, 0))],
        out_shape=[jax.ShapeDtypeStruct((n_seq * seq, D_MODEL), F32),
                   jax.ShapeDtypeStruct((n_seq * seq // CHUNK, RET_HEADS, QK_DIM, V_DIM), MXU_DTYPE)],
        scratch_shapes=[pltpu.VMEM((RET_HEADS, QK_DIM, V_DIM), F32)],
        compiler_params=_params(("arbitrary", "arbitrary")),
    )(pj0, cos_full, sin_signed, decay_in, zeta, xi)


def _merge_bwd_call(dx1, pj1, pj2, pj3, ret_raw, a, b, ws, bst, w_ro, w_so, w_o):
    t = dx1.shape[0]

    def body(dx1_ref, pj1_ref, pj2_ref, pj3_ref, rr_ref, a_ref, b_ref, ws_ref, bst_ref, wro_ref, wso_ref, wo_ref,
             dpj1_ref, dpj2_ref, dpj3_ref, drr_ref, da_ref, db_ref, dws_ref, dbst_ref):
        @pl.when(pl.program_id(0) == 0)
        def _():
            dws_ref[...] = jnp.zeros_like(dws_ref)
            dbst_ref[...] = jnp.zeros_like(dbst_ref)

        wsc = _causal_ws(ws_ref)
        f = _branch_forward(pj1_ref[...], pj2_ref[...], rr_ref[...], wsc, bst_ref[...])
        pj3 = pj3_ref[...]
        smr, sms = _sigmoid(pj3[:, :D_MODEL]), _sigmoid(pj3[:, D_MODEL:])
        dmerged = _mm_nt(dx1_ref[...], wo_ref[...])
        da_f, db_f = dmerged * smr, dmerged * sms
        da, db = da_f.astype(MXU_DTYPE), db_f.astype(MXU_DTYPE)
        dpj3_ref[:, :D_MODEL] = (da_f * a_ref[...] * (1.0 - smr)).astype(dpj3_ref.dtype)
        dpj3_ref[:, D_MODEL:] = (db_f * b_ref[...] * (1.0 - sms)).astype(dpj3_ref.dtype)
        da_ref[...] = da
        db_ref[...] = db

        dret = _mm_nt(da, wro_ref[...])
        dpj1_ref[:, :D_MODEL] = (dret * f["rn"] * f["dsil_rg"]).astype(dpj1_ref.dtype)
        drn = dret * f["sil_rg"]
        for h in range(RET_HEADS):
            cols = slice(h * V_DIM, (h + 1) * V_DIM)
            drr_ref[:, cols] = _unit_norm_bwd(drn[:, cols], f["rn"][:, cols], f["rstd_r"][h])

        dsgu = _mm_nt(db, wso_ref[...])
        dpj2_ref[:, D_MODEL:] = (dsgu * f["u"] * f["mixed"] * f["dsil_sg"]).astype(dpj2_ref.dtype)
        tg = dsgu * f["sil_sg"]
        dpj1_ref[:, D_MODEL:] = (tg * f["mixed"] * _gelu_grad(f["su"])).astype(dpj1_ref.dtype)
        dmixed = tg * f["u"]
        rows = []
        for cc in range(TILE_M // CHUNK):
            cols = []
            for g in range(SGU_GROUPS):
                rs, cs = slice(cc * CHUNK, (cc + 1) * CHUNK), slice(g * GROUP_DIM, (g + 1) * GROUP_DIM)
                dm = dmixed[rs, cs]
                cols.append(_mm_tn(wsc[g], dm))
                dws_ref[g] += _mm_nt(dm, f["vn"][rs, cs])
                dbst_ref[:, g:g + 1] += jnp.sum(dm, axis=1, keepdims=True)
            rows.append(jnp.concatenate(cols, axis=1))
        dvn = jnp.concatenate(rows, axis=0)
        dvv = _unit_norm_bwd(dvn, f["vn"], f["rstd_v"])
        dpj2_ref[:, :D_MODEL] = (dvv * _gelu_grad(f["sv"])).astype(dpj2_ref.dtype)

    sq = (D_MODEL, D_MODEL)
    return pl.pallas_call(
        body, name="merge_bwd", grid=(t // TILE_M,),
        in_specs=[_row_tile(D_MODEL)] + [_proj_tile(j) for j in (1, 2, 3)] + [_row_tile(D_MODEL)] * 3
        + [_resident(ws.shape), _resident(bst.shape), _resident(sq), _resident(sq), _resident(sq)],
        out_specs=[_row_tile(SHARD_W)] * 3 + [_row_tile(D_MODEL)] * 3
        + [pl.BlockSpec(ws.shape, lambda i: (0, 0, 0)), pl.BlockSpec(bst.shape, lambda i: (0, 0))],
        out_shape=[jax.ShapeDtypeStruct((t, SHARD_W), MXU_DTYPE)] * 3 + [jax.ShapeDtypeStruct((t, D_MODEL), F32)]
        + [jax.ShapeDtypeStruct((t, D_MODEL), MXU_DTYPE)] * 2
        + [jax.ShapeDtypeStruct(ws.shape, F32), jax.ShapeDtypeStruct(bst.shape, F32)],
        compiler_params=_params(("arbitrary",)),
    )(dx1, pj1, pj2, pj3, ret_raw, a, b, ws, bst, w_ro, w_so, w_o)


def _tail_call(pj, ret_raw, x2d, p2d, target, ws, bst, g_ple, g_final, w_ro, w_so, w_o, w_pg, w_pp):
    t = x2d.shape[0]
    tm = TILE_M

    def body(pj1_ref, pj2_ref, pj3_ref, rr_ref, x_ref, p_ref, t_ref, ws_ref, bst_ref, gp_ref, gf_ref,
             wro_ref, wso_ref, wo_ref, wpg_ref, wpp_ref,
             dx1_ref, a_ref, b_ref, ret_ref, sgu_ref, mg_ref, hp_ref, dz_ref, dpp_ref, loss_ref, dgp_ref, dgf_ref):
        @pl.when(pl.program_id(0) == 0)
        def _():
            for acc in (loss_ref, dgp_ref, dgf_ref):
                acc[...] = jnp.zeros_like(acc)

        f = _branch_forward(pj1_ref[...], pj2_ref[...], rr_ref[...], _causal_ws(ws_ref), bst_ref[...])
        ret = f["ret"].astype(MXU_DTYPE)
        sgu = f["sgu"].astype(MXU_DTYPE)
        ret_ref[...] = ret
        sgu_ref[...] = sgu
        a = jnp.dot(ret, wro_ref[...], preferred_element_type=F32)
        b = jnp.dot(sgu, wso_ref[...], preferred_element_type=F32)
        a_ref[...] = a
        b_ref[...] = b
        pj3 = pj3_ref[...]
        smr, sms = _sigmoid(pj3[:, :D_MODEL]), _sigmoid(pj3[:, D_MODEL:])
        merged = (smr * a + sms * b).astype(MXU_DTYPE)
        mg_ref[...] = merged
        x1v = x_ref[...] + jnp.dot(merged, wo_ref[...], preferred_element_type=F32)
        xn1, r1 = _rms(x1v)
        hp = (xn1 * gp_ref[...]).astype(MXU_DTYPE)
        hp_ref[...] = hp
        gate = _sigmoid(jnp.dot(hp, wpg_ref[...], preferred_element_type=F32))
        pp = jnp.dot(p_ref[...].astype(MXU_DTYPE), wpp_ref[...], preferred_element_type=F32)
        xn2, r2 = _rms(x1v + gate * pp)
        err = xn2 * gf_ref[...] - t_ref[...]
        loss_ref[...] += (0.5 / D_MODEL) * jnp.sum(jnp.sum(err * err, axis=1, keepdims=True), axis=0, keepdims=True)

        dy = err * (1.0 / D_MODEL)
        dgf_ref[...] += jnp.sum(dy * xn2, axis=0, keepdims=True)
        dx2 = _rms_bwd(dy * gf_ref[...], xn2, r2)
        dpp_ref[...] = (dx2 * gate).astype(MXU_DTYPE)
        dz = (dx2 * pp * gate * (1.0 - gate)).astype(MXU_DTYPE)
        dz_ref[...] = dz
        dhp = _mm_nt(dz, wpg_ref[...])
        dgp_ref[...] += jnp.sum(dhp * xn1, axis=0, keepdims=True)
        dx1_ref[...] = dx2 + _rms_bwd(dhp * gp_ref[...], xn1, r1)

    sq = (D_MODEL, D_MODEL)
    vec = _resident((1, D_MODEL))
    whole = lambda shape: pl.BlockSpec(shape, lambda i: (0,) * len(shape))
    return pl.pallas_call(
        body, name="tail_fwd", grid=(t // tm,),
        in_specs=[_proj_tile(j) for j in (1, 2, 3)]
        + [_row_tile(D_MODEL), _row_tile(D_MODEL), _row_tile(PLE_DIM), _row_tile(D_MODEL), _resident(ws.shape),
           _resident(bst.shape), vec, vec, _resident(sq), _resident(sq), _resident(sq), _resident(sq),
           _resident((PLE_DIM, D_MODEL))],
        out_specs=[_row_tile(D_MODEL)] * 9 + [whole(LOSS_TILE), whole((1, D_MODEL)), whole((1, D_MODEL))],
        out_shape=[jax.ShapeDtypeStruct((t, D_MODEL), F32)] * 3 + [jax.ShapeDtypeStruct((t, D_MODEL), MXU_DTYPE)] * 6
        + [jax.ShapeDtypeStruct(LOSS_TILE, F32), jax.ShapeDtypeStruct((1, D_MODEL), F32),
           jax.ShapeDtypeStruct((1, D_MODEL), F32)],
        compiler_params=_params(("arbitrary",)),
    )(pj, pj, pj, ret_raw, x2d, p2d, target, ws, bst, g_ple, g_final, w_ro, w_so, w_o, w_pg, w_pp)


def _retention_bwd_call(pj0, drr, states, consts, n_seq, seq, side):
    cos_full, sin_signed, decay_in, zeta, xi = consts
    nb = seq // RET_TILE
    cpb = RET_TILE // CHUNK

    def body(pj_ref, do_ref, st_ref, cos_ref, sin_ref, d_ref, zeta_ref, xi_ref, dpj_ref, gstate):
        @pl.when(pl.program_id(1) == 0)
        def _():
            gstate[...] = jnp.zeros_like(gstate)

        for cc in reversed(range(cpb)):
            rows = slice(cc * CHUNK, (cc + 1) * CHUNK)
            cs, sn = cos_ref[rows, :], sin_ref[rows, :]
            for h in range(RET_HEADS):
                q, k, v = _qkv(pj_ref, rows, h)
                qt = (q * cs + _rot(q) * sn) * QK_SCALE
                kt = k * cs + _rot(k) * sn
                d_out = do_ref[rows, h * V_DIM:(h + 1) * V_DIM]
                prev = st_ref[cc, h]
                g = gstate[h]
                dec = d_ref[h]
                scores_d = _mm_nt(qt, kt) * dec
                dscores = _mm_nt(d_out, v) * dec
                kz = kt * zeta_ref[h]
                qx = qt * xi_ref[h]
                dv = _mm_tn(scores_d, d_out) + _mm(kz, g)
                dqt = (_mm(dscores, kt) + _mm_nt(d_out, prev) * xi_ref[h]) * QK_SCALE
                dkt = _mm_tn(dscores, qt) + _mm_nt(v, g) * zeta_ref[h]
                gstate[h] = _mm_tn(qx, d_out) + CHUNK_DECAY[h] * g
                dq = dqt * cs + _rot(dqt * sn)
                dk = dkt * cs + _rot(dkt * sn)
                dpj_ref[rows, h * QK_DIM:(h + 1) * QK_DIM] = dq.astype(dpj_ref.dtype)
                dpj_ref[rows, RET_HEADS * QK_DIM + h * QK_DIM:RET_HEADS * QK_DIM + (h + 1) * QK_DIM] = dk.astype(
                    dpj_ref.dtype)
                dpj_ref[rows, 2 * RET_HEADS * QK_DIM + h * V_DIM:2 * RET_HEADS * QK_DIM + (h + 1) * V_DIM] = dv.astype(
                    dpj_ref.dtype)

    row = lambda b, n: (b * nb + nb - 1 - n, 0)
    pos = lambda b, n: (nb - 1 - n, 0)
    return _compute_call(
        body, name="retention_bwd", grid=(n_seq, nb),
        in_specs=[pl.BlockSpec((None, RET_TILE, SHARD_W), lambda b, n: (0,) + row(b, n)),
                  pl.BlockSpec((RET_TILE, D_MODEL), row),
                  pl.BlockSpec((cpb, RET_HEADS, QK_DIM, V_DIM), lambda b, n: (b * nb + nb - 1 - n, 0, 0, 0)),
                  pl.BlockSpec((RET_TILE, QK_DIM), pos), pl.BlockSpec((RET_TILE, QK_DIM), pos),
                  _resident(decay_in.shape), _resident(zeta.shape), _resident(xi.shape)],
        out_specs=[pl.BlockSpec((RET_TILE, SHARD_W), row)],
        out_shape=[jax.ShapeDtypeStruct((n_seq * seq, SHARD_W), MXU_DTYPE)],
        scratch_shapes=[pltpu.VMEM((RET_HEADS, QK_DIM, V_DIM), F32)],
        operands=(pj0, drr, states, cos_full, sin_signed, decay_in, zeta, xi),
        semantics=("arbitrary", "arbitrary"), side=side)


def _dx_call(dpj, x2d, dx1, g_mixer, wg_in, side):
    t = x2d.shape[0]

    def body(d0, d1, d2, d3, x_ref, dx1_ref, g_ref, w_ref, dx_ref, dg_ref):
        @pl.when(pl.program_id(0) == 0)
        def _():
            dg_ref[...] = jnp.zeros_like(dg_ref)

        dh = _mm_nt(d0[...], w_ref[0])
        for j, d_ref in enumerate((d1, d2, d3)):
            dh += _mm_nt(d_ref[...], w_ref[j + 1])
        xn, r = _rms(x_ref[...])
        dg_ref[...] += jnp.sum(dh * xn, axis=0, keepdims=True)
        dx_ref[...] = dx1_ref[...] + _rms_bwd(dh * g_ref[...], xn, r)

    return _compute_call(
        body, name="dx_bwd", grid=(t // DX_TILE,),
        in_specs=[_row_tile(SHARD_W, DX_TILE)] * N_CHIPS + [_row_tile(D_MODEL, DX_TILE)] * 2
        + [_resident((1, D_MODEL)), _resident(wg_in.shape)],
        out_specs=[_row_tile(D_MODEL, DX_TILE), pl.BlockSpec((1, D_MODEL), lambda i: (0, 0))],
        out_shape=[jax.ShapeDtypeStruct((t, D_MODEL), F32), jax.ShapeDtypeStruct((1, D_MODEL), F32)],
        operands=(*dpj, x2d, dx1, g_mixer, wg_in), semantics=("arbitrary",), side=side)


def _wgrad_call(name, lhs, rhs, block_n, out_cols=None, block_t=1024, into=None, slot=0, n_slots=1, side=None):
    t, n = rhs.shape
    k = lhs.shape[1]
    steps = t // block_t
    out_cols = block_n if out_cols is None else out_cols
    per = block_n // out_cols
    first_block = slot * (n // block_n)

    def body(l_ref, r_ref, *rest):
        o_ref, acc = rest[-2:]
        @pl.when(pl.program_id(1) == 0)
        def _():
            acc[...] = jnp.zeros_like(acc)

        acc[...] += _mm_tn(l_ref[...], r_ref[...])

        @pl.when(pl.program_id(1) == steps - 1)
        def _():
            for s in range(per):
                o_ref[s] = acc[:, s * out_cols:(s + 1) * out_cols].astype(o_ref.dtype)

    res = _compute_call(
        body, name=name, grid=(n // block_n, steps),
        in_specs=[pl.BlockSpec((block_t, k), lambda j, i: (i, 0)), pl.BlockSpec((block_t, block_n), lambda j, i: (i, j))]
        + ([] if into is None else [ANY]),
        out_specs=[pl.BlockSpec((per, k, out_cols), lambda j, i: (first_block + j, 0, 0))],
        out_shape=[jax.ShapeDtypeStruct((n_slots * (n // out_cols), k, out_cols), COMM_DTYPE)],
        scratch_shapes=[pltpu.VMEM((k, block_n), F32)], aliases={} if into is None else {2: 0},
        operands=(lhs, rhs) if into is None else (lhs, rhs, into), semantics=("arbitrary", "arbitrary"), side=side)
    return res[0] if side is None else (res[0][0], res[1])


def _position():
    return lax.axis_index("x"), lax.axis_index("y"), lax.axis_index("c")


def _position_array():
    x, y, c = _position()
    return jnp.stack([2 * x + y, c]).astype(jnp.int32)


def _other_chip(x, y, k):
    return (1 - x if k & 2 else x), (1 - y if k & 1 else y)


def _plan_side(operands, out_shapes, plan, n_remote, aliases=None):
    def copies(ins, outs, send_sem, recv_sem, base=0):
        remote = plan(ins, outs)
        assert len(remote) == n_remote
        return [pltpu.make_async_remote_copy(src_ref=src, dst_ref=dst, send_sem=send_sem.at[base + i],
                                             recv_sem=recv_sem.at[base + i], device_id=dev, device_id_type=MESH)
                for i, (src, dst, dev) in enumerate(remote)]

    def start(*a):
        for cp in copies(*a):
            cp.start()

    def finish(*a):
        for cp in copies(*a):
            cp.wait()

    return _Side(operands, out_shapes, n_remote, start, finish, aliases)


def _place_cast_call(name, w, pos):
    rows, width = w.shape
    block_rows = min(256, rows)

    def body(pos_ref, w_ref, o_ref):
        o_ref[...] = w_ref[...].astype(o_ref.dtype)

    return pl.pallas_call(
        body, name=name,
        grid_spec=pltpu.PrefetchScalarGridSpec(
            num_scalar_prefetch=1, grid=(rows // block_rows,),
            in_specs=[pl.BlockSpec((block_rows, width), lambda i, pos: (i, 0))],
            out_specs=pl.BlockSpec((None, block_rows, width), lambda i, pos: (pos[0], i, 0))),
        out_shape=jax.ShapeDtypeStruct((N_CHIPS, rows, width), MXU_DTYPE),
        compiler_params=_params(("arbitrary",)),
    )(pos, w)


def _gather_side(placed):
    n = len(placed)

    def copies(kind, bufs, send_sem, recv_sem, base):
        x, y, c = _position()
        me = 2 * x + y
        made = []
        for i in range(n):
            hr = placed[i].shape[1] // 2
            for k in (1, 2, 3):
                px, py = _other_chip(x, y, k)
                chip, core, slot, dev = [(me, c, k - 1, (px, py, c)), (2 * px + py, c, 2 + k, (x, y, 1 - c)),
                                         (2 * px + py, 1 - c, 2 + k, (x, y, 1 - c))][kind]
                piece = bufs[i].at[chip, pl.ds(core * hr, hr)]
                made.append(pltpu.make_async_remote_copy(
                    src_ref=piece, dst_ref=piece, send_sem=send_sem.at[base + 6 * i + slot],
                    recv_sem=recv_sem.at[base + 6 * i + slot], device_id=dev, device_id_type=MESH))
        return made

    def start(ins, outs, send_sem, recv_sem, base=0):
        for cp in copies(0, outs, send_sem, recv_sem, base):
            cp.start()

    def finish(ins, outs, send_sem, recv_sem, base=0):
        first, onward = copies(0, outs, send_sem, recv_sem, base), copies(1, outs, send_sem, recv_sem, base)
        for landed, cp in zip(first, onward):
            landed.wait_recv()
            cp.start()
        for cp in copies(2, outs, send_sem, recv_sem, base):
            cp.wait_recv()
        for cp in first + onward:
            cp.wait_send()

    return _Side(placed, [jax.ShapeDtypeStruct(a.shape, a.dtype) for a in placed], 6 * n, start, finish,
                 {i: i for i in range(n)})


def _sum_call(name, pos, terms, rows, width, out_dtype, block_rows, out_rows=None, out_index=None):
    out_rows = rows if out_rows is None else out_rows
    out_index = (lambda i, pos, br: i) if out_index is None else out_index

    def body(pos_ref, *refs):
        acc = refs[0][...].astype(F32)
        for r in refs[1:-1]:
            acc = acc + r[...].astype(F32)
        refs[-1][...] = acc.astype(out_dtype)

    def spec(index):
        return pl.BlockSpec((block_rows, width), lambda i, pos: (index(i, pos, block_rows), 0))

    return pl.pallas_call(
        body, name=name,
        grid_spec=pltpu.PrefetchScalarGridSpec(
            num_scalar_prefetch=1, grid=(rows // block_rows,),
            in_specs=[spec(index) for _, index in terms], out_specs=spec(out_index)),
        out_shape=jax.ShapeDtypeStruct((out_rows, width), out_dtype),
        compiler_params=_params(("arbitrary",)),
    )(pos, *[arr for arr, _ in terms])


def _pair_stage(tag, grads, pos):
    n = len(grads)
    shapes = [(g.shape[0] // N_CHIPS, g.shape[1]) for g in grads]

    def plan(ins, outs):
        x, y, c = _position()
        remote = []
        for i, (r, _) in enumerate(shapes):
            hr = r // 2
            for j in range(N_CHIPS):
                remote.append((ins[i].at[pl.ds(j * r + (1 - c) * hr, hr)], outs[i].at[pl.ds(j * hr, hr)],
                               (x, y, 1 - c)))
        return remote

    pair = _exchange_call(f"grad_pair_exchange_{tag}", _plan_side(
        grads, [jax.ShapeDtypeStruct((N_CHIPS * (r // 2), w), COMM_DTYPE) for r, w in shapes], plan, N_CHIPS * n))

    def own_half(r):
        def index(s, pos, br):
            per = (r // 2) // br
            return (s // per) * (r // br) + pos[1] * per + s % per
        return index

    return [_sum_call(f"grad_pair_sum_{tag}{i}", pos, [(grads[i], own_half(r)), (pair[i], lambda s, pos, br: s)],
                      N_CHIPS * (r // 2), w, COMM_DTYPE, min(256, r // 2)) for i, (r, w) in enumerate(shapes)]


def _chip_side(pair_sums):
    halves = [(p.shape[0] // N_CHIPS, p.shape[1]) for p in pair_sums]

    def plan(ins, outs):
        x, y, c = _position()
        remote = []
        for i, (hr, _) in enumerate(halves):
            for k in (1, 2, 3):
                px, py = _other_chip(x, y, k)
                remote.append((ins[i].at[pl.ds((2 * px + py) * hr, hr)], outs[i].at[pl.ds((k - 1) * hr, hr)],
                               (px, py, c)))
        return remote

    return _plan_side(pair_sums, [jax.ShapeDtypeStruct((3 * hr, w), COMM_DTYPE) for hr, w in halves], plan,
                      3 * len(pair_sums))


def _finish_stage(tag, pair_sums, chip, pos, extra=None):
    n = len(pair_sums)
    halves = [(p.shape[0] // N_CHIPS, p.shape[1]) for p in pair_sums]

    def slab(k, hr):
        return lambda s, pos, br: k * (hr // br) + s

    reduced = []
    for i, (hr, w) in enumerate(halves):
        terms = [(pair_sums[i], lambda s, pos, br, hr=hr: pos[0] * (hr // br) + s)]
        terms += [(chip[i], slab(k, hr)) for k in range(3)]
        reduced.append(_sum_call(f"grad_chip_sum_{tag}{i}", pos, terms, hr, w, F32, min(256, hr), out_rows=2 * hr,
                                 out_index=lambda s, pos, br, hr=hr: pos[1] * (hr // br) + s))

    def plan(ins, outs):
        x, y, c = _position()
        remote = []
        for i, (hr, _) in enumerate(halves):
            mine = outs[i].at[pl.ds(c * hr, hr)]
            remote.append((mine, mine, (x, y, 1 - c)))
        return remote

    swap = _plan_side(reduced, [jax.ShapeDtypeStruct((2 * hr, w), F32) for hr, w in halves], plan, n,
                      {i: i for i in range(n)})
    if extra is None:
        return _exchange_call(f"grad_half_exchange_{tag}", swap)
    res = _exchange_call(f"grad_half_exchange_{tag}", _join_sides(swap, extra))
    return res[:n], res[n:]


def _small_gather_side(parts):
    n = len(parts)

    def copies(ins, outs, send_sem, recv_sem, base):
        x, y, c = _position()
        made = []
        for i in range(n):
            mine = outs[i].at[4 * x + 2 * y + c]
            made.append(pltpu.make_async_copy(ins[i], mine, send_sem.at[base + 8 * i + 7]))
            for d in range(1, 8):
                px, py = _other_chip(x, y, d >> 1)
                made.append(pltpu.make_async_remote_copy(
                    src_ref=ins[i], dst_ref=mine, send_sem=send_sem.at[base + 8 * i + d - 1],
                    recv_sem=recv_sem.at[base + 8 * i + d - 1], device_id=(px, py, 1 - c if d & 1 else c),
                    device_id_type=MESH))
        return made

    def start(ins, outs, send_sem, recv_sem, base=0):
        for cp in copies(ins, outs, send_sem, recv_sem, base):
            cp.start()

    def finish(ins, outs, send_sem, recv_sem, base=0):
        for cp in copies(ins, outs, send_sem, recv_sem, base):
            cp.wait()

    return _Side(parts, [jax.ShapeDtypeStruct((8,) + a_.shape, F32) for a_ in parts], 8 * n, start, finish)


def _adamw(w, g, m, v):
    m = ADAM_B1 * m + (1.0 - ADAM_B1) * g
    v = ADAM_B2 * v + (1.0 - ADAM_B2) * (g * g)
    m_hat = m / (1.0 - ADAM_B1 ** ADAM_STEP)
    v_hat = v / (1.0 - ADAM_B2 ** ADAM_STEP)
    delta = -ADAM_LR * (m_hat / (jnp.sqrt(v_hat) + ADAM_EPS) + ADAM_WD * w)
    return delta, m, v


def _adamw_call(name, w, g, m, v):
    rows, width = w.shape
    block_rows = min(256, rows)

    def body(w_ref, g_ref, m_ref, v_ref, d_out, m_out, v_out):
        d_out[...], m_out[...], v_out[...] = _adamw(w_ref[...], g_ref[...], m_ref[...], v_ref[...])

    spec = pl.BlockSpec((block_rows, width), lambda i: (i, 0))
    return pl.pallas_call(
        body, name=name, grid=(rows // block_rows,), in_specs=[spec] * 4, out_specs=[spec] * 3,
        out_shape=[jax.ShapeDtypeStruct(w.shape, F32)] * 3,
        compiler_params=_params(("arbitrary",)),
    )(w, g, m, v)


def _small_adamw_call(gathered, weights, moments_m, moments_v):
    n = len(weights)

    def body(*refs):
        all_refs, refs = refs[:n + 1], refs[n + 1:]
        w_refs, m_refs, v_refs, outs = refs[:n], refs[n:2 * n], refs[2 * n:3 * n], refs[3 * n:]

        def total(ref):
            acc = ref[0]
            for d in range(1, 8):
                acc = acc + ref[d]
            return acc

        outs[0][...] = total(all_refs[n])
        for i in range(n):
            g = total(all_refs[i])
            if i == 0:
                row = lax.broadcasted_iota(jnp.int32, g.shape, 0)
                col = lax.broadcasted_iota(jnp.int32, g.shape, 1)
                g = jnp.where((row % CHUNK) >= col, g, 0.0)
            g_out, d_out, m_out, v_out = outs[1 + 4 * i:5 + 4 * i]
            g_out[...] = g
            d_out[...], m_out[...], v_out[...] = _adamw(w_refs[i][...], g, m_refs[i][...], v_refs[i][...])

    out_shape = [jax.ShapeDtypeStruct(LOSS_TILE, F32)]
    for w in weights:
        out_shape += [jax.ShapeDtypeStruct(w.shape, F32)] * 4
    res = pl.pallas_call(
        body, name="small_adamw", out_shape=out_shape,
        compiler_params=pltpu.CompilerParams(vmem_limit_bytes=VMEM_LIMIT),
    )(*gathered, *weights, *moments_m, *moments_v)
    return res[0], [res[1 + 4 * i:5 + 4 * i] for i in range(n)]


def kernel(x, p, w_in, w_ret_out, w_sgu_out, w_out, sgu_ws, sgu_bs, w_ple_gate, w_ple_proj, g_mixer, g_ple, g_final, loss_target, m_w_in, m_w_ret_out, m_w_sgu_out, m_w_out, m_sgu_ws, m_sgu_bs, m_w_ple_gate, m_w_ple_proj, m_g_mixer, m_g_ple, m_g_final, v_w_in, v_w_ret_out, v_w_sgu_out, v_w_out, v_sgu_ws, v_sgu_bs, v_w_ple_gate, v_w_ple_proj, v_g_mixer, v_g_ple, v_g_final):
    n_seq, seq, _ = x.shape
    t = n_seq * seq
    x2d = x.reshape(t, D_MODEL)
    p2d = p.reshape(t, PLE_DIM)
    target = loss_target.reshape(t, D_MODEL)
    big = [w_in[0], w_ret_out[0], w_sgu_out[0], w_out[0], w_ple_gate[0], w_ple_proj[0]]
    big_m = [m_w_in[0], m_w_ret_out[0], m_w_sgu_out[0], m_w_out[0], m_w_ple_gate[0], m_w_ple_proj[0]]
    big_v = [v_w_in[0], v_w_ret_out[0], v_w_sgu_out[0], v_w_out[0], v_w_ple_gate[0], v_w_ple_proj[0]]

    pos = _position_array()
    placed = [_place_cast_call(f"place_weight_{i}", w, pos) for i, w in enumerate(big)]
    ws = sgu_ws[0]
    bst = sgu_bs[0].T
    consts = _retention_consts(seq)

    (h, pj, wg_in), gathered = _proj_call(x2d, g_mixer, placed[0], pos, _gather_side(placed[1:]))
    pj0 = pj1 = pj2 = pj3 = pj
    w_ro, w_so, w_o, w_pg = (w.reshape(D_MODEL, D_MODEL) for w in gathered[:4])
    w_pp = gathered[4].transpose(1, 0, 2).reshape(PLE_DIM, D_MODEL)
    ret_raw, states = _retention_fwd_call(pj0, consts, n_seq, seq)
    dx1, a, b, ret, sgu, merged, hp, dz, dpp, loss, dg_ple, dg_final = _tail_call(
        pj, ret_raw, x2d, p2d, target, ws, bst, g_ple, g_final.reshape(1, D_MODEL), w_ro, w_so, w_o, w_pg, w_pp)
    dpj1, dpj2, dpj3, drr, da, db, dws, dbst = _merge_bwd_call(dx1, pj1, pj2, pj3, ret_raw, a, b, ws, bst,
                                                               w_ro, w_so, w_o)

    small_shapes = [(SGU_GROUPS * CHUNK, CHUNK), (SGU_GROUPS, CHUNK), (1, D_MODEL), (1, D_MODEL), (1, D_MODEL)]
    as_small = lambda arrays: [None if a_ is None else a_.reshape(s_) for a_, s_ in zip(arrays, small_shapes)]
    early = as_small([dws, dbst.T, None, dg_ple, dg_final])
    g_ret_out, early_all = _wgrad_call("wgrad_ret_out", ret, da, D_MODEL,
                                       side=_small_gather_side([early[0], early[1], early[3], early[4], loss]))
    rows_of = lambda g: g.reshape(g.shape[0] * g.shape[1], g.shape[2])
    tail_grads = [
        rows_of(g_ret_out),
        rows_of(_wgrad_call("wgrad_sgu_out", sgu, db, D_MODEL)),
        rows_of(_wgrad_call("wgrad_out", merged, dx1, D_MODEL)),
        rows_of(_wgrad_call("wgrad_ple_gate", hp, dz, D_MODEL)),
        rows_of(_wgrad_call("wgrad_ple_proj", p2d, dpp, D_MODEL, out_cols=PLE_DIM)),
    ]
    tail_sums = _pair_stage("tail", tail_grads, pos)
    (dpj0,), tail_chip = _retention_bwd_call(pj0, drr, states, consts, n_seq, seq, _chip_side(tail_sums))
    in_grad = None
    for j, d in enumerate((dpj0, dpj1, dpj2, dpj3)):
        in_grad = _wgrad_call(f"wgrad_in_{j}", h, d, SHARD_W, into=in_grad, slot=j, n_slots=N_CHIPS)
    in_sums = _pair_stage("in", [rows_of(in_grad)], pos)
    (dx, dg_mixer), in_chip = _dx_call((dpj0, dpj1, dpj2, dpj3), x2d, dx1, g_mixer, wg_in, _chip_side(in_sums))
    g_in, (mixer_all,) = _finish_stage("in", in_sums, in_chip, pos, _small_gather_side([dg_mixer]))
    g_big = g_in + _finish_stage("tail", tail_sums, tail_chip, pos)
    upd = [_adamw_call(f"adamw_{i}", big[i], g_big[i], big_m[i], big_v[i]) for i in range(len(big))]

    small_g = [early_all[0], early_all[1], mixer_all, early_all[2], early_all[3], early_all[4]]
    total, small = _small_adamw_call(small_g, as_small([sgu_ws, sgu_bs, g_mixer, g_ple, g_final]),
                                     as_small([m_sgu_ws, m_sgu_bs, m_g_mixer, m_g_ple, m_g_final]),
                                     as_small([v_sgu_ws, v_sgu_bs, v_g_mixer, v_g_ple, v_g_final]))
    out_small_shapes = [sgu_ws.shape, sgu_bs.shape, g_mixer.shape, g_ple.shape, g_final.shape]

    def ordered(big_list, kind):
        w_in_, w_ro_, w_so_, w_o_, w_pg_, w_pp_ = [b_[None] for b_ in big_list]
        s_ws, s_bs, s_gm, s_gp, s_gf = [small[i][kind].reshape(s) for i, s in enumerate(out_small_shapes)]
        return [w_in_, w_ro_, w_so_, w_o_, s_ws, s_bs, w_pg_, w_pp_, s_gm, s_gp, s_gf]

    out = [total[0, 0], dx.reshape(x.shape)]
    out += ordered(g_big, 0)
    out += ordered([u[0] for u in upd], 1)
    out += ordered([u[1] for u in upd], 2)
    out += ordered([u[2] for u in upd], 3)
    return tuple(out)
```

```python
import functools
import math

import numpy as np
import jax
import jax.numpy as jnp
from jax import lax
from jax.experimental import pallas as pl
from jax.experimental.pallas import tpu as pltpu

F32 = jnp.float32
MXU_DTYPE = jnp.bfloat16
COMM_DTYPE = jnp.bfloat16

D_MODEL = 1024
RET_HEADS = 4
QK_DIM = 128
V_DIM = 256
CHUNK = 128
SGU_GROUPS = 4
GROUP_DIM = 256
PLE_DIM = 256
N_CHIPS = 4
SHARD_W = 2048
ROPE_BASE = 10000.0
NORM_EPS = 1e-6
GN_EPS = 1e-5
QK_SCALE = QK_DIM ** -0.5
SQRT_HALF = math.sqrt(0.5)
INV_SQRT_2PI = 1.0 / math.sqrt(2.0 * math.pi)

ADAM_LR = 0.001
ADAM_B1 = 0.9
ADAM_B2 = 0.999
ADAM_EPS = 1e-08
ADAM_WD = 0.01
ADAM_STEP = 10

TILE_M = 256
LOSS_TILE = (8, 128)
LOAD_PARTS = 16
PROJ_TILE = 512
RET_TILE = 512
DX_TILE = 512
VMEM_LIMIT = 56 * 1024 * 1024
MESH = pl.DeviceIdType.MESH
ANY = pl.BlockSpec(memory_space=pl.ANY)

CHUNK_DECAY = tuple(
    float(np.exp(np.float32(CHUNK) * np.log(np.float32(1.0 - 2.0 ** (-5.0 - h))))) for h in range(RET_HEADS))


def _mm(a, b):
    return jnp.dot(a.astype(MXU_DTYPE), b.astype(MXU_DTYPE), preferred_element_type=F32)


def _mm_nt(a, b):
    return lax.dot_general(a.astype(MXU_DTYPE), b.astype(MXU_DTYPE), (((1,), (1,)), ((), ())),
                           preferred_element_type=F32)


def _mm_tn(a, b):
    return lax.dot_general(a.astype(MXU_DTYPE), b.astype(MXU_DTYPE), (((0,), (0,)), ((), ())),
                           preferred_element_type=F32)


def _mean(x):
    return jnp.mean(x, axis=-1, keepdims=True)


def _sigmoid(x):
    return jax.nn.sigmoid(x)


def _silu_and_grad(x):
    s = _sigmoid(x)
    return x * s, s * (1.0 + x * (1.0 - s))


def _gelu(x):
    return 0.5 * x * (1.0 + lax.erf(x * SQRT_HALF))


def _gelu_grad(x):
    return 0.5 * (1.0 + lax.erf(x * SQRT_HALF)) + x * jnp.exp(-0.5 * x * x) * INV_SQRT_2PI


def _unit_norm(x, eps):
    xc = x - _mean(x)
    rstd = lax.rsqrt(_mean(xc * xc) + eps)
    return xc * rstd, rstd


def _unit_norm_bwd(dn, n, rstd):
    return rstd * (dn - _mean(dn) - n * _mean(dn * n))


def _rms(x):
    r = lax.rsqrt(_mean(x * x) + NORM_EPS)
    return x * r, r


def _rms_bwd(dxn, xn, r):
    return r * (dxn - xn * _mean(dxn * xn))


def _rot(x):
    return pltpu.roll(x, QK_DIM // 2, 1)


def _params(semantics, **kw):
    return pltpu.CompilerParams(dimension_semantics=semantics, vmem_limit_bytes=VMEM_LIMIT, **kw)


def _row_tile(width, tm=TILE_M):
    return pl.BlockSpec((tm, width), lambda i: (i, 0))


def _proj_tile(j):
    return pl.BlockSpec((None, TILE_M, SHARD_W), lambda i: (j, i, 0))


def _resident(shape):
    nd = len(shape)
    return pl.BlockSpec(shape, lambda *_: (0,) * nd, pipeline_mode=pl.Buffered(1))


def _causal_ws(ws_ref):
    row = lax.broadcasted_iota(jnp.int32, (CHUNK, CHUNK), 0)
    col = lax.broadcasted_iota(jnp.int32, (CHUNK, CHUNK), 1)
    return [jnp.where(row >= col, ws_ref[g], 0.0).astype(MXU_DTYPE) for g in range(SGU_GROUPS)]


def _heads(x, width):
    return [x[:, h * width:(h + 1) * width] for h in range(x.shape[1] // width)]


def _branch_forward(pj1, pj2, ret_raw, wsc, bst):
    rg, su = pj1[:, :D_MODEL], pj1[:, D_MODEL:]
    sv, sg = pj2[:, :D_MODEL], pj2[:, D_MODEL:]
    rn_parts, rstd_parts = zip(*[_unit_norm(r, GN_EPS) for r in _heads(ret_raw, V_DIM)])
    rn = jnp.concatenate(rn_parts, axis=1)
    sil_rg, dsil_rg = _silu_and_grad(rg)
    ret = rn * sil_rg
    u = _gelu(su)
    vn, rstd_v = _unit_norm(_gelu(sv), GN_EPS)
    rows = []
    for cc in range(pj1.shape[0] // CHUNK):
        cols = []
        for g in range(SGU_GROUPS):
            blk = vn[cc * CHUNK:(cc + 1) * CHUNK, g * GROUP_DIM:(g + 1) * GROUP_DIM]
            cols.append(_mm(wsc[g], blk) + bst[:, g:g + 1])
        rows.append(jnp.concatenate(cols, axis=1))
    mixed = jnp.concatenate(rows, axis=0)
    sil_sg, dsil_sg = _silu_and_grad(sg)
    sgu = u * mixed * sil_sg
    return dict(rg=rg, su=su, sv=sv, sg=sg, rn=rn, rstd_r=rstd_parts, sil_rg=sil_rg, dsil_rg=dsil_rg, ret=ret,
                u=u, vn=vn, rstd_v=rstd_v, mixed=mixed, sil_sg=sil_sg, dsil_sg=dsil_sg, sgu=sgu)


class _Side:
    def __init__(self, operands, out_shapes, n_sems, start, finish, aliases=None):
        self.operands, self.out_shapes, self.n_sems = list(operands), list(out_shapes), n_sems
        self.start, self.finish, self.aliases = start, finish, dict(aliases or {})


def _join_sides(*sides):
    spans, a, b, s = [], 0, 0, 0
    for side in sides:
        spans.append((a, b, s))
        a, b, s = a + len(side.operands), b + len(side.out_shapes), s + side.n_sems

    def run(which):
        def go(ins, outs, send_sem, recv_sem, base=0):
            for side, (a0, b0, s0) in zip(sides, spans):
                getattr(side, which)(ins[a0:a0 + len(side.operands)], outs[b0:b0 + len(side.out_shapes)],
                                     send_sem, recv_sem, base + s0)
        return go

    aliases = {a0 + i: b0 + o for side, (a0, b0, _) in zip(sides, spans) for i, o in side.aliases.items()}
    return _Side([x for side in sides for x in side.operands], [x for side in sides for x in side.out_shapes], s,
                 run("start"), run("finish"), aliases)


def _compute_call(body, *, name, grid, in_specs, out_specs, out_shape, operands, semantics, scratch_shapes=(),
                  side=None, prefetch=None, aliases=None, side_start=None):
    n_pre = 0 if prefetch is None else 1
    pre = () if prefetch is None else (prefetch,)

    def spec(in_specs, out_specs, scratch):
        return pltpu.PrefetchScalarGridSpec(num_scalar_prefetch=n_pre, grid=grid, in_specs=in_specs,
                                            out_specs=out_specs, scratch_shapes=scratch)

    if side is None:
        return pl.pallas_call(body, name=name, grid_spec=spec(in_specs, out_specs, list(scratch_shapes)),
                              out_shape=out_shape,
                              input_output_aliases={n_pre + a: b for a, b in (aliases or {}).items()},
                              compiler_params=_params(semantics))(*pre, *operands)
    n_in, n_out, s_in, s_out = len(operands), len(out_shape), len(side.operands), len(side.out_shapes)

    def carrier(*refs):
        pre_refs, refs = refs[:n_pre], refs[n_pre:]
        ins, refs = refs[:n_in], refs[n_in:]
        side_ins, refs = refs[:s_in], refs[s_in:]
        outs, refs = refs[:n_out], refs[n_out:]
        side_outs, refs = refs[:s_out], refs[s_out:]
        scratch, (send_sem, recv_sem) = refs[:-2], refs[-2:]
        ids = [pl.program_id(a) for a in range(len(grid))]
        at = (0,) * len(grid) if side_start is None else side_start
        first = functools.reduce(jnp.logical_and, [i == a for i, a in zip(ids, at)])
        last = functools.reduce(jnp.logical_and, [i == g - 1 for i, g in zip(ids, grid)])

        @pl.when(first)
        def _():
            side.start(side_ins, side_outs, send_sem, recv_sem)

        body(*pre_refs, *ins, *outs, *scratch)

        @pl.when(last)
        def _():
            side.finish(side_ins, side_outs, send_sem, recv_sem)

    all_aliases = {n_pre + a: b for a, b in (aliases or {}).items()}
    all_aliases.update({n_pre + n_in + a: n_out + b for a, b in side.aliases.items()})
    res = pl.pallas_call(
        carrier, name=name,
        grid_spec=spec(list(in_specs) + [ANY] * s_in, list(out_specs) + [ANY] * s_out,
                       list(scratch_shapes) + [pltpu.SemaphoreType.DMA((side.n_sems,))] * 2),
        out_shape=list(out_shape) + side.out_shapes, input_output_aliases=all_aliases,
        compiler_params=_params(semantics, has_side_effects=True),
    )(*pre, *operands, *side.operands)
    return res[:n_out], res[n_out:]


def _exchange_call(name, side):
    s_in = len(side.operands)

    def body(*refs):
        ins, outs = refs[:s_in], refs[s_in:s_in + len(side.out_shapes)]
        send_sem, recv_sem = refs[s_in + len(side.out_shapes):]
        side.start(ins, outs, send_sem, recv_sem)
        side.finish(ins, outs, send_sem, recv_sem)

    return pl.pallas_call(
        body, name=name, in_specs=[ANY] * s_in, out_specs=[ANY] * len(side.out_shapes), out_shape=side.out_shapes,
        scratch_shapes=[pltpu.SemaphoreType.DMA((side.n_sems,))] * 2, input_output_aliases=side.aliases,
        compiler_params=pltpu.CompilerParams(has_side_effects=True),
    )(*side.operands)


def _proj_call(x2d, g_mixer, placed_in, pos, side):
    t = x2d.shape[0]
    nt = t // PROJ_TILE
    hr = placed_in.shape[1] // 2

    def body(pos_ref, x_ref, g_ref, win_ref, h_ref, pj_ref, w_ref, w_vmem, h_all, loc_sem, send_sem, recv_sem):
        k, i = pl.program_id(0), pl.program_id(1)
        x, y, c = _position()
        me = 2 * x + y

        def copy(slot, chip, core, dev):
            piece = w_ref.at[chip, pl.ds(core * hr, hr)]
            return pltpu.make_async_remote_copy(src_ref=piece, dst_ref=piece, send_sem=send_sem.at[slot],
                                                recv_sem=recv_sem.at[slot], device_id=dev, device_id_type=MESH)

        def first_hop(kk):
            px, py = _other_chip(x, y, kk)
            return copy(kk - 1, me, c, (px, py, c))

        def relay():
            source = jnp.bitwise_xor(me, 2 - c)
            return copy(2, source, c, (jnp.bitwise_xor(x, c), jnp.bitwise_xor(y, 1 - c), c))

        def onward(kk, core):
            px, py = _other_chip(x, y, kk)
            return copy(2 + kk, 2 * px + py, core, (x, y, 1 - c))

        def loads(kk):
            px, py = _other_chip(x, y, kk)
            rows = w_vmem.shape[1] // LOAD_PARTS
            return [pltpu.make_async_copy(w_ref.at[2 * px + py, pl.ds(r * rows, rows)],
                                          w_vmem.at[kk % 2, pl.ds(r * rows, rows)], loc_sem.at[r])
                    for r in range(LOAD_PARTS)]

        @pl.when(jnp.logical_and(k == 0, i == 0))
        def _():
            for kk in (1, 2):
                first_hop(kk).start()
            for cp in loads(0):
                cp.start()
            for cp in loads(0):
                cp.wait()

        for kk in (1, 2, 3):
            @pl.when(jnp.logical_and(k == kk - 1, i == nt - 1))
            def _(kk=kk):
                if kk == 1:
                    first_hop(1).wait_recv()
                    first_hop(2).wait_recv()
                    relay().start()
                if kk == 3:
                    relay().wait_recv()
                onward(kk, c).start()
                onward(kk, 1 - c).wait_recv()
                for cp in loads(kk):
                    cp.start()

            @pl.when(jnp.logical_and(k == kk, i == 0))
            def _(kk=kk):
                for cp in loads(kk):
                    cp.wait()

        rows = pl.ds(pl.multiple_of(i * PROJ_TILE, PROJ_TILE), PROJ_TILE)

        @pl.when(k == 0)
        def _():
            xn, _ = _rms(x_ref[...])
            h = (xn * g_ref[...]).astype(MXU_DTYPE)
            h_all[rows, :] = h
            h_ref[...] = h

        pj_ref[...] = jnp.dot(h_all[rows, :], w_vmem[k % 2], preferred_element_type=F32)

        @pl.when(jnp.logical_and(k == N_CHIPS - 1, i == nt - 1))
        def _():
            for cp in [first_hop(1), first_hop(2), relay()] + [onward(kk, c) for kk in (1, 2, 3)]:
                cp.wait_send()

    parked = lambda k, i, pos: (jnp.where(k == 0, i, nt - 1), 0)
    return _compute_call(
        body, name="proj_fwd", grid=(N_CHIPS, nt),
        in_specs=[pl.BlockSpec((PROJ_TILE, D_MODEL), parked), pl.BlockSpec((1, D_MODEL), lambda k, i, pos: (0, 0)), ANY],
        out_specs=[pl.BlockSpec((PROJ_TILE, D_MODEL), lambda k, i, pos: (jnp.where(k == 0, i, nt), 0)),
                   pl.BlockSpec((None, PROJ_TILE, SHARD_W), lambda k, i, pos: (jnp.bitwise_xor(pos[0], k), i, 0)),
                   ANY],
        out_shape=[jax.ShapeDtypeStruct((t + PROJ_TILE, D_MODEL), MXU_DTYPE),
                   jax.ShapeDtypeStruct((N_CHIPS, t, SHARD_W), F32),
                   jax.ShapeDtypeStruct(placed_in.shape, placed_in.dtype)],
        scratch_shapes=[pltpu.VMEM((2,) + placed_in.shape[1:], placed_in.dtype), pltpu.VMEM((t, D_MODEL), MXU_DTYPE),
                        pltpu.SemaphoreType.DMA((LOAD_PARTS,)), pltpu.SemaphoreType.DMA((6,)),
                        pltpu.SemaphoreType.DMA((6,))],
        operands=(x2d, g_mixer, placed_in), semantics=("arbitrary", "arbitrary"), side=side, prefetch=pos,
        aliases={2: 2}, side_start=(2, 0))


def _retention_consts(seq):
    half = QK_DIM // 2
    inv = ROPE_BASE ** (-jnp.arange(half, dtype=F32) / half)
    ang = jnp.arange(seq, dtype=F32)[:, None] * inv[None, :]
    cos, sin = jnp.cos(ang), jnp.sin(ang)
    cos_full = jnp.concatenate([cos, cos], axis=1)
    sin_signed = jnp.concatenate([-sin, sin], axis=1)
    log_g = jnp.log(1.0 - 2.0 ** (-5.0 - jnp.arange(RET_HEADS, dtype=F32)))
    idx = jnp.arange(CHUNK, dtype=F32)
    diff = idx[:, None] - idx[None, :]
    decay_in = jnp.where(diff[None] >= 0, jnp.exp(jnp.maximum(diff, 0.0)[None] * log_g[:, None, None]), 0.0)
    zeta = jnp.exp((CHUNK - 1.0 - idx)[None, :] * log_g[:, None])
    xi = jnp.exp((idx + 1.0)[None, :] * log_g[:, None])
    zeta = jnp.broadcast_to(zeta[:, :, None], (RET_HEADS, CHUNK, QK_DIM))
    xi = jnp.broadcast_to(xi[:, :, None], (RET_HEADS, CHUNK, QK_DIM))
    return cos_full, sin_signed, decay_in, zeta, xi


def _qkv(pj_ref, rows, h):
    q = pj_ref[rows, h * QK_DIM:(h + 1) * QK_DIM]
    k = pj_ref[rows, RET_HEADS * QK_DIM + h * QK_DIM:RET_HEADS * QK_DIM + (h + 1) * QK_DIM]
    v = pj_ref[rows, 2 * RET_HEADS * QK_DIM + h * V_DIM:2 * RET_HEADS * QK_DIM + (h + 1) * V_DIM]
    return q, k, v


def _retention_fwd_call(pj0, consts, n_seq, seq):
    cos_full, sin_signed, decay_in, zeta, xi = consts
    nb = seq // RET_TILE
    cpb = RET_TILE // CHUNK

    def body(pj_ref, cos_ref, sin_ref, d_ref, zeta_ref, xi_ref, o_ref, st_ref, state):
        @pl.when(pl.program_id(1) == 0)
        def _():
            state[...] = jnp.zeros_like(state)

        for cc in range(cpb):
            rows = slice(cc * CHUNK, (cc + 1) * CHUNK)
            cs, sn = cos_ref[rows, :], sin_ref[rows, :]
            for h in range(RET_HEADS):
                q, k, v = _qkv(pj_ref, rows, h)
                qt = (q * cs + _rot(q) * sn) * QK_SCALE
                kt = k * cs + _rot(k) * sn
                prev = state[h]
                st_ref[cc, h] = prev.astype(st_ref.dtype)
                scores = _mm_nt(qt, kt) * d_ref[h]
                o_ref[rows, h * V_DIM:(h + 1) * V_DIM] = _mm(scores, v) + _mm(qt * xi_ref[h], prev)
                state[h] = _mm_tn(kt * zeta_ref[h], v) + CHUNK_DECAY[h] * prev

    row = lambda b, n: (b * nb + n, 0)
    pos = lambda b, n: (n, 0)
    return pl.pallas_call(
        body, name="retention_fwd", grid=(n_seq, nb),
        in_specs=[pl.BlockSpec((None, RET_TILE, SHARD_W), lambda b, n: (0,) + row(b, n)),
                  pl.BlockSpec((RET_TILE, QK_DIM), pos),
                  pl.BlockSpec((RET_TILE, QK_DIM), pos), _resident(decay_in.shape), _resident(zeta.shape),
                  _resident(xi.shape)],
        out_specs=[pl.BlockSpec((RET_TILE, D_MODEL), row),
                   pl.BlockSpec((cpb, RET_HEADS, QK_DIM, V_DIM), lambda b, n: (b * nb + n, 0, 0, 0))],
        out_shape=[jax.ShapeDtypeStruct((n_seq * seq, D_MODEL), F32),
                   jax.ShapeDtypeStruct((n_seq * seq // CHUNK, RET_HEADS, QK_DIM, V_DIM), MXU_DTYPE)],
        scratch_shapes=[pltpu.VMEM((RET_HEADS, QK_DIM, V_DIM), F32)],
        compiler_params=_params(("arbitrary", "arbitrary")),
    )(pj0, cos_full, sin_signed, decay_in, zeta, xi)


def _merge_bwd_call(dx1, pj1, pj2, pj3, ret_raw, a, b, ws, bst, w_ro, w_so, w_o):
    t = dx1.shape[0]

    def body(dx1_ref, pj1_ref, pj2_ref, pj3_ref, rr_ref, a_ref, b_ref, ws_ref, bst_ref, wro_ref, wso_ref, wo_ref,
             dpj1_ref, dpj2_ref, dpj3_ref, drr_ref, da_ref, db_ref, dws_ref, dbst_ref):
        @pl.when(pl.program_id(0) == 0)
        def _():
            dws_ref[...] = jnp.zeros_like(dws_ref)
            dbst_ref[...] = jnp.zeros_like(dbst_ref)

        wsc = _causal_ws(ws_ref)
        f = _branch_forward(pj1_ref[...], pj2_ref[...], rr_ref[...], wsc, bst_ref[...])
        pj3 = pj3_ref[...]
        smr, sms = _sigmoid(pj3[:, :D_MODEL]), _sigmoid(pj3[:, D_MODEL:])
        dmerged = _mm_nt(dx1_ref[...], wo_ref[...])
        da_f, db_f = dmerged * smr, dmerged * sms
        da, db = da_f.astype(MXU_DTYPE), db_f.astype(MXU_DTYPE)
        dpj3_ref[:, :D_MODEL] = (da_f * a_ref[...] * (1.0 - smr)).astype(dpj3_ref.dtype)
        dpj3_ref[:, D_MODEL:] = (db_f * b_ref[...] * (1.0 - sms)).astype(dpj3_ref.dtype)
        da_ref[...] = da
        db_ref[...] = db

        dret = _mm_nt(da, wro_ref[...])
        dpj1_ref[:, :D_MODEL] = (dret * f["rn"] * f["dsil_rg"]).astype(dpj1_ref.dtype)
        drn = dret * f["sil_rg"]
        for h in range(RET_HEADS):
            cols = slice(h * V_DIM, (h + 1) * V_DIM)
            drr_ref[:, cols] = _unit_norm_bwd(drn[:, cols], f["rn"][:, cols], f["rstd_r"][h]).astype(drr_ref.dtype)

        dsgu = _mm_nt(db, wso_ref[...])
        dpj2_ref[:, D_MODEL:] = (dsgu * f["u"] * f["mixed"] * f["dsil_sg"]).astype(dpj2_ref.dtype)
        tg = dsgu * f["sil_sg"]
        dpj1_ref[:, D_MODEL:] = (tg * f["mixed"] * _gelu_grad(f["su"])).astype(dpj1_ref.dtype)
        dmixed = tg * f["u"]
        rows = []
        for cc in range(TILE_M // CHUNK):
            cols = []
            for g in range(SGU_GROUPS):
                rs, cs = slice(cc * CHUNK, (cc + 1) * CHUNK), slice(g * GROUP_DIM, (g + 1) * GROUP_DIM)
                dm = dmixed[rs, cs]
                cols.append(_mm_tn(wsc[g], dm))
                dws_ref[g] += _mm_nt(dm, f["vn"][rs, cs])
                dbst_ref[:, g:g + 1] += jnp.sum(dm, axis=1, keepdims=True)
            rows.append(jnp.concatenate(cols, axis=1))
        dvn = jnp.concatenate(rows, axis=0)
        dvv = _unit_norm_bwd(dvn, f["vn"], f["rstd_v"])
        dpj2_ref[:, :D_MODEL] = (dvv * _gelu_grad(f["sv"])).astype(dpj2_ref.dtype)

    sq = (D_MODEL, D_MODEL)
    return pl.pallas_call(
        body, name="merge_bwd", grid=(t // TILE_M,),
        in_specs=[_row_tile(D_MODEL)] + [_proj_tile(j) for j in (1, 2, 3)] + [_row_tile(D_MODEL)] * 3
        + [_resident(ws.shape), _resident(bst.shape), _resident(sq), _resident(sq), _resident(sq)],
        out_specs=[_row_tile(SHARD_W)] * 3 + [_row_tile(D_MODEL)] * 3
        + [pl.BlockSpec(ws.shape, lambda i: (0, 0, 0)), pl.BlockSpec(bst.shape, lambda i: (0, 0))],
        out_shape=[jax.ShapeDtypeStruct((t, SHARD_W), MXU_DTYPE)] * 3 + [jax.ShapeDtypeStruct((t, D_MODEL), MXU_DTYPE)] * 3
        + [jax.ShapeDtypeStruct(ws.shape, F32), jax.ShapeDtypeStruct(bst.shape, F32)],
        compiler_params=_params(("arbitrary",)),
    )(dx1, pj1, pj2, pj3, ret_raw, a, b, ws, bst, w_ro, w_so, w_o)


def _tail_call(pj, ret_raw, x2d, p2d, target, ws, bst, g_ple, g_final, w_ro, w_so, w_o, w_pg, w_pp):
    t = x2d.shape[0]
    tm = TILE_M

    def body(pj1_ref, pj2_ref, pj3_ref, rr_ref, x_ref, p_ref, t_ref, ws_ref, bst_ref, gp_ref, gf_ref,
             wro_ref, wso_ref, wo_ref, wpg_ref, wpp_ref,
             dx1_ref, a_ref, b_ref, ret_ref, sgu_ref, mg_ref, hp_ref, dz_ref, dpp_ref, loss_ref, dgp_ref, dgf_ref):
        @pl.when(pl.program_id(0) == 0)
        def _():
            for acc in (loss_ref, dgp_ref, dgf_ref):
                acc[...] = jnp.zeros_like(acc)

        f = _branch_forward(pj1_ref[...], pj2_ref[...], rr_ref[...], _causal_ws(ws_ref), bst_ref[...])
        ret = f["ret"].astype(MXU_DTYPE)
        sgu = f["sgu"].astype(MXU_DTYPE)
        ret_ref[...] = ret
        sgu_ref[...] = sgu
        a = jnp.dot(ret, wro_ref[...], preferred_element_type=F32)
        b = jnp.dot(sgu, wso_ref[...], preferred_element_type=F32)
        a_ref[...] = a
        b_ref[...] = b
        pj3 = pj3_ref[...]
        smr, sms = _sigmoid(pj3[:, :D_MODEL]), _sigmoid(pj3[:, D_MODEL:])
        merged = (smr * a + sms * b).astype(MXU_DTYPE)
        mg_ref[...] = merged
        x1v = x_ref[...] + jnp.dot(merged, wo_ref[...], preferred_element_type=F32)
        xn1, r1 = _rms(x1v)
        hp = (xn1 * gp_ref[...]).astype(MXU_DTYPE)
        hp_ref[...] = hp
        gate = _sigmoid(jnp.dot(hp, wpg_ref[...], preferred_element_type=F32))
        pp = jnp.dot(p_ref[...].astype(MXU_DTYPE), wpp_ref[...], preferred_element_type=F32)
        xn2, r2 = _rms(x1v + gate * pp)
        err = xn2 * gf_ref[...] - t_ref[...]
        loss_ref[...] += (0.5 / D_MODEL) * jnp.sum(jnp.sum(err * err, axis=1, keepdims=True), axis=0, keepdims=True)

        dy = err * (1.0 / D_MODEL)
        dgf_ref[...] += jnp.sum(dy * xn2, axis=0, keepdims=True)
        dx2 = _rms_bwd(dy * gf_ref[...], xn2, r2)
        dpp_ref[...] = (dx2 * gate).astype(MXU_DTYPE)
        dz = (dx2 * pp * gate * (1.0 - gate)).astype(MXU_DTYPE)
        dz_ref[...] = dz
        dhp = _mm_nt(dz, wpg_ref[...])
        dgp_ref[...] += jnp.sum(dhp * xn1, axis=0, keepdims=True)
        dx1_ref[...] = dx2 + _rms_bwd(dhp * gp_ref[...], xn1, r1)

    sq = (D_MODEL, D_MODEL)
    vec = _resident((1, D_MODEL))
    whole = lambda shape: pl.BlockSpec(shape, lambda i: (0,) * len(shape))
    return pl.pallas_call(
        body, name="tail_fwd", grid=(t // tm,),
        in_specs=[_proj_tile(j) for j in (1, 2, 3)]
        + [_row_tile(D_MODEL), _row_tile(D_MODEL), _row_tile(PLE_DIM), _row_tile(D_MODEL), _resident(ws.shape),
           _resident(bst.shape), vec, vec, _resident(sq), _resident(sq), _resident(sq), _resident(sq),
           _resident((PLE_DIM, D_MODEL))],
        out_specs=[_row_tile(D_MODEL)] * 9 + [whole(LOSS_TILE), whole((1, D_MODEL)), whole((1, D_MODEL))],
        out_shape=[jax.ShapeDtypeStruct((t, D_MODEL), F32)] * 3 + [jax.ShapeDtypeStruct((t, D_MODEL), MXU_DTYPE)] * 6
        + [jax.ShapeDtypeStruct(LOSS_TILE, F32), jax.ShapeDtypeStruct((1, D_MODEL), F32),
           jax.ShapeDtypeStruct((1, D_MODEL), F32)],
        compiler_params=_params(("arbitrary",)),
    )(pj, pj, pj, ret_raw, x2d, p2d, target, ws, bst, g_ple, g_final, w_ro, w_so, w_o, w_pg, w_pp)


def _retention_bwd_call(pj0, drr, states, consts, n_seq, seq, side):
    cos_full, sin_signed, decay_in, zeta, xi = consts
    nb = seq // RET_TILE
    cpb = RET_TILE // CHUNK

    def body(pj_ref, do_ref, st_ref, cos_ref, sin_ref, d_ref, zeta_ref, xi_ref, dpj_ref, gstate):
        @pl.when(pl.program_id(1) == 0)
        def _():
            gstate[...] = jnp.zeros_like(gstate)

        for cc in reversed(range(cpb)):
            rows = slice(cc * CHUNK, (cc + 1) * CHUNK)
            cs, sn = cos_ref[rows, :], sin_ref[rows, :]
            for h in range(RET_HEADS):
                q, k, v = _qkv(pj_ref, rows, h)
                qt = (q * cs + _rot(q) * sn) * QK_SCALE
                kt = k * cs + _rot(k) * sn
                d_out = do_ref[rows, h * V_DIM:(h + 1) * V_DIM]
                prev = st_ref[cc, h]
                g = gstate[h]
                dec = d_ref[h]
                scores_d = _mm_nt(qt, kt) * dec
                dscores = _mm_nt(d_out, v) * dec
                kz = kt * zeta_ref[h]
                qx = qt * xi_ref[h]
                dv = _mm_tn(scores_d, d_out) + _mm(kz, g)
                dqt = (_mm(dscores, kt) + _mm_nt(d_out, prev) * xi_ref[h]) * QK_SCALE
                dkt = _mm_tn(dscores, qt) + _mm_nt(v, g) * zeta_ref[h]
                gstate[h] = _mm_tn(qx, d_out) + CHUNK_DECAY[h] * g
                dq = dqt * cs + _rot(dqt * sn)
                dk = dkt * cs + _rot(dkt * sn)
                dpj_ref[rows, h * QK_DIM:(h + 1) * QK_DIM] = dq.astype(dpj_ref.dtype)
                dpj_ref[rows, RET_HEADS * QK_DIM + h * QK_DIM:RET_HEADS * QK_DIM + (h + 1) * QK_DIM] = dk.astype(
                    dpj_ref.dtype)
                dpj_ref[rows, 2 * RET_HEADS * QK_DIM + h * V_DIM:2 * RET_HEADS * QK_DIM + (h + 1) * V_DIM] = dv.astype(
                    dpj_ref.dtype)

    row = lambda b, n: (b * nb + nb - 1 - n, 0)
    pos = lambda b, n: (nb - 1 - n, 0)
    return _compute_call(
        body, name="retention_bwd", grid=(n_seq, nb),
        in_specs=[pl.BlockSpec((None, RET_TILE, SHARD_W), lambda b, n: (0,) + row(b, n)),
                  pl.BlockSpec((RET_TILE, D_MODEL), row),
                  pl.BlockSpec((cpb, RET_HEADS, QK_DIM, V_DIM), lambda b, n: (b * nb + nb - 1 - n, 0, 0, 0)),
                  pl.BlockSpec((RET_TILE, QK_DIM), pos), pl.BlockSpec((RET_TILE, QK_DIM), pos),
                  _resident(decay_in.shape), _resident(zeta.shape), _resident(xi.shape)],
        out_specs=[pl.BlockSpec((RET_TILE, SHARD_W), row)],
        out_shape=[jax.ShapeDtypeStruct((n_seq * seq, SHARD_W), MXU_DTYPE)],
        scratch_shapes=[pltpu.VMEM((RET_HEADS, QK_DIM, V_DIM), F32)],
        operands=(pj0, drr, states, cos_full, sin_signed, decay_in, zeta, xi),
        semantics=("arbitrary", "arbitrary"), side=side)


def _dx_call(dpj, x2d, dx1, g_mixer, wg_in, side):
    t = x2d.shape[0]

    def body(d0, d1, d2, d3, x_ref, dx1_ref, g_ref, w_ref, dx_ref, dg_ref):
        @pl.when(pl.program_id(0) == 0)
        def _():
            dg_ref[...] = jnp.zeros_like(dg_ref)

        dh = _mm_nt(d0[...], w_ref[0])
        for j, d_ref in enumerate((d1, d2, d3)):
            dh += _mm_nt(d_ref[...], w_ref[j + 1])
        xn, r = _rms(x_ref[...])
        dg_ref[...] += jnp.sum(dh * xn, axis=0, keepdims=True)
        dx_ref[...] = dx1_ref[...] + _rms_bwd(dh * g_ref[...], xn, r)

    return _compute_call(
        body, name="dx_bwd", grid=(t // DX_TILE,),
        in_specs=[_row_tile(SHARD_W, DX_TILE)] * N_CHIPS + [_row_tile(D_MODEL, DX_TILE)] * 2
        + [_resident((1, D_MODEL)), _resident(wg_in.shape)],
        out_specs=[_row_tile(D_MODEL, DX_TILE), pl.BlockSpec((1, D_MODEL), lambda i: (0, 0))],
        out_shape=[jax.ShapeDtypeStruct((t, D_MODEL), F32), jax.ShapeDtypeStruct((1, D_MODEL), F32)],
        operands=(*dpj, x2d, dx1, g_mixer, wg_in), semantics=("arbitrary",), side=side)


def _wgrad_call(name, lhs, rhs, block_n, out_cols=None, block_t=1024, into=None, slot=0, n_slots=1, side=None):
    t, n = rhs.shape
    k = lhs.shape[1]
    block_t = min(block_t, t)
    steps = t // block_t
    out_cols = block_n if out_cols is None else out_cols
    per = block_n // out_cols
    first_block = slot * (n // block_n)

    def body(l_ref, r_ref, *rest):
        o_ref, acc = rest[-2:]
        @pl.when(pl.program_id(1) == 0)
        def _():
            acc[...] = jnp.zeros_like(acc)

        acc[...] += _mm_tn(l_ref[...], r_ref[...])

        @pl.when(pl.program_id(1) == steps - 1)
        def _():
            for s in range(per):
                o_ref[s] = acc[:, s * out_cols:(s + 1) * out_cols].astype(o_ref.dtype)

    res = _compute_call(
        body, name=name, grid=(n // block_n, steps),
        in_specs=[pl.BlockSpec((block_t, k), lambda j, i: (i, 0)), pl.BlockSpec((block_t, block_n), lambda j, i: (i, j))]
        + ([] if into is None else [ANY]),
        out_specs=[pl.BlockSpec((per, k, out_cols), lambda j, i: (first_block + j, 0, 0))],
        out_shape=[jax.ShapeDtypeStruct((n_slots * (n // out_cols), k, out_cols), COMM_DTYPE)],
        scratch_shapes=[pltpu.VMEM((k, block_n), F32)], aliases={} if into is None else {2: 0},
        operands=(lhs, rhs) if into is None else (lhs, rhs, into), semantics=("arbitrary", "arbitrary"), side=side)
    return res[0] if side is None else (res[0][0], res[1])


def _position():
    return lax.axis_index("x"), lax.axis_index("y"), lax.axis_index("c")


def _position_array():
    x, y, c = _position()
    return jnp.stack([2 * x + y, c]).astype(jnp.int32)


def _other_chip(x, y, k):
    return (1 - x if k & 2 else x), (1 - y if k & 1 else y)


def _plan_side(operands, out_shapes, plan, n_remote, aliases=None):
    def copies(ins, outs, send_sem, recv_sem, base=0):
        remote = plan(ins, outs)
        assert len(remote) == n_remote
        return [pltpu.make_async_remote_copy(src_ref=src, dst_ref=dst, send_sem=send_sem.at[base + i],
                                             recv_sem=recv_sem.at[base + i], device_id=dev, device_id_type=MESH)
                for i, (src, dst, dev) in enumerate(remote)]

    def start(*a):
        for cp in copies(*a):
            cp.start()

    def finish(*a):
        for cp in copies(*a):
            cp.wait()

    return _Side(operands, out_shapes, n_remote, start, finish, aliases)


def _place_cast_call(name, shards, pos):
    n = len(shards)
    rows, width = shards[0].shape
    block_rows = min(256, rows)

    def body(pos_ref, *refs):
        for w_ref, o_ref in zip(refs[:n], refs[n:]):
            o_ref[...] = w_ref[...].astype(o_ref.dtype)

    return pl.pallas_call(
        body, name=name,
        grid_spec=pltpu.PrefetchScalarGridSpec(
            num_scalar_prefetch=1, grid=(rows // block_rows,),
            in_specs=[pl.BlockSpec((block_rows, width), lambda i, pos: (i, 0))] * n,
            out_specs=[pl.BlockSpec((None, block_rows, width), lambda i, pos: (pos[0], i, 0))] * n),
        out_shape=[jax.ShapeDtypeStruct((N_CHIPS, rows, width), MXU_DTYPE)] * n,
        compiler_params=_params(("arbitrary",)),
    )(pos, *shards)


def _gather_side(placed):
    n = len(placed)

    def copies(kind, bufs, send_sem, recv_sem, base):
        x, y, c = _position()
        me = 2 * x + y
        made = []
        for i in range(n):
            hr = placed[i].shape[1] // 2
            for k in (1, 2, 3):
                px, py = _other_chip(x, y, k)
                chip, core, slot, dev = [(me, c, k - 1, (px, py, c)), (2 * px + py, c, 2 + k, (x, y, 1 - c)),
                                         (2 * px + py, 1 - c, 2 + k, (x, y, 1 - c))][kind]
                piece = bufs[i].at[chip, pl.ds(core * hr, hr)]
                made.append(pltpu.make_async_remote_copy(
                    src_ref=piece, dst_ref=piece, send_sem=send_sem.at[base + 6 * i + slot],
                    recv_sem=recv_sem.at[base + 6 * i + slot], device_id=dev, device_id_type=MESH))
        return made

    def start(ins, outs, send_sem, recv_sem, base=0):
        for cp in copies(0, outs, send_sem, recv_sem, base):
            cp.start()

    def finish(ins, outs, send_sem, recv_sem, base=0):
        first, onward = copies(0, outs, send_sem, recv_sem, base), copies(1, outs, send_sem, recv_sem, base)
        for landed, cp in zip(first, onward):
            landed.wait_recv()
            cp.start()
        for cp in copies(2, outs, send_sem, recv_sem, base):
            cp.wait_recv()
        for cp in first + onward:
            cp.wait_send()

    return _Side(placed, [jax.ShapeDtypeStruct(a.shape, a.dtype) for a in placed], 6 * n, start, finish,
                 {i: i for i in range(n)})


def _same_shape(shapes):
    found = {}
    for i, shape in enumerate(shapes):
        found.setdefault(shape, []).append(i)
    return found


def _sum_call(name, pos, groups, rows, width, out_dtype, block_rows, out_rows=None, out_index=None):
    out_rows = rows if out_rows is None else out_rows
    out_index = (lambda i, pos, br: i) if out_index is None else out_index
    sizes = [len(g) for g in groups]

    def body(pos_ref, *refs):
        ins, outs = refs[:sum(sizes)], refs[sum(sizes):]
        at = 0
        for size, out in zip(sizes, outs):
            acc = ins[at][...].astype(F32)
            for r in ins[at + 1:at + size]:
                acc = acc + r[...].astype(F32)
            out[...] = acc.astype(out_dtype)
            at += size

    def spec(index):
        return pl.BlockSpec((block_rows, width), lambda i, pos: (index(i, pos, block_rows), 0))

    return pl.pallas_call(
        body, name=name,
        grid_spec=pltpu.PrefetchScalarGridSpec(
            num_scalar_prefetch=1, grid=(rows // block_rows,),
            in_specs=[spec(index) for g in groups for _, index in g], out_specs=[spec(out_index)] * len(groups)),
        out_shape=[jax.ShapeDtypeStruct((out_rows, width), out_dtype)] * len(groups),
        compiler_params=_params(("arbitrary",)),
    )(pos, *[arr for g in groups for arr, _ in g])


def _pair_stage(tag, grads, pos):
    n = len(grads)
    shapes = [(g.shape[0] // N_CHIPS, g.shape[1]) for g in grads]

    def plan(ins, outs):
        x, y, c = _position()
        remote = []
        for i, (r, _) in enumerate(shapes):
            hr = r // 2
            for j in range(N_CHIPS):
                remote.append((ins[i].at[pl.ds(j * r + (1 - c) * hr, hr)], outs[i].at[pl.ds(j * hr, hr)],
                               (x, y, 1 - c)))
        return remote

    pair = _exchange_call(f"grad_pair_exchange_{tag}", _plan_side(
        grads, [jax.ShapeDtypeStruct((N_CHIPS * (r // 2), w), COMM_DTYPE) for r, w in shapes], plan, N_CHIPS * n))

    def own_half(r):
        def index(s, pos, br):
            per = (r // 2) // br
            return (s // per) * (r // br) + pos[1] * per + s % per
        return index

    sums = [None] * n
    for (r, w), which in _same_shape(shapes).items():
        groups = [[(grads[i], own_half(r)), (pair[i], lambda s, pos, br: s)] for i in which]
        outs = _sum_call(f"grad_pair_sum_{tag}{which[0]}", pos, groups, N_CHIPS * (r // 2), w, COMM_DTYPE,
                         min(256, r // 2))
        for i, out in zip(which, outs):
            sums[i] = out
    return sums


def _chip_side(pair_sums):
    halves = [(p.shape[0] // N_CHIPS, p.shape[1]) for p in pair_sums]

    def plan(ins, outs):
        x, y, c = _position()
        remote = []
        for i, (hr, _) in enumerate(halves):
            for k in (1, 2, 3):
                px, py = _other_chip(x, y, k)
                remote.append((ins[i].at[pl.ds((2 * px + py) * hr, hr)], outs[i].at[pl.ds((k - 1) * hr, hr)],
                               (px, py, c)))
        return remote

    return _plan_side(pair_sums, [jax.ShapeDtypeStruct((3 * hr, w), COMM_DTYPE) for hr, w in halves], plan,
                      3 * len(pair_sums))


def _finish_stage(tag, pair_sums, chip, pos, extra=None):
    n = len(pair_sums)
    halves = [(p.shape[0] // N_CHIPS, p.shape[1]) for p in pair_sums]

    def slab(k, hr):
        return lambda s, pos, br: k * (hr // br) + s

    reduced = [None] * n
    for (hr, w), which in _same_shape(halves).items():
        groups = [[(pair_sums[i], lambda s, pos, br: pos[0] * (hr // br) + s)] + [(chip[i], slab(k, hr)) for k in range(3)]
                  for i in which]
        outs = _sum_call(f"grad_chip_sum_{tag}{which[0]}", pos, groups, hr, w, F32, min(256, hr), out_rows=2 * hr,
                         out_index=lambda s, pos, br: pos[1] * (hr // br) + s)
        for i, out in zip(which, outs):
            reduced[i] = out

    def plan(ins, outs):
        x, y, c = _position()
        remote = []
        for i, (hr, _) in enumerate(halves):
            mine = outs[i].at[pl.ds(c * hr, hr)]
            remote.append((mine, mine, (x, y, 1 - c)))
        return remote

    swap = _plan_side(reduced, [jax.ShapeDtypeStruct((2 * hr, w), F32) for hr, w in halves], plan, n,
                      {i: i for i in range(n)})
    if extra is None:
        return _exchange_call(f"grad_half_exchange_{tag}", swap)
    res = _exchange_call(f"grad_half_exchange_{tag}", _join_sides(swap, extra))
    return res[:n], res[n:]


def _small_gather_side(parts):
    n = len(parts)

    def copies(ins, outs, send_sem, recv_sem, base):
        x, y, c = _position()
        made = []
        for i in range(n):
            mine = outs[i].at[4 * x + 2 * y + c]
            made.append(pltpu.make_async_copy(ins[i], mine, send_sem.at[base + 8 * i + 7]))
            for d in range(1, 8):
                px, py = _other_chip(x, y, d >> 1)
                made.append(pltpu.make_async_remote_copy(
                    src_ref=ins[i], dst_ref=mine, send_sem=send_sem.at[base + 8 * i + d - 1],
                    recv_sem=recv_sem.at[base + 8 * i + d - 1], device_id=(px, py, 1 - c if d & 1 else c),
                    device_id_type=MESH))
        return made

    def start(ins, outs, send_sem, recv_sem, base=0):
        for cp in copies(ins, outs, send_sem, recv_sem, base):
            cp.start()

    def finish(ins, outs, send_sem, recv_sem, base=0):
        for cp in copies(ins, outs, send_sem, recv_sem, base):
            cp.wait()

    return _Side(parts, [jax.ShapeDtypeStruct((8,) + a_.shape, F32) for a_ in parts], 8 * n, start, finish)


def _adamw(w, g, m, v):
    m = ADAM_B1 * m + (1.0 - ADAM_B1) * g
    v = ADAM_B2 * v + (1.0 - ADAM_B2) * (g * g)
    m_hat = m / (1.0 - ADAM_B1 ** ADAM_STEP)
    v_hat = v / (1.0 - ADAM_B2 ** ADAM_STEP)
    delta = -ADAM_LR * (m_hat / (jnp.sqrt(v_hat) + ADAM_EPS) + ADAM_WD * w)
    return delta, m, v


def _adamw_call(name, ws, gs, ms, vs):
    n = len(ws)
    rows, width = ws[0].shape
    block_rows = min(256 // n, rows)

    def body(*refs):
        for i in range(n):
            w_ref, g_ref, m_ref, v_ref = (refs[j * n + i] for j in range(4))
            d_out, m_out, v_out = (refs[(4 + j) * n + i] for j in range(3))
            d_out[...], m_out[...], v_out[...] = _adamw(w_ref[...], g_ref[...], m_ref[...], v_ref[...])

    spec = pl.BlockSpec((block_rows, width), lambda i: (i, 0))
    res = pl.pallas_call(
        body, name=name, grid=(rows // block_rows,), in_specs=[spec] * (4 * n), out_specs=[spec] * (3 * n),
        out_shape=[jax.ShapeDtypeStruct(ws[0].shape, F32)] * (3 * n),
        compiler_params=_params(("arbitrary",)),
    )(*ws, *gs, *ms, *vs)
    return [(res[i], res[n + i], res[2 * n + i]) for i in range(n)]


def _small_adamw_call(gathered, weights, moments_m, moments_v):
    n = len(weights)

    def body(*refs):
        all_refs, refs = refs[:n + 1], refs[n + 1:]
        w_refs, m_refs, v_refs, outs = refs[:n], refs[n:2 * n], refs[2 * n:3 * n], refs[3 * n:]

        def total(ref):
            acc = ref[0]
            for d in range(1, 8):
                acc = acc + ref[d]
            return acc

        outs[0][...] = total(all_refs[n])
        for i in range(n):
            g = total(all_refs[i])
            if i == 0:
                row = lax.broadcasted_iota(jnp.int32, g.shape, 0)
                col = lax.broadcasted_iota(jnp.int32, g.shape, 1)
                g = jnp.where((row % CHUNK) >= col, g, 0.0)
            g_out, d_out, m_out, v_out = outs[1 + 4 * i:5 + 4 * i]
            g_out[...] = g
            d_out[...], m_out[...], v_out[...] = _adamw(w_refs[i][...], g, m_refs[i][...], v_refs[i][...])

    out_shape = [jax.ShapeDtypeStruct(LOSS_TILE, F32)]
    for w in weights:
        out_shape += [jax.ShapeDtypeStruct(w.shape, F32)] * 4
    res = pl.pallas_call(
        body, name="small_adamw", out_shape=out_shape,
        compiler_params=pltpu.CompilerParams(vmem_limit_bytes=VMEM_LIMIT),
    )(*gathered, *weights, *moments_m, *moments_v)
    return res[0], [res[1 + 4 * i:5 + 4 * i] for i in range(n)]


def kernel(x, p, w_in, w_ret_out, w_sgu_out, w_out, sgu_ws, sgu_bs, w_ple_gate, w_ple_proj, g_mixer, g_ple, g_final, loss_target, m_w_in, m_w_ret_out, m_w_sgu_out, m_w_out, m_sgu_ws, m_sgu_bs, m_w_ple_gate, m_w_ple_proj, m_g_mixer, m_g_ple, m_g_final, v_w_in, v_w_ret_out, v_w_sgu_out, v_w_out, v_sgu_ws, v_sgu_bs, v_w_ple_gate, v_w_ple_proj, v_g_mixer, v_g_ple, v_g_final):
    n_seq, seq, _ = x.shape
    t = n_seq * seq
    x2d = x.reshape(t, D_MODEL)
    p2d = p.reshape(t, PLE_DIM)
    target = loss_target.reshape(t, D_MODEL)
    big = [w_in[0], w_ret_out[0], w_sgu_out[0], w_out[0], w_ple_gate[0], w_ple_proj[0]]
    big_m = [m_w_in[0], m_w_ret_out[0], m_w_sgu_out[0], m_w_out[0], m_w_ple_gate[0], m_w_ple_proj[0]]
    big_v = [v_w_in[0], v_w_ret_out[0], v_w_sgu_out[0], v_w_out[0], v_w_ple_gate[0], v_w_ple_proj[0]]

    pos = _position_array()
    placed = (_place_cast_call("place_w_in", big[:1], pos) + _place_cast_call("place_square_weights", big[1:5], pos)
              + _place_cast_call("place_w_ple_proj", big[5:], pos))
    ws = sgu_ws[0]
    bst = sgu_bs[0].T
    consts = _retention_consts(seq)

    (h, pj, wg_in), gathered = _proj_call(x2d, g_mixer, placed[0], pos, _gather_side(placed[1:]))
    pj0 = pj1 = pj2 = pj3 = pj
    w_ro, w_so, w_o, w_pg = (w.reshape(D_MODEL, D_MODEL) for w in gathered[:4])
    w_pp = gathered[4].transpose(1, 0, 2).reshape(PLE_DIM, D_MODEL)
    ret_raw, states = _retention_fwd_call(pj0, consts, n_seq, seq)
    dx1, a, b, ret, sgu, merged, hp, dz, dpp, loss, dg_ple, dg_final = _tail_call(
        pj, ret_raw, x2d, p2d, target, ws, bst, g_ple, g_final.reshape(1, D_MODEL), w_ro, w_so, w_o, w_pg, w_pp)
    dpj1, dpj2, dpj3, drr, da, db, dws, dbst = _merge_bwd_call(dx1, pj1, pj2, pj3, ret_raw, a, b, ws, bst,
                                                               w_ro, w_so, w_o)

    small_shapes = [(SGU_GROUPS * CHUNK, CHUNK), (SGU_GROUPS, CHUNK), (1, D_MODEL), (1, D_MODEL), (1, D_MODEL)]
    as_small = lambda arrays: [None if a_ is None else a_.reshape(s_) for a_, s_ in zip(arrays, small_shapes)]
    early = as_small([dws, dbst.T, None, dg_ple, dg_final])
    g_ret_out, early_all = _wgrad_call("wgrad_ret_out", ret, da, D_MODEL,
                                       side=_small_gather_side([early[0], early[1], early[3], early[4], loss]))
    rows_of = lambda g: g.reshape(g.shape[0] * g.shape[1], g.shape[2])
    tail_grads = [
        rows_of(g_ret_out),
        rows_of(_wgrad_call("wgrad_sgu_out", sgu, db, D_MODEL)),
        rows_of(_wgrad_call("wgrad_out", merged, dx1, D_MODEL)),
        rows_of(_wgrad_call("wgrad_ple_gate", hp, dz, D_MODEL)),
        rows_of(_wgrad_call("wgrad_ple_proj", p2d, dpp, D_MODEL, out_cols=PLE_DIM)),
    ]
    tail_sums = _pair_stage("tail", tail_grads, pos)
    (dpj0,), tail_chip = _retention_bwd_call(pj0, drr, states, consts, n_seq, seq, _chip_side(tail_sums))
    in_grad = None
    for j, d in enumerate((dpj0, dpj1, dpj2, dpj3)):
        in_grad = _wgrad_call(f"wgrad_in_{j}", h, d, SHARD_W, into=in_grad, slot=j, n_slots=N_CHIPS)
    in_sums = _pair_stage("in", [rows_of(in_grad)], pos)
    (dx, dg_mixer), in_chip = _dx_call((dpj0, dpj1, dpj2, dpj3), x2d, dx1, g_mixer, wg_in, _chip_side(in_sums))
    g_big, (mixer_all,) = _finish_stage("all", in_sums + tail_sums, list(in_chip) + list(tail_chip), pos,
                                        _small_gather_side([dg_mixer]))
    upd = [None] * len(big)
    for _, which in _same_shape([w.shape for w in big]).items():
        pick = lambda arrays: [arrays[i] for i in which]
        for i, triple in zip(which, _adamw_call(f"adamw_{which[0]}", pick(big), pick(g_big), pick(big_m), pick(big_v))):
            upd[i] = triple

    small_g = [early_all[0], early_all[1], mixer_all, early_all[2], early_all[3], early_all[4]]
    total, small = _small_adamw_call(small_g, as_small([sgu_ws, sgu_bs, g_mixer, g_ple, g_final]),
                                     as_small([m_sgu_ws, m_sgu_bs, m_g_mixer, m_g_ple, m_g_final]),
                                     as_small([v_sgu_ws, v_sgu_bs, v_g_mixer, v_g_ple, v_g_final]))
    out_small_shapes = [sgu_ws.shape, sgu_bs.shape, g_mixer.shape, g_ple.shape, g_final.shape]

    def ordered(big_list, kind):
        w_in_, w_ro_, w_so_, w_o_, w_pg_, w_pp_ = [b_[None] for b_ in big_list]
        s_ws, s_bs, s_gm, s_gp, s_gf = [small[i][kind].reshape(s) for i, s in enumerate(out_small_shapes)]
        return [w_in_, w_ro_, w_so_, w_o_, s_ws, s_bs, w_pg_, w_pp_, s_gm, s_gp, s_gf]

    out = [total[0, 0], dx.reshape(x.shape)]
    out += ordered(g_big, 0)
    out += ordered([u[0] for u in upd], 1)
    out += ordered([u[1] for u in upd], 2)
    out += ordered([u[2] for u in upd], 3)
    return tuple(out)
```

```python
import functools
import math

import numpy as np
import jax
import jax.numpy as jnp
from jax import lax
from jax.experimental import pallas as pl
from jax.experimental.pallas import tpu as pltpu

F32 = jnp.float32
MXU_DTYPE = jnp.bfloat16
COMM_DTYPE = jnp.bfloat16

D_MODEL = 1024
RET_HEADS = 4
QK_DIM = 128
V_DIM = 256
CHUNK = 128
SGU_GROUPS = 4
GROUP_DIM = 256
PLE_DIM = 256
N_CHIPS = 4
SHARD_W = 2048
ROPE_BASE = 10000.0
NORM_EPS = 1e-6
GN_EPS = 1e-5
QK_SCALE = QK_DIM ** -0.5
SQRT_HALF = math.sqrt(0.5)
INV_SQRT_2PI = 1.0 / math.sqrt(2.0 * math.pi)

ADAM_LR = 0.001
ADAM_B1 = 0.9
ADAM_B2 = 0.999
ADAM_EPS = 1e-08
ADAM_WD = 0.01
ADAM_STEP = 10

TILE_M = 256
LOSS_TILE = (8, 128)
LOAD_PARTS = 16
PROJ_TILE = 512
RET_TILE = 512
DX_TILE = 512
VMEM_LIMIT = 56 * 1024 * 1024
MESH = pl.DeviceIdType.MESH
ANY = pl.BlockSpec(memory_space=pl.ANY)

CHUNK_DECAY = tuple(
    float(np.exp(np.float32(CHUNK) * np.log(np.float32(1.0 - 2.0 ** (-5.0 - h))))) for h in range(RET_HEADS))


def _mm(a, b):
    return jnp.dot(a.astype(MXU_DTYPE), b.astype(MXU_DTYPE), preferred_element_type=F32)


def _mm_nt(a, b):
    return lax.dot_general(a.astype(MXU_DTYPE), b.astype(MXU_DTYPE), (((1,), (1,)), ((), ())),
                           preferred_element_type=F32)


def _mm_tn(a, b):
    return lax.dot_general(a.astype(MXU_DTYPE), b.astype(MXU_DTYPE), (((0,), (0,)), ((), ())),
                           preferred_element_type=F32)


def _mean(x):
    return jnp.mean(x, axis=-1, keepdims=True)


def _sigmoid(x):
    return jax.nn.sigmoid(x)


def _silu_and_grad(x):
    s = _sigmoid(x)
    return x * s, s * (1.0 + x * (1.0 - s))


def _gelu(x):
    return 0.5 * x * (1.0 + lax.erf(x * SQRT_HALF))


def _gelu_grad(x):
    return 0.5 * (1.0 + lax.erf(x * SQRT_HALF)) + x * jnp.exp(-0.5 * x * x) * INV_SQRT_2PI


def _unit_norm(x, eps):
    xc = x - _mean(x)
    rstd = lax.rsqrt(_mean(xc * xc) + eps)
    return xc * rstd, rstd


def _unit_norm_bwd(dn, n, rstd):
    return rstd * (dn - _mean(dn) - n * _mean(dn * n))


def _rms(x):
    r = lax.rsqrt(_mean(x * x) + NORM_EPS)
    return x * r, r


def _rms_bwd(dxn, xn, r):
    return r * (dxn - xn * _mean(dxn * xn))


def _rot(x):
    return pltpu.roll(x, QK_DIM // 2, 1)


def _params(semantics, **kw):
    return pltpu.CompilerParams(dimension_semantics=semantics, vmem_limit_bytes=VMEM_LIMIT, **kw)


def _row_tile(width, tm=TILE_M):
    return pl.BlockSpec((tm, width), lambda i: (i, 0))


def _proj_tile(j):
    return pl.BlockSpec((None, TILE_M, SHARD_W), lambda i: (j, i, 0))


def _resident(shape):
    nd = len(shape)
    return pl.BlockSpec(shape, lambda *_: (0,) * nd, pipeline_mode=pl.Buffered(1))


def _causal_ws(ws_ref):
    row = lax.broadcasted_iota(jnp.int32, (CHUNK, CHUNK), 0)
    col = lax.broadcasted_iota(jnp.int32, (CHUNK, CHUNK), 1)
    return [jnp.where(row >= col, ws_ref[g], 0.0).astype(MXU_DTYPE) for g in range(SGU_GROUPS)]


def _heads(x, width):
    return [x[:, h * width:(h + 1) * width] for h in range(x.shape[1] // width)]


def _branch_forward(pj1, pj2, ret_raw, wsc, bst):
    rg, su = pj1[:, :D_MODEL], pj1[:, D_MODEL:]
    sv, sg = pj2[:, :D_MODEL], pj2[:, D_MODEL:]
    rn_parts, rstd_parts = zip(*[_unit_norm(r, GN_EPS) for r in _heads(ret_raw, V_DIM)])
    rn = jnp.concatenate(rn_parts, axis=1)
    sil_rg, dsil_rg = _silu_and_grad(rg)
    ret = rn * sil_rg
    u = _gelu(su)
    vn, rstd_v = _unit_norm(_gelu(sv), GN_EPS)
    rows = []
    for cc in range(pj1.shape[0] // CHUNK):
        cols = []
        for g in range(SGU_GROUPS):
            blk = vn[cc * CHUNK:(cc + 1) * CHUNK, g * GROUP_DIM:(g + 1) * GROUP_DIM]
            cols.append(_mm(wsc[g], blk) + bst[:, g:g + 1])
        rows.append(jnp.concatenate(cols, axis=1))
    mixed = jnp.concatenate(rows, axis=0)
    sil_sg, dsil_sg = _silu_and_grad(sg)
    sgu = u * mixed * sil_sg
    return dict(rg=rg, su=su, sv=sv, sg=sg, rn=rn, rstd_r=rstd_parts, sil_rg=sil_rg, dsil_rg=dsil_rg, ret=ret,
                u=u, vn=vn, rstd_v=rstd_v, mixed=mixed, sil_sg=sil_sg, dsil_sg=dsil_sg, sgu=sgu)


class _Side:
    def __init__(self, operands, out_shapes, n_sems, start, finish, aliases=None):
        self.operands, self.out_shapes, self.n_sems = list(operands), list(out_shapes), n_sems
        self.start, self.finish, self.aliases = start, finish, dict(aliases or {})


def _join_sides(*sides):
    spans, a, b, s = [], 0, 0, 0
    for side in sides:
        spans.append((a, b, s))
        a, b, s = a + len(side.operands), b + len(side.out_shapes), s + side.n_sems

    def run(which):
        def go(ins, outs, send_sem, recv_sem, base=0):
            for side, (a0, b0, s0) in zip(sides, spans):
                getattr(side, which)(ins[a0:a0 + len(side.operands)], outs[b0:b0 + len(side.out_shapes)],
                                     send_sem, recv_sem, base + s0)
        return go

    aliases = {a0 + i: b0 + o for side, (a0, b0, _) in zip(sides, spans) for i, o in side.aliases.items()}
    return _Side([x for side in sides for x in side.operands], [x for side in sides for x in side.out_shapes], s,
                 run("start"), run("finish"), aliases)


def _compute_call(body, *, name, grid, in_specs, out_specs, out_shape, operands, semantics, scratch_shapes=(),
                  side=None, prefetch=None, aliases=None, side_start=None):
    n_pre = 0 if prefetch is None else 1
    pre = () if prefetch is None else (prefetch,)

    def spec(in_specs, out_specs, scratch):
        return pltpu.PrefetchScalarGridSpec(num_scalar_prefetch=n_pre, grid=grid, in_specs=in_specs,
                                            out_specs=out_specs, scratch_shapes=scratch)

    if side is None:
        return pl.pallas_call(body, name=name, grid_spec=spec(in_specs, out_specs, list(scratch_shapes)),
                              out_shape=out_shape,
                              input_output_aliases={n_pre + a: b for a, b in (aliases or {}).items()},
                              compiler_params=_params(semantics))(*pre, *operands)
    n_in, n_out, s_in, s_out = len(operands), len(out_shape), len(side.operands), len(side.out_shapes)

    def carrier(*refs):
        pre_refs, refs = refs[:n_pre], refs[n_pre:]
        ins, refs = refs[:n_in], refs[n_in:]
        side_ins, refs = refs[:s_in], refs[s_in:]
        outs, refs = refs[:n_out], refs[n_out:]
        side_outs, refs = refs[:s_out], refs[s_out:]
        scratch, (send_sem, recv_sem) = refs[:-2], refs[-2:]
        ids = [pl.program_id(a) for a in range(len(grid))]
        at = (0,) * len(grid) if side_start is None else side_start
        first = functools.reduce(jnp.logical_and, [i == a for i, a in zip(ids, at)])
        last = functools.reduce(jnp.logical_and, [i == g - 1 for i, g in zip(ids, grid)])

        @pl.when(first)
        def _():
            side.start(side_ins, side_outs, send_sem, recv_sem)

        body(*pre_refs, *ins, *outs, *scratch)

        @pl.when(last)
        def _():
            side.finish(side_ins, side_outs, send_sem, recv_sem)

    all_aliases = {n_pre + a: b for a, b in (aliases or {}).items()}
    all_aliases.update({n_pre + n_in + a: n_out + b for a, b in side.aliases.items()})
    res = pl.pallas_call(
        carrier, name=name,
        grid_spec=spec(list(in_specs) + [ANY] * s_in, list(out_specs) + [ANY] * s_out,
                       list(scratch_shapes) + [pltpu.SemaphoreType.DMA((side.n_sems,))] * 2),
        out_shape=list(out_shape) + side.out_shapes, input_output_aliases=all_aliases,
        compiler_params=_params(semantics, has_side_effects=True),
    )(*pre, *operands, *side.operands)
    return res[:n_out], res[n_out:]


def _exchange_call(name, side):
    s_in = len(side.operands)

    def body(*refs):
        ins, outs = refs[:s_in], refs[s_in:s_in + len(side.out_shapes)]
        send_sem, recv_sem = refs[s_in + len(side.out_shapes):]
        side.start(ins, outs, send_sem, recv_sem)
        side.finish(ins, outs, send_sem, recv_sem)

    return pl.pallas_call(
        body, name=name, in_specs=[ANY] * s_in, out_specs=[ANY] * len(side.out_shapes), out_shape=side.out_shapes,
        scratch_shapes=[pltpu.SemaphoreType.DMA((side.n_sems,))] * 2, input_output_aliases=side.aliases,
        compiler_params=pltpu.CompilerParams(has_side_effects=True),
    )(*side.operands)


def _proj_call(x2d, g_mixer, placed_in, pos, side):
    t = x2d.shape[0]
    nt = t // PROJ_TILE
    hr = placed_in.shape[1] // 2

    def body(pos_ref, x_ref, g_ref, win_ref, h_ref, pj_ref, w_ref, w_vmem, h_all, loc_sem, send_sem, recv_sem):
        k, i = pl.program_id(0), pl.program_id(1)
        x, y, c = _position()
        me = 2 * x + y

        def copy(slot, chip, core, dev):
            piece = w_ref.at[chip, pl.ds(core * hr, hr)]
            return pltpu.make_async_remote_copy(src_ref=piece, dst_ref=piece, send_sem=send_sem.at[slot],
                                                recv_sem=recv_sem.at[slot], device_id=dev, device_id_type=MESH)

        def first_hop(kk):
            px, py = _other_chip(x, y, kk)
            return copy(kk - 1, me, c, (px, py, c))

        def relay():
            source = jnp.bitwise_xor(me, 2 - c)
            return copy(2, source, c, (jnp.bitwise_xor(x, c), jnp.bitwise_xor(y, 1 - c), c))

        def onward(kk, core):
            px, py = _other_chip(x, y, kk)
            return copy(2 + kk, 2 * px + py, core, (x, y, 1 - c))

        def loads(kk):
            px, py = _other_chip(x, y, kk)
            rows = w_vmem.shape[1] // LOAD_PARTS
            return [pltpu.make_async_copy(w_ref.at[2 * px + py, pl.ds(r * rows, rows)],
                                          w_vmem.at[kk % 2, pl.ds(r * rows, rows)], loc_sem.at[r])
                    for r in range(LOAD_PARTS)]

        @pl.when(jnp.logical_and(k == 0, i == 0))
        def _():
            for kk in (1, 2):
                first_hop(kk).start()
            for cp in loads(0):
                cp.start()
            for cp in loads(0):
                cp.wait()

        for kk in (1, 2, 3):
            @pl.when(jnp.logical_and(k == kk - 1, i == nt - 1))
            def _(kk=kk):
                if kk == 1:
                    first_hop(1).wait_recv()
                    first_hop(2).wait_recv()
                    relay().start()
                if kk == 3:
                    relay().wait_recv()
                onward(kk, c).start()
                onward(kk, 1 - c).wait_recv()
                for cp in loads(kk):
                    cp.start()

            @pl.when(jnp.logical_and(k == kk, i == 0))
            def _(kk=kk):
                for cp in loads(kk):
                    cp.wait()

        rows = pl.ds(pl.multiple_of(i * PROJ_TILE, PROJ_TILE), PROJ_TILE)

        @pl.when(k == 0)
        def _():
            xn, _ = _rms(x_ref[...])
            h = (xn * g_ref[...]).astype(MXU_DTYPE)
            h_all[rows, :] = h
            h_ref[...] = h

        pj_ref[...] = jnp.dot(h_all[rows, :], w_vmem[k % 2], preferred_element_type=F32)

        @pl.when(jnp.logical_and(k == N_CHIPS - 1, i == nt - 1))
        def _():
            for cp in [first_hop(1), first_hop(2), relay()] + [onward(kk, c) for kk in (1, 2, 3)]:
                cp.wait_send()

    parked = lambda k, i, pos: (jnp.where(k == 0, i, nt - 1), 0)
    return _compute_call(
        body, name="proj_fwd", grid=(N_CHIPS, nt),
        in_specs=[pl.BlockSpec((PROJ_TILE, D_MODEL), parked), pl.BlockSpec((1, D_MODEL), lambda k, i, pos: (0, 0)), ANY],
        out_specs=[pl.BlockSpec((PROJ_TILE, D_MODEL), lambda k, i, pos: (jnp.where(k == 0, i, nt), 0)),
                   pl.BlockSpec((None, PROJ_TILE, SHARD_W), lambda k, i, pos: (jnp.bitwise_xor(pos[0], k), i, 0)),
                   ANY],
        out_shape=[jax.ShapeDtypeStruct((t + PROJ_TILE, D_MODEL), MXU_DTYPE),
                   jax.ShapeDtypeStruct((N_CHIPS, t, SHARD_W), F32),
                   jax.ShapeDtypeStruct(placed_in.shape, placed_in.dtype)],
        scratch_shapes=[pltpu.VMEM((2,) + placed_in.shape[1:], placed_in.dtype), pltpu.VMEM((t, D_MODEL), MXU_DTYPE),
                        pltpu.SemaphoreType.DMA((LOAD_PARTS,)), pltpu.SemaphoreType.DMA((6,)),
                        pltpu.SemaphoreType.DMA((6,))],
        operands=(x2d, g_mixer, placed_in), semantics=("arbitrary", "arbitrary"), side=side, prefetch=pos,
        aliases={2: 2}, side_start=(2, 0))


def _retention_consts(seq):
    half = QK_DIM // 2
    inv = ROPE_BASE ** (-jnp.arange(half, dtype=F32) / half)
    ang = jnp.arange(seq, dtype=F32)[:, None] * inv[None, :]
    cos, sin = jnp.cos(ang), jnp.sin(ang)
    cos_full = jnp.concatenate([cos, cos], axis=1)
    sin_signed = jnp.concatenate([-sin, sin], axis=1)
    log_g = jnp.log(1.0 - 2.0 ** (-5.0 - jnp.arange(RET_HEADS, dtype=F32)))
    idx = jnp.arange(CHUNK, dtype=F32)
    diff = idx[:, None] - idx[None, :]
    decay_in = jnp.where(diff[None] >= 0, jnp.exp(jnp.maximum(diff, 0.0)[None] * log_g[:, None, None]), 0.0)
    zeta = jnp.exp((CHUNK - 1.0 - idx)[None, :] * log_g[:, None])
    xi = jnp.exp((idx + 1.0)[None, :] * log_g[:, None])
    zeta = jnp.broadcast_to(zeta[:, :, None], (RET_HEADS, CHUNK, QK_DIM))
    xi = jnp.broadcast_to(xi[:, :, None], (RET_HEADS, CHUNK, QK_DIM))
    return cos_full, sin_signed, decay_in, zeta, xi


def _qkv(pj_ref, rows, h):
    q = pj_ref[rows, h * QK_DIM:(h + 1) * QK_DIM]
    k = pj_ref[rows, RET_HEADS * QK_DIM + h * QK_DIM:RET_HEADS * QK_DIM + (h + 1) * QK_DIM]
    v = pj_ref[rows, 2 * RET_HEADS * QK_DIM + h * V_DIM:2 * RET_HEADS * QK_DIM + (h + 1) * V_DIM]
    return q, k, v


def _retention_fwd_call(pj0, consts, n_seq, seq):
    cos_full, sin_signed, decay_in, zeta, xi = consts
    nb = seq // RET_TILE
    cpb = RET_TILE // CHUNK

    def body(pj_ref, cos_ref, sin_ref, d_ref, zeta_ref, xi_ref, o_ref, st_ref, state):
        @pl.when(pl.program_id(1) == 0)
        def _():
            state[...] = jnp.zeros_like(state)

        for cc in range(cpb):
            rows = slice(cc * CHUNK, (cc + 1) * CHUNK)
            cs, sn = cos_ref[rows, :], sin_ref[rows, :]
            for h in range(RET_HEADS):
                q, k, v = _qkv(pj_ref, rows, h)
                qt = (q * cs + _rot(q) * sn) * QK_SCALE
                kt = k * cs + _rot(k) * sn
                prev = state[h]
                st_ref[cc, h] = prev.astype(st_ref.dtype)
                scores = _mm_nt(qt, kt) * d_ref[h]
                o_ref[rows, h * V_DIM:(h + 1) * V_DIM] = _mm(scores, v) + _mm(qt * xi_ref[h], prev)
                state[h] = _mm_tn(kt * zeta_ref[h], v) + CHUNK_DECAY[h] * prev

    row = lambda b, n: (b * nb + n, 0)
    pos = lambda b, n: (n, 0)
    return pl.pallas_call(
        body, name="retention_fwd", grid=(n_seq, nb),
        in_specs=[pl.BlockSpec((None, RET_TILE, SHARD_W), lambda b, n: (0,) + row(b, n)),
                  pl.BlockSpec((RET_TILE, QK_DIM), pos),
                  pl.BlockSpec((RET_TILE, QK_DIM), pos), _resident(decay_in.shape), _resident(zeta.shape),
                  _resident(xi.shape)],
        out_specs=[pl.BlockSpec((RET_TILE, D_MODEL), row),
                   pl.BlockSpec((cpb, RET_HEADS, QK_DIM, V_DIM), lambda b, n: (b * nb + n, 0, 0, 0))],
        out_shape=[jax.ShapeDtypeStruct((n_seq * seq, D_MODEL), F32),
                   jax.ShapeDtypeStruct((n_seq * seq // CHUNK, RET_HEADS, QK_DIM, V_DIM), MXU_DTYPE)],
        scratch_shapes=[pltpu.VMEM((RET_HEADS, QK_DIM, V_DIM), F32)],
        compiler_params=_params(("arbitrary", "arbitrary")),
    )(pj0, cos_full, sin_signed, decay_in, zeta, xi)


def _merge_bwd_call(dx1, pj1, pj2, pj3, ret_raw, a, b, ws, bst, w_ro, w_so, w_o):
    t = dx1.shape[0]

    def body(dx1_ref, pj1_ref, pj2_ref, pj3_ref, rr_ref, a_ref, b_ref, ws_ref, bst_ref, wro_ref, wso_ref, wo_ref,
             dpj1_ref, dpj2_ref, dpj3_ref, drr_ref, da_ref, db_ref, dws_ref, dbst_ref):
        @pl.when(pl.program_id(0) == 0)
        def _():
            dws_ref[...] = jnp.zeros_like(dws_ref)
            dbst_ref[...] = jnp.zeros_like(dbst_ref)

        wsc = _causal_ws(ws_ref)
        f = _branch_forward(pj1_ref[...], pj2_ref[...], rr_ref[...], wsc, bst_ref[...])
        pj3 = pj3_ref[...]
        smr, sms = _sigmoid(pj3[:, :D_MODEL]), _sigmoid(pj3[:, D_MODEL:])
        dmerged = _mm_nt(dx1_ref[...], wo_ref[...])
        da_f, db_f = dmerged * smr, dmerged * sms
        da, db = da_f.astype(MXU_DTYPE), db_f.astype(MXU_DTYPE)
        dpj3_ref[:, :D_MODEL] = (da_f * a_ref[...] * (1.0 - smr)).astype(dpj3_ref.dtype)
        dpj3_ref[:, D_MODEL:] = (db_f * b_ref[...] * (1.0 - sms)).astype(dpj3_ref.dtype)
        da_ref[...] = da
        db_ref[...] = db

        dret = _mm_nt(da, wro_ref[...])
        dpj1_ref[:, :D_MODEL] = (dret * f["rn"] * f["dsil_rg"]).astype(dpj1_ref.dtype)
        drn = dret * f["sil_rg"]
        for h in range(RET_HEADS):
            cols = slice(h * V_DIM, (h + 1) * V_DIM)
            drr_ref[:, cols] = _unit_norm_bwd(drn[:, cols], f["rn"][:, cols], f["rstd_r"][h]).astype(drr_ref.dtype)

        dsgu = _mm_nt(db, wso_ref[...])
        dpj2_ref[:, D_MODEL:] = (dsgu * f["u"] * f["mixed"] * f["dsil_sg"]).astype(dpj2_ref.dtype)
        tg = dsgu * f["sil_sg"]
        dpj1_ref[:, D_MODEL:] = (tg * f["mixed"] * _gelu_grad(f["su"])).astype(dpj1_ref.dtype)
        dmixed = tg * f["u"]
        rows = []
        for cc in range(TILE_M // CHUNK):
            cols = []
            for g in range(SGU_GROUPS):
                rs, cs = slice(cc * CHUNK, (cc + 1) * CHUNK), slice(g * GROUP_DIM, (g + 1) * GROUP_DIM)
                dm = dmixed[rs, cs]
                cols.append(_mm_tn(wsc[g], dm))
                dws_ref[g] += _mm_nt(dm, f["vn"][rs, cs])
                dbst_ref[:, g:g + 1] += jnp.sum(dm, axis=1, keepdims=True)
            rows.append(jnp.concatenate(cols, axis=1))
        dvn = jnp.concatenate(rows, axis=0)
        dvv = _unit_norm_bwd(dvn, f["vn"], f["rstd_v"])
        dpj2_ref[:, :D_MODEL] = (dvv * _gelu_grad(f["sv"])).astype(dpj2_ref.dtype)

    sq = (D_MODEL, D_MODEL)
    return pl.pallas_call(
        body, name="merge_bwd", grid=(t // TILE_M,),
        in_specs=[_row_tile(D_MODEL)] + [_proj_tile(j) for j in (1, 2, 3)] + [_row_tile(D_MODEL)] * 3
        + [_resident(ws.shape), _resident(bst.shape), _resident(sq), _resident(sq), _resident(sq)],
        out_specs=[_row_tile(SHARD_W)] * 3 + [_row_tile(D_MODEL)] * 3
        + [pl.BlockSpec(ws.shape, lambda i: (0, 0, 0)), pl.BlockSpec(bst.shape, lambda i: (0, 0))],
        out_shape=[jax.ShapeDtypeStruct((t, SHARD_W), MXU_DTYPE)] * 3 + [jax.ShapeDtypeStruct((t, D_MODEL), MXU_DTYPE)] * 3
        + [jax.ShapeDtypeStruct(ws.shape, F32), jax.ShapeDtypeStruct(bst.shape, F32)],
        compiler_params=_params(("arbitrary",)),
    )(dx1, pj1, pj2, pj3, ret_raw, a, b, ws, bst, w_ro, w_so, w_o)


def _tail_call(pj, ret_raw, x2d, p2d, target, ws, bst, g_ple, g_final, w_ro, w_so, w_o, w_pg, w_pp):
    t = x2d.shape[0]
    tm = TILE_M

    def body(pj1_ref, pj2_ref, pj3_ref, rr_ref, x_ref, p_ref, t_ref, ws_ref, bst_ref, gp_ref, gf_ref,
             wro_ref, wso_ref, wo_ref, wpg_ref, wpp_ref,
             dx1_ref, a_ref, b_ref, ret_ref, sgu_ref, mg_ref, hp_ref, dz_ref, dpp_ref, loss_ref, dgp_ref, dgf_ref):
        @pl.when(pl.program_id(0) == 0)
        def _():
            for acc in (loss_ref, dgp_ref, dgf_ref):
                acc[...] = jnp.zeros_like(acc)

        f = _branch_forward(pj1_ref[...], pj2_ref[...], rr_ref[...], _causal_ws(ws_ref), bst_ref[...])
        ret = f["ret"].astype(MXU_DTYPE)
        sgu = f["sgu"].astype(MXU_DTYPE)
        ret_ref[...] = ret
        sgu_ref[...] = sgu
        a = jnp.dot(ret, wro_ref[...], preferred_element_type=F32)
        b = jnp.dot(sgu, wso_ref[...], preferred_element_type=F32)
        a_ref[...] = a
        b_ref[...] = b
        pj3 = pj3_ref[...]
        smr, sms = _sigmoid(pj3[:, :D_MODEL]), _sigmoid(pj3[:, D_MODEL:])
        merged = (smr * a + sms * b).astype(MXU_DTYPE)
        mg_ref[...] = merged
        x1v = x_ref[...] + jnp.dot(merged, wo_ref[...], preferred_element_type=F32)
        xn1, r1 = _rms(x1v)
        hp = (xn1 * gp_ref[...]).astype(MXU_DTYPE)
        hp_ref[...] = hp
        gate = _sigmoid(jnp.dot(hp, wpg_ref[...], preferred_element_type=F32))
        pp = jnp.dot(p_ref[...].astype(MXU_DTYPE), wpp_ref[...], preferred_element_type=F32)
        xn2, r2 = _rms(x1v + gate * pp)
        err = xn2 * gf_ref[...] - t_ref[...]
        loss_ref[...] += (0.5 / D_MODEL) * jnp.sum(jnp.sum(err * err, axis=1, keepdims=True), axis=0, keepdims=True)

        dy = err * (1.0 / D_MODEL)
        dgf_ref[...] += jnp.sum(dy * xn2, axis=0, keepdims=True)
        dx2 = _rms_bwd(dy * gf_ref[...], xn2, r2)
        dpp_ref[...] = (dx2 * gate).astype(MXU_DTYPE)
        dz = (dx2 * pp * gate * (1.0 - gate)).astype(MXU_DTYPE)
        dz_ref[...] = dz
        dhp = _mm_nt(dz, wpg_ref[...])
        dgp_ref[...] += jnp.sum(dhp * xn1, axis=0, keepdims=True)
        dx1_ref[...] = dx2 + _rms_bwd(dhp * gp_ref[...], xn1, r1)

    sq = (D_MODEL, D_MODEL)
    vec = _resident((1, D_MODEL))
    whole = lambda shape: pl.BlockSpec(shape, lambda i: (0,) * len(shape))
    return pl.pallas_call(
        body, name="tail_fwd", grid=(t // tm,),
        in_specs=[_proj_tile(j) for j in (1, 2, 3)]
        + [_row_tile(D_MODEL), _row_tile(D_MODEL), _row_tile(PLE_DIM), _row_tile(D_MODEL), _resident(ws.shape),
           _resident(bst.shape), vec, vec, _resident(sq), _resident(sq), _resident(sq), _resident(sq),
           _resident((PLE_DIM, D_MODEL))],
        out_specs=[_row_tile(D_MODEL)] * 9 + [whole(LOSS_TILE), whole((1, D_MODEL)), whole((1, D_MODEL))],
        out_shape=[jax.ShapeDtypeStruct((t, D_MODEL), F32)] * 3 + [jax.ShapeDtypeStruct((t, D_MODEL), MXU_DTYPE)] * 6
        + [jax.ShapeDtypeStruct(LOSS_TILE, F32), jax.ShapeDtypeStruct((1, D_MODEL), F32),
           jax.ShapeDtypeStruct((1, D_MODEL), F32)],
        compiler_params=_params(("arbitrary",)),
    )(pj, pj, pj, ret_raw, x2d, p2d, target, ws, bst, g_ple, g_final, w_ro, w_so, w_o, w_pg, w_pp)


def _retention_bwd_call(pj0, drr, states, consts, n_seq, seq, side):
    cos_full, sin_signed, decay_in, zeta, xi = consts
    nb = seq // RET_TILE
    cpb = RET_TILE // CHUNK

    def body(pj_ref, do_ref, st_ref, cos_ref, sin_ref, d_ref, zeta_ref, xi_ref, dpj_ref, gstate):
        @pl.when(pl.program_id(1) == 0)
        def _():
            gstate[...] = jnp.zeros_like(gstate)

        for cc in reversed(range(cpb)):
            rows = slice(cc * CHUNK, (cc + 1) * CHUNK)
            cs, sn = cos_ref[rows, :], sin_ref[rows, :]
            for h in range(RET_HEADS):
                q, k, v = _qkv(pj_ref, rows, h)
                qt = (q * cs + _rot(q) * sn) * QK_SCALE
                kt = k * cs + _rot(k) * sn
                d_out = do_ref[rows, h * V_DIM:(h + 1) * V_DIM]
                prev = st_ref[cc, h]
                g = gstate[h]
                dec = d_ref[h]
                scores_d = _mm_nt(qt, kt) * dec
                dscores = _mm_nt(d_out, v) * dec
                kz = kt * zeta_ref[h]
                qx = qt * xi_ref[h]
                dv = _mm_tn(scores_d, d_out) + _mm(kz, g)
                dqt = (_mm(dscores, kt) + _mm_nt(d_out, prev) * xi_ref[h]) * QK_SCALE
                dkt = _mm_tn(dscores, qt) + _mm_nt(v, g) * zeta_ref[h]
                gstate[h] = _mm_tn(qx, d_out) + CHUNK_DECAY[h] * g
                dq = dqt * cs + _rot(dqt * sn)
                dk = dkt * cs + _rot(dkt * sn)
                dpj_ref[rows, h * QK_DIM:(h + 1) * QK_DIM] = dq.astype(dpj_ref.dtype)
                dpj_ref[rows, RET_HEADS * QK_DIM + h * QK_DIM:RET_HEADS * QK_DIM + (h + 1) * QK_DIM] = dk.astype(
                    dpj_ref.dtype)
                dpj_ref[rows, 2 * RET_HEADS * QK_DIM + h * V_DIM:2 * RET_HEADS * QK_DIM + (h + 1) * V_DIM] = dv.astype(
                    dpj_ref.dtype)

    row = lambda b, n: (b * nb + nb - 1 - n, 0)
    pos = lambda b, n: (nb - 1 - n, 0)
    return _compute_call(
        body, name="retention_bwd", grid=(n_seq, nb),
        in_specs=[pl.BlockSpec((None, RET_TILE, SHARD_W), lambda b, n: (0,) + row(b, n)),
                  pl.BlockSpec((RET_TILE, D_MODEL), row),
                  pl.BlockSpec((cpb, RET_HEADS, QK_DIM, V_DIM), lambda b, n: (b * nb + nb - 1 - n, 0, 0, 0)),
                  pl.BlockSpec((RET_TILE, QK_DIM), pos), pl.BlockSpec((RET_TILE, QK_DIM), pos),
                  _resident(decay_in.shape), _resident(zeta.shape), _resident(xi.shape)],
        out_specs=[pl.BlockSpec((RET_TILE, SHARD_W), row)],
        out_shape=[jax.ShapeDtypeStruct((n_seq * seq, SHARD_W), MXU_DTYPE)],
        scratch_shapes=[pltpu.VMEM((RET_HEADS, QK_DIM, V_DIM), F32)],
        operands=(pj0, drr, states, cos_full, sin_signed, decay_in, zeta, xi),
        semantics=("arbitrary", "arbitrary"), side=side)


def _dx_call(dpj, x2d, dx1, g_mixer, wg_in, side):
    t = x2d.shape[0]

    def body(d0, d1, d2, d3, x_ref, dx1_ref, g_ref, w_ref, dx_ref, dg_ref):
        @pl.when(pl.program_id(0) == 0)
        def _():
            dg_ref[...] = jnp.zeros_like(dg_ref)

        dh = _mm_nt(d0[...], w_ref[0])
        for j, d_ref in enumerate((d1, d2, d3)):
            dh += _mm_nt(d_ref[...], w_ref[j + 1])
        xn, r = _rms(x_ref[...])
        dg_ref[...] += jnp.sum(dh * xn, axis=0, keepdims=True)
        dx_ref[...] = dx1_ref[...] + _rms_bwd(dh * g_ref[...], xn, r)

    return _compute_call(
        body, name="dx_bwd", grid=(t // DX_TILE,),
        in_specs=[_row_tile(SHARD_W, DX_TILE)] * N_CHIPS + [_row_tile(D_MODEL, DX_TILE)] * 2
        + [_resident((1, D_MODEL)), _resident(wg_in.shape)],
        out_specs=[_row_tile(D_MODEL, DX_TILE), pl.BlockSpec((1, D_MODEL), lambda i: (0, 0))],
        out_shape=[jax.ShapeDtypeStruct((t, D_MODEL), F32), jax.ShapeDtypeStruct((1, D_MODEL), F32)],
        operands=(*dpj, x2d, dx1, g_mixer, wg_in), semantics=("arbitrary",), side=side)


def _wgrad_call(name, lhs, rhs, block_n, out_cols=None, block_t=1024, into=None, slot=0, n_slots=1, side=None):
    t, n = rhs.shape
    k = lhs.shape[1]
    block_t = min(block_t, t)
    steps = t // block_t
    out_cols = block_n if out_cols is None else out_cols
    per = block_n // out_cols
    first_block = slot * (n // block_n)

    def body(l_ref, r_ref, *rest):
        o_ref, acc = rest[-2:]
        @pl.when(pl.program_id(1) == 0)
        def _():
            acc[...] = jnp.zeros_like(acc)

        acc[...] += _mm_tn(l_ref[...], r_ref[...])

        @pl.when(pl.program_id(1) == steps - 1)
        def _():
            for s in range(per):
                o_ref[s] = acc[:, s * out_cols:(s + 1) * out_cols].astype(o_ref.dtype)

    res = _compute_call(
        body, name=name, grid=(n // block_n, steps),
        in_specs=[pl.BlockSpec((block_t, k), lambda j, i: (i, 0)), pl.BlockSpec((block_t, block_n), lambda j, i: (i, j))]
        + ([] if into is None else [ANY]),
        out_specs=[pl.BlockSpec((per, k, out_cols), lambda j, i: (first_block + j, 0, 0))],
        out_shape=[jax.ShapeDtypeStruct((n_slots * (n // out_cols), k, out_cols), COMM_DTYPE)],
        scratch_shapes=[pltpu.VMEM((k, block_n), F32)], aliases={} if into is None else {2: 0},
        operands=(lhs, rhs) if into is None else (lhs, rhs, into), semantics=("arbitrary", "arbitrary"), side=side)
    return res[0] if side is None else (res[0][0], res[1])


def _position():
    return lax.axis_index("x"), lax.axis_index("y"), lax.axis_index("c")


def _position_array():
    x, y, c = _position()
    return jnp.stack([2 * x + y, c]).astype(jnp.int32)


def _other_chip(x, y, k):
    return (1 - x if k & 2 else x), (1 - y if k & 1 else y)


def _plan_side(operands, out_shapes, plan, n_remote, aliases=None):
    def copies(ins, outs, send_sem, recv_sem, base=0):
        remote = plan(ins, outs)
        assert len(remote) == n_remote
        return [pltpu.make_async_remote_copy(src_ref=src, dst_ref=dst, send_sem=send_sem.at[base + i],
                                             recv_sem=recv_sem.at[base + i], device_id=dev, device_id_type=MESH)
                for i, (src, dst, dev) in enumerate(remote)]

    def start(*a):
        for cp in copies(*a):
            cp.start()

    def finish(*a):
        for cp in copies(*a):
            cp.wait()

    return _Side(operands, out_shapes, n_remote, start, finish, aliases)


def _place_cast_call(name, shards, pos):
    n = len(shards)
    rows, width = shards[0].shape
    block_rows = min(256, rows)

    def body(pos_ref, *refs):
        for w_ref, o_ref in zip(refs[:n], refs[n:]):
            o_ref[...] = w_ref[...].astype(o_ref.dtype)

    return pl.pallas_call(
        body, name=name,
        grid_spec=pltpu.PrefetchScalarGridSpec(
            num_scalar_prefetch=1, grid=(rows // block_rows,),
            in_specs=[pl.BlockSpec((block_rows, width), lambda i, pos: (i, 0))] * n,
            out_specs=[pl.BlockSpec((None, block_rows, width), lambda i, pos: (pos[0], i, 0))] * n),
        out_shape=[jax.ShapeDtypeStruct((N_CHIPS, rows, width), MXU_DTYPE)] * n,
        compiler_params=_params(("arbitrary",)),
    )(pos, *shards)


def _gather_side(placed):
    n = len(placed)

    def copies(kind, bufs, send_sem, recv_sem, base):
        x, y, c = _position()
        me = 2 * x + y
        made = []
        for i in range(n):
            hr = placed[i].shape[1] // 2
            for k in (1, 2, 3):
                px, py = _other_chip(x, y, k)
                chip, core, slot, dev = [(me, c, k - 1, (px, py, c)), (2 * px + py, c, 2 + k, (x, y, 1 - c)),
                                         (2 * px + py, 1 - c, 2 + k, (x, y, 1 - c))][kind]
                piece = bufs[i].at[chip, pl.ds(core * hr, hr)]
                made.append(pltpu.make_async_remote_copy(
                    src_ref=piece, dst_ref=piece, send_sem=send_sem.at[base + 6 * i + slot],
                    recv_sem=recv_sem.at[base + 6 * i + slot], device_id=dev, device_id_type=MESH))
        return made

    def start(ins, outs, send_sem, recv_sem, base=0):
        for cp in copies(0, outs, send_sem, recv_sem, base):
            cp.start()

    def finish(ins, outs, send_sem, recv_sem, base=0):
        first, onward = copies(0, outs, send_sem, recv_sem, base), copies(1, outs, send_sem, recv_sem, base)
        for landed, cp in zip(first, onward):
            landed.wait_recv()
            cp.start()
        for cp in copies(2, outs, send_sem, recv_sem, base):
            cp.wait_recv()
        for cp in first + onward:
            cp.wait_send()

    return _Side(placed, [jax.ShapeDtypeStruct(a.shape, a.dtype) for a in placed], 6 * n, start, finish,
                 {i: i for i in range(n)})


def _same_shape(shapes):
    found = {}
    for i, shape in enumerate(shapes):
        found.setdefault(shape, []).append(i)
    return found


def _sum_call(name, pos, groups, rows, width, out_dtype, block_rows, out_rows=None, out_index=None):
    out_rows = rows if out_rows is None else out_rows
    out_index = (lambda i, pos, br: i) if out_index is None else out_index
    sizes = [len(g) for g in groups]

    def body(pos_ref, *refs):
        ins, outs = refs[:sum(sizes)], refs[sum(sizes):]
        at = 0
        for size, out in zip(sizes, outs):
            acc = ins[at][...].astype(F32)
            for r in ins[at + 1:at + size]:
                acc = acc + r[...].astype(F32)
            out[...] = acc.astype(out_dtype)
            at += size

    def spec(index):
        return pl.BlockSpec((block_rows, width), lambda i, pos: (index(i, pos, block_rows), 0))

    return pl.pallas_call(
        body, name=name,
        grid_spec=pltpu.PrefetchScalarGridSpec(
            num_scalar_prefetch=1, grid=(rows // block_rows,),
            in_specs=[spec(index) for g in groups for _, index in g], out_specs=[spec(out_index)] * len(groups)),
        out_shape=[jax.ShapeDtypeStruct((out_rows, width), out_dtype)] * len(groups),
        compiler_params=_params(("arbitrary",)),
    )(pos, *[arr for g in groups for arr, _ in g])


def _pair_stage(tag, grads, pos):
    n = len(grads)
    shapes = [(g.shape[0] // N_CHIPS, g.shape[1]) for g in grads]

    def plan(ins, outs):
        x, y, c = _position()
        remote = []
        for i, (r, _) in enumerate(shapes):
            hr = r // 2
            for j in range(N_CHIPS):
                remote.append((ins[i].at[pl.ds(j * r + (1 - c) * hr, hr)], outs[i].at[pl.ds(j * hr, hr)],
                               (x, y, 1 - c)))
        return remote

    pair = _exchange_call(f"grad_pair_exchange_{tag}", _plan_side(
        grads, [jax.ShapeDtypeStruct((N_CHIPS * (r // 2), w), COMM_DTYPE) for r, w in shapes], plan, N_CHIPS * n))

    def own_half(r):
        def index(s, pos, br):
            per = (r // 2) // br
            return (s // per) * (r // br) + pos[1] * per + s % per
        return index

    sums = [None] * n
    for (r, w), which in _same_shape(shapes).items():
        groups = [[(grads[i], own_half(r)), (pair[i], lambda s, pos, br: s)] for i in which]
        outs = _sum_call(f"grad_pair_sum_{tag}{which[0]}", pos, groups, N_CHIPS * (r // 2), w, COMM_DTYPE,
                         min(256, r // 2))
        for i, out in zip(which, outs):
            sums[i] = out
    return sums


def _chip_side(pair_sums):
    halves = [(p.shape[0] // N_CHIPS, p.shape[1]) for p in pair_sums]

    def plan(ins, outs):
        x, y, c = _position()
        remote = []
        for i, (hr, _) in enumerate(halves):
            for k in (1, 2, 3):
                px, py = _other_chip(x, y, k)
                remote.append((ins[i].at[pl.ds((2 * px + py) * hr, hr)], outs[i].at[pl.ds((k - 1) * hr, hr)],
                               (px, py, c)))
        return remote

    return _plan_side(pair_sums, [jax.ShapeDtypeStruct((3 * hr, w), COMM_DTYPE) for hr, w in halves], plan,
                      3 * len(pair_sums))


def _finish_stage(tag, pair_sums, chip, pos, extra=None):
    n = len(pair_sums)
    halves = [(p.shape[0] // N_CHIPS, p.shape[1]) for p in pair_sums]

    def slab(k, hr):
        return lambda s, pos, br: k * (hr // br) + s

    reduced = [None] * n
    for (hr, w), which in _same_shape(halves).items():
        groups = [[(pair_sums[i], lambda s, pos, br: pos[0] * (hr // br) + s)] + [(chip[i], slab(k, hr)) for k in range(3)]
                  for i in which]
        outs = _sum_call(f"grad_chip_sum_{tag}{which[0]}", pos, groups, hr, w, F32, min(256, hr), out_rows=2 * hr,
                         out_index=lambda s, pos, br: pos[1] * (hr // br) + s)
        for i, out in zip(which, outs):
            reduced[i] = out

    def plan(ins, outs):
        x, y, c = _position()
        remote = []
        for i, (hr, _) in enumerate(halves):
            mine = outs[i].at[pl.ds(c * hr, hr)]
            remote.append((mine, mine, (x, y, 1 - c)))
        return remote

    swap = _plan_side(reduced, [jax.ShapeDtypeStruct((2 * hr, w), F32) for hr, w in halves], plan, n,
                      {i: i for i in range(n)})
    if extra is None:
        return _exchange_call(f"grad_half_exchange_{tag}", swap)
    res = _exchange_call(f"grad_half_exchange_{tag}", _join_sides(swap, extra))
    return res[:n], res[n:]


def _small_gather_side(parts):
    n = len(parts)

    def copies(ins, outs, send_sem, recv_sem, base):
        x, y, c = _position()
        made = []
        for i in range(n):
            mine = outs[i].at[4 * x + 2 * y + c]
            made.append(pltpu.make_async_copy(ins[i], mine, send_sem.at[base + 8 * i + 7]))
            for d in range(1, 8):
                px, py = _other_chip(x, y, d >> 1)
                made.append(pltpu.make_async_remote_copy(
                    src_ref=ins[i], dst_ref=mine, send_sem=send_sem.at[base + 8 * i + d - 1],
                    recv_sem=recv_sem.at[base + 8 * i + d - 1], device_id=(px, py, 1 - c if d & 1 else c),
                    device_id_type=MESH))
        return made

    def start(ins, outs, send_sem, recv_sem, base=0):
        for cp in copies(ins, outs, send_sem, recv_sem, base):
            cp.start()

    def finish(ins, outs, send_sem, recv_sem, base=0):
        for cp in copies(ins, outs, send_sem, recv_sem, base):
            cp.wait()

    return _Side(parts, [jax.ShapeDtypeStruct((8,) + a_.shape, F32) for a_ in parts], 8 * n, start, finish)


def _adamw(w, g, m, v):
    m = ADAM_B1 * m + (1.0 - ADAM_B1) * g
    v = ADAM_B2 * v + (1.0 - ADAM_B2) * (g * g)
    m_hat = m / (1.0 - ADAM_B1 ** ADAM_STEP)
    v_hat = v / (1.0 - ADAM_B2 ** ADAM_STEP)
    delta = -ADAM_LR * (m_hat / (jnp.sqrt(v_hat) + ADAM_EPS) + ADAM_WD * w)
    return delta, m, v


def _adamw_call(name, ws, gs, ms, vs):
    n = len(ws)
    rows, width = ws[0].shape
    block_rows = min(256 // n, rows)

    def body(*refs):
        for i in range(n):
            w_ref, g_ref, m_ref, v_ref = (refs[j * n + i] for j in range(4))
            d_out, m_out, v_out = (refs[(4 + j) * n + i] for j in range(3))
            d_out[...], m_out[...], v_out[...] = _adamw(w_ref[...], g_ref[...], m_ref[...], v_ref[...])

    spec = pl.BlockSpec((block_rows, width), lambda i: (i, 0))
    res = pl.pallas_call(
        body, name=name, grid=(rows // block_rows,), in_specs=[spec] * (4 * n), out_specs=[spec] * (3 * n),
        out_shape=[jax.ShapeDtypeStruct(ws[0].shape, F32)] * (3 * n),
        compiler_params=_params(("arbitrary",)),
    )(*ws, *gs, *ms, *vs)
    return [(res[i], res[n + i], res[2 * n + i]) for i in range(n)]


def _small_adamw_call(gathered, weights, moments_m, moments_v):
    n = len(weights)

    def body(*refs):
        all_refs, refs = refs[:n + 1], refs[n + 1:]
        w_refs, m_refs, v_refs, outs = refs[:n], refs[n:2 * n], refs[2 * n:3 * n], refs[3 * n:]

        def total(ref):
            acc = ref[0]
            for d in range(1, 8):
                acc = acc + ref[d]
            return acc

        outs[0][...] = total(all_refs[n])
        for i in range(n):
            g = total(all_refs[i])
            if i == 0:
                row = lax.broadcasted_iota(jnp.int32, g.shape, 0)
                col = lax.broadcasted_iota(jnp.int32, g.shape, 1)
                g = jnp.where((row % CHUNK) >= col, g, 0.0)
            g_out, d_out, m_out, v_out = outs[1 + 4 * i:5 + 4 * i]
            g_out[...] = g
            d_out[...], m_out[...], v_out[...] = _adamw(w_refs[i][...], g, m_refs[i][...], v_refs[i][...])

    out_shape = [jax.ShapeDtypeStruct(LOSS_TILE, F32)]
    for w in weights:
        out_shape += [jax.ShapeDtypeStruct(w.shape, F32)] * 4
    res = pl.pallas_call(
        body, name="small_adamw", out_shape=out_shape,
        compiler_params=pltpu.CompilerParams(vmem_limit_bytes=VMEM_LIMIT),
    )(*gathered, *weights, *moments_m, *moments_v)
    return res[0], [res[1 + 4 * i:5 + 4 * i] for i in range(n)]


def kernel(x, p, w_in, w_ret_out, w_sgu_out, w_out, sgu_ws, sgu_bs, w_ple_gate, w_ple_proj, g_mixer, g_ple, g_final, loss_target, m_w_in, m_w_ret_out, m_w_sgu_out, m_w_out, m_sgu_ws, m_sgu_bs, m_w_ple_gate, m_w_ple_proj, m_g_mixer, m_g_ple, m_g_final, v_w_in, v_w_ret_out, v_w_sgu_out, v_w_out, v_sgu_ws, v_sgu_bs, v_w_ple_gate, v_w_ple_proj, v_g_mixer, v_g_ple, v_g_final):
    n_seq, seq, _ = x.shape
    t = n_seq * seq
    x2d = x.reshape(t, D_MODEL)
    p2d = p.reshape(t, PLE_DIM)
    target = loss_target.reshape(t, D_MODEL)
    big = [w_in[0], w_ret_out[0], w_sgu_out[0], w_out[0], w_ple_gate[0], w_ple_proj[0]]
    big_m = [m_w_in[0], m_w_ret_out[0], m_w_sgu_out[0], m_w_out[0], m_w_ple_gate[0], m_w_ple_proj[0]]
    big_v = [v_w_in[0], v_w_ret_out[0], v_w_sgu_out[0], v_w_out[0], v_w_ple_gate[0], v_w_ple_proj[0]]

    pos = _position_array()
    placed = (_place_cast_call("place_w_in", big[:1], pos) + _place_cast_call("place_square_weights", big[1:5], pos)
              + _place_cast_call("place_w_ple_proj", big[5:], pos))
    ws = sgu_ws[0]
    bst = sgu_bs[0].T
    consts = _retention_consts(seq)

    (h, pj, wg_in), gathered = _proj_call(x2d, g_mixer, placed[0], pos, _gather_side(placed[1:]))
    pj0 = pj1 = pj2 = pj3 = pj
    w_ro, w_so, w_o, w_pg = (w.reshape(D_MODEL, D_MODEL) for w in gathered[:4])
    w_pp = gathered[4].transpose(1, 0, 2).reshape(PLE_DIM, D_MODEL)
    ret_raw, states = _retention_fwd_call(pj0, consts, n_seq, seq)
    dx1, a, b, ret, sgu, merged, hp, dz, dpp, loss, dg_ple, dg_final = _tail_call(
        pj, ret_raw, x2d, p2d, target, ws, bst, g_ple, g_final.reshape(1, D_MODEL), w_ro, w_so, w_o, w_pg, w_pp)
    dpj1, dpj2, dpj3, drr, da, db, dws, dbst = _merge_bwd_call(dx1, pj1, pj2, pj3, ret_raw, a, b, ws, bst,
                                                               w_ro, w_so, w_o)

    small_shapes = [(SGU_GROUPS * CHUNK, CHUNK), (SGU_GROUPS, CHUNK), (1, D_MODEL), (1, D_MODEL), (1, D_MODEL)]
    as_small = lambda arrays: [None if a_ is None else a_.reshape(s_) for a_, s_ in zip(arrays, small_shapes)]
    early = as_small([dws, dbst.T, None, dg_ple, dg_final])
    rows_of = lambda g: g.reshape(g.shape[0] * g.shape[1], g.shape[2])
    tail_grads = [
        rows_of(_wgrad_call("wgrad_ret_out", ret, da, D_MODEL)),
        rows_of(_wgrad_call("wgrad_sgu_out", sgu, db, D_MODEL)),
        rows_of(_wgrad_call("wgrad_out", merged, dx1, D_MODEL)),
        rows_of(_wgrad_call("wgrad_ple_gate", hp, dz, D_MODEL)),
        rows_of(_wgrad_call("wgrad_ple_proj", p2d, dpp, D_MODEL, out_cols=PLE_DIM)),
    ]
    tail_sums = _pair_stage("tail", tail_grads, pos)
    (dpj0,), carried = _retention_bwd_call(
        pj0, drr, states, consts, n_seq, seq,
        _join_sides(_chip_side(tail_sums), _small_gather_side([early[0], early[1], early[3], early[4], loss])))
    tail_chip, early_all = carried[:len(tail_sums)], carried[len(tail_sums):]
    in_grad = None
    for j, d in enumerate((dpj0, dpj1, dpj2, dpj3)):
        in_grad = _wgrad_call(f"wgrad_in_{j}", h, d, SHARD_W, into=in_grad, slot=j, n_slots=N_CHIPS)
    in_sums = _pair_stage("in", [rows_of(in_grad)], pos)
    (dx, dg_mixer), in_chip = _dx_call((dpj0, dpj1, dpj2, dpj3), x2d, dx1, g_mixer, wg_in, _chip_side(in_sums))
    g_big, (mixer_all,) = _finish_stage("all", in_sums + tail_sums, list(in_chip) + list(tail_chip), pos,
                                        _small_gather_side([dg_mixer]))
    upd = [None] * len(big)
    for _, which in _same_shape([w.shape for w in big]).items():
        pick = lambda arrays: [arrays[i] for i in which]
        for i, triple in zip(which, _adamw_call(f"adamw_{which[0]}", pick(big), pick(g_big), pick(big_m), pick(big_v))):
            upd[i] = triple

    small_g = [early_all[0], early_all[1], mixer_all, early_all[2], early_all[3], early_all[4]]
    total, small = _small_adamw_call(small_g, as_small([sgu_ws, sgu_bs, g_mixer, g_ple, g_final]),
                                     as_small([m_sgu_ws, m_sgu_bs, m_g_mixer, m_g_ple, m_g_final]),
                                     as_small([v_sgu_ws, v_sgu_bs, v_g_mixer, v_g_ple, v_g_final]))
    out_small_shapes = [sgu_ws.shape, sgu_bs.shape, g_mixer.shape, g_ple.shape, g_final.shape]

    def ordered(big_list, kind):
        w_in_, w_ro_, w_so_, w_o_, w_pg_, w_pp_ = [b_[None] for b_ in big_list]
        s_ws, s_bs, s_gm, s_gp, s_gf = [small[i][kind].reshape(s) for i, s in enumerate(out_small_shapes)]
        return [w_in_, w_ro_, w_so_, w_o_, s_ws, s_bs, w_pg_, w_pp_, s_gm, s_gp, s_gf]

    out = [total[0, 0], dx.reshape(x.shape)]
    out += ordered(g_big, 0)
    out += ordered([u[0] for u in upd], 1)
    out += ordered([u[1] for u in upd], 2)
    out += ordered([u[2] for u in upd], 3)
    return tuple(out)
```

```python
import functools
import math

import numpy as np
import jax
import jax.numpy as jnp
from jax import lax
from jax.experimental import pallas as pl
from jax.experimental.pallas import tpu as pltpu

F32 = jnp.float32
MXU_DTYPE = jnp.bfloat16
COMM_DTYPE = jnp.bfloat16

D_MODEL = 1024
RET_HEADS = 4
QK_DIM = 128
V_DIM = 256
CHUNK = 128
SGU_GROUPS = 4
GROUP_DIM = 256
PLE_DIM = 256
N_CHIPS = 4
SHARD_W = 2048
ROPE_BASE = 10000.0
NORM_EPS = 1e-6
GN_EPS = 1e-5
QK_SCALE = QK_DIM ** -0.5
SQRT_HALF = math.sqrt(0.5)
INV_SQRT_2PI = 1.0 / math.sqrt(2.0 * math.pi)

ADAM_LR = 0.001
ADAM_B1 = 0.9
ADAM_B2 = 0.999
ADAM_EPS = 1e-08
ADAM_WD = 0.01
ADAM_STEP = 10

TILE_M = 256
LOSS_TILE = (8, 128)
LOAD_PARTS = 16
PROJ_TILE = 512
RET_TILE = 512
DX_TILE = 512
VMEM_LIMIT = 56 * 1024 * 1024
MESH = pl.DeviceIdType.MESH
ANY = pl.BlockSpec(memory_space=pl.ANY)

CHUNK_DECAY = tuple(
    float(np.exp(np.float32(CHUNK) * np.log(np.float32(1.0 - 2.0 ** (-5.0 - h))))) for h in range(RET_HEADS))


def _mm(a, b):
    return jnp.dot(a.astype(MXU_DTYPE), b.astype(MXU_DTYPE), preferred_element_type=F32)


def _mm_nt(a, b):
    return lax.dot_general(a.astype(MXU_DTYPE), b.astype(MXU_DTYPE), (((1,), (1,)), ((), ())),
                           preferred_element_type=F32)


def _mm_tn(a, b):
    return lax.dot_general(a.astype(MXU_DTYPE), b.astype(MXU_DTYPE), (((0,), (0,)), ((), ())),
                           preferred_element_type=F32)


def _mean(x):
    return jnp.mean(x, axis=-1, keepdims=True)


def _sigmoid(x):
    return jax.nn.sigmoid(x)


def _silu_and_grad(x):
    s = _sigmoid(x)
    silu = x * s
    return silu, s + silu * (1.0 - s)


def _gelu_and_grad(x):
    cdf = 0.5 + 0.5 * lax.erf(x * SQRT_HALF)
    return x * cdf, cdf + x * (jnp.exp(x * x * -0.5) * INV_SQRT_2PI)


def _unit_norm(x, eps):
    xc = x - _mean(x)
    rstd = lax.rsqrt(_mean(xc * xc) + eps)
    return xc * rstd, rstd


def _unit_norm_bwd(dn, n, rstd):
    return rstd * (dn - _mean(dn) - n * _mean(dn * n))


def _rms(x):
    r = lax.rsqrt(_mean(x * x) + NORM_EPS)
    return x * r, r


def _rms_bwd(dxn, xn, r):
    return r * (dxn - xn * _mean(dxn * xn))


def _rot(x):
    return pltpu.roll(x, QK_DIM // 2, 1)


def _params(semantics, **kw):
    return pltpu.CompilerParams(dimension_semantics=semantics, vmem_limit_bytes=VMEM_LIMIT, **kw)


def _row_tile(width, tm=TILE_M):
    return pl.BlockSpec((tm, width), lambda i: (i, 0))


def _proj_tile(j):
    return pl.BlockSpec((None, TILE_M, SHARD_W), lambda i: (j, i, 0))


def _resident(shape):
    nd = len(shape)
    return pl.BlockSpec(shape, lambda *_: (0,) * nd, pipeline_mode=pl.Buffered(1))


def _causal_ws(ws_ref):
    row = lax.broadcasted_iota(jnp.int32, (CHUNK, CHUNK), 0)
    col = lax.broadcasted_iota(jnp.int32, (CHUNK, CHUNK), 1)
    return [jnp.where(row >= col, ws_ref[g], 0.0).astype(MXU_DTYPE) for g in range(SGU_GROUPS)]


def _heads(x, width):
    return [x[:, h * width:(h + 1) * width] for h in range(x.shape[1] // width)]


def _branch_forward(pj1, pj2, ret_raw, wsc, bst, mixed_ref):
    rg, su = pj1[:, :D_MODEL], pj1[:, D_MODEL:]
    sv, sg = pj2[:, :D_MODEL], pj2[:, D_MODEL:]
    rn_parts, rstd_parts = zip(*[_unit_norm(r, GN_EPS) for r in _heads(ret_raw, V_DIM)])
    rn = jnp.concatenate(rn_parts, axis=1)
    sil_rg, dsil_rg = _silu_and_grad(rg)
    ret = rn * sil_rg
    u, du = _gelu_and_grad(su)
    gelu_sv, dgelu_sv = _gelu_and_grad(sv)
    vn, rstd_v = _unit_norm(gelu_sv, GN_EPS)
    for cc in range(pj1.shape[0] // CHUNK):
        for g in range(SGU_GROUPS):
            rs, cs = slice(cc * CHUNK, (cc + 1) * CHUNK), slice(g * GROUP_DIM, (g + 1) * GROUP_DIM)
            mixed_ref[rs, cs] = _mm(wsc[g], vn[rs, cs]) + bst[:, g:g + 1]
    mixed = mixed_ref[...]
    sil_sg, dsil_sg = _silu_and_grad(sg)
    sgu = u * mixed * sil_sg
    return dict(rn=rn, rstd_r=rstd_parts, sil_rg=sil_rg, dsil_rg=dsil_rg, ret=ret, u=u, du=du, dgelu_sv=dgelu_sv,
                vn=vn, rstd_v=rstd_v, mixed=mixed, sil_sg=sil_sg, dsil_sg=dsil_sg, sgu=sgu)


class _Side:
    def __init__(self, operands, out_shapes, n_sems, start, finish, aliases=None):
        self.operands, self.out_shapes, self.n_sems = list(operands), list(out_shapes), n_sems
        self.start, self.finish, self.aliases = start, finish, dict(aliases or {})


def _join_sides(*sides):
    spans, a, b, s = [], 0, 0, 0
    for side in sides:
        spans.append((a, b, s))
        a, b, s = a + len(side.operands), b + len(side.out_shapes), s + side.n_sems

    def run(which):
        def go(ins, outs, send_sem, recv_sem, base=0):
            for side, (a0, b0, s0) in zip(sides, spans):
                getattr(side, which)(ins[a0:a0 + len(side.operands)], outs[b0:b0 + len(side.out_shapes)],
                                     send_sem, recv_sem, base + s0)
        return go

    aliases = {a0 + i: b0 + o for side, (a0, b0, _) in zip(sides, spans) for i, o in side.aliases.items()}
    return _Side([x for side in sides for x in side.operands], [x for side in sides for x in side.out_shapes], s,
                 run("start"), run("finish"), aliases)


def _compute_call(body, *, name, grid, in_specs, out_specs, out_shape, operands, semantics, scratch_shapes=(),
                  side=None, prefetch=None, aliases=None, side_start=None):
    n_pre = 0 if prefetch is None else 1
    pre = () if prefetch is None else (prefetch,)

    def spec(in_specs, out_specs, scratch):
        return pltpu.PrefetchScalarGridSpec(num_scalar_prefetch=n_pre, grid=grid, in_specs=in_specs,
                                            out_specs=out_specs, scratch_shapes=scratch)

    if side is None:
        return pl.pallas_call(body, name=name, grid_spec=spec(in_specs, out_specs, list(scratch_shapes)),
                              out_shape=out_shape,
                              input_output_aliases={n_pre + a: b for a, b in (aliases or {}).items()},
                              compiler_params=_params(semantics))(*pre, *operands)
    n_in, n_out, s_in, s_out = len(operands), len(out_shape), len(side.operands), len(side.out_shapes)

    def carrier(*refs):
        pre_refs, refs = refs[:n_pre], refs[n_pre:]
        ins, refs = refs[:n_in], refs[n_in:]
        side_ins, refs = refs[:s_in], refs[s_in:]
        outs, refs = refs[:n_out], refs[n_out:]
        side_outs, refs = refs[:s_out], refs[s_out:]
        scratch, (send_sem, recv_sem) = refs[:-2], refs[-2:]
        ids = [pl.program_id(a) for a in range(len(grid))]
        at = (0,) * len(grid) if side_start is None else side_start
        first = functools.reduce(jnp.logical_and, [i == a for i, a in zip(ids, at)])
        last = functools.reduce(jnp.logical_and, [i == g - 1 for i, g in zip(ids, grid)])

        @pl.when(first)
        def _():
            side.start(side_ins, side_outs, send_sem, recv_sem)

        body(*pre_refs, *ins, *outs, *scratch)

        @pl.when(last)
        def _():
            side.finish(side_ins, side_outs, send_sem, recv_sem)

    all_aliases = {n_pre + a: b for a, b in (aliases or {}).items()}
    all_aliases.update({n_pre + n_in + a: n_out + b for a, b in side.aliases.items()})
    res = pl.pallas_call(
        carrier, name=name,
        grid_spec=spec(list(in_specs) + [ANY] * s_in, list(out_specs) + [ANY] * s_out,
                       list(scratch_shapes) + [pltpu.SemaphoreType.DMA((side.n_sems,))] * 2),
        out_shape=list(out_shape) + side.out_shapes, input_output_aliases=all_aliases,
        compiler_params=_params(semantics, has_side_effects=True),
    )(*pre, *operands, *side.operands)
    return res[:n_out], res[n_out:]


def _exchange_call(name, side):
    s_in = len(side.operands)

    def body(*refs):
        ins, outs = refs[:s_in], refs[s_in:s_in + len(side.out_shapes)]
        send_sem, recv_sem = refs[s_in + len(side.out_shapes):]
        side.start(ins, outs, send_sem, recv_sem)
        side.finish(ins, outs, send_sem, recv_sem)

    return pl.pallas_call(
        body, name=name, in_specs=[ANY] * s_in, out_specs=[ANY] * len(side.out_shapes), out_shape=side.out_shapes,
        scratch_shapes=[pltpu.SemaphoreType.DMA((side.n_sems,))] * 2, input_output_aliases=side.aliases,
        compiler_params=pltpu.CompilerParams(has_side_effects=True),
    )(*side.operands)


def _proj_call(x2d, g_mixer, placed_in, pos, side):
    t = x2d.shape[0]
    nt = t // PROJ_TILE
    hr = placed_in.shape[1] // 2

    def body(pos_ref, x_ref, g_ref, win_ref, h_ref, pj_ref, w_ref, w_vmem, h_all, loc_sem, send_sem, recv_sem):
        k, i = pl.program_id(0), pl.program_id(1)
        x, y, c = _position()
        me = 2 * x + y

        def copy(slot, chip, core, dev):
            piece = w_ref.at[chip, pl.ds(core * hr, hr)]
            return pltpu.make_async_remote_copy(src_ref=piece, dst_ref=piece, send_sem=send_sem.at[slot],
                                                recv_sem=recv_sem.at[slot], device_id=dev, device_id_type=MESH)

        def first_hop(kk):
            px, py = _other_chip(x, y, kk)
            return copy(kk - 1, me, c, (px, py, c))

        def relay():
            source = jnp.bitwise_xor(me, 2 - c)
            return copy(2, source, c, (jnp.bitwise_xor(x, c), jnp.bitwise_xor(y, 1 - c), c))

        def onward(kk, core):
            px, py = _other_chip(x, y, kk)
            return copy(2 + kk, 2 * px + py, core, (x, y, 1 - c))

        def loads(kk):
            px, py = _other_chip(x, y, kk)
            rows = w_vmem.shape[1] // LOAD_PARTS
            return [pltpu.make_async_copy(w_ref.at[2 * px + py, pl.ds(r * rows, rows)],
                                          w_vmem.at[kk % 2, pl.ds(r * rows, rows)], loc_sem.at[r])
                    for r in range(LOAD_PARTS)]

        @pl.when(jnp.logical_and(k == 0, i == 0))
        def _():
            for kk in (1, 2):
                first_hop(kk).start()
            for cp in loads(0):
                cp.start()
            for cp in loads(0):
                cp.wait()

        for kk in (1, 2, 3):
            @pl.when(jnp.logical_and(k == kk - 1, i == nt - 1))
            def _(kk=kk):
                if kk == 1:
                    first_hop(1).wait_recv()
                    first_hop(2).wait_recv()
                    relay().start()
                if kk == 3:
                    relay().wait_recv()
                onward(kk, c).start()
                onward(kk, 1 - c).wait_recv()
                for cp in loads(kk):
                    cp.start()

            @pl.when(jnp.logical_and(k == kk, i == 0))
            def _(kk=kk):
                for cp in loads(kk):
                    cp.wait()

        rows = pl.ds(pl.multiple_of(i * PROJ_TILE, PROJ_TILE), PROJ_TILE)

        @pl.when(k == 0)
        def _():
            xn, _ = _rms(x_ref[...])
            h = (xn * g_ref[...]).astype(MXU_DTYPE)
            h_all[rows, :] = h
            h_ref[...] = h

        pj_ref[...] = jnp.dot(h_all[rows, :], w_vmem[k % 2], preferred_element_type=F32)

        @pl.when(jnp.logical_and(k == N_CHIPS - 1, i == nt - 1))
        def _():
            for cp in [first_hop(1), first_hop(2), relay()] + [onward(kk, c) for kk in (1, 2, 3)]:
                cp.wait_send()

    parked = lambda k, i, pos: (jnp.where(k == 0, i, nt - 1), 0)
    return _compute_call(
        body, name="proj_fwd", grid=(N_CHIPS, nt),
        in_specs=[pl.BlockSpec((PROJ_TILE, D_MODEL), parked), pl.BlockSpec((1, D_MODEL), lambda k, i, pos: (0, 0)), ANY],
        out_specs=[pl.BlockSpec((PROJ_TILE, D_MODEL), lambda k, i, pos: (jnp.where(k == 0, i, nt), 0)),
                   pl.BlockSpec((None, PROJ_TILE, SHARD_W), lambda k, i, pos: (jnp.bitwise_xor(pos[0], k), i, 0)),
                   ANY],
        out_shape=[jax.ShapeDtypeStruct((t + PROJ_TILE, D_MODEL), MXU_DTYPE),
                   jax.ShapeDtypeStruct((N_CHIPS, t, SHARD_W), F32),
                   jax.ShapeDtypeStruct(placed_in.shape, placed_in.dtype)],
        scratch_shapes=[pltpu.VMEM((2,) + placed_in.shape[1:], placed_in.dtype), pltpu.VMEM((t, D_MODEL), MXU_DTYPE),
                        pltpu.SemaphoreType.DMA((LOAD_PARTS,)), pltpu.SemaphoreType.DMA((6,)),
                        pltpu.SemaphoreType.DMA((6,))],
        operands=(x2d, g_mixer, placed_in), semantics=("arbitrary", "arbitrary"), side=side, prefetch=pos,
        aliases={2: 2}, side_start=(2, 0))


def _retention_consts(seq):
    half = QK_DIM // 2
    inv = ROPE_BASE ** (-jnp.arange(half, dtype=F32) / half)
    ang = jnp.arange(seq, dtype=F32)[:, None] * inv[None, :]
    cos, sin = jnp.cos(ang), jnp.sin(ang)
    cos_full = jnp.concatenate([cos, cos], axis=1)
    sin_signed = jnp.concatenate([-sin, sin], axis=1)
    log_g = jnp.log(1.0 - 2.0 ** (-5.0 - jnp.arange(RET_HEADS, dtype=F32)))
    idx = jnp.arange(CHUNK, dtype=F32)
    diff = idx[:, None] - idx[None, :]
    decay_in = jnp.where(diff[None] >= 0, jnp.exp(jnp.maximum(diff, 0.0)[None] * log_g[:, None, None]), 0.0)
    zeta = jnp.exp((CHUNK - 1.0 - idx)[None, :] * log_g[:, None])
    xi = jnp.exp((idx + 1.0)[None, :] * log_g[:, None])
    zeta = jnp.broadcast_to(zeta[:, :, None], (RET_HEADS, CHUNK, QK_DIM))
    xi = jnp.broadcast_to(xi[:, :, None], (RET_HEADS, CHUNK, QK_DIM))
    return cos_full, sin_signed, decay_in, zeta, xi


def _qkv(pj_ref, rows, h):
    q = pj_ref[rows, h * QK_DIM:(h + 1) * QK_DIM]
    k = pj_ref[rows, RET_HEADS * QK_DIM + h * QK_DIM:RET_HEADS * QK_DIM + (h + 1) * QK_DIM]
    v = pj_ref[rows, 2 * RET_HEADS * QK_DIM + h * V_DIM:2 * RET_HEADS * QK_DIM + (h + 1) * V_DIM]
    return q, k, v


def _retention_fwd_call(pj0, consts, n_seq, seq):
    cos_full, sin_signed, decay_in, zeta, xi = consts
    nb = seq // RET_TILE
    cpb = RET_TILE // CHUNK

    def body(pj_ref, cos_ref, sin_ref, d_ref, zeta_ref, xi_ref, o_ref, st_ref, state):
        @pl.when(pl.program_id(1) == 0)
        def _():
            state[...] = jnp.zeros_like(state)

        for cc in range(cpb):
            rows = slice(cc * CHUNK, (cc + 1) * CHUNK)
            cs, sn = cos_ref[rows, :], sin_ref[rows, :]
            for h in range(RET_HEADS):
                q, k, v = _qkv(pj_ref, rows, h)
                qt = (q * cs + _rot(q) * sn) * QK_SCALE
                kt = k * cs + _rot(k) * sn
                prev = state[h]
                st_ref[cc, h] = prev.astype(st_ref.dtype)
                scores = _mm_nt(qt, kt) * d_ref[h]
                o_ref[rows, h * V_DIM:(h + 1) * V_DIM] = _mm(scores, v) + _mm(qt * xi_ref[h], prev)
                state[h] = _mm_tn(kt * zeta_ref[h], v) + CHUNK_DECAY[h] * prev

    row = lambda b, n: (b * nb + n, 0)
    pos = lambda b, n: (n, 0)
    return pl.pallas_call(
        body, name="retention_fwd", grid=(n_seq, nb),
        in_specs=[pl.BlockSpec((None, RET_TILE, SHARD_W), lambda b, n: (0,) + row(b, n)),
                  pl.BlockSpec((RET_TILE, QK_DIM), pos),
                  pl.BlockSpec((RET_TILE, QK_DIM), pos), _resident(decay_in.shape), _resident(zeta.shape),
                  _resident(xi.shape)],
        out_specs=[pl.BlockSpec((RET_TILE, D_MODEL), row),
                   pl.BlockSpec((cpb, RET_HEADS, QK_DIM, V_DIM), lambda b, n: (b * nb + n, 0, 0, 0))],
        out_shape=[jax.ShapeDtypeStruct((n_seq * seq, D_MODEL), F32),
                   jax.ShapeDtypeStruct((n_seq * seq // CHUNK, RET_HEADS, QK_DIM, V_DIM), MXU_DTYPE)],
        scratch_shapes=[pltpu.VMEM((RET_HEADS, QK_DIM, V_DIM), F32)],
        compiler_params=_params(("arbitrary", "arbitrary")),
    )(pj0, cos_full, sin_signed, decay_in, zeta, xi)


def _merge_bwd_call(dx1, pj1, pj2, pj3, ret_raw, a, b, ws, bst, w_ro, w_so, w_o):
    t = dx1.shape[0]

    def body(dx1_ref, pj1_ref, pj2_ref, pj3_ref, rr_ref, a_ref, b_ref, ws_ref, bst_ref, wro_ref, wso_ref, wo_ref,
             dpj1_ref, dpj2_ref, dpj3_ref, drr_ref, da_ref, db_ref, dws_ref, dbst_ref, mixed_ref):
        @pl.when(pl.program_id(0) == 0)
        def _():
            dws_ref[...] = jnp.zeros_like(dws_ref)
            dbst_ref[...] = jnp.zeros_like(dbst_ref)

        wsc = _causal_ws(ws_ref)
        f = _branch_forward(pj1_ref[...], pj2_ref[...], rr_ref[...], wsc, bst_ref[...], mixed_ref)
        pj3 = pj3_ref[...]
        smr, sms = _sigmoid(pj3[:, :D_MODEL]), _sigmoid(pj3[:, D_MODEL:])
        dmerged = _mm_nt(dx1_ref[...], wo_ref[...])
        da_f, db_f = dmerged * smr, dmerged * sms
        da, db = da_f.astype(MXU_DTYPE), db_f.astype(MXU_DTYPE)
        dpj3_ref[:, :D_MODEL] = (da_f * a_ref[...] * (1.0 - smr)).astype(dpj3_ref.dtype)
        dpj3_ref[:, D_MODEL:] = (db_f * b_ref[...] * (1.0 - sms)).astype(dpj3_ref.dtype)
        da_ref[...] = da
        db_ref[...] = db

        dret = _mm_nt(da, wro_ref[...])
        dpj1_ref[:, :D_MODEL] = (dret * f["rn"] * f["dsil_rg"]).astype(dpj1_ref.dtype)
        drn = dret * f["sil_rg"]
        for h in range(RET_HEADS):
            cols = slice(h * V_DIM, (h + 1) * V_DIM)
            drr_ref[:, cols] = _unit_norm_bwd(drn[:, cols], f["rn"][:, cols], f["rstd_r"][h]).astype(drr_ref.dtype)

        dsgu = _mm_nt(db, wso_ref[...])
        dpj2_ref[:, D_MODEL:] = (dsgu * f["u"] * f["mixed"] * f["dsil_sg"]).astype(dpj2_ref.dtype)
        tg = dsgu * f["sil_sg"]
        dpj1_ref[:, D_MODEL:] = (tg * f["mixed"] * f["du"]).astype(dpj1_ref.dtype)
        dmixed = tg * f["u"]
        for cc in range(TILE_M // CHUNK):
            for g in range(SGU_GROUPS):
                rs, cs = slice(cc * CHUNK, (cc + 1) * CHUNK), slice(g * GROUP_DIM, (g + 1) * GROUP_DIM)
                dm = dmixed[rs, cs]
                mixed_ref[rs, cs] = _mm_tn(wsc[g], dm)
                dws_ref[g] += _mm_nt(dm, f["vn"][rs, cs])
                dbst_ref[:, g:g + 1] += jnp.sum(dm, axis=1, keepdims=True)
        dvv = _unit_norm_bwd(mixed_ref[...], f["vn"], f["rstd_v"])
        dpj2_ref[:, :D_MODEL] = (dvv * f["dgelu_sv"]).astype(dpj2_ref.dtype)

    sq = (D_MODEL, D_MODEL)
    return pl.pallas_call(
        body, name="merge_bwd", grid=(t // TILE_M,),
        in_specs=[_row_tile(D_MODEL)] + [_proj_tile(j) for j in (1, 2, 3)] + [_row_tile(D_MODEL)] * 3
        + [_resident(ws.shape), _resident(bst.shape), _resident(sq), _resident(sq), _resident(sq)],
        out_specs=[_row_tile(SHARD_W)] * 3 + [_row_tile(D_MODEL)] * 3
        + [pl.BlockSpec(ws.shape, lambda i: (0, 0, 0)), pl.BlockSpec(bst.shape, lambda i: (0, 0))],
        out_shape=[jax.ShapeDtypeStruct((t, SHARD_W), MXU_DTYPE)] * 3 + [jax.ShapeDtypeStruct((t, D_MODEL), MXU_DTYPE)] * 3
        + [jax.ShapeDtypeStruct(ws.shape, F32), jax.ShapeDtypeStruct(bst.shape, F32)],
        scratch_shapes=[pltpu.VMEM((TILE_M, D_MODEL), F32)], compiler_params=_params(("arbitrary",)),
    )(dx1, pj1, pj2, pj3, ret_raw, a, b, ws, bst, w_ro, w_so, w_o)


def _tail_call(pj, ret_raw, x2d, p2d, target, ws, bst, g_ple, g_final, w_ro, w_so, w_o, w_pg, w_pp):
    t = x2d.shape[0]
    tm = TILE_M

    def body(pj1_ref, pj2_ref, pj3_ref, rr_ref, x_ref, p_ref, t_ref, ws_ref, bst_ref, gp_ref, gf_ref,
             wro_ref, wso_ref, wo_ref, wpg_ref, wpp_ref,
             dx1_ref, a_ref, b_ref, ret_ref, sgu_ref, mg_ref, hp_ref, dz_ref, dpp_ref, loss_ref, dgp_ref, dgf_ref,
             mixed_ref):
        @pl.when(pl.program_id(0) == 0)
        def _():
            for acc in (loss_ref, dgp_ref, dgf_ref):
                acc[...] = jnp.zeros_like(acc)

        f = _branch_forward(pj1_ref[...], pj2_ref[...], rr_ref[...], _causal_ws(ws_ref), bst_ref[...], mixed_ref)
        ret = f["ret"].astype(MXU_DTYPE)
        sgu = f["sgu"].astype(MXU_DTYPE)
        ret_ref[...] = ret
        sgu_ref[...] = sgu
        a = jnp.dot(ret, wro_ref[...], preferred_element_type=F32)
        b = jnp.dot(sgu, wso_ref[...], preferred_element_type=F32)
        a_ref[...] = a
        b_ref[...] = b
        pj3 = pj3_ref[...]
        smr, sms = _sigmoid(pj3[:, :D_MODEL]), _sigmoid(pj3[:, D_MODEL:])
        merged = (smr * a + sms * b).astype(MXU_DTYPE)
        mg_ref[...] = merged
        x1v = x_ref[...] + jnp.dot(merged, wo_ref[...], preferred_element_type=F32)
        xn1, r1 = _rms(x1v)
        hp = (xn1 * gp_ref[...]).astype(MXU_DTYPE)
        hp_ref[...] = hp
        gate = _sigmoid(jnp.dot(hp, wpg_ref[...], preferred_element_type=F32))
        pp = jnp.dot(p_ref[...].astype(MXU_DTYPE), wpp_ref[...], preferred_element_type=F32)
        xn2, r2 = _rms(x1v + gate * pp)
        err = xn2 * gf_ref[...] - t_ref[...]
        loss_ref[...] += (0.5 / D_MODEL) * jnp.sum(jnp.sum(err * err, axis=1, keepdims=True), axis=0, keepdims=True)

        dy = err * (1.0 / D_MODEL)
        dgf_ref[...] += jnp.sum(dy * xn2, axis=0, keepdims=True)
        dx2 = _rms_bwd(dy * gf_ref[...], xn2, r2)
        dpp_ref[...] = (dx2 * gate).astype(MXU_DTYPE)
        dz = (dx2 * pp * gate * (1.0 - gate)).astype(MXU_DTYPE)
        dz_ref[...] = dz
        dhp = _mm_nt(dz, wpg_ref[...])
        dgp_ref[...] += jnp.sum(dhp * xn1, axis=0, keepdims=True)
        dx1_ref[...] = dx2 + _rms_bwd(dhp * gp_ref[...], xn1, r1)

    sq = (D_MODEL, D_MODEL)
    vec = _resident((1, D_MODEL))
    whole = lambda shape: pl.BlockSpec(shape, lambda i: (0,) * len(shape))
    return pl.pallas_call(
        body, name="tail_fwd", grid=(t // tm,),
        in_specs=[_proj_tile(j) for j in (1, 2, 3)]
        + [_row_tile(D_MODEL), _row_tile(D_MODEL), _row_tile(PLE_DIM), _row_tile(D_MODEL), _resident(ws.shape),
           _resident(bst.shape), vec, vec, _resident(sq), _resident(sq), _resident(sq), _resident(sq),
           _resident((PLE_DIM, D_MODEL))],
        out_specs=[_row_tile(D_MODEL)] * 9 + [whole(LOSS_TILE), whole((1, D_MODEL)), whole((1, D_MODEL))],
        out_shape=[jax.ShapeDtypeStruct((t, D_MODEL), F32)] * 3 + [jax.ShapeDtypeStruct((t, D_MODEL), MXU_DTYPE)] * 6
        + [jax.ShapeDtypeStruct(LOSS_TILE, F32), jax.ShapeDtypeStruct((1, D_MODEL), F32),
           jax.ShapeDtypeStruct((1, D_MODEL), F32)],
        scratch_shapes=[pltpu.VMEM((TILE_M, D_MODEL), F32)], compiler_params=_params(("arbitrary",)),
    )(pj, pj, pj, ret_raw, x2d, p2d, target, ws, bst, g_ple, g_final, w_ro, w_so, w_o, w_pg, w_pp)


def _retention_bwd_call(pj0, drr, states, consts, n_seq, seq, side):
    cos_full, sin_signed, decay_in, zeta, xi = consts
    nb = seq // RET_TILE
    cpb = RET_TILE // CHUNK

    def body(pj_ref, do_ref, st_ref, cos_ref, sin_ref, d_ref, zeta_ref, xi_ref, dpj_ref, gstate):
        @pl.when(pl.program_id(1) == 0)
        def _():
            gstate[...] = jnp.zeros_like(gstate)

        for cc in reversed(range(cpb)):
            rows = slice(cc * CHUNK, (cc + 1) * CHUNK)
            cs, sn = cos_ref[rows, :], sin_ref[rows, :]
            for h in range(RET_HEADS):
                q, k, v = _qkv(pj_ref, rows, h)
                qt = (q * cs + _rot(q) * sn) * QK_SCALE
                kt = k * cs + _rot(k) * sn
                d_out = do_ref[rows, h * V_DIM:(h + 1) * V_DIM]
                prev = st_ref[cc, h]
                g = gstate[h]
                dec = d_ref[h]
                scores_d = _mm_nt(qt, kt) * dec
                dscores = _mm_nt(d_out, v) * dec
                kz = kt * zeta_ref[h]
                qx = qt * xi_ref[h]
                dv = _mm_tn(scores_d, d_out) + _mm(kz, g)
                dqt = (_mm(dscores, kt) + _mm_nt(d_out, prev) * xi_ref[h]) * QK_SCALE
                dkt = _mm_tn(dscores, qt) + _mm_nt(v, g) * zeta_ref[h]
                gstate[h] = _mm_tn(qx, d_out) + CHUNK_DECAY[h] * g
                dq = dqt * cs + _rot(dqt * sn)
                dk = dkt * cs + _rot(dkt * sn)
                dpj_ref[rows, h * QK_DIM:(h + 1) * QK_DIM] = dq.astype(dpj_ref.dtype)
                dpj_ref[rows, RET_HEADS * QK_DIM + h * QK_DIM:RET_HEADS * QK_DIM + (h + 1) * QK_DIM] = dk.astype(
                    dpj_ref.dtype)
                dpj_ref[rows, 2 * RET_HEADS * QK_DIM + h * V_DIM:2 * RET_HEADS * QK_DIM + (h + 1) * V_DIM] = dv.astype(
                    dpj_ref.dtype)

    row = lambda b, n: (b * nb + nb - 1 - n, 0)
    pos = lambda b, n: (nb - 1 - n, 0)
    return _compute_call(
        body, name="retention_bwd", grid=(n_seq, nb),
        in_specs=[pl.BlockSpec((None, RET_TILE, SHARD_W), lambda b, n: (0,) + row(b, n)),
                  pl.BlockSpec((RET_TILE, D_MODEL), row),
                  pl.BlockSpec((cpb, RET_HEADS, QK_DIM, V_DIM), lambda b, n: (b * nb + nb - 1 - n, 0, 0, 0)),
                  pl.BlockSpec((RET_TILE, QK_DIM), pos), pl.BlockSpec((RET_TILE, QK_DIM), pos),
                  _resident(decay_in.shape), _resident(zeta.shape), _resident(xi.shape)],
        out_specs=[pl.BlockSpec((RET_TILE, SHARD_W), row)],
        out_shape=[jax.ShapeDtypeStruct((n_seq * seq, SHARD_W), MXU_DTYPE)],
        scratch_shapes=[pltpu.VMEM((RET_HEADS, QK_DIM, V_DIM), F32)],
        operands=(pj0, drr, states, cos_full, sin_signed, decay_in, zeta, xi),
        semantics=("arbitrary", "arbitrary"), side=side)


def _dx_call(dpj, x2d, dx1, g_mixer, wg_in, side):
    t = x2d.shape[0]

    def body(d0, d1, d2, d3, x_ref, dx1_ref, g_ref, w_ref, dx_ref, dg_ref):
        @pl.when(pl.program_id(0) == 0)
        def _():
            dg_ref[...] = jnp.zeros_like(dg_ref)

        dh = _mm_nt(d0[...], w_ref[0])
        for j, d_ref in enumerate((d1, d2, d3)):
            dh += _mm_nt(d_ref[...], w_ref[j + 1])
        xn, r = _rms(x_ref[...])
        dg_ref[...] += jnp.sum(dh * xn, axis=0, keepdims=True)
        dx_ref[...] = dx1_ref[...] + _rms_bwd(dh * g_ref[...], xn, r)

    return _compute_call(
        body, name="dx_bwd", grid=(t // DX_TILE,),
        in_specs=[_row_tile(SHARD_W, DX_TILE)] * N_CHIPS + [_row_tile(D_MODEL, DX_TILE)] * 2
        + [_resident((1, D_MODEL)), _resident(wg_in.shape)],
        out_specs=[_row_tile(D_MODEL, DX_TILE), pl.BlockSpec((1, D_MODEL), lambda i: (0, 0))],
        out_shape=[jax.ShapeDtypeStruct((t, D_MODEL), F32), jax.ShapeDtypeStruct((1, D_MODEL), F32)],
        operands=(*dpj, x2d, dx1, g_mixer, wg_in), semantics=("arbitrary",), side=side)


def _wgrad_call(name, lhs, rhs, block_n, out_cols=None, block_t=1024, into=None, slot=0, n_slots=1, side=None):
    t, n = rhs.shape
    k = lhs.shape[1]
    block_t = min(block_t, t)
    steps = t // block_t
    out_cols = block_n if out_cols is None else out_cols
    per = block_n // out_cols
    first_block = slot * (n // block_n)

    def body(l_ref, r_ref, *rest):
        o_ref, acc = rest[-2:]
        @pl.when(pl.program_id(1) == 0)
        def _():
            acc[...] = jnp.zeros_like(acc)

        acc[...] += _mm_tn(l_ref[...], r_ref[...])

        @pl.when(pl.program_id(1) == steps - 1)
        def _():
            for s in range(per):
                o_ref[s] = acc[:, s * out_cols:(s + 1) * out_cols].astype(o_ref.dtype)

    res = _compute_call(
        body, name=name, grid=(n // block_n, steps),
        in_specs=[pl.BlockSpec((block_t, k), lambda j, i: (i, 0)), pl.BlockSpec((block_t, block_n), lambda j, i: (i, j))]
        + ([] if into is None else [ANY]),
        out_specs=[pl.BlockSpec((per, k, out_cols), lambda j, i: (first_block + j, 0, 0))],
        out_shape=[jax.ShapeDtypeStruct((n_slots * (n // out_cols), k, out_cols), COMM_DTYPE)],
        scratch_shapes=[pltpu.VMEM((k, block_n), F32)], aliases={} if into is None else {2: 0},
        operands=(lhs, rhs) if into is None else (lhs, rhs, into), semantics=("arbitrary", "arbitrary"), side=side)
    return res[0] if side is None else (res[0][0], res[1])


def _position():
    return lax.axis_index("x"), lax.axis_index("y"), lax.axis_index("c")


def _position_array():
    x, y, c = _position()
    return jnp.stack([2 * x + y, c]).astype(jnp.int32)


def _other_chip(x, y, k):
    return (1 - x if k & 2 else x), (1 - y if k & 1 else y)


def _plan_side(operands, out_shapes, plan, n_remote, aliases=None):
    def copies(ins, outs, send_sem, recv_sem, base=0):
        remote = plan(ins, outs)
        assert len(remote) == n_remote
        return [pltpu.make_async_remote_copy(src_ref=src, dst_ref=dst, send_sem=send_sem.at[base + i],
                                             recv_sem=recv_sem.at[base + i], device_id=dev, device_id_type=MESH)
                for i, (src, dst, dev) in enumerate(remote)]

    def start(*a):
        for cp in copies(*a):
            cp.start()

    def finish(*a):
        for cp in copies(*a):
            cp.wait()

    return _Side(operands, out_shapes, n_remote, start, finish, aliases)


def _place_cast_call(name, shards, pos):
    n = len(shards)
    rows, width = shards[0].shape
    block_rows = min(256, rows)

    def body(pos_ref, *refs):
        for w_ref, o_ref in zip(refs[:n], refs[n:]):
            o_ref[...] = w_ref[...].astype(o_ref.dtype)

    return pl.pallas_call(
        body, name=name,
        grid_spec=pltpu.PrefetchScalarGridSpec(
            num_scalar_prefetch=1, grid=(rows // block_rows,),
            in_specs=[pl.BlockSpec((block_rows, width), lambda i, pos: (i, 0))] * n,
            out_specs=[pl.BlockSpec((None, block_rows, width), lambda i, pos: (pos[0], i, 0))] * n),
        out_shape=[jax.ShapeDtypeStruct((N_CHIPS, rows, width), MXU_DTYPE)] * n,
        compiler_params=_params(("arbitrary",)),
    )(pos, *shards)


def _gather_side(placed):
    n = len(placed)

    def copies(kind, bufs, send_sem, recv_sem, base):
        x, y, c = _position()
        me = 2 * x + y
        made = []
        for i in range(n):
            hr = placed[i].shape[1] // 2
            for k in (1, 2, 3):
                px, py = _other_chip(x, y, k)
                chip, core, slot, dev = [(me, c, k - 1, (px, py, c)), (2 * px + py, c, 2 + k, (x, y, 1 - c)),
                                         (2 * px + py, 1 - c, 2 + k, (x, y, 1 - c))][kind]
                piece = bufs[i].at[chip, pl.ds(core * hr, hr)]
                made.append(pltpu.make_async_remote_copy(
                    src_ref=piece, dst_ref=piece, send_sem=send_sem.at[base + 6 * i + slot],
                    recv_sem=recv_sem.at[base + 6 * i + slot], device_id=dev, device_id_type=MESH))
        return made

    def start(ins, outs, send_sem, recv_sem, base=0):
        for cp in copies(0, outs, send_sem, recv_sem, base):
            cp.start()

    def finish(ins, outs, send_sem, recv_sem, base=0):
        first, onward = copies(0, outs, send_sem, recv_sem, base), copies(1, outs, send_sem, recv_sem, base)
        for landed, cp in zip(first, onward):
            landed.wait_recv()
            cp.start()
        for cp in copies(2, outs, send_sem, recv_sem, base):
            cp.wait_recv()
        for cp in first + onward:
            cp.wait_send()

    return _Side(placed, [jax.ShapeDtypeStruct(a.shape, a.dtype) for a in placed], 6 * n, start, finish,
                 {i: i for i in range(n)})


def _same_shape(shapes):
    found = {}
    for i, shape in enumerate(shapes):
        found.setdefault(shape, []).append(i)
    return found


def _sum_call(name, pos, groups, rows, width, out_dtype, block_rows, out_rows=None, out_index=None):
    out_rows = rows if out_rows is None else out_rows
    out_index = (lambda i, pos, br: i) if out_index is None else out_index
    sizes = [len(g) for g in groups]

    def body(pos_ref, *refs):
        ins, outs = refs[:sum(sizes)], refs[sum(sizes):]
        at = 0
        for size, out in zip(sizes, outs):
            acc = ins[at][...].astype(F32)
            for r in ins[at + 1:at + size]:
                acc = acc + r[...].astype(F32)
            out[...] = acc.astype(out_dtype)
            at += size

    def spec(index):
        return pl.BlockSpec((block_rows, width), lambda i, pos: (index(i, pos, block_rows), 0))

    return pl.pallas_call(
        body, name=name,
        grid_spec=pltpu.PrefetchScalarGridSpec(
            num_scalar_prefetch=1, grid=(rows // block_rows,),
            in_specs=[spec(index) for g in groups for _, index in g], out_specs=[spec(out_index)] * len(groups)),
        out_shape=[jax.ShapeDtypeStruct((out_rows, width), out_dtype)] * len(groups),
        compiler_params=_params(("arbitrary",)),
    )(pos, *[arr for g in groups for arr, _ in g])


def _pair_stage(tag, grads, pos):
    n = len(grads)
    shapes = [(g.shape[0] // N_CHIPS, g.shape[1]) for g in grads]

    def plan(ins, outs):
        x, y, c = _position()
        remote = []
        for i, (r, _) in enumerate(shapes):
            hr = r // 2
            for j in range(N_CHIPS):
                remote.append((ins[i].at[pl.ds(j * r + (1 - c) * hr, hr)], outs[i].at[pl.ds(j * hr, hr)],
                               (x, y, 1 - c)))
        return remote

    pair = _exchange_call(f"grad_pair_exchange_{tag}", _plan_side(
        grads, [jax.ShapeDtypeStruct((N_CHIPS * (r // 2), w), COMM_DTYPE) for r, w in shapes], plan, N_CHIPS * n))

    def own_half(r):
        def index(s, pos, br):
            per = (r // 2) // br
            return (s // per) * (r // br) + pos[1] * per + s % per
        return index

    sums = [None] * n
    for (r, w), which in _same_shape(shapes).items():
        groups = [[(grads[i], own_half(r)), (pair[i], lambda s, pos, br: s)] for i in which]
        outs = _sum_call(f"grad_pair_sum_{tag}{which[0]}", pos, groups, N_CHIPS * (r // 2), w, COMM_DTYPE,
                         min(256, r // 2))
        for i, out in zip(which, outs):
            sums[i] = out
    return sums


def _chip_side(pair_sums):
    halves = [(p.shape[0] // N_CHIPS, p.shape[1]) for p in pair_sums]

    def plan(ins, outs):
        x, y, c = _position()
        remote = []
        for i, (hr, _) in enumerate(halves):
            for k in (1, 2, 3):
                px, py = _other_chip(x, y, k)
                remote.append((ins[i].at[pl.ds((2 * px + py) * hr, hr)], outs[i].at[pl.ds((k - 1) * hr, hr)],
                               (px, py, c)))
        return remote

    return _plan_side(pair_sums, [jax.ShapeDtypeStruct((3 * hr, w), COMM_DTYPE) for hr, w in halves], plan,
                      3 * len(pair_sums))


def _finish_stage(tag, pair_sums, chip, pos, extra=None):
    n = len(pair_sums)
    halves = [(p.shape[0] // N_CHIPS, p.shape[1]) for p in pair_sums]

    def slab(k, hr):
        return lambda s, pos, br: k * (hr // br) + s

    reduced = [None] * n
    for (hr, w), which in _same_shape(halves).items():
        groups = [[(pair_sums[i], lambda s, pos, br: pos[0] * (hr // br) + s)] + [(chip[i], slab(k, hr)) for k in range(3)]
                  for i in which]
        outs = _sum_call(f"grad_chip_sum_{tag}{which[0]}", pos, groups, hr, w, F32, min(256, hr), out_rows=2 * hr,
                         out_index=lambda s, pos, br: pos[1] * (hr // br) + s)
        for i, out in zip(which, outs):
            reduced[i] = out

    def plan(ins, outs):
        x, y, c = _position()
        remote = []
        for i, (hr, _) in enumerate(halves):
            mine = outs[i].at[pl.ds(c * hr, hr)]
            remote.append((mine, mine, (x, y, 1 - c)))
        return remote

    swap = _plan_side(reduced, [jax.ShapeDtypeStruct((2 * hr, w), F32) for hr, w in halves], plan, n,
                      {i: i for i in range(n)})
    if extra is None:
        return _exchange_call(f"grad_half_exchange_{tag}", swap)
    res = _exchange_call(f"grad_half_exchange_{tag}", _join_sides(swap, extra))
    return res[:n], res[n:]


def _small_gather_side(parts):
    n = len(parts)

    def copies(ins, outs, send_sem, recv_sem, base):
        x, y, c = _position()
        made = []
        for i in range(n):
            mine = outs[i].at[4 * x + 2 * y + c]
            made.append(pltpu.make_async_copy(ins[i], mine, send_sem.at[base + 8 * i + 7]))
            for d in range(1, 8):
                px, py = _other_chip(x, y, d >> 1)
                made.append(pltpu.make_async_remote_copy(
                    src_ref=ins[i], dst_ref=mine, send_sem=send_sem.at[base + 8 * i + d - 1],
                    recv_sem=recv_sem.at[base + 8 * i + d - 1], device_id=(px, py, 1 - c if d & 1 else c),
                    device_id_type=MESH))
        return made

    def start(ins, outs, send_sem, recv_sem, base=0):
        for cp in copies(ins, outs, send_sem, recv_sem, base):
            cp.start()

    def finish(ins, outs, send_sem, recv_sem, base=0):
        for cp in copies(ins, outs, send_sem, recv_sem, base):
            cp.wait()

    return _Side(parts, [jax.ShapeDtypeStruct((8,) + a_.shape, F32) for a_ in parts], 8 * n, start, finish)


def _adamw(w, g, m, v):
    m = ADAM_B1 * m + (1.0 - ADAM_B1) * g
    v = ADAM_B2 * v + (1.0 - ADAM_B2) * (g * g)
    m_hat = m / (1.0 - ADAM_B1 ** ADAM_STEP)
    v_hat = v / (1.0 - ADAM_B2 ** ADAM_STEP)
    delta = -ADAM_LR * (m_hat / (jnp.sqrt(v_hat) + ADAM_EPS) + ADAM_WD * w)
    return delta, m, v


def _adamw_call(name, ws, gs, ms, vs):
    n = len(ws)
    rows, width = ws[0].shape
    block_rows = min(256 // n, rows)

    def body(*refs):
        for i in range(n):
            w_ref, g_ref, m_ref, v_ref = (refs[j * n + i] for j in range(4))
            d_out, m_out, v_out = (refs[(4 + j) * n + i] for j in range(3))
            d_out[...], m_out[...], v_out[...] = _adamw(w_ref[...], g_ref[...], m_ref[...], v_ref[...])

    spec = pl.BlockSpec((block_rows, width), lambda i: (i, 0))
    res = pl.pallas_call(
        body, name=name, grid=(rows // block_rows,), in_specs=[spec] * (4 * n), out_specs=[spec] * (3 * n),
        out_shape=[jax.ShapeDtypeStruct(ws[0].shape, F32)] * (3 * n),
        compiler_params=_params(("arbitrary",)),
    )(*ws, *gs, *ms, *vs)
    return [(res[i], res[n + i], res[2 * n + i]) for i in range(n)]


def _small_adamw_call(gathered, weights, moments_m, moments_v):
    n = len(weights)

    def body(*refs):
        all_refs, refs = refs[:n + 1], refs[n + 1:]
        w_refs, m_refs, v_refs, outs = refs[:n], refs[n:2 * n], refs[2 * n:3 * n], refs[3 * n:]

        def total(ref):
            acc = ref[0]
            for d in range(1, 8):
                acc = acc + ref[d]
            return acc

        outs[0][...] = total(all_refs[n])
        for i in range(n):
            g = total(all_refs[i])
            if i == 0:
                row = lax.broadcasted_iota(jnp.int32, g.shape, 0)
                col = lax.broadcasted_iota(jnp.int32, g.shape, 1)
                g = jnp.where((row % CHUNK) >= col, g, 0.0)
            g_out, d_out, m_out, v_out = outs[1 + 4 * i:5 + 4 * i]
            g_out[...] = g
            d_out[...], m_out[...], v_out[...] = _adamw(w_refs[i][...], g, m_refs[i][...], v_refs[i][...])

    out_shape = [jax.ShapeDtypeStruct(LOSS_TILE, F32)]
    for w in weights:
        out_shape += [jax.ShapeDtypeStruct(w.shape, F32)] * 4
    res = pl.pallas_call(
        body, name="small_adamw", out_shape=out_shape,
        compiler_params=pltpu.CompilerParams(vmem_limit_bytes=VMEM_LIMIT),
    )(*gathered, *weights, *moments_m, *moments_v)
    return res[0], [res[1 + 4 * i:5 + 4 * i] for i in range(n)]


def kernel(x, p, w_in, w_ret_out, w_sgu_out, w_out, sgu_ws, sgu_bs, w_ple_gate, w_ple_proj, g_mixer, g_ple, g_final, loss_target, m_w_in, m_w_ret_out, m_w_sgu_out, m_w_out, m_sgu_ws, m_sgu_bs, m_w_ple_gate, m_w_ple_proj, m_g_mixer, m_g_ple, m_g_final, v_w_in, v_w_ret_out, v_w_sgu_out, v_w_out, v_sgu_ws, v_sgu_bs, v_w_ple_gate, v_w_ple_proj, v_g_mixer, v_g_ple, v_g_final):
    n_seq, seq, _ = x.shape
    t = n_seq * seq
    x2d = x.reshape(t, D_MODEL)
    p2d = p.reshape(t, PLE_DIM)
    target = loss_target.reshape(t, D_MODEL)
    big = [w_in[0], w_ret_out[0], w_sgu_out[0], w_out[0], w_ple_gate[0], w_ple_proj[0]]
    big_m = [m_w_in[0], m_w_ret_out[0], m_w_sgu_out[0], m_w_out[0], m_w_ple_gate[0], m_w_ple_proj[0]]
    big_v = [v_w_in[0], v_w_ret_out[0], v_w_sgu_out[0], v_w_out[0], v_w_ple_gate[0], v_w_ple_proj[0]]

    pos = _position_array()
    placed = (_place_cast_call("place_w_in", big[:1], pos) + _place_cast_call("place_square_weights", big[1:5], pos)
              + _place_cast_call("place_w_ple_proj", big[5:], pos))
    ws = sgu_ws[0]
    bst = sgu_bs[0].T
    consts = _retention_consts(seq)

    (h, pj, wg_in), gathered = _proj_call(x2d, g_mixer, placed[0], pos, _gather_side(placed[1:]))
    pj0 = pj1 = pj2 = pj3 = pj
    w_ro, w_so, w_o, w_pg = (w.reshape(D_MODEL, D_MODEL) for w in gathered[:4])
    w_pp = gathered[4].transpose(1, 0, 2).reshape(PLE_DIM, D_MODEL)
    ret_raw, states = _retention_fwd_call(pj0, consts, n_seq, seq)
    dx1, a, b, ret, sgu, merged, hp, dz, dpp, loss, dg_ple, dg_final = _tail_call(
        pj, ret_raw, x2d, p2d, target, ws, bst, g_ple, g_final.reshape(1, D_MODEL), w_ro, w_so, w_o, w_pg, w_pp)
    dpj1, dpj2, dpj3, drr, da, db, dws, dbst = _merge_bwd_call(dx1, pj1, pj2, pj3, ret_raw, a, b, ws, bst,
                                                               w_ro, w_so, w_o)

    small_shapes = [(SGU_GROUPS * CHUNK, CHUNK), (SGU_GROUPS, CHUNK), (1, D_MODEL), (1, D_MODEL), (1, D_MODEL)]
    as_small = lambda arrays: [None if a_ is None else a_.reshape(s_) for a_, s_ in zip(arrays, small_shapes)]
    early = as_small([dws, dbst.T, None, dg_ple, dg_final])
    rows_of = lambda g: g.reshape(g.shape[0] * g.shape[1], g.shape[2])
    tail_grads = [
        rows_of(_wgrad_call("wgrad_ret_out", ret, da, D_MODEL)),
        rows_of(_wgrad_call("wgrad_sgu_out", sgu, db, D_MODEL)),
        rows_of(_wgrad_call("wgrad_out", merged, dx1, D_MODEL)),
        rows_of(_wgrad_call("wgrad_ple_gate", hp, dz, D_MODEL)),
        rows_of(_wgrad_call("wgrad_ple_proj", p2d, dpp, D_MODEL, out_cols=PLE_DIM)),
    ]
    tail_sums = _pair_stage("tail", tail_grads, pos)
    (dpj0,), carried = _retention_bwd_call(
        pj0, drr, states, consts, n_seq, seq,
        _join_sides(_chip_side(tail_sums), _small_gather_side([early[0], early[1], early[3], early[4], loss])))
    tail_chip, early_all = carried[:len(tail_sums)], carried[len(tail_sums):]
    in_grad = None
    for j, d in enumerate((dpj0, dpj1, dpj2, dpj3)):
        in_grad = _wgrad_call(f"wgrad_in_{j}", h, d, SHARD_W, into=in_grad, slot=j, n_slots=N_CHIPS)
    in_sums = _pair_stage("in", [rows_of(in_grad)], pos)
    (dx, dg_mixer), in_chip = _dx_call((dpj0, dpj1, dpj2, dpj3), x2d, dx1, g_mixer, wg_in, _chip_side(in_sums))
    g_big, (mixer_all,) = _finish_stage("all", in_sums + tail_sums, list(in_chip) + list(tail_chip), pos,
                                        _small_gather_side([dg_mixer]))
    upd = [None] * len(big)
    for _, which in _same_shape([w.shape for w in big]).items():
        pick = lambda arrays: [arrays[i] for i in which]
        for i, triple in zip(which, _adamw_call(f"adamw_{which[0]}", pick(big), pick(g_big), pick(big_m), pick(big_v))):
            upd[i] = triple

    small_g = [early_all[0], early_all[1], mixer_all, early_all[2], early_all[3], early_all[4]]
    total, small = _small_adamw_call(small_g, as_small([sgu_ws, sgu_bs, g_mixer, g_ple, g_final]),
                                     as_small([m_sgu_ws, m_sgu_bs, m_g_mixer, m_g_ple, m_g_final]),
                                     as_small([v_sgu_ws, v_sgu_bs, v_g_mixer, v_g_ple, v_g_final]))
    out_small_shapes = [sgu_ws.shape, sgu_bs.shape, g_mixer.shape, g_ple.shape, g_final.shape]

    def ordered(big_list, kind):
        w_in_, w_ro_, w_so_, w_o_, w_pg_, w_pp_ = [b_[None] for b_ in big_list]
        s_ws, s_bs, s_gm, s_gp, s_gf = [small[i][kind].reshape(s) for i, s in enumerate(out_small_shapes)]
        return [w_in_, w_ro_, w_so_, w_o_, s_ws, s_bs, w_pg_, w_pp_, s_gm, s_gp, s_gf]

    out = [total[0, 0], dx.reshape(x.shape)]
    out += ordered(g_big, 0)
    out += ordered([u[0] for u in upd], 1)
    out += ordered([u[1] for u in upd], 2)
    out += ordered([u[2] for u in upd], 3)
    return tuple(out)
```

```python
import functools
import math

import numpy as np
import jax
import jax.numpy as jnp
from jax import lax
from jax.experimental import pallas as pl
from jax.experimental.pallas import tpu as pltpu

F32 = jnp.float32
MXU_DTYPE = jnp.bfloat16
COMM_DTYPE = jnp.bfloat16

D_MODEL = 1024
RET_HEADS = 4
QK_DIM = 128
V_DIM = 256
CHUNK = 128
SGU_GROUPS = 4
GROUP_DIM = 256
PLE_DIM = 256
N_CHIPS = 4
SHARD_W = 2048
ROPE_BASE = 10000.0
NORM_EPS = 1e-6
GN_EPS = 1e-5
QK_SCALE = QK_DIM ** -0.5
SQRT_HALF = math.sqrt(0.5)
INV_SQRT_2PI = 1.0 / math.sqrt(2.0 * math.pi)

ADAM_LR = 0.001
ADAM_B1 = 0.9
ADAM_B2 = 0.999
ADAM_EPS = 1e-08
ADAM_WD = 0.01
ADAM_STEP = 10

TILE_M = 256
LOSS_TILE = (8, 128)
LOAD_PARTS = 16
PROJ_TILE = 512
RET_TILE = 512
DX_TILE = 512
VMEM_LIMIT = 56 * 1024 * 1024
MESH = pl.DeviceIdType.MESH
ANY = pl.BlockSpec(memory_space=pl.ANY)

CHUNK_DECAY = tuple(
    float(np.exp(np.float32(CHUNK) * np.log(np.float32(1.0 - 2.0 ** (-5.0 - h))))) for h in range(RET_HEADS))


def _mm(a, b):
    return jnp.dot(a.astype(MXU_DTYPE), b.astype(MXU_DTYPE), preferred_element_type=F32)


def _mm_nt(a, b):
    return lax.dot_general(a.astype(MXU_DTYPE), b.astype(MXU_DTYPE), (((1,), (1,)), ((), ())),
                           preferred_element_type=F32)


def _mm_tn(a, b):
    return lax.dot_general(a.astype(MXU_DTYPE), b.astype(MXU_DTYPE), (((0,), (0,)), ((), ())),
                           preferred_element_type=F32)


def _mean(x):
    return jnp.mean(x, axis=-1, keepdims=True)


def _sigmoid(x):
    return jax.nn.sigmoid(x)


def _silu_and_grad(x):
    s = _sigmoid(x)
    silu = x * s
    return silu, s + silu * (1.0 - s)


def _gelu_and_grad(x):
    cdf = 0.5 + 0.5 * lax.erf(x * SQRT_HALF)
    return x * cdf, cdf + x * (jnp.exp(x * x * -0.5) * INV_SQRT_2PI)


def _unit_norm(x, eps):
    xc = x - _mean(x)
    rstd = lax.rsqrt(_mean(xc * xc) + eps)
    return xc * rstd, rstd


def _unit_norm_bwd(dn, n, rstd):
    return rstd * (dn - _mean(dn) - n * _mean(dn * n))


def _rms(x):
    r = lax.rsqrt(_mean(x * x) + NORM_EPS)
    return x * r, r


def _rms_bwd(dxn, xn, r):
    return r * (dxn - xn * _mean(dxn * xn))


def _rot(x):
    return pltpu.roll(x, QK_DIM // 2, 1)


def _params(semantics, **kw):
    return pltpu.CompilerParams(dimension_semantics=semantics, vmem_limit_bytes=VMEM_LIMIT, **kw)


def _row_tile(width, tm=TILE_M):
    return pl.BlockSpec((tm, width), lambda i: (i, 0))


def _proj_tile(j):
    return pl.BlockSpec((None, TILE_M, SHARD_W), lambda i: (j, i, 0))


def _resident(shape):
    nd = len(shape)
    return pl.BlockSpec(shape, lambda *_: (0,) * nd, pipeline_mode=pl.Buffered(1))


def _causal_ws(ws_ref):
    row = lax.broadcasted_iota(jnp.int32, (CHUNK, CHUNK), 0)
    col = lax.broadcasted_iota(jnp.int32, (CHUNK, CHUNK), 1)
    return [jnp.where(row >= col, ws_ref[g], 0.0).astype(MXU_DTYPE) for g in range(SGU_GROUPS)]


def _heads(x, width):
    return [x[:, h * width:(h + 1) * width] for h in range(x.shape[1] // width)]


def _branch_forward(pj1, pj2, ret_raw, wsc, bst, mixed_ref):
    rg, su = pj1[:, :D_MODEL], pj1[:, D_MODEL:]
    sv, sg = pj2[:, :D_MODEL], pj2[:, D_MODEL:]
    rn_parts, rstd_parts = zip(*[_unit_norm(r, GN_EPS) for r in _heads(ret_raw, V_DIM)])
    rn = jnp.concatenate(rn_parts, axis=1)
    sil_rg, dsil_rg = _silu_and_grad(rg)
    ret = rn * sil_rg
    u, du = _gelu_and_grad(su)
    gelu_sv, dgelu_sv = _gelu_and_grad(sv)
    vn, rstd_v = _unit_norm(gelu_sv, GN_EPS)
    for cc in range(pj1.shape[0] // CHUNK):
        for g in range(SGU_GROUPS):
            rs, cs = slice(cc * CHUNK, (cc + 1) * CHUNK), slice(g * GROUP_DIM, (g + 1) * GROUP_DIM)
            mixed_ref[rs, cs] = _mm(wsc[g], vn[rs, cs]) + bst[:, g:g + 1]
    mixed = mixed_ref[...]
    sil_sg, dsil_sg = _silu_and_grad(sg)
    sgu = u * mixed * sil_sg
    return dict(rn=rn, rstd_r=rstd_parts, sil_rg=sil_rg, dsil_rg=dsil_rg, ret=ret, u=u, du=du, dgelu_sv=dgelu_sv,
                vn=vn, rstd_v=rstd_v, mixed=mixed, sil_sg=sil_sg, dsil_sg=dsil_sg, sgu=sgu)


class _Side:
    def __init__(self, operands, out_shapes, n_sems, start, finish, aliases=None):
        self.operands, self.out_shapes, self.n_sems = list(operands), list(out_shapes), n_sems
        self.start, self.finish, self.aliases = start, finish, dict(aliases or {})


def _join_sides(*sides):
    spans, a, b, s = [], 0, 0, 0
    for side in sides:
        spans.append((a, b, s))
        a, b, s = a + len(side.operands), b + len(side.out_shapes), s + side.n_sems

    def run(which):
        def go(ins, outs, send_sem, recv_sem, base=0):
            for side, (a0, b0, s0) in zip(sides, spans):
                getattr(side, which)(ins[a0:a0 + len(side.operands)], outs[b0:b0 + len(side.out_shapes)],
                                     send_sem, recv_sem, base + s0)
        return go

    aliases = {a0 + i: b0 + o for side, (a0, b0, _) in zip(sides, spans) for i, o in side.aliases.items()}
    return _Side([x for side in sides for x in side.operands], [x for side in sides for x in side.out_shapes], s,
                 run("start"), run("finish"), aliases)


def _compute_call(body, *, name, grid, in_specs, out_specs, out_shape, operands, semantics, scratch_shapes=(),
                  side=None, prefetch=None, aliases=None, side_start=None):
    n_pre = 0 if prefetch is None else 1
    pre = () if prefetch is None else (prefetch,)

    def spec(in_specs, out_specs, scratch):
        return pltpu.PrefetchScalarGridSpec(num_scalar_prefetch=n_pre, grid=grid, in_specs=in_specs,
                                            out_specs=out_specs, scratch_shapes=scratch)

    if side is None:
        return pl.pallas_call(body, name=name, grid_spec=spec(in_specs, out_specs, list(scratch_shapes)),
                              out_shape=out_shape,
                              input_output_aliases={n_pre + a: b for a, b in (aliases or {}).items()},
                              compiler_params=_params(semantics))(*pre, *operands)
    n_in, n_out, s_in, s_out = len(operands), len(out_shape), len(side.operands), len(side.out_shapes)

    def carrier(*refs):
        pre_refs, refs = refs[:n_pre], refs[n_pre:]
        ins, refs = refs[:n_in], refs[n_in:]
        side_ins, refs = refs[:s_in], refs[s_in:]
        outs, refs = refs[:n_out], refs[n_out:]
        side_outs, refs = refs[:s_out], refs[s_out:]
        scratch, (send_sem, recv_sem) = refs[:-2], refs[-2:]
        ids = [pl.program_id(a) for a in range(len(grid))]
        at = (0,) * len(grid) if side_start is None else side_start
        first = functools.reduce(jnp.logical_and, [i == a for i, a in zip(ids, at)])
        last = functools.reduce(jnp.logical_and, [i == g - 1 for i, g in zip(ids, grid)])

        @pl.when(first)
        def _():
            side.start(side_ins, side_outs, send_sem, recv_sem)

        body(*pre_refs, *ins, *outs, *scratch)

        @pl.when(last)
        def _():
            side.finish(side_ins, side_outs, send_sem, recv_sem)

    all_aliases = {n_pre + a: b for a, b in (aliases or {}).items()}
    all_aliases.update({n_pre + n_in + a: n_out + b for a, b in side.aliases.items()})
    res = pl.pallas_call(
        carrier, name=name,
        grid_spec=spec(list(in_specs) + [ANY] * s_in, list(out_specs) + [ANY] * s_out,
                       list(scratch_shapes) + [pltpu.SemaphoreType.DMA((side.n_sems,))] * 2),
        out_shape=list(out_shape) + side.out_shapes, input_output_aliases=all_aliases,
        compiler_params=_params(semantics, has_side_effects=True),
    )(*pre, *operands, *side.operands)
    return res[:n_out], res[n_out:]


def _exchange_call(name, side):
    s_in = len(side.operands)

    def body(*refs):
        ins, outs = refs[:s_in], refs[s_in:s_in + len(side.out_shapes)]
        send_sem, recv_sem = refs[s_in + len(side.out_shapes):]
        side.start(ins, outs, send_sem, recv_sem)
        side.finish(ins, outs, send_sem, recv_sem)

    return pl.pallas_call(
        body, name=name, in_specs=[ANY] * s_in, out_specs=[ANY] * len(side.out_shapes), out_shape=side.out_shapes,
        scratch_shapes=[pltpu.SemaphoreType.DMA((side.n_sems,))] * 2, input_output_aliases=side.aliases,
        compiler_params=pltpu.CompilerParams(has_side_effects=True),
    )(*side.operands)


def _proj_call(x2d, g_mixer, placed_in, pos, side):
    t = x2d.shape[0]
    nt = t // PROJ_TILE
    hr = placed_in.shape[1] // 2

    def body(pos_ref, x_ref, g_ref, win_ref, h_ref, pj_ref, w_ref, w_vmem, h_all, loc_sem, send_sem, recv_sem):
        k, i = pl.program_id(0), pl.program_id(1)
        x, y, c = _position()
        me = 2 * x + y

        def copy(slot, chip, core, dev):
            piece = w_ref.at[chip, pl.ds(core * hr, hr)]
            return pltpu.make_async_remote_copy(src_ref=piece, dst_ref=piece, send_sem=send_sem.at[slot],
                                                recv_sem=recv_sem.at[slot], device_id=dev, device_id_type=MESH)

        def first_hop(kk):
            px, py = _other_chip(x, y, kk)
            return copy(kk - 1, me, c, (px, py, c))

        def relay():
            source = jnp.bitwise_xor(me, 2 - c)
            return copy(2, source, c, (jnp.bitwise_xor(x, c), jnp.bitwise_xor(y, 1 - c), c))

        def onward(kk, core):
            px, py = _other_chip(x, y, kk)
            return copy(2 + kk, 2 * px + py, core, (x, y, 1 - c))

        def loads(kk):
            px, py = _other_chip(x, y, kk)
            rows = w_vmem.shape[1] // LOAD_PARTS
            return [pltpu.make_async_copy(w_ref.at[2 * px + py, pl.ds(r * rows, rows)],
                                          w_vmem.at[kk % 2, pl.ds(r * rows, rows)], loc_sem.at[r])
                    for r in range(LOAD_PARTS)]

        @pl.when(jnp.logical_and(k == 0, i == 0))
        def _():
            for kk in (1, 2):
                first_hop(kk).start()
            for cp in loads(0):
                cp.start()
            for cp in loads(0):
                cp.wait()

        for kk in (1, 2, 3):
            @pl.when(jnp.logical_and(k == kk - 1, i == nt - 1))
            def _(kk=kk):
                if kk == 1:
                    first_hop(1).wait_recv()
                    first_hop(2).wait_recv()
                    relay().start()
                if kk == 3:
                    relay().wait_recv()
                onward(kk, c).start()
                onward(kk, 1 - c).wait_recv()
                for cp in loads(kk):
                    cp.start()

            @pl.when(jnp.logical_and(k == kk, i == 0))
            def _(kk=kk):
                for cp in loads(kk):
                    cp.wait()

        rows = pl.ds(pl.multiple_of(i * PROJ_TILE, PROJ_TILE), PROJ_TILE)

        @pl.when(k == 0)
        def _():
            xn, _ = _rms(x_ref[...])
            h = (xn * g_ref[...]).astype(MXU_DTYPE)
            h_all[rows, :] = h
            h_ref[...] = h

        pj_ref[...] = jnp.dot(h_all[rows, :], w_vmem[k % 2], preferred_element_type=F32)

        @pl.when(jnp.logical_and(k == N_CHIPS - 1, i == nt - 1))
        def _():
            for cp in [first_hop(1), first_hop(2), relay()] + [onward(kk, c) for kk in (1, 2, 3)]:
                cp.wait_send()

    parked = lambda k, i, pos: (jnp.where(k == 0, i, nt - 1), 0)
    return _compute_call(
        body, name="proj_fwd", grid=(N_CHIPS, nt),
        in_specs=[pl.BlockSpec((PROJ_TILE, D_MODEL), parked), pl.BlockSpec((1, D_MODEL), lambda k, i, pos: (0, 0)), ANY],
        out_specs=[pl.BlockSpec((PROJ_TILE, D_MODEL), lambda k, i, pos: (jnp.where(k == 0, i, nt), 0)),
                   pl.BlockSpec((None, PROJ_TILE, SHARD_W), lambda k, i, pos: (jnp.bitwise_xor(pos[0], k), i, 0)),
                   ANY],
        out_shape=[jax.ShapeDtypeStruct((t + PROJ_TILE, D_MODEL), MXU_DTYPE),
                   jax.ShapeDtypeStruct((N_CHIPS, t, SHARD_W), F32),
                   jax.ShapeDtypeStruct(placed_in.shape, placed_in.dtype)],
        scratch_shapes=[pltpu.VMEM((2,) + placed_in.shape[1:], placed_in.dtype), pltpu.VMEM((t, D_MODEL), MXU_DTYPE),
                        pltpu.SemaphoreType.DMA((LOAD_PARTS,)), pltpu.SemaphoreType.DMA((6,)),
                        pltpu.SemaphoreType.DMA((6,))],
        operands=(x2d, g_mixer, placed_in), semantics=("arbitrary", "arbitrary"), side=side, prefetch=pos,
        aliases={2: 2}, side_start=(2, 0))


def _retention_consts(seq):
    half = QK_DIM // 2
    inv = ROPE_BASE ** (-jnp.arange(half, dtype=F32) / half)
    ang = jnp.arange(seq, dtype=F32)[:, None] * inv[None, :]
    cos, sin = jnp.cos(ang), jnp.sin(ang)
    cos_full = jnp.concatenate([cos, cos], axis=1)
    sin_signed = jnp.concatenate([-sin, sin], axis=1)
    log_g = jnp.log(1.0 - 2.0 ** (-5.0 - jnp.arange(RET_HEADS, dtype=F32)))
    idx = jnp.arange(CHUNK, dtype=F32)
    diff = idx[:, None] - idx[None, :]
    decay_in = jnp.where(diff[None] >= 0, jnp.exp(jnp.maximum(diff, 0.0)[None] * log_g[:, None, None]), 0.0)
    zeta = jnp.exp((CHUNK - 1.0 - idx)[None, :] * log_g[:, None])
    xi = jnp.exp((idx + 1.0)[None, :] * log_g[:, None])
    zeta = jnp.broadcast_to(zeta[:, :, None], (RET_HEADS, CHUNK, QK_DIM))
    xi = jnp.broadcast_to(xi[:, :, None], (RET_HEADS, CHUNK, QK_DIM))
    return cos_full, sin_signed, decay_in, zeta, xi


def _qkv(pj_ref, rows, h):
    q = pj_ref[rows, h * QK_DIM:(h + 1) * QK_DIM]
    k = pj_ref[rows, RET_HEADS * QK_DIM + h * QK_DIM:RET_HEADS * QK_DIM + (h + 1) * QK_DIM]
    v = pj_ref[rows, 2 * RET_HEADS * QK_DIM + h * V_DIM:2 * RET_HEADS * QK_DIM + (h + 1) * V_DIM]
    return q, k, v


def _retention_fwd_call(pj0, consts, n_seq, seq):
    cos_full, sin_signed, decay_in, zeta, xi = consts
    nb = seq // RET_TILE
    cpb = RET_TILE // CHUNK

    def body(pj_ref, cos_ref, sin_ref, d_ref, zeta_ref, xi_ref, o_ref, st_ref, state):
        @pl.when(pl.program_id(1) == 0)
        def _():
            state[...] = jnp.zeros_like(state)

        for cc in range(cpb):
            rows = slice(cc * CHUNK, (cc + 1) * CHUNK)
            cs, sn = cos_ref[rows, :], sin_ref[rows, :]
            for h in range(RET_HEADS):
                q, k, v = _qkv(pj_ref, rows, h)
                qt = (q * cs + _rot(q) * sn) * QK_SCALE
                kt = k * cs + _rot(k) * sn
                prev = state[h]
                st_ref[cc, h] = prev.astype(st_ref.dtype)
                scores = _mm_nt(qt, kt) * d_ref[h]
                o_ref[rows, h * V_DIM:(h + 1) * V_DIM] = _mm(scores, v) + _mm(qt * xi_ref[h], prev)
                state[h] = _mm_tn(kt * zeta_ref[h], v) + CHUNK_DECAY[h] * prev

    row = lambda b, n: (b * nb + n, 0)
    pos = lambda b, n: (n, 0)
    return pl.pallas_call(
        body, name="retention_fwd", grid=(n_seq, nb),
        in_specs=[pl.BlockSpec((None, RET_TILE, SHARD_W), lambda b, n: (0,) + row(b, n)),
                  pl.BlockSpec((RET_TILE, QK_DIM), pos),
                  pl.BlockSpec((RET_TILE, QK_DIM), pos), _resident(decay_in.shape), _resident(zeta.shape),
                  _resident(xi.shape)],
        out_specs=[pl.BlockSpec((RET_TILE, D_MODEL), row),
                   pl.BlockSpec((cpb, RET_HEADS, QK_DIM, V_DIM), lambda b, n: (b * nb + n, 0, 0, 0))],
        out_shape=[jax.ShapeDtypeStruct((n_seq * seq, D_MODEL), F32),
                   jax.ShapeDtypeStruct((n_seq * seq // CHUNK, RET_HEADS, QK_DIM, V_DIM), MXU_DTYPE)],
        scratch_shapes=[pltpu.VMEM((RET_HEADS, QK_DIM, V_DIM), F32)],
        compiler_params=_params(("arbitrary", "arbitrary")),
    )(pj0, cos_full, sin_signed, decay_in, zeta, xi)


def _merge_bwd_call(wide, pj1, pj2, pj3, ret_raw, ws, bst, w_ro, w_so, w_o):
    t = wide.shape[1]

    def body(wide_ref, pj1_ref, pj2_ref, pj3_ref, rr_ref, ws_ref, bst_ref, wro_ref, wso_ref, wo_ref,
             dpj1_ref, dpj2_ref, dpj3_ref, drr_ref, da_ref, db_ref, dws_ref, dbst_ref, mixed_ref):
        @pl.when(pl.program_id(0) == 0)
        def _():
            dws_ref[...] = jnp.zeros_like(dws_ref)
            dbst_ref[...] = jnp.zeros_like(dbst_ref)

        wsc = _causal_ws(ws_ref)
        f = _branch_forward(pj1_ref[...], pj2_ref[...], rr_ref[...], wsc, bst_ref[...], mixed_ref)
        pj3 = pj3_ref[...]
        smr, sms = _sigmoid(pj3[:, :D_MODEL]), _sigmoid(pj3[:, D_MODEL:])
        dmerged = _mm_nt(wide_ref[0], wo_ref[...])
        da_f, db_f = dmerged * smr, dmerged * sms
        da, db = da_f.astype(MXU_DTYPE), db_f.astype(MXU_DTYPE)
        dpj3_ref[:, :D_MODEL] = (da_f * wide_ref[1] * (1.0 - smr)).astype(dpj3_ref.dtype)
        dpj3_ref[:, D_MODEL:] = (db_f * wide_ref[2] * (1.0 - sms)).astype(dpj3_ref.dtype)
        da_ref[...] = da
        db_ref[...] = db

        dret = _mm_nt(da, wro_ref[...])
        dpj1_ref[:, :D_MODEL] = (dret * f["rn"] * f["dsil_rg"]).astype(dpj1_ref.dtype)
        drn = dret * f["sil_rg"]
        for h in range(RET_HEADS):
            cols = slice(h * V_DIM, (h + 1) * V_DIM)
            drr_ref[:, cols] = _unit_norm_bwd(drn[:, cols], f["rn"][:, cols], f["rstd_r"][h]).astype(drr_ref.dtype)

        dsgu = _mm_nt(db, wso_ref[...])
        dpj2_ref[:, D_MODEL:] = (dsgu * f["u"] * f["mixed"] * f["dsil_sg"]).astype(dpj2_ref.dtype)
        tg = dsgu * f["sil_sg"]
        dpj1_ref[:, D_MODEL:] = (tg * f["mixed"] * f["du"]).astype(dpj1_ref.dtype)
        dmixed = tg * f["u"]
        for cc in range(TILE_M // CHUNK):
            for g in range(SGU_GROUPS):
                rs, cs = slice(cc * CHUNK, (cc + 1) * CHUNK), slice(g * GROUP_DIM, (g + 1) * GROUP_DIM)
                dm = dmixed[rs, cs]
                mixed_ref[rs, cs] = _mm_tn(wsc[g], dm)
                dws_ref[g] += _mm_nt(dm, f["vn"][rs, cs])
                dbst_ref[:, g:g + 1] += jnp.sum(dm, axis=1, keepdims=True)
        dvv = _unit_norm_bwd(mixed_ref[...], f["vn"], f["rstd_v"])
        dpj2_ref[:, :D_MODEL] = (dvv * f["dgelu_sv"]).astype(dpj2_ref.dtype)

    sq = (D_MODEL, D_MODEL)
    return pl.pallas_call(
        body, name="merge_bwd", grid=(t // TILE_M,),
        in_specs=[pl.BlockSpec((3, TILE_M, D_MODEL), lambda i: (0, i, 0))] + [_proj_tile(j) for j in (1, 2, 3)]
        + [_row_tile(D_MODEL)] + [_resident(ws.shape), _resident(bst.shape), _resident(sq), _resident(sq), _resident(sq)],
        out_specs=[_row_tile(SHARD_W)] * 3 + [_row_tile(D_MODEL)] * 3
        + [pl.BlockSpec(ws.shape, lambda i: (0, 0, 0)), pl.BlockSpec(bst.shape, lambda i: (0, 0))],
        out_shape=[jax.ShapeDtypeStruct((t, SHARD_W), MXU_DTYPE)] * 3 + [jax.ShapeDtypeStruct((t, D_MODEL), MXU_DTYPE)] * 3
        + [jax.ShapeDtypeStruct(ws.shape, F32), jax.ShapeDtypeStruct(bst.shape, F32)],
        scratch_shapes=[pltpu.VMEM((TILE_M, D_MODEL), F32)], compiler_params=_params(("arbitrary",)),
    )(wide, pj1, pj2, pj3, ret_raw, ws, bst, w_ro, w_so, w_o)


def _tail_call(pj, ret_raw, x2d, p2d, target, ws, bst, g_ple, g_final, w_ro, w_so, w_o, w_pg, w_pp):
    t = x2d.shape[0]
    tm = TILE_M

    def body(pj1_ref, pj2_ref, pj3_ref, rr_ref, x_ref, p_ref, t_ref, ws_ref, bst_ref, gp_ref, gf_ref,
             wro_ref, wso_ref, wo_ref, wpg_ref, wpp_ref,
             wide_ref, half_ref, loss_ref, dgp_ref, dgf_ref, mixed_ref):
        dx1_ref, a_ref, b_ref = (wide_ref.at[k] for k in range(3))
        ret_ref, sgu_ref, mg_ref, hp_ref, dz_ref, dpp_ref = (half_ref.at[k] for k in range(6))
        @pl.when(pl.program_id(0) == 0)
        def _():
            for acc in (loss_ref, dgp_ref, dgf_ref):
                acc[...] = jnp.zeros_like(acc)

        f = _branch_forward(pj1_ref[...], pj2_ref[...], rr_ref[...], _causal_ws(ws_ref), bst_ref[...], mixed_ref)
        ret = f["ret"].astype(MXU_DTYPE)
        sgu = f["sgu"].astype(MXU_DTYPE)
        ret_ref[...] = ret
        sgu_ref[...] = sgu
        a = jnp.dot(ret, wro_ref[...], preferred_element_type=F32)
        b = jnp.dot(sgu, wso_ref[...], preferred_element_type=F32)
        a_ref[...] = a
        b_ref[...] = b
        pj3 = pj3_ref[...]
        smr, sms = _sigmoid(pj3[:, :D_MODEL]), _sigmoid(pj3[:, D_MODEL:])
        merged = (smr * a + sms * b).astype(MXU_DTYPE)
        mg_ref[...] = merged
        x1v = x_ref[...] + jnp.dot(merged, wo_ref[...], preferred_element_type=F32)
        xn1, r1 = _rms(x1v)
        hp = (xn1 * gp_ref[...]).astype(MXU_DTYPE)
        hp_ref[...] = hp
        gate = _sigmoid(jnp.dot(hp, wpg_ref[...], preferred_element_type=F32))
        pp = jnp.dot(p_ref[...].astype(MXU_DTYPE), wpp_ref[...], preferred_element_type=F32)
        xn2, r2 = _rms(x1v + gate * pp)
        err = xn2 * gf_ref[...] - t_ref[...]
        loss_ref[...] += (0.5 / D_MODEL) * jnp.sum(jnp.sum(err * err, axis=1, keepdims=True), axis=0, keepdims=True)

        dy = err * (1.0 / D_MODEL)
        dgf_ref[...] += jnp.sum(dy * xn2, axis=0, keepdims=True)
        dx2 = _rms_bwd(dy * gf_ref[...], xn2, r2)
        dpp_ref[...] = (dx2 * gate).astype(MXU_DTYPE)
        dz = (dx2 * pp * gate * (1.0 - gate)).astype(MXU_DTYPE)
        dz_ref[...] = dz
        dhp = _mm_nt(dz, wpg_ref[...])
        dgp_ref[...] += jnp.sum(dhp * xn1, axis=0, keepdims=True)
        dx1_ref[...] = dx2 + _rms_bwd(dhp * gp_ref[...], xn1, r1)

    sq = (D_MODEL, D_MODEL)
    vec = _resident((1, D_MODEL))
    whole = lambda shape: pl.BlockSpec(shape, lambda i: (0,) * len(shape))
    return pl.pallas_call(
        body, name="tail_fwd", grid=(t // tm,),
        in_specs=[_proj_tile(j) for j in (1, 2, 3)]
        + [_row_tile(D_MODEL), _row_tile(D_MODEL), _row_tile(PLE_DIM), _row_tile(D_MODEL), _resident(ws.shape),
           _resident(bst.shape), vec, vec, _resident(sq), _resident(sq), _resident(sq), _resident(sq),
           _resident((PLE_DIM, D_MODEL))],
        out_specs=[pl.BlockSpec((3, tm, D_MODEL), lambda i: (0, i, 0)), pl.BlockSpec((6, tm, D_MODEL), lambda i: (0, i, 0)),
                   whole(LOSS_TILE), whole((1, D_MODEL)), whole((1, D_MODEL))],
        out_shape=[jax.ShapeDtypeStruct((3, t, D_MODEL), F32), jax.ShapeDtypeStruct((6, t, D_MODEL), MXU_DTYPE),
                   jax.ShapeDtypeStruct(LOSS_TILE, F32), jax.ShapeDtypeStruct((1, D_MODEL), F32),
           jax.ShapeDtypeStruct((1, D_MODEL), F32)],
        scratch_shapes=[pltpu.VMEM((TILE_M, D_MODEL), F32)], compiler_params=_params(("arbitrary",)),
    )(pj, pj, pj, ret_raw, x2d, p2d, target, ws, bst, g_ple, g_final, w_ro, w_so, w_o, w_pg, w_pp)


def _retention_bwd_call(pj0, drr, states, consts, n_seq, seq, side):
    cos_full, sin_signed, decay_in, zeta, xi = consts
    nb = seq // RET_TILE
    cpb = RET_TILE // CHUNK

    def body(pj_ref, do_ref, st_ref, cos_ref, sin_ref, d_ref, zeta_ref, xi_ref, dpj_ref, gstate):
        @pl.when(pl.program_id(1) == 0)
        def _():
            gstate[...] = jnp.zeros_like(gstate)

        for cc in reversed(range(cpb)):
            rows = slice(cc * CHUNK, (cc + 1) * CHUNK)
            cs, sn = cos_ref[rows, :], sin_ref[rows, :]
            for h in range(RET_HEADS):
                q, k, v = _qkv(pj_ref, rows, h)
                qt = (q * cs + _rot(q) * sn) * QK_SCALE
                kt = k * cs + _rot(k) * sn
                d_out = do_ref[rows, h * V_DIM:(h + 1) * V_DIM]
                prev = st_ref[cc, h]
                g = gstate[h]
                dec = d_ref[h]
                scores_d = _mm_nt(qt, kt) * dec
                dscores = _mm_nt(d_out, v) * dec
                kz = kt * zeta_ref[h]
                qx = qt * xi_ref[h]
                dv = _mm_tn(scores_d, d_out) + _mm(kz, g)
                dqt = (_mm(dscores, kt) + _mm_nt(d_out, prev) * xi_ref[h]) * QK_SCALE
                dkt = _mm_tn(dscores, qt) + _mm_nt(v, g) * zeta_ref[h]
                gstate[h] = _mm_tn(qx, d_out) + CHUNK_DECAY[h] * g
                dq = dqt * cs + _rot(dqt * sn)
                dk = dkt * cs + _rot(dkt * sn)
                dpj_ref[rows, h * QK_DIM:(h + 1) * QK_DIM] = dq.astype(dpj_ref.dtype)
                dpj_ref[rows, RET_HEADS * QK_DIM + h * QK_DIM:RET_HEADS * QK_DIM + (h + 1) * QK_DIM] = dk.astype(
                    dpj_ref.dtype)
                dpj_ref[rows, 2 * RET_HEADS * QK_DIM + h * V_DIM:2 * RET_HEADS * QK_DIM + (h + 1) * V_DIM] = dv.astype(
                    dpj_ref.dtype)

    row = lambda b, n: (b * nb + nb - 1 - n, 0)
    pos = lambda b, n: (nb - 1 - n, 0)
    return _compute_call(
        body, name="retention_bwd", grid=(n_seq, nb),
        in_specs=[pl.BlockSpec((None, RET_TILE, SHARD_W), lambda b, n: (0,) + row(b, n)),
                  pl.BlockSpec((RET_TILE, D_MODEL), row),
                  pl.BlockSpec((cpb, RET_HEADS, QK_DIM, V_DIM), lambda b, n: (b * nb + nb - 1 - n, 0, 0, 0)),
                  pl.BlockSpec((RET_TILE, QK_DIM), pos), pl.BlockSpec((RET_TILE, QK_DIM), pos),
                  _resident(decay_in.shape), _resident(zeta.shape), _resident(xi.shape)],
        out_specs=[pl.BlockSpec((RET_TILE, SHARD_W), row)],
        out_shape=[jax.ShapeDtypeStruct((n_seq * seq, SHARD_W), MXU_DTYPE)],
        scratch_shapes=[pltpu.VMEM((RET_HEADS, QK_DIM, V_DIM), F32)],
        operands=(pj0, drr, states, cos_full, sin_signed, decay_in, zeta, xi),
        semantics=("arbitrary", "arbitrary"), side=side)


def _dx_call(dpj, x2d, wide, g_mixer, wg_in, side):
    t = x2d.shape[0]

    def body(d0, d1, d2, d3, x_ref, dx1_ref, g_ref, w_ref, dx_ref, dg_ref):
        @pl.when(pl.program_id(0) == 0)
        def _():
            dg_ref[...] = jnp.zeros_like(dg_ref)

        dh = _mm_nt(d0[...], w_ref[0])
        for j, d_ref in enumerate((d1, d2, d3)):
            dh += _mm_nt(d_ref[...], w_ref[j + 1])
        xn, r = _rms(x_ref[...])
        dg_ref[...] += jnp.sum(dh * xn, axis=0, keepdims=True)
        dx_ref[...] = dx1_ref[...] + _rms_bwd(dh * g_ref[...], xn, r)

    return _compute_call(
        body, name="dx_bwd", grid=(t // DX_TILE,),
        in_specs=[_row_tile(SHARD_W, DX_TILE)] * N_CHIPS
        + [_row_tile(D_MODEL, DX_TILE), pl.BlockSpec((None, DX_TILE, D_MODEL), lambda i: (0, i, 0))]
        + [_resident((1, D_MODEL)), _resident(wg_in.shape)],
        out_specs=[_row_tile(D_MODEL, DX_TILE), pl.BlockSpec((1, D_MODEL), lambda i: (0, 0))],
        out_shape=[jax.ShapeDtypeStruct((t, D_MODEL), F32), jax.ShapeDtypeStruct((1, D_MODEL), F32)],
        operands=(*dpj, x2d, wide, g_mixer, wg_in), semantics=("arbitrary",), side=side)


def _wgrad_call(name, lhs, rhs, block_n, out_cols=None, block_t=1024, into=None, slot=0, n_slots=1, side=None):
    (lhs, lhs_at), (rhs, rhs_at) = [x if isinstance(x, tuple) else (x, None) for x in (lhs, rhs)]
    t, n = rhs.shape[-2:]
    k = lhs.shape[-1]
    block_t = min(block_t, t)

    def operand_spec(at, block, index):
        if at is None:
            return pl.BlockSpec(block, index)
        return pl.BlockSpec((None,) + block, lambda j, i: (at,) + index(j, i))

    steps = t // block_t
    out_cols = block_n if out_cols is None else out_cols
    per = block_n // out_cols
    first_block = slot * (n // block_n)

    def body(l_ref, r_ref, *rest):
        o_ref, acc = rest[-2:]
        @pl.when(pl.program_id(1) == 0)
        def _():
            acc[...] = jnp.zeros_like(acc)

        acc[...] += _mm_tn(l_ref[...], r_ref[...])

        @pl.when(pl.program_id(1) == steps - 1)
        def _():
            for s in range(per):
                o_ref[s] = acc[:, s * out_cols:(s + 1) * out_cols].astype(o_ref.dtype)

    res = _compute_call(
        body, name=name, grid=(n // block_n, steps),
        in_specs=[operand_spec(lhs_at, (block_t, k), lambda j, i: (i, 0)),
                  operand_spec(rhs_at, (block_t, block_n), lambda j, i: (i, j))] + ([] if into is None else [ANY]),
        out_specs=[pl.BlockSpec((per, k, out_cols), lambda j, i: (first_block + j, 0, 0))],
        out_shape=[jax.ShapeDtypeStruct((n_slots * (n // out_cols), k, out_cols), COMM_DTYPE)],
        scratch_shapes=[pltpu.VMEM((k, block_n), F32)], aliases={} if into is None else {2: 0},
        operands=(lhs, rhs) if into is None else (lhs, rhs, into), semantics=("arbitrary", "arbitrary"), side=side)
    return res[0] if side is None else (res[0][0], res[1])


def _position():
    return lax.axis_index("x"), lax.axis_index("y"), lax.axis_index("c")


def _position_array():
    x, y, c = _position()
    return jnp.stack([2 * x + y, c]).astype(jnp.int32)


def _other_chip(x, y, k):
    return (1 - x if k & 2 else x), (1 - y if k & 1 else y)


def _plan_side(operands, out_shapes, plan, n_remote, aliases=None):
    def copies(ins, outs, send_sem, recv_sem, base=0):
        remote = plan(ins, outs)
        assert len(remote) == n_remote
        return [pltpu.make_async_remote_copy(src_ref=src, dst_ref=dst, send_sem=send_sem.at[base + i],
                                             recv_sem=recv_sem.at[base + i], device_id=dev, device_id_type=MESH)
                for i, (src, dst, dev) in enumerate(remote)]

    def start(*a):
        for cp in copies(*a):
            cp.start()

    def finish(*a):
        for cp in copies(*a):
            cp.wait()

    return _Side(operands, out_shapes, n_remote, start, finish, aliases)


def _place_cast_call(name, shards, pos):
    n = len(shards)
    rows, width = shards[0].shape
    block_rows = min(256, rows)

    def body(pos_ref, *refs):
        for w_ref, o_ref in zip(refs[:n], refs[n:]):
            o_ref[...] = w_ref[...].astype(o_ref.dtype)

    return pl.pallas_call(
        body, name=name,
        grid_spec=pltpu.PrefetchScalarGridSpec(
            num_scalar_prefetch=1, grid=(rows // block_rows,),
            in_specs=[pl.BlockSpec((block_rows, width), lambda i, pos: (i, 0))] * n,
            out_specs=[pl.BlockSpec((None, block_rows, width), lambda i, pos: (pos[0], i, 0))] * n),
        out_shape=[jax.ShapeDtypeStruct((N_CHIPS, rows, width), MXU_DTYPE)] * n,
        compiler_params=_params(("arbitrary",)),
    )(pos, *shards)


def _gather_side(placed):
    n = len(placed)

    def copies(kind, bufs, send_sem, recv_sem, base):
        x, y, c = _position()
        me = 2 * x + y
        made = []
        for i in range(n):
            hr = placed[i].shape[1] // 2
            for k in (1, 2, 3):
                px, py = _other_chip(x, y, k)
                chip, core, slot, dev = [(me, c, k - 1, (px, py, c)), (2 * px + py, c, 2 + k, (x, y, 1 - c)),
                                         (2 * px + py, 1 - c, 2 + k, (x, y, 1 - c))][kind]
                piece = bufs[i].at[chip, pl.ds(core * hr, hr)]
                made.append(pltpu.make_async_remote_copy(
                    src_ref=piece, dst_ref=piece, send_sem=send_sem.at[base + 6 * i + slot],
                    recv_sem=recv_sem.at[base + 6 * i + slot], device_id=dev, device_id_type=MESH))
        return made

    def start(ins, outs, send_sem, recv_sem, base=0):
        for cp in copies(0, outs, send_sem, recv_sem, base):
            cp.start()

    def finish(ins, outs, send_sem, recv_sem, base=0):
        first, onward = copies(0, outs, send_sem, recv_sem, base), copies(1, outs, send_sem, recv_sem, base)
        for landed, cp in zip(first, onward):
            landed.wait_recv()
            cp.start()
        for cp in copies(2, outs, send_sem, recv_sem, base):
            cp.wait_recv()
        for cp in first + onward:
            cp.wait_send()

    return _Side(placed, [jax.ShapeDtypeStruct(a.shape, a.dtype) for a in placed], 6 * n, start, finish,
                 {i: i for i in range(n)})


def _same_shape(shapes):
    found = {}
    for i, shape in enumerate(shapes):
        found.setdefault(shape, []).append(i)
    return found


def _sum_call(name, pos, groups, rows, width, out_dtype, block_rows, out_rows=None, out_index=None):
    out_rows = rows if out_rows is None else out_rows
    out_index = (lambda i, pos, br: i) if out_index is None else out_index
    sizes = [len(g) for g in groups]

    def body(pos_ref, *refs):
        ins, outs = refs[:sum(sizes)], refs[sum(sizes):]
        at = 0
        for size, out in zip(sizes, outs):
            acc = ins[at][...].astype(F32)
            for r in ins[at + 1:at + size]:
                acc = acc + r[...].astype(F32)
            out[...] = acc.astype(out_dtype)
            at += size

    def spec(index):
        return pl.BlockSpec((block_rows, width), lambda i, pos: (index(i, pos, block_rows), 0))

    return pl.pallas_call(
        body, name=name,
        grid_spec=pltpu.PrefetchScalarGridSpec(
            num_scalar_prefetch=1, grid=(rows // block_rows,),
            in_specs=[spec(index) for g in groups for _, index in g], out_specs=[spec(out_index)] * len(groups)),
        out_shape=[jax.ShapeDtypeStruct((out_rows, width), out_dtype)] * len(groups),
        compiler_params=_params(("arbitrary",)),
    )(pos, *[arr for g in groups for arr, _ in g])


def _pair_stage(tag, grads, pos):
    n = len(grads)
    shapes = [(g.shape[0] // N_CHIPS, g.shape[1]) for g in grads]

    def plan(ins, outs):
        x, y, c = _position()
        remote = []
        for i, (r, _) in enumerate(shapes):
            hr = r // 2
            for j in range(N_CHIPS):
                remote.append((ins[i].at[pl.ds(j * r + (1 - c) * hr, hr)], outs[i].at[pl.ds(j * hr, hr)],
                               (x, y, 1 - c)))
        return remote

    pair = _exchange_call(f"grad_pair_exchange_{tag}", _plan_side(
        grads, [jax.ShapeDtypeStruct((N_CHIPS * (r // 2), w), COMM_DTYPE) for r, w in shapes], plan, N_CHIPS * n))

    def own_half(r):
        def index(s, pos, br):
            per = (r // 2) // br
            return (s // per) * (r // br) + pos[1] * per + s % per
        return index

    sums = [None] * n
    for (r, w), which in _same_shape(shapes).items():
        groups = [[(grads[i], own_half(r)), (pair[i], lambda s, pos, br: s)] for i in which]
        outs = _sum_call(f"grad_pair_sum_{tag}{which[0]}", pos, groups, N_CHIPS * (r // 2), w, COMM_DTYPE,
                         min(256, r // 2))
        for i, out in zip(which, outs):
            sums[i] = out
    return sums


def _chip_side(pair_sums):
    halves = [(p.shape[0] // N_CHIPS, p.shape[1]) for p in pair_sums]

    def plan(ins, outs):
        x, y, c = _position()
        remote = []
        for i, (hr, _) in enumerate(halves):
            for k in (1, 2, 3):
                px, py = _other_chip(x, y, k)
                remote.append((ins[i].at[pl.ds((2 * px + py) * hr, hr)], outs[i].at[pl.ds((k - 1) * hr, hr)],
                               (px, py, c)))
        return remote

    return _plan_side(pair_sums, [jax.ShapeDtypeStruct((3 * hr, w), COMM_DTYPE) for hr, w in halves], plan,
                      3 * len(pair_sums))


def _finish_stage(tag, pair_sums, chip, pos, extra=None):
    n = len(pair_sums)
    halves = [(p.shape[0] // N_CHIPS, p.shape[1]) for p in pair_sums]

    def slab(k, hr):
        return lambda s, pos, br: k * (hr // br) + s

    reduced = [None] * n
    for (hr, w), which in _same_shape(halves).items():
        groups = [[(pair_sums[i], lambda s, pos, br: pos[0] * (hr // br) + s)] + [(chip[i], slab(k, hr)) for k in range(3)]
                  for i in which]
        outs = _sum_call(f"grad_chip_sum_{tag}{which[0]}", pos, groups, hr, w, F32, min(256, hr), out_rows=2 * hr,
                         out_index=lambda s, pos, br: pos[1] * (hr // br) + s)
        for i, out in zip(which, outs):
            reduced[i] = out

    def plan(ins, outs):
        x, y, c = _position()
        remote = []
        for i, (hr, _) in enumerate(halves):
            mine = outs[i].at[pl.ds(c * hr, hr)]
            remote.append((mine, mine, (x, y, 1 - c)))
        return remote

    swap = _plan_side(reduced, [jax.ShapeDtypeStruct((2 * hr, w), F32) for hr, w in halves], plan, n,
                      {i: i for i in range(n)})
    if extra is None:
        return _exchange_call(f"grad_half_exchange_{tag}", swap)
    res = _exchange_call(f"grad_half_exchange_{tag}", _join_sides(swap, extra))
    return res[:n], res[n:]


def _small_gather_side(parts):
    n = len(parts)

    def copies(ins, outs, send_sem, recv_sem, base):
        x, y, c = _position()
        made = []
        for i in range(n):
            mine = outs[i].at[4 * x + 2 * y + c]
            made.append(pltpu.make_async_copy(ins[i], mine, send_sem.at[base + 8 * i + 7]))
            for d in range(1, 8):
                px, py = _other_chip(x, y, d >> 1)
                made.append(pltpu.make_async_remote_copy(
                    src_ref=ins[i], dst_ref=mine, send_sem=send_sem.at[base + 8 * i + d - 1],
                    recv_sem=recv_sem.at[base + 8 * i + d - 1], device_id=(px, py, 1 - c if d & 1 else c),
                    device_id_type=MESH))
        return made

    def start(ins, outs, send_sem, recv_sem, base=0):
        for cp in copies(ins, outs, send_sem, recv_sem, base):
            cp.start()

    def finish(ins, outs, send_sem, recv_sem, base=0):
        for cp in copies(ins, outs, send_sem, recv_sem, base):
            cp.wait()

    return _Side(parts, [jax.ShapeDtypeStruct((8,) + a_.shape, F32) for a_ in parts], 8 * n, start, finish)


def _adamw(w, g, m, v):
    m = ADAM_B1 * m + (1.0 - ADAM_B1) * g
    v = ADAM_B2 * v + (1.0 - ADAM_B2) * (g * g)
    m_hat = m / (1.0 - ADAM_B1 ** ADAM_STEP)
    v_hat = v / (1.0 - ADAM_B2 ** ADAM_STEP)
    delta = -ADAM_LR * (m_hat / (jnp.sqrt(v_hat) + ADAM_EPS) + ADAM_WD * w)
    return delta, m, v


def _adamw_call(name, ws, gs, ms, vs):
    n = len(ws)
    rows, width = ws[0].shape
    block_rows = min(256 // n, rows)

    def body(*refs):
        for i in range(n):
            w_ref, g_ref, m_ref, v_ref = (refs[j * n + i] for j in range(4))
            d_out, m_out, v_out = (refs[(4 + j) * n + i] for j in range(3))
            d_out[...], m_out[...], v_out[...] = _adamw(w_ref[...], g_ref[...], m_ref[...], v_ref[...])

    spec = pl.BlockSpec((block_rows, width), lambda i: (i, 0))
    res = pl.pallas_call(
        body, name=name, grid=(rows // block_rows,), in_specs=[spec] * (4 * n), out_specs=[spec] * (3 * n),
        out_shape=[jax.ShapeDtypeStruct(ws[0].shape, F32)] * (3 * n),
        compiler_params=_params(("arbitrary",)),
    )(*ws, *gs, *ms, *vs)
    return [(res[i], res[n + i], res[2 * n + i]) for i in range(n)]


def _small_adamw_call(gathered, weights, moments_m, moments_v):
    n = len(weights)

    def body(*refs):
        all_refs, refs = refs[:n + 1], refs[n + 1:]
        w_refs, m_refs, v_refs, outs = refs[:n], refs[n:2 * n], refs[2 * n:3 * n], refs[3 * n:]

        def total(ref):
            acc = ref[0]
            for d in range(1, 8):
                acc = acc + ref[d]
            return acc

        outs[0][...] = total(all_refs[n])
        for i in range(n):
            g = total(all_refs[i])
            if i == 0:
                row = lax.broadcasted_iota(jnp.int32, g.shape, 0)
                col = lax.broadcasted_iota(jnp.int32, g.shape, 1)
                g = jnp.where((row % CHUNK) >= col, g, 0.0)
            g_out, d_out, m_out, v_out = outs[1 + 4 * i:5 + 4 * i]
            g_out[...] = g
            d_out[...], m_out[...], v_out[...] = _adamw(w_refs[i][...], g, m_refs[i][...], v_refs[i][...])

    out_shape = [jax.ShapeDtypeStruct(LOSS_TILE, F32)]
    for w in weights:
        out_shape += [jax.ShapeDtypeStruct(w.shape, F32)] * 4
    res = pl.pallas_call(
        body, name="small_adamw", out_shape=out_shape,
        compiler_params=pltpu.CompilerParams(vmem_limit_bytes=VMEM_LIMIT),
    )(*gathered, *weights, *moments_m, *moments_v)
    return res[0], [res[1 + 4 * i:5 + 4 * i] for i in range(n)]


def kernel(x, p, w_in, w_ret_out, w_sgu_out, w_out, sgu_ws, sgu_bs, w_ple_gate, w_ple_proj, g_mixer, g_ple, g_final, loss_target, m_w_in, m_w_ret_out, m_w_sgu_out, m_w_out, m_sgu_ws, m_sgu_bs, m_w_ple_gate, m_w_ple_proj, m_g_mixer, m_g_ple, m_g_final, v_w_in, v_w_ret_out, v_w_sgu_out, v_w_out, v_sgu_ws, v_sgu_bs, v_w_ple_gate, v_w_ple_proj, v_g_mixer, v_g_ple, v_g_final):
    n_seq, seq, _ = x.shape
    t = n_seq * seq
    x2d = x.reshape(t, D_MODEL)
    p2d = p.reshape(t, PLE_DIM)
    target = loss_target.reshape(t, D_MODEL)
    big = [w_in[0], w_ret_out[0], w_sgu_out[0], w_out[0], w_ple_gate[0], w_ple_proj[0]]
    big_m = [m_w_in[0], m_w_ret_out[0], m_w_sgu_out[0], m_w_out[0], m_w_ple_gate[0], m_w_ple_proj[0]]
    big_v = [v_w_in[0], v_w_ret_out[0], v_w_sgu_out[0], v_w_out[0], v_w_ple_gate[0], v_w_ple_proj[0]]

    pos = _position_array()
    placed = (_place_cast_call("place_w_in", big[:1], pos) + _place_cast_call("place_square_weights", big[1:5], pos)
              + _place_cast_call("place_w_ple_proj", big[5:], pos))
    ws = sgu_ws[0]
    bst = sgu_bs[0].T
    consts = _retention_consts(seq)

    (h, pj, wg_in), gathered = _proj_call(x2d, g_mixer, placed[0], pos, _gather_side(placed[1:]))
    pj0 = pj1 = pj2 = pj3 = pj
    w_ro, w_so, w_o, w_pg = (w.reshape(D_MODEL, D_MODEL) for w in gathered[:4])
    w_pp = gathered[4].transpose(1, 0, 2).reshape(PLE_DIM, D_MODEL)
    ret_raw, states = _retention_fwd_call(pj0, consts, n_seq, seq)
    wide, half, loss, dg_ple, dg_final = _tail_call(
        pj, ret_raw, x2d, p2d, target, ws, bst, g_ple, g_final.reshape(1, D_MODEL), w_ro, w_so, w_o, w_pg, w_pp)
    ret, sgu, merged, hp, dz, dpp = ((half, k) for k in range(6))
    dpj1, dpj2, dpj3, drr, da, db, dws, dbst = _merge_bwd_call(wide, pj1, pj2, pj3, ret_raw, ws, bst, w_ro, w_so, w_o)

    small_shapes = [(SGU_GROUPS * CHUNK, CHUNK), (SGU_GROUPS, CHUNK), (1, D_MODEL), (1, D_MODEL), (1, D_MODEL)]
    as_small = lambda arrays: [None if a_ is None else a_.reshape(s_) for a_, s_ in zip(arrays, small_shapes)]
    early = as_small([dws, dbst.T, None, dg_ple, dg_final])
    rows_of = lambda g: g.reshape(g.shape[0] * g.shape[1], g.shape[2])
    tail_grads = [
        rows_of(_wgrad_call("wgrad_ret_out", ret, da, D_MODEL)),
        rows_of(_wgrad_call("wgrad_sgu_out", sgu, db, D_MODEL)),
        rows_of(_wgrad_call("wgrad_out", merged, (wide, 0), D_MODEL)),
        rows_of(_wgrad_call("wgrad_ple_gate", hp, dz, D_MODEL)),
        rows_of(_wgrad_call("wgrad_ple_proj", p2d, dpp, D_MODEL, out_cols=PLE_DIM)),
    ]
    tail_sums = _pair_stage("tail", tail_grads, pos)
    (dpj0,), carried = _retention_bwd_call(
        pj0, drr, states, consts, n_seq, seq,
        _join_sides(_chip_side(tail_sums), _small_gather_side([early[0], early[1], early[3], early[4], loss])))
    tail_chip, early_all = carried[:len(tail_sums)], carried[len(tail_sums):]
    in_grad = None
    for j, d in enumerate((dpj0, dpj1, dpj2, dpj3)):
        in_grad = _wgrad_call(f"wgrad_in_{j}", h, d, SHARD_W, into=in_grad, slot=j, n_slots=N_CHIPS)
    in_sums = _pair_stage("in", [rows_of(in_grad)], pos)
    (dx, dg_mixer), in_chip = _dx_call((dpj0, dpj1, dpj2, dpj3), x2d, wide, g_mixer, wg_in, _chip_side(in_sums))
    g_big, (mixer_all,) = _finish_stage("all", in_sums + tail_sums, list(in_chip) + list(tail_chip), pos,
                                        _small_gather_side([dg_mixer]))
    upd = [None] * len(big)
    for _, which in _same_shape([w.shape for w in big]).items():
        pick = lambda arrays: [arrays[i] for i in which]
        for i, triple in zip(which, _adamw_call(f"adamw_{which[0]}", pick(big), pick(g_big), pick(big_m), pick(big_v))):
            upd[i] = triple

    small_g = [early_all[0], early_all[1], mixer_all, early_all[2], early_all[3], early_all[4]]
    total, small = _small_adamw_call(small_g, as_small([sgu_ws, sgu_bs, g_mixer, g_ple, g_final]),
                                     as_small([m_sgu_ws, m_sgu_bs, m_g_mixer, m_g_ple, m_g_final]),
                                     as_small([v_sgu_ws, v_sgu_bs, v_g_mixer, v_g_ple, v_g_final]))
    out_small_shapes = [sgu_ws.shape, sgu_bs.shape, g_mixer.shape, g_ple.shape, g_final.shape]

    def ordered(big_list, kind):
        w_in_, w_ro_, w_so_, w_o_, w_pg_, w_pp_ = [b_[None] for b_ in big_list]
        s_ws, s_bs, s_gm, s_gp, s_gf = [small[i][kind].reshape(s) for i, s in enumerate(out_small_shapes)]
        return [w_in_, w_ro_, w_so_, w_o_, s_ws, s_bs, w_pg_, w_pp_, s_gm, s_gp, s_gf]

    out = [total[0, 0], dx.reshape(x.shape)]
    out += ordered(g_big, 0)
    out += ordered([u[0] for u in upd], 1)
    out += ordered([u[1] for u in upd], 2)
    out += ordered([u[2] for u in upd], 3)
    return tuple(out)
```

```python
import functools
import math

import numpy as np
import jax
import jax.numpy as jnp
from jax import lax
from jax.experimental import pallas as pl
from jax.experimental.pallas import tpu as pltpu

F32 = jnp.float32
MXU_DTYPE = jnp.bfloat16
COMM_DTYPE = jnp.bfloat16

D_MODEL = 1024
RET_HEADS = 4
QK_DIM = 128
V_DIM = 256
CHUNK = 128
SGU_GROUPS = 4
GROUP_DIM = 256
PLE_DIM = 256
N_CHIPS = 4
SHARD_W = 2048
ROPE_BASE = 10000.0
NORM_EPS = 1e-6
GN_EPS = 1e-5
QK_SCALE = QK_DIM ** -0.5
SQRT_HALF = math.sqrt(0.5)
INV_SQRT_2PI = 1.0 / math.sqrt(2.0 * math.pi)

ADAM_LR = 0.001
ADAM_B1 = 0.9
ADAM_B2 = 0.999
ADAM_EPS = 1e-08
ADAM_WD = 0.01
ADAM_STEP = 10

TILE_M = 256
LOSS_TILE = (8, 128)
LOAD_PARTS = 4
PROJ_TILE = 512
RET_TILE = 512
DX_TILE = 512
VMEM_LIMIT = 56 * 1024 * 1024
MESH = pl.DeviceIdType.MESH
ANY = pl.BlockSpec(memory_space=pl.ANY)

CHUNK_DECAY = tuple(
    float(np.exp(np.float32(CHUNK) * np.log(np.float32(1.0 - 2.0 ** (-5.0 - h))))) for h in range(RET_HEADS))


def _mm(a, b):
    return jnp.dot(a.astype(MXU_DTYPE), b.astype(MXU_DTYPE), preferred_element_type=F32)


def _mm_nt(a, b):
    return lax.dot_general(a.astype(MXU_DTYPE), b.astype(MXU_DTYPE), (((1,), (1,)), ((), ())),
                           preferred_element_type=F32)


def _mm_tn(a, b):
    return lax.dot_general(a.astype(MXU_DTYPE), b.astype(MXU_DTYPE), (((0,), (0,)), ((), ())),
                           preferred_element_type=F32)


def _mean(x):
    return jnp.mean(x, axis=-1, keepdims=True)


def _sigmoid(x):
    return jax.nn.sigmoid(x)


def _silu_and_grad(x):
    s = _sigmoid(x)
    silu = x * s
    return silu, s + silu * (1.0 - s)


def _gelu_and_grad(x):
    cdf = 0.5 + 0.5 * lax.erf(x * SQRT_HALF)
    return x * cdf, cdf + x * (jnp.exp(x * x * -0.5) * INV_SQRT_2PI)


def _unit_norm(x, eps):
    xc = x - _mean(x)
    rstd = lax.rsqrt(_mean(xc * xc) + eps)
    return xc * rstd, rstd


def _unit_norm_bwd(dn, n, rstd):
    return rstd * (dn - _mean(dn) - n * _mean(dn * n))


def _rms(x):
    r = lax.rsqrt(_mean(x * x) + NORM_EPS)
    return x * r, r


def _rms_bwd(dxn, xn, r):
    return r * (dxn - xn * _mean(dxn * xn))


def _rot(x):
    return pltpu.roll(x, QK_DIM // 2, 1)


def _params(semantics, **kw):
    return pltpu.CompilerParams(dimension_semantics=semantics, vmem_limit_bytes=VMEM_LIMIT, **kw)


def _row_tile(width, tm=TILE_M):
    return pl.BlockSpec((tm, width), lambda i: (i, 0))


def _proj_tile(j):
    return pl.BlockSpec((None, TILE_M, SHARD_W), lambda i: (j, i, 0))


def _resident(shape):
    nd = len(shape)
    return pl.BlockSpec(shape, lambda *_: (0,) * nd, pipeline_mode=pl.Buffered(1))


def _causal_ws(ws_ref):
    row = lax.broadcasted_iota(jnp.int32, (CHUNK, CHUNK), 0)
    col = lax.broadcasted_iota(jnp.int32, (CHUNK, CHUNK), 1)
    return [jnp.where(row >= col, ws_ref[g], 0.0).astype(MXU_DTYPE) for g in range(SGU_GROUPS)]


def _heads(x, width):
    return [x[:, h * width:(h + 1) * width] for h in range(x.shape[1] // width)]


def _branch_forward(pj1, pj2, ret_raw, wsc, bst, mixed_ref):
    rg, su = pj1[:, :D_MODEL], pj1[:, D_MODEL:]
    sv, sg = pj2[:, :D_MODEL], pj2[:, D_MODEL:]
    rn_parts, rstd_parts = zip(*[_unit_norm(r, GN_EPS) for r in _heads(ret_raw, V_DIM)])
    rn = jnp.concatenate(rn_parts, axis=1)
    sil_rg, dsil_rg = _silu_and_grad(rg)
    ret = rn * sil_rg
    u, du = _gelu_and_grad(su)
    gelu_sv, dgelu_sv = _gelu_and_grad(sv)
    vn, rstd_v = _unit_norm(gelu_sv, GN_EPS)
    for cc in range(pj1.shape[0] // CHUNK):
        for g in range(SGU_GROUPS):
            rs, cs = slice(cc * CHUNK, (cc + 1) * CHUNK), slice(g * GROUP_DIM, (g + 1) * GROUP_DIM)
            mixed_ref[rs, cs] = _mm(wsc[g], vn[rs, cs]) + bst[:, g:g + 1]
    mixed = mixed_ref[...]
    sil_sg, dsil_sg = _silu_and_grad(sg)
    sgu = u * mixed * sil_sg
    return dict(rn=rn, rstd_r=rstd_parts, sil_rg=sil_rg, dsil_rg=dsil_rg, ret=ret, u=u, du=du, dgelu_sv=dgelu_sv,
                vn=vn, rstd_v=rstd_v, mixed=mixed, sil_sg=sil_sg, dsil_sg=dsil_sg, sgu=sgu)


class _Side:
    def __init__(self, operands, out_shapes, n_sems, start, finish, aliases=None):
        self.operands, self.out_shapes, self.n_sems = list(operands), list(out_shapes), n_sems
        self.start, self.finish, self.aliases = start, finish, dict(aliases or {})


def _join_sides(*sides):
    spans, a, b, s = [], 0, 0, 0
    for side in sides:
        spans.append((a, b, s))
        a, b, s = a + len(side.operands), b + len(side.out_shapes), s + side.n_sems

    def run(which):
        def go(ins, outs, send_sem, recv_sem, base=0):
            for side, (a0, b0, s0) in zip(sides, spans):
                getattr(side, which)(ins[a0:a0 + len(side.operands)], outs[b0:b0 + len(side.out_shapes)],
                                     send_sem, recv_sem, base + s0)
        return go

    aliases = {a0 + i: b0 + o for side, (a0, b0, _) in zip(sides, spans) for i, o in side.aliases.items()}
    return _Side([x for side in sides for x in side.operands], [x for side in sides for x in side.out_shapes], s,
                 run("start"), run("finish"), aliases)


def _compute_call(body, *, name, grid, in_specs, out_specs, out_shape, operands, semantics, scratch_shapes=(),
                  side=None, prefetch=None, aliases=None, side_start=None):
    n_pre = 0 if prefetch is None else 1
    pre = () if prefetch is None else (prefetch,)

    def spec(in_specs, out_specs, scratch):
        return pltpu.PrefetchScalarGridSpec(num_scalar_prefetch=n_pre, grid=grid, in_specs=in_specs,
                                            out_specs=out_specs, scratch_shapes=scratch)

    if side is None:
        return pl.pallas_call(body, name=name, grid_spec=spec(in_specs, out_specs, list(scratch_shapes)),
                              out_shape=out_shape,
                              input_output_aliases={n_pre + a: b for a, b in (aliases or {}).items()},
                              compiler_params=_params(semantics))(*pre, *operands)
    n_in, n_out, s_in, s_out = len(operands), len(out_shape), len(side.operands), len(side.out_shapes)

    def carrier(*refs):
        pre_refs, refs = refs[:n_pre], refs[n_pre:]
        ins, refs = refs[:n_in], refs[n_in:]
        side_ins, refs = refs[:s_in], refs[s_in:]
        outs, refs = refs[:n_out], refs[n_out:]
        side_outs, refs = refs[:s_out], refs[s_out:]
        scratch, (send_sem, recv_sem) = refs[:-2], refs[-2:]
        ids = [pl.program_id(a) for a in range(len(grid))]
        at = (0,) * len(grid) if side_start is None else side_start
        first = functools.reduce(jnp.logical_and, [i == a for i, a in zip(ids, at)])
        last = functools.reduce(jnp.logical_and, [i == g - 1 for i, g in zip(ids, grid)])

        @pl.when(first)
        def _():
            side.start(side_ins, side_outs, send_sem, recv_sem)

        body(*pre_refs, *ins, *outs, *scratch)

        @pl.when(last)
        def _():
            side.finish(side_ins, side_outs, send_sem, recv_sem)

    all_aliases = {n_pre + a: b for a, b in (aliases or {}).items()}
    all_aliases.update({n_pre + n_in + a: n_out + b for a, b in side.aliases.items()})
    res = pl.pallas_call(
        carrier, name=name,
        grid_spec=spec(list(in_specs) + [ANY] * s_in, list(out_specs) + [ANY] * s_out,
                       list(scratch_shapes) + [pltpu.SemaphoreType.DMA((side.n_sems,))] * 2),
        out_shape=list(out_shape) + side.out_shapes, input_output_aliases=all_aliases,
        compiler_params=_params(semantics, has_side_effects=True),
    )(*pre, *operands, *side.operands)
    return res[:n_out], res[n_out:]


def _exchange_call(name, side):
    s_in = len(side.operands)

    def body(*refs):
        ins, outs = refs[:s_in], refs[s_in:s_in + len(side.out_shapes)]
        send_sem, recv_sem = refs[s_in + len(side.out_shapes):]
        side.start(ins, outs, send_sem, recv_sem)
        side.finish(ins, outs, send_sem, recv_sem)

    return pl.pallas_call(
        body, name=name, in_specs=[ANY] * s_in, out_specs=[ANY] * len(side.out_shapes), out_shape=side.out_shapes,
        scratch_shapes=[pltpu.SemaphoreType.DMA((side.n_sems,))] * 2, input_output_aliases=side.aliases,
        compiler_params=pltpu.CompilerParams(has_side_effects=True),
    )(*side.operands)


def _proj_call(x2d, g_mixer, placed_in, pos, side):
    t = x2d.shape[0]
    nt = t // PROJ_TILE
    hr = placed_in.shape[1] // 2

    def body(pos_ref, x_ref, g_ref, win_ref, h_ref, pj_ref, w_ref, w_vmem, h_all, loc_sem, send_sem, recv_sem):
        k, i = pl.program_id(0), pl.program_id(1)
        x, y, c = _position()
        me = 2 * x + y

        def copy(slot, chip, core, dev):
            piece = w_ref.at[chip, pl.ds(core * hr, hr)]
            return pltpu.make_async_remote_copy(src_ref=piece, dst_ref=piece, send_sem=send_sem.at[slot],
                                                recv_sem=recv_sem.at[slot], device_id=dev, device_id_type=MESH)

        def first_hop(kk):
            px, py = _other_chip(x, y, kk)
            return copy(kk - 1, me, c, (px, py, c))

        def relay():
            source = jnp.bitwise_xor(me, 2 - c)
            return copy(2, source, c, (jnp.bitwise_xor(x, c), jnp.bitwise_xor(y, 1 - c), c))

        def onward(kk, core):
            px, py = _other_chip(x, y, kk)
            return copy(2 + kk, 2 * px + py, core, (x, y, 1 - c))

        def loads(kk):
            px, py = _other_chip(x, y, kk)
            rows = w_vmem.shape[1] // LOAD_PARTS
            return [pltpu.make_async_copy(w_ref.at[2 * px + py, pl.ds(r * rows, rows)],
                                          w_vmem.at[kk % 2, pl.ds(r * rows, rows)], loc_sem.at[r])
                    for r in range(LOAD_PARTS)]

        @pl.when(jnp.logical_and(k == 0, i == 0))
        def _():
            for kk in (1, 2):
                first_hop(kk).start()
            for cp in loads(0):
                cp.start()
            for cp in loads(0):
                cp.wait()

        for kk in (1, 2, 3):
            @pl.when(jnp.logical_and(k == kk - 1, i == nt - 1))
            def _(kk=kk):
                if kk == 1:
                    first_hop(1).wait_recv()
                    first_hop(2).wait_recv()
                    relay().start()
                if kk == 3:
                    relay().wait_recv()
                onward(kk, c).start()
                onward(kk, 1 - c).wait_recv()
                for cp in loads(kk):
                    cp.start()

            @pl.when(jnp.logical_and(k == kk, i == 0))
            def _(kk=kk):
                for cp in loads(kk):
                    cp.wait()

        rows = pl.ds(pl.multiple_of(i * PROJ_TILE, PROJ_TILE), PROJ_TILE)

        @pl.when(k == 0)
        def _():
            xn, _ = _rms(x_ref[...])
            h = (xn * g_ref[...]).astype(MXU_DTYPE)
            h_all[rows, :] = h
            h_ref[...] = h

        pj_ref[...] = jnp.dot(h_all[rows, :], w_vmem[k % 2], preferred_element_type=F32)

        @pl.when(jnp.logical_and(k == N_CHIPS - 1, i == nt - 1))
        def _():
            for cp in [first_hop(1), first_hop(2), relay()] + [onward(kk, c) for kk in (1, 2, 3)]:
                cp.wait_send()

    parked = lambda k, i, pos: (jnp.where(k == 0, i, nt - 1), 0)
    return _compute_call(
        body, name="proj_fwd", grid=(N_CHIPS, nt),
        in_specs=[pl.BlockSpec((PROJ_TILE, D_MODEL), parked), pl.BlockSpec((1, D_MODEL), lambda k, i, pos: (0, 0)), ANY],
        out_specs=[pl.BlockSpec((PROJ_TILE, D_MODEL), lambda k, i, pos: (jnp.where(k == 0, i, nt), 0)),
                   pl.BlockSpec((None, PROJ_TILE, SHARD_W), lambda k, i, pos: (jnp.bitwise_xor(pos[0], k), i, 0)),
                   ANY],
        out_shape=[jax.ShapeDtypeStruct((t + PROJ_TILE, D_MODEL), MXU_DTYPE),
                   jax.ShapeDtypeStruct((N_CHIPS, t, SHARD_W), F32),
                   jax.ShapeDtypeStruct(placed_in.shape, placed_in.dtype)],
        scratch_shapes=[pltpu.VMEM((2,) + placed_in.shape[1:], placed_in.dtype), pltpu.VMEM((t, D_MODEL), MXU_DTYPE),
                        pltpu.SemaphoreType.DMA((LOAD_PARTS,)), pltpu.SemaphoreType.DMA((6,)),
                        pltpu.SemaphoreType.DMA((6,))],
        operands=(x2d, g_mixer, placed_in), semantics=("arbitrary", "arbitrary"), side=side, prefetch=pos,
        aliases={2: 2}, side_start=(2, 0))


def _retention_consts(seq):
    half = QK_DIM // 2
    inv = ROPE_BASE ** (-jnp.arange(half, dtype=F32) / half)
    inv = jnp.concatenate([inv, inv])
    coarse = (jnp.arange(seq // half, dtype=F32) * half)[:, None] * inv[None, :]
    fine = jnp.arange(half, dtype=F32)[:, None] * inv[None, :]
    ca, sa, cb, sb = jnp.cos(coarse)[:, None], jnp.sin(coarse)[:, None], jnp.cos(fine)[None], jnp.sin(fine)[None]
    cos_full = (ca * cb - sa * sb).reshape(seq, QK_DIM)
    sign = jnp.concatenate([-jnp.ones((half,), F32), jnp.ones((half,), F32)])
    sin_signed = (sa * cb + ca * sb).reshape(seq, QK_DIM) * sign
    log_g = jnp.log(1.0 - 2.0 ** (-5.0 - jnp.arange(RET_HEADS, dtype=F32)))
    idx = jnp.arange(CHUNK, dtype=F32)
    diff = idx[:, None] - idx[None, :]
    decay_in = jnp.where(diff[None] >= 0, jnp.exp(jnp.maximum(diff, 0.0)[None] * log_g[:, None, None]), 0.0)
    zeta = jnp.exp((CHUNK - 1.0 - idx)[None, :] * log_g[:, None])
    xi = jnp.exp((idx + 1.0)[None, :] * log_g[:, None])
    zeta = jnp.broadcast_to(zeta[:, :, None], (RET_HEADS, CHUNK, QK_DIM))
    xi = jnp.broadcast_to(xi[:, :, None], (RET_HEADS, CHUNK, QK_DIM))
    return cos_full, sin_signed, decay_in, zeta, xi


def _qkv(pj_ref, rows, h):
    q = pj_ref[rows, h * QK_DIM:(h + 1) * QK_DIM]
    k = pj_ref[rows, RET_HEADS * QK_DIM + h * QK_DIM:RET_HEADS * QK_DIM + (h + 1) * QK_DIM]
    v = pj_ref[rows, 2 * RET_HEADS * QK_DIM + h * V_DIM:2 * RET_HEADS * QK_DIM + (h + 1) * V_DIM]
    return q, k, v


def _retention_fwd_call(pj0, consts, n_seq, seq):
    cos_full, sin_signed, decay_in, zeta, xi = consts
    nb = seq // RET_TILE
    cpb = RET_TILE // CHUNK

    def body(pj_ref, cos_ref, sin_ref, d_ref, zeta_ref, xi_ref, o_ref, st_ref, state):
        @pl.when(pl.program_id(1) == 0)
        def _():
            state[...] = jnp.zeros_like(state)

        for cc in range(cpb):
            rows = slice(cc * CHUNK, (cc + 1) * CHUNK)
            cs, sn = cos_ref[rows, :], sin_ref[rows, :]
            for h in range(RET_HEADS):
                q, k, v = _qkv(pj_ref, rows, h)
                qt = (q * cs + _rot(q) * sn) * QK_SCALE
                kt = k * cs + _rot(k) * sn
                prev = state[h]
                st_ref[cc, h] = prev.astype(st_ref.dtype)
                scores = _mm_nt(qt, kt) * d_ref[h]
                o_ref[rows, h * V_DIM:(h + 1) * V_DIM] = _mm(scores, v) + _mm(qt * xi_ref[h], prev)
                state[h] = _mm_tn(kt * zeta_ref[h], v) + CHUNK_DECAY[h] * prev

    row = lambda b, n: (b * nb + n, 0)
    pos = lambda b, n: (n, 0)
    return pl.pallas_call(
        body, name="retention_fwd", grid=(n_seq, nb),
        in_specs=[pl.BlockSpec((None, RET_TILE, SHARD_W), lambda b, n: (0,) + row(b, n)),
                  pl.BlockSpec((RET_TILE, QK_DIM), pos),
                  pl.BlockSpec((RET_TILE, QK_DIM), pos), _resident(decay_in.shape), _resident(zeta.shape),
                  _resident(xi.shape)],
        out_specs=[pl.BlockSpec((RET_TILE, D_MODEL), row),
                   pl.BlockSpec((cpb, RET_HEADS, QK_DIM, V_DIM), lambda b, n: (b * nb + n, 0, 0, 0))],
        out_shape=[jax.ShapeDtypeStruct((n_seq * seq, D_MODEL), F32),
                   jax.ShapeDtypeStruct((n_seq * seq // CHUNK, RET_HEADS, QK_DIM, V_DIM), MXU_DTYPE)],
        scratch_shapes=[pltpu.VMEM((RET_HEADS, QK_DIM, V_DIM), F32)],
        compiler_params=_params(("arbitrary", "arbitrary")),
    )(pj0, cos_full, sin_signed, decay_in, zeta, xi)


def _merge_bwd_call(wide, pj1, pj2, pj3, ret_raw, ws, bst, w_ro, w_so, w_o):
    t = wide.shape[1]

    def body(wide_ref, pj1_ref, pj2_ref, pj3_ref, rr_ref, ws_ref, bst_ref, wro_ref, wso_ref, wo_ref,
             dpj1_ref, dpj2_ref, dpj3_ref, drr_ref, da_ref, db_ref, dws_ref, dbst_ref, mixed_ref):
        @pl.when(pl.program_id(0) == 0)
        def _():
            dws_ref[...] = jnp.zeros_like(dws_ref)
            dbst_ref[...] = jnp.zeros_like(dbst_ref)

        wsc = _causal_ws(ws_ref)
        f = _branch_forward(pj1_ref[...], pj2_ref[...], rr_ref[...], wsc, bst_ref[...], mixed_ref)
        pj3 = pj3_ref[...]
        smr, sms = _sigmoid(pj3[:, :D_MODEL]), _sigmoid(pj3[:, D_MODEL:])
        dmerged = _mm_nt(wide_ref[0], wo_ref[...])
        da_f, db_f = dmerged * smr, dmerged * sms
        da, db = da_f.astype(MXU_DTYPE), db_f.astype(MXU_DTYPE)
        dpj3_ref[:, :D_MODEL] = (da_f * wide_ref[1] * (1.0 - smr)).astype(dpj3_ref.dtype)
        dpj3_ref[:, D_MODEL:] = (db_f * wide_ref[2] * (1.0 - sms)).astype(dpj3_ref.dtype)
        da_ref[...] = da
        db_ref[...] = db

        dret = _mm_nt(da, wro_ref[...])
        dpj1_ref[:, :D_MODEL] = (dret * f["rn"] * f["dsil_rg"]).astype(dpj1_ref.dtype)
        drn = dret * f["sil_rg"]
        for h in range(RET_HEADS):
            cols = slice(h * V_DIM, (h + 1) * V_DIM)
            drr_ref[:, cols] = _unit_norm_bwd(drn[:, cols], f["rn"][:, cols], f["rstd_r"][h]).astype(drr_ref.dtype)

        dsgu = _mm_nt(db, wso_ref[...])
        dpj2_ref[:, D_MODEL:] = (dsgu * f["u"] * f["mixed"] * f["dsil_sg"]).astype(dpj2_ref.dtype)
        tg = dsgu * f["sil_sg"]
        dpj1_ref[:, D_MODEL:] = (tg * f["mixed"] * f["du"]).astype(dpj1_ref.dtype)
        dmixed = tg * f["u"]
        for cc in range(TILE_M // CHUNK):
            for g in range(SGU_GROUPS):
                rs, cs = slice(cc * CHUNK, (cc + 1) * CHUNK), slice(g * GROUP_DIM, (g + 1) * GROUP_DIM)
                dm = dmixed[rs, cs]
                mixed_ref[rs, cs] = _mm_tn(wsc[g], dm)
                dws_ref[g] += _mm_nt(dm, f["vn"][rs, cs])
                dbst_ref[:, g:g + 1] += jnp.sum(dm, axis=1, keepdims=True)
        dvv = _unit_norm_bwd(mixed_ref[...], f["vn"], f["rstd_v"])
        dpj2_ref[:, :D_MODEL] = (dvv * f["dgelu_sv"]).astype(dpj2_ref.dtype)

    sq = (D_MODEL, D_MODEL)
    return pl.pallas_call(
        body, name="merge_bwd", grid=(t // TILE_M,),
        in_specs=[pl.BlockSpec((3, TILE_M, D_MODEL), lambda i: (0, i, 0))] + [_proj_tile(j) for j in (1, 2, 3)]
        + [_row_tile(D_MODEL)] + [_resident(ws.shape), _resident(bst.shape), _resident(sq), _resident(sq), _resident(sq)],
        out_specs=[_row_tile(SHARD_W)] * 3 + [_row_tile(D_MODEL)] * 3
        + [pl.BlockSpec(ws.shape, lambda i: (0, 0, 0)), pl.BlockSpec(bst.shape, lambda i: (0, 0))],
        out_shape=[jax.ShapeDtypeStruct((t, SHARD_W), MXU_DTYPE)] * 3 + [jax.ShapeDtypeStruct((t, D_MODEL), MXU_DTYPE)] * 3
        + [jax.ShapeDtypeStruct(ws.shape, F32), jax.ShapeDtypeStruct(bst.shape, F32)],
        scratch_shapes=[pltpu.VMEM((TILE_M, D_MODEL), F32)], compiler_params=_params(("arbitrary",)),
    )(wide, pj1, pj2, pj3, ret_raw, ws, bst, w_ro, w_so, w_o)


def _tail_call(pj, ret_raw, x2d, p2d, target, ws, bst, g_ple, g_final, w_ro, w_so, w_o, w_pg, w_pp):
    t = x2d.shape[0]
    tm = TILE_M

    def body(pj1_ref, pj2_ref, pj3_ref, rr_ref, x_ref, p_ref, t_ref, ws_ref, bst_ref, gp_ref, gf_ref,
             wro_ref, wso_ref, wo_ref, wpg_ref, wpp_ref,
             wide_ref, half_ref, loss_ref, dgp_ref, dgf_ref, mixed_ref):
        dx1_ref, a_ref, b_ref = (wide_ref.at[k] for k in range(3))
        ret_ref, sgu_ref, mg_ref, hp_ref, dz_ref, dpp_ref = (half_ref.at[k] for k in range(6))
        @pl.when(pl.program_id(0) == 0)
        def _():
            for acc in (loss_ref, dgp_ref, dgf_ref):
                acc[...] = jnp.zeros_like(acc)

        f = _branch_forward(pj1_ref[...], pj2_ref[...], rr_ref[...], _causal_ws(ws_ref), bst_ref[...], mixed_ref)
        ret = f["ret"].astype(MXU_DTYPE)
        sgu = f["sgu"].astype(MXU_DTYPE)
        ret_ref[...] = ret
        sgu_ref[...] = sgu
        a = jnp.dot(ret, wro_ref[...], preferred_element_type=F32)
        b = jnp.dot(sgu, wso_ref[...], preferred_element_type=F32)
        a_ref[...] = a
        b_ref[...] = b
        pj3 = pj3_ref[...]
        smr, sms = _sigmoid(pj3[:, :D_MODEL]), _sigmoid(pj3[:, D_MODEL:])
        merged = (smr * a + sms * b).astype(MXU_DTYPE)
        mg_ref[...] = merged
        x1v = x_ref[...] + jnp.dot(merged, wo_ref[...], preferred_element_type=F32)
        xn1, r1 = _rms(x1v)
        hp = (xn1 * gp_ref[...]).astype(MXU_DTYPE)
        hp_ref[...] = hp
        gate = _sigmoid(jnp.dot(hp, wpg_ref[...], preferred_element_type=F32))
        pp = jnp.dot(p_ref[...].astype(MXU_DTYPE), wpp_ref[...], preferred_element_type=F32)
        xn2, r2 = _rms(x1v + gate * pp)
        err = xn2 * gf_ref[...] - t_ref[...]
        loss_ref[...] += (0.5 / D_MODEL) * jnp.sum(jnp.sum(err * err, axis=1, keepdims=True), axis=0, keepdims=True)

        dy = err * (1.0 / D_MODEL)
        dgf_ref[...] += jnp.sum(dy * xn2, axis=0, keepdims=True)
        dx2 = _rms_bwd(dy * gf_ref[...], xn2, r2)
        dpp_ref[...] = (dx2 * gate).astype(MXU_DTYPE)
        dz = (dx2 * pp * gate * (1.0 - gate)).astype(MXU_DTYPE)
        dz_ref[...] = dz
        dhp = _mm_nt(dz, wpg_ref[...])
        dgp_ref[...] += jnp.sum(dhp * xn1, axis=0, keepdims=True)
        dx1_ref[...] = dx2 + _rms_bwd(dhp * gp_ref[...], xn1, r1)

    sq = (D_MODEL, D_MODEL)
    vec = _resident((1, D_MODEL))
    whole = lambda shape: pl.BlockSpec(shape, lambda i: (0,) * len(shape))
    return pl.pallas_call(
        body, name="tail_fwd", grid=(t // tm,),
        in_specs=[_proj_tile(j) for j in (1, 2, 3)]
        + [_row_tile(D_MODEL), _row_tile(D_MODEL), _row_tile(PLE_DIM), _row_tile(D_MODEL), _resident(ws.shape),
           _resident(bst.shape), vec, vec, _resident(sq), _resident(sq), _resident(sq), _resident(sq),
           _resident((PLE_DIM, D_MODEL))],
        out_specs=[pl.BlockSpec((3, tm, D_MODEL), lambda i: (0, i, 0)), pl.BlockSpec((6, tm, D_MODEL), lambda i: (0, i, 0)),
                   whole(LOSS_TILE), whole((1, D_MODEL)), whole((1, D_MODEL))],
        out_shape=[jax.ShapeDtypeStruct((3, t, D_MODEL), F32), jax.ShapeDtypeStruct((6, t, D_MODEL), MXU_DTYPE),
                   jax.ShapeDtypeStruct(LOSS_TILE, F32), jax.ShapeDtypeStruct((1, D_MODEL), F32),
           jax.ShapeDtypeStruct((1, D_MODEL), F32)],
        scratch_shapes=[pltpu.VMEM((TILE_M, D_MODEL), F32)], compiler_params=_params(("arbitrary",)),
    )(pj, pj, pj, ret_raw, x2d, p2d, target, ws, bst, g_ple, g_final, w_ro, w_so, w_o, w_pg, w_pp)


def _retention_bwd_call(pj0, drr, states, consts, n_seq, seq, side):
    cos_full, sin_signed, decay_in, zeta, xi = consts
    nb = seq // RET_TILE
    cpb = RET_TILE // CHUNK

    def body(pj_ref, do_ref, st_ref, cos_ref, sin_ref, d_ref, zeta_ref, xi_ref, dpj_ref, gstate):
        @pl.when(pl.program_id(1) == 0)
        def _():
            gstate[...] = jnp.zeros_like(gstate)

        for cc in reversed(range(cpb)):
            rows = slice(cc * CHUNK, (cc + 1) * CHUNK)
            cs, sn = cos_ref[rows, :], sin_ref[rows, :]
            for h in range(RET_HEADS):
                q, k, v = _qkv(pj_ref, rows, h)
                qt = (q * cs + _rot(q) * sn) * QK_SCALE
                kt = k * cs + _rot(k) * sn
                d_out = do_ref[rows, h * V_DIM:(h + 1) * V_DIM]
                prev = st_ref[cc, h]
                g = gstate[h]
                dec = d_ref[h]
                scores_d = _mm_nt(qt, kt) * dec
                dscores = _mm_nt(d_out, v) * dec
                kz = kt * zeta_ref[h]
                qx = qt * xi_ref[h]
                dv = _mm_tn(scores_d, d_out) + _mm(kz, g)
                dqt = (_mm(dscores, kt) + _mm_nt(d_out, prev) * xi_ref[h]) * QK_SCALE
                dkt = _mm_tn(dscores, qt) + _mm_nt(v, g) * zeta_ref[h]
                gstate[h] = _mm_tn(qx, d_out) + CHUNK_DECAY[h] * g
                dq = dqt * cs + _rot(dqt * sn)
                dk = dkt * cs + _rot(dkt * sn)
                dpj_ref[rows, h * QK_DIM:(h + 1) * QK_DIM] = dq.astype(dpj_ref.dtype)
                dpj_ref[rows, RET_HEADS * QK_DIM + h * QK_DIM:RET_HEADS * QK_DIM + (h + 1) * QK_DIM] = dk.astype(
                    dpj_ref.dtype)
                dpj_ref[rows, 2 * RET_HEADS * QK_DIM + h * V_DIM:2 * RET_HEADS * QK_DIM + (h + 1) * V_DIM] = dv.astype(
                    dpj_ref.dtype)

    row = lambda b, n: (b * nb + nb - 1 - n, 0)
    pos = lambda b, n: (nb - 1 - n, 0)
    return _compute_call(
        body, name="retention_bwd", grid=(n_seq, nb),
        in_specs=[pl.BlockSpec((None, RET_TILE, SHARD_W), lambda b, n: (0,) + row(b, n)),
                  pl.BlockSpec((RET_TILE, D_MODEL), row),
                  pl.BlockSpec((cpb, RET_HEADS, QK_DIM, V_DIM), lambda b, n: (b * nb + nb - 1 - n, 0, 0, 0)),
                  pl.BlockSpec((RET_TILE, QK_DIM), pos), pl.BlockSpec((RET_TILE, QK_DIM), pos),
                  _resident(decay_in.shape), _resident(zeta.shape), _resident(xi.shape)],
        out_specs=[pl.BlockSpec((RET_TILE, SHARD_W), row)],
        out_shape=[jax.ShapeDtypeStruct((n_seq * seq, SHARD_W), MXU_DTYPE)],
        scratch_shapes=[pltpu.VMEM((RET_HEADS, QK_DIM, V_DIM), F32)],
        operands=(pj0, drr, states, cos_full, sin_signed, decay_in, zeta, xi),
        semantics=("arbitrary", "arbitrary"), side=side)


def _dx_call(dpj, x2d, wide, g_mixer, wg_in, side):
    t = x2d.shape[0]

    def body(d0, d1, d2, d3, x_ref, dx1_ref, g_ref, w_ref, dx_ref, dg_ref):
        @pl.when(pl.program_id(0) == 0)
        def _():
            dg_ref[...] = jnp.zeros_like(dg_ref)

        dh = _mm_nt(d0[...], w_ref[0])
        for j, d_ref in enumerate((d1, d2, d3)):
            dh += _mm_nt(d_ref[...], w_ref[j + 1])
        xn, r = _rms(x_ref[...])
        dg_ref[...] += jnp.sum(dh * xn, axis=0, keepdims=True)
        dx_ref[...] = dx1_ref[...] + _rms_bwd(dh * g_ref[...], xn, r)

    return _compute_call(
        body, name="dx_bwd", grid=(t // DX_TILE,),
        in_specs=[_row_tile(SHARD_W, DX_TILE)] * N_CHIPS
        + [_row_tile(D_MODEL, DX_TILE), pl.BlockSpec((None, DX_TILE, D_MODEL), lambda i: (0, i, 0))]
        + [_resident((1, D_MODEL)), _resident(wg_in.shape)],
        out_specs=[_row_tile(D_MODEL, DX_TILE), pl.BlockSpec((1, D_MODEL), lambda i: (0, 0))],
        out_shape=[jax.ShapeDtypeStruct((t, D_MODEL), F32), jax.ShapeDtypeStruct((1, D_MODEL), F32)],
        operands=(*dpj, x2d, wide, g_mixer, wg_in), semantics=("arbitrary",), side=side)


def _wgrad_call(name, lhs, rhs, block_n, out_cols=None, block_t=1024, into=None, slot=0, n_slots=1, side=None):
    (lhs, lhs_at), (rhs, rhs_at) = [x if isinstance(x, tuple) else (x, None) for x in (lhs, rhs)]
    t, n = rhs.shape[-2:]
    k = lhs.shape[-1]
    block_t = min(block_t, t)

    def operand_spec(at, block, index):
        if at is None:
            return pl.BlockSpec(block, index)
        return pl.BlockSpec((None,) + block, lambda j, i: (at,) + index(j, i))

    steps = t // block_t
    out_cols = block_n if out_cols is None else out_cols
    per = block_n // out_cols
    first_block = slot * (n // block_n)

    def body(l_ref, r_ref, *rest):
        o_ref, acc = rest[-2:]
        @pl.when(pl.program_id(1) == 0)
        def _():
            acc[...] = jnp.zeros_like(acc)

        acc[...] += _mm_tn(l_ref[...], r_ref[...])

        @pl.when(pl.program_id(1) == steps - 1)
        def _():
            for s in range(per):
                o_ref[s] = acc[:, s * out_cols:(s + 1) * out_cols].astype(o_ref.dtype)

    res = _compute_call(
        body, name=name, grid=(n // block_n, steps),
        in_specs=[operand_spec(lhs_at, (block_t, k), lambda j, i: (i, 0)),
                  operand_spec(rhs_at, (block_t, block_n), lambda j, i: (i, j))] + ([] if into is None else [ANY]),
        out_specs=[pl.BlockSpec((per, k, out_cols), lambda j, i: (first_block + j, 0, 0))],
        out_shape=[jax.ShapeDtypeStruct((n_slots * (n // out_cols), k, out_cols), COMM_DTYPE)],
        scratch_shapes=[pltpu.VMEM((k, block_n), F32)], aliases={} if into is None else {2: 0},
        operands=(lhs, rhs) if into is None else (lhs, rhs, into), semantics=("arbitrary", "arbitrary"), side=side)
    return res[0] if side is None else (res[0][0], res[1])


def _position():
    return lax.axis_index("x"), lax.axis_index("y"), lax.axis_index("c")


def _position_array():
    x, y, c = _position()
    return jnp.stack([2 * x + y, c]).astype(jnp.int32)


def _other_chip(x, y, k):
    return (1 - x if k & 2 else x), (1 - y if k & 1 else y)


def _plan_side(operands, out_shapes, plan, n_remote, aliases=None):
    def copies(ins, outs, send_sem, recv_sem, base=0):
        remote = plan(ins, outs)
        assert len(remote) == n_remote
        return [pltpu.make_async_remote_copy(src_ref=src, dst_ref=dst, send_sem=send_sem.at[base + i],
                                             recv_sem=recv_sem.at[base + i], device_id=dev, device_id_type=MESH)
                for i, (src, dst, dev) in enumerate(remote)]

    def start(*a):
        for cp in copies(*a):
            cp.start()

    def finish(*a):
        for cp in copies(*a):
            cp.wait()

    return _Side(operands, out_shapes, n_remote, start, finish, aliases)


def _place_cast_call(name, shards, pos):
    n = len(shards)
    rows, width = shards[0].shape
    block_rows = min(256, rows)

    def body(pos_ref, *refs):
        for w_ref, o_ref in zip(refs[:n], refs[n:]):
            o_ref[...] = w_ref[...].astype(o_ref.dtype)

    return pl.pallas_call(
        body, name=name,
        grid_spec=pltpu.PrefetchScalarGridSpec(
            num_scalar_prefetch=1, grid=(rows // block_rows,),
            in_specs=[pl.BlockSpec((block_rows, width), lambda i, pos: (i, 0))] * n,
            out_specs=[pl.BlockSpec((None, block_rows, width), lambda i, pos: (pos[0], i, 0))] * n),
        out_shape=[jax.ShapeDtypeStruct((N_CHIPS, rows, width), MXU_DTYPE)] * n,
        compiler_params=_params(("arbitrary",)),
    )(pos, *shards)


def _gather_side(placed):
    n = len(placed)

    def copies(kind, bufs, send_sem, recv_sem, base):
        x, y, c = _position()
        me = 2 * x + y
        made = []
        for i in range(n):
            hr = placed[i].shape[1] // 2
            for k in (1, 2, 3):
                px, py = _other_chip(x, y, k)
                chip, core, slot, dev = [(me, c, k - 1, (px, py, c)), (2 * px + py, c, 2 + k, (x, y, 1 - c)),
                                         (2 * px + py, 1 - c, 2 + k, (x, y, 1 - c))][kind]
                piece = bufs[i].at[chip, pl.ds(core * hr, hr)]
                made.append(pltpu.make_async_remote_copy(
                    src_ref=piece, dst_ref=piece, send_sem=send_sem.at[base + 6 * i + slot],
                    recv_sem=recv_sem.at[base + 6 * i + slot], device_id=dev, device_id_type=MESH))
        return made

    def start(ins, outs, send_sem, recv_sem, base=0):
        for cp in copies(0, outs, send_sem, recv_sem, base):
            cp.start()

    def finish(ins, outs, send_sem, recv_sem, base=0):
        first, onward = copies(0, outs, send_sem, recv_sem, base), copies(1, outs, send_sem, recv_sem, base)
        for landed, cp in zip(first, onward):
            landed.wait_recv()
            cp.start()
        for cp in copies(2, outs, send_sem, recv_sem, base):
            cp.wait_recv()
        for cp in first + onward:
            cp.wait_send()

    return _Side(placed, [jax.ShapeDtypeStruct(a.shape, a.dtype) for a in placed], 6 * n, start, finish,
                 {i: i for i in range(n)})


def _same_shape(shapes):
    found = {}
    for i, shape in enumerate(shapes):
        found.setdefault(shape, []).append(i)
    return found


def _sum_call(name, pos, groups, rows, width, out_dtype, block_rows, out_rows=None, out_index=None):
    out_rows = rows if out_rows is None else out_rows
    out_index = (lambda i, pos, br: i) if out_index is None else out_index
    sizes = [len(g) for g in groups]

    def body(pos_ref, *refs):
        ins, outs = refs[:sum(sizes)], refs[sum(sizes):]
        at = 0
        for size, out in zip(sizes, outs):
            acc = ins[at][...].astype(F32)
            for r in ins[at + 1:at + size]:
                acc = acc + r[...].astype(F32)
            out[...] = acc.astype(out_dtype)
            at += size

    def spec(index):
        return pl.BlockSpec((block_rows, width), lambda i, pos: (index(i, pos, block_rows), 0))

    return pl.pallas_call(
        body, name=name,
        grid_spec=pltpu.PrefetchScalarGridSpec(
            num_scalar_prefetch=1, grid=(rows // block_rows,),
            in_specs=[spec(index) for g in groups for _, index in g], out_specs=[spec(out_index)] * len(groups)),
        out_shape=[jax.ShapeDtypeStruct((out_rows, width), out_dtype)] * len(groups),
        compiler_params=_params(("arbitrary",)),
    )(pos, *[arr for g in groups for arr, _ in g])


def _pair_stage(tag, grads, pos):
    n = len(grads)
    shapes = [(g.shape[0] // N_CHIPS, g.shape[1]) for g in grads]

    def plan(ins, outs):
        x, y, c = _position()
        remote = []
        for i, (r, _) in enumerate(shapes):
            hr = r // 2
            for j in range(N_CHIPS):
                remote.append((ins[i].at[pl.ds(j * r + (1 - c) * hr, hr)], outs[i].at[pl.ds(j * hr, hr)],
                               (x, y, 1 - c)))
        return remote

    pair = _exchange_call(f"grad_pair_exchange_{tag}", _plan_side(
        grads, [jax.ShapeDtypeStruct((N_CHIPS * (r // 2), w), COMM_DTYPE) for r, w in shapes], plan, N_CHIPS * n))

    def own_half(r):
        def index(s, pos, br):
            per = (r // 2) // br
            return (s // per) * (r // br) + pos[1] * per + s % per
        return index

    sums = [None] * n
    for (r, w), which in _same_shape(shapes).items():
        groups = [[(grads[i], own_half(r)), (pair[i], lambda s, pos, br: s)] for i in which]
        outs = _sum_call(f"grad_pair_sum_{tag}{which[0]}", pos, groups, N_CHIPS * (r // 2), w, COMM_DTYPE,
                         min(256, r // 2))
        for i, out in zip(which, outs):
            sums[i] = out
    return sums


def _chip_side(pair_sums):
    halves = [(p.shape[0] // N_CHIPS, p.shape[1]) for p in pair_sums]

    def plan(ins, outs):
        x, y, c = _position()
        remote = []
        for i, (hr, _) in enumerate(halves):
            for k in (1, 2, 3):
                px, py = _other_chip(x, y, k)
                remote.append((ins[i].at[pl.ds((2 * px + py) * hr, hr)], outs[i].at[pl.ds((k - 1) * hr, hr)],
                               (px, py, c)))
        return remote

    return _plan_side(pair_sums, [jax.ShapeDtypeStruct((3 * hr, w), COMM_DTYPE) for hr, w in halves], plan,
                      3 * len(pair_sums))


def _finish_stage(tag, pair_sums, chip, pos, extra=None):
    n = len(pair_sums)
    halves = [(p.shape[0] // N_CHIPS, p.shape[1]) for p in pair_sums]

    def slab(k, hr):
        return lambda s, pos, br: k * (hr // br) + s

    reduced = [None] * n
    for (hr, w), which in _same_shape(halves).items():
        groups = [[(pair_sums[i], lambda s, pos, br: pos[0] * (hr // br) + s)] + [(chip[i], slab(k, hr)) for k in range(3)]
                  for i in which]
        outs = _sum_call(f"grad_chip_sum_{tag}{which[0]}", pos, groups, hr, w, F32, min(256, hr), out_rows=2 * hr,
                         out_index=lambda s, pos, br: pos[1] * (hr // br) + s)
        for i, out in zip(which, outs):
            reduced[i] = out

    def plan(ins, outs):
        x, y, c = _position()
        remote = []
        for i, (hr, _) in enumerate(halves):
            mine = outs[i].at[pl.ds(c * hr, hr)]
            remote.append((mine, mine, (x, y, 1 - c)))
        return remote

    swap = _plan_side(reduced, [jax.ShapeDtypeStruct((2 * hr, w), F32) for hr, w in halves], plan, n,
                      {i: i for i in range(n)})
    if extra is None:
        return _exchange_call(f"grad_half_exchange_{tag}", swap)
    res = _exchange_call(f"grad_half_exchange_{tag}", _join_sides(swap, extra))
    return res[:n], res[n:]


def _small_gather_side(parts):
    n = len(parts)

    def copies(ins, outs, send_sem, recv_sem, base):
        x, y, c = _position()
        made = []
        for i in range(n):
            mine = outs[i].at[4 * x + 2 * y + c]
            made.append(pltpu.make_async_copy(ins[i], mine, send_sem.at[base + 8 * i + 7]))
            for d in range(1, 8):
                px, py = _other_chip(x, y, d >> 1)
                made.append(pltpu.make_async_remote_copy(
                    src_ref=ins[i], dst_ref=mine, send_sem=send_sem.at[base + 8 * i + d - 1],
                    recv_sem=recv_sem.at[base + 8 * i + d - 1], device_id=(px, py, 1 - c if d & 1 else c),
                    device_id_type=MESH))
        return made

    def start(ins, outs, send_sem, recv_sem, base=0):
        for cp in copies(ins, outs, send_sem, recv_sem, base):
            cp.start()

    def finish(ins, outs, send_sem, recv_sem, base=0):
        for cp in copies(ins, outs, send_sem, recv_sem, base):
            cp.wait()

    return _Side(parts, [jax.ShapeDtypeStruct((8,) + a_.shape, F32) for a_ in parts], 8 * n, start, finish)


def _adamw(w, g, m, v):
    m = ADAM_B1 * m + (1.0 - ADAM_B1) * g
    v = ADAM_B2 * v + (1.0 - ADAM_B2) * (g * g)
    m_hat = m / (1.0 - ADAM_B1 ** ADAM_STEP)
    v_hat = v / (1.0 - ADAM_B2 ** ADAM_STEP)
    delta = -ADAM_LR * (m_hat / (jnp.sqrt(v_hat) + ADAM_EPS) + ADAM_WD * w)
    return delta, m, v


def _adamw_call(name, ws, gs, ms, vs):
    n = len(ws)
    rows, width = ws[0].shape
    block_rows = min(256 // n, rows)

    def body(*refs):
        for i in range(n):
            w_ref, g_ref, m_ref, v_ref = (refs[j * n + i] for j in range(4))
            d_out, m_out, v_out = (refs[(4 + j) * n + i] for j in range(3))
            d_out[...], m_out[...], v_out[...] = _adamw(w_ref[...], g_ref[...], m_ref[...], v_ref[...])

    spec = pl.BlockSpec((block_rows, width), lambda i: (i, 0))
    res = pl.pallas_call(
        body, name=name, grid=(rows // block_rows,), in_specs=[spec] * (4 * n), out_specs=[spec] * (3 * n),
        out_shape=[jax.ShapeDtypeStruct(ws[0].shape, F32)] * (3 * n),
        compiler_params=_params(("arbitrary",)),
    )(*ws, *gs, *ms, *vs)
    return [(res[i], res[n + i], res[2 * n + i]) for i in range(n)]


def _small_adamw_call(gathered, weights, moments_m, moments_v):
    n = len(weights)

    def body(*refs):
        all_refs, refs = refs[:n + 1], refs[n + 1:]
        w_refs, m_refs, v_refs, outs = refs[:n], refs[n:2 * n], refs[2 * n:3 * n], refs[3 * n:]

        def total(ref):
            acc = ref[0]
            for d in range(1, 8):
                acc = acc + ref[d]
            return acc

        outs[0][...] = total(all_refs[n])
        for i in range(n):
            g = total(all_refs[i])
            if i == 0:
                row = lax.broadcasted_iota(jnp.int32, g.shape, 0)
                col = lax.broadcasted_iota(jnp.int32, g.shape, 1)
                g = jnp.where((row % CHUNK) >= col, g, 0.0)
            g_out, d_out, m_out, v_out = outs[1 + 4 * i:5 + 4 * i]
            g_out[...] = g
            d_out[...], m_out[...], v_out[...] = _adamw(w_refs[i][...], g, m_refs[i][...], v_refs[i][...])

    out_shape = [jax.ShapeDtypeStruct(LOSS_TILE, F32)]
    for w in weights:
        out_shape += [jax.ShapeDtypeStruct(w.shape, F32)] * 4
    res = pl.pallas_call(
        body, name="small_adamw", out_shape=out_shape,
        compiler_params=pltpu.CompilerParams(vmem_limit_bytes=VMEM_LIMIT),
    )(*gathered, *weights, *moments_m, *moments_v)
    return res[0], [res[1 + 4 * i:5 + 4 * i] for i in range(n)]


def kernel(x, p, w_in, w_ret_out, w_sgu_out, w_out, sgu_ws, sgu_bs, w_ple_gate, w_ple_proj, g_mixer, g_ple, g_final, loss_target, m_w_in, m_w_ret_out, m_w_sgu_out, m_w_out, m_sgu_ws, m_sgu_bs, m_w_ple_gate, m_w_ple_proj, m_g_mixer, m_g_ple, m_g_final, v_w_in, v_w_ret_out, v_w_sgu_out, v_w_out, v_sgu_ws, v_sgu_bs, v_w_ple_gate, v_w_ple_proj, v_g_mixer, v_g_ple, v_g_final):
    n_seq, seq, _ = x.shape
    t = n_seq * seq
    x2d = x.reshape(t, D_MODEL)
    p2d = p.reshape(t, PLE_DIM)
    target = loss_target.reshape(t, D_MODEL)
    big = [w_in[0], w_ret_out[0], w_sgu_out[0], w_out[0], w_ple_gate[0], w_ple_proj[0]]
    big_m = [m_w_in[0], m_w_ret_out[0], m_w_sgu_out[0], m_w_out[0], m_w_ple_gate[0], m_w_ple_proj[0]]
    big_v = [v_w_in[0], v_w_ret_out[0], v_w_sgu_out[0], v_w_out[0], v_w_ple_gate[0], v_w_ple_proj[0]]

    pos = _position_array()
    placed = (_place_cast_call("place_w_in", big[:1], pos) + _place_cast_call("place_square_weights", big[1:5], pos)
              + _place_cast_call("place_w_ple_proj", big[5:], pos))
    ws = sgu_ws[0]
    bst = sgu_bs[0].T
    consts = _retention_consts(seq)

    (h, pj, wg_in), gathered = _proj_call(x2d, g_mixer, placed[0], pos, _gather_side(placed[1:]))
    pj0 = pj1 = pj2 = pj3 = pj
    w_ro, w_so, w_o, w_pg = (w.reshape(D_MODEL, D_MODEL) for w in gathered[:4])
    w_pp = gathered[4].transpose(1, 0, 2).reshape(PLE_DIM, D_MODEL)
    ret_raw, states = _retention_fwd_call(pj0, consts, n_seq, seq)
    wide, half, loss, dg_ple, dg_final = _tail_call(
        pj, ret_raw, x2d, p2d, target, ws, bst, g_ple, g_final.reshape(1, D_MODEL), w_ro, w_so, w_o, w_pg, w_pp)
    ret, sgu, merged, hp, dz, dpp = ((half, k) for k in range(6))
    dpj1, dpj2, dpj3, drr, da, db, dws, dbst = _merge_bwd_call(wide, pj1, pj2, pj3, ret_raw, ws, bst, w_ro, w_so, w_o)

    small_shapes = [(SGU_GROUPS * CHUNK, CHUNK), (SGU_GROUPS, CHUNK), (1, D_MODEL), (1, D_MODEL), (1, D_MODEL)]
    as_small = lambda arrays: [None if a_ is None else a_.reshape(s_) for a_, s_ in zip(arrays, small_shapes)]
    early = as_small([dws, dbst.T, None, dg_ple, dg_final])
    rows_of = lambda g: g.reshape(g.shape[0] * g.shape[1], g.shape[2])
    tail_grads = [
        rows_of(_wgrad_call("wgrad_ret_out", ret, da, D_MODEL)),
        rows_of(_wgrad_call("wgrad_sgu_out", sgu, db, D_MODEL)),
        rows_of(_wgrad_call("wgrad_out", merged, (wide, 0), D_MODEL)),
        rows_of(_wgrad_call("wgrad_ple_gate", hp, dz, D_MODEL)),
        rows_of(_wgrad_call("wgrad_ple_proj", p2d, dpp, D_MODEL, out_cols=PLE_DIM)),
    ]
    tail_sums = _pair_stage("tail", tail_grads, pos)
    (dpj0,), carried = _retention_bwd_call(
        pj0, drr, states, consts, n_seq, seq,
        _join_sides(_chip_side(tail_sums), _small_gather_side([early[0], early[1], early[3], early[4], loss])))
    tail_chip, early_all = carried[:len(tail_sums)], carried[len(tail_sums):]
    in_grad = None
    for j, d in enumerate((dpj0, dpj1, dpj2, dpj3)):
        in_grad = _wgrad_call(f"wgrad_in_{j}", h, d, SHARD_W, into=in_grad, slot=j, n_slots=N_CHIPS)
    in_sums = _pair_stage("in", [rows_of(in_grad)], pos)
    (dx, dg_mixer), in_chip = _dx_call((dpj0, dpj1, dpj2, dpj3), x2d, wide, g_mixer, wg_in, _chip_side(in_sums))
    g_big, (mixer_all,) = _finish_stage("all", in_sums + tail_sums, list(in_chip) + list(tail_chip), pos,
                                        _small_gather_side([dg_mixer]))
    upd = [None] * len(big)
    for _, which in _same_shape([w.shape for w in big]).items():
        pick = lambda arrays: [arrays[i] for i in which]
        for i, triple in zip(which, _adamw_call(f"adamw_{which[0]}", pick(big), pick(g_big), pick(big_m), pick(big_v))):
            upd[i] = triple

    small_g = [early_all[0], early_all[1], mixer_all, early_all[2], early_all[3], early_all[4]]
    total, small = _small_adamw_call(small_g, as_small([sgu_ws, sgu_bs, g_mixer, g_ple, g_final]),
                                     as_small([m_sgu_ws, m_sgu_bs, m_g_mixer, m_g_ple, m_g_final]),
                                     as_small([v_sgu_ws, v_sgu_bs, v_g_mixer, v_g_ple, v_g_final]))
    out_small_shapes = [sgu_ws.shape, sgu_bs.shape, g_mixer.shape, g_ple.shape, g_final.shape]

    def ordered(big_list, kind):
        w_in_, w_ro_, w_so_, w_o_, w_pg_, w_pp_ = [b_[None] for b_ in big_list]
        s_ws, s_bs, s_gm, s_gp, s_gf = [small[i][kind].reshape(s) for i, s in enumerate(out_small_shapes)]
        return [w_in_, w_ro_, w_so_, w_o_, s_ws, s_bs, w_pg_, w_pp_, s_gm, s_gp, s_gf]

    out = [total[0, 0], dx.reshape(x.shape)]
    out += ordered(g_big, 0)
    out += ordered([u[0] for u in upd], 1)
    out += ordered([u[1] for u in upd], 2)
    out += ordered([u[2] for u in upd], 3)
    return tuple(out)
```

```python
import functools
import math

import numpy as np
import jax
import jax.numpy as jnp
from jax import lax
from jax.experimental import pallas as pl
from jax.experimental.pallas import tpu as pltpu

F32 = jnp.float32
MXU_DTYPE = jnp.bfloat16
COMM_DTYPE = jnp.bfloat16

D_MODEL = 1024
RET_HEADS = 4
QK_DIM = 128
V_DIM = 256
CHUNK = 128
SGU_GROUPS = 4
GROUP_DIM = 256
PLE_DIM = 256
N_CHIPS = 4
SHARD_W = 2048
ROPE_BASE = 10000.0
NORM_EPS = 1e-6
GN_EPS = 1e-5
QK_SCALE = QK_DIM ** -0.5
SQRT_HALF = math.sqrt(0.5)
INV_SQRT_2PI = 1.0 / math.sqrt(2.0 * math.pi)

ADAM_LR = 0.001
ADAM_B1 = 0.9
ADAM_B2 = 0.999
ADAM_EPS = 1e-08
ADAM_WD = 0.01
ADAM_STEP = 10

TILE_M = 256
LOSS_TILE = (8, 128)
LOAD_PARTS = 4
PAIR_CHUNK_ROWS = 256
PROJ_TILE = 512
RET_TILE = 512
DX_TILE = 512
VMEM_LIMIT = 56 * 1024 * 1024
MESH = pl.DeviceIdType.MESH
ANY = pl.BlockSpec(memory_space=pl.ANY)

CHUNK_DECAY = tuple(
    float(np.exp(np.float32(CHUNK) * np.log(np.float32(1.0 - 2.0 ** (-5.0 - h))))) for h in range(RET_HEADS))


def _mm(a, b):
    return jnp.dot(a.astype(MXU_DTYPE), b.astype(MXU_DTYPE), preferred_element_type=F32)


def _mm_nt(a, b):
    return lax.dot_general(a.astype(MXU_DTYPE), b.astype(MXU_DTYPE), (((1,), (1,)), ((), ())),
                           preferred_element_type=F32)


def _mm_tn(a, b):
    return lax.dot_general(a.astype(MXU_DTYPE), b.astype(MXU_DTYPE), (((0,), (0,)), ((), ())),
                           preferred_element_type=F32)


def _mean(x):
    return jnp.mean(x, axis=-1, keepdims=True)


def _sigmoid(x):
    return jax.nn.sigmoid(x)


def _silu_and_grad(x):
    s = _sigmoid(x)
    silu = x * s
    return silu, s + silu * (1.0 - s)


def _gelu_and_grad(x):
    cdf = 0.5 + 0.5 * lax.erf(x * SQRT_HALF)
    return x * cdf, cdf + x * (jnp.exp(x * x * -0.5) * INV_SQRT_2PI)


def _unit_norm(x, eps):
    xc = x - _mean(x)
    rstd = lax.rsqrt(_mean(xc * xc) + eps)
    return xc * rstd, rstd


def _unit_norm_bwd(dn, n, rstd):
    return rstd * (dn - _mean(dn) - n * _mean(dn * n))


def _rms(x):
    r = lax.rsqrt(_mean(x * x) + NORM_EPS)
    return x * r, r


def _rms_bwd(dxn, xn, r):
    return r * (dxn - xn * _mean(dxn * xn))


def _rot(x):
    return pltpu.roll(x, QK_DIM // 2, 1)


def _params(semantics, **kw):
    return pltpu.CompilerParams(dimension_semantics=semantics, vmem_limit_bytes=VMEM_LIMIT, **kw)


def _row_tile(width, tm=TILE_M):
    return pl.BlockSpec((tm, width), lambda i: (i, 0))


def _proj_tile(j):
    return pl.BlockSpec((None, TILE_M, SHARD_W), lambda i: (j, i, 0))


def _resident(shape):
    nd = len(shape)
    return pl.BlockSpec(shape, lambda *_: (0,) * nd, pipeline_mode=pl.Buffered(1))


def _causal_ws(ws_ref):
    row = lax.broadcasted_iota(jnp.int32, (CHUNK, CHUNK), 0)
    col = lax.broadcasted_iota(jnp.int32, (CHUNK, CHUNK), 1)
    return [jnp.where(row >= col, ws_ref[g], 0.0).astype(MXU_DTYPE) for g in range(SGU_GROUPS)]


def _heads(x, width):
    return [x[:, h * width:(h + 1) * width] for h in range(x.shape[1] // width)]


def _branch_forward(pj1, pj2, ret_raw, wsc, bst, mixed_ref):
    rg, su = pj1[:, :D_MODEL], pj1[:, D_MODEL:]
    sv, sg = pj2[:, :D_MODEL], pj2[:, D_MODEL:]
    rn_parts, rstd_parts = zip(*[_unit_norm(r, GN_EPS) for r in _heads(ret_raw, V_DIM)])
    rn = jnp.concatenate(rn_parts, axis=1)
    sil_rg, dsil_rg = _silu_and_grad(rg)
    ret = rn * sil_rg
    u, du = _gelu_and_grad(su)
    gelu_sv, dgelu_sv = _gelu_and_grad(sv)
    vn, rstd_v = _unit_norm(gelu_sv, GN_EPS)
    for cc in range(pj1.shape[0] // CHUNK):
        for g in range(SGU_GROUPS):
            rs, cs = slice(cc * CHUNK, (cc + 1) * CHUNK), slice(g * GROUP_DIM, (g + 1) * GROUP_DIM)
            mixed_ref[rs, cs] = _mm(wsc[g], vn[rs, cs]) + bst[:, g:g + 1]
    mixed = mixed_ref[...]
    sil_sg, dsil_sg = _silu_and_grad(sg)
    sgu = u * mixed * sil_sg
    return dict(rn=rn, rstd_r=rstd_parts, sil_rg=sil_rg, dsil_rg=dsil_rg, ret=ret, u=u, du=du, dgelu_sv=dgelu_sv,
                vn=vn, rstd_v=rstd_v, mixed=mixed, sil_sg=sil_sg, dsil_sg=dsil_sg, sgu=sgu)


class _Side:
    def __init__(self, operands, out_shapes, n_sems, start, finish, aliases=None):
        self.operands, self.out_shapes, self.n_sems = list(operands), list(out_shapes), n_sems
        self.start, self.finish, self.aliases = start, finish, dict(aliases or {})


def _join_sides(*sides):
    spans, a, b, s = [], 0, 0, 0
    for side in sides:
        spans.append((a, b, s))
        a, b, s = a + len(side.operands), b + len(side.out_shapes), s + side.n_sems

    def run(which):
        def go(ins, outs, send_sem, recv_sem, base=0):
            for side, (a0, b0, s0) in zip(sides, spans):
                getattr(side, which)(ins[a0:a0 + len(side.operands)], outs[b0:b0 + len(side.out_shapes)],
                                     send_sem, recv_sem, base + s0)
        return go

    aliases = {a0 + i: b0 + o for side, (a0, b0, _) in zip(sides, spans) for i, o in side.aliases.items()}
    return _Side([x for side in sides for x in side.operands], [x for side in sides for x in side.out_shapes], s,
                 run("start"), run("finish"), aliases)


def _compute_call(body, *, name, grid, in_specs, out_specs, out_shape, operands, semantics, scratch_shapes=(),
                  side=None, prefetch=None, aliases=None, side_start=None):
    n_pre = 0 if prefetch is None else 1
    pre = () if prefetch is None else (prefetch,)

    def spec(in_specs, out_specs, scratch):
        return pltpu.PrefetchScalarGridSpec(num_scalar_prefetch=n_pre, grid=grid, in_specs=in_specs,
                                            out_specs=out_specs, scratch_shapes=scratch)

    if side is None:
        return pl.pallas_call(body, name=name, grid_spec=spec(in_specs, out_specs, list(scratch_shapes)),
                              out_shape=out_shape,
                              input_output_aliases={n_pre + a: b for a, b in (aliases or {}).items()},
                              compiler_params=_params(semantics))(*pre, *operands)
    n_in, n_out, s_in, s_out = len(operands), len(out_shape), len(side.operands), len(side.out_shapes)

    def carrier(*refs):
        pre_refs, refs = refs[:n_pre], refs[n_pre:]
        ins, refs = refs[:n_in], refs[n_in:]
        side_ins, refs = refs[:s_in], refs[s_in:]
        outs, refs = refs[:n_out], refs[n_out:]
        side_outs, refs = refs[:s_out], refs[s_out:]
        scratch, (send_sem, recv_sem) = refs[:-2], refs[-2:]
        ids = [pl.program_id(a) for a in range(len(grid))]
        at = (0,) * len(grid) if side_start is None else side_start
        first = functools.reduce(jnp.logical_and, [i == a for i, a in zip(ids, at)])
        last = functools.reduce(jnp.logical_and, [i == g - 1 for i, g in zip(ids, grid)])

        @pl.when(first)
        def _():
            side.start(side_ins, side_outs, send_sem, recv_sem)

        body(*pre_refs, *ins, *outs, *scratch)

        @pl.when(last)
        def _():
            side.finish(side_ins, side_outs, send_sem, recv_sem)

    all_aliases = {n_pre + a: b for a, b in (aliases or {}).items()}
    all_aliases.update({n_pre + n_in + a: n_out + b for a, b in side.aliases.items()})
    res = pl.pallas_call(
        carrier, name=name,
        grid_spec=spec(list(in_specs) + [ANY] * s_in, list(out_specs) + [ANY] * s_out,
                       list(scratch_shapes) + [pltpu.SemaphoreType.DMA((side.n_sems,))] * 2),
        out_shape=list(out_shape) + side.out_shapes, input_output_aliases=all_aliases,
        compiler_params=_params(semantics, has_side_effects=True),
    )(*pre, *operands, *side.operands)
    return res[:n_out], res[n_out:]


def _exchange_call(name, side):
    s_in = len(side.operands)

    def body(*refs):
        ins, outs = refs[:s_in], refs[s_in:s_in + len(side.out_shapes)]
        send_sem, recv_sem = refs[s_in + len(side.out_shapes):]
        side.start(ins, outs, send_sem, recv_sem)
        side.finish(ins, outs, send_sem, recv_sem)

    return pl.pallas_call(
        body, name=name, in_specs=[ANY] * s_in, out_specs=[ANY] * len(side.out_shapes), out_shape=side.out_shapes,
        scratch_shapes=[pltpu.SemaphoreType.DMA((side.n_sems,))] * 2, input_output_aliases=side.aliases,
        compiler_params=pltpu.CompilerParams(has_side_effects=True),
    )(*side.operands)


def _proj_call(x2d, g_mixer, placed_in, pos, side):
    t = x2d.shape[0]
    nt = t // PROJ_TILE
    hr = placed_in.shape[1] // 2

    def body(pos_ref, x_ref, g_ref, win_ref, h_ref, pj_ref, w_ref, w_vmem, h_all, loc_sem, send_sem, recv_sem):
        k, i = pl.program_id(0), pl.program_id(1)
        x, y, c = _position()
        me = 2 * x + y

        def copy(slot, chip, core, dev):
            piece = w_ref.at[chip, pl.ds(core * hr, hr)]
            return pltpu.make_async_remote_copy(src_ref=piece, dst_ref=piece, send_sem=send_sem.at[slot],
                                                recv_sem=recv_sem.at[slot], device_id=dev, device_id_type=MESH)

        def first_hop(kk):
            px, py = _other_chip(x, y, kk)
            return copy(kk - 1, me, c, (px, py, c))

        def relay():
            source = jnp.bitwise_xor(me, 2 - c)
            return copy(2, source, c, (jnp.bitwise_xor(x, c), jnp.bitwise_xor(y, 1 - c), c))

        def onward(kk, core):
            px, py = _other_chip(x, y, kk)
            return copy(2 + kk, 2 * px + py, core, (x, y, 1 - c))

        def loads(kk):
            px, py = _other_chip(x, y, kk)
            rows = w_vmem.shape[1] // LOAD_PARTS
            return [pltpu.make_async_copy(w_ref.at[2 * px + py, pl.ds(r * rows, rows)],
                                          w_vmem.at[kk % 2, pl.ds(r * rows, rows)], loc_sem.at[r])
                    for r in range(LOAD_PARTS)]

        @pl.when(jnp.logical_and(k == 0, i == 0))
        def _():
            for kk in (1, 2):
                first_hop(kk).start()
            for cp in loads(0):
                cp.start()
            for cp in loads(0):
                cp.wait()

        for kk in (1, 2, 3):
            @pl.when(jnp.logical_and(k == kk - 1, i == nt - 1))
            def _(kk=kk):
                if kk == 1:
                    first_hop(1).wait_recv()
                    first_hop(2).wait_recv()
                    relay().start()
                if kk == 3:
                    relay().wait_recv()
                onward(kk, c).start()
                onward(kk, 1 - c).wait_recv()
                for cp in loads(kk):
                    cp.start()

            @pl.when(jnp.logical_and(k == kk, i == 0))
            def _(kk=kk):
                for cp in loads(kk):
                    cp.wait()

        rows = pl.ds(pl.multiple_of(i * PROJ_TILE, PROJ_TILE), PROJ_TILE)

        @pl.when(k == 0)
        def _():
            xn, _ = _rms(x_ref[...])
            h = (xn * g_ref[...]).astype(MXU_DTYPE)
            h_all[rows, :] = h
            h_ref[...] = h

        pj_ref[...] = jnp.dot(h_all[rows, :], w_vmem[k % 2], preferred_element_type=F32)

        @pl.when(jnp.logical_and(k == N_CHIPS - 1, i == nt - 1))
        def _():
            for cp in [first_hop(1), first_hop(2), relay()] + [onward(kk, c) for kk in (1, 2, 3)]:
                cp.wait_send()

    parked = lambda k, i, pos: (jnp.where(k == 0, i, nt - 1), 0)
    return _compute_call(
        body, name="proj_fwd", grid=(N_CHIPS, nt),
        in_specs=[pl.BlockSpec((PROJ_TILE, D_MODEL), parked), pl.BlockSpec((1, D_MODEL), lambda k, i, pos: (0, 0)), ANY],
        out_specs=[pl.BlockSpec((PROJ_TILE, D_MODEL), lambda k, i, pos: (jnp.where(k == 0, i, nt), 0)),
                   pl.BlockSpec((None, PROJ_TILE, SHARD_W), lambda k, i, pos: (jnp.bitwise_xor(pos[0], k), i, 0)),
                   ANY],
        out_shape=[jax.ShapeDtypeStruct((t + PROJ_TILE, D_MODEL), MXU_DTYPE),
                   jax.ShapeDtypeStruct((N_CHIPS, t, SHARD_W), F32),
                   jax.ShapeDtypeStruct(placed_in.shape, placed_in.dtype)],
        scratch_shapes=[pltpu.VMEM((2,) + placed_in.shape[1:], placed_in.dtype), pltpu.VMEM((t, D_MODEL), MXU_DTYPE),
                        pltpu.SemaphoreType.DMA((LOAD_PARTS,)), pltpu.SemaphoreType.DMA((6,)),
                        pltpu.SemaphoreType.DMA((6,))],
        operands=(x2d, g_mixer, placed_in), semantics=("arbitrary", "arbitrary"), side=side, prefetch=pos,
        aliases={2: 2}, side_start=(2, 0))


def _retention_consts(seq):
    half = QK_DIM // 2
    inv = ROPE_BASE ** (-jnp.arange(half, dtype=F32) / half)
    inv = jnp.concatenate([inv, inv])
    coarse = (jnp.arange(seq // half, dtype=F32) * half)[:, None] * inv[None, :]
    fine = jnp.arange(half, dtype=F32)[:, None] * inv[None, :]
    ca, sa, cb, sb = jnp.cos(coarse)[:, None], jnp.sin(coarse)[:, None], jnp.cos(fine)[None], jnp.sin(fine)[None]
    cos_full = (ca * cb - sa * sb).reshape(seq, QK_DIM)
    sign = jnp.concatenate([-jnp.ones((half,), F32), jnp.ones((half,), F32)])
    sin_signed = (sa * cb + ca * sb).reshape(seq, QK_DIM) * sign
    log_g = jnp.log(1.0 - 2.0 ** (-5.0 - jnp.arange(RET_HEADS, dtype=F32)))
    idx = jnp.arange(CHUNK, dtype=F32)
    diff = idx[:, None] - idx[None, :]
    decay_in = jnp.where(diff[None] >= 0, jnp.exp(jnp.maximum(diff, 0.0)[None] * log_g[:, None, None]), 0.0)
    zeta = jnp.exp((CHUNK - 1.0 - idx)[None, :] * log_g[:, None])
    xi = jnp.exp((idx + 1.0)[None, :] * log_g[:, None])
    zeta = jnp.broadcast_to(zeta[:, :, None], (RET_HEADS, CHUNK, QK_DIM))
    xi = jnp.broadcast_to(xi[:, :, None], (RET_HEADS, CHUNK, QK_DIM))
    return cos_full, sin_signed, decay_in, zeta, xi


def _qkv(pj_ref, rows, h):
    q = pj_ref[rows, h * QK_DIM:(h + 1) * QK_DIM]
    k = pj_ref[rows, RET_HEADS * QK_DIM + h * QK_DIM:RET_HEADS * QK_DIM + (h + 1) * QK_DIM]
    v = pj_ref[rows, 2 * RET_HEADS * QK_DIM + h * V_DIM:2 * RET_HEADS * QK_DIM + (h + 1) * V_DIM]
    return q, k, v


def _retention_fwd_call(pj0, consts, n_seq, seq):
    cos_full, sin_signed, decay_in, zeta, xi = consts
    nb = seq // RET_TILE
    cpb = RET_TILE // CHUNK

    def body(pj_ref, cos_ref, sin_ref, d_ref, zeta_ref, xi_ref, o_ref, st_ref, state):
        @pl.when(pl.program_id(1) == 0)
        def _():
            state[...] = jnp.zeros_like(state)

        for cc in range(cpb):
            rows = slice(cc * CHUNK, (cc + 1) * CHUNK)
            cs, sn = cos_ref[rows, :], sin_ref[rows, :]
            for h in range(RET_HEADS):
                q, k, v = _qkv(pj_ref, rows, h)
                qt = (q * cs + _rot(q) * sn) * QK_SCALE
                kt = k * cs + _rot(k) * sn
                prev = state[h]
                st_ref[cc, h] = prev.astype(st_ref.dtype)
                scores = _mm_nt(qt, kt) * d_ref[h]
                o_ref[rows, h * V_DIM:(h + 1) * V_DIM] = _mm(scores, v) + _mm(qt * xi_ref[h], prev)
                state[h] = _mm_tn(kt * zeta_ref[h], v) + CHUNK_DECAY[h] * prev

    row = lambda b, n: (b * nb + n, 0)
    pos = lambda b, n: (n, 0)
    return pl.pallas_call(
        body, name="retention_fwd", grid=(n_seq, nb),
        in_specs=[pl.BlockSpec((None, RET_TILE, SHARD_W), lambda b, n: (0,) + row(b, n)),
                  pl.BlockSpec((RET_TILE, QK_DIM), pos),
                  pl.BlockSpec((RET_TILE, QK_DIM), pos), _resident(decay_in.shape), _resident(zeta.shape),
                  _resident(xi.shape)],
        out_specs=[pl.BlockSpec((RET_TILE, D_MODEL), row),
                   pl.BlockSpec((cpb, RET_HEADS, QK_DIM, V_DIM), lambda b, n: (b * nb + n, 0, 0, 0))],
        out_shape=[jax.ShapeDtypeStruct((n_seq * seq, D_MODEL), F32),
                   jax.ShapeDtypeStruct((n_seq * seq // CHUNK, RET_HEADS, QK_DIM, V_DIM), MXU_DTYPE)],
        scratch_shapes=[pltpu.VMEM((RET_HEADS, QK_DIM, V_DIM), F32)],
        compiler_params=_params(("arbitrary", "arbitrary")),
    )(pj0, cos_full, sin_signed, decay_in, zeta, xi)


def _merge_bwd_call(wide, pj1, pj2, pj3, ret_raw, ws, bst, w_ro, w_so, w_o):
    t = wide.shape[1]

    def body(wide_ref, pj1_ref, pj2_ref, pj3_ref, rr_ref, ws_ref, bst_ref, wro_ref, wso_ref, wo_ref,
             dpj1_ref, dpj2_ref, dpj3_ref, drr_ref, da_ref, db_ref, dws_ref, dbst_ref, mixed_ref):
        @pl.when(pl.program_id(0) == 0)
        def _():
            dws_ref[...] = jnp.zeros_like(dws_ref)
            dbst_ref[...] = jnp.zeros_like(dbst_ref)

        wsc = _causal_ws(ws_ref)
        f = _branch_forward(pj1_ref[...], pj2_ref[...], rr_ref[...], wsc, bst_ref[...], mixed_ref)
        pj3 = pj3_ref[...]
        smr, sms = _sigmoid(pj3[:, :D_MODEL]), _sigmoid(pj3[:, D_MODEL:])
        dmerged = _mm_nt(wide_ref[0], wo_ref[...])
        da_f, db_f = dmerged * smr, dmerged * sms
        da, db = da_f.astype(MXU_DTYPE), db_f.astype(MXU_DTYPE)
        dpj3_ref[:, :D_MODEL] = (da_f * wide_ref[1] * (1.0 - smr)).astype(dpj3_ref.dtype)
        dpj3_ref[:, D_MODEL:] = (db_f * wide_ref[2] * (1.0 - sms)).astype(dpj3_ref.dtype)
        da_ref[...] = da
        db_ref[...] = db

        dret = _mm_nt(da, wro_ref[...])
        dpj1_ref[:, :D_MODEL] = (dret * f["rn"] * f["dsil_rg"]).astype(dpj1_ref.dtype)
        drn = dret * f["sil_rg"]
        for h in range(RET_HEADS):
            cols = slice(h * V_DIM, (h + 1) * V_DIM)
            drr_ref[:, cols] = _unit_norm_bwd(drn[:, cols], f["rn"][:, cols], f["rstd_r"][h]).astype(drr_ref.dtype)

        dsgu = _mm_nt(db, wso_ref[...])
        dpj2_ref[:, D_MODEL:] = (dsgu * f["u"] * f["mixed"] * f["dsil_sg"]).astype(dpj2_ref.dtype)
        tg = dsgu * f["sil_sg"]
        dpj1_ref[:, D_MODEL:] = (tg * f["mixed"] * f["du"]).astype(dpj1_ref.dtype)
        dmixed = tg * f["u"]
        for cc in range(TILE_M // CHUNK):
            for g in range(SGU_GROUPS):
                rs, cs = slice(cc * CHUNK, (cc + 1) * CHUNK), slice(g * GROUP_DIM, (g + 1) * GROUP_DIM)
                dm = dmixed[rs, cs]
                mixed_ref[rs, cs] = _mm_tn(wsc[g], dm)
                dws_ref[g] += _mm_nt(dm, f["vn"][rs, cs])
                dbst_ref[:, g:g + 1] += jnp.sum(dm, axis=1, keepdims=True)
        dvv = _unit_norm_bwd(mixed_ref[...], f["vn"], f["rstd_v"])
        dpj2_ref[:, :D_MODEL] = (dvv * f["dgelu_sv"]).astype(dpj2_ref.dtype)

    sq = (D_MODEL, D_MODEL)
    return pl.pallas_call(
        body, name="merge_bwd", grid=(t // TILE_M,),
        in_specs=[pl.BlockSpec((3, TILE_M, D_MODEL), lambda i: (0, i, 0))] + [_proj_tile(j) for j in (1, 2, 3)]
        + [_row_tile(D_MODEL)] + [_resident(ws.shape), _resident(bst.shape), _resident(sq), _resident(sq), _resident(sq)],
        out_specs=[_row_tile(SHARD_W)] * 3 + [_row_tile(D_MODEL)] * 3
        + [pl.BlockSpec(ws.shape, lambda i: (0, 0, 0)), pl.BlockSpec(bst.shape, lambda i: (0, 0))],
        out_shape=[jax.ShapeDtypeStruct((t, SHARD_W), MXU_DTYPE)] * 3 + [jax.ShapeDtypeStruct((t, D_MODEL), MXU_DTYPE)] * 3
        + [jax.ShapeDtypeStruct(ws.shape, F32), jax.ShapeDtypeStruct(bst.shape, F32)],
        scratch_shapes=[pltpu.VMEM((TILE_M, D_MODEL), F32)], compiler_params=_params(("arbitrary",)),
    )(wide, pj1, pj2, pj3, ret_raw, ws, bst, w_ro, w_so, w_o)


def _tail_call(pj, ret_raw, x2d, p2d, target, ws, bst, g_ple, g_final, w_ro, w_so, w_o, w_pg, w_pp):
    t = x2d.shape[0]
    tm = TILE_M

    def body(pj1_ref, pj2_ref, pj3_ref, rr_ref, x_ref, p_ref, t_ref, ws_ref, bst_ref, gp_ref, gf_ref,
             wro_ref, wso_ref, wo_ref, wpg_ref, wpp_ref,
             wide_ref, half_ref, loss_ref, dgp_ref, dgf_ref, mixed_ref):
        dx1_ref, a_ref, b_ref = (wide_ref.at[k] for k in range(3))
        ret_ref, sgu_ref, mg_ref, hp_ref, dz_ref, dpp_ref = (half_ref.at[k] for k in range(6))
        @pl.when(pl.program_id(0) == 0)
        def _():
            for acc in (loss_ref, dgp_ref, dgf_ref):
                acc[...] = jnp.zeros_like(acc)

        f = _branch_forward(pj1_ref[...], pj2_ref[...], rr_ref[...], _causal_ws(ws_ref), bst_ref[...], mixed_ref)
        ret = f["ret"].astype(MXU_DTYPE)
        sgu = f["sgu"].astype(MXU_DTYPE)
        ret_ref[...] = ret
        sgu_ref[...] = sgu
        a = jnp.dot(ret, wro_ref[...], preferred_element_type=F32)
        b = jnp.dot(sgu, wso_ref[...], preferred_element_type=F32)
        a_ref[...] = a
        b_ref[...] = b
        pj3 = pj3_ref[...]
        smr, sms = _sigmoid(pj3[:, :D_MODEL]), _sigmoid(pj3[:, D_MODEL:])
        merged = (smr * a + sms * b).astype(MXU_DTYPE)
        mg_ref[...] = merged
        x1v = x_ref[...] + jnp.dot(merged, wo_ref[...], preferred_element_type=F32)
        xn1, r1 = _rms(x1v)
        hp = (xn1 * gp_ref[...]).astype(MXU_DTYPE)
        hp_ref[...] = hp
        gate = _sigmoid(jnp.dot(hp, wpg_ref[...], preferred_element_type=F32))
        pp = jnp.dot(p_ref[...].astype(MXU_DTYPE), wpp_ref[...], preferred_element_type=F32)
        xn2, r2 = _rms(x1v + gate * pp)
        err = xn2 * gf_ref[...] - t_ref[...]
        loss_ref[...] += (0.5 / D_MODEL) * jnp.sum(jnp.sum(err * err, axis=1, keepdims=True), axis=0, keepdims=True)

        dy = err * (1.0 / D_MODEL)
        dgf_ref[...] += jnp.sum(dy * xn2, axis=0, keepdims=True)
        dx2 = _rms_bwd(dy * gf_ref[...], xn2, r2)
        dpp_ref[...] = (dx2 * gate).astype(MXU_DTYPE)
        dz = (dx2 * pp * gate * (1.0 - gate)).astype(MXU_DTYPE)
        dz_ref[...] = dz
        dhp = _mm_nt(dz, wpg_ref[...])
        dgp_ref[...] += jnp.sum(dhp * xn1, axis=0, keepdims=True)
        dx1_ref[...] = dx2 + _rms_bwd(dhp * gp_ref[...], xn1, r1)

    sq = (D_MODEL, D_MODEL)
    vec = _resident((1, D_MODEL))
    whole = lambda shape: pl.BlockSpec(shape, lambda i: (0,) * len(shape))
    return pl.pallas_call(
        body, name="tail_fwd", grid=(t // tm,),
        in_specs=[_proj_tile(j) for j in (1, 2, 3)]
        + [_row_tile(D_MODEL), _row_tile(D_MODEL), _row_tile(PLE_DIM), _row_tile(D_MODEL), _resident(ws.shape),
           _resident(bst.shape), vec, vec, _resident(sq), _resident(sq), _resident(sq), _resident(sq),
           _resident((PLE_DIM, D_MODEL))],
        out_specs=[pl.BlockSpec((3, tm, D_MODEL), lambda i: (0, i, 0)), pl.BlockSpec((6, tm, D_MODEL), lambda i: (0, i, 0)),
                   whole(LOSS_TILE), whole((1, D_MODEL)), whole((1, D_MODEL))],
        out_shape=[jax.ShapeDtypeStruct((3, t, D_MODEL), F32), jax.ShapeDtypeStruct((6, t, D_MODEL), MXU_DTYPE),
                   jax.ShapeDtypeStruct(LOSS_TILE, F32), jax.ShapeDtypeStruct((1, D_MODEL), F32),
           jax.ShapeDtypeStruct((1, D_MODEL), F32)],
        scratch_shapes=[pltpu.VMEM((TILE_M, D_MODEL), F32)], compiler_params=_params(("arbitrary",)),
    )(pj, pj, pj, ret_raw, x2d, p2d, target, ws, bst, g_ple, g_final, w_ro, w_so, w_o, w_pg, w_pp)


def _retention_bwd_call(pj0, drr, states, consts, n_seq, seq, side):
    cos_full, sin_signed, decay_in, zeta, xi = consts
    nb = seq // RET_TILE
    cpb = RET_TILE // CHUNK

    def body(pj_ref, do_ref, st_ref, cos_ref, sin_ref, d_ref, zeta_ref, xi_ref, dpj_ref, gstate):
        @pl.when(pl.program_id(1) == 0)
        def _():
            gstate[...] = jnp.zeros_like(gstate)

        for cc in reversed(range(cpb)):
            rows = slice(cc * CHUNK, (cc + 1) * CHUNK)
            cs, sn = cos_ref[rows, :], sin_ref[rows, :]
            for h in range(RET_HEADS):
                q, k, v = _qkv(pj_ref, rows, h)
                qt = (q * cs + _rot(q) * sn) * QK_SCALE
                kt = k * cs + _rot(k) * sn
                d_out = do_ref[rows, h * V_DIM:(h + 1) * V_DIM]
                prev = st_ref[cc, h]
                g = gstate[h]
                dec = d_ref[h]
                scores_d = _mm_nt(qt, kt) * dec
                dscores = _mm_nt(d_out, v) * dec
                kz = kt * zeta_ref[h]
                qx = qt * xi_ref[h]
                dv = _mm_tn(scores_d, d_out) + _mm(kz, g)
                dqt = (_mm(dscores, kt) + _mm_nt(d_out, prev) * xi_ref[h]) * QK_SCALE
                dkt = _mm_tn(dscores, qt) + _mm_nt(v, g) * zeta_ref[h]
                gstate[h] = _mm_tn(qx, d_out) + CHUNK_DECAY[h] * g
                dq = dqt * cs + _rot(dqt * sn)
                dk = dkt * cs + _rot(dkt * sn)
                dpj_ref[rows, h * QK_DIM:(h + 1) * QK_DIM] = dq.astype(dpj_ref.dtype)
                dpj_ref[rows, RET_HEADS * QK_DIM + h * QK_DIM:RET_HEADS * QK_DIM + (h + 1) * QK_DIM] = dk.astype(
                    dpj_ref.dtype)
                dpj_ref[rows, 2 * RET_HEADS * QK_DIM + h * V_DIM:2 * RET_HEADS * QK_DIM + (h + 1) * V_DIM] = dv.astype(
                    dpj_ref.dtype)

    row = lambda b, n: (b * nb + nb - 1 - n, 0)
    pos = lambda b, n: (nb - 1 - n, 0)
    return _compute_call(
        body, name="retention_bwd", grid=(n_seq, nb),
        in_specs=[pl.BlockSpec((None, RET_TILE, SHARD_W), lambda b, n: (0,) + row(b, n)),
                  pl.BlockSpec((RET_TILE, D_MODEL), row),
                  pl.BlockSpec((cpb, RET_HEADS, QK_DIM, V_DIM), lambda b, n: (b * nb + nb - 1 - n, 0, 0, 0)),
                  pl.BlockSpec((RET_TILE, QK_DIM), pos), pl.BlockSpec((RET_TILE, QK_DIM), pos),
                  _resident(decay_in.shape), _resident(zeta.shape), _resident(xi.shape)],
        out_specs=[pl.BlockSpec((RET_TILE, SHARD_W), row)],
        out_shape=[jax.ShapeDtypeStruct((n_seq * seq, SHARD_W), MXU_DTYPE)],
        scratch_shapes=[pltpu.VMEM((RET_HEADS, QK_DIM, V_DIM), F32)],
        operands=(pj0, drr, states, cos_full, sin_signed, decay_in, zeta, xi),
        semantics=("arbitrary", "arbitrary"), side=side)


def _dx_call(dpj, x2d, wide, g_mixer, wg_in, side):
    t = x2d.shape[0]

    def body(d0, d1, d2, d3, x_ref, dx1_ref, g_ref, w_ref, dx_ref, dg_ref):
        @pl.when(pl.program_id(0) == 0)
        def _():
            dg_ref[...] = jnp.zeros_like(dg_ref)

        dh = _mm_nt(d0[...], w_ref[0])
        for j, d_ref in enumerate((d1, d2, d3)):
            dh += _mm_nt(d_ref[...], w_ref[j + 1])
        xn, r = _rms(x_ref[...])
        dg_ref[...] += jnp.sum(dh * xn, axis=0, keepdims=True)
        dx_ref[...] = dx1_ref[...] + _rms_bwd(dh * g_ref[...], xn, r)

    return _compute_call(
        body, name="dx_bwd", grid=(t // DX_TILE,),
        in_specs=[_row_tile(SHARD_W, DX_TILE)] * N_CHIPS
        + [_row_tile(D_MODEL, DX_TILE), pl.BlockSpec((None, DX_TILE, D_MODEL), lambda i: (0, i, 0))]
        + [_resident((1, D_MODEL)), _resident(wg_in.shape)],
        out_specs=[_row_tile(D_MODEL, DX_TILE), pl.BlockSpec((1, D_MODEL), lambda i: (0, 0))],
        out_shape=[jax.ShapeDtypeStruct((t, D_MODEL), F32), jax.ShapeDtypeStruct((1, D_MODEL), F32)],
        operands=(*dpj, x2d, wide, g_mixer, wg_in), semantics=("arbitrary",), side=side)


def _wgrad_call(name, lhs, rhs, block_n, out_cols=None, block_t=1024, into=None, slot=0, n_slots=1, side=None):
    (lhs, lhs_at), (rhs, rhs_at) = [x if isinstance(x, tuple) else (x, None) for x in (lhs, rhs)]
    t, n = rhs.shape[-2:]
    k = lhs.shape[-1]
    block_t = min(block_t, t)

    def operand_spec(at, block, index):
        if at is None:
            return pl.BlockSpec(block, index)
        return pl.BlockSpec((None,) + block, lambda j, i: (at,) + index(j, i))

    steps = t // block_t
    out_cols = block_n if out_cols is None else out_cols
    per = block_n // out_cols
    first_block = slot * (n // block_n)

    def body(l_ref, r_ref, *rest):
        o_ref, acc = rest[-2:]
        @pl.when(pl.program_id(1) == 0)
        def _():
            acc[...] = jnp.zeros_like(acc)

        acc[...] += _mm_tn(l_ref[...], r_ref[...])

        @pl.when(pl.program_id(1) == steps - 1)
        def _():
            for s in range(per):
                o_ref[s] = acc[:, s * out_cols:(s + 1) * out_cols].astype(o_ref.dtype)

    res = _compute_call(
        body, name=name, grid=(n // block_n, steps),
        in_specs=[operand_spec(lhs_at, (block_t, k), lambda j, i: (i, 0)),
                  operand_spec(rhs_at, (block_t, block_n), lambda j, i: (i, j))] + ([] if into is None else [ANY]),
        out_specs=[pl.BlockSpec((per, k, out_cols), lambda j, i: (first_block + j, 0, 0))],
        out_shape=[jax.ShapeDtypeStruct((n_slots * (n // out_cols), k, out_cols), COMM_DTYPE)],
        scratch_shapes=[pltpu.VMEM((k, block_n), F32)], aliases={} if into is None else {2: 0},
        operands=(lhs, rhs) if into is None else (lhs, rhs, into), semantics=("arbitrary", "arbitrary"), side=side)
    return res[0] if side is None else (res[0][0], res[1])


def _position():
    return lax.axis_index("x"), lax.axis_index("y"), lax.axis_index("c")


def _position_array():
    x, y, c = _position()
    return jnp.stack([2 * x + y, c]).astype(jnp.int32)


def _other_chip(x, y, k):
    return (1 - x if k & 2 else x), (1 - y if k & 1 else y)


def _plan_side(operands, out_shapes, plan, n_remote, aliases=None):
    def copies(ins, outs, send_sem, recv_sem, base=0):
        remote = plan(ins, outs)
        assert len(remote) == n_remote
        return [pltpu.make_async_remote_copy(src_ref=src, dst_ref=dst, send_sem=send_sem.at[base + i],
                                             recv_sem=recv_sem.at[base + i], device_id=dev, device_id_type=MESH)
                for i, (src, dst, dev) in enumerate(remote)]

    def start(*a):
        for cp in copies(*a):
            cp.start()

    def finish(*a):
        for cp in copies(*a):
            cp.wait()

    return _Side(operands, out_shapes, n_remote, start, finish, aliases)


def _place_cast_call(name, shards, pos):
    n = len(shards)
    rows, width = shards[0].shape
    block_rows = min(256, rows)

    def body(pos_ref, *refs):
        for w_ref, o_ref in zip(refs[:n], refs[n:]):
            o_ref[...] = w_ref[...].astype(o_ref.dtype)

    return pl.pallas_call(
        body, name=name,
        grid_spec=pltpu.PrefetchScalarGridSpec(
            num_scalar_prefetch=1, grid=(rows // block_rows,),
            in_specs=[pl.BlockSpec((block_rows, width), lambda i, pos: (i, 0))] * n,
            out_specs=[pl.BlockSpec((None, block_rows, width), lambda i, pos: (pos[0], i, 0))] * n),
        out_shape=[jax.ShapeDtypeStruct((N_CHIPS, rows, width), MXU_DTYPE)] * n,
        compiler_params=_params(("arbitrary",)),
    )(pos, *shards)


def _gather_side(placed):
    n = len(placed)

    def copies(kind, bufs, send_sem, recv_sem, base):
        x, y, c = _position()
        me = 2 * x + y
        made = []
        for i in range(n):
            hr = placed[i].shape[1] // 2
            for k in (1, 2, 3):
                px, py = _other_chip(x, y, k)
                chip, core, slot, dev = [(me, c, k - 1, (px, py, c)), (2 * px + py, c, 2 + k, (x, y, 1 - c)),
                                         (2 * px + py, 1 - c, 2 + k, (x, y, 1 - c))][kind]
                piece = bufs[i].at[chip, pl.ds(core * hr, hr)]
                made.append(pltpu.make_async_remote_copy(
                    src_ref=piece, dst_ref=piece, send_sem=send_sem.at[base + 6 * i + slot],
                    recv_sem=recv_sem.at[base + 6 * i + slot], device_id=dev, device_id_type=MESH))
        return made

    def start(ins, outs, send_sem, recv_sem, base=0):
        for cp in copies(0, outs, send_sem, recv_sem, base):
            cp.start()

    def finish(ins, outs, send_sem, recv_sem, base=0):
        first, onward = copies(0, outs, send_sem, recv_sem, base), copies(1, outs, send_sem, recv_sem, base)
        for landed, cp in zip(first, onward):
            landed.wait_recv()
            cp.start()
        for cp in copies(2, outs, send_sem, recv_sem, base):
            cp.wait_recv()
        for cp in first + onward:
            cp.wait_send()

    return _Side(placed, [jax.ShapeDtypeStruct(a.shape, a.dtype) for a in placed], 6 * n, start, finish,
                 {i: i for i in range(n)})


def _same_shape(shapes):
    found = {}
    for i, shape in enumerate(shapes):
        found.setdefault(shape, []).append(i)
    return found


def _sum_call(name, pos, groups, rows, width, out_dtype, block_rows, out_rows=None, out_index=None):
    out_rows = rows if out_rows is None else out_rows
    out_index = (lambda i, pos, br: i) if out_index is None else out_index
    sizes = [len(g) for g in groups]

    def body(pos_ref, *refs):
        ins, outs = refs[:sum(sizes)], refs[sum(sizes):]
        at = 0
        for size, out in zip(sizes, outs):
            acc = ins[at][...].astype(F32)
            for r in ins[at + 1:at + size]:
                acc = acc + r[...].astype(F32)
            out[...] = acc.astype(out_dtype)
            at += size

    def spec(index):
        return pl.BlockSpec((block_rows, width), lambda i, pos: (index(i, pos, block_rows), 0))

    return pl.pallas_call(
        body, name=name,
        grid_spec=pltpu.PrefetchScalarGridSpec(
            num_scalar_prefetch=1, grid=(rows // block_rows,),
            in_specs=[spec(index) for g in groups for _, index in g], out_specs=[spec(out_index)] * len(groups)),
        out_shape=[jax.ShapeDtypeStruct((out_rows, width), out_dtype)] * len(groups),
        compiler_params=_params(("arbitrary",)),
    )(pos, *[arr for g in groups for arr, _ in g])


def _pair_stage(tag, grads, pos):
    del pos
    n = len(grads)
    shapes = [(g.shape[0] // N_CHIPS, g.shape[1]) for g in grads]
    kinds = list(_same_shape(shapes))
    chunk = {(r, w): min(r // 2, PAIR_CHUNK_ROWS) for r, w in kinds}
    work = [(i, j, q) for i, (r, w) in enumerate(shapes) for j in range(N_CHIPS)
            for q in range((r // 2) // chunk[(r, w)])]

    def body(*refs):
        ins, sums, landed = refs[:n], refs[n:2 * n], refs[2 * n:3 * n]
        bufs, (load_sem, store_sem, send_sem, recv_sem) = refs[3 * n:-4], refs[-4:]
        x, y, c = _position()

        def piece(i, j):
            r = shapes[i][0]
            return pltpu.make_async_remote_copy(
                src_ref=ins[i].at[pl.ds(j * r + (1 - c) * (r // 2), r // 2)],
                dst_ref=landed[i].at[pl.ds(j * (r // 2), r // 2)], send_sem=send_sem.at[N_CHIPS * i + j],
                recv_sem=recv_sem.at[N_CHIPS * i + j], device_id=(x, y, 1 - c), device_id_type=MESH)

        def buffers(i, slot):
            own, got, out = bufs[3 * kinds.index(shapes[i]):3 * kinds.index(shapes[i]) + 3]
            return own.at[slot], got.at[slot], out.at[slot]

        def loads(step):
            i, j, q = work[step]
            r, ch = shapes[i][0], chunk[shapes[i]]
            own, got, _ = buffers(i, step % 2)
            return (pltpu.make_async_copy(ins[i].at[pl.ds(j * r + c * (r // 2) + q * ch, ch)], own,
                                          load_sem.at[step % 2, 0]),
                    pltpu.make_async_copy(landed[i].at[pl.ds(j * (r // 2) + q * ch, ch)], got,
                                          load_sem.at[step % 2, 1]))

        def store(step):
            i, j, q = work[step]
            ch = chunk[shapes[i]]
            return pltpu.make_async_copy(buffers(i, step % 2)[2],
                                         sums[i].at[pl.ds(j * (shapes[i][0] // 2) + q * ch, ch)],
                                         store_sem.at[step % 2])

        def begin(step):
            i, j, q = work[step]
            if q == 0:
                piece(i, j).wait_recv()
            for cp in loads(step):
                cp.start()

        for i in range(n):
            for j in range(N_CHIPS):
                piece(i, j).start()
        begin(0)
        for step in range(len(work)):
            if step + 1 < len(work):
                begin(step + 1)
            for cp in loads(step):
                cp.wait()
            if step >= 2:
                store(step - 2).wait()
            own, got, out = buffers(work[step][0], step % 2)
            out[...] = (own[...].astype(F32) + got[...].astype(F32)).astype(out.dtype)
            store(step).start()
        for step in range(max(len(work) - 2, 0), len(work)):
            store(step).wait()
        for i in range(n):
            for j in range(N_CHIPS):
                piece(i, j).wait_send()

    half = [jax.ShapeDtypeStruct((N_CHIPS * (r // 2), w), COMM_DTYPE) for r, w in shapes]
    res = pl.pallas_call(
        body, name=f"grad_pair_sum_{tag}", in_specs=[ANY] * n, out_specs=[ANY] * (2 * n), out_shape=half + half,
        scratch_shapes=[pltpu.VMEM((2, chunk[k], k[1]), COMM_DTYPE) for k in kinds for _ in range(3)]
        + [pltpu.SemaphoreType.DMA((2, 2)), pltpu.SemaphoreType.DMA((2,)), pltpu.SemaphoreType.DMA((N_CHIPS * n,)),
           pltpu.SemaphoreType.DMA((N_CHIPS * n,))],
        compiler_params=pltpu.CompilerParams(has_side_effects=True, vmem_limit_bytes=VMEM_LIMIT),
    )(*grads)
    return list(res[:n])


def _chip_side(pair_sums):
    halves = [(p.shape[0] // N_CHIPS, p.shape[1]) for p in pair_sums]

    def plan(ins, outs):
        x, y, c = _position()
        remote = []
        for i, (hr, _) in enumerate(halves):
            for k in (1, 2, 3):
                px, py = _other_chip(x, y, k)
                remote.append((ins[i].at[pl.ds((2 * px + py) * hr, hr)], outs[i].at[pl.ds((k - 1) * hr, hr)],
                               (px, py, c)))
        return remote

    return _plan_side(pair_sums, [jax.ShapeDtypeStruct((3 * hr, w), COMM_DTYPE) for hr, w in halves], plan,
                      3 * len(pair_sums))


def _finish_stage(tag, pair_sums, chip, pos, extra=None):
    n = len(pair_sums)
    halves = [(p.shape[0] // N_CHIPS, p.shape[1]) for p in pair_sums]

    def slab(k, hr):
        return lambda s, pos, br: k * (hr // br) + s

    reduced = [None] * n
    for (hr, w), which in _same_shape(halves).items():
        groups = [[(pair_sums[i], lambda s, pos, br: pos[0] * (hr // br) + s)] + [(chip[i], slab(k, hr)) for k in range(3)]
                  for i in which]
        outs = _sum_call(f"grad_chip_sum_{tag}{which[0]}", pos, groups, hr, w, F32, min(256, hr), out_rows=2 * hr,
                         out_index=lambda s, pos, br: pos[1] * (hr // br) + s)
        for i, out in zip(which, outs):
            reduced[i] = out

    def plan(ins, outs):
        x, y, c = _position()
        remote = []
        for i, (hr, _) in enumerate(halves):
            mine = outs[i].at[pl.ds(c * hr, hr)]
            remote.append((mine, mine, (x, y, 1 - c)))
        return remote

    swap = _plan_side(reduced, [jax.ShapeDtypeStruct((2 * hr, w), F32) for hr, w in halves], plan, n,
                      {i: i for i in range(n)})
    if extra is None:
        return _exchange_call(f"grad_half_exchange_{tag}", swap)
    res = _exchange_call(f"grad_half_exchange_{tag}", _join_sides(swap, extra))
    return res[:n], res[n:]


def _small_gather_side(parts):
    n = len(parts)

    def copies(ins, outs, send_sem, recv_sem, base):
        x, y, c = _position()
        made = []
        for i in range(n):
            mine = outs[i].at[4 * x + 2 * y + c]
            made.append(pltpu.make_async_copy(ins[i], mine, send_sem.at[base + 8 * i + 7]))
            for d in range(1, 8):
                px, py = _other_chip(x, y, d >> 1)
                made.append(pltpu.make_async_remote_copy(
                    src_ref=ins[i], dst_ref=mine, send_sem=send_sem.at[base + 8 * i + d - 1],
                    recv_sem=recv_sem.at[base + 8 * i + d - 1], device_id=(px, py, 1 - c if d & 1 else c),
                    device_id_type=MESH))
        return made

    def start(ins, outs, send_sem, recv_sem, base=0):
        for cp in copies(ins, outs, send_sem, recv_sem, base):
            cp.start()

    def finish(ins, outs, send_sem, recv_sem, base=0):
        for cp in copies(ins, outs, send_sem, recv_sem, base):
            cp.wait()

    return _Side(parts, [jax.ShapeDtypeStruct((8,) + a_.shape, F32) for a_ in parts], 8 * n, start, finish)


def _adamw(w, g, m, v):
    m = ADAM_B1 * m + (1.0 - ADAM_B1) * g
    v = ADAM_B2 * v + (1.0 - ADAM_B2) * (g * g)
    m_hat = m / (1.0 - ADAM_B1 ** ADAM_STEP)
    v_hat = v / (1.0 - ADAM_B2 ** ADAM_STEP)
    delta = -ADAM_LR * (m_hat / (jnp.sqrt(v_hat) + ADAM_EPS) + ADAM_WD * w)
    return delta, m, v


def _adamw_call(name, ws, gs, ms, vs):
    n = len(ws)
    rows, width = ws[0].shape
    block_rows = min(256 // n, rows)

    def body(*refs):
        for i in range(n):
            w_ref, g_ref, m_ref, v_ref = (refs[j * n + i] for j in range(4))
            d_out, m_out, v_out = (refs[(4 + j) * n + i] for j in range(3))
            d_out[...], m_out[...], v_out[...] = _adamw(w_ref[...], g_ref[...], m_ref[...], v_ref[...])

    spec = pl.BlockSpec((block_rows, width), lambda i: (i, 0))
    res = pl.pallas_call(
        body, name=name, grid=(rows // block_rows,), in_specs=[spec] * (4 * n), out_specs=[spec] * (3 * n),
        out_shape=[jax.ShapeDtypeStruct(ws[0].shape, F32)] * (3 * n),
        compiler_params=_params(("arbitrary",)),
    )(*ws, *gs, *ms, *vs)
    return [(res[i], res[n + i], res[2 * n + i]) for i in range(n)]


def _small_adamw_call(gathered, weights, moments_m, moments_v):
    n = len(weights)

    def body(*refs):
        all_refs, refs = refs[:n + 1], refs[n + 1:]
        w_refs, m_refs, v_refs, outs = refs[:n], refs[n:2 * n], refs[2 * n:3 * n], refs[3 * n:]

        def total(ref):
            acc = ref[0]
            for d in range(1, 8):
                acc = acc + ref[d]
            return acc

        outs[0][...] = total(all_refs[n])
        for i in range(n):
            g = total(all_refs[i])
            if i == 0:
                row = lax.broadcasted_iota(jnp.int32, g.shape, 0)
                col = lax.broadcasted_iota(jnp.int32, g.shape, 1)
                g = jnp.where((row % CHUNK) >= col, g, 0.0)
            g_out, d_out, m_out, v_out = outs[1 + 4 * i:5 + 4 * i]
            g_out[...] = g
            d_out[...], m_out[...], v_out[...] = _adamw(w_refs[i][...], g, m_refs[i][...], v_refs[i][...])

    out_shape = [jax.ShapeDtypeStruct(LOSS_TILE, F32)]
    for w in weights:
        out_shape += [jax.ShapeDtypeStruct(w.shape, F32)] * 4
    res = pl.pallas_call(
        body, name="small_adamw", out_shape=out_shape,
        compiler_params=pltpu.CompilerParams(vmem_limit_bytes=VMEM_LIMIT),
    )(*gathered, *weights, *moments_m, *moments_v)
    return res[0], [res[1 + 4 * i:5 + 4 * i] for i in range(n)]


def kernel(x, p, w_in, w_ret_out, w_sgu_out, w_out, sgu_ws, sgu_bs, w_ple_gate, w_ple_proj, g_mixer, g_ple, g_final, loss_target, m_w_in, m_w_ret_out, m_w_sgu_out, m_w_out, m_sgu_ws, m_sgu_bs, m_w_ple_gate, m_w_ple_proj, m_g_mixer, m_g_ple, m_g_final, v_w_in, v_w_ret_out, v_w_sgu_out, v_w_out, v_sgu_ws, v_sgu_bs, v_w_ple_gate, v_w_ple_proj, v_g_mixer, v_g_ple, v_g_final):
    n_seq, seq, _ = x.shape
    t = n_seq * seq
    x2d = x.reshape(t, D_MODEL)
    p2d = p.reshape(t, PLE_DIM)
    target = loss_target.reshape(t, D_MODEL)
    big = [w_in[0], w_ret_out[0], w_sgu_out[0], w_out[0], w_ple_gate[0], w_ple_proj[0]]
    big_m = [m_w_in[0], m_w_ret_out[0], m_w_sgu_out[0], m_w_out[0], m_w_ple_gate[0], m_w_ple_proj[0]]
    big_v = [v_w_in[0], v_w_ret_out[0], v_w_sgu_out[0], v_w_out[0], v_w_ple_gate[0], v_w_ple_proj[0]]

    pos = _position_array()
    placed = (_place_cast_call("place_w_in", big[:1], pos) + _place_cast_call("place_square_weights", big[1:5], pos)
              + _place_cast_call("place_w_ple_proj", big[5:], pos))
    ws = sgu_ws[0]
    bst = sgu_bs[0].T
    consts = _retention_consts(seq)

    (h, pj, wg_in), gathered = _proj_call(x2d, g_mixer, placed[0], pos, _gather_side(placed[1:]))
    pj0 = pj1 = pj2 = pj3 = pj
    w_ro, w_so, w_o, w_pg = (w.reshape(D_MODEL, D_MODEL) for w in gathered[:4])
    w_pp = gathered[4].transpose(1, 0, 2).reshape(PLE_DIM, D_MODEL)
    ret_raw, states = _retention_fwd_call(pj0, consts, n_seq, seq)
    wide, half, loss, dg_ple, dg_final = _tail_call(
        pj, ret_raw, x2d, p2d, target, ws, bst, g_ple, g_final.reshape(1, D_MODEL), w_ro, w_so, w_o, w_pg, w_pp)
    ret, sgu, merged, hp, dz, dpp = ((half, k) for k in range(6))
    dpj1, dpj2, dpj3, drr, da, db, dws, dbst = _merge_bwd_call(wide, pj1, pj2, pj3, ret_raw, ws, bst, w_ro, w_so, w_o)

    small_shapes = [(SGU_GROUPS * CHUNK, CHUNK), (SGU_GROUPS, CHUNK), (1, D_MODEL), (1, D_MODEL), (1, D_MODEL)]
    as_small = lambda arrays: [None if a_ is None else a_.reshape(s_) for a_, s_ in zip(arrays, small_shapes)]
    early = as_small([dws, dbst.T, None, dg_ple, dg_final])
    rows_of = lambda g: g.reshape(g.shape[0] * g.shape[1], g.shape[2])
    tail_grads = [
        rows_of(_wgrad_call("wgrad_ret_out", ret, da, D_MODEL)),
        rows_of(_wgrad_call("wgrad_sgu_out", sgu, db, D_MODEL)),
        rows_of(_wgrad_call("wgrad_out", merged, (wide, 0), D_MODEL)),
        rows_of(_wgrad_call("wgrad_ple_gate", hp, dz, D_MODEL)),
        rows_of(_wgrad_call("wgrad_ple_proj", p2d, dpp, D_MODEL, out_cols=PLE_DIM)),
    ]
    tail_sums = _pair_stage("tail", tail_grads, pos)
    (dpj0,), carried = _retention_bwd_call(
        pj0, drr, states, consts, n_seq, seq,
        _join_sides(_chip_side(tail_sums), _small_gather_side([early[0], early[1], early[3], early[4], loss])))
    tail_chip, early_all = carried[:len(tail_sums)], carried[len(tail_sums):]
    in_grad = None
    for j, d in enumerate((dpj0, dpj1, dpj2, dpj3)):
        in_grad = _wgrad_call(f"wgrad_in_{j}", h, d, SHARD_W, into=in_grad, slot=j, n_slots=N_CHIPS)
    in_sums = _pair_stage("in", [rows_of(in_grad)], pos)
    (dx, dg_mixer), in_chip = _dx_call((dpj0, dpj1, dpj2, dpj3), x2d, wide, g_mixer, wg_in, _chip_side(in_sums))
    g_big, (mixer_all,) = _finish_stage("all", in_sums + tail_sums, list(in_chip) + list(tail_chip), pos,
                                        _small_gather_side([dg_mixer]))
    upd = [None] * len(big)
    for _, which in _same_shape([w.shape for w in big]).items():
        pick = lambda arrays: [arrays[i] for i in which]
        for i, triple in zip(which, _adamw_call(f"adamw_{which[0]}", pick(big), pick(g_big), pick(big_m), pick(big_v))):
            upd[i] = triple

    small_g = [early_all[0], early_all[1], mixer_all, early_all[2], early_all[3], early_all[4]]
    total, small = _small_adamw_call(small_g, as_small([sgu_ws, sgu_bs, g_mixer, g_ple, g_final]),
                                     as_small([m_sgu_ws, m_sgu_bs, m_g_mixer, m_g_ple, m_g_final]),
                                     as_small([v_sgu_ws, v_sgu_bs, v_g_mixer, v_g_ple, v_g_final]))
    out_small_shapes = [sgu_ws.shape, sgu_bs.shape, g_mixer.shape, g_ple.shape, g_final.shape]

    def ordered(big_list, kind):
        w_in_, w_ro_, w_so_, w_o_, w_pg_, w_pp_ = [b_[None] for b_ in big_list]
        s_ws, s_bs, s_gm, s_gp, s_gf = [small[i][kind].reshape(s) for i, s in enumerate(out_small_shapes)]
        return [w_in_, w_ro_, w_so_, w_o_, s_ws, s_bs, w_pg_, w_pp_, s_gm, s_gp, s_gf]

    out = [total[0, 0], dx.reshape(x.shape)]
    out += ordered(g_big, 0)
    out += ordered([u[0] for u in upd], 1)
    out += ordered([u[1] for u in upd], 2)
    out += ordered([u[2] for u in upd], 3)
    return tuple(out)
```

```python
import functools
import math

import numpy as np
import jax
import jax.numpy as jnp
from jax import lax
from jax.experimental import pallas as pl
from jax.experimental.pallas import tpu as pltpu

F32 = jnp.float32
MXU_DTYPE = jnp.bfloat16
COMM_DTYPE = jnp.bfloat16

D_MODEL = 1024
RET_HEADS = 4
QK_DIM = 128
V_DIM = 256
CHUNK = 128
SGU_GROUPS = 4
GROUP_DIM = 256
PLE_DIM = 256
N_CHIPS = 4
SHARD_W = 2048
ROPE_BASE = 10000.0
NORM_EPS = 1e-6
GN_EPS = 1e-5
QK_SCALE = QK_DIM ** -0.5
SQRT_HALF = math.sqrt(0.5)
INV_SQRT_2PI = 1.0 / math.sqrt(2.0 * math.pi)

ADAM_LR = 0.001
ADAM_B1 = 0.9
ADAM_B2 = 0.999
ADAM_EPS = 1e-08
ADAM_WD = 0.01
ADAM_STEP = 10

TILE_M = 256
LOSS_TILE = (8, 128)
LOAD_PARTS = 4
PAIR_CHUNK_ROWS = 256
PROJ_TILE = 512
RET_TILE = 512
DX_TILE = 512
VMEM_LIMIT = 56 * 1024 * 1024
MESH = pl.DeviceIdType.MESH
ANY = pl.BlockSpec(memory_space=pl.ANY)

CHUNK_DECAY = tuple(
    float(np.exp(np.float32(CHUNK) * np.log(np.float32(1.0 - 2.0 ** (-5.0 - h))))) for h in range(RET_HEADS))


def _mm(a, b):
    return jnp.dot(a.astype(MXU_DTYPE), b.astype(MXU_DTYPE), preferred_element_type=F32)


def _mm_nt(a, b):
    return lax.dot_general(a.astype(MXU_DTYPE), b.astype(MXU_DTYPE), (((1,), (1,)), ((), ())),
                           preferred_element_type=F32)


def _mm_tn(a, b):
    return lax.dot_general(a.astype(MXU_DTYPE), b.astype(MXU_DTYPE), (((0,), (0,)), ((), ())),
                           preferred_element_type=F32)


def _mean(x):
    return jnp.mean(x, axis=-1, keepdims=True)


def _sigmoid(x):
    return jax.nn.sigmoid(x)


def _silu_and_grad(x):
    s = _sigmoid(x)
    silu = x * s
    return silu, s + silu * (1.0 - s)


def _gelu_and_grad(x):
    cdf = 0.5 + 0.5 * lax.erf(x * SQRT_HALF)
    return x * cdf, cdf + x * (jnp.exp(x * x * -0.5) * INV_SQRT_2PI)


def _unit_norm(x, eps):
    xc = x - _mean(x)
    rstd = lax.rsqrt(_mean(xc * xc) + eps)
    return xc * rstd, rstd


def _unit_norm_bwd(dn, n, rstd):
    return rstd * (dn - _mean(dn) - n * _mean(dn * n))


def _rms(x):
    r = lax.rsqrt(_mean(x * x) + NORM_EPS)
    return x * r, r


def _rms_bwd(dxn, xn, r):
    return r * (dxn - xn * _mean(dxn * xn))


def _rot(x):
    return pltpu.roll(x, QK_DIM // 2, 1)


def _params(semantics, **kw):
    return pltpu.CompilerParams(dimension_semantics=semantics, vmem_limit_bytes=VMEM_LIMIT, **kw)


def _row_tile(width, tm=TILE_M):
    return pl.BlockSpec((tm, width), lambda i: (i, 0))


def _proj_tile(j):
    return pl.BlockSpec((None, TILE_M, SHARD_W), lambda i: (j, i, 0))


def _resident(shape):
    nd = len(shape)
    return pl.BlockSpec(shape, lambda *_: (0,) * nd, pipeline_mode=pl.Buffered(1))


def _causal_ws(ws_ref):
    row = lax.broadcasted_iota(jnp.int32, (CHUNK, CHUNK), 0)
    col = lax.broadcasted_iota(jnp.int32, (CHUNK, CHUNK), 1)
    return [jnp.where(row >= col, ws_ref[g], 0.0).astype(MXU_DTYPE) for g in range(SGU_GROUPS)]


def _heads(x, width):
    return [x[:, h * width:(h + 1) * width] for h in range(x.shape[1] // width)]


def _branch_forward(pj1, pj2, ret_raw, wsc, bst, mixed_ref):
    rg, su = pj1[:, :D_MODEL], pj1[:, D_MODEL:]
    sv, sg = pj2[:, :D_MODEL], pj2[:, D_MODEL:]
    rn_parts, rstd_parts = zip(*[_unit_norm(r, GN_EPS) for r in _heads(ret_raw, V_DIM)])
    rn = jnp.concatenate(rn_parts, axis=1)
    sil_rg, dsil_rg = _silu_and_grad(rg)
    ret = rn * sil_rg
    u, du = _gelu_and_grad(su)
    gelu_sv, dgelu_sv = _gelu_and_grad(sv)
    vn, rstd_v = _unit_norm(gelu_sv, GN_EPS)
    for cc in range(pj1.shape[0] // CHUNK):
        for g in range(SGU_GROUPS):
            rs, cs = slice(cc * CHUNK, (cc + 1) * CHUNK), slice(g * GROUP_DIM, (g + 1) * GROUP_DIM)
            mixed_ref[rs, cs] = _mm(wsc[g], vn[rs, cs]) + bst[:, g:g + 1]
    mixed = mixed_ref[...]
    sil_sg, dsil_sg = _silu_and_grad(sg)
    sgu = u * mixed * sil_sg
    return dict(rn=rn, rstd_r=rstd_parts, sil_rg=sil_rg, dsil_rg=dsil_rg, ret=ret, u=u, du=du, dgelu_sv=dgelu_sv,
                vn=vn, rstd_v=rstd_v, mixed=mixed, sil_sg=sil_sg, dsil_sg=dsil_sg, sgu=sgu)


class _Side:
    def __init__(self, operands, out_shapes, n_sems, start, finish, aliases=None):
        self.operands, self.out_shapes, self.n_sems = list(operands), list(out_shapes), n_sems
        self.start, self.finish, self.aliases = start, finish, dict(aliases or {})


def _join_sides(*sides):
    spans, a, b, s = [], 0, 0, 0
    for side in sides:
        spans.append((a, b, s))
        a, b, s = a + len(side.operands), b + len(side.out_shapes), s + side.n_sems

    def run(which):
        def go(ins, outs, send_sem, recv_sem, base=0):
            for side, (a0, b0, s0) in zip(sides, spans):
                getattr(side, which)(ins[a0:a0 + len(side.operands)], outs[b0:b0 + len(side.out_shapes)],
                                     send_sem, recv_sem, base + s0)
        return go

    aliases = {a0 + i: b0 + o for side, (a0, b0, _) in zip(sides, spans) for i, o in side.aliases.items()}
    return _Side([x for side in sides for x in side.operands], [x for side in sides for x in side.out_shapes], s,
                 run("start"), run("finish"), aliases)


def _compute_call(body, *, name, grid, in_specs, out_specs, out_shape, operands, semantics, scratch_shapes=(),
                  side=None, prefetch=None, aliases=None, side_start=None):
    n_pre = 0 if prefetch is None else 1
    pre = () if prefetch is None else (prefetch,)

    def spec(in_specs, out_specs, scratch):
        return pltpu.PrefetchScalarGridSpec(num_scalar_prefetch=n_pre, grid=grid, in_specs=in_specs,
                                            out_specs=out_specs, scratch_shapes=scratch)

    if side is None:
        return pl.pallas_call(body, name=name, grid_spec=spec(in_specs, out_specs, list(scratch_shapes)),
                              out_shape=out_shape,
                              input_output_aliases={n_pre + a: b for a, b in (aliases or {}).items()},
                              compiler_params=_params(semantics))(*pre, *operands)
    n_in, n_out, s_in, s_out = len(operands), len(out_shape), len(side.operands), len(side.out_shapes)

    def carrier(*refs):
        pre_refs, refs = refs[:n_pre], refs[n_pre:]
        ins, refs = refs[:n_in], refs[n_in:]
        side_ins, refs = refs[:s_in], refs[s_in:]
        outs, refs = refs[:n_out], refs[n_out:]
        side_outs, refs = refs[:s_out], refs[s_out:]
        scratch, (send_sem, recv_sem) = refs[:-2], refs[-2:]
        ids = [pl.program_id(a) for a in range(len(grid))]
        at = (0,) * len(grid) if side_start is None else side_start
        first = functools.reduce(jnp.logical_and, [i == a for i, a in zip(ids, at)])
        last = functools.reduce(jnp.logical_and, [i == g - 1 for i, g in zip(ids, grid)])

        @pl.when(first)
        def _():
            side.start(side_ins, side_outs, send_sem, recv_sem)

        body(*pre_refs, *ins, *outs, *scratch)

        @pl.when(last)
        def _():
            side.finish(side_ins, side_outs, send_sem, recv_sem)

    all_aliases = {n_pre + a: b for a, b in (aliases or {}).items()}
    all_aliases.update({n_pre + n_in + a: n_out + b for a, b in side.aliases.items()})
    res = pl.pallas_call(
        carrier, name=name,
        grid_spec=spec(list(in_specs) + [ANY] * s_in, list(out_specs) + [ANY] * s_out,
                       list(scratch_shapes) + [pltpu.SemaphoreType.DMA((side.n_sems,))] * 2),
        out_shape=list(out_shape) + side.out_shapes, input_output_aliases=all_aliases,
        compiler_params=_params(semantics, has_side_effects=True),
    )(*pre, *operands, *side.operands)
    return res[:n_out], res[n_out:]


def _proj_call(x2d, g_mixer, placed_in, pos, side):
    t = x2d.shape[0]
    nt = t // PROJ_TILE
    hr = placed_in.shape[1] // 2

    def body(pos_ref, x_ref, g_ref, win_ref, h_ref, pj_ref, w_ref, w_vmem, h_all, loc_sem, send_sem, recv_sem):
        k, i = pl.program_id(0), pl.program_id(1)
        x, y, c = _position()
        me = 2 * x + y

        def copy(slot, chip, core, dev):
            piece = w_ref.at[chip, pl.ds(core * hr, hr)]
            return pltpu.make_async_remote_copy(src_ref=piece, dst_ref=piece, send_sem=send_sem.at[slot],
                                                recv_sem=recv_sem.at[slot], device_id=dev, device_id_type=MESH)

        def first_hop(kk):
            px, py = _other_chip(x, y, kk)
            return copy(kk - 1, me, c, (px, py, c))

        def relay():
            source = jnp.bitwise_xor(me, 2 - c)
            return copy(2, source, c, (jnp.bitwise_xor(x, c), jnp.bitwise_xor(y, 1 - c), c))

        def onward(kk, core):
            px, py = _other_chip(x, y, kk)
            return copy(2 + kk, 2 * px + py, core, (x, y, 1 - c))

        def loads(kk):
            px, py = _other_chip(x, y, kk)
            rows = w_vmem.shape[1] // LOAD_PARTS
            return [pltpu.make_async_copy(w_ref.at[2 * px + py, pl.ds(r * rows, rows)],
                                          w_vmem.at[kk % 2, pl.ds(r * rows, rows)], loc_sem.at[r])
                    for r in range(LOAD_PARTS)]

        @pl.when(jnp.logical_and(k == 0, i == 0))
        def _():
            for kk in (1, 2):
                first_hop(kk).start()
            for cp in loads(0):
                cp.start()
            for cp in loads(0):
                cp.wait()

        for kk in (1, 2, 3):
            @pl.when(jnp.logical_and(k == kk - 1, i == nt - 1))
            def _(kk=kk):
                if kk == 1:
                    first_hop(1).wait_recv()
                    first_hop(2).wait_recv()
                    relay().start()
                if kk == 3:
                    relay().wait_recv()
                onward(kk, c).start()
                onward(kk, 1 - c).wait_recv()
                for cp in loads(kk):
                    cp.start()

            @pl.when(jnp.logical_and(k == kk, i == 0))
            def _(kk=kk):
                for cp in loads(kk):
                    cp.wait()

        rows = pl.ds(pl.multiple_of(i * PROJ_TILE, PROJ_TILE), PROJ_TILE)

        @pl.when(k == 0)
        def _():
            xn, _ = _rms(x_ref[...])
            h = (xn * g_ref[...]).astype(MXU_DTYPE)
            h_all[rows, :] = h
            h_ref[...] = h

        pj_ref[...] = jnp.dot(h_all[rows, :], w_vmem[k % 2], preferred_element_type=F32)

        @pl.when(jnp.logical_and(k == N_CHIPS - 1, i == nt - 1))
        def _():
            for cp in [first_hop(1), first_hop(2), relay()] + [onward(kk, c) for kk in (1, 2, 3)]:
                cp.wait_send()

    parked = lambda k, i, pos: (jnp.where(k == 0, i, nt - 1), 0)
    return _compute_call(
        body, name="proj_fwd", grid=(N_CHIPS, nt),
        in_specs=[pl.BlockSpec((PROJ_TILE, D_MODEL), parked), pl.BlockSpec((1, D_MODEL), lambda k, i, pos: (0, 0)), ANY],
        out_specs=[pl.BlockSpec((PROJ_TILE, D_MODEL), lambda k, i, pos: (jnp.where(k == 0, i, nt), 0)),
                   pl.BlockSpec((None, PROJ_TILE, SHARD_W), lambda k, i, pos: (jnp.bitwise_xor(pos[0], k), i, 0)),
                   ANY],
        out_shape=[jax.ShapeDtypeStruct((t + PROJ_TILE, D_MODEL), MXU_DTYPE),
                   jax.ShapeDtypeStruct((N_CHIPS, t, SHARD_W), F32),
                   jax.ShapeDtypeStruct(placed_in.shape, placed_in.dtype)],
        scratch_shapes=[pltpu.VMEM((2,) + placed_in.shape[1:], placed_in.dtype), pltpu.VMEM((t, D_MODEL), MXU_DTYPE),
                        pltpu.SemaphoreType.DMA((LOAD_PARTS,)), pltpu.SemaphoreType.DMA((6,)),
                        pltpu.SemaphoreType.DMA((6,))],
        operands=(x2d, g_mixer, placed_in), semantics=("arbitrary", "arbitrary"), side=side, prefetch=pos,
        aliases={2: 2}, side_start=(2, 0))


def _retention_consts(seq):
    half = QK_DIM // 2
    inv = ROPE_BASE ** (-jnp.arange(half, dtype=F32) / half)
    inv = jnp.concatenate([inv, inv])
    coarse = (jnp.arange(seq // half, dtype=F32) * half)[:, None] * inv[None, :]
    fine = jnp.arange(half, dtype=F32)[:, None] * inv[None, :]
    ca, sa, cb, sb = jnp.cos(coarse)[:, None], jnp.sin(coarse)[:, None], jnp.cos(fine)[None], jnp.sin(fine)[None]
    cos_full = (ca * cb - sa * sb).reshape(seq, QK_DIM)
    sign = jnp.concatenate([-jnp.ones((half,), F32), jnp.ones((half,), F32)])
    sin_signed = (sa * cb + ca * sb).reshape(seq, QK_DIM) * sign
    log_g = jnp.log(1.0 - 2.0 ** (-5.0 - jnp.arange(RET_HEADS, dtype=F32)))
    idx = jnp.arange(CHUNK, dtype=F32)
    diff = idx[:, None] - idx[None, :]
    decay_in = jnp.where(diff[None] >= 0, jnp.exp(jnp.maximum(diff, 0.0)[None] * log_g[:, None, None]), 0.0)
    zeta = jnp.exp((CHUNK - 1.0 - idx)[None, :] * log_g[:, None])
    xi = jnp.exp((idx + 1.0)[None, :] * log_g[:, None])
    zeta = jnp.broadcast_to(zeta[:, :, None], (RET_HEADS, CHUNK, QK_DIM))
    xi = jnp.broadcast_to(xi[:, :, None], (RET_HEADS, CHUNK, QK_DIM))
    return cos_full, sin_signed, decay_in, zeta, xi


def _qkv(pj_ref, rows, h):
    q = pj_ref[rows, h * QK_DIM:(h + 1) * QK_DIM]
    k = pj_ref[rows, RET_HEADS * QK_DIM + h * QK_DIM:RET_HEADS * QK_DIM + (h + 1) * QK_DIM]
    v = pj_ref[rows, 2 * RET_HEADS * QK_DIM + h * V_DIM:2 * RET_HEADS * QK_DIM + (h + 1) * V_DIM]
    return q, k, v


def _retention_fwd_call(pj0, consts, n_seq, seq):
    cos_full, sin_signed, decay_in, zeta, xi = consts
    nb = seq // RET_TILE
    cpb = RET_TILE // CHUNK

    def body(pj_ref, cos_ref, sin_ref, d_ref, zeta_ref, xi_ref, o_ref, st_ref, state):
        @pl.when(pl.program_id(1) == 0)
        def _():
            state[...] = jnp.zeros_like(state)

        for cc in range(cpb):
            rows = slice(cc * CHUNK, (cc + 1) * CHUNK)
            cs, sn = cos_ref[rows, :], sin_ref[rows, :]
            for h in range(RET_HEADS):
                q, k, v = _qkv(pj_ref, rows, h)
                qt = (q * cs + _rot(q) * sn) * QK_SCALE
                kt = k * cs + _rot(k) * sn
                prev = state[h]
                st_ref[cc, h] = prev.astype(st_ref.dtype)
                scores = _mm_nt(qt, kt) * d_ref[h]
                o_ref[rows, h * V_DIM:(h + 1) * V_DIM] = _mm(scores, v) + _mm(qt * xi_ref[h], prev)
                state[h] = _mm_tn(kt * zeta_ref[h], v) + CHUNK_DECAY[h] * prev

    row = lambda b, n: (b * nb + n, 0)
    pos = lambda b, n: (n, 0)
    return pl.pallas_call(
        body, name="retention_fwd", grid=(n_seq, nb),
        in_specs=[pl.BlockSpec((None, RET_TILE, SHARD_W), lambda b, n: (0,) + row(b, n)),
                  pl.BlockSpec((RET_TILE, QK_DIM), pos),
                  pl.BlockSpec((RET_TILE, QK_DIM), pos), _resident(decay_in.shape), _resident(zeta.shape),
                  _resident(xi.shape)],
        out_specs=[pl.BlockSpec((RET_TILE, D_MODEL), row),
                   pl.BlockSpec((cpb, RET_HEADS, QK_DIM, V_DIM), lambda b, n: (b * nb + n, 0, 0, 0))],
        out_shape=[jax.ShapeDtypeStruct((n_seq * seq, D_MODEL), F32),
                   jax.ShapeDtypeStruct((n_seq * seq // CHUNK, RET_HEADS, QK_DIM, V_DIM), MXU_DTYPE)],
        scratch_shapes=[pltpu.VMEM((RET_HEADS, QK_DIM, V_DIM), F32)],
        compiler_params=_params(("arbitrary", "arbitrary")),
    )(pj0, cos_full, sin_signed, decay_in, zeta, xi)


def _merge_bwd_call(wide, pj1, pj2, pj3, ret_raw, ws, bst, w_ro, w_so, w_o):
    t = wide.shape[1]

    def body(wide_ref, pj1_ref, pj2_ref, pj3_ref, rr_ref, ws_ref, bst_ref, wro_ref, wso_ref, wo_ref,
             dpj1_ref, dpj2_ref, dpj3_ref, drr_ref, da_ref, db_ref, dws_ref, dbst_ref, mixed_ref):
        @pl.when(pl.program_id(0) == 0)
        def _():
            dws_ref[...] = jnp.zeros_like(dws_ref)
            dbst_ref[...] = jnp.zeros_like(dbst_ref)

        wsc = _causal_ws(ws_ref)
        f = _branch_forward(pj1_ref[...], pj2_ref[...], rr_ref[...], wsc, bst_ref[...], mixed_ref)
        pj3 = pj3_ref[...]
        smr, sms = _sigmoid(pj3[:, :D_MODEL]), _sigmoid(pj3[:, D_MODEL:])
        dmerged = _mm_nt(wide_ref[0], wo_ref[...])
        da_f, db_f = dmerged * smr, dmerged * sms
        da, db = da_f.astype(MXU_DTYPE), db_f.astype(MXU_DTYPE)
        dpj3_ref[:, :D_MODEL] = (da_f * wide_ref[1] * (1.0 - smr)).astype(dpj3_ref.dtype)
        dpj3_ref[:, D_MODEL:] = (db_f * wide_ref[2] * (1.0 - sms)).astype(dpj3_ref.dtype)
        da_ref[...] = da
        db_ref[...] = db

        dret = _mm_nt(da, wro_ref[...])
        dpj1_ref[:, :D_MODEL] = (dret * f["rn"] * f["dsil_rg"]).astype(dpj1_ref.dtype)
        drn = dret * f["sil_rg"]
        for h in range(RET_HEADS):
            cols = slice(h * V_DIM, (h + 1) * V_DIM)
            drr_ref[:, cols] = _unit_norm_bwd(drn[:, cols], f["rn"][:, cols], f["rstd_r"][h]).astype(drr_ref.dtype)

        dsgu = _mm_nt(db, wso_ref[...])
        dpj2_ref[:, D_MODEL:] = (dsgu * f["u"] * f["mixed"] * f["dsil_sg"]).astype(dpj2_ref.dtype)
        tg = dsgu * f["sil_sg"]
        dpj1_ref[:, D_MODEL:] = (tg * f["mixed"] * f["du"]).astype(dpj1_ref.dtype)
        dmixed = tg * f["u"]
        for cc in range(TILE_M // CHUNK):
            for g in range(SGU_GROUPS):
                rs, cs = slice(cc * CHUNK, (cc + 1) * CHUNK), slice(g * GROUP_DIM, (g + 1) * GROUP_DIM)
                dm = dmixed[rs, cs]
                mixed_ref[rs, cs] = _mm_tn(wsc[g], dm)
                dws_ref[g] += _mm_nt(dm, f["vn"][rs, cs])
                dbst_ref[:, g:g + 1] += jnp.sum(dm, axis=1, keepdims=True)
        dvv = _unit_norm_bwd(mixed_ref[...], f["vn"], f["rstd_v"])
        dpj2_ref[:, :D_MODEL] = (dvv * f["dgelu_sv"]).astype(dpj2_ref.dtype)

    sq = (D_MODEL, D_MODEL)
    return pl.pallas_call(
        body, name="merge_bwd", grid=(t // TILE_M,),
        in_specs=[pl.BlockSpec((3, TILE_M, D_MODEL), lambda i: (0, i, 0))] + [_proj_tile(j) for j in (1, 2, 3)]
        + [_row_tile(D_MODEL)] + [_resident(ws.shape), _resident(bst.shape), _resident(sq), _resident(sq), _resident(sq)],
        out_specs=[_row_tile(SHARD_W)] * 3 + [_row_tile(D_MODEL)] * 3
        + [pl.BlockSpec(ws.shape, lambda i: (0, 0, 0)), pl.BlockSpec(bst.shape, lambda i: (0, 0))],
        out_shape=[jax.ShapeDtypeStruct((t, SHARD_W), MXU_DTYPE)] * 3 + [jax.ShapeDtypeStruct((t, D_MODEL), MXU_DTYPE)] * 3
        + [jax.ShapeDtypeStruct(ws.shape, F32), jax.ShapeDtypeStruct(bst.shape, F32)],
        scratch_shapes=[pltpu.VMEM((TILE_M, D_MODEL), F32)], compiler_params=_params(("arbitrary",)),
    )(wide, pj1, pj2, pj3, ret_raw, ws, bst, w_ro, w_so, w_o)


def _tail_call(pj, ret_raw, x2d, p2d, target, ws, bst, g_ple, g_final, w_ro, w_so, w_o, w_pg, w_pp):
    t = x2d.shape[0]
    tm = TILE_M

    def body(pj1_ref, pj2_ref, pj3_ref, rr_ref, x_ref, p_ref, t_ref, ws_ref, bst_ref, gp_ref, gf_ref,
             wro_ref, wso_ref, wo_ref, wpg_ref, wpp_ref,
             wide_ref, half_ref, loss_ref, dgp_ref, dgf_ref, mixed_ref):
        dx1_ref, a_ref, b_ref = (wide_ref.at[k] for k in range(3))
        ret_ref, sgu_ref, mg_ref, hp_ref, dz_ref, dpp_ref = (half_ref.at[k] for k in range(6))
        @pl.when(pl.program_id(0) == 0)
        def _():
            for acc in (loss_ref, dgp_ref, dgf_ref):
                acc[...] = jnp.zeros_like(acc)

        f = _branch_forward(pj1_ref[...], pj2_ref[...], rr_ref[...], _causal_ws(ws_ref), bst_ref[...], mixed_ref)
        ret = f["ret"].astype(MXU_DTYPE)
        sgu = f["sgu"].astype(MXU_DTYPE)
        ret_ref[...] = ret
        sgu_ref[...] = sgu
        a = jnp.dot(ret, wro_ref[...], preferred_element_type=F32)
        b = jnp.dot(sgu, wso_ref[...], preferred_element_type=F32)
        a_ref[...] = a
        b_ref[...] = b
        pj3 = pj3_ref[...]
        smr, sms = _sigmoid(pj3[:, :D_MODEL]), _sigmoid(pj3[:, D_MODEL:])
        merged = (smr * a + sms * b).astype(MXU_DTYPE)
        mg_ref[...] = merged
        x1v = x_ref[...] + jnp.dot(merged, wo_ref[...], preferred_element_type=F32)
        xn1, r1 = _rms(x1v)
        hp = (xn1 * gp_ref[...]).astype(MXU_DTYPE)
        hp_ref[...] = hp
        gate = _sigmoid(jnp.dot(hp, wpg_ref[...], preferred_element_type=F32))
        pp = jnp.dot(p_ref[...].astype(MXU_DTYPE), wpp_ref[...], preferred_element_type=F32)
        xn2, r2 = _rms(x1v + gate * pp)
        err = xn2 * gf_ref[...] - t_ref[...]
        loss_ref[...] += (0.5 / D_MODEL) * jnp.sum(jnp.sum(err * err, axis=1, keepdims=True), axis=0, keepdims=True)

        dy = err * (1.0 / D_MODEL)
        dgf_ref[...] += jnp.sum(dy * xn2, axis=0, keepdims=True)
        dx2 = _rms_bwd(dy * gf_ref[...], xn2, r2)
        dpp_ref[...] = (dx2 * gate).astype(MXU_DTYPE)
        dz = (dx2 * pp * gate * (1.0 - gate)).astype(MXU_DTYPE)
        dz_ref[...] = dz
        dhp = _mm_nt(dz, wpg_ref[...])
        dgp_ref[...] += jnp.sum(dhp * xn1, axis=0, keepdims=True)
        dx1_ref[...] = dx2 + _rms_bwd(dhp * gp_ref[...], xn1, r1)

    sq = (D_MODEL, D_MODEL)
    vec = _resident((1, D_MODEL))
    whole = lambda shape: pl.BlockSpec(shape, lambda i: (0,) * len(shape))
    return pl.pallas_call(
        body, name="tail_fwd", grid=(t // tm,),
        in_specs=[_proj_tile(j) for j in (1, 2, 3)]
        + [_row_tile(D_MODEL), _row_tile(D_MODEL), _row_tile(PLE_DIM), _row_tile(D_MODEL), _resident(ws.shape),
           _resident(bst.shape), vec, vec, _resident(sq), _resident(sq), _resident(sq), _resident(sq),
           _resident((PLE_DIM, D_MODEL))],
        out_specs=[pl.BlockSpec((3, tm, D_MODEL), lambda i: (0, i, 0)), pl.BlockSpec((6, tm, D_MODEL), lambda i: (0, i, 0)),
                   whole(LOSS_TILE), whole((1, D_MODEL)), whole((1, D_MODEL))],
        out_shape=[jax.ShapeDtypeStruct((3, t, D_MODEL), F32), jax.ShapeDtypeStruct((6, t, D_MODEL), MXU_DTYPE),
                   jax.ShapeDtypeStruct(LOSS_TILE, F32), jax.ShapeDtypeStruct((1, D_MODEL), F32),
           jax.ShapeDtypeStruct((1, D_MODEL), F32)],
        scratch_shapes=[pltpu.VMEM((TILE_M, D_MODEL), F32)], compiler_params=_params(("arbitrary",)),
    )(pj, pj, pj, ret_raw, x2d, p2d, target, ws, bst, g_ple, g_final, w_ro, w_so, w_o, w_pg, w_pp)


def _retention_bwd_call(pj0, drr, states, consts, n_seq, seq, side):
    cos_full, sin_signed, decay_in, zeta, xi = consts
    nb = seq // RET_TILE
    cpb = RET_TILE // CHUNK

    def body(pj_ref, do_ref, st_ref, cos_ref, sin_ref, d_ref, zeta_ref, xi_ref, dpj_ref, gstate):
        @pl.when(pl.program_id(1) == 0)
        def _():
            gstate[...] = jnp.zeros_like(gstate)

        for cc in reversed(range(cpb)):
            rows = slice(cc * CHUNK, (cc + 1) * CHUNK)
            cs, sn = cos_ref[rows, :], sin_ref[rows, :]
            for h in range(RET_HEADS):
                q, k, v = _qkv(pj_ref, rows, h)
                qt = (q * cs + _rot(q) * sn) * QK_SCALE
                kt = k * cs + _rot(k) * sn
                d_out = do_ref[rows, h * V_DIM:(h + 1) * V_DIM]
                prev = st_ref[cc, h]
                g = gstate[h]
                dec = d_ref[h]
                scores_d = _mm_nt(qt, kt) * dec
                dscores = _mm_nt(d_out, v) * dec
                kz = kt * zeta_ref[h]
                qx = qt * xi_ref[h]
                dv = _mm_tn(scores_d, d_out) + _mm(kz, g)
                dqt = (_mm(dscores, kt) + _mm_nt(d_out, prev) * xi_ref[h]) * QK_SCALE
                dkt = _mm_tn(dscores, qt) + _mm_nt(v, g) * zeta_ref[h]
                gstate[h] = _mm_tn(qx, d_out) + CHUNK_DECAY[h] * g
                dq = dqt * cs + _rot(dqt * sn)
                dk = dkt * cs + _rot(dkt * sn)
                dpj_ref[rows, h * QK_DIM:(h + 1) * QK_DIM] = dq.astype(dpj_ref.dtype)
                dpj_ref[rows, RET_HEADS * QK_DIM + h * QK_DIM:RET_HEADS * QK_DIM + (h + 1) * QK_DIM] = dk.astype(
                    dpj_ref.dtype)
                dpj_ref[rows, 2 * RET_HEADS * QK_DIM + h * V_DIM:2 * RET_HEADS * QK_DIM + (h + 1) * V_DIM] = dv.astype(
                    dpj_ref.dtype)

    row = lambda b, n: (b * nb + nb - 1 - n, 0)
    pos = lambda b, n: (nb - 1 - n, 0)
    return _compute_call(
        body, name="retention_bwd", grid=(n_seq, nb),
        in_specs=[pl.BlockSpec((None, RET_TILE, SHARD_W), lambda b, n: (0,) + row(b, n)),
                  pl.BlockSpec((RET_TILE, D_MODEL), row),
                  pl.BlockSpec((cpb, RET_HEADS, QK_DIM, V_DIM), lambda b, n: (b * nb + nb - 1 - n, 0, 0, 0)),
                  pl.BlockSpec((RET_TILE, QK_DIM), pos), pl.BlockSpec((RET_TILE, QK_DIM), pos),
                  _resident(decay_in.shape), _resident(zeta.shape), _resident(xi.shape)],
        out_specs=[pl.BlockSpec((RET_TILE, SHARD_W), row)],
        out_shape=[jax.ShapeDtypeStruct((n_seq * seq, SHARD_W), MXU_DTYPE)],
        scratch_shapes=[pltpu.VMEM((RET_HEADS, QK_DIM, V_DIM), F32)],
        operands=(pj0, drr, states, cos_full, sin_signed, decay_in, zeta, xi),
        semantics=("arbitrary", "arbitrary"), side=side)


def _dx_call(dpj, x2d, wide, g_mixer, wg_in, side):
    t = x2d.shape[0]

    def body(d0, d1, d2, d3, x_ref, dx1_ref, g_ref, w_ref, dx_ref, dg_ref):
        @pl.when(pl.program_id(0) == 0)
        def _():
            dg_ref[...] = jnp.zeros_like(dg_ref)

        dh = _mm_nt(d0[...], w_ref[0])
        for j, d_ref in enumerate((d1, d2, d3)):
            dh += _mm_nt(d_ref[...], w_ref[j + 1])
        xn, r = _rms(x_ref[...])
        dg_ref[...] += jnp.sum(dh * xn, axis=0, keepdims=True)
        dx_ref[...] = dx1_ref[...] + _rms_bwd(dh * g_ref[...], xn, r)

    return _compute_call(
        body, name="dx_bwd", grid=(t // DX_TILE,),
        in_specs=[_row_tile(SHARD_W, DX_TILE)] * N_CHIPS
        + [_row_tile(D_MODEL, DX_TILE), pl.BlockSpec((None, DX_TILE, D_MODEL), lambda i: (0, i, 0))]
        + [_resident((1, D_MODEL)), _resident(wg_in.shape)],
        out_specs=[_row_tile(D_MODEL, DX_TILE), pl.BlockSpec((1, D_MODEL), lambda i: (0, 0))],
        out_shape=[jax.ShapeDtypeStruct((t, D_MODEL), F32), jax.ShapeDtypeStruct((1, D_MODEL), F32)],
        operands=(*dpj, x2d, wide, g_mixer, wg_in), semantics=("arbitrary",), side=side)


def _wgrad_call(name, lhs, rhs, block_n, out_cols=None, block_t=1024, into=None, slot=0, n_slots=1, side=None):
    (lhs, lhs_at), (rhs, rhs_at) = [x if isinstance(x, tuple) else (x, None) for x in (lhs, rhs)]
    t, n = rhs.shape[-2:]
    k = lhs.shape[-1]
    block_t = min(block_t, t)

    def operand_spec(at, block, index):
        if at is None:
            return pl.BlockSpec(block, index)
        return pl.BlockSpec((None,) + block, lambda j, i: (at,) + index(j, i))

    steps = t // block_t
    out_cols = block_n if out_cols is None else out_cols
    per = block_n // out_cols
    first_block = slot * (n // block_n)

    def body(l_ref, r_ref, *rest):
        o_ref, acc = rest[-2:]
        @pl.when(pl.program_id(1) == 0)
        def _():
            acc[...] = jnp.zeros_like(acc)

        acc[...] += _mm_tn(l_ref[...], r_ref[...])

        @pl.when(pl.program_id(1) == steps - 1)
        def _():
            for s in range(per):
                o_ref[s] = acc[:, s * out_cols:(s + 1) * out_cols].astype(o_ref.dtype)

    res = _compute_call(
        body, name=name, grid=(n // block_n, steps),
        in_specs=[operand_spec(lhs_at, (block_t, k), lambda j, i: (i, 0)),
                  operand_spec(rhs_at, (block_t, block_n), lambda j, i: (i, j))] + ([] if into is None else [ANY]),
        out_specs=[pl.BlockSpec((per, k, out_cols), lambda j, i: (first_block + j, 0, 0))],
        out_shape=[jax.ShapeDtypeStruct((n_slots * (n // out_cols), k, out_cols), COMM_DTYPE)],
        scratch_shapes=[pltpu.VMEM((k, block_n), F32)], aliases={} if into is None else {2: 0},
        operands=(lhs, rhs) if into is None else (lhs, rhs, into), semantics=("arbitrary", "arbitrary"), side=side)
    return res[0] if side is None else (res[0][0], res[1])


def _position():
    return lax.axis_index("x"), lax.axis_index("y"), lax.axis_index("c")


def _position_array():
    x, y, c = _position()
    return jnp.stack([2 * x + y, c]).astype(jnp.int32)


def _other_chip(x, y, k):
    return (1 - x if k & 2 else x), (1 - y if k & 1 else y)


def _plan_side(operands, out_shapes, plan, n_remote, aliases=None):
    def copies(ins, outs, send_sem, recv_sem, base=0):
        remote = plan(ins, outs)
        assert len(remote) == n_remote
        return [pltpu.make_async_remote_copy(src_ref=src, dst_ref=dst, send_sem=send_sem.at[base + i],
                                             recv_sem=recv_sem.at[base + i], device_id=dev, device_id_type=MESH)
                for i, (src, dst, dev) in enumerate(remote)]

    def start(*a):
        for cp in copies(*a):
            cp.start()

    def finish(*a):
        for cp in copies(*a):
            cp.wait()

    return _Side(operands, out_shapes, n_remote, start, finish, aliases)


def _place_cast_call(name, shards, pos):
    n = len(shards)
    rows, width = shards[0].shape
    block_rows = min(256, rows)

    def body(pos_ref, *refs):
        for w_ref, o_ref in zip(refs[:n], refs[n:]):
            o_ref[...] = w_ref[...].astype(o_ref.dtype)

    return pl.pallas_call(
        body, name=name,
        grid_spec=pltpu.PrefetchScalarGridSpec(
            num_scalar_prefetch=1, grid=(rows // block_rows,),
            in_specs=[pl.BlockSpec((block_rows, width), lambda i, pos: (i, 0))] * n,
            out_specs=[pl.BlockSpec((None, block_rows, width), lambda i, pos: (pos[0], i, 0))] * n),
        out_shape=[jax.ShapeDtypeStruct((N_CHIPS, rows, width), MXU_DTYPE)] * n,
        compiler_params=_params(("arbitrary",)),
    )(pos, *shards)


def _gather_side(placed):
    n = len(placed)

    def copies(kind, bufs, send_sem, recv_sem, base):
        x, y, c = _position()
        me = 2 * x + y
        made = []
        for i in range(n):
            hr = placed[i].shape[1] // 2
            for k in (1, 2, 3):
                px, py = _other_chip(x, y, k)
                chip, core, slot, dev = [(me, c, k - 1, (px, py, c)), (2 * px + py, c, 2 + k, (x, y, 1 - c)),
                                         (2 * px + py, 1 - c, 2 + k, (x, y, 1 - c))][kind]
                piece = bufs[i].at[chip, pl.ds(core * hr, hr)]
                made.append(pltpu.make_async_remote_copy(
                    src_ref=piece, dst_ref=piece, send_sem=send_sem.at[base + 6 * i + slot],
                    recv_sem=recv_sem.at[base + 6 * i + slot], device_id=dev, device_id_type=MESH))
        return made

    def start(ins, outs, send_sem, recv_sem, base=0):
        for cp in copies(0, outs, send_sem, recv_sem, base):
            cp.start()

    def finish(ins, outs, send_sem, recv_sem, base=0):
        first, onward = copies(0, outs, send_sem, recv_sem, base), copies(1, outs, send_sem, recv_sem, base)
        for landed, cp in zip(first, onward):
            landed.wait_recv()
            cp.start()
        for cp in copies(2, outs, send_sem, recv_sem, base):
            cp.wait_recv()
        for cp in first + onward:
            cp.wait_send()

    return _Side(placed, [jax.ShapeDtypeStruct(a.shape, a.dtype) for a in placed], 6 * n, start, finish,
                 {i: i for i in range(n)})


def _same_shape(shapes):
    found = {}
    for i, shape in enumerate(shapes):
        found.setdefault(shape, []).append(i)
    return found


def _pair_stage(tag, grads):
    n = len(grads)
    shapes = [(g.shape[0] // N_CHIPS, g.shape[1]) for g in grads]
    kinds = list(_same_shape(shapes))
    chunk = {(r, w): min(r // 2, PAIR_CHUNK_ROWS) for r, w in kinds}
    work = [(i, j, q) for i, (r, w) in enumerate(shapes) for j in range(N_CHIPS)
            for q in range((r // 2) // chunk[(r, w)])]

    def body(*refs):
        ins, sums, landed = refs[:n], refs[n:2 * n], refs[2 * n:3 * n]
        bufs, (load_sem, store_sem, send_sem, recv_sem) = refs[3 * n:-4], refs[-4:]
        x, y, c = _position()

        def piece(i, j):
            r = shapes[i][0]
            return pltpu.make_async_remote_copy(
                src_ref=ins[i].at[pl.ds(j * r + (1 - c) * (r // 2), r // 2)],
                dst_ref=landed[i].at[pl.ds(j * (r // 2), r // 2)], send_sem=send_sem.at[N_CHIPS * i + j],
                recv_sem=recv_sem.at[N_CHIPS * i + j], device_id=(x, y, 1 - c), device_id_type=MESH)

        def buffers(i, slot):
            own, got, out = bufs[3 * kinds.index(shapes[i]):3 * kinds.index(shapes[i]) + 3]
            return own.at[slot], got.at[slot], out.at[slot]

        def loads(step):
            i, j, q = work[step]
            r, ch = shapes[i][0], chunk[shapes[i]]
            own, got, _ = buffers(i, step % 2)
            return (pltpu.make_async_copy(ins[i].at[pl.ds(j * r + c * (r // 2) + q * ch, ch)], own,
                                          load_sem.at[step % 2, 0]),
                    pltpu.make_async_copy(landed[i].at[pl.ds(j * (r // 2) + q * ch, ch)], got,
                                          load_sem.at[step % 2, 1]))

        def store(step):
            i, j, q = work[step]
            ch = chunk[shapes[i]]
            return pltpu.make_async_copy(buffers(i, step % 2)[2],
                                         sums[i].at[pl.ds(j * (shapes[i][0] // 2) + q * ch, ch)],
                                         store_sem.at[step % 2])

        def begin(step):
            i, j, q = work[step]
            if q == 0:
                piece(i, j).wait_recv()
            for cp in loads(step):
                cp.start()

        for i in range(n):
            for j in range(N_CHIPS):
                piece(i, j).start()
        begin(0)
        for step in range(len(work)):
            if step + 1 < len(work):
                begin(step + 1)
            for cp in loads(step):
                cp.wait()
            if step >= 2:
                store(step - 2).wait()
            own, got, out = buffers(work[step][0], step % 2)
            out[...] = (own[...].astype(F32) + got[...].astype(F32)).astype(out.dtype)
            store(step).start()
        for step in range(max(len(work) - 2, 0), len(work)):
            store(step).wait()
        for i in range(n):
            for j in range(N_CHIPS):
                piece(i, j).wait_send()

    half = [jax.ShapeDtypeStruct((N_CHIPS * (r // 2), w), COMM_DTYPE) for r, w in shapes]
    res = pl.pallas_call(
        body, name=f"grad_pair_sum_{tag}", in_specs=[ANY] * n, out_specs=[ANY] * (2 * n), out_shape=half + half,
        scratch_shapes=[pltpu.VMEM((2, chunk[k], k[1]), COMM_DTYPE) for k in kinds for _ in range(3)]
        + [pltpu.SemaphoreType.DMA((2, 2)), pltpu.SemaphoreType.DMA((2,)), pltpu.SemaphoreType.DMA((N_CHIPS * n,)),
           pltpu.SemaphoreType.DMA((N_CHIPS * n,))],
        compiler_params=pltpu.CompilerParams(has_side_effects=True, vmem_limit_bytes=VMEM_LIMIT),
    )(*grads)
    return list(res[:n])


def _chip_side(pair_sums):
    halves = [(p.shape[0] // N_CHIPS, p.shape[1]) for p in pair_sums]

    def plan(ins, outs):
        x, y, c = _position()
        remote = []
        for i, (hr, _) in enumerate(halves):
            for k in (1, 2, 3):
                px, py = _other_chip(x, y, k)
                remote.append((ins[i].at[pl.ds((2 * px + py) * hr, hr)], outs[i].at[pl.ds((k - 1) * hr, hr)],
                               (px, py, c)))
        return remote

    return _plan_side(pair_sums, [jax.ShapeDtypeStruct((3 * hr, w), COMM_DTYPE) for hr, w in halves], plan,
                      3 * len(pair_sums))


def _finish_stage(tag, pair_sums, chip, extra):
    n = len(pair_sums)
    halves = [(p.shape[0] // N_CHIPS, p.shape[1]) for p in pair_sums]
    kinds = list(_same_shape(halves))
    chunk = {(hr, w): min(hr, PAIR_CHUNK_ROWS) for hr, w in kinds}
    work = [(i, q) for i, (hr, w) in enumerate(halves) for q in range(hr // chunk[(hr, w)])]
    e_in, e_out = len(extra.operands), len(extra.out_shapes)

    def body(*refs):
        sums, got, refs = refs[:n], refs[n:2 * n], refs[2 * n:]
        extra_ins, refs = refs[:e_in], refs[e_in:]
        full, refs = refs[:n], refs[n:]
        extra_outs, refs = refs[:e_out], refs[e_out:]
        bufs, (load_sem, store_sem, send_sem, recv_sem, extra_send, extra_recv) = refs[:-6], refs[-6:]
        x, y, c = _position()
        me = 2 * x + y
        extra.start(extra_ins, extra_outs, extra_send, extra_recv)

        def buffers(i, slot):
            at = 5 * kinds.index(halves[i])
            return [b_.at[slot] for b_ in bufs[at:at + 5]]

        def loads(step):
            i, q = work[step]
            hr, ch = halves[i][0], chunk[halves[i]]
            into = buffers(i, step % 2)
            made = [pltpu.make_async_copy(sums[i].at[pl.ds(me * hr + q * ch, ch)], into[0], load_sem.at[step % 2, 0])]
            made += [pltpu.make_async_copy(got[i].at[pl.ds(k * hr + q * ch, ch)], into[1 + k], load_sem.at[step % 2, 1 + k])
                     for k in range(3)]
            return made

        def stores(step):
            i, q = work[step]
            hr, ch = halves[i][0], chunk[halves[i]]
            result = buffers(i, step % 2)[4]
            rows = full[i].at[pl.ds(c * hr + q * ch, ch)]
            return (pltpu.make_async_copy(result, rows, store_sem.at[step % 2]),
                    pltpu.make_async_remote_copy(src_ref=result, dst_ref=rows, send_sem=send_sem.at[step],
                                                 recv_sem=recv_sem.at[step], device_id=(x, y, 1 - c),
                                                 device_id_type=MESH))

        for cp in loads(0):
            cp.start()
        for step in range(len(work)):
            if step + 1 < len(work):
                for cp in loads(step + 1):
                    cp.start()
            for cp in loads(step):
                cp.wait()
            if step >= 2:
                stores(step - 2)[0].wait()
                stores(step - 2)[1].wait_send()
            mine, k1, k2, k3, result = buffers(work[step][0], step % 2)
            result[...] = ((mine[...].astype(F32) + k1[...].astype(F32)) + k2[...].astype(F32)) + k3[...].astype(F32)
            for cp in stores(step):
                cp.start()
        for step in range(max(len(work) - 2, 0), len(work)):
            stores(step)[0].wait()
            stores(step)[1].wait_send()
        for step in range(len(work)):
            stores(step)[1].wait_recv()
        extra.finish(extra_ins, extra_outs, extra_send, extra_recv)

    scratch = []
    for k in kinds:
        scratch += [pltpu.VMEM((2, chunk[k], k[1]), COMM_DTYPE)] * 4 + [pltpu.VMEM((2, chunk[k], k[1]), F32)]
    res = pl.pallas_call(
        body, name=f"grad_finish_{tag}", in_specs=[ANY] * (2 * n + e_in), out_specs=[ANY] * (n + e_out),
        out_shape=[jax.ShapeDtypeStruct((2 * hr, w), F32) for hr, w in halves] + extra.out_shapes,
        scratch_shapes=scratch + [pltpu.SemaphoreType.DMA((2, 4)), pltpu.SemaphoreType.DMA((2,)),
                                  pltpu.SemaphoreType.DMA((len(work),)), pltpu.SemaphoreType.DMA((len(work),)),
                                  pltpu.SemaphoreType.DMA((extra.n_sems,)), pltpu.SemaphoreType.DMA((extra.n_sems,))],
        compiler_params=pltpu.CompilerParams(has_side_effects=True, vmem_limit_bytes=VMEM_LIMIT),
    )(*pair_sums, *chip, *extra.operands)
    return list(res[:n]), list(res[n:])


def _small_gather_side(parts):
    n = len(parts)

    def copies(ins, outs, send_sem, recv_sem, base):
        x, y, c = _position()
        made = []
        for i in range(n):
            mine = outs[i].at[4 * x + 2 * y + c]
            made.append(pltpu.make_async_copy(ins[i], mine, send_sem.at[base + 8 * i + 7]))
            for d in range(1, 8):
                px, py = _other_chip(x, y, d >> 1)
                made.append(pltpu.make_async_remote_copy(
                    src_ref=ins[i], dst_ref=mine, send_sem=send_sem.at[base + 8 * i + d - 1],
                    recv_sem=recv_sem.at[base + 8 * i + d - 1], device_id=(px, py, 1 - c if d & 1 else c),
                    device_id_type=MESH))
        return made

    def start(ins, outs, send_sem, recv_sem, base=0):
        for cp in copies(ins, outs, send_sem, recv_sem, base):
            cp.start()

    def finish(ins, outs, send_sem, recv_sem, base=0):
        for cp in copies(ins, outs, send_sem, recv_sem, base):
            cp.wait()

    return _Side(parts, [jax.ShapeDtypeStruct((8,) + a_.shape, F32) for a_ in parts], 8 * n, start, finish)


def _adamw(w, g, m, v):
    m = ADAM_B1 * m + (1.0 - ADAM_B1) * g
    v = ADAM_B2 * v + (1.0 - ADAM_B2) * (g * g)
    m_hat = m / (1.0 - ADAM_B1 ** ADAM_STEP)
    v_hat = v / (1.0 - ADAM_B2 ** ADAM_STEP)
    delta = -ADAM_LR * (m_hat / (jnp.sqrt(v_hat) + ADAM_EPS) + ADAM_WD * w)
    return delta, m, v


def _adamw_call(name, ws, gs, ms, vs):
    n = len(ws)
    rows, width = ws[0].shape
    block_rows = min(256 // n, rows)

    def body(*refs):
        for i in range(n):
            w_ref, g_ref, m_ref, v_ref = (refs[j * n + i] for j in range(4))
            d_out, m_out, v_out = (refs[(4 + j) * n + i] for j in range(3))
            d_out[...], m_out[...], v_out[...] = _adamw(w_ref[...], g_ref[...], m_ref[...], v_ref[...])

    spec = pl.BlockSpec((block_rows, width), lambda i: (i, 0))
    res = pl.pallas_call(
        body, name=name, grid=(rows // block_rows,), in_specs=[spec] * (4 * n), out_specs=[spec] * (3 * n),
        out_shape=[jax.ShapeDtypeStruct(ws[0].shape, F32)] * (3 * n),
        compiler_params=_params(("arbitrary",)),
    )(*ws, *gs, *ms, *vs)
    return [(res[i], res[n + i], res[2 * n + i]) for i in range(n)]


def _small_adamw_call(gathered, weights, moments_m, moments_v):
    n = len(weights)

    def body(*refs):
        all_refs, refs = refs[:n + 1], refs[n + 1:]
        w_refs, m_refs, v_refs, outs = refs[:n], refs[n:2 * n], refs[2 * n:3 * n], refs[3 * n:]

        def total(ref):
            acc = ref[0]
            for d in range(1, 8):
                acc = acc + ref[d]
            return acc

        outs[0][...] = total(all_refs[n])
        for i in range(n):
            g = total(all_refs[i])
            if i == 0:
                row = lax.broadcasted_iota(jnp.int32, g.shape, 0)
                col = lax.broadcasted_iota(jnp.int32, g.shape, 1)
                g = jnp.where((row % CHUNK) >= col, g, 0.0)
            g_out, d_out, m_out, v_out = outs[1 + 4 * i:5 + 4 * i]
            g_out[...] = g
            d_out[...], m_out[...], v_out[...] = _adamw(w_refs[i][...], g, m_refs[i][...], v_refs[i][...])

    out_shape = [jax.ShapeDtypeStruct(LOSS_TILE, F32)]
    for w in weights:
        out_shape += [jax.ShapeDtypeStruct(w.shape, F32)] * 4
    res = pl.pallas_call(
        body, name="small_adamw", out_shape=out_shape,
        compiler_params=pltpu.CompilerParams(vmem_limit_bytes=VMEM_LIMIT),
    )(*gathered, *weights, *moments_m, *moments_v)
    return res[0], [res[1 + 4 * i:5 + 4 * i] for i in range(n)]


def kernel(x, p, w_in, w_ret_out, w_sgu_out, w_out, sgu_ws, sgu_bs, w_ple_gate, w_ple_proj, g_mixer, g_ple, g_final, loss_target, m_w_in, m_w_ret_out, m_w_sgu_out, m_w_out, m_sgu_ws, m_sgu_bs, m_w_ple_gate, m_w_ple_proj, m_g_mixer, m_g_ple, m_g_final, v_w_in, v_w_ret_out, v_w_sgu_out, v_w_out, v_sgu_ws, v_sgu_bs, v_w_ple_gate, v_w_ple_proj, v_g_mixer, v_g_ple, v_g_final):
    n_seq, seq, _ = x.shape
    t = n_seq * seq
    x2d = x.reshape(t, D_MODEL)
    p2d = p.reshape(t, PLE_DIM)
    target = loss_target.reshape(t, D_MODEL)
    big = [w_in[0], w_ret_out[0], w_sgu_out[0], w_out[0], w_ple_gate[0], w_ple_proj[0]]
    big_m = [m_w_in[0], m_w_ret_out[0], m_w_sgu_out[0], m_w_out[0], m_w_ple_gate[0], m_w_ple_proj[0]]
    big_v = [v_w_in[0], v_w_ret_out[0], v_w_sgu_out[0], v_w_out[0], v_w_ple_gate[0], v_w_ple_proj[0]]

    pos = _position_array()
    placed = (_place_cast_call("place_w_in", big[:1], pos) + _place_cast_call("place_square_weights", big[1:5], pos)
              + _place_cast_call("place_w_ple_proj", big[5:], pos))
    ws = sgu_ws[0]
    bst = sgu_bs[0].T
    consts = _retention_consts(seq)

    (h, pj, wg_in), gathered = _proj_call(x2d, g_mixer, placed[0], pos, _gather_side(placed[1:]))
    pj0 = pj1 = pj2 = pj3 = pj
    w_ro, w_so, w_o, w_pg = (w.reshape(D_MODEL, D_MODEL) for w in gathered[:4])
    w_pp = gathered[4].transpose(1, 0, 2).reshape(PLE_DIM, D_MODEL)
    ret_raw, states = _retention_fwd_call(pj0, consts, n_seq, seq)
    wide, half, loss, dg_ple, dg_final = _tail_call(
        pj, ret_raw, x2d, p2d, target, ws, bst, g_ple, g_final.reshape(1, D_MODEL), w_ro, w_so, w_o, w_pg, w_pp)
    ret, sgu, merged, hp, dz, dpp = ((half, k) for k in range(6))
    dpj1, dpj2, dpj3, drr, da, db, dws, dbst = _merge_bwd_call(wide, pj1, pj2, pj3, ret_raw, ws, bst, w_ro, w_so, w_o)

    small_shapes = [(SGU_GROUPS * CHUNK, CHUNK), (SGU_GROUPS, CHUNK), (1, D_MODEL), (1, D_MODEL), (1, D_MODEL)]
    as_small = lambda arrays: [None if a_ is None else a_.reshape(s_) for a_, s_ in zip(arrays, small_shapes)]
    early = as_small([dws, dbst.T, None, dg_ple, dg_final])
    rows_of = lambda g: g.reshape(g.shape[0] * g.shape[1], g.shape[2])
    tail_grads = [
        rows_of(_wgrad_call("wgrad_ret_out", ret, da, D_MODEL)),
        rows_of(_wgrad_call("wgrad_sgu_out", sgu, db, D_MODEL)),
        rows_of(_wgrad_call("wgrad_out", merged, (wide, 0), D_MODEL)),
        rows_of(_wgrad_call("wgrad_ple_gate", hp, dz, D_MODEL)),
        rows_of(_wgrad_call("wgrad_ple_proj", p2d, dpp, D_MODEL, out_cols=PLE_DIM)),
    ]
    tail_sums = _pair_stage("tail", tail_grads)
    (dpj0,), carried = _retention_bwd_call(
        pj0, drr, states, consts, n_seq, seq,
        _join_sides(_chip_side(tail_sums), _small_gather_side([early[0], early[1], early[3], early[4], loss])))
    tail_chip, early_all = carried[:len(tail_sums)], carried[len(tail_sums):]
    in_grad = None
    for j, d in enumerate((dpj0, dpj1, dpj2, dpj3)):
        in_grad = _wgrad_call(f"wgrad_in_{j}", h, d, SHARD_W, into=in_grad, slot=j, n_slots=N_CHIPS)
    in_sums = _pair_stage("in", [rows_of(in_grad)])
    (dx, dg_mixer), in_chip = _dx_call((dpj0, dpj1, dpj2, dpj3), x2d, wide, g_mixer, wg_in, _chip_side(in_sums))
    g_big, (mixer_all,) = _finish_stage("all", in_sums + tail_sums, list(in_chip) + list(tail_chip),
                                        _small_gather_side([dg_mixer]))
    upd = [None] * len(big)
    for _, which in _same_shape([w.shape for w in big]).items():
        pick = lambda arrays: [arrays[i] for i in which]
        for i, triple in zip(which, _adamw_call(f"adamw_{which[0]}", pick(big), pick(g_big), pick(big_m), pick(big_v))):
            upd[i] = triple

    small_g = [early_all[0], early_all[1], mixer_all, early_all[2], early_all[3], early_all[4]]
    total, small = _small_adamw_call(small_g, as_small([sgu_ws, sgu_bs, g_mixer, g_ple, g_final]),
                                     as_small([m_sgu_ws, m_sgu_bs, m_g_mixer, m_g_ple, m_g_final]),
                                     as_small([v_sgu_ws, v_sgu_bs, v_g_mixer, v_g_ple, v_g_final]))
    out_small_shapes = [sgu_ws.shape, sgu_bs.shape, g_mixer.shape, g_ple.shape, g_final.shape]

    def ordered(big_list, kind):
        w_in_, w_ro_, w_so_, w_o_, w_pg_, w_pp_ = [b_[None] for b_ in big_list]
        s_ws, s_bs, s_gm, s_gp, s_gf = [small[i][kind].reshape(s) for i, s in enumerate(out_small_shapes)]
        return [w_in_, w_ro_, w_so_, w_o_, s_ws, s_bs, w_pg_, w_pp_, s_gm, s_gp, s_gf]

    out = [total[0, 0], dx.reshape(x.shape)]
    out += ordered(g_big, 0)
    out += ordered([u[0] for u in upd], 1)
    out += ordered([u[1] for u in upd], 2)
    out += ordered([u[2] for u in upd], 3)
    return tuple(out)
```

```python
import functools
import math

import numpy as np
import jax
import jax.numpy as jnp
from jax import lax
from jax.experimental import pallas as pl
from jax.experimental.pallas import tpu as pltpu

F32 = jnp.float32
MXU_DTYPE = jnp.bfloat16
COMM_DTYPE = jnp.bfloat16

D_MODEL = 1024
RET_HEADS = 4
QK_DIM = 128
V_DIM = 256
CHUNK = 128
SGU_GROUPS = 4
GROUP_DIM = 256
PLE_DIM = 256
N_CHIPS = 4
SHARD_W = 2048
ROPE_BASE = 10000.0
NORM_EPS = 1e-6
GN_EPS = 1e-5
QK_SCALE = QK_DIM ** -0.5
SQRT_HALF = math.sqrt(0.5)
INV_SQRT_2PI = 1.0 / math.sqrt(2.0 * math.pi)

ADAM_LR = 0.001
ADAM_B1 = 0.9
ADAM_B2 = 0.999
ADAM_EPS = 1e-08
ADAM_WD = 0.01
ADAM_STEP = 10

TILE_M = 256
LOSS_TILE = (8, 128)
LOAD_PARTS = 4
PAIR_CHUNK_ROWS = 256
PROJ_TILE = 512
RET_TILE = 512
DX_TILE = 512
VMEM_LIMIT = 56 * 1024 * 1024
MESH = pl.DeviceIdType.MESH
ANY = pl.BlockSpec(memory_space=pl.ANY)

CHUNK_DECAY = tuple(
    float(np.exp(np.float32(CHUNK) * np.log(np.float32(1.0 - 2.0 ** (-5.0 - h))))) for h in range(RET_HEADS))


def _mm(a, b):
    return jnp.dot(a.astype(MXU_DTYPE), b.astype(MXU_DTYPE), preferred_element_type=F32)


def _mm_nt(a, b):
    return lax.dot_general(a.astype(MXU_DTYPE), b.astype(MXU_DTYPE), (((1,), (1,)), ((), ())),
                           preferred_element_type=F32)


def _mm_tn(a, b):
    return lax.dot_general(a.astype(MXU_DTYPE), b.astype(MXU_DTYPE), (((0,), (0,)), ((), ())),
                           preferred_element_type=F32)


def _mean(x):
    return jnp.mean(x, axis=-1, keepdims=True)


def _sigmoid(x):
    return jax.nn.sigmoid(x)


def _silu_and_grad(x):
    s = _sigmoid(x)
    silu = x * s
    return silu, s + silu * (1.0 - s)


def _gelu_and_grad(x):
    cdf = 0.5 + 0.5 * lax.erf(x * SQRT_HALF)
    return x * cdf, cdf + x * (jnp.exp(x * x * -0.5) * INV_SQRT_2PI)


def _unit_norm(x, eps):
    xc = x - _mean(x)
    rstd = lax.rsqrt(_mean(xc * xc) + eps)
    return xc * rstd, rstd


def _unit_norm_bwd(dn, n, rstd):
    return rstd * (dn - _mean(dn) - n * _mean(dn * n))


def _rms(x):
    r = lax.rsqrt(_mean(x * x) + NORM_EPS)
    return x * r, r


def _rms_bwd(dxn, xn, r):
    return r * (dxn - xn * _mean(dxn * xn))


def _rot(x):
    return pltpu.roll(x, QK_DIM // 2, 1)


def _params(semantics, **kw):
    return pltpu.CompilerParams(dimension_semantics=semantics, vmem_limit_bytes=VMEM_LIMIT, **kw)


def _row_tile(width, tm=TILE_M):
    return pl.BlockSpec((tm, width), lambda i: (i, 0))


def _proj_tile(j):
    return pl.BlockSpec((None, TILE_M, SHARD_W), lambda i: (j, i, 0))


def _resident(shape):
    nd = len(shape)
    return pl.BlockSpec(shape, lambda *_: (0,) * nd, pipeline_mode=pl.Buffered(1))


def _causal_ws(ws_ref):
    row = lax.broadcasted_iota(jnp.int32, (CHUNK, CHUNK), 0)
    col = lax.broadcasted_iota(jnp.int32, (CHUNK, CHUNK), 1)
    return [jnp.where(row >= col, ws_ref[g], 0.0).astype(MXU_DTYPE) for g in range(SGU_GROUPS)]


def _heads(x, width):
    return [x[:, h * width:(h + 1) * width] for h in range(x.shape[1] // width)]


def _branch_forward(pj1, pj2, ret_raw, wsc, bst, mixed_ref):
    rg, su = pj1[:, :D_MODEL], pj1[:, D_MODEL:]
    sv, sg = pj2[:, :D_MODEL], pj2[:, D_MODEL:]
    rn_parts, rstd_parts = zip(*[_unit_norm(r, GN_EPS) for r in _heads(ret_raw, V_DIM)])
    rn = jnp.concatenate(rn_parts, axis=1)
    sil_rg, dsil_rg = _silu_and_grad(rg)
    ret = rn * sil_rg
    u, du = _gelu_and_grad(su)
    gelu_sv, dgelu_sv = _gelu_and_grad(sv)
    vn, rstd_v = _unit_norm(gelu_sv, GN_EPS)
    for cc in range(pj1.shape[0] // CHUNK):
        for g in range(SGU_GROUPS):
            rs, cs = slice(cc * CHUNK, (cc + 1) * CHUNK), slice(g * GROUP_DIM, (g + 1) * GROUP_DIM)
            mixed_ref[rs, cs] = _mm(wsc[g], vn[rs, cs]) + bst[:, g:g + 1]
    mixed = mixed_ref[...]
    sil_sg, dsil_sg = _silu_and_grad(sg)
    sgu = u * mixed * sil_sg
    return dict(rn=rn, rstd_r=rstd_parts, sil_rg=sil_rg, dsil_rg=dsil_rg, ret=ret, u=u, du=du, dgelu_sv=dgelu_sv,
                vn=vn, rstd_v=rstd_v, mixed=mixed, sil_sg=sil_sg, dsil_sg=dsil_sg, sgu=sgu)


class _Side:
    def __init__(self, operands, out_shapes, n_sems, start, finish, aliases=None):
        self.operands, self.out_shapes, self.n_sems = list(operands), list(out_shapes), n_sems
        self.start, self.finish, self.aliases = start, finish, dict(aliases or {})


def _join_sides(*sides):
    spans, a, b, s = [], 0, 0, 0
    for side in sides:
        spans.append((a, b, s))
        a, b, s = a + len(side.operands), b + len(side.out_shapes), s + side.n_sems

    def run(which):
        def go(ins, outs, send_sem, recv_sem, base=0):
            for side, (a0, b0, s0) in zip(sides, spans):
                getattr(side, which)(ins[a0:a0 + len(side.operands)], outs[b0:b0 + len(side.out_shapes)],
                                     send_sem, recv_sem, base + s0)
        return go

    aliases = {a0 + i: b0 + o for side, (a0, b0, _) in zip(sides, spans) for i, o in side.aliases.items()}
    return _Side([x for side in sides for x in side.operands], [x for side in sides for x in side.out_shapes], s,
                 run("start"), run("finish"), aliases)


def _compute_call(body, *, name, grid, in_specs, out_specs, out_shape, operands, semantics, scratch_shapes=(),
                  side=None, prefetch=None, aliases=None, side_start=None):
    n_pre = 0 if prefetch is None else 1
    pre = () if prefetch is None else (prefetch,)

    def spec(in_specs, out_specs, scratch):
        return pltpu.PrefetchScalarGridSpec(num_scalar_prefetch=n_pre, grid=grid, in_specs=in_specs,
                                            out_specs=out_specs, scratch_shapes=scratch)

    if side is None:
        return pl.pallas_call(body, name=name, grid_spec=spec(in_specs, out_specs, list(scratch_shapes)),
                              out_shape=out_shape,
                              input_output_aliases={n_pre + a: b for a, b in (aliases or {}).items()},
                              compiler_params=_params(semantics))(*pre, *operands)
    n_in, n_out, s_in, s_out = len(operands), len(out_shape), len(side.operands), len(side.out_shapes)

    def carrier(*refs):
        pre_refs, refs = refs[:n_pre], refs[n_pre:]
        ins, refs = refs[:n_in], refs[n_in:]
        side_ins, refs = refs[:s_in], refs[s_in:]
        outs, refs = refs[:n_out], refs[n_out:]
        side_outs, refs = refs[:s_out], refs[s_out:]
        scratch, (send_sem, recv_sem) = refs[:-2], refs[-2:]
        ids = [pl.program_id(a) for a in range(len(grid))]
        at = (0,) * len(grid) if side_start is None else side_start
        first = functools.reduce(jnp.logical_and, [i == a for i, a in zip(ids, at)])
        last = functools.reduce(jnp.logical_and, [i == g - 1 for i, g in zip(ids, grid)])

        @pl.when(first)
        def _():
            side.start(side_ins, side_outs, send_sem, recv_sem)

        body(*pre_refs, *ins, *outs, *scratch)

        @pl.when(last)
        def _():
            side.finish(side_ins, side_outs, send_sem, recv_sem)

    all_aliases = {n_pre + a: b for a, b in (aliases or {}).items()}
    all_aliases.update({n_pre + n_in + a: n_out + b for a, b in side.aliases.items()})
    res = pl.pallas_call(
        carrier, name=name,
        grid_spec=spec(list(in_specs) + [ANY] * s_in, list(out_specs) + [ANY] * s_out,
                       list(scratch_shapes) + [pltpu.SemaphoreType.DMA((side.n_sems,))] * 2),
        out_shape=list(out_shape) + side.out_shapes, input_output_aliases=all_aliases,
        compiler_params=_params(semantics, has_side_effects=True),
    )(*pre, *operands, *side.operands)
    return res[:n_out], res[n_out:]


def _proj_call(x2d, g_mixer, placed_in, pos, side):
    t = x2d.shape[0]
    nt = t // PROJ_TILE
    hr = placed_in.shape[1] // 2

    def body(pos_ref, x_ref, g_ref, win_ref, h_ref, pj_ref, w_ref, w_vmem, h_all, loc_sem, send_sem, recv_sem):
        k, i = pl.program_id(0), pl.program_id(1)
        x, y, c = _position()
        me = 2 * x + y

        def copy(slot, chip, core, dev):
            piece = w_ref.at[chip, pl.ds(core * hr, hr)]
            return pltpu.make_async_remote_copy(src_ref=piece, dst_ref=piece, send_sem=send_sem.at[slot],
                                                recv_sem=recv_sem.at[slot], device_id=dev, device_id_type=MESH)

        def first_hop(kk):
            px, py = _other_chip(x, y, kk)
            return copy(kk - 1, me, c, (px, py, c))

        def relay():
            source = jnp.bitwise_xor(me, 2 - c)
            return copy(2, source, c, (jnp.bitwise_xor(x, c), jnp.bitwise_xor(y, 1 - c), c))

        def onward(kk, core):
            px, py = _other_chip(x, y, kk)
            return copy(2 + kk, 2 * px + py, core, (x, y, 1 - c))

        def loads(kk):
            px, py = _other_chip(x, y, kk)
            rows = w_vmem.shape[1] // LOAD_PARTS
            return [pltpu.make_async_copy(w_ref.at[2 * px + py, pl.ds(r * rows, rows)],
                                          w_vmem.at[kk % 2, pl.ds(r * rows, rows)], loc_sem.at[r])
                    for r in range(LOAD_PARTS)]

        @pl.when(jnp.logical_and(k == 0, i == 0))
        def _():
            for kk in (1, 2):
                first_hop(kk).start()
            for cp in loads(0):
                cp.start()
            for cp in loads(0):
                cp.wait()

        for kk in (1, 2, 3):
            @pl.when(jnp.logical_and(k == kk - 1, i == nt - 1))
            def _(kk=kk):
                if kk == 1:
                    first_hop(1).wait_recv()
                    first_hop(2).wait_recv()
                    relay().start()
                if kk == 3:
                    relay().wait_recv()
                onward(kk, c).start()
                onward(kk, 1 - c).wait_recv()
                for cp in loads(kk):
                    cp.start()

            @pl.when(jnp.logical_and(k == kk, i == 0))
            def _(kk=kk):
                for cp in loads(kk):
                    cp.wait()

        rows = pl.ds(pl.multiple_of(i * PROJ_TILE, PROJ_TILE), PROJ_TILE)

        @pl.when(k == 0)
        def _():
            xn, _ = _rms(x_ref[...])
            h = (xn * g_ref[...]).astype(MXU_DTYPE)
            h_all[rows, :] = h
            h_ref[...] = h

        pj_ref[...] = jnp.dot(h_all[rows, :], w_vmem[k % 2], preferred_element_type=F32)

        @pl.when(jnp.logical_and(k == N_CHIPS - 1, i == nt - 1))
        def _():
            for cp in [first_hop(1), first_hop(2), relay()] + [onward(kk, c) for kk in (1, 2, 3)]:
                cp.wait_send()

    parked = lambda k, i, pos: (jnp.where(k == 0, i, nt - 1), 0)
    return _compute_call(
        body, name="proj_fwd", grid=(N_CHIPS, nt),
        in_specs=[pl.BlockSpec((PROJ_TILE, D_MODEL), parked), pl.BlockSpec((1, D_MODEL), lambda k, i, pos: (0, 0)), ANY],
        out_specs=[pl.BlockSpec((PROJ_TILE, D_MODEL), lambda k, i, pos: (jnp.where(k == 0, i, nt), 0)),
                   pl.BlockSpec((None, PROJ_TILE, SHARD_W), lambda k, i, pos: (jnp.bitwise_xor(pos[0], k), i, 0)),
                   ANY],
        out_shape=[jax.ShapeDtypeStruct((t + PROJ_TILE, D_MODEL), MXU_DTYPE),
                   jax.ShapeDtypeStruct((N_CHIPS, t, SHARD_W), F32),
                   jax.ShapeDtypeStruct(placed_in.shape, placed_in.dtype)],
        scratch_shapes=[pltpu.VMEM((2,) + placed_in.shape[1:], placed_in.dtype), pltpu.VMEM((t, D_MODEL), MXU_DTYPE),
                        pltpu.SemaphoreType.DMA((LOAD_PARTS,)), pltpu.SemaphoreType.DMA((6,)),
                        pltpu.SemaphoreType.DMA((6,))],
        operands=(x2d, g_mixer, placed_in), semantics=("arbitrary", "arbitrary"), side=side, prefetch=pos,
        aliases={2: 2}, side_start=(2, 0))


def _retention_consts(seq):
    half = QK_DIM // 2
    inv = ROPE_BASE ** (-jnp.arange(half, dtype=F32) / half)
    inv = jnp.concatenate([inv, inv])
    coarse = (jnp.arange(seq // half, dtype=F32) * half)[:, None] * inv[None, :]
    fine = jnp.arange(half, dtype=F32)[:, None] * inv[None, :]
    ca, sa, cb, sb = jnp.cos(coarse)[:, None], jnp.sin(coarse)[:, None], jnp.cos(fine)[None], jnp.sin(fine)[None]
    cos_full = (ca * cb - sa * sb).reshape(seq, QK_DIM)
    sign = jnp.concatenate([-jnp.ones((half,), F32), jnp.ones((half,), F32)])
    sin_signed = (sa * cb + ca * sb).reshape(seq, QK_DIM) * sign
    log_g = jnp.log(1.0 - 2.0 ** (-5.0 - jnp.arange(RET_HEADS, dtype=F32)))
    idx = jnp.arange(CHUNK, dtype=F32)
    diff = idx[:, None] - idx[None, :]
    decay_in = jnp.where(diff[None] >= 0, jnp.exp(jnp.maximum(diff, 0.0)[None] * log_g[:, None, None]), 0.0)
    zeta = jnp.exp((CHUNK - 1.0 - idx)[None, :] * log_g[:, None])
    xi = jnp.exp((idx + 1.0)[None, :] * log_g[:, None])
    zeta = jnp.broadcast_to(zeta[:, :, None], (RET_HEADS, CHUNK, QK_DIM))
    xi = jnp.broadcast_to(xi[:, :, None], (RET_HEADS, CHUNK, QK_DIM))
    return cos_full, sin_signed, decay_in, zeta, xi


def _qkv(pj_ref, rows, h):
    q = pj_ref[rows, h * QK_DIM:(h + 1) * QK_DIM]
    k = pj_ref[rows, RET_HEADS * QK_DIM + h * QK_DIM:RET_HEADS * QK_DIM + (h + 1) * QK_DIM]
    v = pj_ref[rows, 2 * RET_HEADS * QK_DIM + h * V_DIM:2 * RET_HEADS * QK_DIM + (h + 1) * V_DIM]
    return q, k, v


def _retention_fwd_call(pj0, consts, n_seq, seq):
    cos_full, sin_signed, decay_in, zeta, xi = consts
    nb = seq // RET_TILE
    cpb = RET_TILE // CHUNK

    def body(pj_ref, cos_ref, sin_ref, d_ref, zeta_ref, xi_ref, o_ref, st_ref, state):
        @pl.when(pl.program_id(1) == 0)
        def _():
            state[...] = jnp.zeros_like(state)

        for cc in range(cpb):
            rows = slice(cc * CHUNK, (cc + 1) * CHUNK)
            cs, sn = cos_ref[rows, :], sin_ref[rows, :]
            for h in range(RET_HEADS):
                q, k, v = _qkv(pj_ref, rows, h)
                qt = (q * cs + _rot(q) * sn) * QK_SCALE
                kt = k * cs + _rot(k) * sn
                prev = state[h]
                st_ref[cc, h] = prev.astype(st_ref.dtype)
                scores = _mm_nt(qt, kt) * d_ref[h]
                o_ref[rows, h * V_DIM:(h + 1) * V_DIM] = _mm(scores, v) + _mm(qt * xi_ref[h], prev)
                state[h] = _mm_tn(kt * zeta_ref[h], v) + CHUNK_DECAY[h] * prev

    row = lambda b, n: (b * nb + n, 0)
    pos = lambda b, n: (n, 0)
    return pl.pallas_call(
        body, name="retention_fwd", grid=(n_seq, nb),
        in_specs=[pl.BlockSpec((None, RET_TILE, SHARD_W), lambda b, n: (0,) + row(b, n)),
                  pl.BlockSpec((RET_TILE, QK_DIM), pos),
                  pl.BlockSpec((RET_TILE, QK_DIM), pos), _resident(decay_in.shape), _resident(zeta.shape),
                  _resident(xi.shape)],
        out_specs=[pl.BlockSpec((RET_TILE, D_MODEL), row),
                   pl.BlockSpec((cpb, RET_HEADS, QK_DIM, V_DIM), lambda b, n: (b * nb + n, 0, 0, 0))],
        out_shape=[jax.ShapeDtypeStruct((n_seq * seq, D_MODEL), F32),
                   jax.ShapeDtypeStruct((n_seq * seq // CHUNK, RET_HEADS, QK_DIM, V_DIM), MXU_DTYPE)],
        scratch_shapes=[pltpu.VMEM((RET_HEADS, QK_DIM, V_DIM), F32)],
        compiler_params=_params(("arbitrary", "arbitrary")),
    )(pj0, cos_full, sin_signed, decay_in, zeta, xi)


def _merge_bwd_call(wide, pj1, pj2, pj3, ret_raw, ws, bst, w_ro, w_so, w_o):
    t = wide.shape[1]

    def body(wide_ref, pj1_ref, pj2_ref, pj3_ref, rr_ref, ws_ref, bst_ref, wro_ref, wso_ref, wo_ref,
             dpj1_ref, dpj2_ref, dpj3_ref, drr_ref, da_ref, db_ref, dws_ref, dbst_ref, mixed_ref):
        @pl.when(pl.program_id(0) == 0)
        def _():
            dws_ref[...] = jnp.zeros_like(dws_ref)
            dbst_ref[...] = jnp.zeros_like(dbst_ref)

        wsc = _causal_ws(ws_ref)
        f = _branch_forward(pj1_ref[...], pj2_ref[...], rr_ref[...], wsc, bst_ref[...], mixed_ref)
        pj3 = pj3_ref[...]
        smr, sms = _sigmoid(pj3[:, :D_MODEL]), _sigmoid(pj3[:, D_MODEL:])
        dmerged = _mm_nt(wide_ref[0], wo_ref[...])
        da_f, db_f = dmerged * smr, dmerged * sms
        da, db = da_f.astype(MXU_DTYPE), db_f.astype(MXU_DTYPE)
        dpj3_ref[:, :D_MODEL] = (da_f * wide_ref[1] * (1.0 - smr)).astype(dpj3_ref.dtype)
        dpj3_ref[:, D_MODEL:] = (db_f * wide_ref[2] * (1.0 - sms)).astype(dpj3_ref.dtype)
        da_ref[...] = da
        db_ref[...] = db

        dret = _mm_nt(da, wro_ref[...])
        dpj1_ref[:, :D_MODEL] = (dret * f["rn"] * f["dsil_rg"]).astype(dpj1_ref.dtype)
        drn = dret * f["sil_rg"]
        for h in range(RET_HEADS):
            cols = slice(h * V_DIM, (h + 1) * V_DIM)
            drr_ref[:, cols] = _unit_norm_bwd(drn[:, cols], f["rn"][:, cols], f["rstd_r"][h]).astype(drr_ref.dtype)

        dsgu = _mm_nt(db, wso_ref[...])
        dpj2_ref[:, D_MODEL:] = (dsgu * f["u"] * f["mixed"] * f["dsil_sg"]).astype(dpj2_ref.dtype)
        tg = dsgu * f["sil_sg"]
        dpj1_ref[:, D_MODEL:] = (tg * f["mixed"] * f["du"]).astype(dpj1_ref.dtype)
        dmixed = tg * f["u"]
        for cc in range(TILE_M // CHUNK):
            for g in range(SGU_GROUPS):
                rs, cs = slice(cc * CHUNK, (cc + 1) * CHUNK), slice(g * GROUP_DIM, (g + 1) * GROUP_DIM)
                dm = dmixed[rs, cs]
                mixed_ref[rs, cs] = _mm_tn(wsc[g], dm)
                dws_ref[g] += _mm_nt(dm, f["vn"][rs, cs])
                dbst_ref[:, g:g + 1] += jnp.sum(dm, axis=1, keepdims=True)
        dvv = _unit_norm_bwd(mixed_ref[...], f["vn"], f["rstd_v"])
        dpj2_ref[:, :D_MODEL] = (dvv * f["dgelu_sv"]).astype(dpj2_ref.dtype)

    sq = (D_MODEL, D_MODEL)
    return pl.pallas_call(
        body, name="merge_bwd", grid=(t // TILE_M,),
        in_specs=[pl.BlockSpec((3, TILE_M, D_MODEL), lambda i: (0, i, 0))] + [_proj_tile(j) for j in (1, 2, 3)]
        + [_row_tile(D_MODEL)] + [_resident(ws.shape), _resident(bst.shape), _resident(sq), _resident(sq), _resident(sq)],
        out_specs=[_row_tile(SHARD_W)] * 3 + [_row_tile(D_MODEL)] * 3
        + [pl.BlockSpec(ws.shape, lambda i: (0, 0, 0)), pl.BlockSpec(bst.shape, lambda i: (0, 0))],
        out_shape=[jax.ShapeDtypeStruct((t, SHARD_W), MXU_DTYPE)] * 3 + [jax.ShapeDtypeStruct((t, D_MODEL), MXU_DTYPE)] * 3
        + [jax.ShapeDtypeStruct(ws.shape, F32), jax.ShapeDtypeStruct(bst.shape, F32)],
        scratch_shapes=[pltpu.VMEM((TILE_M, D_MODEL), F32)], compiler_params=_params(("arbitrary",)),
    )(wide, pj1, pj2, pj3, ret_raw, ws, bst, w_ro, w_so, w_o)


def _tail_call(pj, ret_raw, x2d, p2d, target, ws, bst, g_ple, g_final, w_ro, w_so, w_o, w_pg, w_pp):
    t = x2d.shape[0]
    tm = TILE_M

    def body(pj1_ref, pj2_ref, pj3_ref, rr_ref, x_ref, p_ref, t_ref, ws_ref, bst_ref, gp_ref, gf_ref,
             wro_ref, wso_ref, wo_ref, wpg_ref, wpp_ref,
             wide_ref, half_ref, loss_ref, dgp_ref, dgf_ref, mixed_ref):
        dx1_ref, a_ref, b_ref = (wide_ref.at[k] for k in range(3))
        ret_ref, sgu_ref, mg_ref, hp_ref, dz_ref, dpp_ref = (half_ref.at[k] for k in range(6))
        @pl.when(pl.program_id(0) == 0)
        def _():
            for acc in (loss_ref, dgp_ref, dgf_ref):
                acc[...] = jnp.zeros_like(acc)

        f = _branch_forward(pj1_ref[...], pj2_ref[...], rr_ref[...], _causal_ws(ws_ref), bst_ref[...], mixed_ref)
        ret = f["ret"].astype(MXU_DTYPE)
        sgu = f["sgu"].astype(MXU_DTYPE)
        ret_ref[...] = ret
        sgu_ref[...] = sgu
        a = jnp.dot(ret, wro_ref[...], preferred_element_type=F32)
        b = jnp.dot(sgu, wso_ref[...], preferred_element_type=F32)
        a_ref[...] = a
        b_ref[...] = b
        pj3 = pj3_ref[...]
        smr, sms = _sigmoid(pj3[:, :D_MODEL]), _sigmoid(pj3[:, D_MODEL:])
        merged = (smr * a + sms * b).astype(MXU_DTYPE)
        mg_ref[...] = merged
        x1v = x_ref[...] + jnp.dot(merged, wo_ref[...], preferred_element_type=F32)
        xn1, r1 = _rms(x1v)
        hp = (xn1 * gp_ref[...]).astype(MXU_DTYPE)
        hp_ref[...] = hp
        gate = _sigmoid(jnp.dot(hp, wpg_ref[...], preferred_element_type=F32))
        pp = jnp.dot(p_ref[...].astype(MXU_DTYPE), wpp_ref[...], preferred_element_type=F32)
        xn2, r2 = _rms(x1v + gate * pp)
        err = xn2 * gf_ref[...] - t_ref[...]
        loss_ref[...] += (0.5 / D_MODEL) * jnp.sum(jnp.sum(err * err, axis=1, keepdims=True), axis=0, keepdims=True)

        dy = err * (1.0 / D_MODEL)
        dgf_ref[...] += jnp.sum(dy * xn2, axis=0, keepdims=True)
        dx2 = _rms_bwd(dy * gf_ref[...], xn2, r2)
        dpp_ref[...] = (dx2 * gate).astype(MXU_DTYPE)
        dz = (dx2 * pp * gate * (1.0 - gate)).astype(MXU_DTYPE)
        dz_ref[...] = dz
        dhp = _mm_nt(dz, wpg_ref[...])
        dgp_ref[...] += jnp.sum(dhp * xn1, axis=0, keepdims=True)
        dx1_ref[...] = dx2 + _rms_bwd(dhp * gp_ref[...], xn1, r1)

    sq = (D_MODEL, D_MODEL)
    vec = _resident((1, D_MODEL))
    whole = lambda shape: pl.BlockSpec(shape, lambda i: (0,) * len(shape))
    return pl.pallas_call(
        body, name="tail_fwd", grid=(t // tm,),
        in_specs=[_proj_tile(j) for j in (1, 2, 3)]
        + [_row_tile(D_MODEL), _row_tile(D_MODEL), _row_tile(PLE_DIM), _row_tile(D_MODEL), _resident(ws.shape),
           _resident(bst.shape), vec, vec, _resident(sq), _resident(sq), _resident(sq), _resident(sq),
           _resident((PLE_DIM, D_MODEL))],
        out_specs=[pl.BlockSpec((3, tm, D_MODEL), lambda i: (0, i, 0)), pl.BlockSpec((6, tm, D_MODEL), lambda i: (0, i, 0)),
                   whole(LOSS_TILE), whole((1, D_MODEL)), whole((1, D_MODEL))],
        out_shape=[jax.ShapeDtypeStruct((3, t, D_MODEL), F32), jax.ShapeDtypeStruct((6, t, D_MODEL), MXU_DTYPE),
                   jax.ShapeDtypeStruct(LOSS_TILE, F32), jax.ShapeDtypeStruct((1, D_MODEL), F32),
           jax.ShapeDtypeStruct((1, D_MODEL), F32)],
        scratch_shapes=[pltpu.VMEM((TILE_M, D_MODEL), F32)], compiler_params=_params(("arbitrary",)),
    )(pj, pj, pj, ret_raw, x2d, p2d, target, ws, bst, g_ple, g_final, w_ro, w_so, w_o, w_pg, w_pp)


def _retention_bwd_call(pj0, drr, states, consts, n_seq, seq, side):
    cos_full, sin_signed, decay_in, zeta, xi = consts
    nb = seq // RET_TILE
    cpb = RET_TILE // CHUNK

    def body(pj_ref, do_ref, st_ref, cos_ref, sin_ref, d_ref, zeta_ref, xi_ref, dpj_ref, gstate):
        @pl.when(pl.program_id(1) == 0)
        def _():
            gstate[...] = jnp.zeros_like(gstate)

        for cc in reversed(range(cpb)):
            rows = slice(cc * CHUNK, (cc + 1) * CHUNK)
            cs, sn = cos_ref[rows, :], sin_ref[rows, :]
            for h in range(RET_HEADS):
                q, k, v = _qkv(pj_ref, rows, h)
                qt = (q * cs + _rot(q) * sn) * QK_SCALE
                kt = k * cs + _rot(k) * sn
                d_out = do_ref[rows, h * V_DIM:(h + 1) * V_DIM]
                prev = st_ref[cc, h]
                g = gstate[h]
                dec = d_ref[h]
                scores_d = _mm_nt(qt, kt) * dec
                dscores = _mm_nt(d_out, v) * dec
                kz = kt * zeta_ref[h]
                qx = qt * xi_ref[h]
                dv = _mm_tn(scores_d, d_out) + _mm(kz, g)
                dqt = (_mm(dscores, kt) + _mm_nt(d_out, prev) * xi_ref[h]) * QK_SCALE
                dkt = _mm_tn(dscores, qt) + _mm_nt(v, g) * zeta_ref[h]
                gstate[h] = _mm_tn(qx, d_out) + CHUNK_DECAY[h] * g
                dq = dqt * cs + _rot(dqt * sn)
                dk = dkt * cs + _rot(dkt * sn)
                dpj_ref[rows, h * QK_DIM:(h + 1) * QK_DIM] = dq.astype(dpj_ref.dtype)
                dpj_ref[rows, RET_HEADS * QK_DIM + h * QK_DIM:RET_HEADS * QK_DIM + (h + 1) * QK_DIM] = dk.astype(
                    dpj_ref.dtype)
                dpj_ref[rows, 2 * RET_HEADS * QK_DIM + h * V_DIM:2 * RET_HEADS * QK_DIM + (h + 1) * V_DIM] = dv.astype(
                    dpj_ref.dtype)

    row = lambda b, n: (b * nb + nb - 1 - n, 0)
    pos = lambda b, n: (nb - 1 - n, 0)
    return _compute_call(
        body, name="retention_bwd", grid=(n_seq, nb),
        in_specs=[pl.BlockSpec((None, RET_TILE, SHARD_W), lambda b, n: (0,) + row(b, n)),
                  pl.BlockSpec((RET_TILE, D_MODEL), row),
                  pl.BlockSpec((cpb, RET_HEADS, QK_DIM, V_DIM), lambda b, n: (b * nb + nb - 1 - n, 0, 0, 0)),
                  pl.BlockSpec((RET_TILE, QK_DIM), pos), pl.BlockSpec((RET_TILE, QK_DIM), pos),
                  _resident(decay_in.shape), _resident(zeta.shape), _resident(xi.shape)],
        out_specs=[pl.BlockSpec((RET_TILE, SHARD_W), row)],
        out_shape=[jax.ShapeDtypeStruct((n_seq * seq, SHARD_W), MXU_DTYPE)],
        scratch_shapes=[pltpu.VMEM((RET_HEADS, QK_DIM, V_DIM), F32)],
        operands=(pj0, drr, states, cos_full, sin_signed, decay_in, zeta, xi),
        semantics=("arbitrary", "arbitrary"), side=side)


def _dx_call(dpj, x2d, wide, g_mixer, wg_in, side):
    t = x2d.shape[0]

    def body(d0, d1, d2, d3, x_ref, dx1_ref, g_ref, w_ref, dx_ref, dg_ref):
        @pl.when(pl.program_id(0) == 0)
        def _():
            dg_ref[...] = jnp.zeros_like(dg_ref)

        dh = _mm_nt(d0[...], w_ref[0])
        for j, d_ref in enumerate((d1, d2, d3)):
            dh += _mm_nt(d_ref[...], w_ref[j + 1])
        xn, r = _rms(x_ref[...])
        dg_ref[...] += jnp.sum(dh * xn, axis=0, keepdims=True)
        dx_ref[...] = dx1_ref[...] + _rms_bwd(dh * g_ref[...], xn, r)

    return _compute_call(
        body, name="dx_bwd", grid=(t // DX_TILE,),
        in_specs=[_row_tile(SHARD_W, DX_TILE)] * N_CHIPS
        + [_row_tile(D_MODEL, DX_TILE), pl.BlockSpec((None, DX_TILE, D_MODEL), lambda i: (0, i, 0))]
        + [_resident((1, D_MODEL)), _resident(wg_in.shape)],
        out_specs=[_row_tile(D_MODEL, DX_TILE), pl.BlockSpec((1, D_MODEL), lambda i: (0, 0))],
        out_shape=[jax.ShapeDtypeStruct((t, D_MODEL), F32), jax.ShapeDtypeStruct((1, D_MODEL), F32)],
        operands=(*dpj, x2d, wide, g_mixer, wg_in), semantics=("arbitrary",), side=side)


def _wgrad_call(name, lhs, rhs, block_n, out_cols=None, block_t=1024, into=None, slot=0, n_slots=1, side=None):
    (lhs, lhs_at), (rhs, rhs_at) = [x if isinstance(x, tuple) else (x, None) for x in (lhs, rhs)]
    t, n = rhs.shape[-2:]
    k = lhs.shape[-1]
    block_t = min(block_t, t)

    def operand_spec(at, block, index):
        if at is None:
            return pl.BlockSpec(block, index)
        return pl.BlockSpec((None,) + block, lambda j, i: (at,) + index(j, i))

    steps = t // block_t
    out_cols = block_n if out_cols is None else out_cols
    per = block_n // out_cols
    first_block = slot * (n // block_n)

    def body(l_ref, r_ref, *rest):
        o_ref, acc = rest[-2:]
        @pl.when(pl.program_id(1) == 0)
        def _():
            acc[...] = jnp.zeros_like(acc)

        acc[...] += _mm_tn(l_ref[...], r_ref[...])

        @pl.when(pl.program_id(1) == steps - 1)
        def _():
            for s in range(per):
                o_ref[s] = acc[:, s * out_cols:(s + 1) * out_cols].astype(o_ref.dtype)

    res = _compute_call(
        body, name=name, grid=(n // block_n, steps),
        in_specs=[operand_spec(lhs_at, (block_t, k), lambda j, i: (i, 0)),
                  operand_spec(rhs_at, (block_t, block_n), lambda j, i: (i, j))] + ([] if into is None else [ANY]),
        out_specs=[pl.BlockSpec((per, k, out_cols), lambda j, i: (first_block + j, 0, 0))],
        out_shape=[jax.ShapeDtypeStruct((n_slots * (n // out_cols), k, out_cols), COMM_DTYPE)],
        scratch_shapes=[pltpu.VMEM((k, block_n), F32)], aliases={} if into is None else {2: 0},
        operands=(lhs, rhs) if into is None else (lhs, rhs, into), semantics=("arbitrary", "arbitrary"), side=side)
    return res[0] if side is None else (res[0][0], res[1])


def _position():
    return lax.axis_index("x"), lax.axis_index("y"), lax.axis_index("c")


def _position_array():
    x, y, c = _position()
    return jnp.stack([2 * x + y, c]).astype(jnp.int32)


def _other_chip(x, y, k):
    return (1 - x if k & 2 else x), (1 - y if k & 1 else y)


def _plan_side(operands, out_shapes, plan, n_remote, aliases=None):
    def copies(ins, outs, send_sem, recv_sem, base=0):
        remote = plan(ins, outs)
        assert len(remote) == n_remote
        return [pltpu.make_async_remote_copy(src_ref=src, dst_ref=dst, send_sem=send_sem.at[base + i],
                                             recv_sem=recv_sem.at[base + i], device_id=dev, device_id_type=MESH)
                for i, (src, dst, dev) in enumerate(remote)]

    def start(*a):
        for cp in copies(*a):
            cp.start()

    def finish(*a):
        for cp in copies(*a):
            cp.wait()

    return _Side(operands, out_shapes, n_remote, start, finish, aliases)


def _place_cast_call(name, shards, pos):
    n = len(shards)
    rows, width = shards[0].shape
    block_rows = min(256, rows)

    def body(pos_ref, *refs):
        for w_ref, o_ref in zip(refs[:n], refs[n:]):
            o_ref[...] = w_ref[...].astype(o_ref.dtype)

    return pl.pallas_call(
        body, name=name,
        grid_spec=pltpu.PrefetchScalarGridSpec(
            num_scalar_prefetch=1, grid=(rows // block_rows,),
            in_specs=[pl.BlockSpec((block_rows, width), lambda i, pos: (i, 0))] * n,
            out_specs=[pl.BlockSpec((None, block_rows, width), lambda i, pos: (pos[0], i, 0))] * n),
        out_shape=[jax.ShapeDtypeStruct((N_CHIPS, rows, width), MXU_DTYPE)] * n,
        compiler_params=_params(("arbitrary",)),
    )(pos, *shards)


def _gather_side(placed):
    n = len(placed)

    def copies(kind, bufs, send_sem, recv_sem, base):
        x, y, c = _position()
        me = 2 * x + y
        made = []
        for i in range(n):
            hr = placed[i].shape[1] // 2
            for k in (1, 2, 3):
                px, py = _other_chip(x, y, k)
                chip, core, slot, dev = [(me, c, k - 1, (px, py, c)), (2 * px + py, c, 2 + k, (x, y, 1 - c)),
                                         (2 * px + py, 1 - c, 2 + k, (x, y, 1 - c))][kind]
                piece = bufs[i].at[chip, pl.ds(core * hr, hr)]
                made.append(pltpu.make_async_remote_copy(
                    src_ref=piece, dst_ref=piece, send_sem=send_sem.at[base + 6 * i + slot],
                    recv_sem=recv_sem.at[base + 6 * i + slot], device_id=dev, device_id_type=MESH))
        return made

    def start(ins, outs, send_sem, recv_sem, base=0):
        for cp in copies(0, outs, send_sem, recv_sem, base):
            cp.start()

    def finish(ins, outs, send_sem, recv_sem, base=0):
        first, onward = copies(0, outs, send_sem, recv_sem, base), copies(1, outs, send_sem, recv_sem, base)
        for landed, cp in zip(first, onward):
            landed.wait_recv()
            cp.start()
        for cp in copies(2, outs, send_sem, recv_sem, base):
            cp.wait_recv()
        for cp in first + onward:
            cp.wait_send()

    return _Side(placed, [jax.ShapeDtypeStruct(a.shape, a.dtype) for a in placed], 6 * n, start, finish,
                 {i: i for i in range(n)})


def _same_shape(shapes):
    found = {}
    for i, shape in enumerate(shapes):
        found.setdefault(shape, []).append(i)
    return found


def _pair_stage(tag, grads):
    n = len(grads)
    shapes = [(g.shape[0] // N_CHIPS, g.shape[1]) for g in grads]
    kinds = list(_same_shape(shapes))
    whole = {(r, w): N_CHIPS * (r // 2) <= 2 * PAIR_CHUNK_ROWS for r, w in kinds}
    chunk = {(r, w): N_CHIPS * (r // 2) if whole[(r, w)] else min(r // 2, PAIR_CHUNK_ROWS) for r, w in kinds}
    work = [(i, j, q) for i, (r, w) in enumerate(shapes) for j in ((None,) if whole[(r, w)] else range(N_CHIPS))
            for q in range(1 if whole[(r, w)] else (r // 2) // chunk[(r, w)])]

    def body(*refs):
        ins, sums, landed = refs[:n], refs[n:2 * n], refs[2 * n:3 * n]
        bufs, (load_sem, store_sem, send_sem, recv_sem) = refs[3 * n:-4], refs[-4:]
        x, y, c = _position()

        def piece(i, j):
            r = shapes[i][0]
            return pltpu.make_async_remote_copy(
                src_ref=ins[i].at[pl.ds(j * r + (1 - c) * (r // 2), r // 2)],
                dst_ref=landed[i].at[pl.ds(j * (r // 2), r // 2)], send_sem=send_sem.at[N_CHIPS * i + j],
                recv_sem=recv_sem.at[N_CHIPS * i + j], device_id=(x, y, 1 - c), device_id_type=MESH)

        def buffers(i, slot):
            own, got, out = bufs[3 * kinds.index(shapes[i]):3 * kinds.index(shapes[i]) + 3]
            return own.at[slot], got.at[slot], out.at[slot]

        def loads(step):
            i, j, q = work[step]
            r, ch = shapes[i][0], chunk[shapes[i]]
            own, got, _ = buffers(i, step % 2)
            if j is None:
                return [pltpu.make_async_copy(ins[i].at[pl.ds(jj * r + c * (r // 2), r // 2)],
                                              own.at[pl.ds(jj * (r // 2), r // 2)], load_sem.at[step % 2, jj])
                        for jj in range(N_CHIPS)] + [pltpu.make_async_copy(landed[i], got, load_sem.at[step % 2, N_CHIPS])]
            return [pltpu.make_async_copy(ins[i].at[pl.ds(j * r + c * (r // 2) + q * ch, ch)], own,
                                          load_sem.at[step % 2, 0]),
                    pltpu.make_async_copy(landed[i].at[pl.ds(j * (r // 2) + q * ch, ch)], got,
                                          load_sem.at[step % 2, 1])]

        def store(step):
            i, j, q = work[step]
            ch = chunk[shapes[i]]
            rows = sums[i] if j is None else sums[i].at[pl.ds(j * (shapes[i][0] // 2) + q * ch, ch)]
            return pltpu.make_async_copy(buffers(i, step % 2)[2], rows, store_sem.at[step % 2])

        def begin(step):
            i, j, q = work[step]
            for jj in (range(N_CHIPS) if j is None else [j] if q == 0 else []):
                piece(i, jj).wait_recv()
            for cp in loads(step):
                cp.start()

        for i in range(n):
            for j in range(N_CHIPS):
                piece(i, j).start()
        begin(0)
        for step in range(len(work)):
            if step + 1 < len(work):
                begin(step + 1)
            for cp in loads(step):
                cp.wait()
            if step >= 2:
                store(step - 2).wait()
            own, got, out = buffers(work[step][0], step % 2)
            out[...] = (own[...].astype(F32) + got[...].astype(F32)).astype(out.dtype)
            store(step).start()
        for step in range(max(len(work) - 2, 0), len(work)):
            store(step).wait()
        for i in range(n):
            for j in range(N_CHIPS):
                piece(i, j).wait_send()

    half = [jax.ShapeDtypeStruct((N_CHIPS * (r // 2), w), COMM_DTYPE) for r, w in shapes]
    res = pl.pallas_call(
        body, name=f"grad_pair_sum_{tag}", in_specs=[ANY] * n, out_specs=[ANY] * (2 * n), out_shape=half + half,
        scratch_shapes=[pltpu.VMEM((2, chunk[k], k[1]), COMM_DTYPE) for k in kinds for _ in range(3)]
        + [pltpu.SemaphoreType.DMA((2, N_CHIPS + 1)), pltpu.SemaphoreType.DMA((2,)),
           pltpu.SemaphoreType.DMA((N_CHIPS * n,)),
           pltpu.SemaphoreType.DMA((N_CHIPS * n,))],
        compiler_params=pltpu.CompilerParams(has_side_effects=True, vmem_limit_bytes=VMEM_LIMIT),
    )(*grads)
    return list(res[:n])


def _chip_side(pair_sums):
    halves = [(p.shape[0] // N_CHIPS, p.shape[1]) for p in pair_sums]

    def plan(ins, outs):
        x, y, c = _position()
        remote = []
        for i, (hr, _) in enumerate(halves):
            for k in (1, 2, 3):
                px, py = _other_chip(x, y, k)
                remote.append((ins[i].at[pl.ds((2 * px + py) * hr, hr)], outs[i].at[pl.ds((k - 1) * hr, hr)],
                               (px, py, c)))
        return remote

    return _plan_side(pair_sums, [jax.ShapeDtypeStruct((3 * hr, w), COMM_DTYPE) for hr, w in halves], plan,
                      3 * len(pair_sums))


def _finish_stage(tag, pair_sums, chip, extra):
    n = len(pair_sums)
    halves = [(p.shape[0] // N_CHIPS, p.shape[1]) for p in pair_sums]
    kinds = list(_same_shape(halves))
    chunk = {(hr, w): min(hr, PAIR_CHUNK_ROWS) for hr, w in kinds}
    work = [(i, q) for i, (hr, w) in enumerate(halves) for q in range(hr // chunk[(hr, w)])]
    e_in, e_out = len(extra.operands), len(extra.out_shapes)

    def body(*refs):
        sums, got, refs = refs[:n], refs[n:2 * n], refs[2 * n:]
        extra_ins, refs = refs[:e_in], refs[e_in:]
        full, refs = refs[:n], refs[n:]
        extra_outs, refs = refs[:e_out], refs[e_out:]
        bufs, (load_sem, store_sem, send_sem, recv_sem, extra_send, extra_recv) = refs[:-6], refs[-6:]
        x, y, c = _position()
        me = 2 * x + y
        extra.start(extra_ins, extra_outs, extra_send, extra_recv)

        def buffers(i, slot):
            at = 5 * kinds.index(halves[i])
            return [b_.at[slot] for b_ in bufs[at:at + 5]]

        def loads(step):
            i, q = work[step]
            hr, ch = halves[i][0], chunk[halves[i]]
            into = buffers(i, step % 2)
            made = [pltpu.make_async_copy(sums[i].at[pl.ds(me * hr + q * ch, ch)], into[0], load_sem.at[step % 2, 0])]
            made += [pltpu.make_async_copy(got[i].at[pl.ds(k * hr + q * ch, ch)], into[1 + k], load_sem.at[step % 2, 1 + k])
                     for k in range(3)]
            return made

        def stores(step):
            i, q = work[step]
            hr, ch = halves[i][0], chunk[halves[i]]
            result = buffers(i, step % 2)[4]
            rows = full[i].at[pl.ds(c * hr + q * ch, ch)]
            return (pltpu.make_async_copy(result, rows, store_sem.at[step % 2]),
                    pltpu.make_async_remote_copy(src_ref=result, dst_ref=rows, send_sem=send_sem.at[step],
                                                 recv_sem=recv_sem.at[step], device_id=(x, y, 1 - c),
                                                 device_id_type=MESH))

        for cp in loads(0):
            cp.start()
        for step in range(len(work)):
            if step + 1 < len(work):
                for cp in loads(step + 1):
                    cp.start()
            for cp in loads(step):
                cp.wait()
            if step >= 2:
                stores(step - 2)[0].wait()
                stores(step - 2)[1].wait_send()
            mine, k1, k2, k3, result = buffers(work[step][0], step % 2)
            result[...] = ((mine[...].astype(F32) + k1[...].astype(F32)) + k2[...].astype(F32)) + k3[...].astype(F32)
            for cp in stores(step):
                cp.start()
        for step in range(max(len(work) - 2, 0), len(work)):
            stores(step)[0].wait()
            stores(step)[1].wait_send()
        for step in range(len(work)):
            stores(step)[1].wait_recv()
        extra.finish(extra_ins, extra_outs, extra_send, extra_recv)

    scratch = []
    for k in kinds:
        scratch += [pltpu.VMEM((2, chunk[k], k[1]), COMM_DTYPE)] * 4 + [pltpu.VMEM((2, chunk[k], k[1]), F32)]
    res = pl.pallas_call(
        body, name=f"grad_finish_{tag}", in_specs=[ANY] * (2 * n + e_in), out_specs=[ANY] * (n + e_out),
        out_shape=[jax.ShapeDtypeStruct((2 * hr, w), F32) for hr, w in halves] + extra.out_shapes,
        scratch_shapes=scratch + [pltpu.SemaphoreType.DMA((2, 4)), pltpu.SemaphoreType.DMA((2,)),
                                  pltpu.SemaphoreType.DMA((len(work),)), pltpu.SemaphoreType.DMA((len(work),)),
                                  pltpu.SemaphoreType.DMA((extra.n_sems,)), pltpu.SemaphoreType.DMA((extra.n_sems,))],
        compiler_params=pltpu.CompilerParams(has_side_effects=True, vmem_limit_bytes=VMEM_LIMIT),
    )(*pair_sums, *chip, *extra.operands)
    return list(res[:n]), list(res[n:])


def _small_gather_side(parts):
    n = len(parts)

    def copies(ins, outs, send_sem, recv_sem, base):
        x, y, c = _position()
        made = []
        for i in range(n):
            mine = outs[i].at[4 * x + 2 * y + c]
            made.append(pltpu.make_async_copy(ins[i], mine, send_sem.at[base + 8 * i + 7]))
            for d in range(1, 8):
                px, py = _other_chip(x, y, d >> 1)
                made.append(pltpu.make_async_remote_copy(
                    src_ref=ins[i], dst_ref=mine, send_sem=send_sem.at[base + 8 * i + d - 1],
                    recv_sem=recv_sem.at[base + 8 * i + d - 1], device_id=(px, py, 1 - c if d & 1 else c),
                    device_id_type=MESH))
        return made

    def start(ins, outs, send_sem, recv_sem, base=0):
        for cp in copies(ins, outs, send_sem, recv_sem, base):
            cp.start()

    def finish(ins, outs, send_sem, recv_sem, base=0):
        for cp in copies(ins, outs, send_sem, recv_sem, base):
            cp.wait()

    return _Side(parts, [jax.ShapeDtypeStruct((8,) + a_.shape, F32) for a_ in parts], 8 * n, start, finish)


def _adamw(w, g, m, v):
    m = ADAM_B1 * m + (1.0 - ADAM_B1) * g
    v = ADAM_B2 * v + (1.0 - ADAM_B2) * (g * g)
    m_hat = m / (1.0 - ADAM_B1 ** ADAM_STEP)
    v_hat = v / (1.0 - ADAM_B2 ** ADAM_STEP)
    delta = -ADAM_LR * (m_hat / (jnp.sqrt(v_hat) + ADAM_EPS) + ADAM_WD * w)
    return delta, m, v


def _adamw_call(name, ws, gs, ms, vs):
    n = len(ws)
    rows, width = ws[0].shape
    block_rows = min(256 // n, rows)

    def body(*refs):
        for i in range(n):
            w_ref, g_ref, m_ref, v_ref = (refs[j * n + i] for j in range(4))
            d_out, m_out, v_out = (refs[(4 + j) * n + i] for j in range(3))
            d_out[...], m_out[...], v_out[...] = _adamw(w_ref[...], g_ref[...], m_ref[...], v_ref[...])

    spec = pl.BlockSpec((block_rows, width), lambda i: (i, 0))
    res = pl.pallas_call(
        body, name=name, grid=(rows // block_rows,), in_specs=[spec] * (4 * n), out_specs=[spec] * (3 * n),
        out_shape=[jax.ShapeDtypeStruct(ws[0].shape, F32)] * (3 * n),
        compiler_params=_params(("arbitrary",)),
    )(*ws, *gs, *ms, *vs)
    return [(res[i], res[n + i], res[2 * n + i]) for i in range(n)]


def _small_adamw_call(gathered, weights, moments_m, moments_v):
    n = len(weights)

    def body(*refs):
        all_refs, refs = refs[:n + 1], refs[n + 1:]
        w_refs, m_refs, v_refs, outs = refs[:n], refs[n:2 * n], refs[2 * n:3 * n], refs[3 * n:]

        def total(ref):
            acc = ref[0]
            for d in range(1, 8):
                acc = acc + ref[d]
            return acc

        outs[0][...] = total(all_refs[n])
        for i in range(n):
            g = total(all_refs[i])
            if i == 0:
                row = lax.broadcasted_iota(jnp.int32, g.shape, 0)
                col = lax.broadcasted_iota(jnp.int32, g.shape, 1)
                g = jnp.where((row % CHUNK) >= col, g, 0.0)
            g_out, d_out, m_out, v_out = outs[1 + 4 * i:5 + 4 * i]
            g_out[...] = g
            d_out[...], m_out[...], v_out[...] = _adamw(w_refs[i][...], g, m_refs[i][...], v_refs[i][...])

    out_shape = [jax.ShapeDtypeStruct(LOSS_TILE, F32)]
    for w in weights:
        out_shape += [jax.ShapeDtypeStruct(w.shape, F32)] * 4
    res = pl.pallas_call(
        body, name="small_adamw", out_shape=out_shape,
        compiler_params=pltpu.CompilerParams(vmem_limit_bytes=VMEM_LIMIT),
    )(*gathered, *weights, *moments_m, *moments_v)
    return res[0], [res[1 + 4 * i:5 + 4 * i] for i in range(n)]


def kernel(x, p, w_in, w_ret_out, w_sgu_out, w_out, sgu_ws, sgu_bs, w_ple_gate, w_ple_proj, g_mixer, g_ple, g_final, loss_target, m_w_in, m_w_ret_out, m_w_sgu_out, m_w_out, m_sgu_ws, m_sgu_bs, m_w_ple_gate, m_w_ple_proj, m_g_mixer, m_g_ple, m_g_final, v_w_in, v_w_ret_out, v_w_sgu_out, v_w_out, v_sgu_ws, v_sgu_bs, v_w_ple_gate, v_w_ple_proj, v_g_mixer, v_g_ple, v_g_final):
    n_seq, seq, _ = x.shape
    t = n_seq * seq
    x2d = x.reshape(t, D_MODEL)
    p2d = p.reshape(t, PLE_DIM)
    target = loss_target.reshape(t, D_MODEL)
    big = [w_in[0], w_ret_out[0], w_sgu_out[0], w_out[0], w_ple_gate[0], w_ple_proj[0]]
    big_m = [m_w_in[0], m_w_ret_out[0], m_w_sgu_out[0], m_w_out[0], m_w_ple_gate[0], m_w_ple_proj[0]]
    big_v = [v_w_in[0], v_w_ret_out[0], v_w_sgu_out[0], v_w_out[0], v_w_ple_gate[0], v_w_ple_proj[0]]

    pos = _position_array()
    placed = (_place_cast_call("place_w_in", big[:1], pos) + _place_cast_call("place_square_weights", big[1:5], pos)
              + _place_cast_call("place_w_ple_proj", big[5:], pos))
    ws = sgu_ws[0]
    bst = sgu_bs[0].T
    consts = _retention_consts(seq)

    (h, pj, wg_in), gathered = _proj_call(x2d, g_mixer, placed[0], pos, _gather_side(placed[1:]))
    pj0 = pj1 = pj2 = pj3 = pj
    w_ro, w_so, w_o, w_pg = (w.reshape(D_MODEL, D_MODEL) for w in gathered[:4])
    w_pp = gathered[4].transpose(1, 0, 2).reshape(PLE_DIM, D_MODEL)
    ret_raw, states = _retention_fwd_call(pj0, consts, n_seq, seq)
    wide, half, loss, dg_ple, dg_final = _tail_call(
        pj, ret_raw, x2d, p2d, target, ws, bst, g_ple, g_final.reshape(1, D_MODEL), w_ro, w_so, w_o, w_pg, w_pp)
    ret, sgu, merged, hp, dz, dpp = ((half, k) for k in range(6))
    dpj1, dpj2, dpj3, drr, da, db, dws, dbst = _merge_bwd_call(wide, pj1, pj2, pj3, ret_raw, ws, bst, w_ro, w_so, w_o)

    small_shapes = [(SGU_GROUPS * CHUNK, CHUNK), (SGU_GROUPS, CHUNK), (1, D_MODEL), (1, D_MODEL), (1, D_MODEL)]
    as_small = lambda arrays: [None if a_ is None else a_.reshape(s_) for a_, s_ in zip(arrays, small_shapes)]
    early = as_small([dws, dbst.T, None, dg_ple, dg_final])
    rows_of = lambda g: g.reshape(g.shape[0] * g.shape[1], g.shape[2])
    tail_grads = [
        rows_of(_wgrad_call("wgrad_ret_out", ret, da, D_MODEL)),
        rows_of(_wgrad_call("wgrad_sgu_out", sgu, db, D_MODEL)),
        rows_of(_wgrad_call("wgrad_out", merged, (wide, 0), D_MODEL)),
        rows_of(_wgrad_call("wgrad_ple_gate", hp, dz, D_MODEL)),
        rows_of(_wgrad_call("wgrad_ple_proj", p2d, dpp, D_MODEL, out_cols=PLE_DIM)),
    ]
    tail_sums = _pair_stage("tail", tail_grads)
    (dpj0,), carried = _retention_bwd_call(
        pj0, drr, states, consts, n_seq, seq,
        _join_sides(_chip_side(tail_sums), _small_gather_side([early[0], early[1], early[3], early[4], loss])))
    tail_chip, early_all = carried[:len(tail_sums)], carried[len(tail_sums):]
    in_grad = None
    for j, d in enumerate((dpj0, dpj1, dpj2, dpj3)):
        in_grad = _wgrad_call(f"wgrad_in_{j}", h, d, SHARD_W, into=in_grad, slot=j, n_slots=N_CHIPS)
    in_sums = _pair_stage("in", [rows_of(in_grad)])
    (dx, dg_mixer), in_chip = _dx_call((dpj0, dpj1, dpj2, dpj3), x2d, wide, g_mixer, wg_in, _chip_side(in_sums))
    g_big, (mixer_all,) = _finish_stage("all", in_sums + tail_sums, list(in_chip) + list(tail_chip),
                                        _small_gather_side([dg_mixer]))
    upd = [None] * len(big)
    for _, which in _same_shape([w.shape for w in big]).items():
        pick = lambda arrays: [arrays[i] for i in which]
        for i, triple in zip(which, _adamw_call(f"adamw_{which[0]}", pick(big), pick(g_big), pick(big_m), pick(big_v))):
            upd[i] = triple

    small_g = [early_all[0], early_all[1], mixer_all, early_all[2], early_all[3], early_all[4]]
    total, small = _small_adamw_call(small_g, as_small([sgu_ws, sgu_bs, g_mixer, g_ple, g_final]),
                                     as_small([m_sgu_ws, m_sgu_bs, m_g_mixer, m_g_ple, m_g_final]),
                                     as_small([v_sgu_ws, v_sgu_bs, v_g_mixer, v_g_ple, v_g_final]))
    out_small_shapes = [sgu_ws.shape, sgu_bs.shape, g_mixer.shape, g_ple.shape, g_final.shape]

    def ordered(big_list, kind):
        w_in_, w_ro_, w_so_, w_o_, w_pg_, w_pp_ = [b_[None] for b_ in big_list]
        s_ws, s_bs, s_gm, s_gp, s_gf = [small[i][kind].reshape(s) for i, s in enumerate(out_small_shapes)]
        return [w_in_, w_ro_, w_so_, w_o_, s_ws, s_bs, w_pg_, w_pp_, s_gm, s_gp, s_gf]

    out = [total[0, 0], dx.reshape(x.shape)]
    out += ordered(g_big, 0)
    out += ordered([u[0] for u in upd], 1)
    out += ordered([u[1] for u in upd], 2)
    out += ordered([u[2] for u in upd], 3)
    return tuple(out)
```

```python
import functools
import math

import numpy as np
import jax
import jax.numpy as jnp
from jax import lax
from jax.experimental import pallas as pl
from jax.experimental.pallas import tpu as pltpu

F32 = jnp.float32
MXU_DTYPE = jnp.bfloat16
COMM_DTYPE = jnp.bfloat16

D_MODEL = 1024
RET_HEADS = 4
QK_DIM = 128
V_DIM = 256
CHUNK = 128
SGU_GROUPS = 4
GROUP_DIM = 256
PLE_DIM = 256
N_CHIPS = 4
SHARD_W = 2048
ROPE_BASE = 10000.0
NORM_EPS = 1e-6
GN_EPS = 1e-5
QK_SCALE = QK_DIM ** -0.5
SQRT_HALF = math.sqrt(0.5)
INV_SQRT_2PI = 1.0 / math.sqrt(2.0 * math.pi)

ADAM_LR = 0.001
ADAM_B1 = 0.9
ADAM_B2 = 0.999
ADAM_EPS = 1e-08
ADAM_WD = 0.01
ADAM_STEP = 10

TILE_M = 256
LOSS_TILE = (8, 128)
MERGE_RING = 3
LOAD_PARTS = 4
PAIR_CHUNK_ROWS = 256
PROJ_TILE = 512
RET_TILE = 512
DX_TILE = 512
VMEM_LIMIT = 56 * 1024 * 1024
MESH = pl.DeviceIdType.MESH
ANY = pl.BlockSpec(memory_space=pl.ANY)

CHUNK_DECAY = tuple(
    float(np.exp(np.float32(CHUNK) * np.log(np.float32(1.0 - 2.0 ** (-5.0 - h))))) for h in range(RET_HEADS))


def _mm(a, b):
    return jnp.dot(a.astype(MXU_DTYPE), b.astype(MXU_DTYPE), preferred_element_type=F32)


def _mm_nt(a, b):
    return lax.dot_general(a.astype(MXU_DTYPE), b.astype(MXU_DTYPE), (((1,), (1,)), ((), ())),
                           preferred_element_type=F32)


def _mm_tn(a, b):
    return lax.dot_general(a.astype(MXU_DTYPE), b.astype(MXU_DTYPE), (((0,), (0,)), ((), ())),
                           preferred_element_type=F32)


def _mean(x):
    return jnp.mean(x, axis=-1, keepdims=True)


def _sigmoid(x):
    return jax.nn.sigmoid(x)


def _silu_and_grad(x):
    s = _sigmoid(x)
    silu = x * s
    return silu, s + silu * (1.0 - s)


def _gelu_and_grad(x):
    cdf = 0.5 + 0.5 * lax.erf(x * SQRT_HALF)
    return x * cdf, cdf + x * (jnp.exp(x * x * -0.5) * INV_SQRT_2PI)


def _unit_norm(x, eps):
    xc = x - _mean(x)
    rstd = lax.rsqrt(_mean(xc * xc) + eps)
    return xc * rstd, rstd


def _unit_norm_bwd(dn, n, rstd):
    return rstd * (dn - _mean(dn) - n * _mean(dn * n))


def _rms(x):
    r = lax.rsqrt(_mean(x * x) + NORM_EPS)
    return x * r, r


def _rms_bwd(dxn, xn, r):
    return r * (dxn - xn * _mean(dxn * xn))


def _rot(x):
    return pltpu.roll(x, QK_DIM // 2, 1)


def _params(semantics, **kw):
    return pltpu.CompilerParams(dimension_semantics=semantics, vmem_limit_bytes=VMEM_LIMIT, **kw)


def _row_tile(width, tm=TILE_M):
    return pl.BlockSpec((tm, width), lambda i: (i, 0))


def _proj_tile(j):
    return pl.BlockSpec((None, TILE_M, SHARD_W), lambda i: (j, i, 0))


def _resident(shape):
    nd = len(shape)
    return pl.BlockSpec(shape, lambda *_: (0,) * nd, pipeline_mode=pl.Buffered(1))


def _causal_ws(ws_ref):
    row = lax.broadcasted_iota(jnp.int32, (CHUNK, CHUNK), 0)
    col = lax.broadcasted_iota(jnp.int32, (CHUNK, CHUNK), 1)
    return [jnp.where(row >= col, ws_ref[g], 0.0).astype(MXU_DTYPE) for g in range(SGU_GROUPS)]


def _heads(x, width):
    return [x[:, h * width:(h + 1) * width] for h in range(x.shape[1] // width)]


def _branch_forward(pj1, pj2, ret_raw, wsc, bst, mixed_ref):
    rg, su = pj1[:, :D_MODEL], pj1[:, D_MODEL:]
    sv, sg = pj2[:, :D_MODEL], pj2[:, D_MODEL:]
    rn_parts, rstd_parts = zip(*[_unit_norm(r, GN_EPS) for r in _heads(ret_raw, V_DIM)])
    rn = jnp.concatenate(rn_parts, axis=1)
    sil_rg, dsil_rg = _silu_and_grad(rg)
    ret = rn * sil_rg
    u, du = _gelu_and_grad(su)
    gelu_sv, dgelu_sv = _gelu_and_grad(sv)
    vn, rstd_v = _unit_norm(gelu_sv, GN_EPS)
    for cc in range(pj1.shape[0] // CHUNK):
        for g in range(SGU_GROUPS):
            rs, cs = slice(cc * CHUNK, (cc + 1) * CHUNK), slice(g * GROUP_DIM, (g + 1) * GROUP_DIM)
            mixed_ref[rs, cs] = _mm(wsc[g], vn[rs, cs]) + bst[:, g:g + 1]
    mixed = mixed_ref[...]
    sil_sg, dsil_sg = _silu_and_grad(sg)
    sgu = u * mixed * sil_sg
    return dict(rn=rn, rstd_r=rstd_parts, sil_rg=sil_rg, dsil_rg=dsil_rg, ret=ret, u=u, du=du, dgelu_sv=dgelu_sv,
                vn=vn, rstd_v=rstd_v, mixed=mixed, sil_sg=sil_sg, dsil_sg=dsil_sg, sgu=sgu)


class _Side:
    def __init__(self, operands, out_shapes, n_sems, start, finish, aliases=None):
        self.operands, self.out_shapes, self.n_sems = list(operands), list(out_shapes), n_sems
        self.start, self.finish, self.aliases = start, finish, dict(aliases or {})


def _join_sides(*sides):
    spans, a, b, s = [], 0, 0, 0
    for side in sides:
        spans.append((a, b, s))
        a, b, s = a + len(side.operands), b + len(side.out_shapes), s + side.n_sems

    def run(which):
        def go(ins, outs, send_sem, recv_sem, base=0):
            for side, (a0, b0, s0) in zip(sides, spans):
                getattr(side, which)(ins[a0:a0 + len(side.operands)], outs[b0:b0 + len(side.out_shapes)],
                                     send_sem, recv_sem, base + s0)
        return go

    aliases = {a0 + i: b0 + o for side, (a0, b0, _) in zip(sides, spans) for i, o in side.aliases.items()}
    return _Side([x for side in sides for x in side.operands], [x for side in sides for x in side.out_shapes], s,
                 run("start"), run("finish"), aliases)


def _compute_call(body, *, name, grid, in_specs, out_specs, out_shape, operands, semantics, scratch_shapes=(),
                  side=None, prefetch=None, aliases=None, side_start=None):
    n_pre = 0 if prefetch is None else 1
    pre = () if prefetch is None else (prefetch,)

    def spec(in_specs, out_specs, scratch):
        return pltpu.PrefetchScalarGridSpec(num_scalar_prefetch=n_pre, grid=grid, in_specs=in_specs,
                                            out_specs=out_specs, scratch_shapes=scratch)

    if side is None:
        return pl.pallas_call(body, name=name, grid_spec=spec(in_specs, out_specs, list(scratch_shapes)),
                              out_shape=out_shape,
                              input_output_aliases={n_pre + a: b for a, b in (aliases or {}).items()},
                              compiler_params=_params(semantics))(*pre, *operands)
    n_in, n_out, s_in, s_out = len(operands), len(out_shape), len(side.operands), len(side.out_shapes)

    def carrier(*refs):
        pre_refs, refs = refs[:n_pre], refs[n_pre:]
        ins, refs = refs[:n_in], refs[n_in:]
        side_ins, refs = refs[:s_in], refs[s_in:]
        outs, refs = refs[:n_out], refs[n_out:]
        side_outs, refs = refs[:s_out], refs[s_out:]
        scratch, (send_sem, recv_sem) = refs[:-2], refs[-2:]
        ids = [pl.program_id(a) for a in range(len(grid))]
        at = (0,) * len(grid) if side_start is None else side_start
        first = functools.reduce(jnp.logical_and, [i == a for i, a in zip(ids, at)])
        last = functools.reduce(jnp.logical_and, [i == g - 1 for i, g in zip(ids, grid)])

        @pl.when(first)
        def _():
            side.start(side_ins, side_outs, send_sem, recv_sem)

        body(*pre_refs, *ins, *outs, *scratch)

        @pl.when(last)
        def _():
            side.finish(side_ins, side_outs, send_sem, recv_sem)

    all_aliases = {n_pre + a: b for a, b in (aliases or {}).items()}
    all_aliases.update({n_pre + n_in + a: n_out + b for a, b in side.aliases.items()})
    res = pl.pallas_call(
        carrier, name=name,
        grid_spec=spec(list(in_specs) + [ANY] * s_in, list(out_specs) + [ANY] * s_out,
                       list(scratch_shapes) + [pltpu.SemaphoreType.DMA((side.n_sems,))] * 2),
        out_shape=list(out_shape) + side.out_shapes, input_output_aliases=all_aliases,
        compiler_params=_params(semantics, has_side_effects=True),
    )(*pre, *operands, *side.operands)
    return res[:n_out], res[n_out:]


def _proj_call(x2d, g_mixer, placed_in, pos, side):
    t = x2d.shape[0]
    nt = t // PROJ_TILE
    hr = placed_in.shape[1] // 2

    def body(pos_ref, x_ref, g_ref, win_ref, h_ref, pj_ref, w_ref, w_vmem, h_all, loc_sem, send_sem, recv_sem):
        k, i = pl.program_id(0), pl.program_id(1)
        x, y, c = _position()
        me = 2 * x + y

        def copy(slot, chip, core, dev):
            piece = w_ref.at[chip, pl.ds(core * hr, hr)]
            return pltpu.make_async_remote_copy(src_ref=piece, dst_ref=piece, send_sem=send_sem.at[slot],
                                                recv_sem=recv_sem.at[slot], device_id=dev, device_id_type=MESH)

        def first_hop(kk):
            px, py = _other_chip(x, y, kk)
            return copy(kk - 1, me, c, (px, py, c))

        def relay():
            source = jnp.bitwise_xor(me, 2 - c)
            return copy(2, source, c, (jnp.bitwise_xor(x, c), jnp.bitwise_xor(y, 1 - c), c))

        def onward(kk, core):
            px, py = _other_chip(x, y, kk)
            return copy(2 + kk, 2 * px + py, core, (x, y, 1 - c))

        def loads(kk):
            px, py = _other_chip(x, y, kk)
            rows = w_vmem.shape[1] // LOAD_PARTS
            return [pltpu.make_async_copy(w_ref.at[2 * px + py, pl.ds(r * rows, rows)],
                                          w_vmem.at[kk % 2, pl.ds(r * rows, rows)], loc_sem.at[r])
                    for r in range(LOAD_PARTS)]

        @pl.when(jnp.logical_and(k == 0, i == 0))
        def _():
            for kk in (1, 2):
                first_hop(kk).start()
            for cp in loads(0):
                cp.start()
            for cp in loads(0):
                cp.wait()

        for kk in (1, 2, 3):
            @pl.when(jnp.logical_and(k == kk - 1, i == nt - 1))
            def _(kk=kk):
                if kk == 1:
                    first_hop(1).wait_recv()
                    first_hop(2).wait_recv()
                    relay().start()
                if kk == 3:
                    relay().wait_recv()
                onward(kk, c).start()
                onward(kk, 1 - c).wait_recv()
                for cp in loads(kk):
                    cp.start()

            @pl.when(jnp.logical_and(k == kk, i == 0))
            def _(kk=kk):
                for cp in loads(kk):
                    cp.wait()

        rows = pl.ds(pl.multiple_of(i * PROJ_TILE, PROJ_TILE), PROJ_TILE)

        @pl.when(k == 0)
        def _():
            xn, _ = _rms(x_ref[...])
            h = (xn * g_ref[...]).astype(MXU_DTYPE)
            h_all[rows, :] = h
            h_ref[...] = h

        pj_ref[...] = jnp.dot(h_all[rows, :], w_vmem[k % 2], preferred_element_type=F32)

        @pl.when(jnp.logical_and(k == N_CHIPS - 1, i == nt - 1))
        def _():
            for cp in [first_hop(1), first_hop(2), relay()] + [onward(kk, c) for kk in (1, 2, 3)]:
                cp.wait_send()

    parked = lambda k, i, pos: (jnp.where(k == 0, i, nt - 1), 0)
    return _compute_call(
        body, name="proj_fwd", grid=(N_CHIPS, nt),
        in_specs=[pl.BlockSpec((PROJ_TILE, D_MODEL), parked), pl.BlockSpec((1, D_MODEL), lambda k, i, pos: (0, 0)), ANY],
        out_specs=[pl.BlockSpec((PROJ_TILE, D_MODEL), lambda k, i, pos: (jnp.where(k == 0, i, nt), 0)),
                   pl.BlockSpec((None, PROJ_TILE, SHARD_W), lambda k, i, pos: (jnp.bitwise_xor(pos[0], k), i, 0)),
                   ANY],
        out_shape=[jax.ShapeDtypeStruct((t + PROJ_TILE, D_MODEL), MXU_DTYPE),
                   jax.ShapeDtypeStruct((N_CHIPS, t, SHARD_W), F32),
                   jax.ShapeDtypeStruct(placed_in.shape, placed_in.dtype)],
        scratch_shapes=[pltpu.VMEM((2,) + placed_in.shape[1:], placed_in.dtype), pltpu.VMEM((t, D_MODEL), MXU_DTYPE),
                        pltpu.SemaphoreType.DMA((LOAD_PARTS,)), pltpu.SemaphoreType.DMA((6,)),
                        pltpu.SemaphoreType.DMA((6,))],
        operands=(x2d, g_mixer, placed_in), semantics=("arbitrary", "arbitrary"), side=side, prefetch=pos,
        aliases={2: 2}, side_start=(2, 0))


def _retention_consts(seq):
    half = QK_DIM // 2
    inv = ROPE_BASE ** (-jnp.arange(half, dtype=F32) / half)
    inv = jnp.concatenate([inv, inv])
    coarse = (jnp.arange(seq // half, dtype=F32) * half)[:, None] * inv[None, :]
    fine = jnp.arange(half, dtype=F32)[:, None] * inv[None, :]
    ca, sa, cb, sb = jnp.cos(coarse)[:, None], jnp.sin(coarse)[:, None], jnp.cos(fine)[None], jnp.sin(fine)[None]
    cos_full = (ca * cb - sa * sb).reshape(seq, QK_DIM)
    sign = jnp.concatenate([-jnp.ones((half,), F32), jnp.ones((half,), F32)])
    sin_signed = (sa * cb + ca * sb).reshape(seq, QK_DIM) * sign
    log_g = jnp.log(1.0 - 2.0 ** (-5.0 - jnp.arange(RET_HEADS, dtype=F32)))
    idx = jnp.arange(CHUNK, dtype=F32)
    diff = idx[:, None] - idx[None, :]
    decay_in = jnp.where(diff[None] >= 0, jnp.exp(jnp.maximum(diff, 0.0)[None] * log_g[:, None, None]), 0.0)
    zeta = jnp.exp((CHUNK - 1.0 - idx)[None, :] * log_g[:, None])
    xi = jnp.exp((idx + 1.0)[None, :] * log_g[:, None])
    zeta = jnp.broadcast_to(zeta[:, :, None], (RET_HEADS, CHUNK, QK_DIM))
    xi = jnp.broadcast_to(xi[:, :, None], (RET_HEADS, CHUNK, QK_DIM))
    return cos_full, sin_signed, decay_in, zeta, xi


def _qkv(pj_ref, rows, h):
    q = pj_ref[rows, h * QK_DIM:(h + 1) * QK_DIM]
    k = pj_ref[rows, RET_HEADS * QK_DIM + h * QK_DIM:RET_HEADS * QK_DIM + (h + 1) * QK_DIM]
    v = pj_ref[rows, 2 * RET_HEADS * QK_DIM + h * V_DIM:2 * RET_HEADS * QK_DIM + (h + 1) * V_DIM]
    return q, k, v


def _retention_fwd_call(pj0, consts, n_seq, seq):
    cos_full, sin_signed, decay_in, zeta, xi = consts
    nb = seq // RET_TILE
    cpb = RET_TILE // CHUNK

    def body(pj_ref, cos_ref, sin_ref, d_ref, zeta_ref, xi_ref, o_ref, st_ref, state):
        @pl.when(pl.program_id(1) == 0)
        def _():
            state[...] = jnp.zeros_like(state)

        for cc in range(cpb):
            rows = slice(cc * CHUNK, (cc + 1) * CHUNK)
            cs, sn = cos_ref[rows, :], sin_ref[rows, :]
            for h in range(RET_HEADS):
                q, k, v = _qkv(pj_ref, rows, h)
                qt = (q * cs + _rot(q) * sn) * QK_SCALE
                kt = k * cs + _rot(k) * sn
                prev = state[h]
                st_ref[cc, h] = prev.astype(st_ref.dtype)
                scores = _mm_nt(qt, kt) * d_ref[h]
                o_ref[rows, h * V_DIM:(h + 1) * V_DIM] = _mm(scores, v) + _mm(qt * xi_ref[h], prev)
                state[h] = _mm_tn(kt * zeta_ref[h], v) + CHUNK_DECAY[h] * prev

    row = lambda b, n: (b * nb + n, 0)
    pos = lambda b, n: (n, 0)
    return pl.pallas_call(
        body, name="retention_fwd", grid=(n_seq, nb),
        in_specs=[pl.BlockSpec((None, RET_TILE, SHARD_W), lambda b, n: (0,) + row(b, n)),
                  pl.BlockSpec((RET_TILE, QK_DIM), pos),
                  pl.BlockSpec((RET_TILE, QK_DIM), pos), _resident(decay_in.shape), _resident(zeta.shape),
                  _resident(xi.shape)],
        out_specs=[pl.BlockSpec((RET_TILE, D_MODEL), row),
                   pl.BlockSpec((cpb, RET_HEADS, QK_DIM, V_DIM), lambda b, n: (b * nb + n, 0, 0, 0))],
        out_shape=[jax.ShapeDtypeStruct((n_seq * seq, D_MODEL), F32),
                   jax.ShapeDtypeStruct((n_seq * seq // CHUNK, RET_HEADS, QK_DIM, V_DIM), MXU_DTYPE)],
        scratch_shapes=[pltpu.VMEM((RET_HEADS, QK_DIM, V_DIM), F32)],
        compiler_params=_params(("arbitrary", "arbitrary")),
    )(pj0, cos_full, sin_signed, decay_in, zeta, xi)


def _merge_bwd_call(wide, pj, ret_raw, ws, bst, w_ro, w_so, w_o):
    t = wide.shape[1]
    n_steps = t // TILE_M

    def body(wide_ref, pj_hbm, rr_ref, ws_ref, bst_ref, wro_ref, wso_ref, wo_ref,
             dpj1_ref, dpj2_ref, dpj3_ref, drr_ref, da_ref, db_ref, dws_ref, dbst_ref, mixed_ref, ring, ring_sem):
        i = pl.program_id(0)

        def fetch(step, slot):
            return pltpu.make_async_copy(pj_hbm.at[pl.ds(1, 3), pl.ds(step * TILE_M, TILE_M), :], ring.at[slot],
                                         ring_sem.at[slot])

        @pl.when(i == 0)
        def _():
            dws_ref[...] = jnp.zeros_like(dws_ref)
            dbst_ref[...] = jnp.zeros_like(dbst_ref)
            fetch(0, 0).start()
            fetch(1, 1).start()

        @pl.when(i + 2 < n_steps)
        def _():
            fetch(i + 2, (i + 2) % MERGE_RING).start()

        slot = i % MERGE_RING
        fetch(i, slot).wait()
        wsc = _causal_ws(ws_ref)
        f = _branch_forward(ring[slot, 0], ring[slot, 1], rr_ref[...], wsc, bst_ref[...], mixed_ref)
        pj3 = ring[slot, 2]
        smr, sms = _sigmoid(pj3[:, :D_MODEL]), _sigmoid(pj3[:, D_MODEL:])
        dmerged = _mm_nt(wide_ref[0], wo_ref[...])
        da_f, db_f = dmerged * smr, dmerged * sms
        da, db = da_f.astype(MXU_DTYPE), db_f.astype(MXU_DTYPE)
        dpj3_ref[:, :D_MODEL] = (da_f * wide_ref[1] * (1.0 - smr)).astype(dpj3_ref.dtype)
        dpj3_ref[:, D_MODEL:] = (db_f * wide_ref[2] * (1.0 - sms)).astype(dpj3_ref.dtype)
        da_ref[...] = da
        db_ref[...] = db

        dret = _mm_nt(da, wro_ref[...])
        dpj1_ref[:, :D_MODEL] = (dret * f["rn"] * f["dsil_rg"]).astype(dpj1_ref.dtype)
        drn = dret * f["sil_rg"]
        for h in range(RET_HEADS):
            cols = slice(h * V_DIM, (h + 1) * V_DIM)
            drr_ref[:, cols] = _unit_norm_bwd(drn[:, cols], f["rn"][:, cols], f["rstd_r"][h]).astype(drr_ref.dtype)

        dsgu = _mm_nt(db, wso_ref[...])
        dpj2_ref[:, D_MODEL:] = (dsgu * f["u"] * f["mixed"] * f["dsil_sg"]).astype(dpj2_ref.dtype)
        tg = dsgu * f["sil_sg"]
        dpj1_ref[:, D_MODEL:] = (tg * f["mixed"] * f["du"]).astype(dpj1_ref.dtype)
        dmixed = tg * f["u"]
        for cc in range(TILE_M // CHUNK):
            for g in range(SGU_GROUPS):
                rs, cs = slice(cc * CHUNK, (cc + 1) * CHUNK), slice(g * GROUP_DIM, (g + 1) * GROUP_DIM)
                dm = dmixed[rs, cs]
                mixed_ref[rs, cs] = _mm_tn(wsc[g], dm)
                dws_ref[g] += _mm_nt(dm, f["vn"][rs, cs])
                dbst_ref[:, g:g + 1] += jnp.sum(dm, axis=1, keepdims=True)
        dvv = _unit_norm_bwd(mixed_ref[...], f["vn"], f["rstd_v"])
        dpj2_ref[:, :D_MODEL] = (dvv * f["dgelu_sv"]).astype(dpj2_ref.dtype)

    sq = (D_MODEL, D_MODEL)
    return pl.pallas_call(
        body, name="merge_bwd", grid=(n_steps,),
        in_specs=[pl.BlockSpec((3, TILE_M, D_MODEL), lambda i: (0, i, 0)), pl.BlockSpec(memory_space=pl.ANY)]
        + [_row_tile(D_MODEL)] + [_resident(ws.shape), _resident(bst.shape), _resident(sq), _resident(sq), _resident(sq)],
        out_specs=[_row_tile(SHARD_W)] * 3 + [_row_tile(D_MODEL)] * 3
        + [pl.BlockSpec(ws.shape, lambda i: (0, 0, 0)), pl.BlockSpec(bst.shape, lambda i: (0, 0))],
        out_shape=[jax.ShapeDtypeStruct((t, SHARD_W), MXU_DTYPE)] * 3 + [jax.ShapeDtypeStruct((t, D_MODEL), MXU_DTYPE)] * 3
        + [jax.ShapeDtypeStruct(ws.shape, F32), jax.ShapeDtypeStruct(bst.shape, F32)],
        scratch_shapes=[pltpu.VMEM((TILE_M, D_MODEL), F32), pltpu.VMEM((MERGE_RING, 3, TILE_M, SHARD_W), F32),
                        pltpu.SemaphoreType.DMA((MERGE_RING,))],
        compiler_params=_params(("arbitrary",)),
    )(wide, pj, ret_raw, ws, bst, w_ro, w_so, w_o)


def _tail_call(pj, ret_raw, x2d, p2d, target, ws, bst, g_ple, g_final, w_ro, w_so, w_o, w_pg, w_pp):
    t = x2d.shape[0]
    tm = TILE_M

    def body(pj1_ref, pj2_ref, pj3_ref, rr_ref, x_ref, p_ref, t_ref, ws_ref, bst_ref, gp_ref, gf_ref,
             wro_ref, wso_ref, wo_ref, wpg_ref, wpp_ref,
             wide_ref, half_ref, loss_ref, dgp_ref, dgf_ref, mixed_ref):
        dx1_ref, a_ref, b_ref = (wide_ref.at[k] for k in range(3))
        ret_ref, sgu_ref, mg_ref, hp_ref, dz_ref, dpp_ref = (half_ref.at[k] for k in range(6))
        @pl.when(pl.program_id(0) == 0)
        def _():
            for acc in (loss_ref, dgp_ref, dgf_ref):
                acc[...] = jnp.zeros_like(acc)

        f = _branch_forward(pj1_ref[...], pj2_ref[...], rr_ref[...], _causal_ws(ws_ref), bst_ref[...], mixed_ref)
        ret = f["ret"].astype(MXU_DTYPE)
        sgu = f["sgu"].astype(MXU_DTYPE)
        ret_ref[...] = ret
        sgu_ref[...] = sgu
        a = jnp.dot(ret, wro_ref[...], preferred_element_type=F32)
        b = jnp.dot(sgu, wso_ref[...], preferred_element_type=F32)
        a_ref[...] = a
        b_ref[...] = b
        pj3 = pj3_ref[...]
        smr, sms = _sigmoid(pj3[:, :D_MODEL]), _sigmoid(pj3[:, D_MODEL:])
        merged = (smr * a + sms * b).astype(MXU_DTYPE)
        mg_ref[...] = merged
        x1v = x_ref[...] + jnp.dot(merged, wo_ref[...], preferred_element_type=F32)
        xn1, r1 = _rms(x1v)
        hp = (xn1 * gp_ref[...]).astype(MXU_DTYPE)
        hp_ref[...] = hp
        gate = _sigmoid(jnp.dot(hp, wpg_ref[...], preferred_element_type=F32))
        pp = jnp.dot(p_ref[...].astype(MXU_DTYPE), wpp_ref[...], preferred_element_type=F32)
        xn2, r2 = _rms(x1v + gate * pp)
        err = xn2 * gf_ref[...] - t_ref[...]
        loss_ref[...] += (0.5 / D_MODEL) * jnp.sum(jnp.sum(err * err, axis=1, keepdims=True), axis=0, keepdims=True)

        dy = err * (1.0 / D_MODEL)
        dgf_ref[...] += jnp.sum(dy * xn2, axis=0, keepdims=True)
        dx2 = _rms_bwd(dy * gf_ref[...], xn2, r2)
        dpp_ref[...] = (dx2 * gate).astype(MXU_DTYPE)
        dz = (dx2 * pp * gate * (1.0 - gate)).astype(MXU_DTYPE)
        dz_ref[...] = dz
        dhp = _mm_nt(dz, wpg_ref[...])
        dgp_ref[...] += jnp.sum(dhp * xn1, axis=0, keepdims=True)
        dx1_ref[...] = dx2 + _rms_bwd(dhp * gp_ref[...], xn1, r1)

    sq = (D_MODEL, D_MODEL)
    vec = _resident((1, D_MODEL))
    whole = lambda shape: pl.BlockSpec(shape, lambda i: (0,) * len(shape))
    return pl.pallas_call(
        body, name="tail_fwd", grid=(t // tm,),
        in_specs=[_proj_tile(j) for j in (1, 2, 3)]
        + [_row_tile(D_MODEL), _row_tile(D_MODEL), _row_tile(PLE_DIM), _row_tile(D_MODEL), _resident(ws.shape),
           _resident(bst.shape), vec, vec, _resident(sq), _resident(sq), _resident(sq), _resident(sq),
           _resident((PLE_DIM, D_MODEL))],
        out_specs=[pl.BlockSpec((3, tm, D_MODEL), lambda i: (0, i, 0)), pl.BlockSpec((6, tm, D_MODEL), lambda i: (0, i, 0)),
                   whole(LOSS_TILE), whole((1, D_MODEL)), whole((1, D_MODEL))],
        out_shape=[jax.ShapeDtypeStruct((3, t, D_MODEL), F32), jax.ShapeDtypeStruct((6, t, D_MODEL), MXU_DTYPE),
                   jax.ShapeDtypeStruct(LOSS_TILE, F32), jax.ShapeDtypeStruct((1, D_MODEL), F32),
           jax.ShapeDtypeStruct((1, D_MODEL), F32)],
        scratch_shapes=[pltpu.VMEM((TILE_M, D_MODEL), F32)], compiler_params=_params(("arbitrary",)),
    )(pj, pj, pj, ret_raw, x2d, p2d, target, ws, bst, g_ple, g_final, w_ro, w_so, w_o, w_pg, w_pp)


def _retention_bwd_call(pj0, drr, states, consts, n_seq, seq, side):
    cos_full, sin_signed, decay_in, zeta, xi = consts
    nb = seq // RET_TILE
    cpb = RET_TILE // CHUNK

    def body(pj_ref, do_ref, st_ref, cos_ref, sin_ref, d_ref, zeta_ref, xi_ref, dpj_ref, gstate):
        @pl.when(pl.program_id(1) == 0)
        def _():
            gstate[...] = jnp.zeros_like(gstate)

        for cc in reversed(range(cpb)):
            rows = slice(cc * CHUNK, (cc + 1) * CHUNK)
            cs, sn = cos_ref[rows, :], sin_ref[rows, :]
            for h in range(RET_HEADS):
                q, k, v = _qkv(pj_ref, rows, h)
                qt = (q * cs + _rot(q) * sn) * QK_SCALE
                kt = k * cs + _rot(k) * sn
                d_out = do_ref[rows, h * V_DIM:(h + 1) * V_DIM]
                prev = st_ref[cc, h]
                g = gstate[h]
                dec = d_ref[h]
                scores_d = _mm_nt(qt, kt) * dec
                dscores = _mm_nt(d_out, v) * dec
                kz = kt * zeta_ref[h]
                qx = qt * xi_ref[h]
                dv = _mm_tn(scores_d, d_out) + _mm(kz, g)
                dqt = (_mm(dscores, kt) + _mm_nt(d_out, prev) * xi_ref[h]) * QK_SCALE
                dkt = _mm_tn(dscores, qt) + _mm_nt(v, g) * zeta_ref[h]
                gstate[h] = _mm_tn(qx, d_out) + CHUNK_DECAY[h] * g
                dq = dqt * cs + _rot(dqt * sn)
                dk = dkt * cs + _rot(dkt * sn)
                dpj_ref[rows, h * QK_DIM:(h + 1) * QK_DIM] = dq.astype(dpj_ref.dtype)
                dpj_ref[rows, RET_HEADS * QK_DIM + h * QK_DIM:RET_HEADS * QK_DIM + (h + 1) * QK_DIM] = dk.astype(
                    dpj_ref.dtype)
                dpj_ref[rows, 2 * RET_HEADS * QK_DIM + h * V_DIM:2 * RET_HEADS * QK_DIM + (h + 1) * V_DIM] = dv.astype(
                    dpj_ref.dtype)

    row = lambda b, n: (b * nb + nb - 1 - n, 0)
    pos = lambda b, n: (nb - 1 - n, 0)
    return _compute_call(
        body, name="retention_bwd", grid=(n_seq, nb),
        in_specs=[pl.BlockSpec((None, RET_TILE, SHARD_W), lambda b, n: (0,) + row(b, n)),
                  pl.BlockSpec((RET_TILE, D_MODEL), row),
                  pl.BlockSpec((cpb, RET_HEADS, QK_DIM, V_DIM), lambda b, n: (b * nb + nb - 1 - n, 0, 0, 0)),
                  pl.BlockSpec((RET_TILE, QK_DIM), pos), pl.BlockSpec((RET_TILE, QK_DIM), pos),
                  _resident(decay_in.shape), _resident(zeta.shape), _resident(xi.shape)],
        out_specs=[pl.BlockSpec((RET_TILE, SHARD_W), row)],
        out_shape=[jax.ShapeDtypeStruct((n_seq * seq, SHARD_W), MXU_DTYPE)],
        scratch_shapes=[pltpu.VMEM((RET_HEADS, QK_DIM, V_DIM), F32)],
        operands=(pj0, drr, states, cos_full, sin_signed, decay_in, zeta, xi),
        semantics=("arbitrary", "arbitrary"), side=side)


def _dx_call(dpj, x2d, wide, g_mixer, wg_in, side):
    t = x2d.shape[0]

    def body(d0, d1, d2, d3, x_ref, dx1_ref, g_ref, w_ref, dx_ref, dg_ref):
        @pl.when(pl.program_id(0) == 0)
        def _():
            dg_ref[...] = jnp.zeros_like(dg_ref)

        dh = _mm_nt(d0[...], w_ref[0])
        for j, d_ref in enumerate((d1, d2, d3)):
            dh += _mm_nt(d_ref[...], w_ref[j + 1])
        xn, r = _rms(x_ref[...])
        dg_ref[...] += jnp.sum(dh * xn, axis=0, keepdims=True)
        dx_ref[...] = dx1_ref[...] + _rms_bwd(dh * g_ref[...], xn, r)

    return _compute_call(
        body, name="dx_bwd", grid=(t // DX_TILE,),
        in_specs=[_row_tile(SHARD_W, DX_TILE)] * N_CHIPS
        + [_row_tile(D_MODEL, DX_TILE), pl.BlockSpec((None, DX_TILE, D_MODEL), lambda i: (0, i, 0))]
        + [_resident((1, D_MODEL)), _resident(wg_in.shape)],
        out_specs=[_row_tile(D_MODEL, DX_TILE), pl.BlockSpec((1, D_MODEL), lambda i: (0, 0))],
        out_shape=[jax.ShapeDtypeStruct((t, D_MODEL), F32), jax.ShapeDtypeStruct((1, D_MODEL), F32)],
        operands=(*dpj, x2d, wide, g_mixer, wg_in), semantics=("arbitrary",), side=side)


def _wgrad_call(name, lhs, rhs, block_n, out_cols=None, block_t=1024, into=None, slot=0, n_slots=1, side=None):
    (lhs, lhs_at), (rhs, rhs_at) = [x if isinstance(x, tuple) else (x, None) for x in (lhs, rhs)]
    t, n = rhs.shape[-2:]
    k = lhs.shape[-1]
    block_t = min(block_t, t)

    def operand_spec(at, block, index):
        if at is None:
            return pl.BlockSpec(block, index)
        return pl.BlockSpec((None,) + block, lambda j, i: (at,) + index(j, i))

    steps = t // block_t
    out_cols = block_n if out_cols is None else out_cols
    per = block_n // out_cols
    first_block = slot * (n // block_n)

    def body(l_ref, r_ref, *rest):
        o_ref, acc = rest[-2:]
        @pl.when(pl.program_id(1) == 0)
        def _():
            acc[...] = jnp.zeros_like(acc)

        acc[...] += _mm_tn(l_ref[...], r_ref[...])

        @pl.when(pl.program_id(1) == steps - 1)
        def _():
            for s in range(per):
                o_ref[s] = acc[:, s * out_cols:(s + 1) * out_cols].astype(o_ref.dtype)

    res = _compute_call(
        body, name=name, grid=(n // block_n, steps),
        in_specs=[operand_spec(lhs_at, (block_t, k), lambda j, i: (i, 0)),
                  operand_spec(rhs_at, (block_t, block_n), lambda j, i: (i, j))] + ([] if into is None else [ANY]),
        out_specs=[pl.BlockSpec((per, k, out_cols), lambda j, i: (first_block + j, 0, 0))],
        out_shape=[jax.ShapeDtypeStruct((n_slots * (n // out_cols), k, out_cols), COMM_DTYPE)],
        scratch_shapes=[pltpu.VMEM((k, block_n), F32)], aliases={} if into is None else {2: 0},
        operands=(lhs, rhs) if into is None else (lhs, rhs, into), semantics=("arbitrary", "arbitrary"), side=side)
    return res[0] if side is None else (res[0][0], res[1])


def _position():
    return lax.axis_index("x"), lax.axis_index("y"), lax.axis_index("c")


def _position_array():
    x, y, c = _position()
    return jnp.stack([2 * x + y, c]).astype(jnp.int32)


def _other_chip(x, y, k):
    return (1 - x if k & 2 else x), (1 - y if k & 1 else y)


def _plan_side(operands, out_shapes, plan, n_remote, aliases=None):
    def copies(ins, outs, send_sem, recv_sem, base=0):
        remote = plan(ins, outs)
        assert len(remote) == n_remote
        return [pltpu.make_async_remote_copy(src_ref=src, dst_ref=dst, send_sem=send_sem.at[base + i],
                                             recv_sem=recv_sem.at[base + i], device_id=dev, device_id_type=MESH)
                for i, (src, dst, dev) in enumerate(remote)]

    def start(*a):
        for cp in copies(*a):
            cp.start()

    def finish(*a):
        for cp in copies(*a):
            cp.wait()

    return _Side(operands, out_shapes, n_remote, start, finish, aliases)


def _place_cast_call(name, shards, pos):
    n = len(shards)
    rows, width = shards[0].shape
    block_rows = min(256, rows)

    def body(pos_ref, *refs):
        for w_ref, o_ref in zip(refs[:n], refs[n:]):
            o_ref[...] = w_ref[...].astype(o_ref.dtype)

    return pl.pallas_call(
        body, name=name,
        grid_spec=pltpu.PrefetchScalarGridSpec(
            num_scalar_prefetch=1, grid=(rows // block_rows,),
            in_specs=[pl.BlockSpec((block_rows, width), lambda i, pos: (i, 0))] * n,
            out_specs=[pl.BlockSpec((None, block_rows, width), lambda i, pos: (pos[0], i, 0))] * n),
        out_shape=[jax.ShapeDtypeStruct((N_CHIPS, rows, width), MXU_DTYPE)] * n,
        compiler_params=_params(("arbitrary",)),
    )(pos, *shards)


def _gather_side(placed):
    n = len(placed)

    def copies(kind, bufs, send_sem, recv_sem, base):
        x, y, c = _position()
        me = 2 * x + y
        made = []
        for i in range(n):
            hr = placed[i].shape[1] // 2
            for k in (1, 2, 3):
                px, py = _other_chip(x, y, k)
                chip, core, slot, dev = [(me, c, k - 1, (px, py, c)), (2 * px + py, c, 2 + k, (x, y, 1 - c)),
                                         (2 * px + py, 1 - c, 2 + k, (x, y, 1 - c))][kind]
                piece = bufs[i].at[chip, pl.ds(core * hr, hr)]
                made.append(pltpu.make_async_remote_copy(
                    src_ref=piece, dst_ref=piece, send_sem=send_sem.at[base + 6 * i + slot],
                    recv_sem=recv_sem.at[base + 6 * i + slot], device_id=dev, device_id_type=MESH))
        return made

    def start(ins, outs, send_sem, recv_sem, base=0):
        for cp in copies(0, outs, send_sem, recv_sem, base):
            cp.start()

    def finish(ins, outs, send_sem, recv_sem, base=0):
        first, onward = copies(0, outs, send_sem, recv_sem, base), copies(1, outs, send_sem, recv_sem, base)
        for landed, cp in zip(first, onward):
            landed.wait_recv()
            cp.start()
        for cp in copies(2, outs, send_sem, recv_sem, base):
            cp.wait_recv()
        for cp in first + onward:
            cp.wait_send()

    return _Side(placed, [jax.ShapeDtypeStruct(a.shape, a.dtype) for a in placed], 6 * n, start, finish,
                 {i: i for i in range(n)})


def _same_shape(shapes):
    found = {}
    for i, shape in enumerate(shapes):
        found.setdefault(shape, []).append(i)
    return found


def _pair_stage(tag, grads):
    n = len(grads)
    shapes = [(g.shape[0] // N_CHIPS, g.shape[1]) for g in grads]
    kinds = list(_same_shape(shapes))
    whole = {(r, w): N_CHIPS * (r // 2) <= 2 * PAIR_CHUNK_ROWS for r, w in kinds}
    chunk = {(r, w): N_CHIPS * (r // 2) if whole[(r, w)] else min(r // 2, PAIR_CHUNK_ROWS) for r, w in kinds}
    work = [(i, j, q) for i, (r, w) in enumerate(shapes) for j in ((None,) if whole[(r, w)] else range(N_CHIPS))
            for q in range(1 if whole[(r, w)] else (r // 2) // chunk[(r, w)])]

    def body(*refs):
        ins, sums, landed = refs[:n], refs[n:2 * n], refs[2 * n:3 * n]
        bufs, (load_sem, store_sem, send_sem, recv_sem) = refs[3 * n:-4], refs[-4:]
        x, y, c = _position()

        def piece(i, j):
            r = shapes[i][0]
            return pltpu.make_async_remote_copy(
                src_ref=ins[i].at[pl.ds(j * r + (1 - c) * (r // 2), r // 2)],
                dst_ref=landed[i].at[pl.ds(j * (r // 2), r // 2)], send_sem=send_sem.at[N_CHIPS * i + j],
                recv_sem=recv_sem.at[N_CHIPS * i + j], device_id=(x, y, 1 - c), device_id_type=MESH)

        def buffers(i, slot):
            own, got, out = bufs[3 * kinds.index(shapes[i]):3 * kinds.index(shapes[i]) + 3]
            return own.at[slot], got.at[slot], out.at[slot]

        def loads(step):
            i, j, q = work[step]
            r, ch = shapes[i][0], chunk[shapes[i]]
            own, got, _ = buffers(i, step % 2)
            if j is None:
                return [pltpu.make_async_copy(ins[i].at[pl.ds(jj * r + c * (r // 2), r // 2)],
                                              own.at[pl.ds(jj * (r // 2), r // 2)], load_sem.at[step % 2, jj])
                        for jj in range(N_CHIPS)] + [pltpu.make_async_copy(landed[i], got, load_sem.at[step % 2, N_CHIPS])]
            return [pltpu.make_async_copy(ins[i].at[pl.ds(j * r + c * (r // 2) + q * ch, ch)], own,
                                          load_sem.at[step % 2, 0]),
                    pltpu.make_async_copy(landed[i].at[pl.ds(j * (r // 2) + q * ch, ch)], got,
                                          load_sem.at[step % 2, 1])]

        def store(step):
            i, j, q = work[step]
            ch = chunk[shapes[i]]
            rows = sums[i] if j is None else sums[i].at[pl.ds(j * (shapes[i][0] // 2) + q * ch, ch)]
            return pltpu.make_async_copy(buffers(i, step % 2)[2], rows, store_sem.at[step % 2])

        def begin(step):
            i, j, q = work[step]
            for jj in (range(N_CHIPS) if j is None else [j] if q == 0 else []):
                piece(i, jj).wait_recv()
            for cp in loads(step):
                cp.start()

        for i in range(n):
            for j in range(N_CHIPS):
                piece(i, j).start()
        begin(0)
        for step in range(len(work)):
            if step + 1 < len(work):
                begin(step + 1)
            for cp in loads(step):
                cp.wait()
            if step >= 2:
                store(step - 2).wait()
            own, got, out = buffers(work[step][0], step % 2)
            out[...] = (own[...].astype(F32) + got[...].astype(F32)).astype(out.dtype)
            store(step).start()
        for step in range(max(len(work) - 2, 0), len(work)):
            store(step).wait()
        for i in range(n):
            for j in range(N_CHIPS):
                piece(i, j).wait_send()

    half = [jax.ShapeDtypeStruct((N_CHIPS * (r // 2), w), COMM_DTYPE) for r, w in shapes]
    res = pl.pallas_call(
        body, name=f"grad_pair_sum_{tag}", in_specs=[ANY] * n, out_specs=[ANY] * (2 * n), out_shape=half + half,
        scratch_shapes=[pltpu.VMEM((2, chunk[k], k[1]), COMM_DTYPE) for k in kinds for _ in range(3)]
        + [pltpu.SemaphoreType.DMA((2, N_CHIPS + 1)), pltpu.SemaphoreType.DMA((2,)),
           pltpu.SemaphoreType.DMA((N_CHIPS * n,)),
           pltpu.SemaphoreType.DMA((N_CHIPS * n,))],
        compiler_params=pltpu.CompilerParams(has_side_effects=True, vmem_limit_bytes=VMEM_LIMIT),
    )(*grads)
    return list(res[:n])


def _chip_side(pair_sums):
    halves = [(p.shape[0] // N_CHIPS, p.shape[1]) for p in pair_sums]

    def plan(ins, outs):
        x, y, c = _position()
        remote = []
        for i, (hr, _) in enumerate(halves):
            for k in (1, 2, 3):
                px, py = _other_chip(x, y, k)
                remote.append((ins[i].at[pl.ds((2 * px + py) * hr, hr)], outs[i].at[pl.ds((k - 1) * hr, hr)],
                               (px, py, c)))
        return remote

    return _plan_side(pair_sums, [jax.ShapeDtypeStruct((3 * hr, w), COMM_DTYPE) for hr, w in halves], plan,
                      3 * len(pair_sums))


def _finish_stage(tag, pair_sums, chip, extra):
    n = len(pair_sums)
    halves = [(p.shape[0] // N_CHIPS, p.shape[1]) for p in pair_sums]
    kinds = list(_same_shape(halves))
    chunk = {(hr, w): min(hr, PAIR_CHUNK_ROWS) for hr, w in kinds}
    work = [(i, q) for i, (hr, w) in enumerate(halves) for q in range(hr // chunk[(hr, w)])]
    e_in, e_out = len(extra.operands), len(extra.out_shapes)

    def body(*refs):
        sums, got, refs = refs[:n], refs[n:2 * n], refs[2 * n:]
        extra_ins, refs = refs[:e_in], refs[e_in:]
        full, refs = refs[:n], refs[n:]
        extra_outs, refs = refs[:e_out], refs[e_out:]
        bufs, (load_sem, store_sem, send_sem, recv_sem, extra_send, extra_recv) = refs[:-6], refs[-6:]
        x, y, c = _position()
        me = 2 * x + y
        extra.start(extra_ins, extra_outs, extra_send, extra_recv)

        def buffers(i, slot):
            at = 5 * kinds.index(halves[i])
            return [b_.at[slot] for b_ in bufs[at:at + 5]]

        def loads(step):
            i, q = work[step]
            hr, ch = halves[i][0], chunk[halves[i]]
            into = buffers(i, step % 2)
            made = [pltpu.make_async_copy(sums[i].at[pl.ds(me * hr + q * ch, ch)], into[0], load_sem.at[step % 2, 0])]
            made += [pltpu.make_async_copy(got[i].at[pl.ds(k * hr + q * ch, ch)], into[1 + k], load_sem.at[step % 2, 1 + k])
                     for k in range(3)]
            return made

        def stores(step):
            i, q = work[step]
            hr, ch = halves[i][0], chunk[halves[i]]
            result = buffers(i, step % 2)[4]
            rows = full[i].at[pl.ds(c * hr + q * ch, ch)]
            return (pltpu.make_async_copy(result, rows, store_sem.at[step % 2]),
                    pltpu.make_async_remote_copy(src_ref=result, dst_ref=rows, send_sem=send_sem.at[step],
                                                 recv_sem=recv_sem.at[step], device_id=(x, y, 1 - c),
                                                 device_id_type=MESH))

        for cp in loads(0):
            cp.start()
        for step in range(len(work)):
            if step + 1 < len(work):
                for cp in loads(step + 1):
                    cp.start()
            for cp in loads(step):
                cp.wait()
            if step >= 2:
                stores(step - 2)[0].wait()
                stores(step - 2)[1].wait_send()
            mine, k1, k2, k3, result = buffers(work[step][0], step % 2)
            result[...] = ((mine[...].astype(F32) + k1[...].astype(F32)) + k2[...].astype(F32)) + k3[...].astype(F32)
            for cp in stores(step):
                cp.start()
        for step in range(max(len(work) - 2, 0), len(work)):
            stores(step)[0].wait()
            stores(step)[1].wait_send()
        for step in range(len(work)):
            stores(step)[1].wait_recv()
        extra.finish(extra_ins, extra_outs, extra_send, extra_recv)

    scratch = []
    for k in kinds:
        scratch += [pltpu.VMEM((2, chunk[k], k[1]), COMM_DTYPE)] * 4 + [pltpu.VMEM((2, chunk[k], k[1]), F32)]
    res = pl.pallas_call(
        body, name=f"grad_finish_{tag}", in_specs=[ANY] * (2 * n + e_in), out_specs=[ANY] * (n + e_out),
        out_shape=[jax.ShapeDtypeStruct((2 * hr, w), F32) for hr, w in halves] + extra.out_shapes,
        scratch_shapes=scratch + [pltpu.SemaphoreType.DMA((2, 4)), pltpu.SemaphoreType.DMA((2,)),
                                  pltpu.SemaphoreType.DMA((len(work),)), pltpu.SemaphoreType.DMA((len(work),)),
                                  pltpu.SemaphoreType.DMA((extra.n_sems,)), pltpu.SemaphoreType.DMA((extra.n_sems,))],
        compiler_params=pltpu.CompilerParams(has_side_effects=True, vmem_limit_bytes=VMEM_LIMIT),
    )(*pair_sums, *chip, *extra.operands)
    return list(res[:n]), list(res[n:])


def _small_gather_side(parts):
    n = len(parts)

    def copies(ins, outs, send_sem, recv_sem, base):
        x, y, c = _position()
        made = []
        for i in range(n):
            mine = outs[i].at[4 * x + 2 * y + c]
            made.append(pltpu.make_async_copy(ins[i], mine, send_sem.at[base + 8 * i + 7]))
            for d in range(1, 8):
                px, py = _other_chip(x, y, d >> 1)
                made.append(pltpu.make_async_remote_copy(
                    src_ref=ins[i], dst_ref=mine, send_sem=send_sem.at[base + 8 * i + d - 1],
                    recv_sem=recv_sem.at[base + 8 * i + d - 1], device_id=(px, py, 1 - c if d & 1 else c),
                    device_id_type=MESH))
        return made

    def start(ins, outs, send_sem, recv_sem, base=0):
        for cp in copies(ins, outs, send_sem, recv_sem, base):
            cp.start()

    def finish(ins, outs, send_sem, recv_sem, base=0):
        for cp in copies(ins, outs, send_sem, recv_sem, base):
            cp.wait()

    return _Side(parts, [jax.ShapeDtypeStruct((8,) + a_.shape, F32) for a_ in parts], 8 * n, start, finish)


def _adamw(w, g, m, v):
    m = ADAM_B1 * m + (1.0 - ADAM_B1) * g
    v = ADAM_B2 * v + (1.0 - ADAM_B2) * (g * g)
    m_hat = m / (1.0 - ADAM_B1 ** ADAM_STEP)
    v_hat = v / (1.0 - ADAM_B2 ** ADAM_STEP)
    delta = -ADAM_LR * (m_hat / (jnp.sqrt(v_hat) + ADAM_EPS) + ADAM_WD * w)
    return delta, m, v


def _adamw_call(name, ws, gs, ms, vs):
    n = len(ws)
    rows, width = ws[0].shape
    block_rows = min(256 // n, rows)

    def body(*refs):
        for i in range(n):
            w_ref, g_ref, m_ref, v_ref = (refs[j * n + i] for j in range(4))
            d_out, m_out, v_out = (refs[(4 + j) * n + i] for j in range(3))
            d_out[...], m_out[...], v_out[...] = _adamw(w_ref[...], g_ref[...], m_ref[...], v_ref[...])

    spec = pl.BlockSpec((block_rows, width), lambda i: (i, 0))
    res = pl.pallas_call(
        body, name=name, grid=(rows // block_rows,), in_specs=[spec] * (4 * n), out_specs=[spec] * (3 * n),
        out_shape=[jax.ShapeDtypeStruct(ws[0].shape, F32)] * (3 * n),
        compiler_params=_params(("arbitrary",)),
    )(*ws, *gs, *ms, *vs)
    return [(res[i], res[n + i], res[2 * n + i]) for i in range(n)]


def _small_adamw_call(gathered, weights, moments_m, moments_v):
    n = len(weights)

    def body(*refs):
        all_refs, refs = refs[:n + 1], refs[n + 1:]
        w_refs, m_refs, v_refs, outs = refs[:n], refs[n:2 * n], refs[2 * n:3 * n], refs[3 * n:]

        def total(ref):
            acc = ref[0]
            for d in range(1, 8):
                acc = acc + ref[d]
            return acc

        outs[0][...] = total(all_refs[n])
        for i in range(n):
            g = total(all_refs[i])
            if i == 0:
                row = lax.broadcasted_iota(jnp.int32, g.shape, 0)
                col = lax.broadcasted_iota(jnp.int32, g.shape, 1)
                g = jnp.where((row % CHUNK) >= col, g, 0.0)
            g_out, d_out, m_out, v_out = outs[1 + 4 * i:5 + 4 * i]
            g_out[...] = g
            d_out[...], m_out[...], v_out[...] = _adamw(w_refs[i][...], g, m_refs[i][...], v_refs[i][...])

    out_shape = [jax.ShapeDtypeStruct(LOSS_TILE, F32)]
    for w in weights:
        out_shape += [jax.ShapeDtypeStruct(w.shape, F32)] * 4
    res = pl.pallas_call(
        body, name="small_adamw", out_shape=out_shape,
        compiler_params=pltpu.CompilerParams(vmem_limit_bytes=VMEM_LIMIT),
    )(*gathered, *weights, *moments_m, *moments_v)
    return res[0], [res[1 + 4 * i:5 + 4 * i] for i in range(n)]


def kernel(x, p, w_in, w_ret_out, w_sgu_out, w_out, sgu_ws, sgu_bs, w_ple_gate, w_ple_proj, g_mixer, g_ple, g_final, loss_target, m_w_in, m_w_ret_out, m_w_sgu_out, m_w_out, m_sgu_ws, m_sgu_bs, m_w_ple_gate, m_w_ple_proj, m_g_mixer, m_g_ple, m_g_final, v_w_in, v_w_ret_out, v_w_sgu_out, v_w_out, v_sgu_ws, v_sgu_bs, v_w_ple_gate, v_w_ple_proj, v_g_mixer, v_g_ple, v_g_final):
    n_seq, seq, _ = x.shape
    t = n_seq * seq
    x2d = x.reshape(t, D_MODEL)
    p2d = p.reshape(t, PLE_DIM)
    target = loss_target.reshape(t, D_MODEL)
    big = [w_in[0], w_ret_out[0], w_sgu_out[0], w_out[0], w_ple_gate[0], w_ple_proj[0]]
    big_m = [m_w_in[0], m_w_ret_out[0], m_w_sgu_out[0], m_w_out[0], m_w_ple_gate[0], m_w_ple_proj[0]]
    big_v = [v_w_in[0], v_w_ret_out[0], v_w_sgu_out[0], v_w_out[0], v_w_ple_gate[0], v_w_ple_proj[0]]

    pos = _position_array()
    placed = (_place_cast_call("place_w_in", big[:1], pos) + _place_cast_call("place_square_weights", big[1:5], pos)
              + _place_cast_call("place_w_ple_proj", big[5:], pos))
    ws = sgu_ws[0]
    bst = sgu_bs[0].T
    consts = _retention_consts(seq)

    (h, pj, wg_in), gathered = _proj_call(x2d, g_mixer, placed[0], pos, _gather_side(placed[1:]))
    pj0 = pj1 = pj2 = pj3 = pj
    w_ro, w_so, w_o, w_pg = (w.reshape(D_MODEL, D_MODEL) for w in gathered[:4])
    w_pp = gathered[4].transpose(1, 0, 2).reshape(PLE_DIM, D_MODEL)
    ret_raw, states = _retention_fwd_call(pj0, consts, n_seq, seq)
    wide, half, loss, dg_ple, dg_final = _tail_call(
        pj, ret_raw, x2d, p2d, target, ws, bst, g_ple, g_final.reshape(1, D_MODEL), w_ro, w_so, w_o, w_pg, w_pp)
    ret, sgu, merged, hp, dz, dpp = ((half, k) for k in range(6))
    dpj1, dpj2, dpj3, drr, da, db, dws, dbst = _merge_bwd_call(wide, pj, ret_raw, ws, bst, w_ro, w_so, w_o)

    small_shapes = [(SGU_GROUPS * CHUNK, CHUNK), (SGU_GROUPS, CHUNK), (1, D_MODEL), (1, D_MODEL), (1, D_MODEL)]
    as_small = lambda arrays: [None if a_ is None else a_.reshape(s_) for a_, s_ in zip(arrays, small_shapes)]
    early = as_small([dws, dbst.T, None, dg_ple, dg_final])
    rows_of = lambda g: g.reshape(g.shape[0] * g.shape[1], g.shape[2])
    tail_grads = [
        rows_of(_wgrad_call("wgrad_ret_out", ret, da, D_MODEL)),
        rows_of(_wgrad_call("wgrad_sgu_out", sgu, db, D_MODEL)),
        rows_of(_wgrad_call("wgrad_out", merged, (wide, 0), D_MODEL)),
        rows_of(_wgrad_call("wgrad_ple_gate", hp, dz, D_MODEL)),
        rows_of(_wgrad_call("wgrad_ple_proj", p2d, dpp, D_MODEL, out_cols=PLE_DIM)),
    ]
    tail_sums = _pair_stage("tail", tail_grads)
    (dpj0,), carried = _retention_bwd_call(
        pj0, drr, states, consts, n_seq, seq,
        _join_sides(_chip_side(tail_sums), _small_gather_side([early[0], early[1], early[3], early[4], loss])))
    tail_chip, early_all = carried[:len(tail_sums)], carried[len(tail_sums):]
    in_grad = None
    for j, d in enumerate((dpj0, dpj1, dpj2, dpj3)):
        in_grad = _wgrad_call(f"wgrad_in_{j}", h, d, SHARD_W, into=in_grad, slot=j, n_slots=N_CHIPS)
    in_sums = _pair_stage("in", [rows_of(in_grad)])
    (dx, dg_mixer), in_chip = _dx_call((dpj0, dpj1, dpj2, dpj3), x2d, wide, g_mixer, wg_in, _chip_side(in_sums))
    g_big, (mixer_all,) = _finish_stage("all", in_sums + tail_sums, list(in_chip) + list(tail_chip),
                                        _small_gather_side([dg_mixer]))
    upd = [None] * len(big)
    for _, which in _same_shape([w.shape for w in big]).items():
        pick = lambda arrays: [arrays[i] for i in which]
        for i, triple in zip(which, _adamw_call(f"adamw_{which[0]}", pick(big), pick(g_big), pick(big_m), pick(big_v))):
            upd[i] = triple

    small_g = [early_all[0], early_all[1], mixer_all, early_all[2], early_all[3], early_all[4]]
    total, small = _small_adamw_call(small_g, as_small([sgu_ws, sgu_bs, g_mixer, g_ple, g_final]),
                                     as_small([m_sgu_ws, m_sgu_bs, m_g_mixer, m_g_ple, m_g_final]),
                                     as_small([v_sgu_ws, v_sgu_bs, v_g_mixer, v_g_ple, v_g_final]))
    out_small_shapes = [sgu_ws.shape, sgu_bs.shape, g_mixer.shape, g_ple.shape, g_final.shape]

    def ordered(big_list, kind):
        w_in_, w_ro_, w_so_, w_o_, w_pg_, w_pp_ = [b_[None] for b_ in big_list]
        s_ws, s_bs, s_gm, s_gp, s_gf = [small[i][kind].reshape(s) for i, s in enumerate(out_small_shapes)]
        return [w_in_, w_ro_, w_so_, w_o_, s_ws, s_bs, w_pg_, w_pp_, s_gm, s_gp, s_gf]

    out = [total[0, 0], dx.reshape(x.shape)]
    out += ordered(g_big, 0)
    out += ordered([u[0] for u in upd], 1)
    out += ordered([u[1] for u in upd], 2)
    out += ordered([u[2] for u in upd], 3)
    return tuple(out)
```

```python
import functools
import math

import numpy as np
import jax
import jax.numpy as jnp
from jax import lax
from jax.experimental import pallas as pl
from jax.experimental.pallas import tpu as pltpu

F32 = jnp.float32
MXU_DTYPE = jnp.bfloat16
COMM_DTYPE = jnp.bfloat16

D_MODEL = 1024
RET_HEADS = 4
QK_DIM = 128
V_DIM = 256
CHUNK = 128
SGU_GROUPS = 4
GROUP_DIM = 256
PLE_DIM = 256
N_CHIPS = 4
SHARD_W = 2048
ROPE_BASE = 10000.0
NORM_EPS = 1e-6
GN_EPS = 1e-5
QK_SCALE = QK_DIM ** -0.5
SQRT_HALF = math.sqrt(0.5)
INV_SQRT_2PI = 1.0 / math.sqrt(2.0 * math.pi)

ADAM_LR = 0.001
ADAM_B1 = 0.9
ADAM_B2 = 0.999
ADAM_EPS = 1e-08
ADAM_WD = 0.01
ADAM_STEP = 10

TILE_M = 256
LOSS_TILE = (8, 128)
LOAD_PARTS = 4
PAIR_CHUNK_ROWS = 256
PROJ_TILE = 512
RET_TILE = 512
DX_TILE = 512
VMEM_LIMIT = 56 * 1024 * 1024
MESH = pl.DeviceIdType.MESH
ANY = pl.BlockSpec(memory_space=pl.ANY)

CHUNK_DECAY = tuple(
    float(np.exp(np.float32(CHUNK) * np.log(np.float32(1.0 - 2.0 ** (-5.0 - h))))) for h in range(RET_HEADS))


def _mm(a, b):
    return jnp.dot(a.astype(MXU_DTYPE), b.astype(MXU_DTYPE), preferred_element_type=F32)


def _mm_nt(a, b):
    return lax.dot_general(a.astype(MXU_DTYPE), b.astype(MXU_DTYPE), (((1,), (1,)), ((), ())),
                           preferred_element_type=F32)


def _mm_tn(a, b):
    return lax.dot_general(a.astype(MXU_DTYPE), b.astype(MXU_DTYPE), (((0,), (0,)), ((), ())),
                           preferred_element_type=F32)


def _mean(x):
    return jnp.mean(x, axis=-1, keepdims=True)


def _sigmoid(x):
    return jax.nn.sigmoid(x)


def _silu_and_grad(x):
    s = _sigmoid(x)
    silu = x * s
    return silu, s + silu * (1.0 - s)


def _gelu_and_grad(x):
    cdf = 0.5 + 0.5 * lax.erf(x * SQRT_HALF)
    return x * cdf, cdf + x * (jnp.exp(x * x * -0.5) * INV_SQRT_2PI)


def _unit_norm(x, eps):
    xc = x - _mean(x)
    rstd = lax.rsqrt(_mean(xc * xc) + eps)
    return xc * rstd, rstd


def _unit_norm_bwd(dn, n, rstd):
    return rstd * (dn - _mean(dn) - n * _mean(dn * n))


def _rms(x):
    r = lax.rsqrt(_mean(x * x) + NORM_EPS)
    return x * r, r


def _rms_bwd(dxn, xn, r):
    return r * (dxn - xn * _mean(dxn * xn))


def _rot(x):
    return pltpu.roll(x, QK_DIM // 2, 1)


def _params(semantics, **kw):
    return pltpu.CompilerParams(dimension_semantics=semantics, vmem_limit_bytes=VMEM_LIMIT, **kw)


def _row_tile(width, tm=TILE_M):
    return pl.BlockSpec((tm, width), lambda i: (i, 0))


def _proj_tile(j):
    return pl.BlockSpec((None, TILE_M, SHARD_W), lambda i: (j, i, 0))


def _resident(shape):
    nd = len(shape)
    return pl.BlockSpec(shape, lambda *_: (0,) * nd, pipeline_mode=pl.Buffered(1))


def _causal_ws(ws_ref):
    row = lax.broadcasted_iota(jnp.int32, (CHUNK, CHUNK), 0)
    col = lax.broadcasted_iota(jnp.int32, (CHUNK, CHUNK), 1)
    return [jnp.where(row >= col, ws_ref[g], 0.0).astype(MXU_DTYPE) for g in range(SGU_GROUPS)]


def _heads(x, width):
    return [x[:, h * width:(h + 1) * width] for h in range(x.shape[1] // width)]


def _branch_forward(pj1, pj2, ret_raw, wsc, bst, mixed_ref):
    rg, su = pj1[:, :D_MODEL], pj1[:, D_MODEL:]
    sv, sg = pj2[:, :D_MODEL], pj2[:, D_MODEL:]
    rn_parts, rstd_parts = zip(*[_unit_norm(r, GN_EPS) for r in _heads(ret_raw, V_DIM)])
    rn = jnp.concatenate(rn_parts, axis=1)
    sil_rg, dsil_rg = _silu_and_grad(rg)
    ret = rn * sil_rg
    u, du = _gelu_and_grad(su)
    gelu_sv, dgelu_sv = _gelu_and_grad(sv)
    vn, rstd_v = _unit_norm(gelu_sv, GN_EPS)
    for cc in range(pj1.shape[0] // CHUNK):
        for g in range(SGU_GROUPS):
            rs, cs = slice(cc * CHUNK, (cc + 1) * CHUNK), slice(g * GROUP_DIM, (g + 1) * GROUP_DIM)
            mixed_ref[rs, cs] = _mm(wsc[g], vn[rs, cs]) + bst[:, g:g + 1]
    mixed = mixed_ref[...]
    sil_sg, dsil_sg = _silu_and_grad(sg)
    sgu = u * mixed * sil_sg
    return dict(rn=rn, rstd_r=rstd_parts, sil_rg=sil_rg, dsil_rg=dsil_rg, ret=ret, u=u, du=du, dgelu_sv=dgelu_sv,
                vn=vn, rstd_v=rstd_v, mixed=mixed, sil_sg=sil_sg, dsil_sg=dsil_sg, sgu=sgu)


class _Side:
    def __init__(self, operands, out_shapes, n_sems, start, finish, aliases=None):
        self.operands, self.out_shapes, self.n_sems = list(operands), list(out_shapes), n_sems
        self.start, self.finish, self.aliases = start, finish, dict(aliases or {})


def _join_sides(*sides):
    spans, a, b, s = [], 0, 0, 0
    for side in sides:
        spans.append((a, b, s))
        a, b, s = a + len(side.operands), b + len(side.out_shapes), s + side.n_sems

    def run(which):
        def go(ins, outs, send_sem, recv_sem, base=0):
            for side, (a0, b0, s0) in zip(sides, spans):
                getattr(side, which)(ins[a0:a0 + len(side.operands)], outs[b0:b0 + len(side.out_shapes)],
                                     send_sem, recv_sem, base + s0)
        return go

    aliases = {a0 + i: b0 + o for side, (a0, b0, _) in zip(sides, spans) for i, o in side.aliases.items()}
    return _Side([x for side in sides for x in side.operands], [x for side in sides for x in side.out_shapes], s,
                 run("start"), run("finish"), aliases)


def _compute_call(body, *, name, grid, in_specs, out_specs, out_shape, operands, semantics, scratch_shapes=(),
                  side=None, prefetch=None, aliases=None, side_start=None):
    n_pre = 0 if prefetch is None else 1
    pre = () if prefetch is None else (prefetch,)

    def spec(in_specs, out_specs, scratch):
        return pltpu.PrefetchScalarGridSpec(num_scalar_prefetch=n_pre, grid=grid, in_specs=in_specs,
                                            out_specs=out_specs, scratch_shapes=scratch)

    if side is None:
        return pl.pallas_call(body, name=name, grid_spec=spec(in_specs, out_specs, list(scratch_shapes)),
                              out_shape=out_shape,
                              input_output_aliases={n_pre + a: b for a, b in (aliases or {}).items()},
                              compiler_params=_params(semantics))(*pre, *operands)
    n_in, n_out, s_in, s_out = len(operands), len(out_shape), len(side.operands), len(side.out_shapes)

    def carrier(*refs):
        pre_refs, refs = refs[:n_pre], refs[n_pre:]
        ins, refs = refs[:n_in], refs[n_in:]
        side_ins, refs = refs[:s_in], refs[s_in:]
        outs, refs = refs[:n_out], refs[n_out:]
        side_outs, refs = refs[:s_out], refs[s_out:]
        scratch, (send_sem, recv_sem) = refs[:-2], refs[-2:]
        ids = [pl.program_id(a) for a in range(len(grid))]
        at = (0,) * len(grid) if side_start is None else side_start
        first = functools.reduce(jnp.logical_and, [i == a for i, a in zip(ids, at)])
        last = functools.reduce(jnp.logical_and, [i == g - 1 for i, g in zip(ids, grid)])

        @pl.when(first)
        def _():
            side.start(side_ins, side_outs, send_sem, recv_sem)

        body(*pre_refs, *ins, *outs, *scratch)

        @pl.when(last)
        def _():
            side.finish(side_ins, side_outs, send_sem, recv_sem)

    all_aliases = {n_pre + a: b for a, b in (aliases or {}).items()}
    all_aliases.update({n_pre + n_in + a: n_out + b for a, b in side.aliases.items()})
    res = pl.pallas_call(
        carrier, name=name,
        grid_spec=spec(list(in_specs) + [ANY] * s_in, list(out_specs) + [ANY] * s_out,
                       list(scratch_shapes) + [pltpu.SemaphoreType.DMA((side.n_sems,))] * 2),
        out_shape=list(out_shape) + side.out_shapes, input_output_aliases=all_aliases,
        compiler_params=_params(semantics, has_side_effects=True),
    )(*pre, *operands, *side.operands)
    return res[:n_out], res[n_out:]


def _proj_call(x2d, g_mixer, placed_in, pos, side):
    t = x2d.shape[0]
    nt = t // PROJ_TILE
    hr = placed_in.shape[1] // 2

    def body(pos_ref, x_ref, g_ref, win_ref, h_ref, pj_ref, w_ref, w_vmem, h_all, loc_sem, send_sem, recv_sem):
        k, i = pl.program_id(0), pl.program_id(1)
        x, y, c = _position()
        me = 2 * x + y

        def copy(slot, chip, core, dev):
            piece = w_ref.at[chip, pl.ds(core * hr, hr)]
            return pltpu.make_async_remote_copy(src_ref=piece, dst_ref=piece, send_sem=send_sem.at[slot],
                                                recv_sem=recv_sem.at[slot], device_id=dev, device_id_type=MESH)

        def first_hop(kk):
            px, py = _other_chip(x, y, kk)
            return copy(kk - 1, me, c, (px, py, c))

        def relay():
            source = jnp.bitwise_xor(me, 2 - c)
            return copy(2, source, c, (jnp.bitwise_xor(x, c), jnp.bitwise_xor(y, 1 - c), c))

        def onward(kk, core):
            px, py = _other_chip(x, y, kk)
            return copy(2 + kk, 2 * px + py, core, (x, y, 1 - c))

        def loads(kk):
            px, py = _other_chip(x, y, kk)
            rows = w_vmem.shape[1] // LOAD_PARTS
            return [pltpu.make_async_copy(w_ref.at[2 * px + py, pl.ds(r * rows, rows)],
                                          w_vmem.at[kk % 2, pl.ds(r * rows, rows)], loc_sem.at[r])
                    for r in range(LOAD_PARTS)]

        @pl.when(jnp.logical_and(k == 0, i == 0))
        def _():
            for kk in (1, 2):
                first_hop(kk).start()
            for cp in loads(0):
                cp.start()
            for cp in loads(0):
                cp.wait()

        for kk in (1, 2, 3):
            @pl.when(jnp.logical_and(k == kk - 1, i == nt - 1))
            def _(kk=kk):
                if kk == 1:
                    first_hop(1).wait_recv()
                    first_hop(2).wait_recv()
                    relay().start()
                if kk == 3:
                    relay().wait_recv()
                onward(kk, c).start()
                onward(kk, 1 - c).wait_recv()
                for cp in loads(kk):
                    cp.start()

            @pl.when(jnp.logical_and(k == kk, i == 0))
            def _(kk=kk):
                for cp in loads(kk):
                    cp.wait()

        rows = pl.ds(pl.multiple_of(i * PROJ_TILE, PROJ_TILE), PROJ_TILE)

        @pl.when(k == 0)
        def _():
            xn, _ = _rms(x_ref[...])
            h = (xn * g_ref[...]).astype(MXU_DTYPE)
            h_all[rows, :] = h
            h_ref[...] = h

        pj_ref[...] = jnp.dot(h_all[rows, :], w_vmem[k % 2], preferred_element_type=F32)

        @pl.when(jnp.logical_and(k == N_CHIPS - 1, i == nt - 1))
        def _():
            for cp in [first_hop(1), first_hop(2), relay()] + [onward(kk, c) for kk in (1, 2, 3)]:
                cp.wait_send()

    parked = lambda k, i, pos: (jnp.where(k == 0, i, nt - 1), 0)
    return _compute_call(
        body, name="proj_fwd", grid=(N_CHIPS, nt),
        in_specs=[pl.BlockSpec((PROJ_TILE, D_MODEL), parked), pl.BlockSpec((1, D_MODEL), lambda k, i, pos: (0, 0)), ANY],
        out_specs=[pl.BlockSpec((PROJ_TILE, D_MODEL), lambda k, i, pos: (jnp.where(k == 0, i, nt), 0)),
                   pl.BlockSpec((None, PROJ_TILE, SHARD_W), lambda k, i, pos: (jnp.bitwise_xor(pos[0], k), i, 0)),
                   ANY],
        out_shape=[jax.ShapeDtypeStruct((t + PROJ_TILE, D_MODEL), MXU_DTYPE),
                   jax.ShapeDtypeStruct((N_CHIPS, t, SHARD_W), F32),
                   jax.ShapeDtypeStruct(placed_in.shape, placed_in.dtype)],
        scratch_shapes=[pltpu.VMEM((2,) + placed_in.shape[1:], placed_in.dtype), pltpu.VMEM((t, D_MODEL), MXU_DTYPE),
                        pltpu.SemaphoreType.DMA((LOAD_PARTS,)), pltpu.SemaphoreType.DMA((6,)),
                        pltpu.SemaphoreType.DMA((6,))],
        operands=(x2d, g_mixer, placed_in), semantics=("arbitrary", "arbitrary"), side=side, prefetch=pos,
        aliases={2: 2}, side_start=(2, 0))


def _retention_consts(seq):
    half = QK_DIM // 2
    inv = ROPE_BASE ** (-jnp.arange(half, dtype=F32) / half)
    inv = jnp.concatenate([inv, inv])
    coarse = (jnp.arange(seq // half, dtype=F32) * half)[:, None] * inv[None, :]
    fine = jnp.arange(half, dtype=F32)[:, None] * inv[None, :]
    ca, sa, cb, sb = jnp.cos(coarse)[:, None], jnp.sin(coarse)[:, None], jnp.cos(fine)[None], jnp.sin(fine)[None]
    cos_full = (ca * cb - sa * sb).reshape(seq, QK_DIM)
    sign = jnp.concatenate([-jnp.ones((half,), F32), jnp.ones((half,), F32)])
    sin_signed = (sa * cb + ca * sb).reshape(seq, QK_DIM) * sign
    log_g = jnp.log(1.0 - 2.0 ** (-5.0 - jnp.arange(RET_HEADS, dtype=F32)))
    idx = jnp.arange(CHUNK, dtype=F32)
    diff = idx[:, None] - idx[None, :]
    decay_in = jnp.where(diff[None] >= 0, jnp.exp(jnp.maximum(diff, 0.0)[None] * log_g[:, None, None]), 0.0)
    zeta = jnp.exp((CHUNK - 1.0 - idx)[None, :] * log_g[:, None])
    xi = jnp.exp((idx + 1.0)[None, :] * log_g[:, None])
    zeta = jnp.broadcast_to(zeta[:, :, None], (RET_HEADS, CHUNK, QK_DIM))
    xi = jnp.broadcast_to(xi[:, :, None], (RET_HEADS, CHUNK, QK_DIM))
    return cos_full, sin_signed, decay_in, zeta, xi


def _qkv(pj_ref, rows, h):
    q = pj_ref[rows, h * QK_DIM:(h + 1) * QK_DIM]
    k = pj_ref[rows, RET_HEADS * QK_DIM + h * QK_DIM:RET_HEADS * QK_DIM + (h + 1) * QK_DIM]
    v = pj_ref[rows, 2 * RET_HEADS * QK_DIM + h * V_DIM:2 * RET_HEADS * QK_DIM + (h + 1) * V_DIM]
    return q, k, v


def _retention_fwd_call(pj0, consts, n_seq, seq):
    cos_full, sin_signed, decay_in, zeta, xi = consts
    nb = seq // RET_TILE
    cpb = RET_TILE // CHUNK

    def body(pj_ref, cos_ref, sin_ref, d_ref, zeta_ref, xi_ref, o_ref, st_ref, state):
        @pl.when(pl.program_id(1) == 0)
        def _():
            state[...] = jnp.zeros_like(state)

        for cc in range(cpb):
            rows = slice(cc * CHUNK, (cc + 1) * CHUNK)
            cs, sn = cos_ref[rows, :], sin_ref[rows, :]
            for h in range(RET_HEADS):
                q, k, v = _qkv(pj_ref, rows, h)
                qt = (q * cs + _rot(q) * sn) * QK_SCALE
                kt = k * cs + _rot(k) * sn
                prev = state[h]
                st_ref[cc, h] = prev.astype(st_ref.dtype)
                scores = _mm_nt(qt, kt) * d_ref[h]
                o_ref[rows, h * V_DIM:(h + 1) * V_DIM] = _mm(scores, v) + _mm(qt * xi_ref[h], prev)
                state[h] = _mm_tn(kt * zeta_ref[h], v) + CHUNK_DECAY[h] * prev

    row = lambda b, n: (b * nb + n, 0)
    pos = lambda b, n: (n, 0)
    return pl.pallas_call(
        body, name="retention_fwd", grid=(n_seq, nb),
        in_specs=[pl.BlockSpec((None, RET_TILE, SHARD_W), lambda b, n: (0,) + row(b, n)),
                  pl.BlockSpec((RET_TILE, QK_DIM), pos),
                  pl.BlockSpec((RET_TILE, QK_DIM), pos), _resident(decay_in.shape), _resident(zeta.shape),
                  _resident(xi.shape)],
        out_specs=[pl.BlockSpec((RET_TILE, D_MODEL), row),
                   pl.BlockSpec((cpb, RET_HEADS, QK_DIM, V_DIM), lambda b, n: (b * nb + n, 0, 0, 0))],
        out_shape=[jax.ShapeDtypeStruct((n_seq * seq, D_MODEL), F32),
                   jax.ShapeDtypeStruct((n_seq * seq // CHUNK, RET_HEADS, QK_DIM, V_DIM), MXU_DTYPE)],
        scratch_shapes=[pltpu.VMEM((RET_HEADS, QK_DIM, V_DIM), F32)],
        compiler_params=_params(("arbitrary", "arbitrary")),
    )(pj0, cos_full, sin_signed, decay_in, zeta, xi)


def _merge_bwd_call(wide, pj1, pj2, pj3, ret_raw, ws, bst, w_ro, w_so, w_o):
    t = wide.shape[1]

    def body(wide_ref, pj1_ref, pj2_ref, pj3_ref, rr_ref, ws_ref, bst_ref, wro_ref, wso_ref, wo_ref,
             dpj1_ref, dpj2_ref, dpj3_ref, drr_ref, da_ref, db_ref, dws_ref, dbst_ref, mixed_ref):
        @pl.when(pl.program_id(0) == 0)
        def _():
            dws_ref[...] = jnp.zeros_like(dws_ref)
            dbst_ref[...] = jnp.zeros_like(dbst_ref)

        wsc = _causal_ws(ws_ref)
        f = _branch_forward(pj1_ref[...], pj2_ref[...], rr_ref[...], wsc, bst_ref[...], mixed_ref)
        pj3 = pj3_ref[...]
        smr, sms = _sigmoid(pj3[:, :D_MODEL]), _sigmoid(pj3[:, D_MODEL:])
        dmerged = _mm_nt(wide_ref[0], wo_ref[...])
        da_f, db_f = dmerged * smr, dmerged * sms
        da, db = da_f.astype(MXU_DTYPE), db_f.astype(MXU_DTYPE)
        dpj3_ref[:, :D_MODEL] = (da_f * wide_ref[1] * (1.0 - smr)).astype(dpj3_ref.dtype)
        dpj3_ref[:, D_MODEL:] = (db_f * wide_ref[2] * (1.0 - sms)).astype(dpj3_ref.dtype)
        da_ref[...] = da
        db_ref[...] = db

        dret = _mm_nt(da, wro_ref[...])
        dpj1_ref[:, :D_MODEL] = (dret * f["rn"] * f["dsil_rg"]).astype(dpj1_ref.dtype)
        drn = dret * f["sil_rg"]
        for h in range(RET_HEADS):
            cols = slice(h * V_DIM, (h + 1) * V_DIM)
            drr_ref[:, cols] = _unit_norm_bwd(drn[:, cols], f["rn"][:, cols], f["rstd_r"][h]).astype(drr_ref.dtype)

        dsgu = _mm_nt(db, wso_ref[...])
        dpj2_ref[:, D_MODEL:] = (dsgu * f["u"] * f["mixed"] * f["dsil_sg"]).astype(dpj2_ref.dtype)
        tg = dsgu * f["sil_sg"]
        dpj1_ref[:, D_MODEL:] = (tg * f["mixed"] * f["du"]).astype(dpj1_ref.dtype)
        dmixed = tg * f["u"]
        for cc in range(TILE_M // CHUNK):
            for g in range(SGU_GROUPS):
                rs, cs = slice(cc * CHUNK, (cc + 1) * CHUNK), slice(g * GROUP_DIM, (g + 1) * GROUP_DIM)
                dm = dmixed[rs, cs]
                mixed_ref[rs, cs] = _mm_tn(wsc[g], dm)
                dws_ref[g] += _mm_nt(dm, f["vn"][rs, cs])
                dbst_ref[:, g:g + 1] += jnp.sum(dm, axis=1, keepdims=True)
        dvv = _unit_norm_bwd(mixed_ref[...], f["vn"], f["rstd_v"])
        dpj2_ref[:, :D_MODEL] = (dvv * f["dgelu_sv"]).astype(dpj2_ref.dtype)

    sq = (D_MODEL, D_MODEL)
    return pl.pallas_call(
        body, name="merge_bwd", grid=(t // TILE_M,),
        in_specs=[pl.BlockSpec((3, TILE_M, D_MODEL), lambda i: (0, i, 0))] + [_proj_tile(j) for j in (1, 2, 3)]
        + [_row_tile(D_MODEL)] + [_resident(ws.shape), _resident(bst.shape), _resident(sq), _resident(sq), _resident(sq)],
        out_specs=[_row_tile(SHARD_W)] * 3 + [_row_tile(D_MODEL)] * 3
        + [pl.BlockSpec(ws.shape, lambda i: (0, 0, 0)), pl.BlockSpec(bst.shape, lambda i: (0, 0))],
        out_shape=[jax.ShapeDtypeStruct((t, SHARD_W), MXU_DTYPE)] * 3 + [jax.ShapeDtypeStruct((t, D_MODEL), MXU_DTYPE)] * 3
        + [jax.ShapeDtypeStruct(ws.shape, F32), jax.ShapeDtypeStruct(bst.shape, F32)],
        scratch_shapes=[pltpu.VMEM((TILE_M, D_MODEL), F32)], compiler_params=_params(("arbitrary",)),
    )(wide, pj1, pj2, pj3, ret_raw, ws, bst, w_ro, w_so, w_o)


def _tail_call(pj, ret_raw, x2d, p2d, target, ws, bst, g_ple, g_final, w_ro, w_so, w_o, w_pg, w_pp):
    t = x2d.shape[0]
    tm = TILE_M

    def body(pj1_ref, pj2_ref, pj3_ref, rr_ref, x_ref, p_ref, t_ref, ws_ref, bst_ref, gp_ref, gf_ref,
             wro_ref, wso_ref, wo_ref, wpg_ref, wpp_ref,
             wide_ref, half_ref, loss_ref, dgp_ref, dgf_ref, mixed_ref):
        dx1_ref, a_ref, b_ref = (wide_ref.at[k] for k in range(3))
        ret_ref, sgu_ref, mg_ref, hp_ref, dz_ref, dpp_ref = (half_ref.at[k] for k in range(6))
        @pl.when(pl.program_id(0) == 0)
        def _():
            for acc in (loss_ref, dgp_ref, dgf_ref):
                acc[...] = jnp.zeros_like(acc)

        f = _branch_forward(pj1_ref[...], pj2_ref[...], rr_ref[...], _causal_ws(ws_ref), bst_ref[...], mixed_ref)
        ret = f["ret"].astype(MXU_DTYPE)
        sgu = f["sgu"].astype(MXU_DTYPE)
        ret_ref[...] = ret
        sgu_ref[...] = sgu
        a = jnp.dot(ret, wro_ref[...], preferred_element_type=F32)
        b = jnp.dot(sgu, wso_ref[...], preferred_element_type=F32)
        a_ref[...] = a
        b_ref[...] = b
        pj3 = pj3_ref[...]
        smr, sms = _sigmoid(pj3[:, :D_MODEL]), _sigmoid(pj3[:, D_MODEL:])
        merged = (smr * a + sms * b).astype(MXU_DTYPE)
        mg_ref[...] = merged
        x1v = x_ref[...] + jnp.dot(merged, wo_ref[...], preferred_element_type=F32)
        xn1, r1 = _rms(x1v)
        hp = (xn1 * gp_ref[...]).astype(MXU_DTYPE)
        hp_ref[...] = hp
        gate = _sigmoid(jnp.dot(hp, wpg_ref[...], preferred_element_type=F32))
        pp = jnp.dot(p_ref[...].astype(MXU_DTYPE), wpp_ref[...], preferred_element_type=F32)
        xn2, r2 = _rms(x1v + gate * pp)
        err = xn2 * gf_ref[...] - t_ref[...]
        loss_ref[...] += (0.5 / D_MODEL) * jnp.sum(jnp.sum(err * err, axis=1, keepdims=True), axis=0, keepdims=True)

        dy = err * (1.0 / D_MODEL)
        dgf_ref[...] += jnp.sum(dy * xn2, axis=0, keepdims=True)
        dx2 = _rms_bwd(dy * gf_ref[...], xn2, r2)
        dpp_ref[...] = (dx2 * gate).astype(MXU_DTYPE)
        dz = (dx2 * pp * gate * (1.0 - gate)).astype(MXU_DTYPE)
        dz_ref[...] = dz
        dhp = _mm_nt(dz, wpg_ref[...])
        dgp_ref[...] += jnp.sum(dhp * xn1, axis=0, keepdims=True)
        dx1_ref[...] = dx2 + _rms_bwd(dhp * gp_ref[...], xn1, r1)

    sq = (D_MODEL, D_MODEL)
    vec = _resident((1, D_MODEL))
    whole = lambda shape: pl.BlockSpec(shape, lambda i: (0,) * len(shape))
    return pl.pallas_call(
        body, name="tail_fwd", grid=(t // tm,),
        in_specs=[_proj_tile(j) for j in (1, 2, 3)]
        + [_row_tile(D_MODEL), _row_tile(D_MODEL), _row_tile(PLE_DIM), _row_tile(D_MODEL), _resident(ws.shape),
           _resident(bst.shape), vec, vec, _resident(sq), _resident(sq), _resident(sq), _resident(sq),
           _resident((PLE_DIM, D_MODEL))],
        out_specs=[pl.BlockSpec((3, tm, D_MODEL), lambda i: (0, i, 0)), pl.BlockSpec((6, tm, D_MODEL), lambda i: (0, i, 0)),
                   whole(LOSS_TILE), whole((1, D_MODEL)), whole((1, D_MODEL))],
        out_shape=[jax.ShapeDtypeStruct((3, t, D_MODEL), F32), jax.ShapeDtypeStruct((6, t, D_MODEL), MXU_DTYPE),
                   jax.ShapeDtypeStruct(LOSS_TILE, F32), jax.ShapeDtypeStruct((1, D_MODEL), F32),
           jax.ShapeDtypeStruct((1, D_MODEL), F32)],
        scratch_shapes=[pltpu.VMEM((TILE_M, D_MODEL), F32)], compiler_params=_params(("arbitrary",)),
    )(pj, pj, pj, ret_raw, x2d, p2d, target, ws, bst, g_ple, g_final, w_ro, w_so, w_o, w_pg, w_pp)


def _retention_bwd_call(pj0, drr, states, consts, n_seq, seq, side):
    cos_full, sin_signed, decay_in, zeta, xi = consts
    nb = seq // RET_TILE
    cpb = RET_TILE // CHUNK

    def body(pj_ref, do_ref, st_ref, cos_ref, sin_ref, d_ref, zeta_ref, xi_ref, dpj_ref, gstate):
        @pl.when(pl.program_id(1) == 0)
        def _():
            gstate[...] = jnp.zeros_like(gstate)

        for cc in reversed(range(cpb)):
            rows = slice(cc * CHUNK, (cc + 1) * CHUNK)
            cs, sn = cos_ref[rows, :], sin_ref[rows, :]
            for h in range(RET_HEADS):
                q, k, v = _qkv(pj_ref, rows, h)
                qt = (q * cs + _rot(q) * sn) * QK_SCALE
                kt = k * cs + _rot(k) * sn
                d_out = do_ref[rows, h * V_DIM:(h + 1) * V_DIM]
                prev = st_ref[cc, h]
                g = gstate[h]
                dec = d_ref[h]
                scores_d = _mm_nt(qt, kt) * dec
                dscores = _mm_nt(d_out, v) * dec
                kz = kt * zeta_ref[h]
                qx = qt * xi_ref[h]
                dv = _mm_tn(scores_d, d_out) + _mm(kz, g)
                dqt = (_mm(dscores, kt) + _mm_nt(d_out, prev) * xi_ref[h]) * QK_SCALE
                dkt = _mm_tn(dscores, qt) + _mm_nt(v, g) * zeta_ref[h]
                gstate[h] = _mm_tn(qx, d_out) + CHUNK_DECAY[h] * g
                dq = dqt * cs + _rot(dqt * sn)
                dk = dkt * cs + _rot(dkt * sn)
                dpj_ref[rows, h * QK_DIM:(h + 1) * QK_DIM] = dq.astype(dpj_ref.dtype)
                dpj_ref[rows, RET_HEADS * QK_DIM + h * QK_DIM:RET_HEADS * QK_DIM + (h + 1) * QK_DIM] = dk.astype(
                    dpj_ref.dtype)
                dpj_ref[rows, 2 * RET_HEADS * QK_DIM + h * V_DIM:2 * RET_HEADS * QK_DIM + (h + 1) * V_DIM] = dv.astype(
                    dpj_ref.dtype)

    row = lambda b, n: (b * nb + nb - 1 - n, 0)
    pos = lambda b, n: (nb - 1 - n, 0)
    return _compute_call(
        body, name="retention_bwd", grid=(n_seq, nb),
        in_specs=[pl.BlockSpec((None, RET_TILE, SHARD_W), lambda b, n: (0,) + row(b, n)),
                  pl.BlockSpec((RET_TILE, D_MODEL), row),
                  pl.BlockSpec((cpb, RET_HEADS, QK_DIM, V_DIM), lambda b, n: (b * nb + nb - 1 - n, 0, 0, 0)),
                  pl.BlockSpec((RET_TILE, QK_DIM), pos), pl.BlockSpec((RET_TILE, QK_DIM), pos),
                  _resident(decay_in.shape), _resident(zeta.shape), _resident(xi.shape)],
        out_specs=[pl.BlockSpec((RET_TILE, SHARD_W), row)],
        out_shape=[jax.ShapeDtypeStruct((n_seq * seq, SHARD_W), MXU_DTYPE)],
        scratch_shapes=[pltpu.VMEM((RET_HEADS, QK_DIM, V_DIM), F32)],
        operands=(pj0, drr, states, cos_full, sin_signed, decay_in, zeta, xi),
        semantics=("arbitrary", "arbitrary"), side=side)


def _dx_call(dpj, x2d, wide, g_mixer, wg_in, side):
    t = x2d.shape[0]

    def body(d0, d1, d2, d3, x_ref, dx1_ref, g_ref, w_ref, dx_ref, dg_ref):
        @pl.when(pl.program_id(0) == 0)
        def _():
            dg_ref[...] = jnp.zeros_like(dg_ref)

        dh = _mm_nt(d0[...], w_ref[0])
        for j, d_ref in enumerate((d1, d2, d3)):
            dh += _mm_nt(d_ref[...], w_ref[j + 1])
        xn, r = _rms(x_ref[...])
        dg_ref[...] += jnp.sum(dh * xn, axis=0, keepdims=True)
        dx_ref[...] = dx1_ref[...] + _rms_bwd(dh * g_ref[...], xn, r)

    return _compute_call(
        body, name="dx_bwd", grid=(t // DX_TILE,),
        in_specs=[_row_tile(SHARD_W, DX_TILE)] * N_CHIPS
        + [_row_tile(D_MODEL, DX_TILE), pl.BlockSpec((None, DX_TILE, D_MODEL), lambda i: (0, i, 0))]
        + [_resident((1, D_MODEL)), _resident(wg_in.shape)],
        out_specs=[_row_tile(D_MODEL, DX_TILE), pl.BlockSpec((1, D_MODEL), lambda i: (0, 0))],
        out_shape=[jax.ShapeDtypeStruct((t, D_MODEL), F32), jax.ShapeDtypeStruct((1, D_MODEL), F32)],
        operands=(*dpj, x2d, wide, g_mixer, wg_in), semantics=("arbitrary",), side=side)


def _wgrad_call(name, lhs, rhs, block_n, out_cols=None, block_t=1024, into=None, slot=0, n_slots=1, side=None):
    (lhs, lhs_at), (rhs, rhs_at) = [x if isinstance(x, tuple) else (x, None) for x in (lhs, rhs)]
    t, n = rhs.shape[-2:]
    k = lhs.shape[-1]
    block_t = min(block_t, t)

    def operand_spec(at, block, index):
        if at is None:
            return pl.BlockSpec(block, index)
        return pl.BlockSpec((None,) + block, lambda j, i: (at,) + index(j, i))

    steps = t // block_t
    out_cols = block_n if out_cols is None else out_cols
    per = block_n // out_cols
    first_block = slot * (n // block_n)

    def body(l_ref, r_ref, *rest):
        o_ref, acc = rest[-2:]
        @pl.when(pl.program_id(1) == 0)
        def _():
            acc[...] = jnp.zeros_like(acc)

        acc[...] += _mm_tn(l_ref[...], r_ref[...])

        @pl.when(pl.program_id(1) == steps - 1)
        def _():
            for s in range(per):
                o_ref[s] = acc[:, s * out_cols:(s + 1) * out_cols].astype(o_ref.dtype)

    res = _compute_call(
        body, name=name, grid=(n // block_n, steps),
        in_specs=[operand_spec(lhs_at, (block_t, k), lambda j, i: (i, 0)),
                  operand_spec(rhs_at, (block_t, block_n), lambda j, i: (i, j))] + ([] if into is None else [ANY]),
        out_specs=[pl.BlockSpec((per, k, out_cols), lambda j, i: (first_block + j, 0, 0))],
        out_shape=[jax.ShapeDtypeStruct((n_slots * (n // out_cols), k, out_cols), COMM_DTYPE)],
        scratch_shapes=[pltpu.VMEM((k, block_n), F32)], aliases={} if into is None else {2: 0},
        operands=(lhs, rhs) if into is None else (lhs, rhs, into), semantics=("arbitrary", "arbitrary"), side=side)
    return res[0] if side is None else (res[0][0], res[1])


def _wgrad_in_call(h, dpjs, block_t=1024):
    t = dpjs[0].shape[0]
    steps = t // block_t

    def body(l_ref, *rest):
        r_refs, o_ref, acc = rest[:N_CHIPS], rest[N_CHIPS], rest[N_CHIPS + 1]
        @pl.when(pl.program_id(1) == 0)
        def _():
            acc[...] = jnp.zeros_like(acc)

        for k in range(N_CHIPS):
            @pl.when(pl.program_id(0) == k)
            def _(k=k):
                acc[...] += _mm_tn(l_ref[...], r_refs[k][...])

        @pl.when(pl.program_id(1) == steps - 1)
        def _():
            o_ref[0] = acc[...].astype(o_ref.dtype)

    def turn(k):
        return lambda j, i: (jnp.where(j == k, i, jnp.where(j < k, 0, steps - 1)), 0)

    return _compute_call(
        body, name="wgrad_in", grid=(N_CHIPS, steps),
        in_specs=[pl.BlockSpec((block_t, D_MODEL), lambda j, i: (i, 0))]
        + [pl.BlockSpec((block_t, SHARD_W), turn(k)) for k in range(N_CHIPS)],
        out_specs=[pl.BlockSpec((1, D_MODEL, SHARD_W), lambda j, i: (j, 0, 0))],
        out_shape=[jax.ShapeDtypeStruct((N_CHIPS, D_MODEL, SHARD_W), COMM_DTYPE)],
        scratch_shapes=[pltpu.VMEM((D_MODEL, SHARD_W), F32)], operands=(h,) + tuple(dpjs),
        semantics=("arbitrary", "arbitrary"))[0]


def _position():
    return lax.axis_index("x"), lax.axis_index("y"), lax.axis_index("c")


def _position_array():
    x, y, c = _position()
    return jnp.stack([2 * x + y, c]).astype(jnp.int32)


def _other_chip(x, y, k):
    return (1 - x if k & 2 else x), (1 - y if k & 1 else y)


def _plan_side(operands, out_shapes, plan, n_remote, aliases=None):
    def copies(ins, outs, send_sem, recv_sem, base=0):
        remote = plan(ins, outs)
        assert len(remote) == n_remote
        return [pltpu.make_async_remote_copy(src_ref=src, dst_ref=dst, send_sem=send_sem.at[base + i],
                                             recv_sem=recv_sem.at[base + i], device_id=dev, device_id_type=MESH)
                for i, (src, dst, dev) in enumerate(remote)]

    def start(*a):
        for cp in copies(*a):
            cp.start()

    def finish(*a):
        for cp in copies(*a):
            cp.wait()

    return _Side(operands, out_shapes, n_remote, start, finish, aliases)


def _place_cast_call(name, shards, pos):
    n = len(shards)
    rows, width = shards[0].shape
    block_rows = min(256, rows)

    def body(pos_ref, *refs):
        for w_ref, o_ref in zip(refs[:n], refs[n:]):
            o_ref[...] = w_ref[...].astype(o_ref.dtype)

    return pl.pallas_call(
        body, name=name,
        grid_spec=pltpu.PrefetchScalarGridSpec(
            num_scalar_prefetch=1, grid=(rows // block_rows,),
            in_specs=[pl.BlockSpec((block_rows, width), lambda i, pos: (i, 0))] * n,
            out_specs=[pl.BlockSpec((None, block_rows, width), lambda i, pos: (pos[0], i, 0))] * n),
        out_shape=[jax.ShapeDtypeStruct((N_CHIPS, rows, width), MXU_DTYPE)] * n,
        compiler_params=_params(("arbitrary",)),
    )(pos, *shards)


def _gather_side(placed):
    n = len(placed)

    def copies(kind, bufs, send_sem, recv_sem, base):
        x, y, c = _position()
        me = 2 * x + y
        made = []
        for i in range(n):
            hr = placed[i].shape[1] // 2
            for k in (1, 2, 3):
                px, py = _other_chip(x, y, k)
                chip, core, slot, dev = [(me, c, k - 1, (px, py, c)), (2 * px + py, c, 2 + k, (x, y, 1 - c)),
                                         (2 * px + py, 1 - c, 2 + k, (x, y, 1 - c))][kind]
                piece = bufs[i].at[chip, pl.ds(core * hr, hr)]
                made.append(pltpu.make_async_remote_copy(
                    src_ref=piece, dst_ref=piece, send_sem=send_sem.at[base + 6 * i + slot],
                    recv_sem=recv_sem.at[base + 6 * i + slot], device_id=dev, device_id_type=MESH))
        return made

    def start(ins, outs, send_sem, recv_sem, base=0):
        for cp in copies(0, outs, send_sem, recv_sem, base):
            cp.start()

    def finish(ins, outs, send_sem, recv_sem, base=0):
        first, onward = copies(0, outs, send_sem, recv_sem, base), copies(1, outs, send_sem, recv_sem, base)
        for landed, cp in zip(first, onward):
            landed.wait_recv()
            cp.start()
        for cp in copies(2, outs, send_sem, recv_sem, base):
            cp.wait_recv()
        for cp in first + onward:
            cp.wait_send()

    return _Side(placed, [jax.ShapeDtypeStruct(a.shape, a.dtype) for a in placed], 6 * n, start, finish,
                 {i: i for i in range(n)})


def _same_shape(shapes):
    found = {}
    for i, shape in enumerate(shapes):
        found.setdefault(shape, []).append(i)
    return found


def _pair_stage(tag, grads):
    n = len(grads)
    shapes = [(g.shape[0] // N_CHIPS, g.shape[1]) for g in grads]
    kinds = list(_same_shape(shapes))
    whole = {(r, w): N_CHIPS * (r // 2) <= 2 * PAIR_CHUNK_ROWS for r, w in kinds}
    chunk = {(r, w): N_CHIPS * (r // 2) if whole[(r, w)] else min(r // 2, PAIR_CHUNK_ROWS) for r, w in kinds}
    work = [(i, j, q) for i, (r, w) in enumerate(shapes) for j in ((None,) if whole[(r, w)] else range(N_CHIPS))
            for q in range(1 if whole[(r, w)] else (r // 2) // chunk[(r, w)])]

    def body(*refs):
        ins, sums, landed = refs[:n], refs[n:2 * n], refs[2 * n:3 * n]
        bufs, (load_sem, store_sem, send_sem, recv_sem) = refs[3 * n:-4], refs[-4:]
        x, y, c = _position()

        def piece(i, j):
            r = shapes[i][0]
            return pltpu.make_async_remote_copy(
                src_ref=ins[i].at[pl.ds(j * r + (1 - c) * (r // 2), r // 2)],
                dst_ref=landed[i].at[pl.ds(j * (r // 2), r // 2)], send_sem=send_sem.at[N_CHIPS * i + j],
                recv_sem=recv_sem.at[N_CHIPS * i + j], device_id=(x, y, 1 - c), device_id_type=MESH)

        def buffers(i, slot):
            own, got, out = bufs[3 * kinds.index(shapes[i]):3 * kinds.index(shapes[i]) + 3]
            return own.at[slot], got.at[slot], out.at[slot]

        def loads(step):
            i, j, q = work[step]
            r, ch = shapes[i][0], chunk[shapes[i]]
            own, got, _ = buffers(i, step % 2)
            if j is None:
                return [pltpu.make_async_copy(ins[i].at[pl.ds(jj * r + c * (r // 2), r // 2)],
                                              own.at[pl.ds(jj * (r // 2), r // 2)], load_sem.at[step % 2, jj])
                        for jj in range(N_CHIPS)] + [pltpu.make_async_copy(landed[i], got, load_sem.at[step % 2, N_CHIPS])]
            return [pltpu.make_async_copy(ins[i].at[pl.ds(j * r + c * (r // 2) + q * ch, ch)], own,
                                          load_sem.at[step % 2, 0]),
                    pltpu.make_async_copy(landed[i].at[pl.ds(j * (r // 2) + q * ch, ch)], got,
                                          load_sem.at[step % 2, 1])]

        def store(step):
            i, j, q = work[step]
            ch = chunk[shapes[i]]
            rows = sums[i] if j is None else sums[i].at[pl.ds(j * (shapes[i][0] // 2) + q * ch, ch)]
            return pltpu.make_async_copy(buffers(i, step % 2)[2], rows, store_sem.at[step % 2])

        def begin(step):
            i, j, q = work[step]
            for jj in (range(N_CHIPS) if j is None else [j] if q == 0 else []):
                piece(i, jj).wait_recv()
            for cp in loads(step):
                cp.start()

        for i in range(n):
            for j in range(N_CHIPS):
                piece(i, j).start()
        begin(0)
        for step in range(len(work)):
            if step + 1 < len(work):
                begin(step + 1)
            for cp in loads(step):
                cp.wait()
            if step >= 2:
                store(step - 2).wait()
            own, got, out = buffers(work[step][0], step % 2)
            out[...] = (own[...].astype(F32) + got[...].astype(F32)).astype(out.dtype)
            store(step).start()
        for step in range(max(len(work) - 2, 0), len(work)):
            store(step).wait()
        for i in range(n):
            for j in range(N_CHIPS):
                piece(i, j).wait_send()

    half = [jax.ShapeDtypeStruct((N_CHIPS * (r // 2), w), COMM_DTYPE) for r, w in shapes]
    res = pl.pallas_call(
        body, name=f"grad_pair_sum_{tag}", in_specs=[ANY] * n, out_specs=[ANY] * (2 * n), out_shape=half + half,
        scratch_shapes=[pltpu.VMEM((2, chunk[k], k[1]), COMM_DTYPE) for k in kinds for _ in range(3)]
        + [pltpu.SemaphoreType.DMA((2, N_CHIPS + 1)), pltpu.SemaphoreType.DMA((2,)),
           pltpu.SemaphoreType.DMA((N_CHIPS * n,)),
           pltpu.SemaphoreType.DMA((N_CHIPS * n,))],
        compiler_params=pltpu.CompilerParams(has_side_effects=True, vmem_limit_bytes=VMEM_LIMIT),
    )(*grads)
    return list(res[:n])


def _chip_side(pair_sums):
    halves = [(p.shape[0] // N_CHIPS, p.shape[1]) for p in pair_sums]

    def plan(ins, outs):
        x, y, c = _position()
        remote = []
        for i, (hr, _) in enumerate(halves):
            for k in (1, 2, 3):
                px, py = _other_chip(x, y, k)
                remote.append((ins[i].at[pl.ds((2 * px + py) * hr, hr)], outs[i].at[pl.ds((k - 1) * hr, hr)],
                               (px, py, c)))
        return remote

    return _plan_side(pair_sums, [jax.ShapeDtypeStruct((3 * hr, w), COMM_DTYPE) for hr, w in halves], plan,
                      3 * len(pair_sums))


def _finish_stage(tag, pair_sums, chip, extra):
    n = len(pair_sums)
    halves = [(p.shape[0] // N_CHIPS, p.shape[1]) for p in pair_sums]
    kinds = list(_same_shape(halves))
    chunk = {(hr, w): min(hr, PAIR_CHUNK_ROWS) for hr, w in kinds}
    work = [(i, q) for i, (hr, w) in enumerate(halves) for q in range(hr // chunk[(hr, w)])]
    e_in, e_out = len(extra.operands), len(extra.out_shapes)

    def body(*refs):
        sums, got, refs = refs[:n], refs[n:2 * n], refs[2 * n:]
        extra_ins, refs = refs[:e_in], refs[e_in:]
        full, refs = refs[:n], refs[n:]
        extra_outs, refs = refs[:e_out], refs[e_out:]
        bufs, (load_sem, store_sem, send_sem, recv_sem, extra_send, extra_recv) = refs[:-6], refs[-6:]
        x, y, c = _position()
        me = 2 * x + y
        extra.start(extra_ins, extra_outs, extra_send, extra_recv)

        def buffers(i, slot):
            at = 5 * kinds.index(halves[i])
            return [b_.at[slot] for b_ in bufs[at:at + 5]]

        def loads(step):
            i, q = work[step]
            hr, ch = halves[i][0], chunk[halves[i]]
            into = buffers(i, step % 2)
            made = [pltpu.make_async_copy(sums[i].at[pl.ds(me * hr + q * ch, ch)], into[0], load_sem.at[step % 2, 0])]
            made += [pltpu.make_async_copy(got[i].at[pl.ds(k * hr + q * ch, ch)], into[1 + k], load_sem.at[step % 2, 1 + k])
                     for k in range(3)]
            return made

        def stores(step):
            i, q = work[step]
            hr, ch = halves[i][0], chunk[halves[i]]
            result = buffers(i, step % 2)[4]
            rows = full[i].at[pl.ds(c * hr + q * ch, ch)]
            return (pltpu.make_async_copy(result, rows, store_sem.at[step % 2]),
                    pltpu.make_async_remote_copy(src_ref=result, dst_ref=rows, send_sem=send_sem.at[step],
                                                 recv_sem=recv_sem.at[step], device_id=(x, y, 1 - c),
                                                 device_id_type=MESH))

        for cp in loads(0):
            cp.start()
        for step in range(len(work)):
            if step + 1 < len(work):
                for cp in loads(step + 1):
                    cp.start()
            for cp in loads(step):
                cp.wait()
            if step >= 2:
                stores(step - 2)[0].wait()
                stores(step - 2)[1].wait_send()
            mine, k1, k2, k3, result = buffers(work[step][0], step % 2)
            result[...] = ((mine[...].astype(F32) + k1[...].astype(F32)) + k2[...].astype(F32)) + k3[...].astype(F32)
            for cp in stores(step):
                cp.start()
        for step in range(max(len(work) - 2, 0), len(work)):
            stores(step)[0].wait()
            stores(step)[1].wait_send()
        for step in range(len(work)):
            stores(step)[1].wait_recv()
        extra.finish(extra_ins, extra_outs, extra_send, extra_recv)

    scratch = []
    for k in kinds:
        scratch += [pltpu.VMEM((2, chunk[k], k[1]), COMM_DTYPE)] * 4 + [pltpu.VMEM((2, chunk[k], k[1]), F32)]
    res = pl.pallas_call(
        body, name=f"grad_finish_{tag}", in_specs=[ANY] * (2 * n + e_in), out_specs=[ANY] * (n + e_out),
        out_shape=[jax.ShapeDtypeStruct((2 * hr, w), F32) for hr, w in halves] + extra.out_shapes,
        scratch_shapes=scratch + [pltpu.SemaphoreType.DMA((2, 4)), pltpu.SemaphoreType.DMA((2,)),
                                  pltpu.SemaphoreType.DMA((len(work),)), pltpu.SemaphoreType.DMA((len(work),)),
                                  pltpu.SemaphoreType.DMA((extra.n_sems,)), pltpu.SemaphoreType.DMA((extra.n_sems,))],
        compiler_params=pltpu.CompilerParams(has_side_effects=True, vmem_limit_bytes=VMEM_LIMIT),
    )(*pair_sums, *chip, *extra.operands)
    return list(res[:n]), list(res[n:])


def _small_gather_side(parts):
    n = len(parts)

    def copies(ins, outs, send_sem, recv_sem, base):
        x, y, c = _position()
        made = []
        for i in range(n):
            mine = outs[i].at[4 * x + 2 * y + c]
            made.append(pltpu.make_async_copy(ins[i], mine, send_sem.at[base + 8 * i + 7]))
            for d in range(1, 8):
                px, py = _other_chip(x, y, d >> 1)
                made.append(pltpu.make_async_remote_copy(
                    src_ref=ins[i], dst_ref=mine, send_sem=send_sem.at[base + 8 * i + d - 1],
                    recv_sem=recv_sem.at[base + 8 * i + d - 1], device_id=(px, py, 1 - c if d & 1 else c),
                    device_id_type=MESH))
        return made

    def start(ins, outs, send_sem, recv_sem, base=0):
        for cp in copies(ins, outs, send_sem, recv_sem, base):
            cp.start()

    def finish(ins, outs, send_sem, recv_sem, base=0):
        for cp in copies(ins, outs, send_sem, recv_sem, base):
            cp.wait()

    return _Side(parts, [jax.ShapeDtypeStruct((8,) + a_.shape, F32) for a_ in parts], 8 * n, start, finish)


def _adamw(w, g, m, v):
    m = ADAM_B1 * m + (1.0 - ADAM_B1) * g
    v = ADAM_B2 * v + (1.0 - ADAM_B2) * (g * g)
    m_hat = m / (1.0 - ADAM_B1 ** ADAM_STEP)
    v_hat = v / (1.0 - ADAM_B2 ** ADAM_STEP)
    delta = -ADAM_LR * (m_hat / (jnp.sqrt(v_hat) + ADAM_EPS) + ADAM_WD * w)
    return delta, m, v


def _adamw_call(name, ws, gs, ms, vs):
    n = len(ws)
    rows, width = ws[0].shape
    block_rows = min(256 // n, rows)

    def body(*refs):
        for i in range(n):
            w_ref, g_ref, m_ref, v_ref = (refs[j * n + i] for j in range(4))
            d_out, m_out, v_out = (refs[(4 + j) * n + i] for j in range(3))
            d_out[...], m_out[...], v_out[...] = _adamw(w_ref[...], g_ref[...], m_ref[...], v_ref[...])

    spec = pl.BlockSpec((block_rows, width), lambda i: (i, 0))
    res = pl.pallas_call(
        body, name=name, grid=(rows // block_rows,), in_specs=[spec] * (4 * n), out_specs=[spec] * (3 * n),
        out_shape=[jax.ShapeDtypeStruct(ws[0].shape, F32)] * (3 * n),
        compiler_params=_params(("arbitrary",)),
    )(*ws, *gs, *ms, *vs)
    return [(res[i], res[n + i], res[2 * n + i]) for i in range(n)]


def _small_adamw_call(gathered, weights, moments_m, moments_v):
    n = len(weights)

    def body(*refs):
        all_refs, refs = refs[:n + 1], refs[n + 1:]
        w_refs, m_refs, v_refs, outs = refs[:n], refs[n:2 * n], refs[2 * n:3 * n], refs[3 * n:]

        def total(ref):
            acc = ref[0]
            for d in range(1, 8):
                acc = acc + ref[d]
            return acc

        outs[0][...] = total(all_refs[n])
        for i in range(n):
            g = total(all_refs[i])
            if i == 0:
                row = lax.broadcasted_iota(jnp.int32, g.shape, 0)
                col = lax.broadcasted_iota(jnp.int32, g.shape, 1)
                g = jnp.where((row % CHUNK) >= col, g, 0.0)
            g_out, d_out, m_out, v_out = outs[1 + 4 * i:5 + 4 * i]
            g_out[...] = g
            d_out[...], m_out[...], v_out[...] = _adamw(w_refs[i][...], g, m_refs[i][...], v_refs[i][...])

    out_shape = [jax.ShapeDtypeStruct(LOSS_TILE, F32)]
    for w in weights:
        out_shape += [jax.ShapeDtypeStruct(w.shape, F32)] * 4
    res = pl.pallas_call(
        body, name="small_adamw", out_shape=out_shape,
        compiler_params=pltpu.CompilerParams(vmem_limit_bytes=VMEM_LIMIT),
    )(*gathered, *weights, *moments_m, *moments_v)
    return res[0], [res[1 + 4 * i:5 + 4 * i] for i in range(n)]


def kernel(x, p, w_in, w_ret_out, w_sgu_out, w_out, sgu_ws, sgu_bs, w_ple_gate, w_ple_proj, g_mixer, g_ple, g_final, loss_target, m_w_in, m_w_ret_out, m_w_sgu_out, m_w_out, m_sgu_ws, m_sgu_bs, m_w_ple_gate, m_w_ple_proj, m_g_mixer, m_g_ple, m_g_final, v_w_in, v_w_ret_out, v_w_sgu_out, v_w_out, v_sgu_ws, v_sgu_bs, v_w_ple_gate, v_w_ple_proj, v_g_mixer, v_g_ple, v_g_final):
    n_seq, seq, _ = x.shape
    t = n_seq * seq
    x2d = x.reshape(t, D_MODEL)
    p2d = p.reshape(t, PLE_DIM)
    target = loss_target.reshape(t, D_MODEL)
    big = [w_in[0], w_ret_out[0], w_sgu_out[0], w_out[0], w_ple_gate[0], w_ple_proj[0]]
    big_m = [m_w_in[0], m_w_ret_out[0], m_w_sgu_out[0], m_w_out[0], m_w_ple_gate[0], m_w_ple_proj[0]]
    big_v = [v_w_in[0], v_w_ret_out[0], v_w_sgu_out[0], v_w_out[0], v_w_ple_gate[0], v_w_ple_proj[0]]

    pos = _position_array()
    placed = (_place_cast_call("place_w_in", big[:1], pos) + _place_cast_call("place_square_weights", big[1:5], pos)
              + _place_cast_call("place_w_ple_proj", big[5:], pos))
    ws = sgu_ws[0]
    bst = sgu_bs[0].T
    consts = _retention_consts(seq)

    (h, pj, wg_in), gathered = _proj_call(x2d, g_mixer, placed[0], pos, _gather_side(placed[1:]))
    pj0 = pj1 = pj2 = pj3 = pj
    w_ro, w_so, w_o, w_pg = (w.reshape(D_MODEL, D_MODEL) for w in gathered[:4])
    w_pp = gathered[4].transpose(1, 0, 2).reshape(PLE_DIM, D_MODEL)
    ret_raw, states = _retention_fwd_call(pj0, consts, n_seq, seq)
    wide, half, loss, dg_ple, dg_final = _tail_call(
        pj, ret_raw, x2d, p2d, target, ws, bst, g_ple, g_final.reshape(1, D_MODEL), w_ro, w_so, w_o, w_pg, w_pp)
    ret, sgu, merged, hp, dz, dpp = ((half, k) for k in range(6))
    dpj1, dpj2, dpj3, drr, da, db, dws, dbst = _merge_bwd_call(wide, pj1, pj2, pj3, ret_raw, ws, bst, w_ro, w_so, w_o)

    small_shapes = [(SGU_GROUPS * CHUNK, CHUNK), (SGU_GROUPS, CHUNK), (1, D_MODEL), (1, D_MODEL), (1, D_MODEL)]
    as_small = lambda arrays: [None if a_ is None else a_.reshape(s_) for a_, s_ in zip(arrays, small_shapes)]
    early = as_small([dws, dbst.T, None, dg_ple, dg_final])
    rows_of = lambda g: g.reshape(g.shape[0] * g.shape[1], g.shape[2])
    tail_grads = [
        rows_of(_wgrad_call("wgrad_ret_out", ret, da, D_MODEL)),
        rows_of(_wgrad_call("wgrad_sgu_out", sgu, db, D_MODEL)),
        rows_of(_wgrad_call("wgrad_out", merged, (wide, 0), D_MODEL)),
        rows_of(_wgrad_call("wgrad_ple_gate", hp, dz, D_MODEL)),
        rows_of(_wgrad_call("wgrad_ple_proj", p2d, dpp, D_MODEL, out_cols=PLE_DIM)),
    ]
    tail_sums = _pair_stage("tail", tail_grads)
    (dpj0,), carried = _retention_bwd_call(
        pj0, drr, states, consts, n_seq, seq,
        _join_sides(_chip_side(tail_sums), _small_gather_side([early[0], early[1], early[3], early[4], loss])))
    tail_chip, early_all = carried[:len(tail_sums)], carried[len(tail_sums):]
    in_grad = _wgrad_in_call(h, (dpj0, dpj1, dpj2, dpj3))
    in_sums = _pair_stage("in", [rows_of(in_grad)])
    (dx, dg_mixer), in_chip = _dx_call((dpj0, dpj1, dpj2, dpj3), x2d, wide, g_mixer, wg_in, _chip_side(in_sums))
    g_big, (mixer_all,) = _finish_stage("all", in_sums + tail_sums, list(in_chip) + list(tail_chip),
                                        _small_gather_side([dg_mixer]))
    upd = [None] * len(big)
    for _, which in _same_shape([w.shape for w in big]).items():
        pick = lambda arrays: [arrays[i] for i in which]
        for i, triple in zip(which, _adamw_call(f"adamw_{which[0]}", pick(big), pick(g_big), pick(big_m), pick(big_v))):
            upd[i] = triple

    small_g = [early_all[0], early_all[1], mixer_all, early_all[2], early_all[3], early_all[4]]
    total, small = _small_adamw_call(small_g, as_small([sgu_ws, sgu_bs, g_mixer, g_ple, g_final]),
                                     as_small([m_sgu_ws, m_sgu_bs, m_g_mixer, m_g_ple, m_g_final]),
                                     as_small([v_sgu_ws, v_sgu_bs, v_g_mixer, v_g_ple, v_g_final]))
    out_small_shapes = [sgu_ws.shape, sgu_bs.shape, g_mixer.shape, g_ple.shape, g_final.shape]

    def ordered(big_list, kind):
        w_in_, w_ro_, w_so_, w_o_, w_pg_, w_pp_ = [b_[None] for b_ in big_list]
        s_ws, s_bs, s_gm, s_gp, s_gf = [small[i][kind].reshape(s) for i, s in enumerate(out_small_shapes)]
        return [w_in_, w_ro_, w_so_, w_o_, s_ws, s_bs, w_pg_, w_pp_, s_gm, s_gp, s_gf]

    out = [total[0, 0], dx.reshape(x.shape)]
    out += ordered(g_big, 0)
    out += ordered([u[0] for u in upd], 1)
    out += ordered([u[1] for u in upd], 2)
    out += ordered([u[2] for u in upd], 3)
    return tuple(out)
```

```python
import functools
import math

import numpy as np
import jax
import jax.numpy as jnp
from jax import lax
from jax.experimental import pallas as pl
from jax.experimental.pallas import tpu as pltpu

F32 = jnp.float32
MXU_DTYPE = jnp.bfloat16
COMM_DTYPE = jnp.bfloat16

D_MODEL = 1024
RET_HEADS = 4
QK_DIM = 128
V_DIM = 256
CHUNK = 128
SGU_GROUPS = 4
GROUP_DIM = 256
PLE_DIM = 256
N_CHIPS = 4
SHARD_W = 2048
ROPE_BASE = 10000.0
NORM_EPS = 1e-6
GN_EPS = 1e-5
QK_SCALE = QK_DIM ** -0.5
SQRT_HALF = math.sqrt(0.5)
INV_SQRT_2PI = 1.0 / math.sqrt(2.0 * math.pi)

ADAM_LR = 0.001
ADAM_B1 = 0.9
ADAM_B2 = 0.999
ADAM_EPS = 1e-08
ADAM_WD = 0.01
ADAM_STEP = 10

TILE_M = 256
LOSS_TILE = (8, 128)
LOAD_PARTS = 4
PAIR_CHUNK_ROWS = 256
PROJ_TILE = 512
RET_TILE = 512
DX_TILE = 512
VMEM_LIMIT = 56 * 1024 * 1024
MESH = pl.DeviceIdType.MESH
ANY = pl.BlockSpec(memory_space=pl.ANY)

CHUNK_DECAY = tuple(
    float(np.exp(np.float32(CHUNK) * np.log(np.float32(1.0 - 2.0 ** (-5.0 - h))))) for h in range(RET_HEADS))


def _mm(a, b):
    return jnp.dot(a.astype(MXU_DTYPE), b.astype(MXU_DTYPE), preferred_element_type=F32)


def _mm_nt(a, b):
    return lax.dot_general(a.astype(MXU_DTYPE), b.astype(MXU_DTYPE), (((1,), (1,)), ((), ())),
                           preferred_element_type=F32)


def _mm_tn(a, b):
    return lax.dot_general(a.astype(MXU_DTYPE), b.astype(MXU_DTYPE), (((0,), (0,)), ((), ())),
                           preferred_element_type=F32)


def _mean(x):
    return jnp.mean(x, axis=-1, keepdims=True)


def _sigmoid(x):
    return jax.nn.sigmoid(x)


def _silu_and_grad(x):
    s = _sigmoid(x)
    silu = x * s
    return silu, s + silu * (1.0 - s)


def _gelu_and_grad(x):
    cdf = 0.5 + 0.5 * lax.erf(x * SQRT_HALF)
    return x * cdf, cdf + x * (jnp.exp(x * x * -0.5) * INV_SQRT_2PI)


def _unit_norm(x, eps):
    xc = x - _mean(x)
    rstd = lax.rsqrt(_mean(xc * xc) + eps)
    return xc * rstd, rstd


def _unit_norm_bwd(dn, n, rstd):
    return rstd * (dn - _mean(dn) - n * _mean(dn * n))


def _rms(x):
    r = lax.rsqrt(_mean(x * x) + NORM_EPS)
    return x * r, r


def _rms_bwd(dxn, xn, r):
    return r * (dxn - xn * _mean(dxn * xn))


def _rot(x):
    return pltpu.roll(x, QK_DIM // 2, 1)


def _params(semantics, **kw):
    return pltpu.CompilerParams(dimension_semantics=semantics, vmem_limit_bytes=VMEM_LIMIT, **kw)


def _row_tile(width, tm=TILE_M):
    return pl.BlockSpec((tm, width), lambda i: (i, 0))


def _proj_tile(j):
    return pl.BlockSpec((None, TILE_M, SHARD_W), lambda i: (j, i, 0))


def _resident(shape):
    nd = len(shape)
    return pl.BlockSpec(shape, lambda *_: (0,) * nd, pipeline_mode=pl.Buffered(1))


def _causal_ws(ws_ref):
    row = lax.broadcasted_iota(jnp.int32, (CHUNK, CHUNK), 0)
    col = lax.broadcasted_iota(jnp.int32, (CHUNK, CHUNK), 1)
    return [jnp.where(row >= col, ws_ref[g], 0.0).astype(MXU_DTYPE) for g in range(SGU_GROUPS)]


def _heads(x, width):
    return [x[:, h * width:(h + 1) * width] for h in range(x.shape[1] // width)]


def _branch_forward(pj1, pj2, ret_raw, wsc, bst, mixed_ref):
    rg, su = pj1[:, :D_MODEL], pj1[:, D_MODEL:]
    sv, sg = pj2[:, :D_MODEL], pj2[:, D_MODEL:]
    rn_parts, rstd_parts = zip(*[_unit_norm(r, GN_EPS) for r in _heads(ret_raw, V_DIM)])
    rn = jnp.concatenate(rn_parts, axis=1)
    sil_rg, dsil_rg = _silu_and_grad(rg)
    ret = rn * sil_rg
    u, du = _gelu_and_grad(su)
    gelu_sv, dgelu_sv = _gelu_and_grad(sv)
    vn, rstd_v = _unit_norm(gelu_sv, GN_EPS)
    for cc in range(pj1.shape[0] // CHUNK):
        for g in range(SGU_GROUPS):
            rs, cs = slice(cc * CHUNK, (cc + 1) * CHUNK), slice(g * GROUP_DIM, (g + 1) * GROUP_DIM)
            mixed_ref[rs, cs] = _mm(wsc[g], vn[rs, cs]) + bst[:, g:g + 1]
    mixed = mixed_ref[...]
    sil_sg, dsil_sg = _silu_and_grad(sg)
    sgu = u * mixed * sil_sg
    return dict(rn=rn, rstd_r=rstd_parts, sil_rg=sil_rg, dsil_rg=dsil_rg, ret=ret, u=u, du=du, dgelu_sv=dgelu_sv,
                vn=vn, rstd_v=rstd_v, mixed=mixed, sil_sg=sil_sg, dsil_sg=dsil_sg, sgu=sgu)


class _Side:
    def __init__(self, operands, out_shapes, n_sems, start, finish, aliases=None):
        self.operands, self.out_shapes, self.n_sems = list(operands), list(out_shapes), n_sems
        self.start, self.finish, self.aliases = start, finish, dict(aliases or {})


def _join_sides(*sides):
    spans, a, b, s = [], 0, 0, 0
    for side in sides:
        spans.append((a, b, s))
        a, b, s = a + len(side.operands), b + len(side.out_shapes), s + side.n_sems

    def run(which):
        def go(ins, outs, send_sem, recv_sem, base=0):
            for side, (a0, b0, s0) in zip(sides, spans):
                getattr(side, which)(ins[a0:a0 + len(side.operands)], outs[b0:b0 + len(side.out_shapes)],
                                     send_sem, recv_sem, base + s0)
        return go

    aliases = {a0 + i: b0 + o for side, (a0, b0, _) in zip(sides, spans) for i, o in side.aliases.items()}
    return _Side([x for side in sides for x in side.operands], [x for side in sides for x in side.out_shapes], s,
                 run("start"), run("finish"), aliases)


def _compute_call(body, *, name, grid, in_specs, out_specs, out_shape, operands, semantics, scratch_shapes=(),
                  side=None, prefetch=None, aliases=None, side_start=None):
    n_pre = 0 if prefetch is None else 1
    pre = () if prefetch is None else (prefetch,)

    def spec(in_specs, out_specs, scratch):
        return pltpu.PrefetchScalarGridSpec(num_scalar_prefetch=n_pre, grid=grid, in_specs=in_specs,
                                            out_specs=out_specs, scratch_shapes=scratch)

    if side is None:
        return pl.pallas_call(body, name=name, grid_spec=spec(in_specs, out_specs, list(scratch_shapes)),
                              out_shape=out_shape,
                              input_output_aliases={n_pre + a: b for a, b in (aliases or {}).items()},
                              compiler_params=_params(semantics))(*pre, *operands)
    n_in, n_out, s_in, s_out = len(operands), len(out_shape), len(side.operands), len(side.out_shapes)

    def carrier(*refs):
        pre_refs, refs = refs[:n_pre], refs[n_pre:]
        ins, refs = refs[:n_in], refs[n_in:]
        side_ins, refs = refs[:s_in], refs[s_in:]
        outs, refs = refs[:n_out], refs[n_out:]
        side_outs, refs = refs[:s_out], refs[s_out:]
        scratch, (send_sem, recv_sem) = refs[:-2], refs[-2:]
        ids = [pl.program_id(a) for a in range(len(grid))]
        at = (0,) * len(grid) if side_start is None else side_start
        first = functools.reduce(jnp.logical_and, [i == a for i, a in zip(ids, at)])
        last = functools.reduce(jnp.logical_and, [i == g - 1 for i, g in zip(ids, grid)])

        @pl.when(first)
        def _():
            side.start(side_ins, side_outs, send_sem, recv_sem)

        body(*pre_refs, *ins, *outs, *scratch)

        @pl.when(last)
        def _():
            side.finish(side_ins, side_outs, send_sem, recv_sem)

    all_aliases = {n_pre + a: b for a, b in (aliases or {}).items()}
    all_aliases.update({n_pre + n_in + a: n_out + b for a, b in side.aliases.items()})
    res = pl.pallas_call(
        carrier, name=name,
        grid_spec=spec(list(in_specs) + [ANY] * s_in, list(out_specs) + [ANY] * s_out,
                       list(scratch_shapes) + [pltpu.SemaphoreType.DMA((side.n_sems,))] * 2),
        out_shape=list(out_shape) + side.out_shapes, input_output_aliases=all_aliases,
        compiler_params=_params(semantics, has_side_effects=True),
    )(*pre, *operands, *side.operands)
    return res[:n_out], res[n_out:]


def _proj_call(x2d, g_mixer, placed_in, pos, side):
    t = x2d.shape[0]
    nt = t // PROJ_TILE
    hr = placed_in.shape[1] // 2

    def body(pos_ref, x_ref, g_ref, win_ref, h_ref, pj_ref, w_ref, w_vmem, h_all, loc_sem, send_sem, recv_sem):
        k, i = pl.program_id(0), pl.program_id(1)
        x, y, c = _position()
        me = 2 * x + y

        def copy(slot, chip, core, dev):
            piece = w_ref.at[chip, pl.ds(core * hr, hr)]
            return pltpu.make_async_remote_copy(src_ref=piece, dst_ref=piece, send_sem=send_sem.at[slot],
                                                recv_sem=recv_sem.at[slot], device_id=dev, device_id_type=MESH)

        def first_hop(kk):
            px, py = _other_chip(x, y, kk)
            return copy(kk - 1, me, c, (px, py, c))

        def relay():
            source = jnp.bitwise_xor(me, 2 - c)
            return copy(2, source, c, (jnp.bitwise_xor(x, c), jnp.bitwise_xor(y, 1 - c), c))

        def onward(kk, core):
            px, py = _other_chip(x, y, kk)
            return copy(2 + kk, 2 * px + py, core, (x, y, 1 - c))

        def loads(kk):
            px, py = _other_chip(x, y, kk)
            rows = w_vmem.shape[1] // LOAD_PARTS
            return [pltpu.make_async_copy(w_ref.at[2 * px + py, pl.ds(r * rows, rows)],
                                          w_vmem.at[kk % 2, pl.ds(r * rows, rows)], loc_sem.at[r])
                    for r in range(LOAD_PARTS)]

        @pl.when(jnp.logical_and(k == 0, i == 0))
        def _():
            for kk in (1, 2):
                first_hop(kk).start()
            for cp in loads(0):
                cp.start()
            for cp in loads(0):
                cp.wait()

        for kk in (1, 2, 3):
            @pl.when(jnp.logical_and(k == kk - 1, i == nt - 1))
            def _(kk=kk):
                if kk == 1:
                    first_hop(1).wait_recv()
                    first_hop(2).wait_recv()
                    relay().start()
                if kk == 3:
                    relay().wait_recv()
                onward(kk, c).start()
                onward(kk, 1 - c).wait_recv()
                for cp in loads(kk):
                    cp.start()

            @pl.when(jnp.logical_and(k == kk, i == 0))
            def _(kk=kk):
                for cp in loads(kk):
                    cp.wait()

        rows = pl.ds(pl.multiple_of(i * PROJ_TILE, PROJ_TILE), PROJ_TILE)

        @pl.when(k == 0)
        def _():
            xn, _ = _rms(x_ref[...])
            h = (xn * g_ref[...]).astype(MXU_DTYPE)
            h_all[rows, :] = h
            h_ref[...] = h

        pj_ref[...] = jnp.dot(h_all[rows, :], w_vmem[k % 2], preferred_element_type=F32)

        @pl.when(jnp.logical_and(k == N_CHIPS - 1, i == nt - 1))
        def _():
            for cp in [first_hop(1), first_hop(2), relay()] + [onward(kk, c) for kk in (1, 2, 3)]:
                cp.wait_send()

    parked = lambda k, i, pos: (jnp.where(k == 0, i, nt - 1), 0)
    return _compute_call(
        body, name="proj_fwd", grid=(N_CHIPS, nt),
        in_specs=[pl.BlockSpec((PROJ_TILE, D_MODEL), parked), pl.BlockSpec((1, D_MODEL), lambda k, i, pos: (0, 0)), ANY],
        out_specs=[pl.BlockSpec((PROJ_TILE, D_MODEL), lambda k, i, pos: (jnp.where(k == 0, i, nt), 0)),
                   pl.BlockSpec((None, PROJ_TILE, SHARD_W), lambda k, i, pos: (jnp.bitwise_xor(pos[0], k), i, 0)),
                   ANY],
        out_shape=[jax.ShapeDtypeStruct((t + PROJ_TILE, D_MODEL), MXU_DTYPE),
                   jax.ShapeDtypeStruct((N_CHIPS, t, SHARD_W), F32),
                   jax.ShapeDtypeStruct(placed_in.shape, placed_in.dtype)],
        scratch_shapes=[pltpu.VMEM((2,) + placed_in.shape[1:], placed_in.dtype), pltpu.VMEM((t, D_MODEL), MXU_DTYPE),
                        pltpu.SemaphoreType.DMA((LOAD_PARTS,)), pltpu.SemaphoreType.DMA((6,)),
                        pltpu.SemaphoreType.DMA((6,))],
        operands=(x2d, g_mixer, placed_in), semantics=("arbitrary", "arbitrary"), side=side, prefetch=pos,
        aliases={2: 2}, side_start=(2, 0))


def _retention_consts(seq):
    half = QK_DIM // 2
    inv = ROPE_BASE ** (-jnp.arange(half, dtype=F32) / half)
    inv = jnp.concatenate([inv, inv])
    coarse = (jnp.arange(seq // half, dtype=F32) * half)[:, None] * inv[None, :]
    fine = jnp.arange(half, dtype=F32)[:, None] * inv[None, :]
    ca, sa, cb, sb = jnp.cos(coarse)[:, None], jnp.sin(coarse)[:, None], jnp.cos(fine)[None], jnp.sin(fine)[None]
    cos_full = (ca * cb - sa * sb).reshape(seq, QK_DIM)
    sign = jnp.concatenate([-jnp.ones((half,), F32), jnp.ones((half,), F32)])
    sin_signed = (sa * cb + ca * sb).reshape(seq, QK_DIM) * sign
    log_g = jnp.log(1.0 - 2.0 ** (-5.0 - jnp.arange(RET_HEADS, dtype=F32)))
    idx = jnp.arange(CHUNK, dtype=F32)
    diff = idx[:, None] - idx[None, :]
    decay_in = jnp.where(diff[None] >= 0, jnp.exp(jnp.maximum(diff, 0.0)[None] * log_g[:, None, None]), 0.0)
    zeta = jnp.exp((CHUNK - 1.0 - idx)[None, :] * log_g[:, None])
    xi = jnp.exp((idx + 1.0)[None, :] * log_g[:, None])
    zeta = jnp.broadcast_to(zeta[:, :, None], (RET_HEADS, CHUNK, QK_DIM))
    xi = jnp.broadcast_to(xi[:, :, None], (RET_HEADS, CHUNK, QK_DIM))
    return cos_full, sin_signed, decay_in, zeta, xi


def _qkv(pj_ref, rows, h):
    q = pj_ref[rows, h * QK_DIM:(h + 1) * QK_DIM]
    k = pj_ref[rows, RET_HEADS * QK_DIM + h * QK_DIM:RET_HEADS * QK_DIM + (h + 1) * QK_DIM]
    v = pj_ref[rows, 2 * RET_HEADS * QK_DIM + h * V_DIM:2 * RET_HEADS * QK_DIM + (h + 1) * V_DIM]
    return q, k, v


def _retention_fwd_call(pj0, consts, n_seq, seq):
    cos_full, sin_signed, decay_in, zeta, xi = consts
    nb = seq // RET_TILE
    cpb = RET_TILE // CHUNK

    def body(pj_ref, cos_ref, sin_ref, d_ref, zeta_ref, xi_ref, o_ref, st_ref, state):
        @pl.when(pl.program_id(1) == 0)
        def _():
            state[...] = jnp.zeros_like(state)

        for cc in range(cpb):
            rows = slice(cc * CHUNK, (cc + 1) * CHUNK)
            cs, sn = cos_ref[rows, :], sin_ref[rows, :]
            for h in range(RET_HEADS):
                q, k, v = _qkv(pj_ref, rows, h)
                qt = (q * cs + _rot(q) * sn) * QK_SCALE
                kt = k * cs + _rot(k) * sn
                prev = state[h]
                st_ref[cc, h] = prev.astype(st_ref.dtype)
                scores = _mm_nt(qt, kt) * d_ref[h]
                o_ref[rows, h * V_DIM:(h + 1) * V_DIM] = _mm(scores, v) + _mm(qt * xi_ref[h], prev)
                state[h] = _mm_tn(kt * zeta_ref[h], v) + CHUNK_DECAY[h] * prev

    row = lambda b, n: (b * nb + n, 0)
    pos = lambda b, n: (n, 0)
    return pl.pallas_call(
        body, name="retention_fwd", grid=(n_seq, nb),
        in_specs=[pl.BlockSpec((None, RET_TILE, SHARD_W), lambda b, n: (0,) + row(b, n)),
                  pl.BlockSpec((RET_TILE, QK_DIM), pos),
                  pl.BlockSpec((RET_TILE, QK_DIM), pos), _resident(decay_in.shape), _resident(zeta.shape),
                  _resident(xi.shape)],
        out_specs=[pl.BlockSpec((RET_TILE, D_MODEL), row),
                   pl.BlockSpec((cpb, RET_HEADS, QK_DIM, V_DIM), lambda b, n: (b * nb + n, 0, 0, 0))],
        out_shape=[jax.ShapeDtypeStruct((n_seq * seq, D_MODEL), F32),
                   jax.ShapeDtypeStruct((n_seq * seq // CHUNK, RET_HEADS, QK_DIM, V_DIM), MXU_DTYPE)],
        scratch_shapes=[pltpu.VMEM((RET_HEADS, QK_DIM, V_DIM), F32)],
        compiler_params=_params(("arbitrary", "arbitrary")),
    )(pj0, cos_full, sin_signed, decay_in, zeta, xi)


def _merge_bwd_call(wide, pj1, pj2, pj3, ret_raw, ws, bst, w_ro, w_so, w_o):
    t = wide.shape[1]

    def body(wide_ref, pj1_ref, pj2_ref, pj3_ref, rr_ref, ws_ref, bst_ref, wro_ref, wso_ref, wo_ref,
             dpj1_ref, dpj2_ref, dpj3_ref, drr_ref, da_ref, db_ref, dws_ref, dbst_ref, mixed_ref):
        @pl.when(pl.program_id(0) == 0)
        def _():
            dws_ref[...] = jnp.zeros_like(dws_ref)
            dbst_ref[...] = jnp.zeros_like(dbst_ref)

        wsc = _causal_ws(ws_ref)
        f = _branch_forward(pj1_ref[...], pj2_ref[...], rr_ref[...], wsc, bst_ref[...], mixed_ref)
        pj3 = pj3_ref[...]
        smr, sms = _sigmoid(pj3[:, :D_MODEL]), _sigmoid(pj3[:, D_MODEL:])
        dmerged = _mm_nt(wide_ref[0], wo_ref[...])
        da_f, db_f = dmerged * smr, dmerged * sms
        da, db = da_f.astype(MXU_DTYPE), db_f.astype(MXU_DTYPE)
        dpj3_ref[:, :D_MODEL] = (da_f * wide_ref[1] * (1.0 - smr)).astype(dpj3_ref.dtype)
        dpj3_ref[:, D_MODEL:] = (db_f * wide_ref[2] * (1.0 - sms)).astype(dpj3_ref.dtype)
        da_ref[...] = da
        db_ref[...] = db

        dret = _mm_nt(da, wro_ref[...])
        dpj1_ref[:, :D_MODEL] = (dret * f["rn"] * f["dsil_rg"]).astype(dpj1_ref.dtype)
        drn = dret * f["sil_rg"]
        for h in range(RET_HEADS):
            cols = slice(h * V_DIM, (h + 1) * V_DIM)
            drr_ref[:, cols] = _unit_norm_bwd(drn[:, cols], f["rn"][:, cols], f["rstd_r"][h]).astype(drr_ref.dtype)

        dsgu = _mm_nt(db, wso_ref[...])
        dpj2_ref[:, D_MODEL:] = (dsgu * f["u"] * f["mixed"] * f["dsil_sg"]).astype(dpj2_ref.dtype)
        tg = dsgu * f["sil_sg"]
        dpj1_ref[:, D_MODEL:] = (tg * f["mixed"] * f["du"]).astype(dpj1_ref.dtype)
        dmixed = tg * f["u"]
        for cc in range(TILE_M // CHUNK):
            for g in range(SGU_GROUPS):
                rs, cs = slice(cc * CHUNK, (cc + 1) * CHUNK), slice(g * GROUP_DIM, (g + 1) * GROUP_DIM)
                dm = dmixed[rs, cs]
                mixed_ref[rs, cs] = _mm_tn(wsc[g], dm)
                dws_ref[g] += _mm_nt(dm, f["vn"][rs, cs])
                dbst_ref[:, g:g + 1] += jnp.sum(dm, axis=1, keepdims=True)
        dvv = _unit_norm_bwd(mixed_ref[...], f["vn"], f["rstd_v"])
        dpj2_ref[:, :D_MODEL] = (dvv * f["dgelu_sv"]).astype(dpj2_ref.dtype)

    sq = (D_MODEL, D_MODEL)
    return pl.pallas_call(
        body, name="merge_bwd", grid=(t // TILE_M,),
        in_specs=[pl.BlockSpec((3, TILE_M, D_MODEL), lambda i: (0, i, 0))] + [_proj_tile(j) for j in (1, 2, 3)]
        + [_row_tile(D_MODEL)] + [_resident(ws.shape), _resident(bst.shape), _resident(sq), _resident(sq), _resident(sq)],
        out_specs=[_row_tile(SHARD_W)] * 3 + [_row_tile(D_MODEL)] * 3
        + [pl.BlockSpec(ws.shape, lambda i: (0, 0, 0)), pl.BlockSpec(bst.shape, lambda i: (0, 0))],
        out_shape=[jax.ShapeDtypeStruct((t, SHARD_W), MXU_DTYPE)] * 3 + [jax.ShapeDtypeStruct((t, D_MODEL), MXU_DTYPE)] * 3
        + [jax.ShapeDtypeStruct(ws.shape, F32), jax.ShapeDtypeStruct(bst.shape, F32)],
        scratch_shapes=[pltpu.VMEM((TILE_M, D_MODEL), F32)], compiler_params=_params(("arbitrary",)),
    )(wide, pj1, pj2, pj3, ret_raw, ws, bst, w_ro, w_so, w_o)


def _tail_call(pj, ret_raw, x2d, p2d, target, ws, bst, g_ple, g_final, w_ro, w_so, w_o, w_pg, w_pp):
    t = x2d.shape[0]
    tm = TILE_M

    def body(pj1_ref, pj2_ref, pj3_ref, rr_ref, x_ref, p_ref, t_ref, ws_ref, bst_ref, gp_ref, gf_ref,
             wro_ref, wso_ref, wo_ref, wpg_ref, wpp_ref,
             wide_ref, half_ref, loss_ref, dgp_ref, dgf_ref, mixed_ref):
        dx1_ref, a_ref, b_ref = (wide_ref.at[k] for k in range(3))
        ret_ref, sgu_ref, mg_ref, hp_ref, dz_ref, dpp_ref = (half_ref.at[k] for k in range(6))
        @pl.when(pl.program_id(0) == 0)
        def _():
            for acc in (loss_ref, dgp_ref, dgf_ref):
                acc[...] = jnp.zeros_like(acc)

        f = _branch_forward(pj1_ref[...], pj2_ref[...], rr_ref[...], _causal_ws(ws_ref), bst_ref[...], mixed_ref)
        ret = f["ret"].astype(MXU_DTYPE)
        sgu = f["sgu"].astype(MXU_DTYPE)
        ret_ref[...] = ret
        sgu_ref[...] = sgu
        a = jnp.dot(ret, wro_ref[...], preferred_element_type=F32)
        b = jnp.dot(sgu, wso_ref[...], preferred_element_type=F32)
        a_ref[...] = a
        b_ref[...] = b
        pj3 = pj3_ref[...]
        smr, sms = _sigmoid(pj3[:, :D_MODEL]), _sigmoid(pj3[:, D_MODEL:])
        merged = (smr * a + sms * b).astype(MXU_DTYPE)
        mg_ref[...] = merged
        x1v = x_ref[...] + jnp.dot(merged, wo_ref[...], preferred_element_type=F32)
        xn1, r1 = _rms(x1v)
        hp = (xn1 * gp_ref[...]).astype(MXU_DTYPE)
        hp_ref[...] = hp
        gate = _sigmoid(jnp.dot(hp, wpg_ref[...], preferred_element_type=F32))
        pp = jnp.dot(p_ref[...].astype(MXU_DTYPE), wpp_ref[...], preferred_element_type=F32)
        xn2, r2 = _rms(x1v + gate * pp)
        err = xn2 * gf_ref[...] - t_ref[...]
        loss_ref[...] += (0.5 / D_MODEL) * jnp.sum(jnp.sum(err * err, axis=1, keepdims=True), axis=0, keepdims=True)

        dy = err * (1.0 / D_MODEL)
        dgf_ref[...] += jnp.sum(dy * xn2, axis=0, keepdims=True)
        dx2 = _rms_bwd(dy * gf_ref[...], xn2, r2)
        dpp_ref[...] = (dx2 * gate).astype(MXU_DTYPE)
        dz = (dx2 * pp * gate * (1.0 - gate)).astype(MXU_DTYPE)
        dz_ref[...] = dz
        dhp = _mm_nt(dz, wpg_ref[...])
        dgp_ref[...] += jnp.sum(dhp * xn1, axis=0, keepdims=True)
        dx1_ref[...] = dx2 + _rms_bwd(dhp * gp_ref[...], xn1, r1)

    sq = (D_MODEL, D_MODEL)
    vec = _resident((1, D_MODEL))
    whole = lambda shape: pl.BlockSpec(shape, lambda i: (0,) * len(shape))
    return pl.pallas_call(
        body, name="tail_fwd", grid=(t // tm,),
        in_specs=[_proj_tile(j) for j in (1, 2, 3)]
        + [_row_tile(D_MODEL), _row_tile(D_MODEL), _row_tile(PLE_DIM), _row_tile(D_MODEL), _resident(ws.shape),
           _resident(bst.shape), vec, vec, _resident(sq), _resident(sq), _resident(sq), _resident(sq),
           _resident((PLE_DIM, D_MODEL))],
        out_specs=[pl.BlockSpec((3, tm, D_MODEL), lambda i: (0, i, 0)), pl.BlockSpec((6, tm, D_MODEL), lambda i: (0, i, 0)),
                   whole(LOSS_TILE), whole((1, D_MODEL)), whole((1, D_MODEL))],
        out_shape=[jax.ShapeDtypeStruct((3, t, D_MODEL), F32), jax.ShapeDtypeStruct((6, t, D_MODEL), MXU_DTYPE),
                   jax.ShapeDtypeStruct(LOSS_TILE, F32), jax.ShapeDtypeStruct((1, D_MODEL), F32),
           jax.ShapeDtypeStruct((1, D_MODEL), F32)],
        scratch_shapes=[pltpu.VMEM((TILE_M, D_MODEL), F32)], compiler_params=_params(("arbitrary",)),
    )(pj, pj, pj, ret_raw, x2d, p2d, target, ws, bst, g_ple, g_final, w_ro, w_so, w_o, w_pg, w_pp)


def _retention_bwd_call(pj0, drr, states, consts, n_seq, seq, side):
    cos_full, sin_signed, decay_in, zeta, xi = consts
    nb = seq // RET_TILE
    cpb = RET_TILE // CHUNK

    def body(pj_ref, do_ref, st_ref, cos_ref, sin_ref, d_ref, zeta_ref, xi_ref, dpj_ref, gstate):
        @pl.when(pl.program_id(1) == 0)
        def _():
            gstate[...] = jnp.zeros_like(gstate)

        for cc in reversed(range(cpb)):
            rows = slice(cc * CHUNK, (cc + 1) * CHUNK)
            cs, sn = cos_ref[rows, :], sin_ref[rows, :]
            for h in range(RET_HEADS):
                q, k, v = _qkv(pj_ref, rows, h)
                qt = (q * cs + _rot(q) * sn) * QK_SCALE
                kt = k * cs + _rot(k) * sn
                d_out = do_ref[rows, h * V_DIM:(h + 1) * V_DIM]
                prev = st_ref[cc, h]
                g = gstate[h]
                dec = d_ref[h]
                scores_d = _mm_nt(qt, kt) * dec
                dscores = _mm_nt(d_out, v) * dec
                kz = kt * zeta_ref[h]
                qx = qt * xi_ref[h]
                dv = _mm_tn(scores_d, d_out) + _mm(kz, g)
                dqt = (_mm(dscores, kt) + _mm_nt(d_out, prev) * xi_ref[h]) * QK_SCALE
                dkt = _mm_tn(dscores, qt) + _mm_nt(v, g) * zeta_ref[h]
                gstate[h] = _mm_tn(qx, d_out) + CHUNK_DECAY[h] * g
                dq = dqt * cs + _rot(dqt * sn)
                dk = dkt * cs + _rot(dkt * sn)
                dpj_ref[rows, h * QK_DIM:(h + 1) * QK_DIM] = dq.astype(dpj_ref.dtype)
                dpj_ref[rows, RET_HEADS * QK_DIM + h * QK_DIM:RET_HEADS * QK_DIM + (h + 1) * QK_DIM] = dk.astype(
                    dpj_ref.dtype)
                dpj_ref[rows, 2 * RET_HEADS * QK_DIM + h * V_DIM:2 * RET_HEADS * QK_DIM + (h + 1) * V_DIM] = dv.astype(
                    dpj_ref.dtype)

    row = lambda b, n: (b * nb + nb - 1 - n, 0)
    pos = lambda b, n: (nb - 1 - n, 0)
    return _compute_call(
        body, name="retention_bwd", grid=(n_seq, nb),
        in_specs=[pl.BlockSpec((None, RET_TILE, SHARD_W), lambda b, n: (0,) + row(b, n)),
                  pl.BlockSpec((RET_TILE, D_MODEL), row),
                  pl.BlockSpec((cpb, RET_HEADS, QK_DIM, V_DIM), lambda b, n: (b * nb + nb - 1 - n, 0, 0, 0)),
                  pl.BlockSpec((RET_TILE, QK_DIM), pos), pl.BlockSpec((RET_TILE, QK_DIM), pos),
                  _resident(decay_in.shape), _resident(zeta.shape), _resident(xi.shape)],
        out_specs=[pl.BlockSpec((RET_TILE, SHARD_W), row)],
        out_shape=[jax.ShapeDtypeStruct((n_seq * seq, SHARD_W), MXU_DTYPE)],
        scratch_shapes=[pltpu.VMEM((RET_HEADS, QK_DIM, V_DIM), F32)],
        operands=(pj0, drr, states, cos_full, sin_signed, decay_in, zeta, xi),
        semantics=("arbitrary", "arbitrary"), side=side)


def _dx_call(dpj, x2d, wide, g_mixer, wg_in, side):
    t = x2d.shape[0]

    def body(d0, d1, d2, d3, x_ref, dx1_ref, g_ref, w_ref, dx_ref, dg_ref):
        @pl.when(pl.program_id(0) == 0)
        def _():
            dg_ref[...] = jnp.zeros_like(dg_ref)

        dh = _mm_nt(d0[...], w_ref[0])
        for j, d_ref in enumerate((d1, d2, d3)):
            dh += _mm_nt(d_ref[...], w_ref[j + 1])
        xn, r = _rms(x_ref[...])
        dg_ref[...] += jnp.sum(dh * xn, axis=0, keepdims=True)
        dx_ref[...] = dx1_ref[...] + _rms_bwd(dh * g_ref[...], xn, r)

    return _compute_call(
        body, name="dx_bwd", grid=(t // DX_TILE,),
        in_specs=[_row_tile(SHARD_W, DX_TILE)] * N_CHIPS
        + [_row_tile(D_MODEL, DX_TILE), pl.BlockSpec((None, DX_TILE, D_MODEL), lambda i: (0, i, 0))]
        + [_resident((1, D_MODEL)), _resident(wg_in.shape)],
        out_specs=[_row_tile(D_MODEL, DX_TILE), pl.BlockSpec((1, D_MODEL), lambda i: (0, 0))],
        out_shape=[jax.ShapeDtypeStruct((t, D_MODEL), F32), jax.ShapeDtypeStruct((1, D_MODEL), F32)],
        operands=(*dpj, x2d, wide, g_mixer, wg_in), semantics=("arbitrary",), side=side)


def _wgrad_call(name, lhs, rhs, block_n, out_cols=None, block_t=1024, into=None, slot=0, n_slots=1, side=None):
    (lhs, lhs_at), (rhs, rhs_at) = [x if isinstance(x, tuple) else (x, None) for x in (lhs, rhs)]
    t, n = rhs.shape[-2:]
    k = lhs.shape[-1]
    block_t = min(block_t, t)

    def operand_spec(at, block, index):
        if at is None:
            return pl.BlockSpec(block, index)
        return pl.BlockSpec((None,) + block, lambda j, i: (at,) + index(j, i))

    steps = t // block_t
    out_cols = block_n if out_cols is None else out_cols
    per = block_n // out_cols
    first_block = slot * (n // block_n)

    def body(l_ref, r_ref, *rest):
        o_ref, acc = rest[-2:]
        @pl.when(pl.program_id(1) == 0)
        def _():
            acc[...] = jnp.zeros_like(acc)

        acc[...] += _mm_tn(l_ref[...], r_ref[...])

        @pl.when(pl.program_id(1) == steps - 1)
        def _():
            for s in range(per):
                o_ref[s] = acc[:, s * out_cols:(s + 1) * out_cols].astype(o_ref.dtype)

    res = _compute_call(
        body, name=name, grid=(n // block_n, steps),
        in_specs=[operand_spec(lhs_at, (block_t, k), lambda j, i: (i, 0)),
                  operand_spec(rhs_at, (block_t, block_n), lambda j, i: (i, j))] + ([] if into is None else [ANY]),
        out_specs=[pl.BlockSpec((per, k, out_cols), lambda j, i: (first_block + j, 0, 0))],
        out_shape=[jax.ShapeDtypeStruct((n_slots * (n // out_cols), k, out_cols), COMM_DTYPE)],
        scratch_shapes=[pltpu.VMEM((k, block_n), F32)], aliases={} if into is None else {2: 0},
        operands=(lhs, rhs) if into is None else (lhs, rhs, into), semantics=("arbitrary", "arbitrary"), side=side)
    return res[0] if side is None else (res[0][0], res[1])


def _wgrad_in_call(h, dpjs, block_t=1024):
    t = dpjs[0].shape[0]
    steps = t // block_t

    def body(l_ref, *rest):
        r_refs, o_ref, acc = rest[:N_CHIPS], rest[N_CHIPS], rest[N_CHIPS + 1]
        @pl.when(pl.program_id(1) == 0)
        def _():
            acc[...] = jnp.zeros_like(acc)

        for k in range(N_CHIPS):
            @pl.when(pl.program_id(0) == k)
            def _(k=k):
                acc[...] += _mm_tn(l_ref[...], r_refs[k][...])

        @pl.when(pl.program_id(1) == steps - 1)
        def _():
            o_ref[0] = acc[...].astype(o_ref.dtype)

    def turn(k):
        return lambda j, i: (jnp.where(j == k, i, jnp.where(j < k, 0, steps - 1)), 0)

    return _compute_call(
        body, name="wgrad_in", grid=(N_CHIPS, steps),
        in_specs=[pl.BlockSpec((block_t, D_MODEL), lambda j, i: (i, 0))]
        + [pl.BlockSpec((block_t, SHARD_W), turn(k)) for k in range(N_CHIPS)],
        out_specs=[pl.BlockSpec((1, D_MODEL, SHARD_W), lambda j, i: (j, 0, 0))],
        out_shape=[jax.ShapeDtypeStruct((N_CHIPS, D_MODEL, SHARD_W), COMM_DTYPE)],
        scratch_shapes=[pltpu.VMEM((D_MODEL, SHARD_W), F32)], operands=(h,) + tuple(dpjs),
        semantics=("arbitrary", "arbitrary"))[0]


def _wgrad_tail_call(half, da, db, wide, block_t=1024):
    t = da.shape[0]
    steps = t // block_t
    n_w = 4

    def body(l_ref, *rest):
        r_refs, o_refs, acc = rest[:n_w], rest[n_w:2 * n_w], rest[2 * n_w]
        @pl.when(pl.program_id(1) == 0)
        def _():
            acc[...] = jnp.zeros_like(acc)

        for k in range(n_w):
            @pl.when(pl.program_id(0) == k)
            def _(k=k):
                acc[...] += _mm_tn(l_ref[...], r_refs[k][...])

            @pl.when((pl.program_id(0) == k) & (pl.program_id(1) == steps - 1))
            def _(k=k):
                o_refs[k][0] = acc[...].astype(o_refs[k].dtype)

    def turn(k, at=None):
        rows = lambda j, i: jnp.where(j == k, i, jnp.where(j < k, 0, steps - 1))
        if at is None:
            return pl.BlockSpec((block_t, D_MODEL), lambda j, i: (rows(j, i), 0))
        return pl.BlockSpec((None, block_t, D_MODEL), lambda j, i: (at, rows(j, i), 0))

    sq = (1, D_MODEL, D_MODEL)
    return _compute_call(
        body, name="wgrad_tail", grid=(n_w, steps),
        in_specs=[pl.BlockSpec((None, block_t, D_MODEL), lambda j, i: (j, i, 0)), turn(0), turn(1), turn(2, 0), turn(3, 4)],
        out_specs=[pl.BlockSpec(sq, lambda j, i: (0, 0, 0))] * n_w,
        out_shape=[jax.ShapeDtypeStruct(sq, COMM_DTYPE)] * n_w,
        scratch_shapes=[pltpu.VMEM((D_MODEL, D_MODEL), F32)], operands=(half, da, db, wide, half),
        semantics=("arbitrary", "arbitrary"))


def _position():
    return lax.axis_index("x"), lax.axis_index("y"), lax.axis_index("c")


def _position_array():
    x, y, c = _position()
    return jnp.stack([2 * x + y, c]).astype(jnp.int32)


def _other_chip(x, y, k):
    return (1 - x if k & 2 else x), (1 - y if k & 1 else y)


def _plan_side(operands, out_shapes, plan, n_remote, aliases=None):
    def copies(ins, outs, send_sem, recv_sem, base=0):
        remote = plan(ins, outs)
        assert len(remote) == n_remote
        return [pltpu.make_async_remote_copy(src_ref=src, dst_ref=dst, send_sem=send_sem.at[base + i],
                                             recv_sem=recv_sem.at[base + i], device_id=dev, device_id_type=MESH)
                for i, (src, dst, dev) in enumerate(remote)]

    def start(*a):
        for cp in copies(*a):
            cp.start()

    def finish(*a):
        for cp in copies(*a):
            cp.wait()

    return _Side(operands, out_shapes, n_remote, start, finish, aliases)


def _place_cast_call(name, shards, pos):
    n = len(shards)
    rows, width = shards[0].shape
    block_rows = min(256, rows)

    def body(pos_ref, *refs):
        for w_ref, o_ref in zip(refs[:n], refs[n:]):
            o_ref[...] = w_ref[...].astype(o_ref.dtype)

    return pl.pallas_call(
        body, name=name,
        grid_spec=pltpu.PrefetchScalarGridSpec(
            num_scalar_prefetch=1, grid=(rows // block_rows,),
            in_specs=[pl.BlockSpec((block_rows, width), lambda i, pos: (i, 0))] * n,
            out_specs=[pl.BlockSpec((None, block_rows, width), lambda i, pos: (pos[0], i, 0))] * n),
        out_shape=[jax.ShapeDtypeStruct((N_CHIPS, rows, width), MXU_DTYPE)] * n,
        compiler_params=_params(("arbitrary",)),
    )(pos, *shards)


def _gather_side(placed):
    n = len(placed)

    def copies(kind, bufs, send_sem, recv_sem, base):
        x, y, c = _position()
        me = 2 * x + y
        made = []
        for i in range(n):
            hr = placed[i].shape[1] // 2
            for k in (1, 2, 3):
                px, py = _other_chip(x, y, k)
                chip, core, slot, dev = [(me, c, k - 1, (px, py, c)), (2 * px + py, c, 2 + k, (x, y, 1 - c)),
                                         (2 * px + py, 1 - c, 2 + k, (x, y, 1 - c))][kind]
                piece = bufs[i].at[chip, pl.ds(core * hr, hr)]
                made.append(pltpu.make_async_remote_copy(
                    src_ref=piece, dst_ref=piece, send_sem=send_sem.at[base + 6 * i + slot],
                    recv_sem=recv_sem.at[base + 6 * i + slot], device_id=dev, device_id_type=MESH))
        return made

    def start(ins, outs, send_sem, recv_sem, base=0):
        for cp in copies(0, outs, send_sem, recv_sem, base):
            cp.start()

    def finish(ins, outs, send_sem, recv_sem, base=0):
        first, onward = copies(0, outs, send_sem, recv_sem, base), copies(1, outs, send_sem, recv_sem, base)
        for landed, cp in zip(first, onward):
            landed.wait_recv()
            cp.start()
        for cp in copies(2, outs, send_sem, recv_sem, base):
            cp.wait_recv()
        for cp in first + onward:
            cp.wait_send()

    return _Side(placed, [jax.ShapeDtypeStruct(a.shape, a.dtype) for a in placed], 6 * n, start, finish,
                 {i: i for i in range(n)})


def _same_shape(shapes):
    found = {}
    for i, shape in enumerate(shapes):
        found.setdefault(shape, []).append(i)
    return found


def _pair_stage(tag, grads):
    n = len(grads)
    shapes = [(g.shape[0] // N_CHIPS, g.shape[1]) for g in grads]
    kinds = list(_same_shape(shapes))
    whole = {(r, w): N_CHIPS * (r // 2) <= 2 * PAIR_CHUNK_ROWS for r, w in kinds}
    chunk = {(r, w): N_CHIPS * (r // 2) if whole[(r, w)] else min(r // 2, PAIR_CHUNK_ROWS) for r, w in kinds}
    work = [(i, j, q) for i, (r, w) in enumerate(shapes) for j in ((None,) if whole[(r, w)] else range(N_CHIPS))
            for q in range(1 if whole[(r, w)] else (r // 2) // chunk[(r, w)])]

    def body(*refs):
        ins, sums, landed = refs[:n], refs[n:2 * n], refs[2 * n:3 * n]
        bufs, (load_sem, store_sem, send_sem, recv_sem) = refs[3 * n:-4], refs[-4:]
        x, y, c = _position()

        def piece(i, j):
            r = shapes[i][0]
            return pltpu.make_async_remote_copy(
                src_ref=ins[i].at[pl.ds(j * r + (1 - c) * (r // 2), r // 2)],
                dst_ref=landed[i].at[pl.ds(j * (r // 2), r // 2)], send_sem=send_sem.at[N_CHIPS * i + j],
                recv_sem=recv_sem.at[N_CHIPS * i + j], device_id=(x, y, 1 - c), device_id_type=MESH)

        def buffers(i, slot):
            own, got, out = bufs[3 * kinds.index(shapes[i]):3 * kinds.index(shapes[i]) + 3]
            return own.at[slot], got.at[slot], out.at[slot]

        def loads(step):
            i, j, q = work[step]
            r, ch = shapes[i][0], chunk[shapes[i]]
            own, got, _ = buffers(i, step % 2)
            if j is None:
                return [pltpu.make_async_copy(ins[i].at[pl.ds(jj * r + c * (r // 2), r // 2)],
                                              own.at[pl.ds(jj * (r // 2), r // 2)], load_sem.at[step % 2, jj])
                        for jj in range(N_CHIPS)] + [pltpu.make_async_copy(landed[i], got, load_sem.at[step % 2, N_CHIPS])]
            return [pltpu.make_async_copy(ins[i].at[pl.ds(j * r + c * (r // 2) + q * ch, ch)], own,
                                          load_sem.at[step % 2, 0]),
                    pltpu.make_async_copy(landed[i].at[pl.ds(j * (r // 2) + q * ch, ch)], got,
                                          load_sem.at[step % 2, 1])]

        def store(step):
            i, j, q = work[step]
            ch = chunk[shapes[i]]
            rows = sums[i] if j is None else sums[i].at[pl.ds(j * (shapes[i][0] // 2) + q * ch, ch)]
            return pltpu.make_async_copy(buffers(i, step % 2)[2], rows, store_sem.at[step % 2])

        def begin(step):
            i, j, q = work[step]
            for jj in (range(N_CHIPS) if j is None else [j] if q == 0 else []):
                piece(i, jj).wait_recv()
            for cp in loads(step):
                cp.start()

        for i in range(n):
            for j in range(N_CHIPS):
                piece(i, j).start()
        begin(0)
        for step in range(len(work)):
            if step + 1 < len(work):
                begin(step + 1)
            for cp in loads(step):
                cp.wait()
            if step >= 2:
                store(step - 2).wait()
            own, got, out = buffers(work[step][0], step % 2)
            out[...] = (own[...].astype(F32) + got[...].astype(F32)).astype(out.dtype)
            store(step).start()
        for step in range(max(len(work) - 2, 0), len(work)):
            store(step).wait()
        for i in range(n):
            for j in range(N_CHIPS):
                piece(i, j).wait_send()

    half = [jax.ShapeDtypeStruct((N_CHIPS * (r // 2), w), COMM_DTYPE) for r, w in shapes]
    res = pl.pallas_call(
        body, name=f"grad_pair_sum_{tag}", in_specs=[ANY] * n, out_specs=[ANY] * (2 * n), out_shape=half + half,
        scratch_shapes=[pltpu.VMEM((2, chunk[k], k[1]), COMM_DTYPE) for k in kinds for _ in range(3)]
        + [pltpu.SemaphoreType.DMA((2, N_CHIPS + 1)), pltpu.SemaphoreType.DMA((2,)),
           pltpu.SemaphoreType.DMA((N_CHIPS * n,)),
           pltpu.SemaphoreType.DMA((N_CHIPS * n,))],
        compiler_params=pltpu.CompilerParams(has_side_effects=True, vmem_limit_bytes=VMEM_LIMIT),
    )(*grads)
    return list(res[:n])


def _chip_side(pair_sums):
    halves = [(p.shape[0] // N_CHIPS, p.shape[1]) for p in pair_sums]

    def plan(ins, outs):
        x, y, c = _position()
        remote = []
        for i, (hr, _) in enumerate(halves):
            for k in (1, 2, 3):
                px, py = _other_chip(x, y, k)
                remote.append((ins[i].at[pl.ds((2 * px + py) * hr, hr)], outs[i].at[pl.ds((k - 1) * hr, hr)],
                               (px, py, c)))
        return remote

    return _plan_side(pair_sums, [jax.ShapeDtypeStruct((3 * hr, w), COMM_DTYPE) for hr, w in halves], plan,
                      3 * len(pair_sums))


def _finish_stage(tag, pair_sums, chip, extra):
    n = len(pair_sums)
    halves = [(p.shape[0] // N_CHIPS, p.shape[1]) for p in pair_sums]
    kinds = list(_same_shape(halves))
    chunk = {(hr, w): min(hr, PAIR_CHUNK_ROWS) for hr, w in kinds}
    work = [(i, q) for i, (hr, w) in enumerate(halves) for q in range(hr // chunk[(hr, w)])]
    e_in, e_out = len(extra.operands), len(extra.out_shapes)

    def body(*refs):
        sums, got, refs = refs[:n], refs[n:2 * n], refs[2 * n:]
        extra_ins, refs = refs[:e_in], refs[e_in:]
        full, refs = refs[:n], refs[n:]
        extra_outs, refs = refs[:e_out], refs[e_out:]
        bufs, (load_sem, store_sem, send_sem, recv_sem, extra_send, extra_recv) = refs[:-6], refs[-6:]
        x, y, c = _position()
        me = 2 * x + y
        extra.start(extra_ins, extra_outs, extra_send, extra_recv)

        def buffers(i, slot):
            at = 5 * kinds.index(halves[i])
            return [b_.at[slot] for b_ in bufs[at:at + 5]]

        def loads(step):
            i, q = work[step]
            hr, ch = halves[i][0], chunk[halves[i]]
            into = buffers(i, step % 2)
            made = [pltpu.make_async_copy(sums[i].at[pl.ds(me * hr + q * ch, ch)], into[0], load_sem.at[step % 2, 0])]
            made += [pltpu.make_async_copy(got[i].at[pl.ds(k * hr + q * ch, ch)], into[1 + k], load_sem.at[step % 2, 1 + k])
                     for k in range(3)]
            return made

        def stores(step):
            i, q = work[step]
            hr, ch = halves[i][0], chunk[halves[i]]
            result = buffers(i, step % 2)[4]
            rows = full[i].at[pl.ds(c * hr + q * ch, ch)]
            return (pltpu.make_async_copy(result, rows, store_sem.at[step % 2]),
                    pltpu.make_async_remote_copy(src_ref=result, dst_ref=rows, send_sem=send_sem.at[step],
                                                 recv_sem=recv_sem.at[step], device_id=(x, y, 1 - c),
                                                 device_id_type=MESH))

        for cp in loads(0):
            cp.start()
        for step in range(len(work)):
            if step + 1 < len(work):
                for cp in loads(step + 1):
                    cp.start()
            for cp in loads(step):
                cp.wait()
            if step >= 2:
                stores(step - 2)[0].wait()
                stores(step - 2)[1].wait_send()
            mine, k1, k2, k3, result = buffers(work[step][0], step % 2)
            result[...] = ((mine[...].astype(F32) + k1[...].astype(F32)) + k2[...].astype(F32)) + k3[...].astype(F32)
            for cp in stores(step):
                cp.start()
        for step in range(max(len(work) - 2, 0), len(work)):
            stores(step)[0].wait()
            stores(step)[1].wait_send()
        for step in range(len(work)):
            stores(step)[1].wait_recv()
        extra.finish(extra_ins, extra_outs, extra_send, extra_recv)

    scratch = []
    for k in kinds:
        scratch += [pltpu.VMEM((2, chunk[k], k[1]), COMM_DTYPE)] * 4 + [pltpu.VMEM((2, chunk[k], k[1]), F32)]
    res = pl.pallas_call(
        body, name=f"grad_finish_{tag}", in_specs=[ANY] * (2 * n + e_in), out_specs=[ANY] * (n + e_out),
        out_shape=[jax.ShapeDtypeStruct((2 * hr, w), F32) for hr, w in halves] + extra.out_shapes,
        scratch_shapes=scratch + [pltpu.SemaphoreType.DMA((2, 4)), pltpu.SemaphoreType.DMA((2,)),
                                  pltpu.SemaphoreType.DMA((len(work),)), pltpu.SemaphoreType.DMA((len(work),)),
                                  pltpu.SemaphoreType.DMA((extra.n_sems,)), pltpu.SemaphoreType.DMA((extra.n_sems,))],
        compiler_params=pltpu.CompilerParams(has_side_effects=True, vmem_limit_bytes=VMEM_LIMIT),
    )(*pair_sums, *chip, *extra.operands)
    return list(res[:n]), list(res[n:])


def _small_gather_side(parts):
    n = len(parts)

    def copies(ins, outs, send_sem, recv_sem, base):
        x, y, c = _position()
        made = []
        for i in range(n):
            mine = outs[i].at[4 * x + 2 * y + c]
            made.append(pltpu.make_async_copy(ins[i], mine, send_sem.at[base + 8 * i + 7]))
            for d in range(1, 8):
                px, py = _other_chip(x, y, d >> 1)
                made.append(pltpu.make_async_remote_copy(
                    src_ref=ins[i], dst_ref=mine, send_sem=send_sem.at[base + 8 * i + d - 1],
                    recv_sem=recv_sem.at[base + 8 * i + d - 1], device_id=(px, py, 1 - c if d & 1 else c),
                    device_id_type=MESH))
        return made

    def start(ins, outs, send_sem, recv_sem, base=0):
        for cp in copies(ins, outs, send_sem, recv_sem, base):
            cp.start()

    def finish(ins, outs, send_sem, recv_sem, base=0):
        for cp in copies(ins, outs, send_sem, recv_sem, base):
            cp.wait()

    return _Side(parts, [jax.ShapeDtypeStruct((8,) + a_.shape, F32) for a_ in parts], 8 * n, start, finish)


def _adamw(w, g, m, v):
    m = ADAM_B1 * m + (1.0 - ADAM_B1) * g
    v = ADAM_B2 * v + (1.0 - ADAM_B2) * (g * g)
    m_hat = m / (1.0 - ADAM_B1 ** ADAM_STEP)
    v_hat = v / (1.0 - ADAM_B2 ** ADAM_STEP)
    delta = -ADAM_LR * (m_hat / (jnp.sqrt(v_hat) + ADAM_EPS) + ADAM_WD * w)
    return delta, m, v


def _adamw_call(name, ws, gs, ms, vs):
    n = len(ws)
    rows, width = ws[0].shape
    block_rows = min(256 // n, rows)

    def body(*refs):
        for i in range(n):
            w_ref, g_ref, m_ref, v_ref = (refs[j * n + i] for j in range(4))
            d_out, m_out, v_out = (refs[(4 + j) * n + i] for j in range(3))
            d_out[...], m_out[...], v_out[...] = _adamw(w_ref[...], g_ref[...], m_ref[...], v_ref[...])

    spec = pl.BlockSpec((block_rows, width), lambda i: (i, 0))
    res = pl.pallas_call(
        body, name=name, grid=(rows // block_rows,), in_specs=[spec] * (4 * n), out_specs=[spec] * (3 * n),
        out_shape=[jax.ShapeDtypeStruct(ws[0].shape, F32)] * (3 * n),
        compiler_params=_params(("arbitrary",)),
    )(*ws, *gs, *ms, *vs)
    return [(res[i], res[n + i], res[2 * n + i]) for i in range(n)]


def _small_adamw_call(gathered, weights, moments_m, moments_v):
    n = len(weights)

    def body(*refs):
        all_refs, refs = refs[:n + 1], refs[n + 1:]
        w_refs, m_refs, v_refs, outs = refs[:n], refs[n:2 * n], refs[2 * n:3 * n], refs[3 * n:]

        def total(ref):
            acc = ref[0]
            for d in range(1, 8):
                acc = acc + ref[d]
            return acc

        outs[0][...] = total(all_refs[n])
        for i in range(n):
            g = total(all_refs[i])
            if i == 0:
                row = lax.broadcasted_iota(jnp.int32, g.shape, 0)
                col = lax.broadcasted_iota(jnp.int32, g.shape, 1)
                g = jnp.where((row % CHUNK) >= col, g, 0.0)
            g_out, d_out, m_out, v_out = outs[1 + 4 * i:5 + 4 * i]
            g_out[...] = g
            d_out[...], m_out[...], v_out[...] = _adamw(w_refs[i][...], g, m_refs[i][...], v_refs[i][...])

    out_shape = [jax.ShapeDtypeStruct(LOSS_TILE, F32)]
    for w in weights:
        out_shape += [jax.ShapeDtypeStruct(w.shape, F32)] * 4
    res = pl.pallas_call(
        body, name="small_adamw", out_shape=out_shape,
        compiler_params=pltpu.CompilerParams(vmem_limit_bytes=VMEM_LIMIT),
    )(*gathered, *weights, *moments_m, *moments_v)
    return res[0], [res[1 + 4 * i:5 + 4 * i] for i in range(n)]


def kernel(x, p, w_in, w_ret_out, w_sgu_out, w_out, sgu_ws, sgu_bs, w_ple_gate, w_ple_proj, g_mixer, g_ple, g_final, loss_target, m_w_in, m_w_ret_out, m_w_sgu_out, m_w_out, m_sgu_ws, m_sgu_bs, m_w_ple_gate, m_w_ple_proj, m_g_mixer, m_g_ple, m_g_final, v_w_in, v_w_ret_out, v_w_sgu_out, v_w_out, v_sgu_ws, v_sgu_bs, v_w_ple_gate, v_w_ple_proj, v_g_mixer, v_g_ple, v_g_final):
    n_seq, seq, _ = x.shape
    t = n_seq * seq
    x2d = x.reshape(t, D_MODEL)
    p2d = p.reshape(t, PLE_DIM)
    target = loss_target.reshape(t, D_MODEL)
    big = [w_in[0], w_ret_out[0], w_sgu_out[0], w_out[0], w_ple_gate[0], w_ple_proj[0]]
    big_m = [m_w_in[0], m_w_ret_out[0], m_w_sgu_out[0], m_w_out[0], m_w_ple_gate[0], m_w_ple_proj[0]]
    big_v = [v_w_in[0], v_w_ret_out[0], v_w_sgu_out[0], v_w_out[0], v_w_ple_gate[0], v_w_ple_proj[0]]

    pos = _position_array()
    placed = (_place_cast_call("place_w_in", big[:1], pos) + _place_cast_call("place_square_weights", big[1:5], pos)
              + _place_cast_call("place_w_ple_proj", big[5:], pos))
    ws = sgu_ws[0]
    bst = sgu_bs[0].T
    consts = _retention_consts(seq)

    (h, pj, wg_in), gathered = _proj_call(x2d, g_mixer, placed[0], pos, _gather_side(placed[1:]))
    pj0 = pj1 = pj2 = pj3 = pj
    w_ro, w_so, w_o, w_pg = (w.reshape(D_MODEL, D_MODEL) for w in gathered[:4])
    w_pp = gathered[4].transpose(1, 0, 2).reshape(PLE_DIM, D_MODEL)
    ret_raw, states = _retention_fwd_call(pj0, consts, n_seq, seq)
    wide, half, loss, dg_ple, dg_final = _tail_call(
        pj, ret_raw, x2d, p2d, target, ws, bst, g_ple, g_final.reshape(1, D_MODEL), w_ro, w_so, w_o, w_pg, w_pp)
    ret, sgu, merged, hp, dz, dpp = ((half, k) for k in range(6))
    dpj1, dpj2, dpj3, drr, da, db, dws, dbst = _merge_bwd_call(wide, pj1, pj2, pj3, ret_raw, ws, bst, w_ro, w_so, w_o)

    small_shapes = [(SGU_GROUPS * CHUNK, CHUNK), (SGU_GROUPS, CHUNK), (1, D_MODEL), (1, D_MODEL), (1, D_MODEL)]
    as_small = lambda arrays: [None if a_ is None else a_.reshape(s_) for a_, s_ in zip(arrays, small_shapes)]
    early = as_small([dws, dbst.T, None, dg_ple, dg_final])
    rows_of = lambda g: g.reshape(g.shape[0] * g.shape[1], g.shape[2])
    tail_grads = [rows_of(g) for g in _wgrad_tail_call(half, da, db, wide)] + [
        rows_of(_wgrad_call("wgrad_ple_proj", p2d, dpp, D_MODEL, out_cols=PLE_DIM)),
    ]
    tail_sums = _pair_stage("tail", tail_grads)
    (dpj0,), carried = _retention_bwd_call(
        pj0, drr, states, consts, n_seq, seq,
        _join_sides(_chip_side(tail_sums), _small_gather_side([early[0], early[1], early[3], early[4], loss])))
    tail_chip, early_all = carried[:len(tail_sums)], carried[len(tail_sums):]
    in_grad = _wgrad_in_call(h, (dpj0, dpj1, dpj2, dpj3))
    in_sums = _pair_stage("in", [rows_of(in_grad)])
    (dx, dg_mixer), in_chip = _dx_call((dpj0, dpj1, dpj2, dpj3), x2d, wide, g_mixer, wg_in, _chip_side(in_sums))
    g_big, (mixer_all,) = _finish_stage("all", in_sums + tail_sums, list(in_chip) + list(tail_chip),
                                        _small_gather_side([dg_mixer]))
    upd = [None] * len(big)
    for _, which in _same_shape([w.shape for w in big]).items():
        pick = lambda arrays: [arrays[i] for i in which]
        for i, triple in zip(which, _adamw_call(f"adamw_{which[0]}", pick(big), pick(g_big), pick(big_m), pick(big_v))):
            upd[i] = triple

    small_g = [early_all[0], early_all[1], mixer_all, early_all[2], early_all[3], early_all[4]]
    total, small = _small_adamw_call(small_g, as_small([sgu_ws, sgu_bs, g_mixer, g_ple, g_final]),
                                     as_small([m_sgu_ws, m_sgu_bs, m_g_mixer, m_g_ple, m_g_final]),
                                     as_small([v_sgu_ws, v_sgu_bs, v_g_mixer, v_g_ple, v_g_final]))
    out_small_shapes = [sgu_ws.shape, sgu_bs.shape, g_mixer.shape, g_ple.shape, g_final.shape]

    def ordered(big_list, kind):
        w_in_, w_ro_, w_so_, w_o_, w_pg_, w_pp_ = [b_[None] for b_ in big_list]
        s_ws, s_bs, s_gm, s_gp, s_gf = [small[i][kind].reshape(s) for i, s in enumerate(out_small_shapes)]
        return [w_in_, w_ro_, w_so_, w_o_, s_ws, s_bs, w_pg_, w_pp_, s_gm, s_gp, s_gf]

    out = [total[0, 0], dx.reshape(x.shape)]
    out += ordered(g_big, 0)
    out += ordered([u[0] for u in upd], 1)
    out += ordered([u[1] for u in upd], 2)
    out += ordered([u[2] for u in upd], 3)
    return tuple(out)
```
